```python
import functools
import jax, jax.numpy as jnp
from jax import lax
import numpy as np

D_MODEL = 1024
BATCH = 4
SEQ = 4096
DEPTH = 1
DEC_BATCH = 128
DEC_SEQ = 1
PAST_LEN = 8192
PAGE_SIZE = 128

CHUNK = 128
A_GROUPS = 4
A_GROUP_DIM = 128
A_WIDTH = A_GROUPS * A_GROUP_DIM
N_HEADS = 8
N_KV_HEADS = 2
HEAD_DIM = 64
Q_WIDTH = N_HEADS * HEAD_DIM
KV_WIDTH = N_KV_HEADS * HEAD_DIM
GQA_GROUP = N_HEADS // N_KV_HEADS
WINDOW = 128
ROPE_THETA = 10000.0
IN_COLS = 2 * A_WIDTH + Q_WIDTH + 2 * KV_WIDTH + 2 * D_MODEL
N_EXPERTS = 32
TOP_K = 4
D_EXPERT = D_MODEL
SWIGLU_ALPHA = 1.702
SWIGLU_LIMIT = 7.0
MOE_BLOCK = 128
PLE_DIM = 256
RMS_EPS = 1e-5
LN_EPS = 1e-5

kernel_name = 'hybrid_gmlp_swa_sink_moe_decoder_step'


def rmsnorm(x, g):
    xf = x.astype(jnp.float32)
    y = xf * lax.rsqrt(jnp.mean(xf * xf, axis=-1, keepdims=True) + RMS_EPS)
    return (y * g.astype(jnp.float32)).astype(x.dtype)


def group_layernorm(v, g, b):
    vf = v.astype(jnp.float32)
    mu = jnp.mean(vf, axis=-1, keepdims=True)
    var = jnp.mean(jnp.square(vf - mu), axis=-1, keepdims=True)
    y = (vf - mu) * lax.rsqrt(var + LN_EPS) * g.astype(jnp.float32) + b.astype(jnp.float32)
    return y.astype(v.dtype)


def rope(x, pos):
    half = HEAD_DIM // 2
    inv = ROPE_THETA ** (-jnp.arange(half, dtype=jnp.float32) / half)
    ang = pos.astype(jnp.float32)[:, None] * inv[None, :]
    cos = jnp.cos(ang)[:, None, :]
    sin = jnp.sin(ang)[:, None, :]
    xf = x.astype(jnp.float32)
    x1, x2 = xf[..., :half], xf[..., half:]
    return jnp.concatenate([x1 * cos - x2 * sin, x2 * cos + x1 * sin], axis=-1).astype(x.dtype)


def softmax_with_sink(scores, sink):
    m = jnp.maximum(jnp.max(scores, axis=-1, keepdims=True), sink)
    e = jnp.exp(scores - m)
    return e / (jnp.sum(e, axis=-1, keepdims=True) + jnp.exp(sink - m))


def chunk_spatial_gate(u, v, w_s, b_s):
    bsz, s = u.shape[0], u.shape[1]
    n_chunks = -(-s // CHUNK)
    pad = n_chunks * CHUNK - s
    vp = jnp.pad(v, ((0, 0), (0, pad), (0, 0), (0, 0)))
    vp = vp.reshape(bsz, n_chunks, CHUNK, A_GROUPS, A_GROUP_DIM)
    causal = jnp.tril(jnp.ones((CHUNK, CHUNK), dtype=bool))
    ws = jnp.where(causal[None], w_s, 0.0).astype(v.dtype)
    z = jnp.einsum('gts,bnsgc->bntgc', ws, vp) + jnp.transpose(b_s)[None, None, :, :, None]
    z = z.reshape(bsz, n_chunks * CHUNK, A_GROUPS, A_GROUP_DIM)[:, :s]
    return u * z


def prompt_attention(q, k, v, sinks):
    bsz, s = q.shape[0], q.shape[1]
    nb = s // WINDOW
    qb = q.reshape(bsz, nb, WINDOW, N_KV_HEADS, GQA_GROUP, HEAD_DIM)

    def band(t):
        tb = t.reshape(bsz, nb, WINDOW, N_KV_HEADS, HEAD_DIM)
        prev = jnp.pad(tb, ((0, 0), (1, 0), (0, 0), (0, 0), (0, 0)))[:, :nb]
        return jnp.concatenate([prev, tb], axis=2)

    kb, vb = band(k), band(v)
    scores = jnp.einsum('bnqhgd,bnkhd->bnhgqk', qb, kb).astype(jnp.float32) * (HEAD_DIM ** -0.5)
    blk = jnp.arange(nb, dtype=jnp.int32)[:, None] * WINDOW
    qpos = blk + jnp.arange(WINDOW, dtype=jnp.int32)[None, :]
    kpos = blk - WINDOW + jnp.arange(2 * WINDOW, dtype=jnp.int32)[None, :]
    diff = qpos[:, :, None] - kpos[:, None, :]
    valid = (diff >= 0) & (diff < WINDOW) & (kpos[:, None, :] >= 0)
    scores = jnp.where(valid[None, :, None, None], scores, -jnp.inf)
    sink = sinks.astype(jnp.float32).reshape(N_KV_HEADS, GQA_GROUP)[None, None, :, :, None, None]
    probs = softmax_with_sink(scores, sink).astype(vb.dtype)
    out = jnp.einsum('bnhgqk,bnkhd->bnqhgd', probs, vb).reshape(bsz, s, Q_WIDTH)
    return out, k[:, -WINDOW:], v[:, -WINDOW:]


def sample_attention(q, k_new, v_new, sinks, k_buf, v_buf):
    db, t = q.shape[0], q.shape[1]
    kk = jnp.concatenate([k_buf.astype(k_new.dtype), k_new], axis=1)
    vv = jnp.concatenate([v_buf.astype(v_new.dtype), v_new], axis=1)
    qg = q.reshape(db, t, N_KV_HEADS, GQA_GROUP, HEAD_DIM)
    scores = jnp.einsum('bqhgd,bkhd->bhgqk', qg, kk).astype(jnp.float32) * (HEAD_DIM ** -0.5)
    diff = (WINDOW + jnp.arange(t, dtype=jnp.int32))[:, None] - jnp.arange(WINDOW + t, dtype=jnp.int32)[None, :]
    valid = (diff >= 0) & (diff < WINDOW)
    scores = jnp.where(valid, scores, -jnp.inf)
    sink = sinks.astype(jnp.float32).reshape(N_KV_HEADS, GQA_GROUP)[None, :, :, None, None]
    probs = softmax_with_sink(scores, sink).astype(vv.dtype)
    out = jnp.einsum('bhgqk,bkhd->bqhgd', probs, vv).reshape(db, t, Q_WIDTH)
    return out, kk[:, -WINDOW:], vv[:, -WINDOW:]


def moe(h, w_router, b_router, w_gu, b_gu, w_down, b_down):
    n = h.shape[0]
    nk = n * TOP_K
    logits = (h @ w_router).astype(jnp.float32) + b_router.astype(jnp.float32)
    top_logit, top_idx = lax.top_k(logits, TOP_K)
    gate = jax.nn.softmax(top_logit, axis=-1)
    flat_e = top_idx.reshape(-1).astype(jnp.int32)
    flat_tok = jnp.arange(nk, dtype=jnp.int32) // TOP_K
    flat_w = gate.reshape(-1)
    order = jnp.argsort(flat_e)
    se, stok, sw = flat_e[order], flat_tok[order], flat_w[order]
    counts = jnp.bincount(flat_e, length=N_EXPERTS).astype(jnp.int32)
    blocks_per = (counts + MOE_BLOCK - 1) // MOE_BLOCK
    block_end = jnp.cumsum(blocks_per)
    block_start = block_end - blocks_per
    start = jnp.cumsum(counts) - counts
    dest = block_start[se] * MOE_BLOCK + jnp.arange(nk, dtype=jnp.int32) - start[se]
    n_blocks = -(-nk // MOE_BLOCK) + N_EXPERTS
    slot_tok = jnp.full((n_blocks * MOE_BLOCK,), n, jnp.int32).at[dest].set(stok)
    block_expert = jnp.minimum(
        jnp.searchsorted(block_end, jnp.arange(n_blocks, dtype=jnp.int32), side='right'),
        N_EXPERTS - 1).astype(jnp.int32)
    h_pad = jnp.concatenate([h, jnp.zeros((1, h.shape[1]), h.dtype)], axis=0)
    xb = h_pad[slot_tok].reshape(n_blocks, MOE_BLOCK, h.shape[1])

    def expert_block(args):
        xe, e = args
        gu = xe @ w_gu[e] + b_gu[e]
        g = jnp.minimum(gu[:, :D_EXPERT], SWIGLU_LIMIT)
        u = jnp.clip(gu[:, D_EXPERT:], -SWIGLU_LIMIT, SWIGLU_LIMIT)
        a = (u + 1.0) * (g * jax.nn.sigmoid(SWIGLU_ALPHA * g))
        return a @ w_down[e] + b_down[e]

    yb = lax.map(expert_block, (xb, block_expert)).reshape(n_blocks * MOE_BLOCK, -1)
    contrib = yb[dest] * sw[:, None].astype(yb.dtype)
    return jnp.zeros((n, yb.shape[1]), yb.dtype).at[stok].add(contrib)


def decoder_layer(x, ple, pos, attend, lw):
    (g_mix, w_in, a_ln_g, a_ln_b, a_ws, a_bs, sinks, w_pa, w_pb, w_o, g_ffn,
     w_router, b_router, w_gu, b_gu, w_down, b_down, w_ple, g_ple, w_ple_gate) = lw
    bsz, s, _ = x.shape
    h = rmsnorm(x, g_mix)
    z = h @ w_in
    o_q = 2 * A_WIDTH
    o_k = o_q + Q_WIDTH
    o_v = o_k + KV_WIDTH
    o_ga = o_v + KV_WIDTH
    o_gb = o_ga + D_MODEL
    uv = jax.nn.gelu(z[..., :o_q], approximate=False)
    u = uv[..., :A_WIDTH].reshape(bsz, s, A_GROUPS, A_GROUP_DIM)
    va = group_layernorm(uv[..., A_WIDTH:].reshape(bsz, s, A_GROUPS, A_GROUP_DIM), a_ln_g, a_ln_b)
    q = rope(z[..., o_q:o_k].reshape(bsz, s, N_HEADS, HEAD_DIM), pos)
    k = rope(z[..., o_k:o_v].reshape(bsz, s, N_KV_HEADS, HEAD_DIM), pos)
    vb = z[..., o_v:o_ga].reshape(bsz, s, N_KV_HEADS, HEAD_DIM)
    gate_a = jax.nn.sigmoid(z[..., o_ga:o_gb])
    gate_b = jax.nn.sigmoid(z[..., o_gb:])
    ya = chunk_spatial_gate(u, va, a_ws, a_bs).reshape(bsz, s, A_WIDTH) @ w_pa
    att, new_k, new_v = attend(q, k, vb, sinks)
    yb = att @ w_pb
    x = x + (gate_a * ya + gate_b * yb) @ w_o
    h2 = rmsnorm(x, g_ffn)
    x = x + moe(h2.reshape(bsz * s, D_MODEL), w_router, b_router, w_gu, b_gu, w_down, b_down).reshape(bsz, s, D_MODEL)
    e = rmsnorm(ple.astype(x.dtype) @ w_ple, g_ple)
    x = x + jax.nn.sigmoid(x @ w_ple_gate) * e
    return x, new_k, new_v, va.reshape(bsz, s, A_WIDTH)


def setup_inputs(seed: int = 0) -> dict:
    key = jax.random.key(seed)
    ks = jax.random.split(key, 32)

    def nrm(k, shape, scale):
        return jax.random.normal(k, shape, jnp.float32) * scale

    return {
        'x_prompt': nrm(ks[0], (BATCH, SEQ, D_MODEL), 1.0),
        'x_sample': nrm(ks[1], (DEC_BATCH, DEC_SEQ, D_MODEL), 1.0),
        'cache_win_k': nrm(ks[2], (DEPTH, DEC_BATCH, WINDOW, N_KV_HEADS, HEAD_DIM), 1.0),
        'cache_win_v': nrm(ks[3], (DEPTH, DEC_BATCH, WINDOW, N_KV_HEADS, HEAD_DIM), 1.0),
        'p_prompt': nrm(ks[4], (DEPTH, BATCH, SEQ, PLE_DIM), 1.0),
        'p_sample': nrm(ks[5], (DEPTH, DEC_BATCH, DEC_SEQ, PLE_DIM), 1.0),
        'g_mix': 1.0 + nrm(ks[6], (DEPTH, D_MODEL), 0.02),
        'w_in': nrm(ks[7], (DEPTH, D_MODEL, IN_COLS), D_MODEL ** -0.5),
        'a_ln_g': 1.0 + nrm(ks[8], (DEPTH, A_GROUPS, A_GROUP_DIM), 0.02),
        'a_ln_b': nrm(ks[9], (DEPTH, A_GROUPS, A_GROUP_DIM), 0.02),
        'a_ws': nrm(ks[10], (DEPTH, A_GROUPS, CHUNK, CHUNK), CHUNK ** -0.5),
        'a_bs': 1.0 + nrm(ks[11], (DEPTH, A_GROUPS, CHUNK), 0.1),
        'sinks': nrm(ks[12], (DEPTH, N_HEADS), 0.5),
        'w_pa': nrm(ks[13], (DEPTH, A_WIDTH, D_MODEL), A_WIDTH ** -0.5),
        'w_pb': nrm(ks[14], (DEPTH, Q_WIDTH, D_MODEL), Q_WIDTH ** -0.5),
        'w_o': nrm(ks[15], (DEPTH, D_MODEL, D_MODEL), D_MODEL ** -0.5),
        'g_ffn': 1.0 + nrm(ks[16], (DEPTH, D_MODEL), 0.02),
        'w_router': nrm(ks[17], (DEPTH, D_MODEL, N_EXPERTS), D_MODEL ** -0.5),
        'b_router': nrm(ks[18], (DEPTH, N_EXPERTS), 0.01),
        'w_gu': nrm(ks[19], (DEPTH, N_EXPERTS, D_MODEL, 2 * D_EXPERT), D_MODEL ** -0.5),
        'b_gu': nrm(ks[20], (DEPTH, N_EXPERTS, 2 * D_EXPERT), 0.01),
        'w_down': nrm(ks[21], (DEPTH, N_EXPERTS, D_EXPERT, D_MODEL), D_EXPERT ** -0.5),
        'b_down': nrm(ks[22], (DEPTH, N_EXPERTS, D_MODEL), 0.01),
        'w_ple': nrm(ks[23], (DEPTH, PLE_DIM, D_MODEL), PLE_DIM ** -0.5),
        'g_ple': 1.0 + nrm(ks[24], (DEPTH, D_MODEL), 0.02),
        'w_ple_gate': nrm(ks[25], (DEPTH, D_MODEL, D_MODEL), D_MODEL ** -0.5),
        'g_final': 1.0 + nrm(ks[26], (D_MODEL,), 0.02),
    }


def reference(x_prompt, x_sample, cache_win_k, cache_win_v, p_prompt, p_sample,
              g_mix, w_in, a_ln_g, a_ln_b, a_ws, a_bs, sinks, w_pa, w_pb, w_o, g_ffn,
              w_router, b_router, w_gu, b_gu, w_down, b_down, w_ple, g_ple, w_ple_gate, g_final):
    pos_p = jnp.arange(x_prompt.shape[1], dtype=jnp.int32)
    pos_s = PAST_LEN + jnp.arange(x_sample.shape[1], dtype=jnp.int32)
    xp, xs = x_prompt, x_sample
    kp_l, vp_l, ks_l, vs_l, cv_l = [], [], [], [], []
    for l in range(DEPTH):
        lw = (g_mix[l], w_in[l], a_ln_g[l], a_ln_b[l], a_ws[l], a_bs[l], sinks[l], w_pa[l], w_pb[l],
              w_o[l], g_ffn[l], w_router[l], b_router[l], w_gu[l], b_gu[l], w_down[l], b_down[l],
              w_ple[l], g_ple[l], w_ple_gate[l])
        xp, kp, vp, _ = decoder_layer(xp, p_prompt[l], pos_p, prompt_attention, lw)
        attend_s = functools.partial(sample_attention, k_buf=cache_win_k[l], v_buf=cache_win_v[l])
        xs, ksn, vsn, cv = decoder_layer(xs, p_sample[l], pos_s, attend_s, lw)
        kp_l.append(kp)
        vp_l.append(vp)
        ks_l.append(ksn)
        vs_l.append(vsn)
        cv_l.append(cv)
    y_prompt = rmsnorm(xp, g_final)
    y_sample = rmsnorm(xs, g_final)
    return (y_prompt, y_sample, jnp.stack(kp_l), jnp.stack(vp_l), jnp.stack(ks_l), jnp.stack(vs_l), jnp.stack(cv_l))
```

```python
import functools

import jax
import jax.numpy as jnp
from jax import lax
from jax.experimental import pallas as pl
from jax.experimental.pallas import tpu as pltpu

F32 = jnp.float32
BF16 = jnp.bfloat16
I32 = jnp.int32

D_MODEL = 1024
BATCH = 4
SEQ = 4096
DEC_BATCH = 128
PAST_LEN = 8192
CHUNK = 128
A_GROUPS = 4
A_GROUP_DIM = 128
A_WIDTH = A_GROUPS * A_GROUP_DIM
N_HEADS = 8
N_KV_HEADS = 2
HEAD_DIM = 64
Q_WIDTH = N_HEADS * HEAD_DIM
KV_WIDTH = N_KV_HEADS * HEAD_DIM
GQA_GROUP = N_HEADS // N_KV_HEADS
WINDOW = 128
ROPE_THETA = 10000.0
N_EXPERTS = 32
TOP_K = 4
D_EXPERT = D_MODEL
SWIGLU_ALPHA = 1.702
SWIGLU_LIMIT = 7.0
PLE_DIM = 256
RMS_EPS = 1e-5
LN_EPS = 1e-5

O_Q = 2 * A_WIDTH
O_K = O_Q + Q_WIDTH
O_V = O_K + KV_WIDTH
O_GA = O_V + KV_WIDTH
O_GB = O_GA + D_MODEL
IN_COLS = O_GB + D_MODEL

LANES = 128
SUBLANES = 8
LANE_CHUNKS = D_MODEL // LANES
VMEM_LIMIT = 56 * 1024 * 1024

N_PROMPT = BATCH * SEQ
TM = 256
TILES_PER_SEQ = SEQ // TM
BLOCKS_PER_TILE = TM // WINDOW

N_GROUPS = 4
GROUP_PROMPT = N_PROMPT // N_GROUPS
GROUP_SLOTS = GROUP_PROMPT + DEC_BATCH
GROUP_ASSIGN = GROUP_SLOTS * TOP_K
SLOT_TILES = GROUP_SLOTS // LANES
MOE_ROWS = 256
XS_STRIDE = MOE_ROWS + SUBLANES
SLOT_BITS = 13
assert GROUP_SLOTS < (1 << SLOT_BITS)


def _bdot(a, b):
    return jnp.dot(a, b, preferred_element_type=F32)


def _rmsnorm(x, g):
    return x * lax.rsqrt(jnp.mean(x * x, axis=-1, keepdims=True) + RMS_EPS) * g


def _gelu(x):
    return 0.5 * x * (1.0 + lax.erf(x * (0.5 ** 0.5)))


def _group_layernorm(v, g, b):
    cols = []
    for gi in range(A_GROUPS):
        s = slice(gi * A_GROUP_DIM, (gi + 1) * A_GROUP_DIM)
        vg = v[:, s]
        mu = jnp.mean(vg, axis=-1, keepdims=True)
        d = vg - mu
        var = jnp.mean(d * d, axis=-1, keepdims=True)
        cols.append(d * lax.rsqrt(var + LN_EPS) * g[:, s] + b[:, s])
    return jnp.concatenate(cols, axis=1)


def _rope(x, cos, sin_signed):
    width = x.shape[1]
    reps = width // LANES
    cosf = jnp.concatenate([cos] * reps, axis=1) if reps > 1 else cos
    sinf = jnp.concatenate([sin_signed] * reps, axis=1) if reps > 1 else sin_signed
    half = HEAD_DIM // 2
    lane = lax.broadcasted_iota(I32, x.shape, 1)
    up = pltpu.roll(x, width - half, 1)
    down = pltpu.roll(x, half, 1)
    partner = jnp.where((lane & (HEAD_DIM - 1)) < half, up, down)
    return x * cosf + partner * sinf


def _in_projection(x, gmix_ref, win_ref, lng_ref, lnb_ref, cos, sin_signed):
    hb = _rmsnorm(x, gmix_ref[...]).astype(BF16)
    zuv = _gelu(_bdot(hb, win_ref[:, 0:O_Q]))
    u = zuv[:, :A_WIDTH]
    va = _group_layernorm(zuv[:, A_WIDTH:], lng_ref[...], lnb_ref[...])
    zqkv = _bdot(hb, win_ref[:, O_Q:O_GA])
    q = _rope(zqkv[:, :Q_WIDTH], cos, sin_signed)
    k = _rope(zqkv[:, Q_WIDTH:Q_WIDTH + KV_WIDTH], cos, sin_signed)
    v = zqkv[:, Q_WIDTH + KV_WIDTH:]
    zg = _bdot(hb, win_ref[:, O_GA:IN_COLS])
    gate_a = jax.nn.sigmoid(zg[:, :D_MODEL])
    gate_b = jax.nn.sigmoid(zg[:, D_MODEL:])
    return u, va, q, k, v, gate_a, gate_b


def _merge_and_route(x, ya_in, att, gate_a, gate_b, wpa_ref, wpb_ref, wo_ref, gffn_ref, wrt_ref, br_ref):
    ya = _bdot(ya_in.astype(BF16), wpa_ref[...])
    yb = _bdot(att.astype(BF16), wpb_ref[...])
    mix = (gate_a * ya + gate_b * yb).astype(BF16)
    x1 = x + _bdot(mix, wo_ref[...])
    h2 = _rmsnorm(x1, gffn_ref[...])
    logits = lax.dot_general(wrt_ref[...], h2, (((1,), (1,)), ((), ())),
                             precision=lax.Precision.HIGHEST, preferred_element_type=F32) + br_ref[...]
    eid = lax.broadcasted_iota(I32, logits.shape, 0)
    vals, idxs = [], []
    for _ in range(TOP_K):
        m = jnp.max(logits, axis=0, keepdims=True)
        idx = jnp.min(jnp.where(logits == m, eid, N_EXPERTS), axis=0, keepdims=True)
        logits = jnp.where(eid == idx, -jnp.inf, logits)
        vals.append(m)
        idxs.append(idx)
    es = [jnp.exp(v - vals[0]) for v in vals]
    inv = 1.0 / (es[0] + es[1] + es[2] + es[3])
    topw = jnp.concatenate([e * inv for e in es], axis=0)
    topi = jnp.concatenate(idxs, axis=0)
    return x1, h2, topi, topw


def _store_token_major(ref, val):
    rows = val.shape[0]
    for c in range(LANE_CHUNKS):
        ref[pl.ds(c, rows, stride=LANE_CHUNKS), :] = val[:, c * LANES:(c + 1) * LANES]


def _load_token_major(ref, rows):
    return jnp.concatenate([ref[pl.ds(c, rows, stride=LANE_CHUNKS), :] for c in range(LANE_CHUNKS)], axis=1)


def _band_attention(q_b, kk, vv, sinks_ref, first):
    rows = GQA_GROUP * WINDOW
    qi = lax.broadcasted_iota(I32, (rows, 2 * WINDOW), 0) & (WINDOW - 1)
    kj = lax.broadcasted_iota(I32, (rows, 2 * WINDOW), 1)
    lo = jnp.where(first, WINDOW, 0)
    valid = (kj > qi) & (kj <= qi + WINDOW) & (kj >= lo)
    outs = []
    for h in range(N_KV_HEADS):
        ks = slice(h * HEAD_DIM, (h + 1) * HEAD_DIM)
        qh = jnp.concatenate(
            [q_b[:, (h * GQA_GROUP + j) * HEAD_DIM:(h * GQA_GROUP + j + 1) * HEAD_DIM] for j in range(GQA_GROUP)],
            axis=0).astype(BF16)
        s = lax.dot_general(qh, kk[:, ks].astype(BF16), (((1,), (1,)), ((), ())),
                            preferred_element_type=F32) * (HEAD_DIM ** -0.5)
        s = jnp.where(valid, s, -jnp.inf)
        sink = jnp.concatenate(
            [jnp.full((WINDOW, 1), sinks_ref[h * GQA_GROUP + j], F32) for j in range(GQA_GROUP)], axis=0)
        m = jnp.maximum(jnp.max(s, axis=-1, keepdims=True), sink)
        e = jnp.exp(s - m)
        inv = 1.0 / (jnp.sum(e, axis=-1, keepdims=True) + jnp.exp(sink - m))
        o = _bdot((e * inv).astype(BF16), vv[:, ks].astype(BF16))
        outs.extend(o[j * WINDOW:(j + 1) * WINDOW, :] for j in range(GQA_GROUP))
    return jnp.concatenate(outs, axis=1)


def _prompt_front_kernel(x_ref, cos_ref, sin_ref, gmix_ref, win_ref, lng_ref, lnb_ref, ws_ref, bsf_ref,
                         sinks_ref, wpa_ref, wpb_ref, wo_ref, gffn_ref, wrt_ref, br_ref,
                         x1_ref, h2_ref, topi_ref, topw_ref, kwin_ref, vwin_ref, kprev_ref, vprev_ref):
    i = pl.program_id(0)
    seq_start = (i % TILES_PER_SEQ) == 0

    @pl.when(seq_start)
    def _():
        kprev_ref[...] = jnp.zeros_like(kprev_ref)
        vprev_ref[...] = jnp.zeros_like(vprev_ref)

    x = x_ref[...]
    u, va, q, k, v, gate_a, gate_b = _in_projection(
        x, gmix_ref, win_ref, lng_ref, lnb_ref, cos_ref[...], sin_ref[...])

    ya_rows, att_rows = [], []
    k_before, v_before = kprev_ref[...], vprev_ref[...]
    for b in range(BLOCKS_PER_TILE):
        r = slice(b * WINDOW, (b + 1) * WINDOW)
        zc = jnp.concatenate(
            [_bdot(ws_ref[g], va[r, g * A_GROUP_DIM:(g + 1) * A_GROUP_DIM].astype(BF16)) for g in range(A_GROUPS)],
            axis=1) + bsf_ref[...]
        ya_rows.append(u[r] * zc)
        kk = jnp.concatenate([k_before, k[r]], axis=0)
        vv = jnp.concatenate([v_before, v[r]], axis=0)
        first = jnp.logical_and(seq_start, b == 0)
        att_rows.append(_band_attention(q[r], kk, vv, sinks_ref, first))
        k_before, v_before = k[r], v[r]
    kprev_ref[...] = k_before
    vprev_ref[...] = v_before
    kwin_ref[0] = k_before
    vwin_ref[0] = v_before

    x1, h2, topi, topw = _merge_and_route(
        x, jnp.concatenate(ya_rows, axis=0), jnp.concatenate(att_rows, axis=0), gate_a, gate_b,
        wpa_ref, wpb_ref, wo_ref, gffn_ref, wrt_ref, br_ref)
    _store_token_major(x1_ref, x1)
    _store_token_major(h2_ref, h2)
    topi_ref[...] = topi
    topw_ref[...] = topw


def _full(shape):
    return pl.BlockSpec(shape, lambda i: (0,) * len(shape))


def _prompt_front(x, cos, sin, gmix, win, lng, lnb, ws, bsf, sinks, wpa, wpb, wo, gffn, wrt, br):
    n = x.shape[0]
    grid = (n // TM,)
    in_specs = [
        pl.BlockSpec((TM, D_MODEL), lambda i: (i, 0)),
        pl.BlockSpec((TM, LANES), lambda i: (i % TILES_PER_SEQ, 0)),
        pl.BlockSpec((TM, LANES), lambda i: (i % TILES_PER_SEQ, 0)),
        _full((1, D_MODEL)),
        _full((D_MODEL, IN_COLS)),
        _full((1, A_WIDTH)),
        _full((1, A_WIDTH)),
        _full((A_GROUPS, CHUNK, CHUNK)),
        _full((CHUNK, A_WIDTH)),
        pl.BlockSpec(memory_space=pltpu.SMEM),
        _full((A_WIDTH, D_MODEL)),
        _full((Q_WIDTH, D_MODEL)),
        _full((D_MODEL, D_MODEL)),
        _full((1, D_MODEL)),
        _full((N_EXPERTS, D_MODEL)),
        _full((N_EXPERTS, 1)),
    ]
    out_shape = [
        jax.ShapeDtypeStruct((n * LANE_CHUNKS, LANES), F32),
        jax.ShapeDtypeStruct((n * LANE_CHUNKS, LANES), F32),
        jax.ShapeDtypeStruct((TOP_K, n), I32),
        jax.ShapeDtypeStruct((TOP_K, n), F32),
        jax.ShapeDtypeStruct((n // SEQ, WINDOW, KV_WIDTH), F32),
        jax.ShapeDtypeStruct((n // SEQ, WINDOW, KV_WIDTH), F32),
    ]
    out_specs = [
        pl.BlockSpec((TM * LANE_CHUNKS, LANES), lambda i: (i, 0)),
        pl.BlockSpec((TM * LANE_CHUNKS, LANES), lambda i: (i, 0)),
        pl.BlockSpec((TOP_K, TM), lambda i: (0, i)),
        pl.BlockSpec((TOP_K, TM), lambda i: (0, i)),
        pl.BlockSpec((1, WINDOW, KV_WIDTH), lambda i: (i // TILES_PER_SEQ, 0, 0)),
        pl.BlockSpec((1, WINDOW, KV_WIDTH), lambda i: (i // TILES_PER_SEQ, 0, 0)),
    ]
    return pl.pallas_call(
        _prompt_front_kernel,
        grid=grid,
        in_specs=in_specs,
        out_specs=out_specs,
        out_shape=out_shape,
        scratch_shapes=[pltpu.VMEM((WINDOW, KV_WIDTH), F32), pltpu.VMEM((WINDOW, KV_WIDTH), F32)],
        compiler_params=pltpu.CompilerParams(dimension_semantics=("arbitrary",), vmem_limit_bytes=VMEM_LIMIT),
        name="prompt_front",
    )(x, cos, sin, gmix, win, lng, lnb, ws, bsf, sinks, wpa, wpb, wo, gffn, wrt, br)


SAMPLE_STEP = 16
SAMPLE_STEPS = DEC_BATCH // SAMPLE_STEP


def _sample_kernel(x_ref, cos_ref, sin_ref, gmix_ref, win_ref, lng_ref, lnb_ref, wdiag_ref, bs0_ref, sinks_ref,
                   kc_ref, vc_ref, wpa_ref, wpb_ref, wo_ref, gffn_ref, wrt_ref, br_ref,
                   x1_ref, h2_ref, topi_ref, topw_ref, kwin_ref, vwin_ref, va_ref,
                   q_s, k_s, v_s, yain_s, ga_s, gb_s, att_s):
    i = pl.program_id(0)

    @pl.when(i == 0)
    def _():
        x = x_ref[...]
        cos = jnp.broadcast_to(cos_ref[...], (DEC_BATCH, LANES))
        sin = jnp.broadcast_to(sin_ref[...], (DEC_BATCH, LANES))
        u, va, q, k, v, gate_a, gate_b = _in_projection(x, gmix_ref, win_ref, lng_ref, lnb_ref, cos, sin)
        va_ref[...] = va
        z = wdiag_ref[...].astype(F32) * va.astype(BF16).astype(F32) + bs0_ref[...]
        yain_s[...] = u * z
        q_s[...] = q
        k_s[...] = k
        v_s[...] = v
        ga_s[...] = gate_a
        gb_s[...] = gate_b

    r0 = pl.multiple_of(i * SAMPLE_STEP, SAMPLE_STEP)
    kwin = jnp.concatenate([kc_ref[:, 1:, :], k_s[pl.ds(r0, SAMPLE_STEP), :][:, None, :]], axis=1)
    vwin = jnp.concatenate([vc_ref[:, 1:, :], v_s[pl.ds(r0, SAMPLE_STEP), :][:, None, :]], axis=1)
    kwin_ref[...] = kwin
    vwin_ref[...] = vwin

    q16 = q_s[pl.ds(r0, SAMPLE_STEP), :]
    lane = lax.broadcasted_iota(I32, (SAMPLE_STEP, LANES), 1)
    heads = []
    for hq in range(N_HEADS):
        c, p, h = hq // 2, hq % 2, hq // GQA_GROUP
        chunk = q16[:, c * LANES:(c + 1) * LANES]
        if p != h:
            chunk = pltpu.roll(chunk, HEAD_DIM, 1)
        keep = (lane < HEAD_DIM) if h == 0 else (lane >= HEAD_DIM)
        heads.append(jnp.where(keep, chunk, 0.0))
    qpad = pltpu.einshape("hbd->bhd", jnp.stack(heads, axis=0)).astype(BF16)
    s = jnp.einsum("bhd,bkd->bhk", qpad, kwin.astype(BF16), preferred_element_type=F32) * (HEAD_DIM ** -0.5)
    hid = lax.broadcasted_iota(I32, (1, N_HEADS, 1), 1)
    sink = jnp.zeros((1, N_HEADS, 1), F32)
    for hq in range(N_HEADS):
        sink = jnp.where(hid == hq, sinks_ref[hq], sink)
    m = jnp.maximum(jnp.max(s, axis=-1, keepdims=True), sink)
    e = jnp.exp(s - m)
    inv = 1.0 / (jnp.sum(e, axis=-1, keepdims=True) + jnp.exp(sink - m))
    o = jnp.einsum("bhk,bkd->bhd", (e * inv).astype(BF16), vwin.astype(BF16), preferred_element_type=F32)
    o = pltpu.einshape("bhd->hbd", o)
    chunks = []
    for c in range(N_HEADS // 2):
        parts = []
        for p in range(2):
            hq = 2 * c + p
            oh = o[hq]
            if p != hq // GQA_GROUP:
                oh = pltpu.roll(oh, HEAD_DIM, 1)
            parts.append(oh)
        chunks.append(jnp.where(lane < HEAD_DIM, parts[0], parts[1]))
    att_s[pl.ds(r0, SAMPLE_STEP), :] = jnp.concatenate(chunks, axis=1)

    @pl.when(i == SAMPLE_STEPS - 1)
    def _():
        x1, h2, topi, topw = _merge_and_route(
            x_ref[...], yain_s[...], att_s[...], ga_s[...], gb_s[...],
            wpa_ref, wpb_ref, wo_ref, gffn_ref, wrt_ref, br_ref)
        _store_token_major(x1_ref, x1)
        _store_token_major(h2_ref, h2)
        topi_ref[...] = topi
        topw_ref[...] = topw


def _sample_front(x, cos, sin, gmix, win, lng, lnb, wdiag, bs0, sinks, kc, vc, wpa, wpb, wo, gffn, wrt, br):
    n = DEC_BATCH
    cache_spec = pl.BlockSpec((SAMPLE_STEP, WINDOW, KV_WIDTH), lambda i: (i, 0, 0))
    in_specs = [
        _full((n, D_MODEL)),
        _full((1, LANES)),
        _full((1, LANES)),
        _full((1, D_MODEL)),
        _full((D_MODEL, IN_COLS)),
        _full((1, A_WIDTH)),
        _full((1, A_WIDTH)),
        _full((1, A_WIDTH)),
        _full((1, A_WIDTH)),
        pl.BlockSpec(memory_space=pltpu.SMEM),
        cache_spec,
        cache_spec,
        _full((A_WIDTH, D_MODEL)),
        _full((Q_WIDTH, D_MODEL)),
        _full((D_MODEL, D_MODEL)),
        _full((1, D_MODEL)),
        _full((N_EXPERTS, D_MODEL)),
        _full((N_EXPERTS, 1)),
    ]
    out_shape = [
        jax.ShapeDtypeStruct((n * LANE_CHUNKS, LANES), F32),
        jax.ShapeDtypeStruct((n * LANE_CHUNKS, LANES), F32),
        jax.ShapeDtypeStruct((TOP_K, n), I32),
        jax.ShapeDtypeStruct((TOP_K, n), F32),
        jax.ShapeDtypeStruct((n, WINDOW, KV_WIDTH), F32),
        jax.ShapeDtypeStruct((n, WINDOW, KV_WIDTH), F32),
        jax.ShapeDtypeStruct((n, A_WIDTH), F32),
    ]
    out_specs = [
        _full((n * LANE_CHUNKS, LANES)),
        _full((n * LANE_CHUNKS, LANES)),
        _full((TOP_K, n)),
        _full((TOP_K, n)),
        cache_spec,
        cache_spec,
        _full((n, A_WIDTH)),
    ]
    scratch = [
        pltpu.VMEM((n, Q_WIDTH), F32), pltpu.VMEM((n, KV_WIDTH), F32), pltpu.VMEM((n, KV_WIDTH), F32),
        pltpu.VMEM((n, A_WIDTH), F32), pltpu.VMEM((n, D_MODEL), F32), pltpu.VMEM((n, D_MODEL), F32),
        pltpu.VMEM((n, Q_WIDTH), F32),
    ]
    return pl.pallas_call(
        _sample_kernel,
        grid=(SAMPLE_STEPS,),
        in_specs=in_specs,
        out_specs=out_specs,
        out_shape=out_shape,
        scratch_shapes=scratch,
        compiler_params=pltpu.CompilerParams(dimension_semantics=("arbitrary",), vmem_limit_bytes=VMEM_LIMIT),
        name="sample_front",
    )(x, cos, sin, gmix, win, lng, lnb, wdiag, bs0, sinks, kc, vc, wpa, wpb, wo, gffn, wrt, br)


def _route_plan_kernel(ip_ref, is_ref, wp_ref, wsm_ref, dest_ref, wts_ref, off_ref):
    g = pl.program_id(0)
    e_s = jnp.where(g == N_GROUPS - 1, is_ref[...], N_EXPERTS)
    eall = jnp.concatenate([ip_ref[...], e_s], axis=1)
    wts_ref[...] = jnp.concatenate([wp_ref[...], wsm_ref[...]], axis=1)
    eid = lax.broadcasted_iota(I32, (N_EXPERTS, GROUP_SLOTS), 0)
    onehots = [eall[k:k + 1, :] == eid for k in range(TOP_K)]
    count = jnp.zeros((N_EXPERTS, GROUP_SLOTS), F32)
    for oh in onehots:
        count = count + oh.astype(F32)
    total = jnp.broadcast_to(jnp.sum(count, axis=1, keepdims=True), (N_EXPERTS, LANES))
    r = lax.broadcasted_iota(I32, (N_EXPERTS, N_EXPERTS), 0)
    c = lax.broadcasted_iota(I32, (N_EXPERTS, N_EXPERTS), 1)
    start = lax.dot_general((c < r).astype(F32), total, (((1,), (0,)), ((), ())),
                            precision=lax.Precision.HIGHEST, preferred_element_type=F32)
    lane = lax.broadcasted_iota(I32, (N_EXPERTS, LANES), 1)
    off_ref[...] = jnp.where(lane == 0, start, start + total).astype(I32)
    ti = lax.broadcasted_iota(I32, (LANES, LANES), 0)
    tj = lax.broadcasted_iota(I32, (LANES, LANES), 1)
    before = (ti < tj).astype(BF16)
    ones = jnp.ones((LANES, LANES), BF16)
    running = start
    for t in range(SLOT_TILES):
        sl = slice(t * LANES, (t + 1) * LANES)
        cb = count[:, sl].astype(BF16)
        pos = running + _bdot(cb, before)
        rows = [jnp.sum(jnp.where(oh[:, sl], pos, 0.0), axis=0, keepdims=True) for oh in onehots]
        dest_ref[:, sl] = jnp.concatenate(rows, axis=0).astype(I32)
        running = running + _bdot(cb, ones)


def _route_plan(topi_p, topi_s, topw_p, topw_s):
    in_specs = [
        pl.BlockSpec((TOP_K, GROUP_PROMPT), lambda g: (0, g)),
        pl.BlockSpec((TOP_K, DEC_BATCH), lambda g: (0, 0)),
        pl.BlockSpec((TOP_K, GROUP_PROMPT), lambda g: (0, g)),
        pl.BlockSpec((TOP_K, DEC_BATCH), lambda g: (0, 0)),
    ]
    out_shape = [
        jax.ShapeDtypeStruct((N_GROUPS, TOP_K, GROUP_SLOTS), I32),
        jax.ShapeDtypeStruct((N_GROUPS, TOP_K, GROUP_SLOTS), F32),
        jax.ShapeDtypeStruct((N_GROUPS, N_EXPERTS, LANES), I32),
    ]
    out_specs = [
        pl.BlockSpec((None, TOP_K, GROUP_SLOTS), lambda g: (g, 0, 0)),
        pl.BlockSpec((None, TOP_K, GROUP_SLOTS), lambda g: (g, 0, 0)),
        pl.BlockSpec((None, N_EXPERTS, LANES), lambda g: (g, 0, 0)),
    ]
    return pl.pallas_call(
        _route_plan_kernel,
        grid=(N_GROUPS,),
        in_specs=in_specs,
        out_specs=out_specs,
        out_shape=out_shape,
        compiler_params=pltpu.CompilerParams(dimension_semantics=("arbitrary",)),
        name="route_plan",
    )(topi_p, topi_s, topw_p, topw_s)


GROUP_ROWS = GROUP_PROMPT * LANE_CHUNKS
SAMPLE_ROWS = DEC_BATCH * LANE_CHUNKS
TRASH_SLOT = GROUP_SLOTS
BUF_ROWS = (GROUP_SLOTS + 1) * LANE_CHUNKS
SCATTER_BATCH = 8


def _moe_kernel(dest_ref, wts_ref, off_ref, h2p_ref, h2s_ref, x1p_ref, x1s_ref, wgu_ref, bgu_ref, wdn_ref, bdn_ref,
                x2p_ref, x2s_ref,
                h2buf, acc, wgubuf, bgubuf, wdnbuf, bdnbuf, xs, ys, src_ref, act_sem, w_sem):
    g = pl.program_id(0)
    last = g == N_GROUPS - 1
    row0 = pl.multiple_of(g * GROUP_ROWS, GROUP_ROWS)

    def prompt_copies():
        return (pltpu.make_async_copy(h2p_ref.at[pl.ds(row0, GROUP_ROWS)], h2buf.at[pl.ds(0, GROUP_ROWS)], act_sem.at[0]),
                pltpu.make_async_copy(x1p_ref.at[pl.ds(row0, GROUP_ROWS)], acc.at[pl.ds(0, GROUP_ROWS)], act_sem.at[1]))

    def sample_copies():
        return (pltpu.make_async_copy(h2s_ref, h2buf.at[pl.ds(GROUP_ROWS, SAMPLE_ROWS)], act_sem.at[2]),
                pltpu.make_async_copy(x1s_ref, acc.at[pl.ds(GROUP_ROWS, SAMPLE_ROWS)], act_sem.at[3]))

    def weight_copies(e, slot):
        return (pltpu.make_async_copy(wgu_ref.at[e], wgubuf.at[slot], w_sem.at[0, slot]),
                pltpu.make_async_copy(bgu_ref.at[e], bgubuf.at[slot], w_sem.at[1, slot]),
                pltpu.make_async_copy(wdn_ref.at[e], wdnbuf.at[slot], w_sem.at[2, slot]),
                pltpu.make_async_copy(bdn_ref.at[e], bdnbuf.at[slot], w_sem.at[3, slot]))

    for cp in prompt_copies():
        cp.start()

    @pl.when(last)
    def _():
        for cp in sample_copies():
            cp.start()

    for cp in weight_copies(0, 0):
        cp.start()

    trash = pl.ds(TRASH_SLOT * LANE_CHUNKS, LANE_CHUNKS)
    h2buf[trash, :] = jnp.zeros((LANE_CHUNKS, LANES), F32)
    acc[trash, :] = jnp.zeros((LANE_CHUNKS, LANES), F32)

    nvalid = jnp.where(last, GROUP_SLOTS, GROUP_PROMPT)
    for k in range(TOP_K):
        def fill(j, carry, k=k):
            s0 = j * SUBLANES
            for d in range(SUBLANES):
                s = s0 + d
                src_ref[dest_ref[k, s]] = s | (k << SLOT_BITS)
            return carry
        lax.fori_loop(0, nvalid // SUBLANES, fill, 0)

    for cp in prompt_copies():
        cp.wait()

    @pl.when(last)
    def _():
        for cp in sample_copies():
            cp.wait()

    def expert(e, carry):
        slot = e & 1
        for cp in weight_copies(e, slot):
            cp.wait()

        @pl.when(e + 1 < N_EXPERTS)
        def _():
            for cp in weight_copies(e + 1, 1 - slot):
                cp.start()

        begin = off_ref[e, 0]
        end = off_ref[e, 1]
        nblk = (end - begin + MOE_ROWS - 1) // MOE_ROWS

        def block(blk, carry2):
            base = begin + blk * MOE_ROWS

            def lookup(m):
                p = base + m
                ok = p < end
                code = src_ref[jnp.minimum(p, end - 1)]
                slot_id = code & ((1 << SLOT_BITS) - 1)
                tok = jnp.where(ok, slot_id, TRASH_SLOT)
                w = jnp.where(ok, wts_ref[code >> SLOT_BITS, slot_id], 0.0)
                return pl.ds(pl.multiple_of(tok * LANE_CHUNKS, LANE_CHUNKS), LANE_CHUNKS), w

            for m in range(MOE_ROWS):
                rows, _ = lookup(m)
                xs[pl.ds(m, LANE_CHUNKS, stride=XS_STRIDE), :] = h2buf[rows, :]
            x = jnp.concatenate(
                [xs[c * XS_STRIDE:c * XS_STRIDE + MOE_ROWS, :] for c in range(LANE_CHUNKS)], axis=1).astype(BF16)
            gu = _bdot(x, wgubuf[slot]) + bgubuf[slot]
            gl = jnp.minimum(gu[:, :D_EXPERT], SWIGLU_LIMIT)
            ul = jnp.clip(gu[:, D_EXPERT:], -SWIGLU_LIMIT, SWIGLU_LIMIT)
            a = (ul + 1.0) * (gl * jax.nn.sigmoid(SWIGLU_ALPHA * gl))
            y = _bdot(a.astype(BF16), wdnbuf[slot]) + bdnbuf[slot]
            for c in range(LANE_CHUNKS):
                ys[c * XS_STRIDE:c * XS_STRIDE + MOE_ROWS, :] = y[:, c * LANES:(c + 1) * LANES]
            for m0 in range(0, MOE_ROWS, SCATTER_BATCH):
                pending = []
                for m in range(m0, m0 + SCATTER_BATCH):
                    rows, w = lookup(m)
                    pending.append((rows, acc[rows, :] + w * ys[pl.ds(m, LANE_CHUNKS, stride=XS_STRIDE), :]))
                for rows, val in pending:
                    acc[rows, :] = val
            return carry2

        lax.fori_loop(0, nblk, block, 0)
        return carry

    lax.fori_loop(0, N_EXPERTS, expert, 0)

    out_p = pltpu.make_async_copy(acc.at[pl.ds(0, GROUP_ROWS)], x2p_ref.at[pl.ds(row0, GROUP_ROWS)], act_sem.at[0])
    out_p.start()

    @pl.when(last)
    def _():
        out_s = pltpu.make_async_copy(acc.at[pl.ds(GROUP_ROWS, SAMPLE_ROWS)], x2s_ref, act_sem.at[2])
        out_s.start()
        out_s.wait()

    out_p.wait()


def _moe(dest, wts, off, h2p, h2s, x1p, x1s, wgu, bgu, wdn, bdn):
    smem = lambda: pl.BlockSpec((None, TOP_K, GROUP_SLOTS), lambda g: (g, 0, 0), memory_space=pltpu.SMEM)
    anyspec = pl.BlockSpec(memory_space=pl.ANY)
    in_specs = [
        smem(), smem(),
        pl.BlockSpec((None, N_EXPERTS, LANES), lambda g: (g, 0, 0), memory_space=pltpu.SMEM),
        anyspec, anyspec, anyspec, anyspec, anyspec, anyspec, anyspec, anyspec,
    ]
    scratch = [
        pltpu.VMEM((BUF_ROWS, LANES), F32),
        pltpu.VMEM((BUF_ROWS, LANES), F32),
        pltpu.VMEM((2, D_MODEL, 2 * D_EXPERT), BF16),
        pltpu.VMEM((2, 1, 2 * D_EXPERT), F32),
        pltpu.VMEM((2, D_EXPERT, D_MODEL), BF16),
        pltpu.VMEM((2, 1, D_MODEL), F32),
        pltpu.VMEM((LANE_CHUNKS * XS_STRIDE, LANES), F32),
        pltpu.VMEM((LANE_CHUNKS * XS_STRIDE, LANES), F32),
        pltpu.SMEM((GROUP_ASSIGN,), I32),
        pltpu.SemaphoreType.DMA((4,)),
        pltpu.SemaphoreType.DMA((4, 2)),
    ]
    return pl.pallas_call(
        _moe_kernel,
        grid=(N_GROUPS,),
        in_specs=in_specs,
        out_specs=[anyspec, anyspec],
        out_shape=[jax.ShapeDtypeStruct(x1p.shape, F32), jax.ShapeDtypeStruct(x1s.shape, F32)],
        scratch_shapes=scratch,
        compiler_params=pltpu.CompilerParams(dimension_semantics=("arbitrary",), vmem_limit_bytes=VMEM_LIMIT),
        name="moe",
    )(dest, wts, off, h2p, h2s, x1p, x1s, wgu, bgu, wdn, bdn)


def _ple_final_kernel(x2_ref, ple_ref, wple_ref, gple_ref, wpg_ref, gfin_ref, y_ref):
    rows = y_ref.shape[0]
    x2 = _load_token_major(x2_ref, rows)
    e = _rmsnorm(_bdot(ple_ref[...].astype(BF16), wple_ref[...]), gple_ref[...])
    x3 = x2 + jax.nn.sigmoid(_bdot(x2.astype(BF16), wpg_ref[...])) * e
    y_ref[...] = _rmsnorm(x3, gfin_ref[...])


def _ple_final(x2_tm, ple, wple, gple, wpg, gfin, tile):
    n = ple.shape[0]
    return pl.pallas_call(
        _ple_final_kernel,
        grid=(n // tile,),
        in_specs=[
            pl.BlockSpec((tile * LANE_CHUNKS, LANES), lambda i: (i, 0)),
            pl.BlockSpec((tile, PLE_DIM), lambda i: (i, 0)),
            _full((PLE_DIM, D_MODEL)),
            _full((1, D_MODEL)),
            _full((D_MODEL, D_MODEL)),
            _full((1, D_MODEL)),
        ],
        out_specs=pl.BlockSpec((tile, D_MODEL), lambda i: (i, 0)),
        out_shape=jax.ShapeDtypeStruct((n, D_MODEL), F32),
        compiler_params=pltpu.CompilerParams(dimension_semantics=("arbitrary",), vmem_limit_bytes=VMEM_LIMIT),
        name="ple_final",
    )(x2_tm, ple, wple, gple, wpg, gfin)


def _rope_tables(pos):
    half = HEAD_DIM // 2
    inv = ROPE_THETA ** (-jnp.arange(half, dtype=F32) / half)
    ang = pos.astype(F32)[:, None] * inv[None, :]
    cos, sin = jnp.cos(ang), jnp.sin(ang)
    cos2 = jnp.concatenate([cos, cos, cos, cos], axis=1)
    sin2 = jnp.concatenate([-sin, sin, -sin, sin], axis=1)
    return cos2, sin2


def _prep_weights(g_mix, w_in, a_ln_g, a_ln_b, a_ws, a_bs, w_pa, w_pb, w_o, g_ffn, w_router, b_router):
    causal = jnp.tril(jnp.ones((CHUNK, CHUNK), dtype=bool))
    return dict(
        gmix=g_mix.reshape(1, D_MODEL),
        win=w_in.astype(BF16),
        lng=a_ln_g.reshape(1, A_WIDTH),
        lnb=a_ln_b.reshape(1, A_WIDTH),
        ws=jnp.where(causal[None], a_ws, 0.0).astype(BF16),
        bsf=jnp.repeat(jnp.transpose(a_bs), A_GROUP_DIM, axis=1),
        wpa=w_pa.astype(BF16),
        wpb=w_pb.astype(BF16),
        wo=w_o.astype(BF16),
        gffn=g_ffn.reshape(1, D_MODEL),
        wrt=jnp.transpose(w_router),
        br=b_router.reshape(N_EXPERTS, 1),
    )


def kernel(x_prompt, x_sample, cache_win_k, cache_win_v, p_prompt, p_sample, g_mix, w_in, a_ln_g, a_ln_b, a_ws, a_bs, sinks, w_pa, w_pb, w_o, g_ffn, w_router, b_router, w_gu, b_gu, w_down, b_down, w_ple, g_ple, w_ple_gate, g_final):
    W = _prep_weights(g_mix[0], w_in[0], a_ln_g[0], a_ln_b[0], a_ws[0], a_bs[0], w_pa[0], w_pb[0], w_o[0],
                      g_ffn[0], w_router[0], b_router[0])
    cos_p, sin_p = _rope_tables(jnp.arange(SEQ, dtype=I32))
    cos_s, sin_s = _rope_tables(jnp.full((1,), PAST_LEN, I32))
    x1p, h2p, topi_p, topw_p, kwin_p, vwin_p = _prompt_front(
        x_prompt.reshape(N_PROMPT, D_MODEL), cos_p, sin_p, W["gmix"], W["win"], W["lng"], W["lnb"],
        W["ws"], W["bsf"], sinks[0], W["wpa"], W["wpb"], W["wo"], W["gffn"], W["wrt"], W["br"])

    wdiag = jnp.repeat(a_ws[0, :, 0, 0], A_GROUP_DIM)[None, :].astype(BF16)
    bs0 = jnp.repeat(a_bs[0, :, 0], A_GROUP_DIM)[None, :]
    x1s, h2s, topi_s, topw_s, kwin_s, vwin_s, va_s = _sample_front(
        x_sample.reshape(DEC_BATCH, D_MODEL), cos_s, sin_s, W["gmix"], W["win"], W["lng"], W["lnb"], wdiag, bs0,
        sinks[0], cache_win_k[0].reshape(DEC_BATCH, WINDOW, KV_WIDTH), cache_win_v[0].reshape(DEC_BATCH, WINDOW, KV_WIDTH),
        W["wpa"], W["wpb"], W["wo"], W["gffn"], W["wrt"], W["br"])

    dest, wts, off = _route_plan(topi_p, topi_s, topw_p, topw_s)
    x2p, x2s = _moe(dest, wts, off, h2p, h2s, x1p, x1s,
                    w_gu[0].astype(BF16), b_gu[0].reshape(N_EXPERTS, 1, 2 * D_EXPERT),
                    w_down[0].astype(BF16), b_down[0].reshape(N_EXPERTS, 1, D_MODEL))

    wple = w_ple[0].astype(BF16)
    gple = g_ple[0].reshape(1, D_MODEL)
    wpg = w_ple_gate[0].astype(BF16)
    gfin = g_final.reshape(1, D_MODEL)
    y_p = _ple_final(x2p, p_prompt[0].reshape(N_PROMPT, PLE_DIM), wple, gple, wpg, gfin, TM)
    y_s = _ple_final(x2s, p_sample[0].reshape(DEC_BATCH, PLE_DIM), wple, gple, wpg, gfin, DEC_BATCH)

    return (
        y_p.reshape(BATCH, SEQ, D_MODEL),
        y_s.reshape(DEC_BATCH, 1, D_MODEL),
        kwin_p.reshape(1, BATCH, WINDOW, N_KV_HEADS, HEAD_DIM),
        vwin_p.reshape(1, BATCH, WINDOW, N_KV_HEADS, HEAD_DIM),
        kwin_s.reshape(1, DEC_BATCH, WINDOW, N_KV_HEADS, HEAD_DIM),
        vwin_s.reshape(1, DEC_BATCH, WINDOW, N_KV_HEADS, HEAD_DIM),
        va_s.reshape(1, DEC_BATCH, 1, A_WIDTH),
    )
```

```python
import functools

import jax
import jax.numpy as jnp
from jax import lax
from jax.experimental import pallas as pl
from jax.experimental.pallas import tpu as pltpu

F32 = jnp.float32
BF16 = jnp.bfloat16
I32 = jnp.int32

D_MODEL = 1024
BATCH = 4
SEQ = 4096
DEC_BATCH = 128
PAST_LEN = 8192
CHUNK = 128
A_GROUPS = 4
A_GROUP_DIM = 128
A_WIDTH = A_GROUPS * A_GROUP_DIM
N_HEADS = 8
N_KV_HEADS = 2
HEAD_DIM = 64
Q_WIDTH = N_HEADS * HEAD_DIM
KV_WIDTH = N_KV_HEADS * HEAD_DIM
GQA_GROUP = N_HEADS // N_KV_HEADS
WINDOW = 128
ROPE_THETA = 10000.0
N_EXPERTS = 32
TOP_K = 4
D_EXPERT = D_MODEL
SWIGLU_ALPHA = 1.702
SWIGLU_LIMIT = 7.0
PLE_DIM = 256
RMS_EPS = 1e-5
LN_EPS = 1e-5

O_Q = 2 * A_WIDTH
O_K = O_Q + Q_WIDTH
O_V = O_K + KV_WIDTH
O_GA = O_V + KV_WIDTH
O_GB = O_GA + D_MODEL
IN_COLS = O_GB + D_MODEL

LANES = 128
SUBLANES = 8
LANE_CHUNKS = D_MODEL // LANES
VMEM_LIMIT = 56 * 1024 * 1024

N_PROMPT = BATCH * SEQ
TM = 256
TILES_PER_SEQ = SEQ // TM
BLOCKS_PER_TILE = TM // WINDOW

N_GROUPS = 4
GROUP_PROMPT = N_PROMPT // N_GROUPS
GROUP_SLOTS = GROUP_PROMPT + DEC_BATCH
GROUP_ASSIGN = GROUP_SLOTS * TOP_K
SLOT_TILES = GROUP_SLOTS // LANES
MOE_ROWS = 256
XS_STRIDE = MOE_ROWS + SUBLANES
SLOT_BITS = 13
K_STRIDE = 1 << SLOT_BITS
assert GROUP_SLOTS < K_STRIDE
MAX_BLOCKS = GROUP_ASSIGN // MOE_ROWS + N_EXPERTS
POS_TABLE = 1 << 15
assert (MAX_BLOCKS + 3) * MOE_ROWS <= POS_TABLE
assert MAX_BLOCKS + 2 <= LANES


def _bdot(a, b):
    return jnp.dot(a, b, preferred_element_type=F32)


def _rmsnorm(x, g):
    return x * lax.rsqrt(jnp.mean(x * x, axis=-1, keepdims=True) + RMS_EPS) * g


def _gelu(x):
    return 0.5 * x * (1.0 + lax.erf(x * (0.5 ** 0.5)))


def _group_layernorm(v, g, b):
    cols = []
    for gi in range(A_GROUPS):
        s = slice(gi * A_GROUP_DIM, (gi + 1) * A_GROUP_DIM)
        vg = v[:, s]
        mu = jnp.mean(vg, axis=-1, keepdims=True)
        d = vg - mu
        var = jnp.mean(d * d, axis=-1, keepdims=True)
        cols.append(d * lax.rsqrt(var + LN_EPS) * g[:, s] + b[:, s])
    return jnp.concatenate(cols, axis=1)


def _rope(x, cos, sin_signed):
    width = x.shape[1]
    reps = width // LANES
    cosf = jnp.concatenate([cos] * reps, axis=1) if reps > 1 else cos
    sinf = jnp.concatenate([sin_signed] * reps, axis=1) if reps > 1 else sin_signed
    half = HEAD_DIM // 2
    lane = lax.broadcasted_iota(I32, x.shape, 1)
    up = pltpu.roll(x, width - half, 1)
    down = pltpu.roll(x, half, 1)
    partner = jnp.where((lane & (HEAD_DIM - 1)) < half, up, down)
    return x * cosf + partner * sinf


def _in_projection(x, gmix_ref, win_ref, lng_ref, lnb_ref, cos, sin_signed):
    hb = _rmsnorm(x, gmix_ref[...]).astype(BF16)
    zuv = _gelu(_bdot(hb, win_ref[:, 0:O_Q]))
    u = zuv[:, :A_WIDTH]
    va = _group_layernorm(zuv[:, A_WIDTH:], lng_ref[...], lnb_ref[...])
    zqkv = _bdot(hb, win_ref[:, O_Q:O_GA])
    q = _rope(zqkv[:, :Q_WIDTH], cos, sin_signed)
    k = _rope(zqkv[:, Q_WIDTH:Q_WIDTH + KV_WIDTH], cos, sin_signed)
    v = zqkv[:, Q_WIDTH + KV_WIDTH:]
    zg = _bdot(hb, win_ref[:, O_GA:IN_COLS])
    gate_a = jax.nn.sigmoid(zg[:, :D_MODEL])
    gate_b = jax.nn.sigmoid(zg[:, D_MODEL:])
    return u, va, q, k, v, gate_a, gate_b


def _merge_and_route(x, ya_in, att, gate_a, gate_b, wpa_ref, wpb_ref, wo_ref, gffn_ref, wrt_ref, br_ref):
    ya = _bdot(ya_in.astype(BF16), wpa_ref[...])
    yb = _bdot(att.astype(BF16), wpb_ref[...])
    mix = (gate_a * ya + gate_b * yb).astype(BF16)
    x1 = x + _bdot(mix, wo_ref[...])
    h2 = _rmsnorm(x1, gffn_ref[...])
    logits = lax.dot_general(wrt_ref[...], h2, (((1,), (1,)), ((), ())),
                             precision=lax.Precision.HIGHEST, preferred_element_type=F32) + br_ref[...]
    eid = lax.broadcasted_iota(I32, logits.shape, 0)
    vals, idxs = [], []
    for _ in range(TOP_K):
        m = jnp.max(logits, axis=0, keepdims=True)
        idx = jnp.min(jnp.where(logits == m, eid, N_EXPERTS), axis=0, keepdims=True)
        logits = jnp.where(eid == idx, -jnp.inf, logits)
        vals.append(m)
        idxs.append(idx)
    es = [jnp.exp(v - vals[0]) for v in vals]
    inv = 1.0 / (es[0] + es[1] + es[2] + es[3])
    topw = jnp.concatenate([e * inv for e in es], axis=0)
    topi = jnp.concatenate(idxs, axis=0)
    return x1, h2, topi, topw


def _store_token_major(ref, val):
    rows = val.shape[0]
    for c in range(LANE_CHUNKS):
        ref[pl.ds(c, rows, stride=LANE_CHUNKS), :] = val[:, c * LANES:(c + 1) * LANES]


def _load_token_major(ref, rows):
    return jnp.concatenate([ref[pl.ds(c, rows, stride=LANE_CHUNKS), :] for c in range(LANE_CHUNKS)], axis=1)


def _band_attention(q_b, kk, vv, sinks_ref, first):
    rows = GQA_GROUP * WINDOW
    qi = lax.broadcasted_iota(I32, (rows, 2 * WINDOW), 0) & (WINDOW - 1)
    kj = lax.broadcasted_iota(I32, (rows, 2 * WINDOW), 1)
    lo = jnp.where(first, WINDOW, 0)
    valid = (kj > qi) & (kj <= qi + WINDOW) & (kj >= lo)
    outs = []
    for h in range(N_KV_HEADS):
        ks = slice(h * HEAD_DIM, (h + 1) * HEAD_DIM)
        qh = jnp.concatenate(
            [q_b[:, (h * GQA_GROUP + j) * HEAD_DIM:(h * GQA_GROUP + j + 1) * HEAD_DIM] for j in range(GQA_GROUP)],
            axis=0).astype(BF16)
        s = lax.dot_general(qh, kk[:, ks].astype(BF16), (((1,), (1,)), ((), ())),
                            preferred_element_type=F32) * (HEAD_DIM ** -0.5)
        s = jnp.where(valid, s, -jnp.inf)
        sink = jnp.concatenate(
            [jnp.full((WINDOW, 1), sinks_ref[h * GQA_GROUP + j], F32) for j in range(GQA_GROUP)], axis=0)
        m = jnp.maximum(jnp.max(s, axis=-1, keepdims=True), sink)
        e = jnp.exp(s - m)
        inv = 1.0 / (jnp.sum(e, axis=-1, keepdims=True) + jnp.exp(sink - m))
        o = _bdot((e * inv).astype(BF16), vv[:, ks].astype(BF16))
        outs.extend(o[j * WINDOW:(j + 1) * WINDOW, :] for j in range(GQA_GROUP))
    return jnp.concatenate(outs, axis=1)


def _prompt_front_kernel(x_ref, cos_ref, sin_ref, gmix_ref, win_ref, lng_ref, lnb_ref, ws_ref, bsf_ref,
                         sinks_ref, wpa_ref, wpb_ref, wo_ref, gffn_ref, wrt_ref, br_ref,
                         x1_ref, h2_ref, topi_ref, topw_ref, kwin_ref, vwin_ref, kprev_ref, vprev_ref):
    i = pl.program_id(0)
    seq_start = (i % TILES_PER_SEQ) == 0

    @pl.when(seq_start)
    def _():
        kprev_ref[...] = jnp.zeros_like(kprev_ref)
        vprev_ref[...] = jnp.zeros_like(vprev_ref)

    x = x_ref[...]
    u, va, q, k, v, gate_a, gate_b = _in_projection(
        x, gmix_ref, win_ref, lng_ref, lnb_ref, cos_ref[...], sin_ref[...])

    ya_rows, att_rows = [], []
    k_before, v_before = kprev_ref[...], vprev_ref[...]
    for b in range(BLOCKS_PER_TILE):
        r = slice(b * WINDOW, (b + 1) * WINDOW)
        zc = jnp.concatenate(
            [_bdot(ws_ref[g], va[r, g * A_GROUP_DIM:(g + 1) * A_GROUP_DIM].astype(BF16)) for g in range(A_GROUPS)],
            axis=1) + bsf_ref[...]
        ya_rows.append(u[r] * zc)
        kk = jnp.concatenate([k_before, k[r]], axis=0)
        vv = jnp.concatenate([v_before, v[r]], axis=0)
        first = jnp.logical_and(seq_start, b == 0)
        att_rows.append(_band_attention(q[r], kk, vv, sinks_ref, first))
        k_before, v_before = k[r], v[r]
    kprev_ref[...] = k_before
    vprev_ref[...] = v_before
    kwin_ref[0] = k_before
    vwin_ref[0] = v_before

    x1, h2, topi, topw = _merge_and_route(
        x, jnp.concatenate(ya_rows, axis=0), jnp.concatenate(att_rows, axis=0), gate_a, gate_b,
        wpa_ref, wpb_ref, wo_ref, gffn_ref, wrt_ref, br_ref)
    _store_token_major(x1_ref, x1)
    _store_token_major(h2_ref, h2)
    topi_ref[...] = topi
    topw_ref[...] = topw


def _full(shape):
    return pl.BlockSpec(shape, lambda i: (0,) * len(shape))


def _prompt_front(x, cos, sin, gmix, win, lng, lnb, ws, bsf, sinks, wpa, wpb, wo, gffn, wrt, br):
    n = x.shape[0]
    grid = (n // TM,)
    in_specs = [
        pl.BlockSpec((TM, D_MODEL), lambda i: (i, 0)),
        pl.BlockSpec((TM, LANES), lambda i: (i % TILES_PER_SEQ, 0)),
        pl.BlockSpec((TM, LANES), lambda i: (i % TILES_PER_SEQ, 0)),
        _full((1, D_MODEL)),
        _full((D_MODEL, IN_COLS)),
        _full((1, A_WIDTH)),
        _full((1, A_WIDTH)),
        _full((A_GROUPS, CHUNK, CHUNK)),
        _full((CHUNK, A_WIDTH)),
        pl.BlockSpec(memory_space=pltpu.SMEM),
        _full((A_WIDTH, D_MODEL)),
        _full((Q_WIDTH, D_MODEL)),
        _full((D_MODEL, D_MODEL)),
        _full((1, D_MODEL)),
        _full((N_EXPERTS, D_MODEL)),
        _full((N_EXPERTS, 1)),
    ]
    out_shape = [
        jax.ShapeDtypeStruct((n * LANE_CHUNKS, LANES), F32),
        jax.ShapeDtypeStruct((n * LANE_CHUNKS, LANES), F32),
        jax.ShapeDtypeStruct((TOP_K, n), I32),
        jax.ShapeDtypeStruct((TOP_K, n), F32),
        jax.ShapeDtypeStruct((n // SEQ, WINDOW, KV_WIDTH), F32),
        jax.ShapeDtypeStruct((n // SEQ, WINDOW, KV_WIDTH), F32),
    ]
    out_specs = [
        pl.BlockSpec((TM * LANE_CHUNKS, LANES), lambda i: (i, 0)),
        pl.BlockSpec((TM * LANE_CHUNKS, LANES), lambda i: (i, 0)),
        pl.BlockSpec((TOP_K, TM), lambda i: (0, i)),
        pl.BlockSpec((TOP_K, TM), lambda i: (0, i)),
        pl.BlockSpec((1, WINDOW, KV_WIDTH), lambda i: (i // TILES_PER_SEQ, 0, 0)),
        pl.BlockSpec((1, WINDOW, KV_WIDTH), lambda i: (i // TILES_PER_SEQ, 0, 0)),
    ]
    return pl.pallas_call(
        _prompt_front_kernel,
        grid=grid,
        in_specs=in_specs,
        out_specs=out_specs,
        out_shape=out_shape,
        scratch_shapes=[pltpu.VMEM((WINDOW, KV_WIDTH), F32), pltpu.VMEM((WINDOW, KV_WIDTH), F32)],
        compiler_params=pltpu.CompilerParams(dimension_semantics=("arbitrary",), vmem_limit_bytes=VMEM_LIMIT),
        name="prompt_front",
    )(x, cos, sin, gmix, win, lng, lnb, ws, bsf, sinks, wpa, wpb, wo, gffn, wrt, br)


SAMPLE_STEP = 16
SAMPLE_STEPS = DEC_BATCH // SAMPLE_STEP


def _sample_kernel(x_ref, cos_ref, sin_ref, gmix_ref, win_ref, lng_ref, lnb_ref, wdiag_ref, bs0_ref, sinks_ref,
                   kc_ref, vc_ref, wpa_ref, wpb_ref, wo_ref, gffn_ref, wrt_ref, br_ref,
                   x1_ref, h2_ref, topi_ref, topw_ref, kwin_ref, vwin_ref, va_ref,
                   q_s, k_s, v_s, yain_s, ga_s, gb_s, att_s):
    i = pl.program_id(0)

    @pl.when(i == 0)
    def _():
        x = x_ref[...]
        cos = jnp.broadcast_to(cos_ref[...], (DEC_BATCH, LANES))
        sin = jnp.broadcast_to(sin_ref[...], (DEC_BATCH, LANES))
        u, va, q, k, v, gate_a, gate_b = _in_projection(x, gmix_ref, win_ref, lng_ref, lnb_ref, cos, sin)
        va_ref[...] = va
        z = wdiag_ref[...].astype(F32) * va.astype(BF16).astype(F32) + bs0_ref[...]
        yain_s[...] = u * z
        q_s[...] = q
        k_s[...] = k
        v_s[...] = v
        ga_s[...] = gate_a
        gb_s[...] = gate_b

    r0 = pl.multiple_of(i * SAMPLE_STEP, SAMPLE_STEP)
    kwin = jnp.concatenate([kc_ref[:, 1:, :], k_s[pl.ds(r0, SAMPLE_STEP), :][:, None, :]], axis=1)
    vwin = jnp.concatenate([vc_ref[:, 1:, :], v_s[pl.ds(r0, SAMPLE_STEP), :][:, None, :]], axis=1)
    kwin_ref[...] = kwin
    vwin_ref[...] = vwin

    q16 = q_s[pl.ds(r0, SAMPLE_STEP), :]
    lane = lax.broadcasted_iota(I32, (SAMPLE_STEP, LANES), 1)
    heads = []
    for hq in range(N_HEADS):
        c, p, h = hq // 2, hq % 2, hq // GQA_GROUP
        chunk = q16[:, c * LANES:(c + 1) * LANES]
        if p != h:
            chunk = pltpu.roll(chunk, HEAD_DIM, 1)
        keep = (lane < HEAD_DIM) if h == 0 else (lane >= HEAD_DIM)
        heads.append(jnp.where(keep, chunk, 0.0))
    qpad = pltpu.einshape("hbd->bhd", jnp.stack(heads, axis=0)).astype(BF16)
    s = jnp.einsum("bhd,bkd->bhk", qpad, kwin.astype(BF16), preferred_element_type=F32) * (HEAD_DIM ** -0.5)
    hid = lax.broadcasted_iota(I32, (1, N_HEADS, 1), 1)
    sink = jnp.zeros((1, N_HEADS, 1), F32)
    for hq in range(N_HEADS):
        sink = jnp.where(hid == hq, sinks_ref[hq], sink)
    m = jnp.maximum(jnp.max(s, axis=-1, keepdims=True), sink)
    e = jnp.exp(s - m)
    inv = 1.0 / (jnp.sum(e, axis=-1, keepdims=True) + jnp.exp(sink - m))
    o = jnp.einsum("bhk,bkd->bhd", (e * inv).astype(BF16), vwin.astype(BF16), preferred_element_type=F32)
    o = pltpu.einshape("bhd->hbd", o)
    chunks = []
    for c in range(N_HEADS // 2):
        parts = []
        for p in range(2):
            hq = 2 * c + p
            oh = o[hq]
            if p != hq // GQA_GROUP:
                oh = pltpu.roll(oh, HEAD_DIM, 1)
            parts.append(oh)
        chunks.append(jnp.where(lane < HEAD_DIM, parts[0], parts[1]))
    att_s[pl.ds(r0, SAMPLE_STEP), :] = jnp.concatenate(chunks, axis=1)

    @pl.when(i == SAMPLE_STEPS - 1)
    def _():
        x1, h2, topi, topw = _merge_and_route(
            x_ref[...], yain_s[...], att_s[...], ga_s[...], gb_s[...],
            wpa_ref, wpb_ref, wo_ref, gffn_ref, wrt_ref, br_ref)
        _store_token_major(x1_ref, x1)
        _store_token_major(h2_ref, h2)
        topi_ref[...] = topi
        topw_ref[...] = topw


def _sample_front(x, cos, sin, gmix, win, lng, lnb, wdiag, bs0, sinks, kc, vc, wpa, wpb, wo, gffn, wrt, br):
    n = DEC_BATCH
    cache_spec = pl.BlockSpec((SAMPLE_STEP, WINDOW, KV_WIDTH), lambda i: (i, 0, 0))
    in_specs = [
        _full((n, D_MODEL)),
        _full((1, LANES)),
        _full((1, LANES)),
        _full((1, D_MODEL)),
        _full((D_MODEL, IN_COLS)),
        _full((1, A_WIDTH)),
        _full((1, A_WIDTH)),
        _full((1, A_WIDTH)),
        _full((1, A_WIDTH)),
        pl.BlockSpec(memory_space=pltpu.SMEM),
        cache_spec,
        cache_spec,
        _full((A_WIDTH, D_MODEL)),
        _full((Q_WIDTH, D_MODEL)),
        _full((D_MODEL, D_MODEL)),
        _full((1, D_MODEL)),
        _full((N_EXPERTS, D_MODEL)),
        _full((N_EXPERTS, 1)),
    ]
    out_shape = [
        jax.ShapeDtypeStruct((n * LANE_CHUNKS, LANES), F32),
        jax.ShapeDtypeStruct((n * LANE_CHUNKS, LANES), F32),
        jax.ShapeDtypeStruct((TOP_K, n), I32),
        jax.ShapeDtypeStruct((TOP_K, n), F32),
        jax.ShapeDtypeStruct((n, WINDOW, KV_WIDTH), F32),
        jax.ShapeDtypeStruct((n, WINDOW, KV_WIDTH), F32),
        jax.ShapeDtypeStruct((n, A_WIDTH), F32),
    ]
    out_specs = [
        _full((n * LANE_CHUNKS, LANES)),
        _full((n * LANE_CHUNKS, LANES)),
        _full((TOP_K, n)),
        _full((TOP_K, n)),
        cache_spec,
        cache_spec,
        _full((n, A_WIDTH)),
    ]
    scratch = [
        pltpu.VMEM((n, Q_WIDTH), F32), pltpu.VMEM((n, KV_WIDTH), F32), pltpu.VMEM((n, KV_WIDTH), F32),
        pltpu.VMEM((n, A_WIDTH), F32), pltpu.VMEM((n, D_MODEL), F32), pltpu.VMEM((n, D_MODEL), F32),
        pltpu.VMEM((n, Q_WIDTH), F32),
    ]
    return pl.pallas_call(
        _sample_kernel,
        grid=(SAMPLE_STEPS,),
        in_specs=in_specs,
        out_specs=out_specs,
        out_shape=out_shape,
        scratch_shapes=scratch,
        compiler_params=pltpu.CompilerParams(dimension_semantics=("arbitrary",), vmem_limit_bytes=VMEM_LIMIT),
        name="sample_front",
    )(x, cos, sin, gmix, win, lng, lnb, wdiag, bs0, sinks, kc, vc, wpa, wpb, wo, gffn, wrt, br)


def _route_plan_kernel(ip_ref, is_ref, wp_ref, wsm_ref, dest_ref, wts_ref, off_ref):
    g = pl.program_id(0)
    e_s = jnp.where(g == N_GROUPS - 1, is_ref[...], N_EXPERTS)
    eall = jnp.concatenate([ip_ref[...], e_s], axis=1)
    wts_ref[:, 0:GROUP_SLOTS] = jnp.concatenate([wp_ref[...], wsm_ref[...]], axis=1)
    wts_ref[:, GROUP_SLOTS:] = jnp.zeros((TOP_K, K_STRIDE - GROUP_SLOTS), F32)
    dest_ref[:, GROUP_SLOTS:] = jnp.zeros((TOP_K, K_STRIDE - GROUP_SLOTS), I32)
    eid = lax.broadcasted_iota(I32, (N_EXPERTS, GROUP_SLOTS), 0)
    onehots = [eall[k:k + 1, :] == eid for k in range(TOP_K)]
    count = jnp.zeros((N_EXPERTS, GROUP_SLOTS), F32)
    for oh in onehots:
        count = count + oh.astype(F32)
    total = jnp.broadcast_to(jnp.sum(count, axis=1, keepdims=True), (N_EXPERTS, LANES))
    nblk = jnp.floor((total + (MOE_ROWS - 1)) * (1.0 / MOE_ROWS))
    r = lax.broadcasted_iota(I32, (N_EXPERTS, N_EXPERTS), 0)
    c = lax.broadcasted_iota(I32, (N_EXPERTS, N_EXPERTS), 1)
    first_blk = lax.dot_general((c < r).astype(F32), nblk, (((1,), (0,)), ((), ())),
                                precision=lax.Precision.HIGHEST, preferred_element_type=F32)
    start = (first_blk + 1.0) * MOE_ROWS
    lane = lax.broadcasted_iota(I32, (N_EXPERTS, LANES), 1)
    info = jnp.where(lane == 0, start, jnp.where(lane == 1, start + total, jnp.where(lane == 2, nblk, first_blk)))
    off_ref[...] = info.astype(I32)
    ti = lax.broadcasted_iota(I32, (LANES, LANES), 0)
    tj = lax.broadcasted_iota(I32, (LANES, LANES), 1)
    before = (ti < tj).astype(BF16)
    ones = jnp.ones((LANES, LANES), BF16)
    running = start
    for t in range(SLOT_TILES):
        sl = slice(t * LANES, (t + 1) * LANES)
        cb = count[:, sl].astype(BF16)
        pos = running + _bdot(cb, before)
        rows = [jnp.sum(jnp.where(oh[:, sl], pos, 0.0), axis=0, keepdims=True) for oh in onehots]
        dest_ref[:, sl] = jnp.concatenate(rows, axis=0).astype(I32)
        running = running + _bdot(cb, ones)


def _route_plan(topi_p, topi_s, topw_p, topw_s):
    in_specs = [
        pl.BlockSpec((TOP_K, GROUP_PROMPT), lambda g: (0, g)),
        pl.BlockSpec((TOP_K, DEC_BATCH), lambda g: (0, 0)),
        pl.BlockSpec((TOP_K, GROUP_PROMPT), lambda g: (0, g)),
        pl.BlockSpec((TOP_K, DEC_BATCH), lambda g: (0, 0)),
    ]
    out_shape = [
        jax.ShapeDtypeStruct((N_GROUPS, TOP_K, K_STRIDE), I32),
        jax.ShapeDtypeStruct((N_GROUPS, TOP_K, K_STRIDE), F32),
        jax.ShapeDtypeStruct((N_GROUPS, N_EXPERTS, LANES), I32),
    ]
    out_specs = [
        pl.BlockSpec((None, TOP_K, K_STRIDE), lambda g: (g, 0, 0)),
        pl.BlockSpec((None, TOP_K, K_STRIDE), lambda g: (g, 0, 0)),
        pl.BlockSpec((None, N_EXPERTS, LANES), lambda g: (g, 0, 0)),
    ]
    return pl.pallas_call(
        _route_plan_kernel,
        grid=(N_GROUPS,),
        in_specs=in_specs,
        out_specs=out_specs,
        out_shape=out_shape,
        compiler_params=pltpu.CompilerParams(dimension_semantics=("arbitrary",)),
        name="route_plan",
    )(topi_p, topi_s, topw_p, topw_s)


GROUP_ROWS = GROUP_PROMPT * LANE_CHUNKS
SAMPLE_ROWS = DEC_BATCH * LANE_CHUNKS
TRASH_SLOT = GROUP_SLOTS
BUF_ROWS = (GROUP_SLOTS + 1) * LANE_CHUNKS
SCATTER_BATCH = 8


def _moe_kernel(dest_ref, wts_ref, off_ref, h2p_ref, h2s_ref, x1p_ref, x1s_ref, wgu_ref, bgu_ref, wdn_ref, bdn_ref,
                x2p_ref, x2s_ref,
                h2buf, acc, wgubuf, bgubuf, wdnbuf, bdnbuf, xs0, xs1, ys0, ys1,
                src_ref, seg_expert, seg_first, blk_seg, act_sem, w_sem):
    g = pl.program_id(0)
    last = g == N_GROUPS - 1
    row0 = pl.multiple_of(g * GROUP_ROWS, GROUP_ROWS)

    def prompt_copies():
        return (pltpu.make_async_copy(h2p_ref.at[pl.ds(row0, GROUP_ROWS)], h2buf.at[pl.ds(0, GROUP_ROWS)], act_sem.at[0]),
                pltpu.make_async_copy(x1p_ref.at[pl.ds(row0, GROUP_ROWS)], acc.at[pl.ds(0, GROUP_ROWS)], act_sem.at[1]))

    def sample_copies():
        return (pltpu.make_async_copy(h2s_ref, h2buf.at[pl.ds(GROUP_ROWS, SAMPLE_ROWS)], act_sem.at[2]),
                pltpu.make_async_copy(x1s_ref, acc.at[pl.ds(GROUP_ROWS, SAMPLE_ROWS)], act_sem.at[3]))

    def weight_copies(e, slot):
        return (pltpu.make_async_copy(wgu_ref.at[e], wgubuf.at[slot], w_sem.at[0, slot]),
                pltpu.make_async_copy(bgu_ref.at[e], bgubuf.at[slot], w_sem.at[1, slot]),
                pltpu.make_async_copy(wdn_ref.at[e], wdnbuf.at[slot], w_sem.at[2, slot]),
                pltpu.make_async_copy(bdn_ref.at[e], bdnbuf.at[slot], w_sem.at[3, slot]))

    for cp in prompt_copies():
        cp.start()

    @pl.when(last)
    def _():
        for cp in sample_copies():
            cp.start()

    trash = pl.ds(TRASH_SLOT * LANE_CHUNKS, LANE_CHUNKS)
    h2buf[trash, :] = jnp.zeros((LANE_CHUNKS, LANES), F32)
    acc[trash, :] = jnp.zeros((LANE_CHUNKS, LANES), F32)
    ys1[...] = jnp.zeros_like(ys1)

    def pad_codes(lo, hi):
        def body(p, carry):
            src_ref[p] = TRASH_SLOT
            return carry
        lax.fori_loop(lo, hi, body, 0)

    def scan_expert(e, carry):
        nseg, nblocks = carry
        nblk = off_ref[e, 2]
        first = off_ref[e, 3]

        @pl.when(nblk > 0)
        def _():
            seg_expert[nseg] = e
            seg_first[nseg] = first
            pad_codes(off_ref[e, 1], off_ref[e, 0] + nblk * MOE_ROWS)

            def mark(b, c2):
                blk_seg[first + b] = nseg
                return c2
            lax.fori_loop(0, nblk, mark, 0)

        return nseg + jnp.where(nblk > 0, 1, 0), nblocks + nblk

    nseg, nblocks = lax.fori_loop(0, N_EXPERTS, scan_expert, (jnp.int32(0), jnp.int32(0)))
    pad_codes(0, MOE_ROWS)
    pad_codes((nblocks + 1) * MOE_ROWS, (nblocks + 3) * MOE_ROWS)
    blk_seg[nblocks] = nseg - 1
    blk_seg[nblocks + 1] = nseg - 1

    for cp in weight_copies(seg_expert[0], 0):
        cp.start()

    nvalid = jnp.where(last, GROUP_SLOTS, GROUP_PROMPT)
    for k in range(TOP_K):
        def fill(j, carry, k=k):
            c0 = k * K_STRIDE + j * SUBLANES
            for d in range(SUBLANES):
                src_ref[dest_ref[c0 + d]] = c0 + d
            return carry
        lax.fori_loop(0, nvalid // SUBLANES, fill, 0)

    for cp in prompt_copies():
        cp.wait()

    @pl.when(last)
    def _():
        for cp in sample_copies():
            cp.wait()

    def token_rows(code):
        slot_id = code & (K_STRIDE - 1)
        return pl.ds(pl.multiple_of(slot_id * LANE_CHUNKS, LANE_CHUNKS), LANE_CHUNKS)

    def gather(b, xs):
        base = (b + 1) * MOE_ROWS
        for m in range(MOE_ROWS):
            xs[pl.ds(m, LANE_CHUNKS, stride=XS_STRIDE), :] = h2buf[token_rows(src_ref[base + m]), :]

    def scatter_add(b, ys):
        base = (b + 1) * MOE_ROWS
        for m0 in range(0, MOE_ROWS, SCATTER_BATCH):
            pending = []
            for m in range(m0, m0 + SCATTER_BATCH):
                code = src_ref[base + m]
                rows = token_rows(code)
                pending.append((rows, acc[rows, :] + wts_ref[code] * ys[pl.ds(m, LANE_CHUNKS, stride=XS_STRIDE), :]))
            for rows, val in pending:
                acc[rows, :] = val

    def expert_ffn(xs, ys, slot):
        x = jnp.concatenate(
            [xs[c * XS_STRIDE:c * XS_STRIDE + MOE_ROWS, :] for c in range(LANE_CHUNKS)], axis=1).astype(BF16)
        gu = _bdot(x, wgubuf[slot]) + bgubuf[slot]
        gl = jnp.minimum(gu[:, :D_EXPERT], SWIGLU_LIMIT)
        ul = jnp.clip(gu[:, D_EXPERT:], -SWIGLU_LIMIT, SWIGLU_LIMIT)
        a = (ul + 1.0) * (gl * jax.nn.sigmoid(SWIGLU_ALPHA * gl))
        y = _bdot(a.astype(BF16), wdnbuf[slot]) + bdnbuf[slot]
        for c in range(LANE_CHUNKS):
            ys[c * XS_STRIDE:c * XS_STRIDE + MOE_ROWS, :] = y[:, c * LANES:(c + 1) * LANES]

    def step(b, xs_cur, xs_next, ys_cur, ys_prev):
        seg = blk_seg[b]
        slot = seg & 1

        @pl.when(jnp.logical_and(b == seg_first[seg], b < nblocks))
        def _():
            for cp in weight_copies(seg_expert[seg], slot):
                cp.wait()

            @pl.when(seg + 1 < nseg)
            def _():
                for cp in weight_copies(seg_expert[seg + 1], 1 - slot):
                    cp.start()

        gather(b + 1, xs_next)
        expert_ffn(xs_cur, ys_cur, slot)
        scatter_add(b - 1, ys_prev)

    gather(0, xs0)
    npairs = (nblocks + 1) // 2

    def pair(t, carry):
        step(2 * t, xs0, xs1, ys0, ys1)
        step(2 * t + 1, xs1, xs0, ys1, ys0)
        return carry

    lax.fori_loop(0, npairs, pair, 0)
    scatter_add(2 * npairs - 1, ys1)

    out_p =pltpu.make_async_copy(acc.at[pl.ds(0, GROUP_ROWS)], x2p_ref.at[pl.ds(row0, GROUP_ROWS)], act_sem.at[0])
    out_p.start()

    @pl.when(last)
    def _():
        out_s = pltpu.make_async_copy(acc.at[pl.ds(GROUP_ROWS, SAMPLE_ROWS)], x2s_ref, act_sem.at[2])
        out_s.start()
        out_s.wait()

    out_p.wait()


def _moe(dest, wts, off, h2p, h2s, x1p, x1s, wgu, bgu, wdn, bdn):
    smem = lambda: pl.BlockSpec((TOP_K * K_STRIDE,), lambda g: (g,), memory_space=pltpu.SMEM)
    anyspec = pl.BlockSpec(memory_space=pl.ANY)
    dest = dest.reshape(N_GROUPS * TOP_K * K_STRIDE)
    wts = wts.reshape(N_GROUPS * TOP_K * K_STRIDE)
    in_specs = [
        smem(), smem(),
        pl.BlockSpec((None, N_EXPERTS, LANES), lambda g: (g, 0, 0), memory_space=pltpu.SMEM),
        anyspec, anyspec, anyspec, anyspec, anyspec, anyspec, anyspec, anyspec,
    ]
    scratch = [
        pltpu.VMEM((BUF_ROWS, LANES), F32),
        pltpu.VMEM((BUF_ROWS, LANES), F32),
        pltpu.VMEM((2, D_MODEL, 2 * D_EXPERT), BF16),
        pltpu.VMEM((2, 1, 2 * D_EXPERT), F32),
        pltpu.VMEM((2, D_EXPERT, D_MODEL), BF16),
        pltpu.VMEM((2, 1, D_MODEL), F32),
        pltpu.VMEM((LANE_CHUNKS * XS_STRIDE, LANES), F32),
        pltpu.VMEM((LANE_CHUNKS * XS_STRIDE, LANES), F32),
        pltpu.VMEM((LANE_CHUNKS * XS_STRIDE, LANES), F32),
        pltpu.VMEM((LANE_CHUNKS * XS_STRIDE, LANES), F32),
        pltpu.SMEM((POS_TABLE,), I32),
        pltpu.SMEM((N_EXPERTS,), I32),
        pltpu.SMEM((N_EXPERTS,), I32),
        pltpu.SMEM((LANES,), I32),
        pltpu.SemaphoreType.DMA((4,)),
        pltpu.SemaphoreType.DMA((4, 2)),
    ]
    return pl.pallas_call(
        _moe_kernel,
        grid=(N_GROUPS,),
        in_specs=in_specs,
        out_specs=[anyspec, anyspec],
        out_shape=[jax.ShapeDtypeStruct(x1p.shape, F32), jax.ShapeDtypeStruct(x1s.shape, F32)],
        scratch_shapes=scratch,
        compiler_params=pltpu.CompilerParams(dimension_semantics=("arbitrary",), vmem_limit_bytes=VMEM_LIMIT),
        name="moe",
    )(dest, wts, off, h2p, h2s, x1p, x1s, wgu, bgu, wdn, bdn)


def _ple_final_kernel(x2_ref, ple_ref, wple_ref, gple_ref, wpg_ref, gfin_ref, y_ref):
    rows = y_ref.shape[0]
    x2 = _load_token_major(x2_ref, rows)
    e = _rmsnorm(_bdot(ple_ref[...].astype(BF16), wple_ref[...]), gple_ref[...])
    x3 = x2 + jax.nn.sigmoid(_bdot(x2.astype(BF16), wpg_ref[...])) * e
    y_ref[...] = _rmsnorm(x3, gfin_ref[...])


def _ple_final(x2_tm, ple, wple, gple, wpg, gfin, tile):
    n = ple.shape[0]
    return pl.pallas_call(
        _ple_final_kernel,
        grid=(n // tile,),
        in_specs=[
            pl.BlockSpec((tile * LANE_CHUNKS, LANES), lambda i: (i, 0)),
            pl.BlockSpec((tile, PLE_DIM), lambda i: (i, 0)),
            _full((PLE_DIM, D_MODEL)),
            _full((1, D_MODEL)),
            _full((D_MODEL, D_MODEL)),
            _full((1, D_MODEL)),
        ],
        out_specs=pl.BlockSpec((tile, D_MODEL), lambda i: (i, 0)),
        out_shape=jax.ShapeDtypeStruct((n, D_MODEL), F32),
        compiler_params=pltpu.CompilerParams(dimension_semantics=("arbitrary",), vmem_limit_bytes=VMEM_LIMIT),
        name="ple_final",
    )(x2_tm, ple, wple, gple, wpg, gfin)


def _rope_tables(pos):
    half = HEAD_DIM // 2
    inv = ROPE_THETA ** (-jnp.arange(half, dtype=F32) / half)
    ang = pos.astype(F32)[:, None] * inv[None, :]
    cos, sin = jnp.cos(ang), jnp.sin(ang)
    cos2 = jnp.concatenate([cos, cos, cos, cos], axis=1)
    sin2 = jnp.concatenate([-sin, sin, -sin, sin], axis=1)
    return cos2, sin2


def _prep_weights(g_mix, w_in, a_ln_g, a_ln_b, a_ws, a_bs, w_pa, w_pb, w_o, g_ffn, w_router, b_router):
    causal = jnp.tril(jnp.ones((CHUNK, CHUNK), dtype=bool))
    return dict(
        gmix=g_mix.reshape(1, D_MODEL),
        win=w_in.astype(BF16),
        lng=a_ln_g.reshape(1, A_WIDTH),
        lnb=a_ln_b.reshape(1, A_WIDTH),
        ws=jnp.where(causal[None], a_ws, 0.0).astype(BF16),
        bsf=jnp.repeat(jnp.transpose(a_bs), A_GROUP_DIM, axis=1),
        wpa=w_pa.astype(BF16),
        wpb=w_pb.astype(BF16),
        wo=w_o.astype(BF16),
        gffn=g_ffn.reshape(1, D_MODEL),
        wrt=jnp.transpose(w_router),
        br=b_router.reshape(N_EXPERTS, 1),
    )


def kernel(x_prompt, x_sample, cache_win_k, cache_win_v, p_prompt, p_sample, g_mix, w_in, a_ln_g, a_ln_b, a_ws, a_bs, sinks, w_pa, w_pb, w_o, g_ffn, w_router, b_router, w_gu, b_gu, w_down, b_down, w_ple, g_ple, w_ple_gate, g_final):
    W = _prep_weights(g_mix[0], w_in[0], a_ln_g[0], a_ln_b[0], a_ws[0], a_bs[0], w_pa[0], w_pb[0], w_o[0],
                      g_ffn[0], w_router[0], b_router[0])
    cos_p, sin_p = _rope_tables(jnp.arange(SEQ, dtype=I32))
    cos_s, sin_s = _rope_tables(jnp.full((1,), PAST_LEN, I32))
    x1p, h2p, topi_p, topw_p, kwin_p, vwin_p = _prompt_front(
        x_prompt.reshape(N_PROMPT, D_MODEL), cos_p, sin_p, W["gmix"], W["win"], W["lng"], W["lnb"],
        W["ws"], W["bsf"], sinks[0], W["wpa"], W["wpb"], W["wo"], W["gffn"], W["wrt"], W["br"])

    wdiag = jnp.repeat(a_ws[0, :, 0, 0], A_GROUP_DIM)[None, :].astype(BF16)
    bs0 = jnp.repeat(a_bs[0, :, 0], A_GROUP_DIM)[None, :]
    x1s, h2s, topi_s, topw_s, kwin_s, vwin_s, va_s = _sample_front(
        x_sample.reshape(DEC_BATCH, D_MODEL), cos_s, sin_s, W["gmix"], W["win"], W["lng"], W["lnb"], wdiag, bs0,
        sinks[0], cache_win_k[0].reshape(DEC_BATCH, WINDOW, KV_WIDTH), cache_win_v[0].reshape(DEC_BATCH, WINDOW, KV_WIDTH),
        W["wpa"], W["wpb"], W["wo"], W["gffn"], W["wrt"], W["br"])

    dest, wts, off = _route_plan(topi_p, topi_s, topw_p, topw_s)
    x2p, x2s = _moe(dest, wts, off, h2p, h2s, x1p, x1s,
                    w_gu[0].astype(BF16), b_gu[0].reshape(N_EXPERTS, 1, 2 * D_EXPERT),
                    w_down[0].astype(BF16), b_down[0].reshape(N_EXPERTS, 1, D_MODEL))

    wple = w_ple[0].astype(BF16)
    gple = g_ple[0].reshape(1, D_MODEL)
    wpg = w_ple_gate[0].astype(BF16)
    gfin = g_final.reshape(1, D_MODEL)
    y_p = _ple_final(x2p, p_prompt[0].reshape(N_PROMPT, PLE_DIM), wple, gple, wpg, gfin, TM)
    y_s = _ple_final(x2s, p_sample[0].reshape(DEC_BATCH, PLE_DIM), wple, gple, wpg, gfin, DEC_BATCH)

    return (
        y_p.reshape(BATCH, SEQ, D_MODEL),
        y_s.reshape(DEC_BATCH, 1, D_MODEL),
        kwin_p.reshape(1, BATCH, WINDOW, N_KV_HEADS, HEAD_DIM),
        vwin_p.reshape(1, BATCH, WINDOW, N_KV_HEADS, HEAD_DIM),
        kwin_s.reshape(1, DEC_BATCH, WINDOW, N_KV_HEADS, HEAD_DIM),
        vwin_s.reshape(1, DEC_BATCH, WINDOW, N_KV_HEADS, HEAD_DIM),
        va_s.reshape(1, DEC_BATCH, 1, A_WIDTH),
    )
```

```python
import functools

import jax
import jax.numpy as jnp
from jax import lax
from jax.experimental import pallas as pl
from jax.experimental.pallas import tpu as pltpu

F32 = jnp.float32
BF16 = jnp.bfloat16
I32 = jnp.int32

D_MODEL = 1024
BATCH = 4
SEQ = 4096
DEC_BATCH = 128
PAST_LEN = 8192
CHUNK = 128
A_GROUPS = 4
A_GROUP_DIM = 128
A_WIDTH = A_GROUPS * A_GROUP_DIM
N_HEADS = 8
N_KV_HEADS = 2
HEAD_DIM = 64
Q_WIDTH = N_HEADS * HEAD_DIM
KV_WIDTH = N_KV_HEADS * HEAD_DIM
GQA_GROUP = N_HEADS // N_KV_HEADS
WINDOW = 128
ROPE_THETA = 10000.0
N_EXPERTS = 32
TOP_K = 4
D_EXPERT = D_MODEL
SWIGLU_ALPHA = 1.702
SWIGLU_LIMIT = 7.0
PLE_DIM = 256
RMS_EPS = 1e-5
LN_EPS = 1e-5

O_Q = 2 * A_WIDTH
O_K = O_Q + Q_WIDTH
O_V = O_K + KV_WIDTH
O_GA = O_V + KV_WIDTH
O_GB = O_GA + D_MODEL
IN_COLS = O_GB + D_MODEL

LANES = 128
SUBLANES = 8
LANE_CHUNKS = D_MODEL // LANES
VMEM_LIMIT = 56 * 1024 * 1024

N_PROMPT = BATCH * SEQ
TM = 256
TILES_PER_SEQ = SEQ // TM
BLOCKS_PER_TILE = TM // WINDOW

N_GROUPS = 4
GROUP_PROMPT = N_PROMPT // N_GROUPS
GROUP_SLOTS = GROUP_PROMPT + DEC_BATCH
GROUP_ASSIGN = GROUP_SLOTS * TOP_K
SLOT_TILES = GROUP_SLOTS // LANES
MOE_ROWS = 256
XS_STRIDE = MOE_ROWS + SUBLANES
SLOT_BITS = 13
K_STRIDE = 1 << SLOT_BITS
assert GROUP_SLOTS < K_STRIDE
MAX_BLOCKS = GROUP_ASSIGN // MOE_ROWS + N_EXPERTS
POS_TABLE = 1 << 15
assert (MAX_BLOCKS + 3) * MOE_ROWS <= POS_TABLE
assert MAX_BLOCKS + 2 <= LANES


def _bdot(a, b):
    return jnp.dot(a, b, preferred_element_type=F32)


def _rmsnorm(x, g):
    return x * lax.rsqrt(jnp.mean(x * x, axis=-1, keepdims=True) + RMS_EPS) * g


def _gelu(x):
    return 0.5 * x * (1.0 + lax.erf(x * (0.5 ** 0.5)))


def _group_layernorm(v, g, b):
    cols = []
    for gi in range(A_GROUPS):
        s = slice(gi * A_GROUP_DIM, (gi + 1) * A_GROUP_DIM)
        vg = v[:, s]
        mu = jnp.mean(vg, axis=-1, keepdims=True)
        d = vg - mu
        var = jnp.mean(d * d, axis=-1, keepdims=True)
        cols.append(d * lax.rsqrt(var + LN_EPS) * g[:, s] + b[:, s])
    return jnp.concatenate(cols, axis=1)


def _rope(x, cos, sin_signed):
    width = x.shape[1]
    reps = width // LANES
    cosf = jnp.concatenate([cos] * reps, axis=1) if reps > 1 else cos
    sinf = jnp.concatenate([sin_signed] * reps, axis=1) if reps > 1 else sin_signed
    half = HEAD_DIM // 2
    lane = lax.broadcasted_iota(I32, x.shape, 1)
    up = pltpu.roll(x, width - half, 1)
    down = pltpu.roll(x, half, 1)
    partner = jnp.where((lane & (HEAD_DIM - 1)) < half, up, down)
    return x * cosf + partner * sinf


def _in_projection(x, gmix_ref, win_ref, lng_ref, lnb_ref, cos, sin_signed):
    hb = _rmsnorm(x, gmix_ref[...]).astype(BF16)
    zuv = _gelu(_bdot(hb, win_ref[:, 0:O_Q]))
    u = zuv[:, :A_WIDTH]
    va = _group_layernorm(zuv[:, A_WIDTH:], lng_ref[...], lnb_ref[...])
    zqkv = _bdot(hb, win_ref[:, O_Q:O_GA])
    q = _rope(zqkv[:, :Q_WIDTH], cos, sin_signed)
    k = _rope(zqkv[:, Q_WIDTH:Q_WIDTH + KV_WIDTH], cos, sin_signed)
    v = zqkv[:, Q_WIDTH + KV_WIDTH:]
    zg = _bdot(hb, win_ref[:, O_GA:IN_COLS])
    gate_a = jax.nn.sigmoid(zg[:, :D_MODEL])
    gate_b = jax.nn.sigmoid(zg[:, D_MODEL:])
    return u, va, q, k, v, gate_a, gate_b


def _merge_and_route(x, ya_in, att, gate_a, gate_b, wpa_ref, wpb_ref, wo_ref, gffn_ref, wrt_ref, br_ref):
    ya = _bdot(ya_in.astype(BF16), wpa_ref[...])
    yb = _bdot(att.astype(BF16), wpb_ref[...])
    mix = (gate_a * ya + gate_b * yb).astype(BF16)
    x1 = x + _bdot(mix, wo_ref[...])
    h2 = _rmsnorm(x1, gffn_ref[...])
    logits = lax.dot_general(wrt_ref[...], h2, (((1,), (1,)), ((), ())),
                             precision=lax.Precision.HIGHEST, preferred_element_type=F32) + br_ref[...]
    eid = lax.broadcasted_iota(I32, logits.shape, 0)
    vals, idxs = [], []
    for _ in range(TOP_K):
        m = jnp.max(logits, axis=0, keepdims=True)
        idx = jnp.min(jnp.where(logits == m, eid, N_EXPERTS), axis=0, keepdims=True)
        logits = jnp.where(eid == idx, -jnp.inf, logits)
        vals.append(m)
        idxs.append(idx)
    es = [jnp.exp(v - vals[0]) for v in vals]
    inv = 1.0 / (es[0] + es[1] + es[2] + es[3])
    topw = jnp.concatenate([e * inv for e in es], axis=0)
    topi = jnp.concatenate(idxs, axis=0)
    return x1, h2, topi, topw


def _store_token_major(ref, val):
    rows = val.shape[0]
    for c in range(LANE_CHUNKS):
        ref[pl.ds(c, rows, stride=LANE_CHUNKS), :] = val[:, c * LANES:(c + 1) * LANES]


def _load_token_major(ref, rows):
    return jnp.concatenate([ref[pl.ds(c, rows, stride=LANE_CHUNKS), :] for c in range(LANE_CHUNKS)], axis=1)


def _band_attention(q_b, kk, vv, sinks_ref, first):
    rows = GQA_GROUP * WINDOW
    qi = lax.broadcasted_iota(I32, (rows, 2 * WINDOW), 0) & (WINDOW - 1)
    kj = lax.broadcasted_iota(I32, (rows, 2 * WINDOW), 1)
    lo = jnp.where(first, WINDOW, 0)
    valid = (kj > qi) & (kj <= qi + WINDOW) & (kj >= lo)
    outs = []
    for h in range(N_KV_HEADS):
        ks = slice(h * HEAD_DIM, (h + 1) * HEAD_DIM)
        qh = jnp.concatenate(
            [q_b[:, (h * GQA_GROUP + j) * HEAD_DIM:(h * GQA_GROUP + j + 1) * HEAD_DIM] for j in range(GQA_GROUP)],
            axis=0).astype(BF16)
        s = lax.dot_general(qh, kk[:, ks].astype(BF16), (((1,), (1,)), ((), ())),
                            preferred_element_type=F32) * (HEAD_DIM ** -0.5)
        s = jnp.where(valid, s, -jnp.inf)
        sink = jnp.concatenate(
            [jnp.full((WINDOW, 1), sinks_ref[h * GQA_GROUP + j], F32) for j in range(GQA_GROUP)], axis=0)
        m = jnp.maximum(jnp.max(s, axis=-1, keepdims=True), sink)
        e = jnp.exp(s - m)
        inv = 1.0 / (jnp.sum(e, axis=-1, keepdims=True) + jnp.exp(sink - m))
        o = _bdot((e * inv).astype(BF16), vv[:, ks].astype(BF16))
        outs.extend(o[j * WINDOW:(j + 1) * WINDOW, :] for j in range(GQA_GROUP))
    return jnp.concatenate(outs, axis=1)


def _prompt_front_kernel(x_ref, cos_ref, sin_ref, gmix_ref, win_ref, lng_ref, lnb_ref, ws_ref, bsf_ref,
                         sinks_ref, wpa_ref, wpb_ref, wo_ref, gffn_ref, wrt_ref, br_ref,
                         x1_ref, h2_ref, topi_ref, topw_ref, kwin_ref, vwin_ref, kprev_ref, vprev_ref):
    i = pl.program_id(0)
    seq_start = (i % TILES_PER_SEQ) == 0

    @pl.when(seq_start)
    def _():
        kprev_ref[...] = jnp.zeros_like(kprev_ref)
        vprev_ref[...] = jnp.zeros_like(vprev_ref)

    x = x_ref[...]
    u, va, q, k, v, gate_a, gate_b = _in_projection(
        x, gmix_ref, win_ref, lng_ref, lnb_ref, cos_ref[...], sin_ref[...])

    ya_rows, att_rows = [], []
    k_before, v_before = kprev_ref[...], vprev_ref[...]
    for b in range(BLOCKS_PER_TILE):
        r = slice(b * WINDOW, (b + 1) * WINDOW)
        zc = jnp.concatenate(
            [_bdot(ws_ref[g], va[r, g * A_GROUP_DIM:(g + 1) * A_GROUP_DIM].astype(BF16)) for g in range(A_GROUPS)],
            axis=1) + bsf_ref[...]
        ya_rows.append(u[r] * zc)
        kk = jnp.concatenate([k_before, k[r]], axis=0)
        vv = jnp.concatenate([v_before, v[r]], axis=0)
        first = jnp.logical_and(seq_start, b == 0)
        att_rows.append(_band_attention(q[r], kk, vv, sinks_ref, first))
        k_before, v_before = k[r], v[r]
    kprev_ref[...] = k_before
    vprev_ref[...] = v_before
    kwin_ref[0] = k_before
    vwin_ref[0] = v_before

    x1, h2, topi, topw = _merge_and_route(
        x, jnp.concatenate(ya_rows, axis=0), jnp.concatenate(att_rows, axis=0), gate_a, gate_b,
        wpa_ref, wpb_ref, wo_ref, gffn_ref, wrt_ref, br_ref)
    _store_token_major(x1_ref, x1)
    _store_token_major(h2_ref, h2)
    topi_ref[...] = topi
    topw_ref[...] = topw


def _full(shape):
    return pl.BlockSpec(shape, lambda i: (0,) * len(shape))


def _prompt_front(x, cos, sin, gmix, win, lng, lnb, ws, bsf, sinks, wpa, wpb, wo, gffn, wrt, br):
    n = x.shape[0]
    grid = (n // TM,)
    in_specs = [
        pl.BlockSpec((TM, D_MODEL), lambda i: (i, 0)),
        pl.BlockSpec((TM, LANES), lambda i: (i % TILES_PER_SEQ, 0)),
        pl.BlockSpec((TM, LANES), lambda i: (i % TILES_PER_SEQ, 0)),
        _full((1, D_MODEL)),
        _full((D_MODEL, IN_COLS)),
        _full((1, A_WIDTH)),
        _full((1, A_WIDTH)),
        _full((A_GROUPS, CHUNK, CHUNK)),
        _full((CHUNK, A_WIDTH)),
        pl.BlockSpec(memory_space=pltpu.SMEM),
        _full((A_WIDTH, D_MODEL)),
        _full((Q_WIDTH, D_MODEL)),
        _full((D_MODEL, D_MODEL)),
        _full((1, D_MODEL)),
        _full((N_EXPERTS, D_MODEL)),
        _full((N_EXPERTS, 1)),
    ]
    out_shape = [
        jax.ShapeDtypeStruct((n * LANE_CHUNKS, LANES), F32),
        jax.ShapeDtypeStruct((n * LANE_CHUNKS, LANES), F32),
        jax.ShapeDtypeStruct((TOP_K, n), I32),
        jax.ShapeDtypeStruct((TOP_K, n), F32),
        jax.ShapeDtypeStruct((n // SEQ, WINDOW, KV_WIDTH), F32),
        jax.ShapeDtypeStruct((n // SEQ, WINDOW, KV_WIDTH), F32),
    ]
    out_specs = [
        pl.BlockSpec((TM * LANE_CHUNKS, LANES), lambda i: (i, 0)),
        pl.BlockSpec((TM * LANE_CHUNKS, LANES), lambda i: (i, 0)),
        pl.BlockSpec((TOP_K, TM), lambda i: (0, i)),
        pl.BlockSpec((TOP_K, TM), lambda i: (0, i)),
        pl.BlockSpec((1, WINDOW, KV_WIDTH), lambda i: (i // TILES_PER_SEQ, 0, 0)),
        pl.BlockSpec((1, WINDOW, KV_WIDTH), lambda i: (i // TILES_PER_SEQ, 0, 0)),
    ]
    return pl.pallas_call(
        _prompt_front_kernel,
        grid=grid,
        in_specs=in_specs,
        out_specs=out_specs,
        out_shape=out_shape,
        scratch_shapes=[pltpu.VMEM((WINDOW, KV_WIDTH), F32), pltpu.VMEM((WINDOW, KV_WIDTH), F32)],
        compiler_params=pltpu.CompilerParams(dimension_semantics=("arbitrary",), vmem_limit_bytes=VMEM_LIMIT),
        name="prompt_front",
    )(x, cos, sin, gmix, win, lng, lnb, ws, bsf, sinks, wpa, wpb, wo, gffn, wrt, br)


SAMPLE_STEP = 16
SAMPLE_STEPS = DEC_BATCH // SAMPLE_STEP


def _sample_kernel(x_ref, cos_ref, sin_ref, gmix_ref, win_ref, lng_ref, lnb_ref, wdiag_ref, bs0_ref, sinks_ref,
                   kc_ref, vc_ref, wpa_ref, wpb_ref, wo_ref, gffn_ref, wrt_ref, br_ref,
                   x1_ref, h2_ref, topi_ref, topw_ref, kwin_ref, vwin_ref, va_ref,
                   q_s, k_s, v_s, yain_s, ga_s, gb_s, att_s):
    i = pl.program_id(0)

    @pl.when(i == 0)
    def _():
        x = x_ref[...]
        cos = jnp.broadcast_to(cos_ref[...], (DEC_BATCH, LANES))
        sin = jnp.broadcast_to(sin_ref[...], (DEC_BATCH, LANES))
        u, va, q, k, v, gate_a, gate_b = _in_projection(x, gmix_ref, win_ref, lng_ref, lnb_ref, cos, sin)
        va_ref[...] = va
        z = wdiag_ref[...].astype(F32) * va.astype(BF16).astype(F32) + bs0_ref[...]
        yain_s[...] = u * z
        q_s[...] = q
        k_s[...] = k
        v_s[...] = v
        ga_s[...] = gate_a
        gb_s[...] = gate_b

    r0 = pl.multiple_of(i * SAMPLE_STEP, SAMPLE_STEP)
    kwin = jnp.concatenate([kc_ref[:, 1:, :], k_s[pl.ds(r0, SAMPLE_STEP), :][:, None, :]], axis=1)
    vwin = jnp.concatenate([vc_ref[:, 1:, :], v_s[pl.ds(r0, SAMPLE_STEP), :][:, None, :]], axis=1)
    kwin_ref[...] = kwin
    vwin_ref[...] = vwin

    q16 = q_s[pl.ds(r0, SAMPLE_STEP), :]
    lane = lax.broadcasted_iota(I32, (SAMPLE_STEP, LANES), 1)
    heads = []
    for hq in range(N_HEADS):
        c, p, h = hq // 2, hq % 2, hq // GQA_GROUP
        chunk = q16[:, c * LANES:(c + 1) * LANES]
        if p != h:
            chunk = pltpu.roll(chunk, HEAD_DIM, 1)
        keep = (lane < HEAD_DIM) if h == 0 else (lane >= HEAD_DIM)
        heads.append(jnp.where(keep, chunk, 0.0))
    qpad = pltpu.einshape("hbd->bhd", jnp.stack(heads, axis=0)).astype(BF16)
    s = jnp.einsum("bhd,bkd->bhk", qpad, kwin.astype(BF16), preferred_element_type=F32) * (HEAD_DIM ** -0.5)
    hid = lax.broadcasted_iota(I32, (1, N_HEADS, 1), 1)
    sink = jnp.zeros((1, N_HEADS, 1), F32)
    for hq in range(N_HEADS):
        sink = jnp.where(hid == hq, sinks_ref[hq], sink)
    m = jnp.maximum(jnp.max(s, axis=-1, keepdims=True), sink)
    e = jnp.exp(s - m)
    inv = 1.0 / (jnp.sum(e, axis=-1, keepdims=True) + jnp.exp(sink - m))
    o = jnp.einsum("bhk,bkd->bhd", (e * inv).astype(BF16), vwin.astype(BF16), preferred_element_type=F32)
    o = pltpu.einshape("bhd->hbd", o)
    chunks = []
    for c in range(N_HEADS // 2):
        parts = []
        for p in range(2):
            hq = 2 * c + p
            oh = o[hq]
            if p != hq // GQA_GROUP:
                oh = pltpu.roll(oh, HEAD_DIM, 1)
            parts.append(oh)
        chunks.append(jnp.where(lane < HEAD_DIM, parts[0], parts[1]))
    att_s[pl.ds(r0, SAMPLE_STEP), :] = jnp.concatenate(chunks, axis=1)

    @pl.when(i == SAMPLE_STEPS - 1)
    def _():
        x1, h2, topi, topw = _merge_and_route(
            x_ref[...], yain_s[...], att_s[...], ga_s[...], gb_s[...],
            wpa_ref, wpb_ref, wo_ref, gffn_ref, wrt_ref, br_ref)
        _store_token_major(x1_ref, x1)
        _store_token_major(h2_ref, h2)
        topi_ref[...] = topi
        topw_ref[...] = topw


def _sample_front(x, cos, sin, gmix, win, lng, lnb, wdiag, bs0, sinks, kc, vc, wpa, wpb, wo, gffn, wrt, br):
    n = DEC_BATCH
    cache_spec = pl.BlockSpec((SAMPLE_STEP, WINDOW, KV_WIDTH), lambda i: (i, 0, 0))
    in_specs = [
        _full((n, D_MODEL)),
        _full((1, LANES)),
        _full((1, LANES)),
        _full((1, D_MODEL)),
        _full((D_MODEL, IN_COLS)),
        _full((1, A_WIDTH)),
        _full((1, A_WIDTH)),
        _full((1, A_WIDTH)),
        _full((1, A_WIDTH)),
        pl.BlockSpec(memory_space=pltpu.SMEM),
        cache_spec,
        cache_spec,
        _full((A_WIDTH, D_MODEL)),
        _full((Q_WIDTH, D_MODEL)),
        _full((D_MODEL, D_MODEL)),
        _full((1, D_MODEL)),
        _full((N_EXPERTS, D_MODEL)),
        _full((N_EXPERTS, 1)),
    ]
    out_shape = [
        jax.ShapeDtypeStruct((n * LANE_CHUNKS, LANES), F32),
        jax.ShapeDtypeStruct((n * LANE_CHUNKS, LANES), F32),
        jax.ShapeDtypeStruct((TOP_K, n), I32),
        jax.ShapeDtypeStruct((TOP_K, n), F32),
        jax.ShapeDtypeStruct((n, WINDOW, KV_WIDTH), F32),
        jax.ShapeDtypeStruct((n, WINDOW, KV_WIDTH), F32),
        jax.ShapeDtypeStruct((n, A_WIDTH), F32),
    ]
    out_specs = [
        _full((n * LANE_CHUNKS, LANES)),
        _full((n * LANE_CHUNKS, LANES)),
        _full((TOP_K, n)),
        _full((TOP_K, n)),
        cache_spec,
        cache_spec,
        _full((n, A_WIDTH)),
    ]
    scratch = [
        pltpu.VMEM((n, Q_WIDTH), F32), pltpu.VMEM((n, KV_WIDTH), F32), pltpu.VMEM((n, KV_WIDTH), F32),
        pltpu.VMEM((n, A_WIDTH), F32), pltpu.VMEM((n, D_MODEL), F32), pltpu.VMEM((n, D_MODEL), F32),
        pltpu.VMEM((n, Q_WIDTH), F32),
    ]
    return pl.pallas_call(
        _sample_kernel,
        grid=(SAMPLE_STEPS,),
        in_specs=in_specs,
        out_specs=out_specs,
        out_shape=out_shape,
        scratch_shapes=scratch,
        compiler_params=pltpu.CompilerParams(dimension_semantics=("arbitrary",), vmem_limit_bytes=VMEM_LIMIT),
        name="sample_front",
    )(x, cos, sin, gmix, win, lng, lnb, wdiag, bs0, sinks, kc, vc, wpa, wpb, wo, gffn, wrt, br)


def _route_plan_kernel(ip_ref, is_ref, wp_ref, wsm_ref, dest_ref, wts_ref, off_ref):
    g = pl.program_id(0)
    e_s = jnp.where(g == N_GROUPS - 1, is_ref[...], N_EXPERTS)
    eall = jnp.concatenate([ip_ref[...], e_s], axis=1)
    wts_ref[:, 0:GROUP_SLOTS] = jnp.concatenate([wp_ref[...], wsm_ref[...]], axis=1)
    wts_ref[:, GROUP_SLOTS:] = jnp.zeros((TOP_K, K_STRIDE - GROUP_SLOTS), F32)
    dest_ref[:, GROUP_SLOTS:] = jnp.zeros((TOP_K, K_STRIDE - GROUP_SLOTS), I32)
    eid = lax.broadcasted_iota(I32, (N_EXPERTS, GROUP_SLOTS), 0)
    onehots = [eall[k:k + 1, :] == eid for k in range(TOP_K)]
    count = jnp.zeros((N_EXPERTS, GROUP_SLOTS), F32)
    for oh in onehots:
        count = count + oh.astype(F32)
    total = jnp.broadcast_to(jnp.sum(count, axis=1, keepdims=True), (N_EXPERTS, LANES))
    nblk = jnp.floor((total + (MOE_ROWS - 1)) * (1.0 / MOE_ROWS))
    r = lax.broadcasted_iota(I32, (N_EXPERTS, N_EXPERTS), 0)
    c = lax.broadcasted_iota(I32, (N_EXPERTS, N_EXPERTS), 1)
    first_blk = lax.dot_general((c < r).astype(F32), nblk, (((1,), (0,)), ((), ())),
                                precision=lax.Precision.HIGHEST, preferred_element_type=F32)
    start = (first_blk + 1.0) * MOE_ROWS
    lane = lax.broadcasted_iota(I32, (N_EXPERTS, LANES), 1)
    info = jnp.where(lane == 0, start, jnp.where(lane == 1, start + total, jnp.where(lane == 2, nblk, first_blk)))
    off_ref[...] = info.astype(I32)
    ti = lax.broadcasted_iota(I32, (LANES, LANES), 0)
    tj = lax.broadcasted_iota(I32, (LANES, LANES), 1)
    before = (ti < tj).astype(BF16)
    ones = jnp.ones((LANES, LANES), BF16)
    running = start
    for t in range(SLOT_TILES):
        sl = slice(t * LANES, (t + 1) * LANES)
        cb = count[:, sl].astype(BF16)
        pos = running + _bdot(cb, before)
        rows = [jnp.sum(jnp.where(oh[:, sl], pos, 0.0), axis=0, keepdims=True) for oh in onehots]
        dest_ref[:, sl] = jnp.concatenate(rows, axis=0).astype(I32)
        running = running + _bdot(cb, ones)


def _route_plan(topi_p, topi_s, topw_p, topw_s):
    in_specs = [
        pl.BlockSpec((TOP_K, GROUP_PROMPT), lambda g: (0, g)),
        pl.BlockSpec((TOP_K, DEC_BATCH), lambda g: (0, 0)),
        pl.BlockSpec((TOP_K, GROUP_PROMPT), lambda g: (0, g)),
        pl.BlockSpec((TOP_K, DEC_BATCH), lambda g: (0, 0)),
    ]
    out_shape = [
        jax.ShapeDtypeStruct((N_GROUPS, TOP_K, K_STRIDE), I32),
        jax.ShapeDtypeStruct((N_GROUPS, TOP_K, K_STRIDE), F32),
        jax.ShapeDtypeStruct((N_GROUPS, N_EXPERTS, LANES), I32),
    ]
    out_specs = [
        pl.BlockSpec((None, TOP_K, K_STRIDE), lambda g: (g, 0, 0)),
        pl.BlockSpec((None, TOP_K, K_STRIDE), lambda g: (g, 0, 0)),
        pl.BlockSpec((None, N_EXPERTS, LANES), lambda g: (g, 0, 0)),
    ]
    return pl.pallas_call(
        _route_plan_kernel,
        grid=(N_GROUPS,),
        in_specs=in_specs,
        out_specs=out_specs,
        out_shape=out_shape,
        compiler_params=pltpu.CompilerParams(dimension_semantics=("arbitrary",)),
        name="route_plan",
    )(topi_p, topi_s, topw_p, topw_s)


GROUP_ROWS = GROUP_PROMPT * LANE_CHUNKS
SAMPLE_ROWS = DEC_BATCH * LANE_CHUNKS
TRASH_SLOT = GROUP_SLOTS
BUF_ROWS = (GROUP_SLOTS + 1) * LANE_CHUNKS
SCATTER_BATCH = 8


def _moe_kernel(off_ref, desth_ref, wtsh_ref, h2p_ref, h2s_ref, x1p_ref, x1s_ref, wgu_ref, bgu_ref, wdn_ref, bdn_ref,
                x2p_ref, x2s_ref,
                h2buf, acc, wgubuf, bgubuf, wdnbuf, bdnbuf, xs0, xs1, ys0, ys1,
                dest_ref, wts_ref, src_ref, seg_expert, seg_first, blk_seg, act_sem, w_sem):
    g = pl.program_id(0)
    last = g == N_GROUPS - 1
    row0 = pl.multiple_of(g * GROUP_ROWS, GROUP_ROWS)

    def prompt_copies():
        return (pltpu.make_async_copy(h2p_ref.at[pl.ds(row0, GROUP_ROWS)], h2buf.at[pl.ds(0, GROUP_ROWS)], act_sem.at[0]),
                pltpu.make_async_copy(x1p_ref.at[pl.ds(row0, GROUP_ROWS)], acc.at[pl.ds(0, GROUP_ROWS)], act_sem.at[1]))

    def sample_copies():
        return (pltpu.make_async_copy(h2s_ref, h2buf.at[pl.ds(GROUP_ROWS, SAMPLE_ROWS)], act_sem.at[2]),
                pltpu.make_async_copy(x1s_ref, acc.at[pl.ds(GROUP_ROWS, SAMPLE_ROWS)], act_sem.at[3]))

    def weight_copies(e, slot):
        return (pltpu.make_async_copy(wgu_ref.at[e], wgubuf.at[slot], w_sem.at[0, slot]),
                pltpu.make_async_copy(bgu_ref.at[e], bgubuf.at[slot], w_sem.at[1, slot]),
                pltpu.make_async_copy(wdn_ref.at[e], wdnbuf.at[slot], w_sem.at[2, slot]),
                pltpu.make_async_copy(bdn_ref.at[e], bdnbuf.at[slot], w_sem.at[3, slot]))

    tab0 = pl.multiple_of(g * (TOP_K * K_STRIDE), TOP_K * K_STRIDE)
    table_copies = (
        pltpu.make_async_copy(desth_ref.at[pl.ds(tab0, TOP_K * K_STRIDE)], dest_ref, act_sem.at[4]),
        pltpu.make_async_copy(wtsh_ref.at[pl.ds(tab0, TOP_K * K_STRIDE)], wts_ref, act_sem.at[5]))
    for cp in table_copies:
        cp.start()

    for cp in prompt_copies():
        cp.start()

    @pl.when(last)
    def _():
        for cp in sample_copies():
            cp.start()

    trash = pl.ds(TRASH_SLOT * LANE_CHUNKS, LANE_CHUNKS)
    h2buf[trash, :] = jnp.zeros((LANE_CHUNKS, LANES), F32)
    acc[trash, :] = jnp.zeros((LANE_CHUNKS, LANES), F32)
    ys1[...] = jnp.zeros_like(ys1)

    def pad_codes(lo, hi):
        def body(p, carry):
            src_ref[p] = TRASH_SLOT
            return carry
        lax.fori_loop(lo, hi, body, 0)

    def scan_expert(e, carry):
        nseg, nblocks = carry
        nblk = off_ref[e, 2]
        first = off_ref[e, 3]

        @pl.when(nblk > 0)
        def _():
            seg_expert[nseg] = e
            seg_first[nseg] = first
            pad_codes(off_ref[e, 1], off_ref[e, 0] + nblk * MOE_ROWS)

            def mark(b, c2):
                blk_seg[first + b] = nseg
                return c2
            lax.fori_loop(0, nblk, mark, 0)

        return nseg + jnp.where(nblk > 0, 1, 0), nblocks + nblk

    nseg, nblocks = lax.fori_loop(0, N_EXPERTS, scan_expert, (jnp.int32(0), jnp.int32(0)))
    pad_codes(0, MOE_ROWS)
    pad_codes((nblocks + 1) * MOE_ROWS, (nblocks + 3) * MOE_ROWS)
    blk_seg[nblocks] = nseg - 1
    blk_seg[nblocks + 1] = nseg - 1

    for cp in weight_copies(seg_expert[0], 0):
        cp.start()

    for cp in table_copies:
        cp.wait()

    nvalid = jnp.where(last, GROUP_SLOTS, GROUP_PROMPT)
    for k in range(TOP_K):
        def fill(j, carry, k=k):
            c0 = k * K_STRIDE + j * SUBLANES
            for d in range(SUBLANES):
                src_ref[dest_ref[c0 + d]] = c0 + d
            return carry
        lax.fori_loop(0, nvalid // SUBLANES, fill, 0)

    for cp in prompt_copies():
        cp.wait()

    @pl.when(last)
    def _():
        for cp in sample_copies():
            cp.wait()

    def token_rows(code):
        slot_id = code & (K_STRIDE - 1)
        return pl.ds(pl.multiple_of(slot_id * LANE_CHUNKS, LANE_CHUNKS), LANE_CHUNKS)

    def gather(b, xs):
        base = (b + 1) * MOE_ROWS
        for m in range(MOE_ROWS):
            xs[pl.ds(m, LANE_CHUNKS, stride=XS_STRIDE), :] = h2buf[token_rows(src_ref[base + m]), :]

    def scatter_add(b, ys):
        base = (b + 1) * MOE_ROWS
        for m0 in range(0, MOE_ROWS, SCATTER_BATCH):
            pending = []
            for m in range(m0, m0 + SCATTER_BATCH):
                code = src_ref[base + m]
                rows = token_rows(code)
                pending.append((rows, acc[rows, :] + wts_ref[code] * ys[pl.ds(m, LANE_CHUNKS, stride=XS_STRIDE), :]))
            for rows, val in pending:
                acc[rows, :] = val

    def expert_ffn(xs, ys, slot):
        x = jnp.concatenate(
            [xs[c * XS_STRIDE:c * XS_STRIDE + MOE_ROWS, :] for c in range(LANE_CHUNKS)], axis=1).astype(BF16)
        gu = _bdot(x, wgubuf[slot]) + bgubuf[slot]
        gl = jnp.minimum(gu[:, :D_EXPERT], SWIGLU_LIMIT)
        ul = jnp.clip(gu[:, D_EXPERT:], -SWIGLU_LIMIT, SWIGLU_LIMIT)
        a = (ul + 1.0) * (gl * jax.nn.sigmoid(SWIGLU_ALPHA * gl))
        y = _bdot(a.astype(BF16), wdnbuf[slot]) + bdnbuf[slot]
        for c in range(LANE_CHUNKS):
            ys[c * XS_STRIDE:c * XS_STRIDE + MOE_ROWS, :] = y[:, c * LANES:(c + 1) * LANES]

    def step(b, xs_cur, xs_next, ys_cur, ys_prev):
        seg = blk_seg[b]
        slot = seg & 1

        @pl.when(jnp.logical_and(b == seg_first[seg], b < nblocks))
        def _():
            for cp in weight_copies(seg_expert[seg], slot):
                cp.wait()

            @pl.when(seg + 1 < nseg)
            def _():
                for cp in weight_copies(seg_expert[seg + 1], 1 - slot):
                    cp.start()

        gather(b + 1, xs_next)
        expert_ffn(xs_cur, ys_cur, slot)
        scatter_add(b - 1, ys_prev)

    gather(0, xs0)
    npairs = (nblocks + 1) // 2

    def pair(t, carry):
        step(2 * t, xs0, xs1, ys0, ys1)
        step(2 * t + 1, xs1, xs0, ys1, ys0)
        return carry

    lax.fori_loop(0, npairs, pair, 0)
    scatter_add(2 * npairs - 1, ys1)

    out_p =pltpu.make_async_copy(acc.at[pl.ds(0, GROUP_ROWS)], x2p_ref.at[pl.ds(row0, GROUP_ROWS)], act_sem.at[0])
    out_p.start()

    @pl.when(last)
    def _():
        out_s = pltpu.make_async_copy(acc.at[pl.ds(GROUP_ROWS, SAMPLE_ROWS)], x2s_ref, act_sem.at[2])
        out_s.start()
        out_s.wait()

    out_p.wait()


def _moe(dest, wts, off, h2p, h2s, x1p, x1s, wgu, bgu, wdn, bdn):
    anyspec = pl.BlockSpec(memory_space=pl.ANY)
    dest = dest.reshape(N_GROUPS * TOP_K * K_STRIDE)
    wts = wts.reshape(N_GROUPS * TOP_K * K_STRIDE)
    in_specs = [
        pl.BlockSpec((None, N_EXPERTS, LANES), lambda g: (g, 0, 0), memory_space=pltpu.SMEM),
        anyspec, anyspec, anyspec, anyspec, anyspec, anyspec, anyspec, anyspec, anyspec, anyspec,
    ]
    scratch = [
        pltpu.VMEM((BUF_ROWS, LANES), F32),
        pltpu.VMEM((BUF_ROWS, LANES), F32),
        pltpu.VMEM((2, D_MODEL, 2 * D_EXPERT), BF16),
        pltpu.VMEM((2, 1, 2 * D_EXPERT), F32),
        pltpu.VMEM((2, D_EXPERT, D_MODEL), BF16),
        pltpu.VMEM((2, 1, D_MODEL), F32),
        pltpu.VMEM((LANE_CHUNKS * XS_STRIDE, LANES), F32),
        pltpu.VMEM((LANE_CHUNKS * XS_STRIDE, LANES), F32),
        pltpu.VMEM((LANE_CHUNKS * XS_STRIDE, LANES), F32),
        pltpu.VMEM((LANE_CHUNKS * XS_STRIDE, LANES), F32),
        pltpu.SMEM((TOP_K * K_STRIDE,), I32),
        pltpu.SMEM((TOP_K * K_STRIDE,), F32),
        pltpu.SMEM((POS_TABLE,), I32),
        pltpu.SMEM((N_EXPERTS,), I32),
        pltpu.SMEM((N_EXPERTS,), I32),
        pltpu.SMEM((LANES,), I32),
        pltpu.SemaphoreType.DMA((6,)),
        pltpu.SemaphoreType.DMA((4, 2)),
    ]
    return pl.pallas_call(
        _moe_kernel,
        grid=(N_GROUPS,),
        in_specs=in_specs,
        out_specs=[anyspec, anyspec],
        out_shape=[jax.ShapeDtypeStruct(x1p.shape, F32), jax.ShapeDtypeStruct(x1s.shape, F32)],
        scratch_shapes=scratch,
        compiler_params=pltpu.CompilerParams(dimension_semantics=("arbitrary",), vmem_limit_bytes=VMEM_LIMIT),
        name="moe",
    )(off, dest, wts, h2p, h2s, x1p, x1s, wgu, bgu, wdn, bdn)


def _ple_final_kernel(x2_ref, ple_ref, wple_ref, gple_ref, wpg_ref, gfin_ref, y_ref):
    rows = y_ref.shape[0]
    x2 = _load_token_major(x2_ref, rows)
    e = _rmsnorm(_bdot(ple_ref[...].astype(BF16), wple_ref[...]), gple_ref[...])
    x3 = x2 + jax.nn.sigmoid(_bdot(x2.astype(BF16), wpg_ref[...])) * e
    y_ref[...] = _rmsnorm(x3, gfin_ref[...])


def _ple_final(x2_tm, ple, wple, gple, wpg, gfin, tile):
    n = ple.shape[0]
    return pl.pallas_call(
        _ple_final_kernel,
        grid=(n // tile,),
        in_specs=[
            pl.BlockSpec((tile * LANE_CHUNKS, LANES), lambda i: (i, 0)),
            pl.BlockSpec((tile, PLE_DIM), lambda i: (i, 0)),
            _full((PLE_DIM, D_MODEL)),
            _full((1, D_MODEL)),
            _full((D_MODEL, D_MODEL)),
            _full((1, D_MODEL)),
        ],
        out_specs=pl.BlockSpec((tile, D_MODEL), lambda i: (i, 0)),
        out_shape=jax.ShapeDtypeStruct((n, D_MODEL), F32),
        compiler_params=pltpu.CompilerParams(dimension_semantics=("arbitrary",), vmem_limit_bytes=VMEM_LIMIT),
        name="ple_final",
    )(x2_tm, ple, wple, gple, wpg, gfin)


def _rope_tables(pos):
    half = HEAD_DIM // 2
    inv = ROPE_THETA ** (-jnp.arange(half, dtype=F32) / half)
    ang = pos.astype(F32)[:, None] * inv[None, :]
    cos, sin = jnp.cos(ang), jnp.sin(ang)
    cos2 = jnp.concatenate([cos, cos, cos, cos], axis=1)
    sin2 = jnp.concatenate([-sin, sin, -sin, sin], axis=1)
    return cos2, sin2


def _prep_weights(g_mix, w_in, a_ln_g, a_ln_b, a_ws, a_bs, w_pa, w_pb, w_o, g_ffn, w_router, b_router):
    causal = jnp.tril(jnp.ones((CHUNK, CHUNK), dtype=bool))
    return dict(
        gmix=g_mix.reshape(1, D_MODEL),
        win=w_in.astype(BF16),
        lng=a_ln_g.reshape(1, A_WIDTH),
        lnb=a_ln_b.reshape(1, A_WIDTH),
        ws=jnp.where(causal[None], a_ws, 0.0).astype(BF16),
        bsf=jnp.repeat(jnp.transpose(a_bs), A_GROUP_DIM, axis=1),
        wpa=w_pa.astype(BF16),
        wpb=w_pb.astype(BF16),
        wo=w_o.astype(BF16),
        gffn=g_ffn.reshape(1, D_MODEL),
        wrt=jnp.transpose(w_router),
        br=b_router.reshape(N_EXPERTS, 1),
    )


def kernel(x_prompt, x_sample, cache_win_k, cache_win_v, p_prompt, p_sample, g_mix, w_in, a_ln_g, a_ln_b, a_ws, a_bs, sinks, w_pa, w_pb, w_o, g_ffn, w_router, b_router, w_gu, b_gu, w_down, b_down, w_ple, g_ple, w_ple_gate, g_final):
    W = _prep_weights(g_mix[0], w_in[0], a_ln_g[0], a_ln_b[0], a_ws[0], a_bs[0], w_pa[0], w_pb[0], w_o[0],
                      g_ffn[0], w_router[0], b_router[0])
    cos_p, sin_p = _rope_tables(jnp.arange(SEQ, dtype=I32))
    cos_s, sin_s = _rope_tables(jnp.full((1,), PAST_LEN, I32))
    x1p, h2p, topi_p, topw_p, kwin_p, vwin_p = _prompt_front(
        x_prompt.reshape(N_PROMPT, D_MODEL), cos_p, sin_p, W["gmix"], W["win"], W["lng"], W["lnb"],
        W["ws"], W["bsf"], sinks[0], W["wpa"], W["wpb"], W["wo"], W["gffn"], W["wrt"], W["br"])

    wdiag = jnp.repeat(a_ws[0, :, 0, 0], A_GROUP_DIM)[None, :].astype(BF16)
    bs0 = jnp.repeat(a_bs[0, :, 0], A_GROUP_DIM)[None, :]
    x1s, h2s, topi_s, topw_s, kwin_s, vwin_s, va_s = _sample_front(
        x_sample.reshape(DEC_BATCH, D_MODEL), cos_s, sin_s, W["gmix"], W["win"], W["lng"], W["lnb"], wdiag, bs0,
        sinks[0], cache_win_k[0].reshape(DEC_BATCH, WINDOW, KV_WIDTH), cache_win_v[0].reshape(DEC_BATCH, WINDOW, KV_WIDTH),
        W["wpa"], W["wpb"], W["wo"], W["gffn"], W["wrt"], W["br"])

    dest, wts, off = _route_plan(topi_p, topi_s, topw_p, topw_s)
    x2p, x2s = _moe(dest, wts, off, h2p, h2s, x1p, x1s,
                    w_gu[0].astype(BF16), b_gu[0].reshape(N_EXPERTS, 1, 2 * D_EXPERT),
                    w_down[0].astype(BF16), b_down[0].reshape(N_EXPERTS, 1, D_MODEL))

    wple = w_ple[0].astype(BF16)
    gple = g_ple[0].reshape(1, D_MODEL)
    wpg = w_ple_gate[0].astype(BF16)
    gfin = g_final.reshape(1, D_MODEL)
    y_p = _ple_final(x2p, p_prompt[0].reshape(N_PROMPT, PLE_DIM), wple, gple, wpg, gfin, TM)
    y_s = _ple_final(x2s, p_sample[0].reshape(DEC_BATCH, PLE_DIM), wple, gple, wpg, gfin, DEC_BATCH)

    return (
        y_p.reshape(BATCH, SEQ, D_MODEL),
        y_s.reshape(DEC_BATCH, 1, D_MODEL),
        kwin_p.reshape(1, BATCH, WINDOW, N_KV_HEADS, HEAD_DIM),
        vwin_p.reshape(1, BATCH, WINDOW, N_KV_HEADS, HEAD_DIM),
        kwin_s.reshape(1, DEC_BATCH, WINDOW, N_KV_HEADS, HEAD_DIM),
        vwin_s.reshape(1, DEC_BATCH, WINDOW, N_KV_HEADS, HEAD_DIM),
        va_s.reshape(1, DEC_BATCH, 1, A_WIDTH),
    )
```

```python
import functools

import jax
import jax.numpy as jnp
from jax import lax
from jax.experimental import pallas as pl
from jax.experimental.pallas import tpu as pltpu

F32 = jnp.float32
BF16 = jnp.bfloat16
I32 = jnp.int32

D_MODEL = 1024
BATCH = 4
SEQ = 4096
DEC_BATCH = 128
PAST_LEN = 8192
CHUNK = 128
A_GROUPS = 4
A_GROUP_DIM = 128
A_WIDTH = A_GROUPS * A_GROUP_DIM
N_HEADS = 8
N_KV_HEADS = 2
HEAD_DIM = 64
Q_WIDTH = N_HEADS * HEAD_DIM
KV_WIDTH = N_KV_HEADS * HEAD_DIM
GQA_GROUP = N_HEADS // N_KV_HEADS
WINDOW = 128
ROPE_THETA = 10000.0
N_EXPERTS = 32
TOP_K = 4
D_EXPERT = D_MODEL
SWIGLU_ALPHA = 1.702
SWIGLU_LIMIT = 7.0
PLE_DIM = 256
RMS_EPS = 1e-5
LN_EPS = 1e-5

O_Q = 2 * A_WIDTH
O_K = O_Q + Q_WIDTH
O_V = O_K + KV_WIDTH
O_GA = O_V + KV_WIDTH
O_GB = O_GA + D_MODEL
IN_COLS = O_GB + D_MODEL

LANES = 128
SUBLANES = 8
LANE_CHUNKS = D_MODEL // LANES
VMEM_LIMIT = 56 * 1024 * 1024

N_PROMPT = BATCH * SEQ
TM = 256
TILES_PER_SEQ = SEQ // TM
BLOCKS_PER_TILE = TM // WINDOW

N_GROUPS = 4
GROUP_PROMPT = N_PROMPT // N_GROUPS
GROUP_SLOTS = GROUP_PROMPT + DEC_BATCH
GROUP_ASSIGN = GROUP_SLOTS * TOP_K
SLOT_TILES = GROUP_SLOTS // LANES
MOE_ROWS = 256
XS_STRIDE = MOE_ROWS + SUBLANES
SLOT_BITS = 13
K_STRIDE = 1 << SLOT_BITS
assert GROUP_SLOTS < K_STRIDE
MAX_BLOCKS = GROUP_ASSIGN // MOE_ROWS + N_EXPERTS
POS_TABLE = 1 << 15
assert (MAX_BLOCKS + 3) * MOE_ROWS <= POS_TABLE
assert MAX_BLOCKS + 2 <= LANES


def _bdot(a, b):
    return jnp.dot(a, b, preferred_element_type=F32)


def _rmsnorm(x, g):
    return x * lax.rsqrt(jnp.mean(x * x, axis=-1, keepdims=True) + RMS_EPS) * g


def _gelu(x):
    return 0.5 * x * (1.0 + lax.erf(x * (0.5 ** 0.5)))


def _group_layernorm(v, g, b):
    cols = []
    for gi in range(A_GROUPS):
        s = slice(gi * A_GROUP_DIM, (gi + 1) * A_GROUP_DIM)
        vg = v[:, s]
        mu = jnp.mean(vg, axis=-1, keepdims=True)
        d = vg - mu
        var = jnp.mean(d * d, axis=-1, keepdims=True)
        cols.append(d * lax.rsqrt(var + LN_EPS) * g[:, s] + b[:, s])
    return jnp.concatenate(cols, axis=1)


def _rope(x, cos, sin_signed):
    width = x.shape[1]
    reps = width // LANES
    cosf = jnp.concatenate([cos] * reps, axis=1) if reps > 1 else cos
    sinf = jnp.concatenate([sin_signed] * reps, axis=1) if reps > 1 else sin_signed
    half = HEAD_DIM // 2
    lane = lax.broadcasted_iota(I32, x.shape, 1)
    up = pltpu.roll(x, width - half, 1)
    down = pltpu.roll(x, half, 1)
    partner = jnp.where((lane & (HEAD_DIM - 1)) < half, up, down)
    return x * cosf + partner * sinf


def _in_projection(x, gmix_ref, win_ref, lng_ref, lnb_ref, cos, sin_signed):
    hb = _rmsnorm(x, gmix_ref[...]).astype(BF16)
    zuv = _gelu(_bdot(hb, win_ref[:, 0:O_Q]))
    u = zuv[:, :A_WIDTH]
    va = _group_layernorm(zuv[:, A_WIDTH:], lng_ref[...], lnb_ref[...])
    zqkv = _bdot(hb, win_ref[:, O_Q:O_GA])
    q = _rope(zqkv[:, :Q_WIDTH], cos, sin_signed)
    k = _rope(zqkv[:, Q_WIDTH:Q_WIDTH + KV_WIDTH], cos, sin_signed)
    v = zqkv[:, Q_WIDTH + KV_WIDTH:]
    zg = _bdot(hb, win_ref[:, O_GA:IN_COLS])
    gate_a = jax.nn.sigmoid(zg[:, :D_MODEL])
    gate_b = jax.nn.sigmoid(zg[:, D_MODEL:])
    return u, va, q, k, v, gate_a, gate_b


def _merge_and_route(x, ya_in, att, gate_a, gate_b, wpa_ref, wpb_ref, wo_ref, gffn_ref, wr3_ref, br_ref):
    ya = _bdot(ya_in.astype(BF16), wpa_ref[...])
    yb = _bdot(att.astype(BF16), wpb_ref[...])
    mix = (gate_a * ya + gate_b * yb).astype(BF16)
    x1 = x + _bdot(mix, wo_ref[...])
    h2 = _rmsnorm(x1, gffn_ref[...])
    hi = h2.astype(BF16)
    lo = (h2 - hi.astype(F32)).astype(BF16)
    logits = _bdot(jnp.concatenate([hi, hi, lo], axis=1), wr3_ref[...])
    return x1, h2, jnp.transpose(logits)[:N_EXPERTS, :] + br_ref[...]


def _top4_softmax(logits):
    eid = lax.broadcasted_iota(I32, logits.shape, 0)
    vals, idxs = [], []
    for _ in range(TOP_K):
        m = jnp.max(logits, axis=0, keepdims=True)
        idx = jnp.min(jnp.where(logits == m, eid, N_EXPERTS), axis=0, keepdims=True)
        logits = jnp.where(eid == idx, -jnp.inf, logits)
        vals.append(m)
        idxs.append(idx)
    es = [jnp.exp(v - vals[0]) for v in vals]
    inv = 1.0 / (es[0] + es[1] + es[2] + es[3])
    return jnp.concatenate(idxs, axis=0), jnp.concatenate([e * inv for e in es], axis=0)


def _store_token_major(ref, val):
    rows = val.shape[0]
    for c in range(LANE_CHUNKS):
        ref[pl.ds(c, rows, stride=LANE_CHUNKS), :] = val[:, c * LANES:(c + 1) * LANES]


def _load_token_major(ref, rows):
    return jnp.concatenate([ref[pl.ds(c, rows, stride=LANE_CHUNKS), :] for c in range(LANE_CHUNKS)], axis=1)


def _band_attention(q, k, v, k_prev, v_prev, sinks_ref, seq_start):
    unit_rows = GQA_GROUP * WINDOW
    kb = jnp.concatenate([k_prev, k], axis=0).astype(BF16)
    vb = jnp.concatenate([v_prev, v], axis=0).astype(BF16)
    qb = q.astype(BF16)
    units = [(b, h) for b in range(BLOCKS_PER_TILE) for h in range(N_KV_HEADS)]
    scores = []
    for b, h in units:
        qh = jnp.concatenate(
            [qb[b * WINDOW:(b + 1) * WINDOW, (h * GQA_GROUP + j) * HEAD_DIM:(h * GQA_GROUP + j + 1) * HEAD_DIM]
             for j in range(GQA_GROUP)], axis=0)
        kh = kb[b * WINDOW:(b + 2) * WINDOW, h * HEAD_DIM:(h + 1) * HEAD_DIM]
        scores.append(lax.dot_general(qh, kh, (((1,), (1,)), ((), ())), preferred_element_type=F32))
    s = jnp.concatenate(scores, axis=0) * (HEAD_DIM ** -0.5)
    shape = s.shape
    row = lax.broadcasted_iota(I32, shape, 0)
    qi = row & (WINDOW - 1)
    kj = lax.broadcasted_iota(I32, shape, 1)
    lo = jnp.where(jnp.logical_and(seq_start, row < N_KV_HEADS * unit_rows), WINDOW, 0)
    valid = (kj > qi) & (kj <= qi + WINDOW) & (kj >= lo)
    s = jnp.where(valid, s, -jnp.inf)
    sink = jnp.concatenate(
        [jnp.full((WINDOW, 1), sinks_ref[h * GQA_GROUP + j], F32) for b, h in units for j in range(GQA_GROUP)], axis=0)
    m = jnp.maximum(jnp.max(s, axis=-1, keepdims=True), sink)
    e = jnp.exp(s - m)
    inv = 1.0 / (jnp.sum(e, axis=-1, keepdims=True) + jnp.exp(sink - m))
    p = (e * inv).astype(BF16)
    att_rows = []
    for b in range(BLOCKS_PER_TILE):
        heads = []
        for h in range(N_KV_HEADS):
            u = b * N_KV_HEADS + h
            vh = vb[b * WINDOW:(b + 2) * WINDOW, h * HEAD_DIM:(h + 1) * HEAD_DIM]
            o = _bdot(p[u * unit_rows:(u + 1) * unit_rows], vh)
            heads.extend(o[j * WINDOW:(j + 1) * WINDOW, :] for j in range(GQA_GROUP))
        att_rows.append(jnp.concatenate(heads, axis=1))
    return jnp.concatenate(att_rows, axis=0)


def _prompt_front_kernel(x_ref, cos_ref, sin_ref, gmix_ref, win_ref, lng_ref, lnb_ref, ws_ref, bsf_ref,
                         sinks_ref, wpa_ref, wpb_ref, wo_ref, gffn_ref, wr3_ref, br_ref,
                         x1_ref, h2_ref, logits_ref, kwin_ref, vwin_ref, kprev_ref, vprev_ref):
    i = pl.program_id(0)
    seq_start = (i % TILES_PER_SEQ) == 0

    @pl.when(seq_start)
    def _():
        kprev_ref[...] = jnp.zeros_like(kprev_ref)
        vprev_ref[...] = jnp.zeros_like(vprev_ref)

    x = x_ref[...]
    u, va, q, k, v, gate_a, gate_b = _in_projection(
        x, gmix_ref, win_ref, lng_ref, lnb_ref, cos_ref[...], sin_ref[...])

    att = _band_attention(q, k, v, kprev_ref[...], vprev_ref[...], sinks_ref, seq_start)
    k_last, v_last = k[TM - WINDOW:], v[TM - WINDOW:]
    kprev_ref[...] = k_last
    vprev_ref[...] = v_last
    kwin_ref[0] = k_last
    vwin_ref[0] = v_last

    vab = va.astype(BF16)
    zc = jnp.concatenate(
        [jnp.concatenate(
            [_bdot(ws_ref[g], vab[b * CHUNK:(b + 1) * CHUNK, g * A_GROUP_DIM:(g + 1) * A_GROUP_DIM])
             for g in range(A_GROUPS)], axis=1) + bsf_ref[...]
         for b in range(BLOCKS_PER_TILE)], axis=0)

    x1, h2, logits = _merge_and_route(x, u * zc, att, gate_a, gate_b,
                                      wpa_ref, wpb_ref, wo_ref, gffn_ref, wr3_ref, br_ref)
    _store_token_major(x1_ref, x1)
    _store_token_major(h2_ref, h2)
    logits_ref[...] = logits


def _full(shape):
    return pl.BlockSpec(shape, lambda i: (0,) * len(shape))


def _prompt_front(x, cos, sin, gmix, win, lng, lnb, ws, bsf, sinks, wpa, wpb, wo, gffn, wrt, br):
    n = x.shape[0]
    grid = (n // TM,)
    in_specs = [
        pl.BlockSpec((TM, D_MODEL), lambda i: (i, 0)),
        pl.BlockSpec((TM, LANES), lambda i: (i % TILES_PER_SEQ, 0)),
        pl.BlockSpec((TM, LANES), lambda i: (i % TILES_PER_SEQ, 0)),
        _full((1, D_MODEL)),
        _full((D_MODEL, IN_COLS)),
        _full((1, A_WIDTH)),
        _full((1, A_WIDTH)),
        _full((A_GROUPS, CHUNK, CHUNK)),
        _full((CHUNK, A_WIDTH)),
        pl.BlockSpec(memory_space=pltpu.SMEM),
        _full((A_WIDTH, D_MODEL)),
        _full((Q_WIDTH, D_MODEL)),
        _full((D_MODEL, D_MODEL)),
        _full((1, D_MODEL)),
        _full((3 * D_MODEL, LANES)),
        _full((N_EXPERTS, 1)),
    ]
    out_shape = [
        jax.ShapeDtypeStruct((n * LANE_CHUNKS, LANES), F32),
        jax.ShapeDtypeStruct((n * LANE_CHUNKS, LANES), F32),
        jax.ShapeDtypeStruct((N_EXPERTS, n), F32),
        jax.ShapeDtypeStruct((n // SEQ, WINDOW, KV_WIDTH), F32),
        jax.ShapeDtypeStruct((n // SEQ, WINDOW, KV_WIDTH), F32),
    ]
    out_specs = [
        pl.BlockSpec((TM * LANE_CHUNKS, LANES), lambda i: (i, 0)),
        pl.BlockSpec((TM * LANE_CHUNKS, LANES), lambda i: (i, 0)),
        pl.BlockSpec((N_EXPERTS, TM), lambda i: (0, i)),
        pl.BlockSpec((1, WINDOW, KV_WIDTH), lambda i: (i // TILES_PER_SEQ, 0, 0)),
        pl.BlockSpec((1, WINDOW, KV_WIDTH), lambda i: (i // TILES_PER_SEQ, 0, 0)),
    ]
    return pl.pallas_call(
        _prompt_front_kernel,
        grid=grid,
        in_specs=in_specs,
        out_specs=out_specs,
        out_shape=out_shape,
        scratch_shapes=[pltpu.VMEM((WINDOW, KV_WIDTH), F32), pltpu.VMEM((WINDOW, KV_WIDTH), F32)],
        compiler_params=pltpu.CompilerParams(dimension_semantics=("arbitrary",), vmem_limit_bytes=VMEM_LIMIT),
        name="prompt_front",
    )(x, cos, sin, gmix, win, lng, lnb, ws, bsf, sinks, wpa, wpb, wo, gffn, wrt, br)


SAMPLE_STEP = 16
SAMPLE_STEPS = DEC_BATCH // SAMPLE_STEP


def _sample_kernel(x_ref, cos_ref, sin_ref, gmix_ref, win_ref, lng_ref, lnb_ref, wdiag_ref, bs0_ref, sinks_ref,
                   kc_ref, vc_ref, wpa_ref, wpb_ref, wo_ref, gffn_ref, wr3_ref, br_ref,
                   x1_ref, h2_ref, logits_ref, kwin_ref, vwin_ref, va_ref,
                   q_s, k_s, v_s, yain_s, ga_s, gb_s, att_s):
    i = pl.program_id(0)

    @pl.when(i == 0)
    def _():
        x = x_ref[...]
        cos = jnp.broadcast_to(cos_ref[...], (DEC_BATCH, LANES))
        sin = jnp.broadcast_to(sin_ref[...], (DEC_BATCH, LANES))
        u, va, q, k, v, gate_a, gate_b = _in_projection(x, gmix_ref, win_ref, lng_ref, lnb_ref, cos, sin)
        va_ref[...] = va
        z = wdiag_ref[...].astype(F32) * va.astype(BF16).astype(F32) + bs0_ref[...]
        yain_s[...] = u * z
        q_s[...] = q
        k_s[...] = k
        v_s[...] = v
        ga_s[...] = gate_a
        gb_s[...] = gate_b

    r0 = pl.multiple_of(i * SAMPLE_STEP, SAMPLE_STEP)
    kwin = jnp.concatenate([kc_ref[:, 1:, :], k_s[pl.ds(r0, SAMPLE_STEP), :][:, None, :]], axis=1)
    vwin = jnp.concatenate([vc_ref[:, 1:, :], v_s[pl.ds(r0, SAMPLE_STEP), :][:, None, :]], axis=1)
    kwin_ref[...] = kwin
    vwin_ref[...] = vwin

    q16 = q_s[pl.ds(r0, SAMPLE_STEP), :]
    lane = lax.broadcasted_iota(I32, (SAMPLE_STEP, LANES), 1)
    heads = []
    for hq in range(N_HEADS):
        c, p, h = hq // 2, hq % 2, hq // GQA_GROUP
        chunk = q16[:, c * LANES:(c + 1) * LANES]
        if p != h:
            chunk = pltpu.roll(chunk, HEAD_DIM, 1)
        keep = (lane < HEAD_DIM) if h == 0 else (lane >= HEAD_DIM)
        heads.append(jnp.where(keep, chunk, 0.0))
    qpad = pltpu.einshape("hbd->bhd", jnp.stack(heads, axis=0)).astype(BF16)
    s = jnp.einsum("bhd,bkd->bhk", qpad, kwin.astype(BF16), preferred_element_type=F32) * (HEAD_DIM ** -0.5)
    hid = lax.broadcasted_iota(I32, (1, N_HEADS, 1), 1)
    sink = jnp.zeros((1, N_HEADS, 1), F32)
    for hq in range(N_HEADS):
        sink = jnp.where(hid == hq, sinks_ref[hq], sink)
    m = jnp.maximum(jnp.max(s, axis=-1, keepdims=True), sink)
    e = jnp.exp(s - m)
    inv = 1.0 / (jnp.sum(e, axis=-1, keepdims=True) + jnp.exp(sink - m))
    o = jnp.einsum("bhk,bkd->bhd", (e * inv).astype(BF16), vwin.astype(BF16), preferred_element_type=F32)
    o = pltpu.einshape("bhd->hbd", o)
    chunks = []
    for c in range(N_HEADS // 2):
        parts = []
        for p in range(2):
            hq = 2 * c + p
            oh = o[hq]
            if p != hq // GQA_GROUP:
                oh = pltpu.roll(oh, HEAD_DIM, 1)
            parts.append(oh)
        chunks.append(jnp.where(lane < HEAD_DIM, parts[0], parts[1]))
    att_s[pl.ds(r0, SAMPLE_STEP), :] = jnp.concatenate(chunks, axis=1)

    @pl.when(i == SAMPLE_STEPS - 1)
    def _():
        x1, h2, logits = _merge_and_route(
            x_ref[...], yain_s[...], att_s[...], ga_s[...], gb_s[...],
            wpa_ref, wpb_ref, wo_ref, gffn_ref, wr3_ref, br_ref)
        _store_token_major(x1_ref, x1)
        _store_token_major(h2_ref, h2)
        logits_ref[...] = logits


def _sample_front(x, cos, sin, gmix, win, lng, lnb, wdiag, bs0, sinks, kc, vc, wpa, wpb, wo, gffn, wrt, br):
    n = DEC_BATCH
    cache_spec = pl.BlockSpec((SAMPLE_STEP, WINDOW, KV_WIDTH), lambda i: (i, 0, 0))
    in_specs = [
        _full((n, D_MODEL)),
        _full((1, LANES)),
        _full((1, LANES)),
        _full((1, D_MODEL)),
        _full((D_MODEL, IN_COLS)),
        _full((1, A_WIDTH)),
        _full((1, A_WIDTH)),
        _full((1, A_WIDTH)),
        _full((1, A_WIDTH)),
        pl.BlockSpec(memory_space=pltpu.SMEM),
        cache_spec,
        cache_spec,
        _full((A_WIDTH, D_MODEL)),
        _full((Q_WIDTH, D_MODEL)),
        _full((D_MODEL, D_MODEL)),
        _full((1, D_MODEL)),
        _full((3 * D_MODEL, LANES)),
        _full((N_EXPERTS, 1)),
    ]
    out_shape = [
        jax.ShapeDtypeStruct((n * LANE_CHUNKS, LANES), F32),
        jax.ShapeDtypeStruct((n * LANE_CHUNKS, LANES), F32),
        jax.ShapeDtypeStruct((N_EXPERTS, n), F32),
        jax.ShapeDtypeStruct((n, WINDOW, KV_WIDTH), F32),
        jax.ShapeDtypeStruct((n, WINDOW, KV_WIDTH), F32),
        jax.ShapeDtypeStruct((n, A_WIDTH), F32),
    ]
    out_specs = [
        _full((n * LANE_CHUNKS, LANES)),
        _full((n * LANE_CHUNKS, LANES)),
        _full((N_EXPERTS, n)),
        cache_spec,
        cache_spec,
        _full((n, A_WIDTH)),
    ]
    scratch = [
        pltpu.VMEM((n, Q_WIDTH), F32), pltpu.VMEM((n, KV_WIDTH), F32), pltpu.VMEM((n, KV_WIDTH), F32),
        pltpu.VMEM((n, A_WIDTH), F32), pltpu.VMEM((n, D_MODEL), F32), pltpu.VMEM((n, D_MODEL), F32),
        pltpu.VMEM((n, Q_WIDTH), F32),
    ]
    return pl.pallas_call(
        _sample_kernel,
        grid=(SAMPLE_STEPS,),
        in_specs=in_specs,
        out_specs=out_specs,
        out_shape=out_shape,
        scratch_shapes=scratch,
        compiler_params=pltpu.CompilerParams(dimension_semantics=("arbitrary",), vmem_limit_bytes=VMEM_LIMIT),
        name="sample_front",
    )(x, cos, sin, gmix, win, lng, lnb, wdiag, bs0, sinks, kc, vc, wpa, wpb, wo, gffn, wrt, br)


def _route_plan_kernel(lp_ref, ls_ref, dest_ref, wts_ref, off_ref):
    g = pl.program_id(0)
    topi, topw = _top4_softmax(jnp.concatenate([lp_ref[...], ls_ref[...]], axis=1))
    slot = lax.broadcasted_iota(I32, (TOP_K, GROUP_SLOTS), 1)
    eall = jnp.where(jnp.logical_or(slot < GROUP_PROMPT, g == N_GROUPS - 1), topi, N_EXPERTS)
    wts_ref[:, 0:GROUP_SLOTS] = topw
    wts_ref[:, GROUP_SLOTS:] = jnp.zeros((TOP_K, K_STRIDE - GROUP_SLOTS), F32)
    dest_ref[:, GROUP_SLOTS:] = jnp.zeros((TOP_K, K_STRIDE - GROUP_SLOTS), I32)
    eid = lax.broadcasted_iota(I32, (N_EXPERTS, GROUP_SLOTS), 0)
    onehots = [eall[k:k + 1, :] == eid for k in range(TOP_K)]
    count = jnp.zeros((N_EXPERTS, GROUP_SLOTS), F32)
    for oh in onehots:
        count = count + oh.astype(F32)
    total = jnp.broadcast_to(jnp.sum(count, axis=1, keepdims=True), (N_EXPERTS, LANES))
    nblk = jnp.floor((total + (MOE_ROWS - 1)) * (1.0 / MOE_ROWS))
    r = lax.broadcasted_iota(I32, (N_EXPERTS, N_EXPERTS), 0)
    c = lax.broadcasted_iota(I32, (N_EXPERTS, N_EXPERTS), 1)
    first_blk = lax.dot_general((c < r).astype(F32), nblk, (((1,), (0,)), ((), ())),
                                precision=lax.Precision.HIGHEST, preferred_element_type=F32)
    start = (first_blk + 1.0) * MOE_ROWS
    lane = lax.broadcasted_iota(I32, (N_EXPERTS, LANES), 1)
    info = jnp.where(lane == 0, start, jnp.where(lane == 1, start + total, jnp.where(lane == 2, nblk, first_blk)))
    off_ref[...] = info.astype(I32)
    ti = lax.broadcasted_iota(I32, (LANES, LANES), 0)
    tj = lax.broadcasted_iota(I32, (LANES, LANES), 1)
    before = (ti < tj).astype(BF16)
    ones = jnp.ones((LANES, LANES), BF16)
    running = start
    for t in range(SLOT_TILES):
        sl = slice(t * LANES, (t + 1) * LANES)
        cb = count[:, sl].astype(BF16)
        pos = running + _bdot(cb, before)
        rows = [jnp.sum(jnp.where(oh[:, sl], pos, 0.0), axis=0, keepdims=True) for oh in onehots]
        dest_ref[:, sl] = jnp.concatenate(rows, axis=0).astype(I32)
        running = running + _bdot(cb, ones)


def _route_plan(logits_p, logits_s):
    in_specs = [
        pl.BlockSpec((N_EXPERTS, GROUP_PROMPT), lambda g: (0, g)),
        pl.BlockSpec((N_EXPERTS, DEC_BATCH), lambda g: (0, 0)),
    ]
    out_shape = [
        jax.ShapeDtypeStruct((N_GROUPS, TOP_K, K_STRIDE), I32),
        jax.ShapeDtypeStruct((N_GROUPS, TOP_K, K_STRIDE), F32),
        jax.ShapeDtypeStruct((N_GROUPS, N_EXPERTS, LANES), I32),
    ]
    out_specs = [
        pl.BlockSpec((None, TOP_K, K_STRIDE), lambda g: (g, 0, 0)),
        pl.BlockSpec((None, TOP_K, K_STRIDE), lambda g: (g, 0, 0)),
        pl.BlockSpec((None, N_EXPERTS, LANES), lambda g: (g, 0, 0)),
    ]
    return pl.pallas_call(
        _route_plan_kernel,
        grid=(N_GROUPS,),
        in_specs=in_specs,
        out_specs=out_specs,
        out_shape=out_shape,
        compiler_params=pltpu.CompilerParams(dimension_semantics=("arbitrary",)),
        name="route_plan",
    )(logits_p, logits_s)


GROUP_ROWS = GROUP_PROMPT * LANE_CHUNKS
SAMPLE_ROWS = DEC_BATCH * LANE_CHUNKS
TRASH_SLOT = GROUP_SLOTS
BUF_ROWS = (GROUP_SLOTS + 1) * LANE_CHUNKS
SCATTER_BATCH = 8


def _moe_kernel(off_ref, desth_ref, wtsh_ref, h2p_ref, h2s_ref, x1p_ref, x1s_ref, wgu_ref, bgu_ref, wdn_ref, bdn_ref,
                x2p_ref, x2s_ref,
                h2buf, acc, wgubuf, bgubuf, wdnbuf, bdnbuf, xs0, xs1, ys0, ys1,
                dest_ref, wts_ref, src_ref, seg_expert, seg_first, blk_seg, act_sem, w_sem):
    g = pl.program_id(0)
    last = g == N_GROUPS - 1
    row0 = pl.multiple_of(g * GROUP_ROWS, GROUP_ROWS)

    def prompt_copies():
        return (pltpu.make_async_copy(h2p_ref.at[pl.ds(row0, GROUP_ROWS)], h2buf.at[pl.ds(0, GROUP_ROWS)], act_sem.at[0]),
                pltpu.make_async_copy(x1p_ref.at[pl.ds(row0, GROUP_ROWS)], acc.at[pl.ds(0, GROUP_ROWS)], act_sem.at[1]))

    def sample_copies():
        return (pltpu.make_async_copy(h2s_ref, h2buf.at[pl.ds(GROUP_ROWS, SAMPLE_ROWS)], act_sem.at[2]),
                pltpu.make_async_copy(x1s_ref, acc.at[pl.ds(GROUP_ROWS, SAMPLE_ROWS)], act_sem.at[3]))

    def weight_copies(e, slot):
        return (pltpu.make_async_copy(wgu_ref.at[e], wgubuf.at[slot], w_sem.at[0, slot]),
                pltpu.make_async_copy(bgu_ref.at[e], bgubuf.at[slot], w_sem.at[1, slot]),
                pltpu.make_async_copy(wdn_ref.at[e], wdnbuf.at[slot], w_sem.at[2, slot]),
                pltpu.make_async_copy(bdn_ref.at[e], bdnbuf.at[slot], w_sem.at[3, slot]))

    tab0 = pl.multiple_of(g * (TOP_K * K_STRIDE), TOP_K * K_STRIDE)
    table_copies = (
        pltpu.make_async_copy(desth_ref.at[pl.ds(tab0, TOP_K * K_STRIDE)], dest_ref, act_sem.at[4]),
        pltpu.make_async_copy(wtsh_ref.at[pl.ds(tab0, TOP_K * K_STRIDE)], wts_ref, act_sem.at[5]))
    for cp in table_copies:
        cp.start()

    for cp in prompt_copies():
        cp.start()

    @pl.when(last)
    def _():
        for cp in sample_copies():
            cp.start()

    trash = pl.ds(TRASH_SLOT * LANE_CHUNKS, LANE_CHUNKS)
    h2buf[trash, :] = jnp.zeros((LANE_CHUNKS, LANES), F32)
    acc[trash, :] = jnp.zeros((LANE_CHUNKS, LANES), F32)
    ys1[...] = jnp.zeros_like(ys1)

    def pad_codes(lo, hi):
        def body(p, carry):
            src_ref[p] = TRASH_SLOT
            return carry
        lax.fori_loop(lo, hi, body, 0)

    def scan_expert(e, carry):
        nseg, nblocks = carry
        nblk = off_ref[e, 2]
        first = off_ref[e, 3]

        @pl.when(nblk > 0)
        def _():
            seg_expert[nseg] = e
            seg_first[nseg] = first
            pad_codes(off_ref[e, 1], off_ref[e, 0] + nblk * MOE_ROWS)

            def mark(b, c2):
                blk_seg[first + b] = nseg
                return c2
            lax.fori_loop(0, nblk, mark, 0)

        return nseg + jnp.where(nblk > 0, 1, 0), nblocks + nblk

    nseg, nblocks = lax.fori_loop(0, N_EXPERTS, scan_expert, (jnp.int32(0), jnp.int32(0)))
    pad_codes(0, MOE_ROWS)
    pad_codes((nblocks + 1) * MOE_ROWS, (nblocks + 3) * MOE_ROWS)
    blk_seg[nblocks] = nseg - 1
    blk_seg[nblocks + 1] = nseg - 1

    for cp in weight_copies(seg_expert[0], 0):
        cp.start()

    for cp in table_copies:
        cp.wait()

    nvalid = jnp.where(last, GROUP_SLOTS, GROUP_PROMPT)
    for k in range(TOP_K):
        def fill(j, carry, k=k):
            c0 = k * K_STRIDE + j * SUBLANES
            for d in range(SUBLANES):
                src_ref[dest_ref[c0 + d]] = c0 + d
            return carry
        lax.fori_loop(0, nvalid // SUBLANES, fill, 0)

    for cp in prompt_copies():
        cp.wait()

    @pl.when(last)
    def _():
        for cp in sample_copies():
            cp.wait()

    def token_rows(code):
        slot_id = code & (K_STRIDE - 1)
        return pl.ds(pl.multiple_of(slot_id * LANE_CHUNKS, LANE_CHUNKS), LANE_CHUNKS)

    def gather(b, xs):
        base = (b + 1) * MOE_ROWS
        for m in range(MOE_ROWS):
            xs[pl.ds(m, LANE_CHUNKS, stride=XS_STRIDE), :] = h2buf[token_rows(src_ref[base + m]), :]

    def scatter_add(b, ys):
        base = (b + 1) * MOE_ROWS
        for m0 in range(0, MOE_ROWS, SCATTER_BATCH):
            pending = []
            for m in range(m0, m0 + SCATTER_BATCH):
                code = src_ref[base + m]
                rows = token_rows(code)
                pending.append((rows, acc[rows, :] + wts_ref[code] * ys[pl.ds(m, LANE_CHUNKS, stride=XS_STRIDE), :]))
            for rows, val in pending:
                acc[rows, :] = val

    def expert_ffn(xs, ys, slot):
        x = jnp.concatenate(
            [xs[c * XS_STRIDE:c * XS_STRIDE + MOE_ROWS, :] for c in range(LANE_CHUNKS)], axis=1).astype(BF16)
        gu = _bdot(x, wgubuf[slot]) + bgubuf[slot]
        gl = jnp.minimum(gu[:, :D_EXPERT], SWIGLU_LIMIT)
        ul = jnp.clip(gu[:, D_EXPERT:], -SWIGLU_LIMIT, SWIGLU_LIMIT)
        a = (ul + 1.0) * (gl * jax.nn.sigmoid(SWIGLU_ALPHA * gl))
        y = _bdot(a.astype(BF16), wdnbuf[slot]) + bdnbuf[slot]
        for c in range(LANE_CHUNKS):
            ys[c * XS_STRIDE:c * XS_STRIDE + MOE_ROWS, :] = y[:, c * LANES:(c + 1) * LANES]

    def step(b, xs_cur, xs_next, ys_cur, ys_prev):
        seg = blk_seg[b]
        slot = seg & 1

        @pl.when(jnp.logical_and(b == seg_first[seg], b < nblocks))
        def _():
            for cp in weight_copies(seg_expert[seg], slot):
                cp.wait()

            @pl.when(seg + 1 < nseg)
            def _():
                for cp in weight_copies(seg_expert[seg + 1], 1 - slot):
                    cp.start()

        gather(b + 1, xs_next)
        expert_ffn(xs_cur, ys_cur, slot)
        scatter_add(b - 1, ys_prev)

    gather(0, xs0)
    npairs = (nblocks + 1) // 2

    def pair(t, carry):
        step(2 * t, xs0, xs1, ys0, ys1)
        step(2 * t + 1, xs1, xs0, ys1, ys0)
        return carry

    lax.fori_loop(0, npairs, pair, 0)
    scatter_add(2 * npairs - 1, ys1)

    out_p =pltpu.make_async_copy(acc.at[pl.ds(0, GROUP_ROWS)], x2p_ref.at[pl.ds(row0, GROUP_ROWS)], act_sem.at[0])
    out_p.start()

    @pl.when(last)
    def _():
        out_s = pltpu.make_async_copy(acc.at[pl.ds(GROUP_ROWS, SAMPLE_ROWS)], x2s_ref, act_sem.at[2])
        out_s.start()
        out_s.wait()

    out_p.wait()


def _moe(dest, wts, off, h2p, h2s, x1p, x1s, wgu, bgu, wdn, bdn):
    anyspec = pl.BlockSpec(memory_space=pl.ANY)
    dest = dest.reshape(N_GROUPS * TOP_K * K_STRIDE)
    wts = wts.reshape(N_GROUPS * TOP_K * K_STRIDE)
    in_specs = [
        pl.BlockSpec((None, N_EXPERTS, LANES), lambda g: (g, 0, 0), memory_space=pltpu.SMEM),
        anyspec, anyspec, anyspec, anyspec, anyspec, anyspec, anyspec, anyspec, anyspec, anyspec,
    ]
    scratch = [
        pltpu.VMEM((BUF_ROWS, LANES), F32),
        pltpu.VMEM((BUF_ROWS, LANES), F32),
        pltpu.VMEM((2, D_MODEL, 2 * D_EXPERT), BF16),
        pltpu.VMEM((2, 1, 2 * D_EXPERT), F32),
        pltpu.VMEM((2, D_EXPERT, D_MODEL), BF16),
        pltpu.VMEM((2, 1, D_MODEL), F32),
        pltpu.VMEM((LANE_CHUNKS * XS_STRIDE, LANES), F32),
        pltpu.VMEM((LANE_CHUNKS * XS_STRIDE, LANES), F32),
        pltpu.VMEM((LANE_CHUNKS * XS_STRIDE, LANES), F32),
        pltpu.VMEM((LANE_CHUNKS * XS_STRIDE, LANES), F32),
        pltpu.SMEM((TOP_K * K_STRIDE,), I32),
        pltpu.SMEM((TOP_K * K_STRIDE,), F32),
        pltpu.SMEM((POS_TABLE,), I32),
        pltpu.SMEM((N_EXPERTS,), I32),
        pltpu.SMEM((N_EXPERTS,), I32),
        pltpu.SMEM((LANES,), I32),
        pltpu.SemaphoreType.DMA((6,)),
        pltpu.SemaphoreType.DMA((4, 2)),
    ]
    return pl.pallas_call(
        _moe_kernel,
        grid=(N_GROUPS,),
        in_specs=in_specs,
        out_specs=[anyspec, anyspec],
        out_shape=[jax.ShapeDtypeStruct(x1p.shape, F32), jax.ShapeDtypeStruct(x1s.shape, F32)],
        scratch_shapes=scratch,
        compiler_params=pltpu.CompilerParams(dimension_semantics=("arbitrary",), vmem_limit_bytes=VMEM_LIMIT),
        name="moe",
    )(off, dest, wts, h2p, h2s, x1p, x1s, wgu, bgu, wdn, bdn)


def _ple_final_kernel(x2_ref, ple_ref, wple_ref, gple_ref, wpg_ref, gfin_ref, y_ref):
    rows = y_ref.shape[0]
    x2 = _load_token_major(x2_ref, rows)
    e = _rmsnorm(_bdot(ple_ref[...].astype(BF16), wple_ref[...]), gple_ref[...])
    x3 = x2 + jax.nn.sigmoid(_bdot(x2.astype(BF16), wpg_ref[...])) * e
    y_ref[...] = _rmsnorm(x3, gfin_ref[...])


def _ple_final(x2_tm, ple, wple, gple, wpg, gfin, tile):
    n = ple.shape[0]
    return pl.pallas_call(
        _ple_final_kernel,
        grid=(n // tile,),
        in_specs=[
            pl.BlockSpec((tile * LANE_CHUNKS, LANES), lambda i: (i, 0)),
            pl.BlockSpec((tile, PLE_DIM), lambda i: (i, 0)),
            _full((PLE_DIM, D_MODEL)),
            _full((1, D_MODEL)),
            _full((D_MODEL, D_MODEL)),
            _full((1, D_MODEL)),
        ],
        out_specs=pl.BlockSpec((tile, D_MODEL), lambda i: (i, 0)),
        out_shape=jax.ShapeDtypeStruct((n, D_MODEL), F32),
        compiler_params=pltpu.CompilerParams(dimension_semantics=("arbitrary",), vmem_limit_bytes=VMEM_LIMIT),
        name="ple_final",
    )(x2_tm, ple, wple, gple, wpg, gfin)


def _rope_tables(pos):
    half = HEAD_DIM // 2
    inv = ROPE_THETA ** (-jnp.arange(half, dtype=F32) / half)
    ang = pos.astype(F32)[:, None] * inv[None, :]
    cos, sin = jnp.cos(ang), jnp.sin(ang)
    cos2 = jnp.concatenate([cos, cos, cos, cos], axis=1)
    sin2 = jnp.concatenate([-sin, sin, -sin, sin], axis=1)
    return cos2, sin2


def _router_passes(w_router):
    hi = w_router.astype(BF16)
    lo = (w_router - hi.astype(F32)).astype(BF16)
    w3 = jnp.concatenate([hi, lo, hi], axis=0)
    return jnp.pad(w3, ((0, 0), (0, LANES - N_EXPERTS)))


def _prep_weights(g_mix, w_in, a_ln_g, a_ln_b, a_ws, a_bs, w_pa, w_pb, w_o, g_ffn, w_router, b_router):
    causal = jnp.tril(jnp.ones((CHUNK, CHUNK), dtype=bool))
    return dict(
        gmix=g_mix.reshape(1, D_MODEL),
        win=w_in.astype(BF16),
        lng=a_ln_g.reshape(1, A_WIDTH),
        lnb=a_ln_b.reshape(1, A_WIDTH),
        ws=jnp.where(causal[None], a_ws, 0.0).astype(BF16),
        bsf=jnp.repeat(jnp.transpose(a_bs), A_GROUP_DIM, axis=1),
        wpa=w_pa.astype(BF16),
        wpb=w_pb.astype(BF16),
        wo=w_o.astype(BF16),
        gffn=g_ffn.reshape(1, D_MODEL),
        wrt=_router_passes(w_router),
        br=b_router.reshape(N_EXPERTS, 1),
    )


def kernel(x_prompt, x_sample, cache_win_k, cache_win_v, p_prompt, p_sample, g_mix, w_in, a_ln_g, a_ln_b, a_ws, a_bs, sinks, w_pa, w_pb, w_o, g_ffn, w_router, b_router, w_gu, b_gu, w_down, b_down, w_ple, g_ple, w_ple_gate, g_final):
    W = _prep_weights(g_mix[0], w_in[0], a_ln_g[0], a_ln_b[0], a_ws[0], a_bs[0], w_pa[0], w_pb[0], w_o[0],
                      g_ffn[0], w_router[0], b_router[0])
    cos_p, sin_p = _rope_tables(jnp.arange(SEQ, dtype=I32))
    cos_s, sin_s = _rope_tables(jnp.full((1,), PAST_LEN, I32))
    x1p, h2p, logits_p, kwin_p, vwin_p = _prompt_front(
        x_prompt.reshape(N_PROMPT, D_MODEL), cos_p, sin_p, W["gmix"], W["win"], W["lng"], W["lnb"],
        W["ws"], W["bsf"], sinks[0], W["wpa"], W["wpb"], W["wo"], W["gffn"], W["wrt"], W["br"])

    wdiag = jnp.repeat(a_ws[0, :, 0, 0], A_GROUP_DIM)[None, :].astype(BF16)
    bs0 = jnp.repeat(a_bs[0, :, 0], A_GROUP_DIM)[None, :]
    x1s, h2s, logits_s, kwin_s, vwin_s, va_s = _sample_front(
        x_sample.reshape(DEC_BATCH, D_MODEL), cos_s, sin_s, W["gmix"], W["win"], W["lng"], W["lnb"], wdiag, bs0,
        sinks[0], cache_win_k[0].reshape(DEC_BATCH, WINDOW, KV_WIDTH), cache_win_v[0].reshape(DEC_BATCH, WINDOW, KV_WIDTH),
        W["wpa"], W["wpb"], W["wo"], W["gffn"], W["wrt"], W["br"])

    dest, wts, off = _route_plan(logits_p, logits_s)
    x2p, x2s = _moe(dest, wts, off, h2p, h2s, x1p, x1s,
                    w_gu[0].astype(BF16), b_gu[0].reshape(N_EXPERTS, 1, 2 * D_EXPERT),
                    w_down[0].astype(BF16), b_down[0].reshape(N_EXPERTS, 1, D_MODEL))

    wple = w_ple[0].astype(BF16)
    gple = g_ple[0].reshape(1, D_MODEL)
    wpg = w_ple_gate[0].astype(BF16)
    gfin = g_final.reshape(1, D_MODEL)
    y_p = _ple_final(x2p, p_prompt[0].reshape(N_PROMPT, PLE_DIM), wple, gple, wpg, gfin, TM)
    y_s = _ple_final(x2s, p_sample[0].reshape(DEC_BATCH, PLE_DIM), wple, gple, wpg, gfin, DEC_BATCH)

    return (
        y_p.reshape(BATCH, SEQ, D_MODEL),
        y_s.reshape(DEC_BATCH, 1, D_MODEL),
        kwin_p.reshape(1, BATCH, WINDOW, N_KV_HEADS, HEAD_DIM),
        vwin_p.reshape(1, BATCH, WINDOW, N_KV_HEADS, HEAD_DIM),
        kwin_s.reshape(1, DEC_BATCH, WINDOW, N_KV_HEADS, HEAD_DIM),
        vwin_s.reshape(1, DEC_BATCH, WINDOW, N_KV_HEADS, HEAD_DIM),
        va_s.reshape(1, DEC_BATCH, 1, A_WIDTH),
    )
```

```python
import functools

import jax
import jax.numpy as jnp
from jax import lax
from jax.experimental import pallas as pl
from jax.experimental.pallas import tpu as pltpu

F32 = jnp.float32
BF16 = jnp.bfloat16
I32 = jnp.int32

D_MODEL = 1024
BATCH = 4
SEQ = 4096
DEC_BATCH = 128
PAST_LEN = 8192
CHUNK = 128
A_GROUPS = 4
A_GROUP_DIM = 128
A_WIDTH = A_GROUPS * A_GROUP_DIM
N_HEADS = 8
N_KV_HEADS = 2
HEAD_DIM = 64
Q_WIDTH = N_HEADS * HEAD_DIM
KV_WIDTH = N_KV_HEADS * HEAD_DIM
GQA_GROUP = N_HEADS // N_KV_HEADS
WINDOW = 128
ROPE_THETA = 10000.0
N_EXPERTS = 32
TOP_K = 4
D_EXPERT = D_MODEL
SWIGLU_ALPHA = 1.702
SWIGLU_LIMIT = 7.0
PLE_DIM = 256
RMS_EPS = 1e-5
LN_EPS = 1e-5

O_Q = 2 * A_WIDTH
O_K = O_Q + Q_WIDTH
O_V = O_K + KV_WIDTH
O_GA = O_V + KV_WIDTH
O_GB = O_GA + D_MODEL
IN_COLS = O_GB + D_MODEL

LANES = 128
SUBLANES = 8
LANE_CHUNKS = D_MODEL // LANES
VMEM_LIMIT = 56 * 1024 * 1024

N_PROMPT = BATCH * SEQ
TM = 256
TILES_PER_SEQ = SEQ // TM
BLOCKS_PER_TILE = TM // WINDOW
FRONT_STEPS = N_PROMPT // TM
CAST_ROWS = N_EXPERTS * D_MODEL // FRONT_STEPS

N_GROUPS = 4
GROUP_PROMPT = N_PROMPT // N_GROUPS
GROUP_SLOTS = GROUP_PROMPT + DEC_BATCH
GROUP_ASSIGN = GROUP_SLOTS * TOP_K
SLOT_TILES = GROUP_SLOTS // LANES
MOE_ROWS = 256
XS_STRIDE = MOE_ROWS + SUBLANES
SLOT_BITS = 13
K_STRIDE = 1 << SLOT_BITS
assert GROUP_SLOTS < K_STRIDE
MAX_BLOCKS = GROUP_ASSIGN // MOE_ROWS + N_EXPERTS
POS_TABLE = 1 << 15
assert (MAX_BLOCKS + 3) * MOE_ROWS <= POS_TABLE
assert MAX_BLOCKS + 2 <= LANES


def _bdot(a, b):
    return jnp.dot(a, b, preferred_element_type=F32)


def _rmsnorm(x, g):
    return x * lax.rsqrt(jnp.mean(x * x, axis=-1, keepdims=True) + RMS_EPS) * g


def _gelu(x):
    return 0.5 * x * (1.0 + lax.erf(x * (0.5 ** 0.5)))


def _group_layernorm(v, g, b):
    cols = []
    for gi in range(A_GROUPS):
        s = slice(gi * A_GROUP_DIM, (gi + 1) * A_GROUP_DIM)
        vg = v[:, s]
        mu = jnp.mean(vg, axis=-1, keepdims=True)
        d = vg - mu
        var = jnp.mean(d * d, axis=-1, keepdims=True)
        cols.append(d * lax.rsqrt(var + LN_EPS) * g[:, s] + b[:, s])
    return jnp.concatenate(cols, axis=1)


def _rope(x, cos, sin_signed):
    width = x.shape[1]
    reps = width // LANES
    cosf = jnp.concatenate([cos] * reps, axis=1) if reps > 1 else cos
    sinf = jnp.concatenate([sin_signed] * reps, axis=1) if reps > 1 else sin_signed
    half = HEAD_DIM // 2
    lane = lax.broadcasted_iota(I32, x.shape, 1)
    up = pltpu.roll(x, width - half, 1)
    down = pltpu.roll(x, half, 1)
    partner = jnp.where((lane & (HEAD_DIM - 1)) < half, up, down)
    return x * cosf + partner * sinf


def _in_projection(x, gmix_ref, win_ref, lng_ref, lnb_ref, cos, sin_signed):
    hb = _rmsnorm(x, gmix_ref[...]).astype(BF16)
    zuv = _gelu(_bdot(hb, win_ref[:, 0:O_Q]))
    u = zuv[:, :A_WIDTH]
    va = _group_layernorm(zuv[:, A_WIDTH:], lng_ref[...], lnb_ref[...])
    zqkv = _bdot(hb, win_ref[:, O_Q:O_GA])
    q = _rope(zqkv[:, :Q_WIDTH], cos, sin_signed)
    k = _rope(zqkv[:, Q_WIDTH:Q_WIDTH + KV_WIDTH], cos, sin_signed)
    v = zqkv[:, Q_WIDTH + KV_WIDTH:]
    zg = _bdot(hb, win_ref[:, O_GA:IN_COLS])
    gate_a = jax.nn.sigmoid(zg[:, :D_MODEL])
    gate_b = jax.nn.sigmoid(zg[:, D_MODEL:])
    return u, va, q, k, v, gate_a, gate_b


def _merge_and_route(x, ya_in, att, gate_a, gate_b, wpa_ref, wpb_ref, wo_ref, gffn_ref, wr3_ref, br_ref):
    ya = _bdot(ya_in.astype(BF16), wpa_ref[...])
    yb = _bdot(att.astype(BF16), wpb_ref[...])
    mix = (gate_a * ya + gate_b * yb).astype(BF16)
    x1 = x + _bdot(mix, wo_ref[...])
    h2 = _rmsnorm(x1, gffn_ref[...])
    hi = h2.astype(BF16)
    lo = (h2 - hi.astype(F32)).astype(BF16)
    logits = _bdot(jnp.concatenate([hi, hi, lo], axis=1), wr3_ref[...])
    return x1, h2, jnp.transpose(logits)[:N_EXPERTS, :] + br_ref[...]


def _top4_softmax(logits):
    eid = lax.broadcasted_iota(I32, logits.shape, 0)
    vals, idxs = [], []
    for _ in range(TOP_K):
        m = jnp.max(logits, axis=0, keepdims=True)
        idx = jnp.min(jnp.where(logits == m, eid, N_EXPERTS), axis=0, keepdims=True)
        logits = jnp.where(eid == idx, -jnp.inf, logits)
        vals.append(m)
        idxs.append(idx)
    es = [jnp.exp(v - vals[0]) for v in vals]
    inv = 1.0 / (es[0] + es[1] + es[2] + es[3])
    return jnp.concatenate(idxs, axis=0), jnp.concatenate([e * inv for e in es], axis=0)


def _store_token_major(ref, val):
    rows = val.shape[0]
    for c in range(LANE_CHUNKS):
        ref[pl.ds(c, rows, stride=LANE_CHUNKS), :] = val[:, c * LANES:(c + 1) * LANES]


def _load_token_major(ref, rows):
    return jnp.concatenate([ref[pl.ds(c, rows, stride=LANE_CHUNKS), :] for c in range(LANE_CHUNKS)], axis=1)


def _band_attention(q, k, v, k_prev, v_prev, sinks_ref, seq_start):
    unit_rows = GQA_GROUP * WINDOW
    kb = jnp.concatenate([k_prev, k], axis=0).astype(BF16)
    vb = jnp.concatenate([v_prev, v], axis=0).astype(BF16)
    qb = q.astype(BF16)
    units = [(b, h) for b in range(BLOCKS_PER_TILE) for h in range(N_KV_HEADS)]
    scores = []
    for b, h in units:
        qh = jnp.concatenate(
            [qb[b * WINDOW:(b + 1) * WINDOW, (h * GQA_GROUP + j) * HEAD_DIM:(h * GQA_GROUP + j + 1) * HEAD_DIM]
             for j in range(GQA_GROUP)], axis=0)
        kh = kb[b * WINDOW:(b + 2) * WINDOW, h * HEAD_DIM:(h + 1) * HEAD_DIM]
        scores.append(lax.dot_general(qh, kh, (((1,), (1,)), ((), ())), preferred_element_type=F32))
    s = jnp.concatenate(scores, axis=0) * (HEAD_DIM ** -0.5)
    shape = s.shape
    row = lax.broadcasted_iota(I32, shape, 0)
    qi = row & (WINDOW - 1)
    kj = lax.broadcasted_iota(I32, shape, 1)
    lo = jnp.where(jnp.logical_and(seq_start, row < N_KV_HEADS * unit_rows), WINDOW, 0)
    valid = (kj > qi) & (kj <= qi + WINDOW) & (kj >= lo)
    s = jnp.where(valid, s, -jnp.inf)
    sink = jnp.concatenate(
        [jnp.full((WINDOW, 1), sinks_ref[h * GQA_GROUP + j], F32) for b, h in units for j in range(GQA_GROUP)], axis=0)
    m = jnp.maximum(jnp.max(s, axis=-1, keepdims=True), sink)
    e = jnp.exp(s - m)
    inv = 1.0 / (jnp.sum(e, axis=-1, keepdims=True) + jnp.exp(sink - m))
    p = (e * inv).astype(BF16)
    att_rows = []
    for b in range(BLOCKS_PER_TILE):
        heads = []
        for h in range(N_KV_HEADS):
            u = b * N_KV_HEADS + h
            vh = vb[b * WINDOW:(b + 2) * WINDOW, h * HEAD_DIM:(h + 1) * HEAD_DIM]
            o = _bdot(p[u * unit_rows:(u + 1) * unit_rows], vh)
            heads.extend(o[j * WINDOW:(j + 1) * WINDOW, :] for j in range(GQA_GROUP))
        att_rows.append(jnp.concatenate(heads, axis=1))
    return jnp.concatenate(att_rows, axis=0)


def _prompt_front_kernel(x_ref, cos_ref, sin_ref, gmix_ref, win_ref, lng_ref, lnb_ref, ws_ref, bsf_ref,
                         sinks_ref, wpa_ref, wpb_ref, wo_ref, gffn_ref, wr3_ref, br_ref, wgu32_ref, wdn32_ref,
                         x1_ref, h2_ref, logits_ref, kwin_ref, vwin_ref, wgu16_ref, wdn16_ref,
                         kprev_ref, vprev_ref, gu_in, dn_in, gu_out, dn_out, cast_sem):
    i = pl.program_id(0)
    seq_start = (i % TILES_PER_SEQ) == 0

    def cast_in(c, slot):
        r = pl.ds(pl.multiple_of(c * CAST_ROWS, CAST_ROWS), CAST_ROWS)
        return (pltpu.make_async_copy(wgu32_ref.at[r], gu_in.at[slot], cast_sem.at[0, slot]),
                pltpu.make_async_copy(wdn32_ref.at[r], dn_in.at[slot], cast_sem.at[1, slot]))

    def cast_out(c, slot):
        r = pl.ds(pl.multiple_of(c * CAST_ROWS, CAST_ROWS), CAST_ROWS)
        return (pltpu.make_async_copy(gu_out.at[slot], wgu16_ref.at[r], cast_sem.at[2, slot]),
                pltpu.make_async_copy(dn_out.at[slot], wdn16_ref.at[r], cast_sem.at[3, slot]))

    slot = i & 1

    @pl.when(i == 0)
    def _():
        for cp in cast_in(0, 0):
            cp.start()

    @pl.when(i + 1 < FRONT_STEPS)
    def _():
        for cp in cast_in(i + 1, 1 - slot):
            cp.start()

    for cp in cast_in(i, slot):
        cp.wait()

    @pl.when(i >= 2)
    def _():
        for cp in cast_out(i - 2, slot):
            cp.wait()

    gu_out[slot] = gu_in[slot].astype(BF16)
    dn_out[slot] = dn_in[slot].astype(BF16)
    for cp in cast_out(i, slot):
        cp.start()

    @pl.when(seq_start)
    def _():
        kprev_ref[...] = jnp.zeros_like(kprev_ref)
        vprev_ref[...] = jnp.zeros_like(vprev_ref)

    x = x_ref[...]
    u, va, q, k, v, gate_a, gate_b = _in_projection(
        x, gmix_ref, win_ref, lng_ref, lnb_ref, cos_ref[...], sin_ref[...])

    att = _band_attention(q, k, v, kprev_ref[...], vprev_ref[...], sinks_ref, seq_start)
    k_last, v_last = k[TM - WINDOW:], v[TM - WINDOW:]
    kprev_ref[...] = k_last
    vprev_ref[...] = v_last
    kwin_ref[0] = k_last
    vwin_ref[0] = v_last

    vab = va.astype(BF16)
    zc = jnp.concatenate(
        [jnp.concatenate(
            [_bdot(ws_ref[g], vab[b * CHUNK:(b + 1) * CHUNK, g * A_GROUP_DIM:(g + 1) * A_GROUP_DIM])
             for g in range(A_GROUPS)], axis=1) + bsf_ref[...]
         for b in range(BLOCKS_PER_TILE)], axis=0)

    x1, h2, logits = _merge_and_route(x, u * zc, att, gate_a, gate_b,
                                      wpa_ref, wpb_ref, wo_ref, gffn_ref, wr3_ref, br_ref)
    _store_token_major(x1_ref, x1)
    _store_token_major(h2_ref, h2)
    logits_ref[...] = logits

    @pl.when(i == FRONT_STEPS - 1)
    def _():
        for cp in cast_out(i - 1, 1 - slot) + cast_out(i, slot):
            cp.wait()


def _full(shape):
    return pl.BlockSpec(shape, lambda i: (0,) * len(shape))


def _prompt_front(x, cos, sin, gmix, win, lng, lnb, ws, bsf, sinks, wpa, wpb, wo, gffn, wrt, br, wgu32, wdn32):
    n = x.shape[0]
    assert n == N_PROMPT
    grid = (FRONT_STEPS,)
    anyspec = pl.BlockSpec(memory_space=pl.ANY)
    in_specs = [
        pl.BlockSpec((TM, D_MODEL), lambda i: (i, 0)),
        pl.BlockSpec((TM, LANES), lambda i: (i % TILES_PER_SEQ, 0)),
        pl.BlockSpec((TM, LANES), lambda i: (i % TILES_PER_SEQ, 0)),
        _full((1, D_MODEL)),
        _full((D_MODEL, IN_COLS)),
        _full((1, A_WIDTH)),
        _full((1, A_WIDTH)),
        _full((A_GROUPS, CHUNK, CHUNK)),
        _full((CHUNK, A_WIDTH)),
        pl.BlockSpec(memory_space=pltpu.SMEM),
        _full((A_WIDTH, D_MODEL)),
        _full((Q_WIDTH, D_MODEL)),
        _full((D_MODEL, D_MODEL)),
        _full((1, D_MODEL)),
        _full((3 * D_MODEL, LANES)),
        _full((N_EXPERTS, 1)),
        anyspec,
        anyspec,
    ]
    out_shape = [
        jax.ShapeDtypeStruct((n * LANE_CHUNKS, LANES), F32),
        jax.ShapeDtypeStruct((n * LANE_CHUNKS, LANES), F32),
        jax.ShapeDtypeStruct((N_EXPERTS, n), F32),
        jax.ShapeDtypeStruct((n // SEQ, WINDOW, KV_WIDTH), F32),
        jax.ShapeDtypeStruct((n // SEQ, WINDOW, KV_WIDTH), F32),
        jax.ShapeDtypeStruct(wgu32.shape, BF16),
        jax.ShapeDtypeStruct(wdn32.shape, BF16),
    ]
    out_specs = [
        pl.BlockSpec((TM * LANE_CHUNKS, LANES), lambda i: (i, 0)),
        pl.BlockSpec((TM * LANE_CHUNKS, LANES), lambda i: (i, 0)),
        pl.BlockSpec((N_EXPERTS, TM), lambda i: (0, i)),
        pl.BlockSpec((1, WINDOW, KV_WIDTH), lambda i: (i // TILES_PER_SEQ, 0, 0)),
        pl.BlockSpec((1, WINDOW, KV_WIDTH), lambda i: (i // TILES_PER_SEQ, 0, 0)),
        anyspec,
        anyspec,
    ]
    scratch = [
        pltpu.VMEM((WINDOW, KV_WIDTH), F32),
        pltpu.VMEM((WINDOW, KV_WIDTH), F32),
        pltpu.VMEM((2, CAST_ROWS, 2 * D_EXPERT), F32),
        pltpu.VMEM((2, CAST_ROWS, D_MODEL), F32),
        pltpu.VMEM((2, CAST_ROWS, 2 * D_EXPERT), BF16),
        pltpu.VMEM((2, CAST_ROWS, D_MODEL), BF16),
        pltpu.SemaphoreType.DMA((4, 2)),
    ]
    return pl.pallas_call(
        _prompt_front_kernel,
        grid=grid,
        in_specs=in_specs,
        out_specs=out_specs,
        out_shape=out_shape,
        scratch_shapes=scratch,
        compiler_params=pltpu.CompilerParams(dimension_semantics=("arbitrary",), vmem_limit_bytes=VMEM_LIMIT),
        name="prompt_front",
    )(x, cos, sin, gmix, win, lng, lnb, ws, bsf, sinks, wpa, wpb, wo, gffn, wrt, br, wgu32, wdn32)


SAMPLE_STEP = 16
SAMPLE_STEPS = DEC_BATCH // SAMPLE_STEP


def _sample_kernel(x_ref, cos_ref, sin_ref, gmix_ref, win_ref, lng_ref, lnb_ref, wdiag_ref, bs0_ref, sinks_ref,
                   kc_ref, vc_ref, wpa_ref, wpb_ref, wo_ref, gffn_ref, wr3_ref, br_ref,
                   x1_ref, h2_ref, logits_ref, kwin_ref, vwin_ref, va_ref,
                   q_s, k_s, v_s, yain_s, ga_s, gb_s, att_s):
    i = pl.program_id(0)

    @pl.when(i == 0)
    def _():
        x = x_ref[...]
        cos = jnp.broadcast_to(cos_ref[...], (DEC_BATCH, LANES))
        sin = jnp.broadcast_to(sin_ref[...], (DEC_BATCH, LANES))
        u, va, q, k, v, gate_a, gate_b = _in_projection(x, gmix_ref, win_ref, lng_ref, lnb_ref, cos, sin)
        va_ref[...] = va
        z = wdiag_ref[...].astype(F32) * va.astype(BF16).astype(F32) + bs0_ref[...]
        yain_s[...] = u * z
        q_s[...] = q
        k_s[...] = k
        v_s[...] = v
        ga_s[...] = gate_a
        gb_s[...] = gate_b

    r0 = pl.multiple_of(i * SAMPLE_STEP, SAMPLE_STEP)
    kwin = jnp.concatenate([kc_ref[:, 1:, :], k_s[pl.ds(r0, SAMPLE_STEP), :][:, None, :]], axis=1)
    vwin = jnp.concatenate([vc_ref[:, 1:, :], v_s[pl.ds(r0, SAMPLE_STEP), :][:, None, :]], axis=1)
    kwin_ref[...] = kwin
    vwin_ref[...] = vwin

    q16 = q_s[pl.ds(r0, SAMPLE_STEP), :]
    lane = lax.broadcasted_iota(I32, (SAMPLE_STEP, LANES), 1)
    heads = []
    for hq in range(N_HEADS):
        c, p, h = hq // 2, hq % 2, hq // GQA_GROUP
        chunk = q16[:, c * LANES:(c + 1) * LANES]
        if p != h:
            chunk = pltpu.roll(chunk, HEAD_DIM, 1)
        keep = (lane < HEAD_DIM) if h == 0 else (lane >= HEAD_DIM)
        heads.append(jnp.where(keep, chunk, 0.0))
    qpad = pltpu.einshape("hbd->bhd", jnp.stack(heads, axis=0)).astype(BF16)
    s = jnp.einsum("bhd,bkd->bhk", qpad, kwin.astype(BF16), preferred_element_type=F32) * (HEAD_DIM ** -0.5)
    hid = lax.broadcasted_iota(I32, (1, N_HEADS, 1), 1)
    sink = jnp.zeros((1, N_HEADS, 1), F32)
    for hq in range(N_HEADS):
        sink = jnp.where(hid == hq, sinks_ref[hq], sink)
    m = jnp.maximum(jnp.max(s, axis=-1, keepdims=True), sink)
    e = jnp.exp(s - m)
    inv = 1.0 / (jnp.sum(e, axis=-1, keepdims=True) + jnp.exp(sink - m))
    o = jnp.einsum("bhk,bkd->bhd", (e * inv).astype(BF16), vwin.astype(BF16), preferred_element_type=F32)
    o = pltpu.einshape("bhd->hbd", o)
    chunks = []
    for c in range(N_HEADS // 2):
        parts = []
        for p in range(2):
            hq = 2 * c + p
            oh = o[hq]
            if p != hq // GQA_GROUP:
                oh = pltpu.roll(oh, HEAD_DIM, 1)
            parts.append(oh)
        chunks.append(jnp.where(lane < HEAD_DIM, parts[0], parts[1]))
    att_s[pl.ds(r0, SAMPLE_STEP), :] = jnp.concatenate(chunks, axis=1)

    @pl.when(i == SAMPLE_STEPS - 1)
    def _():
        x1, h2, logits = _merge_and_route(
            x_ref[...], yain_s[...], att_s[...], ga_s[...], gb_s[...],
            wpa_ref, wpb_ref, wo_ref, gffn_ref, wr3_ref, br_ref)
        _store_token_major(x1_ref, x1)
        _store_token_major(h2_ref, h2)
        logits_ref[...] = logits


def _sample_front(x, cos, sin, gmix, win, lng, lnb, wdiag, bs0, sinks, kc, vc, wpa, wpb, wo, gffn, wrt, br):
    n = DEC_BATCH
    cache_spec = pl.BlockSpec((SAMPLE_STEP, WINDOW, KV_WIDTH), lambda i: (i, 0, 0))
    in_specs = [
        _full((n, D_MODEL)),
        _full((1, LANES)),
        _full((1, LANES)),
        _full((1, D_MODEL)),
        _full((D_MODEL, IN_COLS)),
        _full((1, A_WIDTH)),
        _full((1, A_WIDTH)),
        _full((1, A_WIDTH)),
        _full((1, A_WIDTH)),
        pl.BlockSpec(memory_space=pltpu.SMEM),
        cache_spec,
        cache_spec,
        _full((A_WIDTH, D_MODEL)),
        _full((Q_WIDTH, D_MODEL)),
        _full((D_MODEL, D_MODEL)),
        _full((1, D_MODEL)),
        _full((3 * D_MODEL, LANES)),
        _full((N_EXPERTS, 1)),
    ]
    out_shape = [
        jax.ShapeDtypeStruct((n * LANE_CHUNKS, LANES), F32),
        jax.ShapeDtypeStruct((n * LANE_CHUNKS, LANES), F32),
        jax.ShapeDtypeStruct((N_EXPERTS, n), F32),
        jax.ShapeDtypeStruct((n, WINDOW, KV_WIDTH), F32),
        jax.ShapeDtypeStruct((n, WINDOW, KV_WIDTH), F32),
        jax.ShapeDtypeStruct((n, A_WIDTH), F32),
    ]
    out_specs = [
        _full((n * LANE_CHUNKS, LANES)),
        _full((n * LANE_CHUNKS, LANES)),
        _full((N_EXPERTS, n)),
        cache_spec,
        cache_spec,
        _full((n, A_WIDTH)),
    ]
    scratch = [
        pltpu.VMEM((n, Q_WIDTH), F32), pltpu.VMEM((n, KV_WIDTH), F32), pltpu.VMEM((n, KV_WIDTH), F32),
        pltpu.VMEM((n, A_WIDTH), F32), pltpu.VMEM((n, D_MODEL), F32), pltpu.VMEM((n, D_MODEL), F32),
        pltpu.VMEM((n, Q_WIDTH), F32),
    ]
    return pl.pallas_call(
        _sample_kernel,
        grid=(SAMPLE_STEPS,),
        in_specs=in_specs,
        out_specs=out_specs,
        out_shape=out_shape,
        scratch_shapes=scratch,
        compiler_params=pltpu.CompilerParams(dimension_semantics=("arbitrary",), vmem_limit_bytes=VMEM_LIMIT),
        name="sample_front",
    )(x, cos, sin, gmix, win, lng, lnb, wdiag, bs0, sinks, kc, vc, wpa, wpb, wo, gffn, wrt, br)


def _route_plan_kernel(lp_ref, ls_ref, dest_ref, wts_ref, off_ref):
    g = pl.program_id(0)
    topi, topw = _top4_softmax(jnp.concatenate([lp_ref[...], ls_ref[...]], axis=1))
    slot = lax.broadcasted_iota(I32, (TOP_K, GROUP_SLOTS), 1)
    eall = jnp.where(jnp.logical_or(slot < GROUP_PROMPT, g == N_GROUPS - 1), topi, N_EXPERTS)
    wts_ref[:, 0:GROUP_SLOTS] = topw
    wts_ref[:, GROUP_SLOTS:] = jnp.zeros((TOP_K, K_STRIDE - GROUP_SLOTS), F32)
    dest_ref[:, GROUP_SLOTS:] = jnp.zeros((TOP_K, K_STRIDE - GROUP_SLOTS), I32)
    eid = lax.broadcasted_iota(I32, (N_EXPERTS, GROUP_SLOTS), 0)
    onehots = [eall[k:k + 1, :] == eid for k in range(TOP_K)]
    count = jnp.zeros((N_EXPERTS, GROUP_SLOTS), F32)
    for oh in onehots:
        count = count + oh.astype(F32)
    total = jnp.broadcast_to(jnp.sum(count, axis=1, keepdims=True), (N_EXPERTS, LANES))
    nblk = jnp.floor((total + (MOE_ROWS - 1)) * (1.0 / MOE_ROWS))
    r = lax.broadcasted_iota(I32, (N_EXPERTS, N_EXPERTS), 0)
    c = lax.broadcasted_iota(I32, (N_EXPERTS, N_EXPERTS), 1)
    first_blk = lax.dot_general((c < r).astype(F32), nblk, (((1,), (0,)), ((), ())),
                                precision=lax.Precision.HIGHEST, preferred_element_type=F32)
    start = (first_blk + 1.0) * MOE_ROWS
    lane = lax.broadcasted_iota(I32, (N_EXPERTS, LANES), 1)
    info = jnp.where(lane == 0, start, jnp.where(lane == 1, start + total, jnp.where(lane == 2, nblk, first_blk)))
    off_ref[...] = info.astype(I32)
    ti = lax.broadcasted_iota(I32, (LANES, LANES), 0)
    tj = lax.broadcasted_iota(I32, (LANES, LANES), 1)
    before = (ti < tj).astype(BF16)
    ones = jnp.ones((LANES, LANES), BF16)
    running = start
    for t in range(SLOT_TILES):
        sl = slice(t * LANES, (t + 1) * LANES)
        cb = count[:, sl].astype(BF16)
        pos = running + _bdot(cb, before)
        rows = [jnp.sum(jnp.where(oh[:, sl], pos, 0.0), axis=0, keepdims=True) for oh in onehots]
        dest_ref[:, sl] = jnp.concatenate(rows, axis=0).astype(I32)
        running = running + _bdot(cb, ones)


def _route_plan(logits_p, logits_s):
    in_specs = [
        pl.BlockSpec((N_EXPERTS, GROUP_PROMPT), lambda g: (0, g)),
        pl.BlockSpec((N_EXPERTS, DEC_BATCH), lambda g: (0, 0)),
    ]
    out_shape = [
        jax.ShapeDtypeStruct((N_GROUPS, TOP_K, K_STRIDE), I32),
        jax.ShapeDtypeStruct((N_GROUPS, TOP_K, K_STRIDE), F32),
        jax.ShapeDtypeStruct((N_GROUPS, N_EXPERTS, LANES), I32),
    ]
    out_specs = [
        pl.BlockSpec((None, TOP_K, K_STRIDE), lambda g: (g, 0, 0)),
        pl.BlockSpec((None, TOP_K, K_STRIDE), lambda g: (g, 0, 0)),
        pl.BlockSpec((None, N_EXPERTS, LANES), lambda g: (g, 0, 0)),
    ]
    return pl.pallas_call(
        _route_plan_kernel,
        grid=(N_GROUPS,),
        in_specs=in_specs,
        out_specs=out_specs,
        out_shape=out_shape,
        compiler_params=pltpu.CompilerParams(dimension_semantics=("arbitrary",)),
        name="route_plan",
    )(logits_p, logits_s)


GROUP_ROWS = GROUP_PROMPT * LANE_CHUNKS
SAMPLE_ROWS = DEC_BATCH * LANE_CHUNKS
TRASH_SLOT = GROUP_SLOTS
BUF_ROWS = (GROUP_SLOTS + 1) * LANE_CHUNKS
SCATTER_BATCH = 8


def _moe_kernel(off_ref, desth_ref, wtsh_ref, h2p_ref, h2s_ref, x1p_ref, x1s_ref, wgu_ref, bgu_ref, wdn_ref, bdn_ref,
                x2p_ref, x2s_ref,
                h2buf, acc, wgubuf, bgubuf, wdnbuf, bdnbuf, xs0, xs1, ys0, ys1,
                dest_ref, wts_ref, src_ref, seg_expert, seg_first, blk_seg, act_sem, w_sem):
    g = pl.program_id(0)
    last = g == N_GROUPS - 1
    row0 = pl.multiple_of(g * GROUP_ROWS, GROUP_ROWS)

    def prompt_copies():
        return (pltpu.make_async_copy(h2p_ref.at[pl.ds(row0, GROUP_ROWS)], h2buf.at[pl.ds(0, GROUP_ROWS)], act_sem.at[0]),
                pltpu.make_async_copy(x1p_ref.at[pl.ds(row0, GROUP_ROWS)], acc.at[pl.ds(0, GROUP_ROWS)], act_sem.at[1]))

    def sample_copies():
        return (pltpu.make_async_copy(h2s_ref, h2buf.at[pl.ds(GROUP_ROWS, SAMPLE_ROWS)], act_sem.at[2]),
                pltpu.make_async_copy(x1s_ref, acc.at[pl.ds(GROUP_ROWS, SAMPLE_ROWS)], act_sem.at[3]))

    def weight_copies(e, slot):
        return (pltpu.make_async_copy(wgu_ref.at[e], wgubuf.at[slot], w_sem.at[0, slot]),
                pltpu.make_async_copy(bgu_ref.at[e], bgubuf.at[slot], w_sem.at[1, slot]),
                pltpu.make_async_copy(wdn_ref.at[e], wdnbuf.at[slot], w_sem.at[2, slot]),
                pltpu.make_async_copy(bdn_ref.at[e], bdnbuf.at[slot], w_sem.at[3, slot]))

    tab0 = pl.multiple_of(g * (TOP_K * K_STRIDE), TOP_K * K_STRIDE)
    table_copies = (
        pltpu.make_async_copy(desth_ref.at[pl.ds(tab0, TOP_K * K_STRIDE)], dest_ref, act_sem.at[4]),
        pltpu.make_async_copy(wtsh_ref.at[pl.ds(tab0, TOP_K * K_STRIDE)], wts_ref, act_sem.at[5]))
    for cp in table_copies:
        cp.start()

    for cp in prompt_copies():
        cp.start()

    @pl.when(last)
    def _():
        for cp in sample_copies():
            cp.start()

    trash = pl.ds(TRASH_SLOT * LANE_CHUNKS, LANE_CHUNKS)
    h2buf[trash, :] = jnp.zeros((LANE_CHUNKS, LANES), F32)
    acc[trash, :] = jnp.zeros((LANE_CHUNKS, LANES), F32)
    ys1[...] = jnp.zeros_like(ys1)

    def pad_block(pos0):
        def body(j, carry):
            for d in range(SUBLANES):
                src_ref[pos0 + j * SUBLANES + d] = TRASH_SLOT
            return carry
        lax.fori_loop(0, MOE_ROWS // SUBLANES, body, 0)

    def scan_expert(e, carry):
        nseg, nblocks = carry
        nblk = off_ref[e, 2]
        first = off_ref[e, 3]

        @pl.when(nblk > 0)
        def _():
            seg_expert[nseg] = e
            seg_first[nseg] = first
            pad_block(off_ref[e, 0] + (nblk - 1) * MOE_ROWS)

            def mark(b, c2):
                blk_seg[first + b] = nseg
                return c2
            lax.fori_loop(0, nblk, mark, 0)

        return nseg + jnp.where(nblk > 0, 1, 0), nblocks + nblk

    nseg, nblocks = lax.fori_loop(0, N_EXPERTS, scan_expert, (jnp.int32(0), jnp.int32(0)))
    pad_block(0)
    pad_block((nblocks + 1) * MOE_ROWS)
    pad_block((nblocks + 2) * MOE_ROWS)
    blk_seg[nblocks] = nseg - 1
    blk_seg[nblocks + 1] = nseg - 1

    for cp in weight_copies(seg_expert[0], 0):
        cp.start()

    for cp in table_copies:
        cp.wait()

    nvalid = jnp.where(last, GROUP_SLOTS, GROUP_PROMPT)
    for k in range(TOP_K):
        def fill(j, carry, k=k):
            c0 = k * K_STRIDE + j * SUBLANES
            for d in range(SUBLANES):
                src_ref[dest_ref[c0 + d]] = c0 + d
            return carry
        lax.fori_loop(0, nvalid // SUBLANES, fill, 0)

    for cp in prompt_copies():
        cp.wait()

    @pl.when(last)
    def _():
        for cp in sample_copies():
            cp.wait()

    def token_rows(code):
        slot_id = code & (K_STRIDE - 1)
        return pl.ds(pl.multiple_of(slot_id * LANE_CHUNKS, LANE_CHUNKS), LANE_CHUNKS)

    def gather(b, xs):
        base = (b + 1) * MOE_ROWS
        for m in range(MOE_ROWS):
            xs[pl.ds(m, LANE_CHUNKS, stride=XS_STRIDE), :] = h2buf[token_rows(src_ref[base + m]), :]

    def scatter_add(b, ys):
        base = (b + 1) * MOE_ROWS
        for m0 in range(0, MOE_ROWS, SCATTER_BATCH):
            pending = []
            for m in range(m0, m0 + SCATTER_BATCH):
                code = src_ref[base + m]
                rows = token_rows(code)
                pending.append((rows, acc[rows, :] + wts_ref[code] * ys[pl.ds(m, LANE_CHUNKS, stride=XS_STRIDE), :]))
            for rows, val in pending:
                acc[rows, :] = val

    def expert_ffn(xs, ys, slot):
        x = jnp.concatenate(
            [xs[c * XS_STRIDE:c * XS_STRIDE + MOE_ROWS, :] for c in range(LANE_CHUNKS)], axis=1).astype(BF16)
        gu = _bdot(x, wgubuf[slot]) + bgubuf[slot]
        gl = jnp.minimum(gu[:, :D_EXPERT], SWIGLU_LIMIT)
        ul = jnp.clip(gu[:, D_EXPERT:], -SWIGLU_LIMIT, SWIGLU_LIMIT)
        a = (ul + 1.0) * (gl * jax.nn.sigmoid(SWIGLU_ALPHA * gl))
        y = _bdot(a.astype(BF16), wdnbuf[slot]) + bdnbuf[slot]
        for c in range(LANE_CHUNKS):
            ys[c * XS_STRIDE:c * XS_STRIDE + MOE_ROWS, :] = y[:, c * LANES:(c + 1) * LANES]

    def step(b, xs_cur, xs_next, ys_cur, ys_prev):
        seg = blk_seg[b]
        slot = seg & 1

        @pl.when(jnp.logical_and(b == seg_first[seg], b < nblocks))
        def _():
            for cp in weight_copies(seg_expert[seg], slot):
                cp.wait()

            @pl.when(seg + 1 < nseg)
            def _():
                for cp in weight_copies(seg_expert[seg + 1], 1 - slot):
                    cp.start()

        gather(b + 1, xs_next)
        expert_ffn(xs_cur, ys_cur, slot)
        scatter_add(b - 1, ys_prev)

    gather(0, xs0)
    npairs = (nblocks + 1) // 2

    def pair(t, carry):
        step(2 * t, xs0, xs1, ys0, ys1)
        step(2 * t + 1, xs1, xs0, ys1, ys0)
        return carry

    lax.fori_loop(0, npairs, pair, 0)
    scatter_add(2 * npairs - 1, ys1)

    out_p =pltpu.make_async_copy(acc.at[pl.ds(0, GROUP_ROWS)], x2p_ref.at[pl.ds(row0, GROUP_ROWS)], act_sem.at[0])
    out_p.start()

    @pl.when(last)
    def _():
        out_s = pltpu.make_async_copy(acc.at[pl.ds(GROUP_ROWS, SAMPLE_ROWS)], x2s_ref, act_sem.at[2])
        out_s.start()
        out_s.wait()

    out_p.wait()


def _moe(dest, wts, off, h2p, h2s, x1p, x1s, wgu, bgu, wdn, bdn):
    anyspec = pl.BlockSpec(memory_space=pl.ANY)
    dest = dest.reshape(N_GROUPS * TOP_K * K_STRIDE)
    wts = wts.reshape(N_GROUPS * TOP_K * K_STRIDE)
    in_specs = [
        pl.BlockSpec((None, N_EXPERTS, LANES), lambda g: (g, 0, 0), memory_space=pltpu.SMEM),
        anyspec, anyspec, anyspec, anyspec, anyspec, anyspec, anyspec, anyspec, anyspec, anyspec,
    ]
    scratch = [
        pltpu.VMEM((BUF_ROWS, LANES), F32),
        pltpu.VMEM((BUF_ROWS, LANES), F32),
        pltpu.VMEM((2, D_MODEL, 2 * D_EXPERT), BF16),
        pltpu.VMEM((2, 1, 2 * D_EXPERT), F32),
        pltpu.VMEM((2, D_EXPERT, D_MODEL), BF16),
        pltpu.VMEM((2, 1, D_MODEL), F32),
        pltpu.VMEM((LANE_CHUNKS * XS_STRIDE, LANES), F32),
        pltpu.VMEM((LANE_CHUNKS * XS_STRIDE, LANES), F32),
        pltpu.VMEM((LANE_CHUNKS * XS_STRIDE, LANES), F32),
        pltpu.VMEM((LANE_CHUNKS * XS_STRIDE, LANES), F32),
        pltpu.SMEM((TOP_K * K_STRIDE,), I32),
        pltpu.SMEM((TOP_K * K_STRIDE,), F32),
        pltpu.SMEM((POS_TABLE,), I32),
        pltpu.SMEM((N_EXPERTS,), I32),
        pltpu.SMEM((N_EXPERTS,), I32),
        pltpu.SMEM((LANES,), I32),
        pltpu.SemaphoreType.DMA((6,)),
        pltpu.SemaphoreType.DMA((4, 2)),
    ]
    return pl.pallas_call(
        _moe_kernel,
        grid=(N_GROUPS,),
        in_specs=in_specs,
        out_specs=[anyspec, anyspec],
        out_shape=[jax.ShapeDtypeStruct(x1p.shape, F32), jax.ShapeDtypeStruct(x1s.shape, F32)],
        scratch_shapes=scratch,
        compiler_params=pltpu.CompilerParams(dimension_semantics=("arbitrary",), vmem_limit_bytes=VMEM_LIMIT),
        name="moe",
    )(off, dest, wts, h2p, h2s, x1p, x1s, wgu, bgu, wdn, bdn)


def _ple_final_kernel(x2_ref, ple_ref, wple_ref, gple_ref, wpg_ref, gfin_ref, y_ref):
    rows = y_ref.shape[0]
    x2 = _load_token_major(x2_ref, rows)
    e = _rmsnorm(_bdot(ple_ref[...].astype(BF16), wple_ref[...]), gple_ref[...])
    x3 = x2 + jax.nn.sigmoid(_bdot(x2.astype(BF16), wpg_ref[...])) * e
    y_ref[...] = _rmsnorm(x3, gfin_ref[...])


def _ple_final(x2_tm, ple, wple, gple, wpg, gfin, tile):
    n = ple.shape[0]
    return pl.pallas_call(
        _ple_final_kernel,
        grid=(n // tile,),
        in_specs=[
            pl.BlockSpec((tile * LANE_CHUNKS, LANES), lambda i: (i, 0)),
            pl.BlockSpec((tile, PLE_DIM), lambda i: (i, 0)),
            _full((PLE_DIM, D_MODEL)),
            _full((1, D_MODEL)),
            _full((D_MODEL, D_MODEL)),
            _full((1, D_MODEL)),
        ],
        out_specs=pl.BlockSpec((tile, D_MODEL), lambda i: (i, 0)),
        out_shape=jax.ShapeDtypeStruct((n, D_MODEL), F32),
        compiler_params=pltpu.CompilerParams(dimension_semantics=("arbitrary",), vmem_limit_bytes=VMEM_LIMIT),
        name="ple_final",
    )(x2_tm, ple, wple, gple, wpg, gfin)


def _rope_tables(pos):
    half = HEAD_DIM // 2
    inv = ROPE_THETA ** (-jnp.arange(half, dtype=F32) / half)
    ang = pos.astype(F32)[:, None] * inv[None, :]
    cos, sin = jnp.cos(ang), jnp.sin(ang)
    cos2 = jnp.concatenate([cos, cos, cos, cos], axis=1)
    sin2 = jnp.concatenate([-sin, sin, -sin, sin], axis=1)
    return cos2, sin2


def _router_passes(w_router):
    hi = w_router.astype(BF16)
    lo = (w_router - hi.astype(F32)).astype(BF16)
    w3 = jnp.concatenate([hi, lo, hi], axis=0)
    return jnp.pad(w3, ((0, 0), (0, LANES - N_EXPERTS)))


def _prep_weights(g_mix, w_in, a_ln_g, a_ln_b, a_ws, a_bs, w_pa, w_pb, w_o, g_ffn, w_router, b_router):
    causal = jnp.tril(jnp.ones((CHUNK, CHUNK), dtype=bool))
    return dict(
        gmix=g_mix.reshape(1, D_MODEL),
        win=w_in.astype(BF16),
        lng=a_ln_g.reshape(1, A_WIDTH),
        lnb=a_ln_b.reshape(1, A_WIDTH),
        ws=jnp.where(causal[None], a_ws, 0.0).astype(BF16),
        bsf=jnp.repeat(jnp.transpose(a_bs), A_GROUP_DIM, axis=1),
        wpa=w_pa.astype(BF16),
        wpb=w_pb.astype(BF16),
        wo=w_o.astype(BF16),
        gffn=g_ffn.reshape(1, D_MODEL),
        wrt=_router_passes(w_router),
        br=b_router.reshape(N_EXPERTS, 1),
    )


def kernel(x_prompt, x_sample, cache_win_k, cache_win_v, p_prompt, p_sample, g_mix, w_in, a_ln_g, a_ln_b, a_ws, a_bs, sinks, w_pa, w_pb, w_o, g_ffn, w_router, b_router, w_gu, b_gu, w_down, b_down, w_ple, g_ple, w_ple_gate, g_final):
    W = _prep_weights(g_mix[0], w_in[0], a_ln_g[0], a_ln_b[0], a_ws[0], a_bs[0], w_pa[0], w_pb[0], w_o[0],
                      g_ffn[0], w_router[0], b_router[0])
    cos_p, sin_p = _rope_tables(jnp.arange(SEQ, dtype=I32))
    cos_s, sin_s = _rope_tables(jnp.full((1,), PAST_LEN, I32))
    x1p, h2p, logits_p, kwin_p, vwin_p, wgu16, wdn16 = _prompt_front(
        x_prompt.reshape(N_PROMPT, D_MODEL), cos_p, sin_p, W["gmix"], W["win"], W["lng"], W["lnb"],
        W["ws"], W["bsf"], sinks[0], W["wpa"], W["wpb"], W["wo"], W["gffn"], W["wrt"], W["br"],
        w_gu[0].reshape(N_EXPERTS * D_MODEL, 2 * D_EXPERT), w_down[0].reshape(N_EXPERTS * D_EXPERT, D_MODEL))

    wdiag = jnp.repeat(a_ws[0, :, 0, 0], A_GROUP_DIM)[None, :].astype(BF16)
    bs0 = jnp.repeat(a_bs[0, :, 0], A_GROUP_DIM)[None, :]
    x1s, h2s, logits_s, kwin_s, vwin_s, va_s = _sample_front(
        x_sample.reshape(DEC_BATCH, D_MODEL), cos_s, sin_s, W["gmix"], W["win"], W["lng"], W["lnb"], wdiag, bs0,
        sinks[0], cache_win_k[0].reshape(DEC_BATCH, WINDOW, KV_WIDTH), cache_win_v[0].reshape(DEC_BATCH, WINDOW, KV_WIDTH),
        W["wpa"], W["wpb"], W["wo"], W["gffn"], W["wrt"], W["br"])

    dest, wts, off = _route_plan(logits_p, logits_s)
    x2p, x2s = _moe(dest, wts, off, h2p, h2s, x1p, x1s,
                    wgu16.reshape(N_EXPERTS, D_MODEL, 2 * D_EXPERT), b_gu[0].reshape(N_EXPERTS, 1, 2 * D_EXPERT),
                    wdn16.reshape(N_EXPERTS, D_EXPERT, D_MODEL), b_down[0].reshape(N_EXPERTS, 1, D_MODEL))

    wple = w_ple[0].astype(BF16)
    gple = g_ple[0].reshape(1, D_MODEL)
    wpg = w_ple_gate[0].astype(BF16)
    gfin = g_final.reshape(1, D_MODEL)
    y_p = _ple_final(x2p, p_prompt[0].reshape(N_PROMPT, PLE_DIM), wple, gple, wpg, gfin, TM)
    y_s = _ple_final(x2s, p_sample[0].reshape(DEC_BATCH, PLE_DIM), wple, gple, wpg, gfin, DEC_BATCH)

    return (
        y_p.reshape(BATCH, SEQ, D_MODEL),
        y_s.reshape(DEC_BATCH, 1, D_MODEL),
        kwin_p.reshape(1, BATCH, WINDOW, N_KV_HEADS, HEAD_DIM),
        vwin_p.reshape(1, BATCH, WINDOW, N_KV_HEADS, HEAD_DIM),
        kwin_s.reshape(1, DEC_BATCH, WINDOW, N_KV_HEADS, HEAD_DIM),
        vwin_s.reshape(1, DEC_BATCH, WINDOW, N_KV_HEADS, HEAD_DIM),
        va_s.reshape(1, DEC_BATCH, 1, A_WIDTH),
    )
```

```python
import functools

import jax
import jax.numpy as jnp
from jax import lax
from jax.experimental import pallas as pl
from jax.experimental.pallas import tpu as pltpu

F32 = jnp.float32
BF16 = jnp.bfloat16
I32 = jnp.int32

D_MODEL = 1024
BATCH = 4
SEQ = 4096
DEC_BATCH = 128
PAST_LEN = 8192
CHUNK = 128
A_GROUPS = 4
A_GROUP_DIM = 128
A_WIDTH = A_GROUPS * A_GROUP_DIM
N_HEADS = 8
N_KV_HEADS = 2
HEAD_DIM = 64
Q_WIDTH = N_HEADS * HEAD_DIM
KV_WIDTH = N_KV_HEADS * HEAD_DIM
GQA_GROUP = N_HEADS // N_KV_HEADS
WINDOW = 128
ROPE_THETA = 10000.0
N_EXPERTS = 32
TOP_K = 4
D_EXPERT = D_MODEL
SWIGLU_ALPHA = 1.702
SWIGLU_LIMIT = 7.0
PLE_DIM = 256
RMS_EPS = 1e-5
LN_EPS = 1e-5

O_Q = 2 * A_WIDTH
O_K = O_Q + Q_WIDTH
O_V = O_K + KV_WIDTH
O_GA = O_V + KV_WIDTH
O_GB = O_GA + D_MODEL
IN_COLS = O_GB + D_MODEL

LANES = 128
SUBLANES = 8
LANE_CHUNKS = D_MODEL // LANES
VMEM_LIMIT = 56 * 1024 * 1024

N_PROMPT = BATCH * SEQ
TM = 256
TILES_PER_SEQ = SEQ // TM
BLOCKS_PER_TILE = TM // WINDOW
FRONT_STEPS = N_PROMPT // TM
CAST_ROWS = N_EXPERTS * D_MODEL // FRONT_STEPS

N_GROUPS = 4
GROUP_PROMPT = N_PROMPT // N_GROUPS
GROUP_SLOTS = GROUP_PROMPT + DEC_BATCH
GROUP_ASSIGN = GROUP_SLOTS * TOP_K
SLOT_TILES = GROUP_SLOTS // LANES
MOE_ROWS = 288
XS_STRIDE = MOE_ROWS + SUBLANES
SLOT_BITS = 13
K_STRIDE = 1 << SLOT_BITS
assert GROUP_SLOTS < K_STRIDE
MAX_BLOCKS = GROUP_ASSIGN // MOE_ROWS + N_EXPERTS
POS_TABLE = 1 << 15
assert (MAX_BLOCKS + 3) * MOE_ROWS <= POS_TABLE
assert MAX_BLOCKS + 2 <= LANES


def _bdot(a, b):
    return jnp.dot(a, b, preferred_element_type=F32)


def _rmsnorm(x, g):
    return x * lax.rsqrt(jnp.mean(x * x, axis=-1, keepdims=True) + RMS_EPS) * g


def _gelu(x):
    return 0.5 * x * (1.0 + lax.erf(x * (0.5 ** 0.5)))


def _group_layernorm(v, g, b):
    cols = []
    for gi in range(A_GROUPS):
        s = slice(gi * A_GROUP_DIM, (gi + 1) * A_GROUP_DIM)
        vg = v[:, s]
        mu = jnp.mean(vg, axis=-1, keepdims=True)
        d = vg - mu
        var = jnp.mean(d * d, axis=-1, keepdims=True)
        cols.append(d * lax.rsqrt(var + LN_EPS) * g[:, s] + b[:, s])
    return jnp.concatenate(cols, axis=1)


def _rope(x, cos, sin_signed):
    width = x.shape[1]
    reps = width // LANES
    cosf = jnp.concatenate([cos] * reps, axis=1) if reps > 1 else cos
    sinf = jnp.concatenate([sin_signed] * reps, axis=1) if reps > 1 else sin_signed
    half = HEAD_DIM // 2
    lane = lax.broadcasted_iota(I32, x.shape, 1)
    up = pltpu.roll(x, width - half, 1)
    down = pltpu.roll(x, half, 1)
    partner = jnp.where((lane & (HEAD_DIM - 1)) < half, up, down)
    return x * cosf + partner * sinf


def _in_projection(x, gmix_ref, win_ref, lng_ref, lnb_ref, cos, sin_signed):
    hb = _rmsnorm(x, gmix_ref[...]).astype(BF16)
    zuv = _gelu(_bdot(hb, win_ref[:, 0:O_Q]))
    u = zuv[:, :A_WIDTH]
    va = _group_layernorm(zuv[:, A_WIDTH:], lng_ref[...], lnb_ref[...])
    zqkv = _bdot(hb, win_ref[:, O_Q:O_GA])
    q = _rope(zqkv[:, :Q_WIDTH], cos, sin_signed)
    k = _rope(zqkv[:, Q_WIDTH:Q_WIDTH + KV_WIDTH], cos, sin_signed)
    v = zqkv[:, Q_WIDTH + KV_WIDTH:]
    zg = _bdot(hb, win_ref[:, O_GA:IN_COLS])
    gate_a = jax.nn.sigmoid(zg[:, :D_MODEL])
    gate_b = jax.nn.sigmoid(zg[:, D_MODEL:])
    return u, va, q, k, v, gate_a, gate_b


def _merge_and_route(x, ya_in, att, gate_a, gate_b, wpa_ref, wpb_ref, wo_ref, gffn_ref, wr3_ref, br_ref):
    ya = _bdot(ya_in.astype(BF16), wpa_ref[...])
    yb = _bdot(att.astype(BF16), wpb_ref[...])
    mix = (gate_a * ya + gate_b * yb).astype(BF16)
    x1 = x + _bdot(mix, wo_ref[...])
    h2 = _rmsnorm(x1, gffn_ref[...])
    hi = h2.astype(BF16)
    lo = (h2 - hi.astype(F32)).astype(BF16)
    logits = _bdot(jnp.concatenate([hi, hi, lo], axis=1), wr3_ref[...])
    return x1, h2, jnp.transpose(logits)[:N_EXPERTS, :] + br_ref[...]


def _top4_softmax(logits):
    eid = lax.broadcasted_iota(I32, logits.shape, 0)
    vals, idxs = [], []
    for _ in range(TOP_K):
        m = jnp.max(logits, axis=0, keepdims=True)
        idx = jnp.min(jnp.where(logits == m, eid, N_EXPERTS), axis=0, keepdims=True)
        logits = jnp.where(eid == idx, -jnp.inf, logits)
        vals.append(m)
        idxs.append(idx)
    es = [jnp.exp(v - vals[0]) for v in vals]
    inv = 1.0 / (es[0] + es[1] + es[2] + es[3])
    return jnp.concatenate(idxs, axis=0), jnp.concatenate([e * inv for e in es], axis=0)


def _store_token_major(ref, val):
    rows = val.shape[0]
    for c in range(LANE_CHUNKS):
        ref[pl.ds(c, rows, stride=LANE_CHUNKS), :] = val[:, c * LANES:(c + 1) * LANES]


def _load_token_major(ref, rows):
    return jnp.concatenate([ref[pl.ds(c, rows, stride=LANE_CHUNKS), :] for c in range(LANE_CHUNKS)], axis=1)


def _band_attention(q, k, v, k_prev, v_prev, sinks_ref, seq_start):
    unit_rows = GQA_GROUP * WINDOW
    kb = jnp.concatenate([k_prev, k], axis=0).astype(BF16)
    vb = jnp.concatenate([v_prev, v], axis=0).astype(BF16)
    qb = q.astype(BF16)
    units = [(b, h) for b in range(BLOCKS_PER_TILE) for h in range(N_KV_HEADS)]
    scores = []
    for b, h in units:
        qh = jnp.concatenate(
            [qb[b * WINDOW:(b + 1) * WINDOW, (h * GQA_GROUP + j) * HEAD_DIM:(h * GQA_GROUP + j + 1) * HEAD_DIM]
             for j in range(GQA_GROUP)], axis=0)
        kh = kb[b * WINDOW:(b + 2) * WINDOW, h * HEAD_DIM:(h + 1) * HEAD_DIM]
        scores.append(lax.dot_general(qh, kh, (((1,), (1,)), ((), ())), preferred_element_type=F32))
    s = jnp.concatenate(scores, axis=0) * (HEAD_DIM ** -0.5)
    shape = s.shape
    row = lax.broadcasted_iota(I32, shape, 0)
    qi = row & (WINDOW - 1)
    kj = lax.broadcasted_iota(I32, shape, 1)
    lo = jnp.where(jnp.logical_and(seq_start, row < N_KV_HEADS * unit_rows), WINDOW, 0)
    valid = (kj > qi) & (kj <= qi + WINDOW) & (kj >= lo)
    s = jnp.where(valid, s, -jnp.inf)
    sink = jnp.concatenate(
        [jnp.full((WINDOW, 1), sinks_ref[h * GQA_GROUP + j], F32) for b, h in units for j in range(GQA_GROUP)], axis=0)
    m = jnp.maximum(jnp.max(s, axis=-1, keepdims=True), sink)
    e = jnp.exp(s - m)
    inv = 1.0 / (jnp.sum(e, axis=-1, keepdims=True) + jnp.exp(sink - m))
    p = (e * inv).astype(BF16)
    att_rows = []
    for b in range(BLOCKS_PER_TILE):
        heads = []
        for h in range(N_KV_HEADS):
            u = b * N_KV_HEADS + h
            vh = vb[b * WINDOW:(b + 2) * WINDOW, h * HEAD_DIM:(h + 1) * HEAD_DIM]
            o = _bdot(p[u * unit_rows:(u + 1) * unit_rows], vh)
            heads.extend(o[j * WINDOW:(j + 1) * WINDOW, :] for j in range(GQA_GROUP))
        att_rows.append(jnp.concatenate(heads, axis=1))
    return jnp.concatenate(att_rows, axis=0)


def _prompt_front_kernel(x_ref, cos_ref, sin_ref, gmix_ref, win_ref, lng_ref, lnb_ref, ws_ref, bsf_ref,
                         sinks_ref, wpa_ref, wpb_ref, wo_ref, gffn_ref, wr3_ref, br_ref, wgu32_ref, wdn32_ref,
                         x1_ref, h2_ref, logits_ref, kwin_ref, vwin_ref, wgu16_ref, wdn16_ref,
                         kprev_ref, vprev_ref, gu_in, dn_in, gu_out, dn_out, cast_sem):
    i = pl.program_id(0)
    seq_start = (i % TILES_PER_SEQ) == 0

    def cast_in(c, slot):
        r = pl.ds(pl.multiple_of(c * CAST_ROWS, CAST_ROWS), CAST_ROWS)
        return (pltpu.make_async_copy(wgu32_ref.at[r], gu_in.at[slot], cast_sem.at[0, slot]),
                pltpu.make_async_copy(wdn32_ref.at[r], dn_in.at[slot], cast_sem.at[1, slot]))

    def cast_out(c, slot):
        r = pl.ds(pl.multiple_of(c * CAST_ROWS, CAST_ROWS), CAST_ROWS)
        return (pltpu.make_async_copy(gu_out.at[slot], wgu16_ref.at[r], cast_sem.at[2, slot]),
                pltpu.make_async_copy(dn_out.at[slot], wdn16_ref.at[r], cast_sem.at[3, slot]))

    slot = i & 1

    @pl.when(i == 0)
    def _():
        for cp in cast_in(0, 0):
            cp.start()

    @pl.when(i + 1 < FRONT_STEPS)
    def _():
        for cp in cast_in(i + 1, 1 - slot):
            cp.start()

    for cp in cast_in(i, slot):
        cp.wait()

    @pl.when(i >= 2)
    def _():
        for cp in cast_out(i - 2, slot):
            cp.wait()

    gu_out[slot] = gu_in[slot].astype(BF16)
    dn_out[slot] = dn_in[slot].astype(BF16)
    for cp in cast_out(i, slot):
        cp.start()

    @pl.when(seq_start)
    def _():
        kprev_ref[...] = jnp.zeros_like(kprev_ref)
        vprev_ref[...] = jnp.zeros_like(vprev_ref)

    x = x_ref[...]
    u, va, q, k, v, gate_a, gate_b = _in_projection(
        x, gmix_ref, win_ref, lng_ref, lnb_ref, cos_ref[...], sin_ref[...])

    att = _band_attention(q, k, v, kprev_ref[...], vprev_ref[...], sinks_ref, seq_start)
    k_last, v_last = k[TM - WINDOW:], v[TM - WINDOW:]
    kprev_ref[...] = k_last
    vprev_ref[...] = v_last
    kwin_ref[0] = k_last
    vwin_ref[0] = v_last

    vab = va.astype(BF16)
    zc = jnp.concatenate(
        [jnp.concatenate(
            [_bdot(ws_ref[g], vab[b * CHUNK:(b + 1) * CHUNK, g * A_GROUP_DIM:(g + 1) * A_GROUP_DIM])
             for g in range(A_GROUPS)], axis=1) + bsf_ref[...]
         for b in range(BLOCKS_PER_TILE)], axis=0)

    x1, h2, logits = _merge_and_route(x, u * zc, att, gate_a, gate_b,
                                      wpa_ref, wpb_ref, wo_ref, gffn_ref, wr3_ref, br_ref)
    _store_token_major(x1_ref, x1)
    _store_token_major(h2_ref, h2)
    logits_ref[...] = logits

    @pl.when(i == FRONT_STEPS - 1)
    def _():
        for cp in cast_out(i - 1, 1 - slot) + cast_out(i, slot):
            cp.wait()


def _full(shape):
    return pl.BlockSpec(shape, lambda i: (0,) * len(shape))


def _prompt_front(x, cos, sin, gmix, win, lng, lnb, ws, bsf, sinks, wpa, wpb, wo, gffn, wrt, br, wgu32, wdn32):
    n = x.shape[0]
    assert n == N_PROMPT
    grid = (FRONT_STEPS,)
    anyspec = pl.BlockSpec(memory_space=pl.ANY)
    in_specs = [
        pl.BlockSpec((TM, D_MODEL), lambda i: (i, 0)),
        pl.BlockSpec((TM, LANES), lambda i: (i % TILES_PER_SEQ, 0)),
        pl.BlockSpec((TM, LANES), lambda i: (i % TILES_PER_SEQ, 0)),
        _full((1, D_MODEL)),
        _full((D_MODEL, IN_COLS)),
        _full((1, A_WIDTH)),
        _full((1, A_WIDTH)),
        _full((A_GROUPS, CHUNK, CHUNK)),
        _full((CHUNK, A_WIDTH)),
        pl.BlockSpec(memory_space=pltpu.SMEM),
        _full((A_WIDTH, D_MODEL)),
        _full((Q_WIDTH, D_MODEL)),
        _full((D_MODEL, D_MODEL)),
        _full((1, D_MODEL)),
        _full((3 * D_MODEL, LANES)),
        _full((N_EXPERTS, 1)),
        anyspec,
        anyspec,
    ]
    out_shape = [
        jax.ShapeDtypeStruct((n * LANE_CHUNKS, LANES), F32),
        jax.ShapeDtypeStruct((n * LANE_CHUNKS, LANES), F32),
        jax.ShapeDtypeStruct((N_EXPERTS, n), F32),
        jax.ShapeDtypeStruct((n // SEQ, WINDOW, KV_WIDTH), F32),
        jax.ShapeDtypeStruct((n // SEQ, WINDOW, KV_WIDTH), F32),
        jax.ShapeDtypeStruct(wgu32.shape, BF16),
        jax.ShapeDtypeStruct(wdn32.shape, BF16),
    ]
    out_specs = [
        pl.BlockSpec((TM * LANE_CHUNKS, LANES), lambda i: (i, 0)),
        pl.BlockSpec((TM * LANE_CHUNKS, LANES), lambda i: (i, 0)),
        pl.BlockSpec((N_EXPERTS, TM), lambda i: (0, i)),
        pl.BlockSpec((1, WINDOW, KV_WIDTH), lambda i: (i // TILES_PER_SEQ, 0, 0)),
        pl.BlockSpec((1, WINDOW, KV_WIDTH), lambda i: (i // TILES_PER_SEQ, 0, 0)),
        anyspec,
        anyspec,
    ]
    scratch = [
        pltpu.VMEM((WINDOW, KV_WIDTH), F32),
        pltpu.VMEM((WINDOW, KV_WIDTH), F32),
        pltpu.VMEM((2, CAST_ROWS, 2 * D_EXPERT), F32),
        pltpu.VMEM((2, CAST_ROWS, D_MODEL), F32),
        pltpu.VMEM((2, CAST_ROWS, 2 * D_EXPERT), BF16),
        pltpu.VMEM((2, CAST_ROWS, D_MODEL), BF16),
        pltpu.SemaphoreType.DMA((4, 2)),
    ]
    return pl.pallas_call(
        _prompt_front_kernel,
        grid=grid,
        in_specs=in_specs,
        out_specs=out_specs,
        out_shape=out_shape,
        scratch_shapes=scratch,
        compiler_params=pltpu.CompilerParams(dimension_semantics=("arbitrary",), vmem_limit_bytes=VMEM_LIMIT),
        name="prompt_front",
    )(x, cos, sin, gmix, win, lng, lnb, ws, bsf, sinks, wpa, wpb, wo, gffn, wrt, br, wgu32, wdn32)


SAMPLE_STEP = 16
SAMPLE_STEPS = DEC_BATCH // SAMPLE_STEP


def _sample_kernel(x_ref, cos_ref, sin_ref, gmix_ref, win_ref, lng_ref, lnb_ref, wdiag_ref, bs0_ref, sinks_ref,
                   kc_ref, vc_ref, wpa_ref, wpb_ref, wo_ref, gffn_ref, wr3_ref, br_ref,
                   x1_ref, h2_ref, logits_ref, kwin_ref, vwin_ref, va_ref,
                   q_s, k_s, v_s, yain_s, ga_s, gb_s, att_s):
    i = pl.program_id(0)

    @pl.when(i == 0)
    def _():
        x = x_ref[...]
        cos = jnp.broadcast_to(cos_ref[...], (DEC_BATCH, LANES))
        sin = jnp.broadcast_to(sin_ref[...], (DEC_BATCH, LANES))
        u, va, q, k, v, gate_a, gate_b = _in_projection(x, gmix_ref, win_ref, lng_ref, lnb_ref, cos, sin)
        va_ref[...] = va
        z = wdiag_ref[...].astype(F32) * va.astype(BF16).astype(F32) + bs0_ref[...]
        yain_s[...] = u * z
        q_s[...] = q
        k_s[...] = k
        v_s[...] = v
        ga_s[...] = gate_a
        gb_s[...] = gate_b

    r0 = pl.multiple_of(i * SAMPLE_STEP, SAMPLE_STEP)
    kwin = jnp.concatenate([kc_ref[:, 1:, :], k_s[pl.ds(r0, SAMPLE_STEP), :][:, None, :]], axis=1)
    vwin = jnp.concatenate([vc_ref[:, 1:, :], v_s[pl.ds(r0, SAMPLE_STEP), :][:, None, :]], axis=1)
    kwin_ref[...] = kwin
    vwin_ref[...] = vwin

    q16 = q_s[pl.ds(r0, SAMPLE_STEP), :]
    lane = lax.broadcasted_iota(I32, (SAMPLE_STEP, LANES), 1)
    heads = []
    for hq in range(N_HEADS):
        c, p, h = hq // 2, hq % 2, hq // GQA_GROUP
        chunk = q16[:, c * LANES:(c + 1) * LANES]
        if p != h:
            chunk = pltpu.roll(chunk, HEAD_DIM, 1)
        keep = (lane < HEAD_DIM) if h == 0 else (lane >= HEAD_DIM)
        heads.append(jnp.where(keep, chunk, 0.0))
    qpad = pltpu.einshape("hbd->bhd", jnp.stack(heads, axis=0)).astype(BF16)
    s = jnp.einsum("bhd,bkd->bhk", qpad, kwin.astype(BF16), preferred_element_type=F32) * (HEAD_DIM ** -0.5)
    hid = lax.broadcasted_iota(I32, (1, N_HEADS, 1), 1)
    sink = jnp.zeros((1, N_HEADS, 1), F32)
    for hq in range(N_HEADS):
        sink = jnp.where(hid == hq, sinks_ref[hq], sink)
    m = jnp.maximum(jnp.max(s, axis=-1, keepdims=True), sink)
    e = jnp.exp(s - m)
    inv = 1.0 / (jnp.sum(e, axis=-1, keepdims=True) + jnp.exp(sink - m))
    o = jnp.einsum("bhk,bkd->bhd", (e * inv).astype(BF16), vwin.astype(BF16), preferred_element_type=F32)
    o = pltpu.einshape("bhd->hbd", o)
    chunks = []
    for c in range(N_HEADS // 2):
        parts = []
        for p in range(2):
            hq = 2 * c + p
            oh = o[hq]
            if p != hq // GQA_GROUP:
                oh = pltpu.roll(oh, HEAD_DIM, 1)
            parts.append(oh)
        chunks.append(jnp.where(lane < HEAD_DIM, parts[0], parts[1]))
    att_s[pl.ds(r0, SAMPLE_STEP), :] = jnp.concatenate(chunks, axis=1)

    @pl.when(i == SAMPLE_STEPS - 1)
    def _():
        x1, h2, logits = _merge_and_route(
            x_ref[...], yain_s[...], att_s[...], ga_s[...], gb_s[...],
            wpa_ref, wpb_ref, wo_ref, gffn_ref, wr3_ref, br_ref)
        _store_token_major(x1_ref, x1)
        _store_token_major(h2_ref, h2)
        logits_ref[...] = logits


def _sample_front(x, cos, sin, gmix, win, lng, lnb, wdiag, bs0, sinks, kc, vc, wpa, wpb, wo, gffn, wrt, br):
    n = DEC_BATCH
    cache_spec = pl.BlockSpec((SAMPLE_STEP, WINDOW, KV_WIDTH), lambda i: (i, 0, 0))
    in_specs = [
        _full((n, D_MODEL)),
        _full((1, LANES)),
        _full((1, LANES)),
        _full((1, D_MODEL)),
        _full((D_MODEL, IN_COLS)),
        _full((1, A_WIDTH)),
        _full((1, A_WIDTH)),
        _full((1, A_WIDTH)),
        _full((1, A_WIDTH)),
        pl.BlockSpec(memory_space=pltpu.SMEM),
        cache_spec,
        cache_spec,
        _full((A_WIDTH, D_MODEL)),
        _full((Q_WIDTH, D_MODEL)),
        _full((D_MODEL, D_MODEL)),
        _full((1, D_MODEL)),
        _full((3 * D_MODEL, LANES)),
        _full((N_EXPERTS, 1)),
    ]
    out_shape = [
        jax.ShapeDtypeStruct((n * LANE_CHUNKS, LANES), F32),
        jax.ShapeDtypeStruct((n * LANE_CHUNKS, LANES), F32),
        jax.ShapeDtypeStruct((N_EXPERTS, n), F32),
        jax.ShapeDtypeStruct((n, WINDOW, KV_WIDTH), F32),
        jax.ShapeDtypeStruct((n, WINDOW, KV_WIDTH), F32),
        jax.ShapeDtypeStruct((n, A_WIDTH), F32),
    ]
    out_specs = [
        _full((n * LANE_CHUNKS, LANES)),
        _full((n * LANE_CHUNKS, LANES)),
        _full((N_EXPERTS, n)),
        cache_spec,
        cache_spec,
        _full((n, A_WIDTH)),
    ]
    scratch = [
        pltpu.VMEM((n, Q_WIDTH), F32), pltpu.VMEM((n, KV_WIDTH), F32), pltpu.VMEM((n, KV_WIDTH), F32),
        pltpu.VMEM((n, A_WIDTH), F32), pltpu.VMEM((n, D_MODEL), F32), pltpu.VMEM((n, D_MODEL), F32),
        pltpu.VMEM((n, Q_WIDTH), F32),
    ]
    return pl.pallas_call(
        _sample_kernel,
        grid=(SAMPLE_STEPS,),
        in_specs=in_specs,
        out_specs=out_specs,
        out_shape=out_shape,
        scratch_shapes=scratch,
        compiler_params=pltpu.CompilerParams(dimension_semantics=("arbitrary",), vmem_limit_bytes=VMEM_LIMIT),
        name="sample_front",
    )(x, cos, sin, gmix, win, lng, lnb, wdiag, bs0, sinks, kc, vc, wpa, wpb, wo, gffn, wrt, br)


def _route_plan_kernel(lp_ref, ls_ref, dest_ref, wts_ref, off_ref):
    g = pl.program_id(0)
    topi, topw = _top4_softmax(jnp.concatenate([lp_ref[...], ls_ref[...]], axis=1))
    slot = lax.broadcasted_iota(I32, (TOP_K, GROUP_SLOTS), 1)
    eall = jnp.where(jnp.logical_or(slot < GROUP_PROMPT, g == N_GROUPS - 1), topi, N_EXPERTS)
    wts_ref[:, 0:GROUP_SLOTS] = topw
    wts_ref[:, GROUP_SLOTS:] = jnp.zeros((TOP_K, K_STRIDE - GROUP_SLOTS), F32)
    dest_ref[:, GROUP_SLOTS:] = jnp.zeros((TOP_K, K_STRIDE - GROUP_SLOTS), I32)
    eid = lax.broadcasted_iota(I32, (N_EXPERTS, GROUP_SLOTS), 0)
    onehots = [eall[k:k + 1, :] == eid for k in range(TOP_K)]
    count = jnp.zeros((N_EXPERTS, GROUP_SLOTS), F32)
    for oh in onehots:
        count = count + oh.astype(F32)
    total = jnp.broadcast_to(jnp.sum(count, axis=1, keepdims=True), (N_EXPERTS, LANES))
    padded = total + (MOE_ROWS - 1)
    nblk = jnp.floor(padded * (1.0 / MOE_ROWS))
    rem = padded - nblk * MOE_ROWS
    nblk = jnp.where(rem >= MOE_ROWS, nblk + 1.0, jnp.where(rem < 0.0, nblk - 1.0, nblk))
    r = lax.broadcasted_iota(I32, (N_EXPERTS, N_EXPERTS), 0)
    c = lax.broadcasted_iota(I32, (N_EXPERTS, N_EXPERTS), 1)
    first_blk = lax.dot_general((c < r).astype(F32), nblk, (((1,), (0,)), ((), ())),
                                precision=lax.Precision.HIGHEST, preferred_element_type=F32)
    start = (first_blk + 1.0) * MOE_ROWS
    lane = lax.broadcasted_iota(I32, (N_EXPERTS, LANES), 1)
    info = jnp.where(lane == 0, start, jnp.where(lane == 1, start + total, jnp.where(lane == 2, nblk, first_blk)))
    off_ref[...] = info.astype(I32)
    ti = lax.broadcasted_iota(I32, (LANES, LANES), 0)
    tj = lax.broadcasted_iota(I32, (LANES, LANES), 1)
    before = (ti < tj).astype(BF16)
    ones = jnp.ones((LANES, LANES), BF16)
    running = start
    for t in range(SLOT_TILES):
        sl = slice(t * LANES, (t + 1) * LANES)
        cb = count[:, sl].astype(BF16)
        pos = running + _bdot(cb, before)
        rows = [jnp.sum(jnp.where(oh[:, sl], pos, 0.0), axis=0, keepdims=True) for oh in onehots]
        dest_ref[:, sl] = jnp.concatenate(rows, axis=0).astype(I32)
        running = running + _bdot(cb, ones)


def _route_plan(logits_p, logits_s):
    in_specs = [
        pl.BlockSpec((N_EXPERTS, GROUP_PROMPT), lambda g: (0, g)),
        pl.BlockSpec((N_EXPERTS, DEC_BATCH), lambda g: (0, 0)),
    ]
    out_shape = [
        jax.ShapeDtypeStruct((N_GROUPS, TOP_K, K_STRIDE), I32),
        jax.ShapeDtypeStruct((N_GROUPS, TOP_K, K_STRIDE), F32),
        jax.ShapeDtypeStruct((N_GROUPS, N_EXPERTS, LANES), I32),
    ]
    out_specs = [
        pl.BlockSpec((None, TOP_K, K_STRIDE), lambda g: (g, 0, 0)),
        pl.BlockSpec((None, TOP_K, K_STRIDE), lambda g: (g, 0, 0)),
        pl.BlockSpec((None, N_EXPERTS, LANES), lambda g: (g, 0, 0)),
    ]
    return pl.pallas_call(
        _route_plan_kernel,
        grid=(N_GROUPS,),
        in_specs=in_specs,
        out_specs=out_specs,
        out_shape=out_shape,
        compiler_params=pltpu.CompilerParams(dimension_semantics=("arbitrary",)),
        name="route_plan",
    )(logits_p, logits_s)


GROUP_ROWS = GROUP_PROMPT * LANE_CHUNKS
SAMPLE_ROWS = DEC_BATCH * LANE_CHUNKS
TRASH_SLOT = GROUP_SLOTS
BUF_ROWS = (GROUP_SLOTS + 1) * LANE_CHUNKS
SCATTER_BATCH = 8


def _moe_kernel(off_ref, desth_ref, wtsh_ref, h2p_ref, h2s_ref, x1p_ref, x1s_ref, wgu_ref, bgu_ref, wdn_ref, bdn_ref,
                x2p_ref, x2s_ref,
                h2buf, acc, wgubuf, bgubuf, wdnbuf, bdnbuf, xs0, xs1, ys0, ys1,
                dest_ref, wts_ref, src_ref, seg_expert, seg_first, blk_seg, act_sem, w_sem):
    g = pl.program_id(0)
    last = g == N_GROUPS - 1
    row0 = pl.multiple_of(g * GROUP_ROWS, GROUP_ROWS)

    def prompt_copies():
        return (pltpu.make_async_copy(h2p_ref.at[pl.ds(row0, GROUP_ROWS)], h2buf.at[pl.ds(0, GROUP_ROWS)], act_sem.at[0]),
                pltpu.make_async_copy(x1p_ref.at[pl.ds(row0, GROUP_ROWS)], acc.at[pl.ds(0, GROUP_ROWS)], act_sem.at[1]))

    def sample_copies():
        return (pltpu.make_async_copy(h2s_ref, h2buf.at[pl.ds(GROUP_ROWS, SAMPLE_ROWS)], act_sem.at[2]),
                pltpu.make_async_copy(x1s_ref, acc.at[pl.ds(GROUP_ROWS, SAMPLE_ROWS)], act_sem.at[3]))

    def weight_copies(e, slot):
        return (pltpu.make_async_copy(wgu_ref.at[e], wgubuf.at[slot], w_sem.at[0, slot]),
                pltpu.make_async_copy(bgu_ref.at[e], bgubuf.at[slot], w_sem.at[1, slot]),
                pltpu.make_async_copy(wdn_ref.at[e], wdnbuf.at[slot], w_sem.at[2, slot]),
                pltpu.make_async_copy(bdn_ref.at[e], bdnbuf.at[slot], w_sem.at[3, slot]))

    tab0 = pl.multiple_of(g * (TOP_K * K_STRIDE), TOP_K * K_STRIDE)
    table_copies = (
        pltpu.make_async_copy(desth_ref.at[pl.ds(tab0, TOP_K * K_STRIDE)], dest_ref, act_sem.at[4]),
        pltpu.make_async_copy(wtsh_ref.at[pl.ds(tab0, TOP_K * K_STRIDE)], wts_ref, act_sem.at[5]))
    for cp in table_copies:
        cp.start()

    for cp in prompt_copies():
        cp.start()

    @pl.when(last)
    def _():
        for cp in sample_copies():
            cp.start()

    trash = pl.ds(TRASH_SLOT * LANE_CHUNKS, LANE_CHUNKS)
    h2buf[trash, :] = jnp.zeros((LANE_CHUNKS, LANES), F32)
    acc[trash, :] = jnp.zeros((LANE_CHUNKS, LANES), F32)
    ys1[...] = jnp.zeros_like(ys1)

    def pad_block(pos0):
        def body(j, carry):
            for d in range(SUBLANES):
                src_ref[pos0 + j * SUBLANES + d] = TRASH_SLOT
            return carry
        lax.fori_loop(0, MOE_ROWS // SUBLANES, body, 0)

    def scan_expert(e, carry):
        nseg, nblocks = carry
        nblk = off_ref[e, 2]
        first = off_ref[e, 3]

        @pl.when(nblk > 0)
        def _():
            seg_expert[nseg] = e
            seg_first[nseg] = first
            pad_block(off_ref[e, 0] + (nblk - 1) * MOE_ROWS)

            def mark(b, c2):
                blk_seg[first + b] = nseg
                return c2
            lax.fori_loop(0, nblk, mark, 0)

        return nseg + jnp.where(nblk > 0, 1, 0), nblocks + nblk

    nseg, nblocks = lax.fori_loop(0, N_EXPERTS, scan_expert, (jnp.int32(0), jnp.int32(0)))
    pad_block(0)
    pad_block((nblocks + 1) * MOE_ROWS)
    pad_block((nblocks + 2) * MOE_ROWS)
    blk_seg[nblocks] = nseg - 1
    blk_seg[nblocks + 1] = nseg - 1

    for cp in weight_copies(seg_expert[0], 0):
        cp.start()

    for cp in table_copies:
        cp.wait()

    nvalid = jnp.where(last, GROUP_SLOTS, GROUP_PROMPT)
    for k in range(TOP_K):
        def fill(j, carry, k=k):
            c0 = k * K_STRIDE + j * SUBLANES
            for d in range(SUBLANES):
                src_ref[dest_ref[c0 + d]] = c0 + d
            return carry
        lax.fori_loop(0, nvalid // SUBLANES, fill, 0)

    for cp in prompt_copies():
        cp.wait()

    @pl.when(last)
    def _():
        for cp in sample_copies():
            cp.wait()

    def token_rows(code):
        slot_id = code & (K_STRIDE - 1)
        return pl.ds(pl.multiple_of(slot_id * LANE_CHUNKS, LANE_CHUNKS), LANE_CHUNKS)

    def gather(b, xs):
        base = (b + 1) * MOE_ROWS
        for m in range(MOE_ROWS):
            xs[pl.ds(m, LANE_CHUNKS, stride=XS_STRIDE), :] = h2buf[token_rows(src_ref[base + m]), :]

    def scatter_add(b, ys):
        base = (b + 1) * MOE_ROWS
        for m0 in range(0, MOE_ROWS, SCATTER_BATCH):
            pending = []
            for m in range(m0, m0 + SCATTER_BATCH):
                code = src_ref[base + m]
                rows = token_rows(code)
                pending.append((rows, acc[rows, :] + wts_ref[code] * ys[pl.ds(m, LANE_CHUNKS, stride=XS_STRIDE), :]))
            for rows, val in pending:
                acc[rows, :] = val

    def expert_ffn(xs, ys, slot):
        x = jnp.concatenate(
            [xs[c * XS_STRIDE:c * XS_STRIDE + MOE_ROWS, :] for c in range(LANE_CHUNKS)], axis=1).astype(BF16)
        gu = _bdot(x, wgubuf[slot]) + bgubuf[slot]
        gl = jnp.minimum(gu[:, :D_EXPERT], SWIGLU_LIMIT)
        ul = jnp.clip(gu[:, D_EXPERT:], -SWIGLU_LIMIT, SWIGLU_LIMIT)
        a = (ul + 1.0) * (gl * jax.nn.sigmoid(SWIGLU_ALPHA * gl))
        y = _bdot(a.astype(BF16), wdnbuf[slot]) + bdnbuf[slot]
        for c in range(LANE_CHUNKS):
            ys[c * XS_STRIDE:c * XS_STRIDE + MOE_ROWS, :] = y[:, c * LANES:(c + 1) * LANES]

    def step(b, xs_cur, xs_next, ys_cur, ys_prev):
        seg = blk_seg[b]
        slot = seg & 1

        @pl.when(jnp.logical_and(b == seg_first[seg], b < nblocks))
        def _():
            for cp in weight_copies(seg_expert[seg], slot):
                cp.wait()

            @pl.when(seg + 1 < nseg)
            def _():
                for cp in weight_copies(seg_expert[seg + 1], 1 - slot):
                    cp.start()

        gather(b + 1, xs_next)
        expert_ffn(xs_cur, ys_cur, slot)
        scatter_add(b - 1, ys_prev)

    gather(0, xs0)
    npairs = (nblocks + 1) // 2

    def pair(t, carry):
        step(2 * t, xs0, xs1, ys0, ys1)
        step(2 * t + 1, xs1, xs0, ys1, ys0)
        return carry

    lax.fori_loop(0, npairs, pair, 0)
    scatter_add(2 * npairs - 1, ys1)

    out_p =pltpu.make_async_copy(acc.at[pl.ds(0, GROUP_ROWS)], x2p_ref.at[pl.ds(row0, GROUP_ROWS)], act_sem.at[0])
    out_p.start()

    @pl.when(last)
    def _():
        out_s = pltpu.make_async_copy(acc.at[pl.ds(GROUP_ROWS, SAMPLE_ROWS)], x2s_ref, act_sem.at[2])
        out_s.start()
        out_s.wait()

    out_p.wait()


def _moe(dest, wts, off, h2p, h2s, x1p, x1s, wgu, bgu, wdn, bdn):
    anyspec = pl.BlockSpec(memory_space=pl.ANY)
    dest = dest.reshape(N_GROUPS * TOP_K * K_STRIDE)
    wts = wts.reshape(N_GROUPS * TOP_K * K_STRIDE)
    in_specs = [
        pl.BlockSpec((None, N_EXPERTS, LANES), lambda g: (g, 0, 0), memory_space=pltpu.SMEM),
        anyspec, anyspec, anyspec, anyspec, anyspec, anyspec, anyspec, anyspec, anyspec, anyspec,
    ]
    scratch = [
        pltpu.VMEM((BUF_ROWS, LANES), F32),
        pltpu.VMEM((BUF_ROWS, LANES), F32),
        pltpu.VMEM((2, D_MODEL, 2 * D_EXPERT), BF16),
        pltpu.VMEM((2, 1, 2 * D_EXPERT), F32),
        pltpu.VMEM((2, D_EXPERT, D_MODEL), BF16),
        pltpu.VMEM((2, 1, D_MODEL), F32),
        pltpu.VMEM((LANE_CHUNKS * XS_STRIDE, LANES), F32),
        pltpu.VMEM((LANE_CHUNKS * XS_STRIDE, LANES), F32),
        pltpu.VMEM((LANE_CHUNKS * XS_STRIDE, LANES), F32),
        pltpu.VMEM((LANE_CHUNKS * XS_STRIDE, LANES), F32),
        pltpu.SMEM((TOP_K * K_STRIDE,), I32),
        pltpu.SMEM((TOP_K * K_STRIDE,), F32),
        pltpu.SMEM((POS_TABLE,), I32),
        pltpu.SMEM((N_EXPERTS,), I32),
        pltpu.SMEM((N_EXPERTS,), I32),
        pltpu.SMEM((LANES,), I32),
        pltpu.SemaphoreType.DMA((6,)),
        pltpu.SemaphoreType.DMA((4, 2)),
    ]
    return pl.pallas_call(
        _moe_kernel,
        grid=(N_GROUPS,),
        in_specs=in_specs,
        out_specs=[anyspec, anyspec],
        out_shape=[jax.ShapeDtypeStruct(x1p.shape, F32), jax.ShapeDtypeStruct(x1s.shape, F32)],
        scratch_shapes=scratch,
        compiler_params=pltpu.CompilerParams(dimension_semantics=("arbitrary",), vmem_limit_bytes=VMEM_LIMIT),
        name="moe",
    )(off, dest, wts, h2p, h2s, x1p, x1s, wgu, bgu, wdn, bdn)


def _ple_final_kernel(x2_ref, ple_ref, wple_ref, gple_ref, wpg_ref, gfin_ref, y_ref):
    rows = y_ref.shape[0]
    x2 = _load_token_major(x2_ref, rows)
    e = _rmsnorm(_bdot(ple_ref[...].astype(BF16), wple_ref[...]), gple_ref[...])
    x3 = x2 + jax.nn.sigmoid(_bdot(x2.astype(BF16), wpg_ref[...])) * e
    y_ref[...] = _rmsnorm(x3, gfin_ref[...])


def _ple_final(x2_tm, ple, wple, gple, wpg, gfin, tile):
    n = ple.shape[0]
    return pl.pallas_call(
        _ple_final_kernel,
        grid=(n // tile,),
        in_specs=[
            pl.BlockSpec((tile * LANE_CHUNKS, LANES), lambda i: (i, 0)),
            pl.BlockSpec((tile, PLE_DIM), lambda i: (i, 0)),
            _full((PLE_DIM, D_MODEL)),
            _full((1, D_MODEL)),
            _full((D_MODEL, D_MODEL)),
            _full((1, D_MODEL)),
        ],
        out_specs=pl.BlockSpec((tile, D_MODEL), lambda i: (i, 0)),
        out_shape=jax.ShapeDtypeStruct((n, D_MODEL), F32),
        compiler_params=pltpu.CompilerParams(dimension_semantics=("arbitrary",), vmem_limit_bytes=VMEM_LIMIT),
        name="ple_final",
    )(x2_tm, ple, wple, gple, wpg, gfin)


def _rope_tables(pos):
    half = HEAD_DIM // 2
    inv = ROPE_THETA ** (-jnp.arange(half, dtype=F32) / half)
    ang = pos.astype(F32)[:, None] * inv[None, :]
    cos, sin = jnp.cos(ang), jnp.sin(ang)
    cos2 = jnp.concatenate([cos, cos, cos, cos], axis=1)
    sin2 = jnp.concatenate([-sin, sin, -sin, sin], axis=1)
    return cos2, sin2


def _router_passes(w_router):
    hi = w_router.astype(BF16)
    lo = (w_router - hi.astype(F32)).astype(BF16)
    w3 = jnp.concatenate([hi, lo, hi], axis=0)
    return jnp.pad(w3, ((0, 0), (0, LANES - N_EXPERTS)))


def _prep_weights(g_mix, w_in, a_ln_g, a_ln_b, a_ws, a_bs, w_pa, w_pb, w_o, g_ffn, w_router, b_router):
    causal = jnp.tril(jnp.ones((CHUNK, CHUNK), dtype=bool))
    return dict(
        gmix=g_mix.reshape(1, D_MODEL),
        win=w_in.astype(BF16),
        lng=a_ln_g.reshape(1, A_WIDTH),
        lnb=a_ln_b.reshape(1, A_WIDTH),
        ws=jnp.where(causal[None], a_ws, 0.0).astype(BF16),
        bsf=jnp.repeat(jnp.transpose(a_bs), A_GROUP_DIM, axis=1),
        wpa=w_pa.astype(BF16),
        wpb=w_pb.astype(BF16),
        wo=w_o.astype(BF16),
        gffn=g_ffn.reshape(1, D_MODEL),
        wrt=_router_passes(w_router),
        br=b_router.reshape(N_EXPERTS, 1),
    )


def kernel(x_prompt, x_sample, cache_win_k, cache_win_v, p_prompt, p_sample, g_mix, w_in, a_ln_g, a_ln_b, a_ws, a_bs, sinks, w_pa, w_pb, w_o, g_ffn, w_router, b_router, w_gu, b_gu, w_down, b_down, w_ple, g_ple, w_ple_gate, g_final):
    W = _prep_weights(g_mix[0], w_in[0], a_ln_g[0], a_ln_b[0], a_ws[0], a_bs[0], w_pa[0], w_pb[0], w_o[0],
                      g_ffn[0], w_router[0], b_router[0])
    cos_p, sin_p = _rope_tables(jnp.arange(SEQ, dtype=I32))
    cos_s, sin_s = _rope_tables(jnp.full((1,), PAST_LEN, I32))
    x1p, h2p, logits_p, kwin_p, vwin_p, wgu16, wdn16 = _prompt_front(
        x_prompt.reshape(N_PROMPT, D_MODEL), cos_p, sin_p, W["gmix"], W["win"], W["lng"], W["lnb"],
        W["ws"], W["bsf"], sinks[0], W["wpa"], W["wpb"], W["wo"], W["gffn"], W["wrt"], W["br"],
        w_gu[0].reshape(N_EXPERTS * D_MODEL, 2 * D_EXPERT), w_down[0].reshape(N_EXPERTS * D_EXPERT, D_MODEL))

    wdiag = jnp.repeat(a_ws[0, :, 0, 0], A_GROUP_DIM)[None, :].astype(BF16)
    bs0 = jnp.repeat(a_bs[0, :, 0], A_GROUP_DIM)[None, :]
    x1s, h2s, logits_s, kwin_s, vwin_s, va_s = _sample_front(
        x_sample.reshape(DEC_BATCH, D_MODEL), cos_s, sin_s, W["gmix"], W["win"], W["lng"], W["lnb"], wdiag, bs0,
        sinks[0], cache_win_k[0].reshape(DEC_BATCH, WINDOW, KV_WIDTH), cache_win_v[0].reshape(DEC_BATCH, WINDOW, KV_WIDTH),
        W["wpa"], W["wpb"], W["wo"], W["gffn"], W["wrt"], W["br"])

    dest, wts, off = _route_plan(logits_p, logits_s)
    x2p, x2s = _moe(dest, wts, off, h2p, h2s, x1p, x1s,
                    wgu16.reshape(N_EXPERTS, D_MODEL, 2 * D_EXPERT), b_gu[0].reshape(N_EXPERTS, 1, 2 * D_EXPERT),
                    wdn16.reshape(N_EXPERTS, D_EXPERT, D_MODEL), b_down[0].reshape(N_EXPERTS, 1, D_MODEL))

    wple = w_ple[0].astype(BF16)
    gple = g_ple[0].reshape(1, D_MODEL)
    wpg = w_ple_gate[0].astype(BF16)
    gfin = g_final.reshape(1, D_MODEL)
    y_p = _ple_final(x2p, p_prompt[0].reshape(N_PROMPT, PLE_DIM), wple, gple, wpg, gfin, TM)
    y_s = _ple_final(x2s, p_sample[0].reshape(DEC_BATCH, PLE_DIM), wple, gple, wpg, gfin, DEC_BATCH)

    return (
        y_p.reshape(BATCH, SEQ, D_MODEL),
        y_s.reshape(DEC_BATCH, 1, D_MODEL),
        kwin_p.reshape(1, BATCH, WINDOW, N_KV_HEADS, HEAD_DIM),
        vwin_p.reshape(1, BATCH, WINDOW, N_KV_HEADS, HEAD_DIM),
        kwin_s.reshape(1, DEC_BATCH, WINDOW, N_KV_HEADS, HEAD_DIM),
        vwin_s.reshape(1, DEC_BATCH, WINDOW, N_KV_HEADS, HEAD_DIM),
        va_s.reshape(1, DEC_BATCH, 1, A_WIDTH),
    )
```

```python
import functools

import jax
import jax.numpy as jnp
from jax import lax
from jax.experimental import pallas as pl
from jax.experimental.pallas import tpu as pltpu

F32 = jnp.float32
BF16 = jnp.bfloat16
I32 = jnp.int32

D_MODEL = 1024
BATCH = 4
SEQ = 4096
DEC_BATCH = 128
PAST_LEN = 8192
CHUNK = 128
A_GROUPS = 4
A_GROUP_DIM = 128
A_WIDTH = A_GROUPS * A_GROUP_DIM
N_HEADS = 8
N_KV_HEADS = 2
HEAD_DIM = 64
Q_WIDTH = N_HEADS * HEAD_DIM
KV_WIDTH = N_KV_HEADS * HEAD_DIM
GQA_GROUP = N_HEADS // N_KV_HEADS
WINDOW = 128
ROPE_THETA = 10000.0
N_EXPERTS = 32
TOP_K = 4
D_EXPERT = D_MODEL
SWIGLU_ALPHA = 1.702
SWIGLU_LIMIT = 7.0
PLE_DIM = 256
RMS_EPS = 1e-5
LN_EPS = 1e-5

O_Q = 2 * A_WIDTH
O_K = O_Q + Q_WIDTH
O_V = O_K + KV_WIDTH
O_GA = O_V + KV_WIDTH
O_GB = O_GA + D_MODEL
IN_COLS = O_GB + D_MODEL

LANES = 128
SUBLANES = 8
LANE_CHUNKS = D_MODEL // LANES
VMEM_LIMIT = 56 * 1024 * 1024

N_PROMPT = BATCH * SEQ
TM = 256
TILES_PER_SEQ = SEQ // TM
BLOCKS_PER_TILE = TM // WINDOW
FRONT_STEPS = N_PROMPT // TM
CAST_ROWS = N_EXPERTS * D_MODEL // FRONT_STEPS
CAST_SPLIT = 4

N_GROUPS = 4
GROUP_PROMPT = N_PROMPT // N_GROUPS
GROUP_SLOTS = GROUP_PROMPT + DEC_BATCH
GROUP_ASSIGN = GROUP_SLOTS * TOP_K
SLOT_TILES = GROUP_SLOTS // LANES
MOE_ROWS = 256
XS_STRIDE = MOE_ROWS + SUBLANES
SLOT_BITS = 13
K_STRIDE = 1 << SLOT_BITS
assert GROUP_SLOTS < K_STRIDE
MAX_BLOCKS = GROUP_ASSIGN // MOE_ROWS + N_EXPERTS
POS_TABLE = 1 << 15
assert (MAX_BLOCKS + 3) * MOE_ROWS <= POS_TABLE
assert MAX_BLOCKS + 2 <= LANES


def _bdot(a, b):
    return jnp.dot(a, b, preferred_element_type=F32)


def _rmsnorm(x, g):
    return x * lax.rsqrt(jnp.mean(x * x, axis=-1, keepdims=True) + RMS_EPS) * g


def _gelu(x):
    return 0.5 * x * (1.0 + lax.erf(x * (0.5 ** 0.5)))


def _group_layernorm(v, g, b):
    cols = []
    for gi in range(A_GROUPS):
        s = slice(gi * A_GROUP_DIM, (gi + 1) * A_GROUP_DIM)
        vg = v[:, s]
        mu = jnp.mean(vg, axis=-1, keepdims=True)
        d = vg - mu
        var = jnp.mean(d * d, axis=-1, keepdims=True)
        cols.append(d * lax.rsqrt(var + LN_EPS) * g[:, s] + b[:, s])
    return jnp.concatenate(cols, axis=1)


def _rope(x, cos, sin_signed):
    width = x.shape[1]
    reps = width // LANES
    cosf = jnp.concatenate([cos] * reps, axis=1) if reps > 1 else cos
    sinf = jnp.concatenate([sin_signed] * reps, axis=1) if reps > 1 else sin_signed
    half = HEAD_DIM // 2
    lane = lax.broadcasted_iota(I32, x.shape, 1)
    up = pltpu.roll(x, width - half, 1)
    down = pltpu.roll(x, half, 1)
    partner = jnp.where((lane & (HEAD_DIM - 1)) < half, up, down)
    return x * cosf + partner * sinf


def _in_projection(x, gmix_ref, win_ref, lng_ref, lnb_ref, cos, sin_signed):
    hb = _rmsnorm(x, gmix_ref[...]).astype(BF16)
    zuv = _gelu(_bdot(hb, win_ref[:, 0:O_Q]))
    u = zuv[:, :A_WIDTH]
    va = _group_layernorm(zuv[:, A_WIDTH:], lng_ref[...], lnb_ref[...])
    zqkv = _bdot(hb, win_ref[:, O_Q:O_GA])
    q = _rope(zqkv[:, :Q_WIDTH], cos, sin_signed)
    k = _rope(zqkv[:, Q_WIDTH:Q_WIDTH + KV_WIDTH], cos, sin_signed)
    v = zqkv[:, Q_WIDTH + KV_WIDTH:]
    zg = _bdot(hb, win_ref[:, O_GA:IN_COLS])
    gate_a = jax.nn.sigmoid(zg[:, :D_MODEL])
    gate_b = jax.nn.sigmoid(zg[:, D_MODEL:])
    return u, va, q, k, v, gate_a, gate_b


def _merge_and_route(x, ya_in, att, gate_a, gate_b, wpa_ref, wpb_ref, wo_ref, gffn_ref, wr3_ref, br_ref):
    ya = _bdot(ya_in.astype(BF16), wpa_ref[...])
    yb = _bdot(att.astype(BF16), wpb_ref[...])
    mix = (gate_a * ya + gate_b * yb).astype(BF16)
    x1 = x + _bdot(mix, wo_ref[...])
    h2 = _rmsnorm(x1, gffn_ref[...])
    hi = h2.astype(BF16)
    lo = (h2 - hi.astype(F32)).astype(BF16)
    logits = _bdot(jnp.concatenate([hi, hi, lo], axis=1), wr3_ref[...])
    return x1, h2, jnp.transpose(logits)[:N_EXPERTS, :] + br_ref[...]


def _top4_softmax(logits):
    eid = lax.broadcasted_iota(I32, logits.shape, 0)
    vals, idxs = [], []
    for _ in range(TOP_K):
        m = jnp.max(logits, axis=0, keepdims=True)
        idx = jnp.min(jnp.where(logits == m, eid, N_EXPERTS), axis=0, keepdims=True)
        logits = jnp.where(eid == idx, -jnp.inf, logits)
        vals.append(m)
        idxs.append(idx)
    es = [jnp.exp(v - vals[0]) for v in vals]
    inv = 1.0 / (es[0] + es[1] + es[2] + es[3])
    return jnp.concatenate(idxs, axis=0), jnp.concatenate([e * inv for e in es], axis=0)


def _store_token_major(ref, val):
    rows = val.shape[0]
    for c in range(LANE_CHUNKS):
        ref[pl.ds(c, rows, stride=LANE_CHUNKS), :] = val[:, c * LANES:(c + 1) * LANES]


def _load_token_major(ref, rows):
    return jnp.concatenate([ref[pl.ds(c, rows, stride=LANE_CHUNKS), :] for c in range(LANE_CHUNKS)], axis=1)


def _band_attention(q, k, v, k_prev, v_prev, sinks_ref, seq_start):
    unit_rows = GQA_GROUP * WINDOW
    kb = jnp.concatenate([k_prev, k], axis=0).astype(BF16)
    vb = jnp.concatenate([v_prev, v], axis=0).astype(BF16)
    qb = q.astype(BF16)
    units = [(b, h) for b in range(BLOCKS_PER_TILE) for h in range(N_KV_HEADS)]
    scores = []
    for b, h in units:
        qh = jnp.concatenate(
            [qb[b * WINDOW:(b + 1) * WINDOW, (h * GQA_GROUP + j) * HEAD_DIM:(h * GQA_GROUP + j + 1) * HEAD_DIM]
             for j in range(GQA_GROUP)], axis=0)
        kh = kb[b * WINDOW:(b + 2) * WINDOW, h * HEAD_DIM:(h + 1) * HEAD_DIM]
        scores.append(lax.dot_general(qh, kh, (((1,), (1,)), ((), ())), preferred_element_type=F32))
    s = jnp.concatenate(scores, axis=0) * (HEAD_DIM ** -0.5)
    shape = s.shape
    row = lax.broadcasted_iota(I32, shape, 0)
    qi = row & (WINDOW - 1)
    kj = lax.broadcasted_iota(I32, shape, 1)
    lo = jnp.where(jnp.logical_and(seq_start, row < N_KV_HEADS * unit_rows), WINDOW, 0)
    valid = (kj > qi) & (kj <= qi + WINDOW) & (kj >= lo)
    s = jnp.where(valid, s, -jnp.inf)
    sink = jnp.concatenate(
        [jnp.full((WINDOW, 1), sinks_ref[h * GQA_GROUP + j], F32) for b, h in units for j in range(GQA_GROUP)], axis=0)
    m = jnp.maximum(jnp.max(s, axis=-1, keepdims=True), sink)
    e = jnp.exp(s - m)
    inv = 1.0 / (jnp.sum(e, axis=-1, keepdims=True) + jnp.exp(sink - m))
    p = (e * inv).astype(BF16)
    att_rows = []
    for b in range(BLOCKS_PER_TILE):
        heads = []
        for h in range(N_KV_HEADS):
            u = b * N_KV_HEADS + h
            vh = vb[b * WINDOW:(b + 2) * WINDOW, h * HEAD_DIM:(h + 1) * HEAD_DIM]
            o = _bdot(p[u * unit_rows:(u + 1) * unit_rows], vh)
            heads.extend(o[j * WINDOW:(j + 1) * WINDOW, :] for j in range(GQA_GROUP))
        att_rows.append(jnp.concatenate(heads, axis=1))
    return jnp.concatenate(att_rows, axis=0)


def _prompt_front_kernel(x_ref, cos_ref, sin_ref, gmix_ref, win_ref, lng_ref, lnb_ref, ws_ref, bsf_ref,
                         sinks_ref, wpa_ref, wpb_ref, wo_ref, gffn_ref, wr3_ref, br_ref, wgu32_ref, wdn32_ref,
                         x1_ref, h2_ref, logits_ref, kwin_ref, vwin_ref, wgu16_ref, wdn16_ref,
                         kprev_ref, vprev_ref, gu_in, dn_in, gu_out, dn_out, cast_sem):
    i = pl.program_id(0)
    seq_start = (i % TILES_PER_SEQ) == 0

    def cast_rows(c, j):
        part = CAST_ROWS // CAST_SPLIT
        return pl.ds(pl.multiple_of(c * CAST_ROWS + j * part, part), part), pl.ds(j * part, part)

    def cast_in(c, slot):
        cps = []
        for j in range(CAST_SPLIT):
            hbm, loc = cast_rows(c, j)
            cps.append(pltpu.make_async_copy(wgu32_ref.at[hbm], gu_in.at[slot, loc], cast_sem.at[0, slot]))
            cps.append(pltpu.make_async_copy(wdn32_ref.at[hbm], dn_in.at[slot, loc], cast_sem.at[1, slot]))
        return cps

    def cast_out(c, slot):
        cps = []
        for j in range(CAST_SPLIT):
            hbm, loc = cast_rows(c, j)
            cps.append(pltpu.make_async_copy(gu_out.at[slot, loc], wgu16_ref.at[hbm], cast_sem.at[2, slot]))
            cps.append(pltpu.make_async_copy(dn_out.at[slot, loc], wdn16_ref.at[hbm], cast_sem.at[3, slot]))
        return cps

    slot = i & 1

    @pl.when(i == 0)
    def _():
        for cp in cast_in(0, 0):
            cp.start()

    @pl.when(i + 1 < FRONT_STEPS)
    def _():
        for cp in cast_in(i + 1, 1 - slot):
            cp.start()

    for cp in cast_in(i, slot):
        cp.wait()

    @pl.when(i >= 2)
    def _():
        for cp in cast_out(i - 2, slot):
            cp.wait()

    gu_out[slot] = gu_in[slot].astype(BF16)
    dn_out[slot] = dn_in[slot].astype(BF16)
    for cp in cast_out(i, slot):
        cp.start()

    @pl.when(seq_start)
    def _():
        kprev_ref[...] = jnp.zeros_like(kprev_ref)
        vprev_ref[...] = jnp.zeros_like(vprev_ref)

    x = x_ref[...]
    u, va, q, k, v, gate_a, gate_b = _in_projection(
        x, gmix_ref, win_ref, lng_ref, lnb_ref, cos_ref[...], sin_ref[...])

    att = _band_attention(q, k, v, kprev_ref[...], vprev_ref[...], sinks_ref, seq_start)
    k_last, v_last = k[TM - WINDOW:], v[TM - WINDOW:]
    kprev_ref[...] = k_last
    vprev_ref[...] = v_last
    kwin_ref[0] = k_last
    vwin_ref[0] = v_last

    vab = va.astype(BF16)
    zc = jnp.concatenate(
        [jnp.concatenate(
            [_bdot(ws_ref[g], vab[b * CHUNK:(b + 1) * CHUNK, g * A_GROUP_DIM:(g + 1) * A_GROUP_DIM])
             for g in range(A_GROUPS)], axis=1) + bsf_ref[...]
         for b in range(BLOCKS_PER_TILE)], axis=0)

    x1, h2, logits = _merge_and_route(x, u * zc, att, gate_a, gate_b,
                                      wpa_ref, wpb_ref, wo_ref, gffn_ref, wr3_ref, br_ref)
    _store_token_major(x1_ref, x1)
    _store_token_major(h2_ref, h2)
    logits_ref[...] = logits

    @pl.when(i == FRONT_STEPS - 1)
    def _():
        for cp in cast_out(i - 1, 1 - slot) + cast_out(i, slot):
            cp.wait()


def _full(shape):
    return pl.BlockSpec(shape, lambda i: (0,) * len(shape))


def _prompt_front(x, cos, sin, gmix, win, lng, lnb, ws, bsf, sinks, wpa, wpb, wo, gffn, wrt, br, wgu32, wdn32):
    n = x.shape[0]
    assert n == N_PROMPT
    grid = (FRONT_STEPS,)
    anyspec = pl.BlockSpec(memory_space=pl.ANY)
    in_specs = [
        pl.BlockSpec((TM, D_MODEL), lambda i: (i, 0)),
        pl.BlockSpec((TM, LANES), lambda i: (i % TILES_PER_SEQ, 0)),
        pl.BlockSpec((TM, LANES), lambda i: (i % TILES_PER_SEQ, 0)),
        _full((1, D_MODEL)),
        _full((D_MODEL, IN_COLS)),
        _full((1, A_WIDTH)),
        _full((1, A_WIDTH)),
        _full((A_GROUPS, CHUNK, CHUNK)),
        _full((CHUNK, A_WIDTH)),
        pl.BlockSpec(memory_space=pltpu.SMEM),
        _full((A_WIDTH, D_MODEL)),
        _full((Q_WIDTH, D_MODEL)),
        _full((D_MODEL, D_MODEL)),
        _full((1, D_MODEL)),
        _full((3 * D_MODEL, LANES)),
        _full((N_EXPERTS, 1)),
        anyspec,
        anyspec,
    ]
    out_shape = [
        jax.ShapeDtypeStruct((n * LANE_CHUNKS, LANES), F32),
        jax.ShapeDtypeStruct((n * LANE_CHUNKS, LANES), F32),
        jax.ShapeDtypeStruct((N_EXPERTS, n), F32),
        jax.ShapeDtypeStruct((n // SEQ, WINDOW, KV_WIDTH), F32),
        jax.ShapeDtypeStruct((n // SEQ, WINDOW, KV_WIDTH), F32),
        jax.ShapeDtypeStruct(wgu32.shape, BF16),
        jax.ShapeDtypeStruct(wdn32.shape, BF16),
    ]
    out_specs = [
        pl.BlockSpec((TM * LANE_CHUNKS, LANES), lambda i: (i, 0)),
        pl.BlockSpec((TM * LANE_CHUNKS, LANES), lambda i: (i, 0)),
        pl.BlockSpec((N_EXPERTS, TM), lambda i: (0, i)),
        pl.BlockSpec((1, WINDOW, KV_WIDTH), lambda i: (i // TILES_PER_SEQ, 0, 0)),
        pl.BlockSpec((1, WINDOW, KV_WIDTH), lambda i: (i // TILES_PER_SEQ, 0, 0)),
        anyspec,
        anyspec,
    ]
    scratch = [
        pltpu.VMEM((WINDOW, KV_WIDTH), F32),
        pltpu.VMEM((WINDOW, KV_WIDTH), F32),
        pltpu.VMEM((2, CAST_ROWS, 2 * D_EXPERT), F32),
        pltpu.VMEM((2, CAST_ROWS, D_MODEL), F32),
        pltpu.VMEM((2, CAST_ROWS, 2 * D_EXPERT), BF16),
        pltpu.VMEM((2, CAST_ROWS, D_MODEL), BF16),
        pltpu.SemaphoreType.DMA((4, 2)),
    ]
    return pl.pallas_call(
        _prompt_front_kernel,
        grid=grid,
        in_specs=in_specs,
        out_specs=out_specs,
        out_shape=out_shape,
        scratch_shapes=scratch,
        compiler_params=pltpu.CompilerParams(dimension_semantics=("arbitrary",), vmem_limit_bytes=VMEM_LIMIT),
        name="prompt_front",
    )(x, cos, sin, gmix, win, lng, lnb, ws, bsf, sinks, wpa, wpb, wo, gffn, wrt, br, wgu32, wdn32)


SAMPLE_STEP = 16
SAMPLE_STEPS = DEC_BATCH // SAMPLE_STEP


def _sample_kernel(x_ref, cos_ref, sin_ref, gmix_ref, win_ref, lng_ref, lnb_ref, wdiag_ref, bs0_ref, sinks_ref,
                   kc_ref, vc_ref, wpa_ref, wpb_ref, wo_ref, gffn_ref, wr3_ref, br_ref,
                   x1_ref, h2_ref, logits_ref, kwin_ref, vwin_ref, va_ref,
                   q_s, k_s, v_s, yain_s, ga_s, gb_s, att_s):
    i = pl.program_id(0)

    @pl.when(i == 0)
    def _():
        x = x_ref[...]
        cos = jnp.broadcast_to(cos_ref[...], (DEC_BATCH, LANES))
        sin = jnp.broadcast_to(sin_ref[...], (DEC_BATCH, LANES))
        u, va, q, k, v, gate_a, gate_b = _in_projection(x, gmix_ref, win_ref, lng_ref, lnb_ref, cos, sin)
        va_ref[...] = va
        z = wdiag_ref[...].astype(F32) * va.astype(BF16).astype(F32) + bs0_ref[...]
        yain_s[...] = u * z
        q_s[...] = q
        k_s[...] = k
        v_s[...] = v
        ga_s[...] = gate_a
        gb_s[...] = gate_b

    r0 = pl.multiple_of(i * SAMPLE_STEP, SAMPLE_STEP)
    kwin = jnp.concatenate([kc_ref[:, 1:, :], k_s[pl.ds(r0, SAMPLE_STEP), :][:, None, :]], axis=1)
    vwin = jnp.concatenate([vc_ref[:, 1:, :], v_s[pl.ds(r0, SAMPLE_STEP), :][:, None, :]], axis=1)
    kwin_ref[...] = kwin
    vwin_ref[...] = vwin

    q16 = q_s[pl.ds(r0, SAMPLE_STEP), :]
    lane = lax.broadcasted_iota(I32, (SAMPLE_STEP, LANES), 1)
    heads = []
    for hq in range(N_HEADS):
        c, p, h = hq // 2, hq % 2, hq // GQA_GROUP
        chunk = q16[:, c * LANES:(c + 1) * LANES]
        if p != h:
            chunk = pltpu.roll(chunk, HEAD_DIM, 1)
        keep = (lane < HEAD_DIM) if h == 0 else (lane >= HEAD_DIM)
        heads.append(jnp.where(keep, chunk, 0.0))
    qpad = pltpu.einshape("hbd->bhd", jnp.stack(heads, axis=0)).astype(BF16)
    s = jnp.einsum("bhd,bkd->bhk", qpad, kwin.astype(BF16), preferred_element_type=F32) * (HEAD_DIM ** -0.5)
    hid = lax.broadcasted_iota(I32, (1, N_HEADS, 1), 1)
    sink = jnp.zeros((1, N_HEADS, 1), F32)
    for hq in range(N_HEADS):
        sink = jnp.where(hid == hq, sinks_ref[hq], sink)
    m = jnp.maximum(jnp.max(s, axis=-1, keepdims=True), sink)
    e = jnp.exp(s - m)
    inv = 1.0 / (jnp.sum(e, axis=-1, keepdims=True) + jnp.exp(sink - m))
    o = jnp.einsum("bhk,bkd->bhd", (e * inv).astype(BF16), vwin.astype(BF16), preferred_element_type=F32)
    o = pltpu.einshape("bhd->hbd", o)
    chunks = []
    for c in range(N_HEADS // 2):
        parts = []
        for p in range(2):
            hq = 2 * c + p
            oh = o[hq]
            if p != hq // GQA_GROUP:
                oh = pltpu.roll(oh, HEAD_DIM, 1)
            parts.append(oh)
        chunks.append(jnp.where(lane < HEAD_DIM, parts[0], parts[1]))
    att_s[pl.ds(r0, SAMPLE_STEP), :] = jnp.concatenate(chunks, axis=1)

    @pl.when(i == SAMPLE_STEPS - 1)
    def _():
        x1, h2, logits = _merge_and_route(
            x_ref[...], yain_s[...], att_s[...], ga_s[...], gb_s[...],
            wpa_ref, wpb_ref, wo_ref, gffn_ref, wr3_ref, br_ref)
        _store_token_major(x1_ref, x1)
        _store_token_major(h2_ref, h2)
        logits_ref[...] = logits


def _sample_front(x, cos, sin, gmix, win, lng, lnb, wdiag, bs0, sinks, kc, vc, wpa, wpb, wo, gffn, wrt, br):
    n = DEC_BATCH
    cache_spec = pl.BlockSpec((SAMPLE_STEP, WINDOW, KV_WIDTH), lambda i: (i, 0, 0))
    in_specs = [
        _full((n, D_MODEL)),
        _full((1, LANES)),
        _full((1, LANES)),
        _full((1, D_MODEL)),
        _full((D_MODEL, IN_COLS)),
        _full((1, A_WIDTH)),
        _full((1, A_WIDTH)),
        _full((1, A_WIDTH)),
        _full((1, A_WIDTH)),
        pl.BlockSpec(memory_space=pltpu.SMEM),
        cache_spec,
        cache_spec,
        _full((A_WIDTH, D_MODEL)),
        _full((Q_WIDTH, D_MODEL)),
        _full((D_MODEL, D_MODEL)),
        _full((1, D_MODEL)),
        _full((3 * D_MODEL, LANES)),
        _full((N_EXPERTS, 1)),
    ]
    out_shape = [
        jax.ShapeDtypeStruct((n * LANE_CHUNKS, LANES), F32),
        jax.ShapeDtypeStruct((n * LANE_CHUNKS, LANES), F32),
        jax.ShapeDtypeStruct((N_EXPERTS, n), F32),
        jax.ShapeDtypeStruct((n, WINDOW, KV_WIDTH), F32),
        jax.ShapeDtypeStruct((n, WINDOW, KV_WIDTH), F32),
        jax.ShapeDtypeStruct((n, A_WIDTH), F32),
    ]
    out_specs = [
        _full((n * LANE_CHUNKS, LANES)),
        _full((n * LANE_CHUNKS, LANES)),
        _full((N_EXPERTS, n)),
        cache_spec,
        cache_spec,
        _full((n, A_WIDTH)),
    ]
    scratch = [
        pltpu.VMEM((n, Q_WIDTH), F32), pltpu.VMEM((n, KV_WIDTH), F32), pltpu.VMEM((n, KV_WIDTH), F32),
        pltpu.VMEM((n, A_WIDTH), F32), pltpu.VMEM((n, D_MODEL), F32), pltpu.VMEM((n, D_MODEL), F32),
        pltpu.VMEM((n, Q_WIDTH), F32),
    ]
    return pl.pallas_call(
        _sample_kernel,
        grid=(SAMPLE_STEPS,),
        in_specs=in_specs,
        out_specs=out_specs,
        out_shape=out_shape,
        scratch_shapes=scratch,
        compiler_params=pltpu.CompilerParams(dimension_semantics=("arbitrary",), vmem_limit_bytes=VMEM_LIMIT),
        name="sample_front",
    )(x, cos, sin, gmix, win, lng, lnb, wdiag, bs0, sinks, kc, vc, wpa, wpb, wo, gffn, wrt, br)


def _route_plan_kernel(lp_ref, ls_ref, dest_ref, wts_ref, off_ref):
    g = pl.program_id(0)
    topi, topw = _top4_softmax(jnp.concatenate([lp_ref[...], ls_ref[...]], axis=1))
    slot = lax.broadcasted_iota(I32, (TOP_K, GROUP_SLOTS), 1)
    eall = jnp.where(jnp.logical_or(slot < GROUP_PROMPT, g == N_GROUPS - 1), topi, N_EXPERTS)
    wts_ref[:, 0:GROUP_SLOTS] = topw
    wts_ref[:, GROUP_SLOTS:] = jnp.zeros((TOP_K, K_STRIDE - GROUP_SLOTS), F32)
    dest_ref[:, GROUP_SLOTS:] = jnp.zeros((TOP_K, K_STRIDE - GROUP_SLOTS), I32)
    eid = lax.broadcasted_iota(I32, (N_EXPERTS, GROUP_SLOTS), 0)
    onehots = [eall[k:k + 1, :] == eid for k in range(TOP_K)]
    count = jnp.zeros((N_EXPERTS, GROUP_SLOTS), F32)
    for oh in onehots:
        count = count + oh.astype(F32)
    total = jnp.broadcast_to(jnp.sum(count, axis=1, keepdims=True), (N_EXPERTS, LANES))
    padded = total + (MOE_ROWS - 1)
    nblk = jnp.floor(padded * (1.0 / MOE_ROWS))
    rem = padded - nblk * MOE_ROWS
    nblk = jnp.where(rem >= MOE_ROWS, nblk + 1.0, jnp.where(rem < 0.0, nblk - 1.0, nblk))
    r = lax.broadcasted_iota(I32, (N_EXPERTS, N_EXPERTS), 0)
    c = lax.broadcasted_iota(I32, (N_EXPERTS, N_EXPERTS), 1)
    first_blk = lax.dot_general((c < r).astype(F32), nblk, (((1,), (0,)), ((), ())),
                                precision=lax.Precision.HIGHEST, preferred_element_type=F32)
    start = (first_blk + 1.0) * MOE_ROWS
    lane = lax.broadcasted_iota(I32, (N_EXPERTS, LANES), 1)
    info = jnp.where(lane == 0, start, jnp.where(lane == 1, start + total, jnp.where(lane == 2, nblk, first_blk)))
    off_ref[...] = info.astype(I32)
    ti = lax.broadcasted_iota(I32, (LANES, LANES), 0)
    tj = lax.broadcasted_iota(I32, (LANES, LANES), 1)
    before = (ti < tj).astype(BF16)
    ones = jnp.ones((LANES, LANES), BF16)
    running = start
    for t in range(SLOT_TILES):
        sl = slice(t * LANES, (t + 1) * LANES)
        cb = count[:, sl].astype(BF16)
        pos = running + _bdot(cb, before)
        rows = [jnp.sum(jnp.where(oh[:, sl], pos, 0.0), axis=0, keepdims=True) for oh in onehots]
        dest_ref[:, sl] = jnp.concatenate(rows, axis=0).astype(I32)
        running = running + _bdot(cb, ones)


def _route_plan(logits_p, logits_s):
    in_specs = [
        pl.BlockSpec((N_EXPERTS, GROUP_PROMPT), lambda g: (0, g)),
        pl.BlockSpec((N_EXPERTS, DEC_BATCH), lambda g: (0, 0)),
    ]
    out_shape = [
        jax.ShapeDtypeStruct((N_GROUPS, TOP_K, K_STRIDE), I32),
        jax.ShapeDtypeStruct((N_GROUPS, TOP_K, K_STRIDE), F32),
        jax.ShapeDtypeStruct((N_GROUPS, N_EXPERTS, LANES), I32),
    ]
    out_specs = [
        pl.BlockSpec((None, TOP_K, K_STRIDE), lambda g: (g, 0, 0)),
        pl.BlockSpec((None, TOP_K, K_STRIDE), lambda g: (g, 0, 0)),
        pl.BlockSpec((None, N_EXPERTS, LANES), lambda g: (g, 0, 0)),
    ]
    return pl.pallas_call(
        _route_plan_kernel,
        grid=(N_GROUPS,),
        in_specs=in_specs,
        out_specs=out_specs,
        out_shape=out_shape,
        compiler_params=pltpu.CompilerParams(dimension_semantics=("arbitrary",)),
        name="route_plan",
    )(logits_p, logits_s)


GROUP_ROWS = GROUP_PROMPT * LANE_CHUNKS
SAMPLE_ROWS = DEC_BATCH * LANE_CHUNKS
TRASH_SLOT = GROUP_SLOTS
BUF_ROWS = (GROUP_SLOTS + 1) * LANE_CHUNKS
SCATTER_BATCH = 8
DMA_SPLIT = 8


def _moe_kernel(off_ref, desth_ref, wtsh_ref, h2p_ref, h2s_ref, x1p_ref, x1s_ref, wgu_ref, bgu_ref, wdn_ref, bdn_ref,
                x2p_ref, x2s_ref,
                h2buf, acc, wgubuf, bgubuf, wdnbuf, bdnbuf, xs0, xs1, ys0, ys1,
                dest_ref, wts_ref, src_ref, seg_expert, seg_first, blk_seg, act_sem, w_sem):
    g = pl.program_id(0)
    last = g == N_GROUPS - 1
    row0 = pl.multiple_of(g * GROUP_ROWS, GROUP_ROWS)

    def prompt_copies():
        cps = []
        for j in range(DMA_SPLIT):
            src = pl.ds(row0 + j * (GROUP_ROWS // DMA_SPLIT), GROUP_ROWS // DMA_SPLIT)
            dst = pl.ds(j * (GROUP_ROWS // DMA_SPLIT), GROUP_ROWS // DMA_SPLIT)
            cps.append(pltpu.make_async_copy(h2p_ref.at[src], h2buf.at[dst], act_sem.at[0]))
            cps.append(pltpu.make_async_copy(x1p_ref.at[src], acc.at[dst], act_sem.at[1]))
        return cps

    def sample_copies():
        return (pltpu.make_async_copy(h2s_ref, h2buf.at[pl.ds(GROUP_ROWS, SAMPLE_ROWS)], act_sem.at[2]),
                pltpu.make_async_copy(x1s_ref, acc.at[pl.ds(GROUP_ROWS, SAMPLE_ROWS)], act_sem.at[3]))

    def weight_copies(e, slot):
        cps = [pltpu.make_async_copy(bgu_ref.at[e], bgubuf.at[slot], w_sem.at[1, slot]),
               pltpu.make_async_copy(bdn_ref.at[e], bdnbuf.at[slot], w_sem.at[3, slot])]
        for j in range(DMA_SPLIT):
            rg = pl.ds(j * (D_MODEL // DMA_SPLIT), D_MODEL // DMA_SPLIT)
            rd = pl.ds(j * (D_EXPERT // DMA_SPLIT), D_EXPERT // DMA_SPLIT)
            cps.append(pltpu.make_async_copy(wgu_ref.at[e, rg], wgubuf.at[slot, rg], w_sem.at[0, slot]))
            cps.append(pltpu.make_async_copy(wdn_ref.at[e, rd], wdnbuf.at[slot, rd], w_sem.at[2, slot]))
        return cps

    def output_copies():
        return [pltpu.make_async_copy(
            acc.at[pl.ds(j * (GROUP_ROWS // DMA_SPLIT), GROUP_ROWS // DMA_SPLIT)],
            x2p_ref.at[pl.ds(row0 + j * (GROUP_ROWS // DMA_SPLIT), GROUP_ROWS // DMA_SPLIT)], act_sem.at[0])
            for j in range(DMA_SPLIT)]

    tab0 = pl.multiple_of(g * (TOP_K * K_STRIDE), TOP_K * K_STRIDE)
    table_copies = (
        pltpu.make_async_copy(desth_ref.at[pl.ds(tab0, TOP_K * K_STRIDE)], dest_ref, act_sem.at[4]),
        pltpu.make_async_copy(wtsh_ref.at[pl.ds(tab0, TOP_K * K_STRIDE)], wts_ref, act_sem.at[5]))
    for cp in table_copies:
        cp.start()

    for cp in prompt_copies():
        cp.start()

    @pl.when(last)
    def _():
        for cp in sample_copies():
            cp.start()

    trash = pl.ds(TRASH_SLOT * LANE_CHUNKS, LANE_CHUNKS)
    h2buf[trash, :] = jnp.zeros((LANE_CHUNKS, LANES), F32)
    acc[trash, :] = jnp.zeros((LANE_CHUNKS, LANES), F32)
    ys1[...] = jnp.zeros_like(ys1)

    def pad_block(pos0):
        def body(j, carry):
            for d in range(SUBLANES):
                src_ref[pos0 + j * SUBLANES + d] = TRASH_SLOT
            return carry
        lax.fori_loop(0, MOE_ROWS // SUBLANES, body, 0)

    def scan_expert(e, carry):
        nseg, nblocks = carry
        nblk = off_ref[e, 2]
        first = off_ref[e, 3]

        @pl.when(nblk > 0)
        def _():
            seg_expert[nseg] = e
            seg_first[nseg] = first
            pad_block(off_ref[e, 0] + (nblk - 1) * MOE_ROWS)

            def mark(b, c2):
                blk_seg[first + b] = nseg
                return c2
            lax.fori_loop(0, nblk, mark, 0)

        return nseg + jnp.where(nblk > 0, 1, 0), nblocks + nblk

    nseg, nblocks = lax.fori_loop(0, N_EXPERTS, scan_expert, (jnp.int32(0), jnp.int32(0)))
    pad_block(0)
    pad_block((nblocks + 1) * MOE_ROWS)
    pad_block((nblocks + 2) * MOE_ROWS)
    blk_seg[nblocks] = nseg - 1
    blk_seg[nblocks + 1] = nseg - 1

    for cp in weight_copies(seg_expert[0], 0):
        cp.start()

    for cp in table_copies:
        cp.wait()

    nvalid = jnp.where(last, GROUP_SLOTS, GROUP_PROMPT)
    for k in range(TOP_K):
        def fill(j, carry, k=k):
            c0 = k * K_STRIDE + j * SUBLANES
            for d in range(SUBLANES):
                src_ref[dest_ref[c0 + d]] = c0 + d
            return carry
        lax.fori_loop(0, nvalid // SUBLANES, fill, 0)

    for cp in prompt_copies():
        cp.wait()

    @pl.when(last)
    def _():
        for cp in sample_copies():
            cp.wait()

    def token_rows(code):
        slot_id = code & (K_STRIDE - 1)
        return pl.ds(pl.multiple_of(slot_id * LANE_CHUNKS, LANE_CHUNKS), LANE_CHUNKS)

    def gather(b, xs):
        base = (b + 1) * MOE_ROWS
        for m in range(MOE_ROWS):
            xs[pl.ds(m, LANE_CHUNKS, stride=XS_STRIDE), :] = h2buf[token_rows(src_ref[base + m]), :]

    def scatter_add(b, ys):
        base = (b + 1) * MOE_ROWS
        for m0 in range(0, MOE_ROWS, SCATTER_BATCH):
            pending = []
            for m in range(m0, m0 + SCATTER_BATCH):
                code = src_ref[base + m]
                rows = token_rows(code)
                pending.append((rows, acc[rows, :] + wts_ref[code] * ys[pl.ds(m, LANE_CHUNKS, stride=XS_STRIDE), :]))
            for rows, val in pending:
                acc[rows, :] = val

    def expert_ffn(xs, ys, slot):
        x = jnp.concatenate(
            [xs[c * XS_STRIDE:c * XS_STRIDE + MOE_ROWS, :] for c in range(LANE_CHUNKS)], axis=1).astype(BF16)
        gu = _bdot(x, wgubuf[slot]) + bgubuf[slot]
        gl = jnp.minimum(gu[:, :D_EXPERT], SWIGLU_LIMIT)
        ul = jnp.clip(gu[:, D_EXPERT:], -SWIGLU_LIMIT, SWIGLU_LIMIT)
        a = (ul + 1.0) * (gl * jax.nn.sigmoid(SWIGLU_ALPHA * gl))
        y = _bdot(a.astype(BF16), wdnbuf[slot]) + bdnbuf[slot]
        for c in range(LANE_CHUNKS):
            ys[c * XS_STRIDE:c * XS_STRIDE + MOE_ROWS, :] = y[:, c * LANES:(c + 1) * LANES]

    def step(b, xs_cur, xs_next, ys_cur, ys_prev):
        seg = blk_seg[b]
        slot = seg & 1

        @pl.when(jnp.logical_and(b == seg_first[seg], b < nblocks))
        def _():
            for cp in weight_copies(seg_expert[seg], slot):
                cp.wait()

            @pl.when(seg + 1 < nseg)
            def _():
                for cp in weight_copies(seg_expert[seg + 1], 1 - slot):
                    cp.start()

        gather(b + 1, xs_next)
        expert_ffn(xs_cur, ys_cur, slot)
        scatter_add(b - 1, ys_prev)

    gather(0, xs0)
    npairs = (nblocks + 1) // 2

    def pair(t, carry):
        step(2 * t, xs0, xs1, ys0, ys1)
        step(2 * t + 1, xs1, xs0, ys1, ys0)
        return carry

    lax.fori_loop(0, npairs, pair, 0)
    scatter_add(2 * npairs - 1, ys1)

    for cp in output_copies():
        cp.start()

    @pl.when(last)
    def _():
        out_s = pltpu.make_async_copy(acc.at[pl.ds(GROUP_ROWS, SAMPLE_ROWS)], x2s_ref, act_sem.at[2])
        out_s.start()
        out_s.wait()

    for cp in output_copies():
        cp.wait()


def _moe(dest, wts, off, h2p, h2s, x1p, x1s, wgu, bgu, wdn, bdn):
    anyspec = pl.BlockSpec(memory_space=pl.ANY)
    dest = dest.reshape(N_GROUPS * TOP_K * K_STRIDE)
    wts = wts.reshape(N_GROUPS * TOP_K * K_STRIDE)
    in_specs = [
        pl.BlockSpec((None, N_EXPERTS, LANES), lambda g: (g, 0, 0), memory_space=pltpu.SMEM),
        anyspec, anyspec, anyspec, anyspec, anyspec, anyspec, anyspec, anyspec, anyspec, anyspec,
    ]
    scratch = [
        pltpu.VMEM((BUF_ROWS, LANES), F32),
        pltpu.VMEM((BUF_ROWS, LANES), F32),
        pltpu.VMEM((2, D_MODEL, 2 * D_EXPERT), BF16),
        pltpu.VMEM((2, 1, 2 * D_EXPERT), F32),
        pltpu.VMEM((2, D_EXPERT, D_MODEL), BF16),
        pltpu.VMEM((2, 1, D_MODEL), F32),
        pltpu.VMEM((LANE_CHUNKS * XS_STRIDE, LANES), F32),
        pltpu.VMEM((LANE_CHUNKS * XS_STRIDE, LANES), F32),
        pltpu.VMEM((LANE_CHUNKS * XS_STRIDE, LANES), F32),
        pltpu.VMEM((LANE_CHUNKS * XS_STRIDE, LANES), F32),
        pltpu.SMEM((TOP_K * K_STRIDE,), I32),
        pltpu.SMEM((TOP_K * K_STRIDE,), F32),
        pltpu.SMEM((POS_TABLE,), I32),
        pltpu.SMEM((N_EXPERTS,), I32),
        pltpu.SMEM((N_EXPERTS,), I32),
        pltpu.SMEM((LANES,), I32),
        pltpu.SemaphoreType.DMA((6,)),
        pltpu.SemaphoreType.DMA((4, 2)),
    ]
    return pl.pallas_call(
        _moe_kernel,
        grid=(N_GROUPS,),
        in_specs=in_specs,
        out_specs=[anyspec, anyspec],
        out_shape=[jax.ShapeDtypeStruct(x1p.shape, F32), jax.ShapeDtypeStruct(x1s.shape, F32)],
        scratch_shapes=scratch,
        compiler_params=pltpu.CompilerParams(dimension_semantics=("arbitrary",), vmem_limit_bytes=VMEM_LIMIT),
        name="moe",
    )(off, dest, wts, h2p, h2s, x1p, x1s, wgu, bgu, wdn, bdn)


def _ple_final_kernel(x2_ref, ple_ref, wple_ref, gple_ref, wpg_ref, gfin_ref, y_ref):
    rows = y_ref.shape[0]
    x2 = _load_token_major(x2_ref, rows)
    e = _rmsnorm(_bdot(ple_ref[...].astype(BF16), wple_ref[...]), gple_ref[...])
    x3 = x2 + jax.nn.sigmoid(_bdot(x2.astype(BF16), wpg_ref[...])) * e
    y_ref[...] = _rmsnorm(x3, gfin_ref[...])


def _ple_final(x2_tm, ple, wple, gple, wpg, gfin, tile):
    n = ple.shape[0]
    return pl.pallas_call(
        _ple_final_kernel,
        grid=(n // tile,),
        in_specs=[
            pl.BlockSpec((tile * LANE_CHUNKS, LANES), lambda i: (i, 0)),
            pl.BlockSpec((tile, PLE_DIM), lambda i: (i, 0)),
            _full((PLE_DIM, D_MODEL)),
            _full((1, D_MODEL)),
            _full((D_MODEL, D_MODEL)),
            _full((1, D_MODEL)),
        ],
        out_specs=pl.BlockSpec((tile, D_MODEL), lambda i: (i, 0)),
        out_shape=jax.ShapeDtypeStruct((n, D_MODEL), F32),
        compiler_params=pltpu.CompilerParams(dimension_semantics=("arbitrary",), vmem_limit_bytes=VMEM_LIMIT),
        name="ple_final",
    )(x2_tm, ple, wple, gple, wpg, gfin)


def _rope_tables(pos):
    half = HEAD_DIM // 2
    inv = ROPE_THETA ** (-jnp.arange(half, dtype=F32) / half)
    ang = pos.astype(F32)[:, None] * inv[None, :]
    cos, sin = jnp.cos(ang), jnp.sin(ang)
    cos2 = jnp.concatenate([cos, cos, cos, cos], axis=1)
    sin2 = jnp.concatenate([-sin, sin, -sin, sin], axis=1)
    return cos2, sin2


def _router_passes(w_router):
    hi = w_router.astype(BF16)
    lo = (w_router - hi.astype(F32)).astype(BF16)
    w3 = jnp.concatenate([hi, lo, hi], axis=0)
    return jnp.pad(w3, ((0, 0), (0, LANES - N_EXPERTS)))


def _prep_weights(g_mix, w_in, a_ln_g, a_ln_b, a_ws, a_bs, w_pa, w_pb, w_o, g_ffn, w_router, b_router):
    causal = jnp.tril(jnp.ones((CHUNK, CHUNK), dtype=bool))
    return dict(
        gmix=g_mix.reshape(1, D_MODEL),
        win=w_in.astype(BF16),
        lng=a_ln_g.reshape(1, A_WIDTH),
        lnb=a_ln_b.reshape(1, A_WIDTH),
        ws=jnp.where(causal[None], a_ws, 0.0).astype(BF16),
        bsf=jnp.repeat(jnp.transpose(a_bs), A_GROUP_DIM, axis=1),
        wpa=w_pa.astype(BF16),
        wpb=w_pb.astype(BF16),
        wo=w_o.astype(BF16),
        gffn=g_ffn.reshape(1, D_MODEL),
        wrt=_router_passes(w_router),
        br=b_router.reshape(N_EXPERTS, 1),
    )


def kernel(x_prompt, x_sample, cache_win_k, cache_win_v, p_prompt, p_sample, g_mix, w_in, a_ln_g, a_ln_b, a_ws, a_bs, sinks, w_pa, w_pb, w_o, g_ffn, w_router, b_router, w_gu, b_gu, w_down, b_down, w_ple, g_ple, w_ple_gate, g_final):
    W = _prep_weights(g_mix[0], w_in[0], a_ln_g[0], a_ln_b[0], a_ws[0], a_bs[0], w_pa[0], w_pb[0], w_o[0],
                      g_ffn[0], w_router[0], b_router[0])
    cos_p, sin_p = _rope_tables(jnp.arange(SEQ, dtype=I32))
    cos_s, sin_s = _rope_tables(jnp.full((1,), PAST_LEN, I32))
    x1p, h2p, logits_p, kwin_p, vwin_p, wgu16, wdn16 = _prompt_front(
        x_prompt.reshape(N_PROMPT, D_MODEL), cos_p, sin_p, W["gmix"], W["win"], W["lng"], W["lnb"],
        W["ws"], W["bsf"], sinks[0], W["wpa"], W["wpb"], W["wo"], W["gffn"], W["wrt"], W["br"],
        w_gu[0].reshape(N_EXPERTS * D_MODEL, 2 * D_EXPERT), w_down[0].reshape(N_EXPERTS * D_EXPERT, D_MODEL))

    wdiag = jnp.repeat(a_ws[0, :, 0, 0], A_GROUP_DIM)[None, :].astype(BF16)
    bs0 = jnp.repeat(a_bs[0, :, 0], A_GROUP_DIM)[None, :]
    x1s, h2s, logits_s, kwin_s, vwin_s, va_s = _sample_front(
        x_sample.reshape(DEC_BATCH, D_MODEL), cos_s, sin_s, W["gmix"], W["win"], W["lng"], W["lnb"], wdiag, bs0,
        sinks[0], cache_win_k[0].reshape(DEC_BATCH, WINDOW, KV_WIDTH), cache_win_v[0].reshape(DEC_BATCH, WINDOW, KV_WIDTH),
        W["wpa"], W["wpb"], W["wo"], W["gffn"], W["wrt"], W["br"])

    dest, wts, off = _route_plan(logits_p, logits_s)
    x2p, x2s = _moe(dest, wts, off, h2p, h2s, x1p, x1s,
                    wgu16.reshape(N_EXPERTS, D_MODEL, 2 * D_EXPERT), b_gu[0].reshape(N_EXPERTS, 1, 2 * D_EXPERT),
                    wdn16.reshape(N_EXPERTS, D_EXPERT, D_MODEL), b_down[0].reshape(N_EXPERTS, 1, D_MODEL))

    wple = w_ple[0].astype(BF16)
    gple = g_ple[0].reshape(1, D_MODEL)
    wpg = w_ple_gate[0].astype(BF16)
    gfin = g_final.reshape(1, D_MODEL)
    y_p = _ple_final(x2p, p_prompt[0].reshape(N_PROMPT, PLE_DIM), wple, gple, wpg, gfin, TM)
    y_s = _ple_final(x2s, p_sample[0].reshape(DEC_BATCH, PLE_DIM), wple, gple, wpg, gfin, DEC_BATCH)

    return (
        y_p.reshape(BATCH, SEQ, D_MODEL),
        y_s.reshape(DEC_BATCH, 1, D_MODEL),
        kwin_p.reshape(1, BATCH, WINDOW, N_KV_HEADS, HEAD_DIM),
        vwin_p.reshape(1, BATCH, WINDOW, N_KV_HEADS, HEAD_DIM),
        kwin_s.reshape(1, DEC_BATCH, WINDOW, N_KV_HEADS, HEAD_DIM),
        vwin_s.reshape(1, DEC_BATCH, WINDOW, N_KV_HEADS, HEAD_DIM),
        va_s.reshape(1, DEC_BATCH, 1, A_WIDTH),
    )
```

```python
import functools

import jax
import jax.numpy as jnp
from jax import lax
from jax.experimental import pallas as pl
from jax.experimental.pallas import tpu as pltpu

F32 = jnp.float32
BF16 = jnp.bfloat16
I32 = jnp.int32

D_MODEL = 1024
BATCH = 4
SEQ = 4096
DEC_BATCH = 128
PAST_LEN = 8192
CHUNK = 128
A_GROUPS = 4
A_GROUP_DIM = 128
A_WIDTH = A_GROUPS * A_GROUP_DIM
N_HEADS = 8
N_KV_HEADS = 2
HEAD_DIM = 64
Q_WIDTH = N_HEADS * HEAD_DIM
KV_WIDTH = N_KV_HEADS * HEAD_DIM
GQA_GROUP = N_HEADS // N_KV_HEADS
WINDOW = 128
ROPE_THETA = 10000.0
N_EXPERTS = 32
TOP_K = 4
D_EXPERT = D_MODEL
SWIGLU_ALPHA = 1.702
SWIGLU_LIMIT = 7.0
PLE_DIM = 256
RMS_EPS = 1e-5
LN_EPS = 1e-5

LANES = 128

QPAD_WIDTH = N_HEADS * LANES
O_Q = 2 * A_WIDTH
O_K = O_Q + QPAD_WIDTH
O_V = O_K + KV_WIDTH
O_GA = O_V + KV_WIDTH
O_GB = O_GA + D_MODEL
IN_COLS = O_GB + D_MODEL
SUBLANES = 8
LANE_CHUNKS = D_MODEL // LANES
VMEM_LIMIT = 56 * 1024 * 1024

N_PROMPT = BATCH * SEQ
TM = 256
TILES_PER_SEQ = SEQ // TM
BLOCKS_PER_TILE = TM // WINDOW
FRONT_STEPS = N_PROMPT // TM
CAST_ROWS = N_EXPERTS * D_MODEL // FRONT_STEPS
CAST_SPLIT = 4

N_GROUPS = 4
GROUP_PROMPT = N_PROMPT // N_GROUPS
GROUP_SLOTS = GROUP_PROMPT + DEC_BATCH
GROUP_ASSIGN = GROUP_SLOTS * TOP_K
SLOT_TILES = GROUP_SLOTS // LANES
MOE_ROWS = 256
XS_STRIDE = MOE_ROWS + SUBLANES
SLOT_BITS = 13
K_STRIDE = 1 << SLOT_BITS
assert GROUP_SLOTS < K_STRIDE
MAX_BLOCKS = GROUP_ASSIGN // MOE_ROWS + N_EXPERTS
POS_TABLE = 1 << 15
assert (MAX_BLOCKS + 3) * MOE_ROWS <= POS_TABLE
assert MAX_BLOCKS + 2 <= LANES


def _bdot(a, b):
    return jnp.dot(a, b, preferred_element_type=F32)


def _rmsnorm(x, g):
    return x * lax.rsqrt(jnp.mean(x * x, axis=-1, keepdims=True) + RMS_EPS) * g


def _gelu(x):
    return 0.5 * x * (1.0 + lax.erf(x * (0.5 ** 0.5)))


def _group_layernorm(v, g, b):
    cols = []
    for gi in range(A_GROUPS):
        s = slice(gi * A_GROUP_DIM, (gi + 1) * A_GROUP_DIM)
        vg = v[:, s]
        mu = jnp.mean(vg, axis=-1, keepdims=True)
        d = vg - mu
        var = jnp.mean(d * d, axis=-1, keepdims=True)
        cols.append(d * lax.rsqrt(var + LN_EPS) * g[:, s] + b[:, s])
    return jnp.concatenate(cols, axis=1)


def _rope(x, cos, sin_signed):
    width = x.shape[1]
    reps = width // LANES
    cosf = jnp.concatenate([cos] * reps, axis=1) if reps > 1 else cos
    sinf = jnp.concatenate([sin_signed] * reps, axis=1) if reps > 1 else sin_signed
    half = HEAD_DIM // 2
    lane = lax.broadcasted_iota(I32, x.shape, 1)
    up = pltpu.roll(x, width - half, 1)
    down = pltpu.roll(x, half, 1)
    partner = jnp.where((lane & (HEAD_DIM - 1)) < half, up, down)
    return x * cosf + partner * sinf


def _in_projection(x, gmix_ref, win_ref, lng_ref, lnb_ref, cos, sin_signed):
    hb = _rmsnorm(x, gmix_ref[...]).astype(BF16)
    zuv = _gelu(_bdot(hb, win_ref[:, 0:O_Q]))
    u = zuv[:, :A_WIDTH]
    va = _group_layernorm(zuv[:, A_WIDTH:], lng_ref[...], lnb_ref[...])
    zqkv = _bdot(hb, win_ref[:, O_Q:O_GA])
    q = _rope(zqkv[:, :QPAD_WIDTH], cos, sin_signed)
    k = _rope(zqkv[:, QPAD_WIDTH:QPAD_WIDTH + KV_WIDTH], cos, sin_signed)
    v = zqkv[:, QPAD_WIDTH + KV_WIDTH:]
    zg = _bdot(hb, win_ref[:, O_GA:IN_COLS])
    gate_a = jax.nn.sigmoid(zg[:, :D_MODEL])
    gate_b = jax.nn.sigmoid(zg[:, D_MODEL:])
    return u, va, q, k, v, gate_a, gate_b


def _merge_and_route(x, ya_in, att, gate_a, gate_b, wpa_ref, wpb_ref, wo_ref, gffn_ref, wr3_ref, br_ref):
    ya = _bdot(ya_in.astype(BF16), wpa_ref[...])
    yb = _bdot(att.astype(BF16), wpb_ref[...])
    mix = (gate_a * ya + gate_b * yb).astype(BF16)
    x1 = x + _bdot(mix, wo_ref[...])
    h2 = _rmsnorm(x1, gffn_ref[...])
    hi = h2.astype(BF16)
    lo = (h2 - hi.astype(F32)).astype(BF16)
    logits = _bdot(jnp.concatenate([hi, hi, lo], axis=1), wr3_ref[...])
    return x1, h2, jnp.transpose(logits)[:N_EXPERTS, :] + br_ref[...]


def _top4_softmax(logits):
    eid = lax.broadcasted_iota(I32, logits.shape, 0)
    vals, idxs = [], []
    for _ in range(TOP_K):
        m = jnp.max(logits, axis=0, keepdims=True)
        idx = jnp.min(jnp.where(logits == m, eid, N_EXPERTS), axis=0, keepdims=True)
        logits = jnp.where(eid == idx, -jnp.inf, logits)
        vals.append(m)
        idxs.append(idx)
    es = [jnp.exp(v - vals[0]) for v in vals]
    inv = 1.0 / (es[0] + es[1] + es[2] + es[3])
    return jnp.concatenate(idxs, axis=0), jnp.concatenate([e * inv for e in es], axis=0)


def _store_token_major(ref, val):
    rows = val.shape[0]
    for c in range(LANE_CHUNKS):
        ref[pl.ds(c, rows, stride=LANE_CHUNKS), :] = val[:, c * LANES:(c + 1) * LANES]


def _load_token_major(ref, rows):
    return jnp.concatenate([ref[pl.ds(c, rows, stride=LANE_CHUNKS), :] for c in range(LANE_CHUNKS)], axis=1)


def _band_attention(qpad, k, v, k_prev, v_prev, sinks_ref, bias_ref, seq_start):
    kb = jnp.concatenate([k_prev, k], axis=0).astype(BF16)
    vt = jnp.transpose(jnp.concatenate([v_prev, v], axis=0)).astype(BF16)
    qb = qpad.astype(BF16)
    lane = lax.broadcasted_iota(I32, (1, GQA_GROUP * WINDOW), 1)
    blocks = []
    for b in range(BLOCKS_PER_TILE):
        bias = bias_ref[jnp.where(seq_start, 1, 0)] if b == 0 else bias_ref[0]
        keys = kb[b * WINDOW:(b + 2) * WINDOW, :]
        pieces = []
        for h in range(N_KV_HEADS):
            qh = jnp.concatenate(
                [qb[b * WINDOW:(b + 1) * WINDOW, (h * GQA_GROUP + j) * LANES:(h * GQA_GROUP + j + 1) * LANES]
                 for j in range(GQA_GROUP)], axis=0)
            st = lax.dot_general(keys, qh, (((1,), (1,)), ((), ())), preferred_element_type=F32) + bias
            sink = jnp.zeros((1, GQA_GROUP * WINDOW), F32)
            for j in range(GQA_GROUP):
                sink = jnp.where(lane // WINDOW == j, sinks_ref[h * GQA_GROUP + j], sink)
            m = jnp.maximum(jnp.max(st, axis=0, keepdims=True), sink)
            e = jnp.exp(st - m)
            inv = 1.0 / (jnp.sum(e, axis=0, keepdims=True) + jnp.exp(sink - m))
            ot = _bdot(vt[h * HEAD_DIM:(h + 1) * HEAD_DIM, b * WINDOW:(b + 2) * WINDOW], (e * inv).astype(BF16))
            pieces.extend(ot[:, j * WINDOW:(j + 1) * WINDOW] for j in range(GQA_GROUP))
        blocks.append(jnp.transpose(jnp.concatenate(pieces, axis=0)))
    return jnp.concatenate(blocks, axis=0)


def _band_bias():
    kj = lax.broadcasted_iota(I32, (2, 2 * WINDOW, GQA_GROUP * WINDOW), 1)
    qi = lax.broadcasted_iota(I32, (2, 2 * WINDOW, GQA_GROUP * WINDOW), 2) % WINDOW
    lo = lax.broadcasted_iota(I32, (2, 2 * WINDOW, GQA_GROUP * WINDOW), 0) * WINDOW
    valid = (kj > qi) & (kj <= qi + WINDOW) & (kj >= lo)
    return jnp.where(valid, 0.0, -jnp.inf).astype(F32)


def _prompt_front_kernel(x_ref, cos_ref, sin_ref, gmix_ref, win_ref, lng_ref, lnb_ref, ws_ref, bsf_ref,
                         sinks_ref, bias_ref, wpa_ref, wpb_ref, wo_ref, gffn_ref, wr3_ref, br_ref, wgu32_ref, wdn32_ref,
                         x1_ref, h2_ref, logits_ref, kwin_ref, vwin_ref, wgu16_ref, wdn16_ref,
                         kprev_ref, vprev_ref, gu_in, dn_in, gu_out, dn_out, cast_sem):
    i = pl.program_id(0)
    seq_start = (i % TILES_PER_SEQ) == 0

    def cast_rows(c, j):
        part = CAST_ROWS // CAST_SPLIT
        return pl.ds(pl.multiple_of(c * CAST_ROWS + j * part, part), part), pl.ds(j * part, part)

    def cast_in(c, slot):
        cps = []
        for j in range(CAST_SPLIT):
            hbm, loc = cast_rows(c, j)
            cps.append(pltpu.make_async_copy(wgu32_ref.at[hbm], gu_in.at[slot, loc], cast_sem.at[0, slot]))
            cps.append(pltpu.make_async_copy(wdn32_ref.at[hbm], dn_in.at[slot, loc], cast_sem.at[1, slot]))
        return cps

    def cast_out(c, slot):
        cps = []
        for j in range(CAST_SPLIT):
            hbm, loc = cast_rows(c, j)
            cps.append(pltpu.make_async_copy(gu_out.at[slot, loc], wgu16_ref.at[hbm], cast_sem.at[2, slot]))
            cps.append(pltpu.make_async_copy(dn_out.at[slot, loc], wdn16_ref.at[hbm], cast_sem.at[3, slot]))
        return cps

    slot = i & 1

    @pl.when(i == 0)
    def _():
        for cp in cast_in(0, 0):
            cp.start()

    @pl.when(i + 1 < FRONT_STEPS)
    def _():
        for cp in cast_in(i + 1, 1 - slot):
            cp.start()

    for cp in cast_in(i, slot):
        cp.wait()

    @pl.when(i >= 2)
    def _():
        for cp in cast_out(i - 2, slot):
            cp.wait()

    gu_out[slot] = gu_in[slot].astype(BF16)
    dn_out[slot] = dn_in[slot].astype(BF16)
    for cp in cast_out(i, slot):
        cp.start()

    @pl.when(seq_start)
    def _():
        kprev_ref[...] = jnp.zeros_like(kprev_ref)
        vprev_ref[...] = jnp.zeros_like(vprev_ref)

    x = x_ref[...]
    u, va, q, k, v, gate_a, gate_b = _in_projection(
        x, gmix_ref, win_ref, lng_ref, lnb_ref, cos_ref[...], sin_ref[...])

    att = _band_attention(q, k, v, kprev_ref[...], vprev_ref[...], sinks_ref, bias_ref, seq_start)
    k_last, v_last = k[TM - WINDOW:], v[TM - WINDOW:]
    kprev_ref[...] = k_last
    vprev_ref[...] = v_last
    kwin_ref[0] = k_last
    vwin_ref[0] = v_last

    vab = va.astype(BF16)
    zc = jnp.concatenate(
        [jnp.concatenate(
            [_bdot(ws_ref[g], vab[b * CHUNK:(b + 1) * CHUNK, g * A_GROUP_DIM:(g + 1) * A_GROUP_DIM])
             for g in range(A_GROUPS)], axis=1) + bsf_ref[...]
         for b in range(BLOCKS_PER_TILE)], axis=0)

    x1, h2, logits = _merge_and_route(x, u * zc, att, gate_a, gate_b,
                                      wpa_ref, wpb_ref, wo_ref, gffn_ref, wr3_ref, br_ref)
    _store_token_major(x1_ref, x1)
    _store_token_major(h2_ref, h2)
    logits_ref[...] = logits

    @pl.when(i == FRONT_STEPS - 1)
    def _():
        for cp in cast_out(i - 1, 1 - slot) + cast_out(i, slot):
            cp.wait()


def _full(shape):
    return pl.BlockSpec(shape, lambda i: (0,) * len(shape))


def _prompt_front(x, cos, sin, gmix, win, lng, lnb, ws, bsf, sinks, bias, wpa, wpb, wo, gffn, wrt, br, wgu32, wdn32):
    n = x.shape[0]
    assert n == N_PROMPT
    grid = (FRONT_STEPS,)
    anyspec = pl.BlockSpec(memory_space=pl.ANY)
    in_specs = [
        pl.BlockSpec((TM, D_MODEL), lambda i: (i, 0)),
        pl.BlockSpec((TM, LANES), lambda i: (i % TILES_PER_SEQ, 0)),
        pl.BlockSpec((TM, LANES), lambda i: (i % TILES_PER_SEQ, 0)),
        _full((1, D_MODEL)),
        _full((D_MODEL, IN_COLS)),
        _full((1, A_WIDTH)),
        _full((1, A_WIDTH)),
        _full((A_GROUPS, CHUNK, CHUNK)),
        _full((CHUNK, A_WIDTH)),
        pl.BlockSpec(memory_space=pltpu.SMEM),
        _full((2, 2 * WINDOW, GQA_GROUP * WINDOW)),
        _full((A_WIDTH, D_MODEL)),
        _full((Q_WIDTH, D_MODEL)),
        _full((D_MODEL, D_MODEL)),
        _full((1, D_MODEL)),
        _full((3 * D_MODEL, LANES)),
        _full((N_EXPERTS, 1)),
        anyspec,
        anyspec,
    ]
    out_shape = [
        jax.ShapeDtypeStruct((n * LANE_CHUNKS, LANES), F32),
        jax.ShapeDtypeStruct((n * LANE_CHUNKS, LANES), F32),
        jax.ShapeDtypeStruct((N_EXPERTS, n), F32),
        jax.ShapeDtypeStruct((n // SEQ, WINDOW, KV_WIDTH), F32),
        jax.ShapeDtypeStruct((n // SEQ, WINDOW, KV_WIDTH), F32),
        jax.ShapeDtypeStruct(wgu32.shape, BF16),
        jax.ShapeDtypeStruct(wdn32.shape, BF16),
    ]
    out_specs = [
        pl.BlockSpec((TM * LANE_CHUNKS, LANES), lambda i: (i, 0)),
        pl.BlockSpec((TM * LANE_CHUNKS, LANES), lambda i: (i, 0)),
        pl.BlockSpec((N_EXPERTS, TM), lambda i: (0, i)),
        pl.BlockSpec((1, WINDOW, KV_WIDTH), lambda i: (i // TILES_PER_SEQ, 0, 0)),
        pl.BlockSpec((1, WINDOW, KV_WIDTH), lambda i: (i // TILES_PER_SEQ, 0, 0)),
        anyspec,
        anyspec,
    ]
    scratch = [
        pltpu.VMEM((WINDOW, KV_WIDTH), F32),
        pltpu.VMEM((WINDOW, KV_WIDTH), F32),
        pltpu.VMEM((2, CAST_ROWS, 2 * D_EXPERT), F32),
        pltpu.VMEM((2, CAST_ROWS, D_MODEL), F32),
        pltpu.VMEM((2, CAST_ROWS, 2 * D_EXPERT), BF16),
        pltpu.VMEM((2, CAST_ROWS, D_MODEL), BF16),
        pltpu.SemaphoreType.DMA((4, 2)),
    ]
    return pl.pallas_call(
        _prompt_front_kernel,
        grid=grid,
        in_specs=in_specs,
        out_specs=out_specs,
        out_shape=out_shape,
        scratch_shapes=scratch,
        compiler_params=pltpu.CompilerParams(dimension_semantics=("arbitrary",), vmem_limit_bytes=VMEM_LIMIT),
        name="prompt_front",
    )(x, cos, sin, gmix, win, lng, lnb, ws, bsf, sinks, bias, wpa, wpb, wo, gffn, wrt, br, wgu32, wdn32)


SAMPLE_STEP = 16
SAMPLE_STEPS = DEC_BATCH // SAMPLE_STEP


def _sample_kernel(x_ref, cos_ref, sin_ref, gmix_ref, win_ref, lng_ref, lnb_ref, wdiag_ref, bs0_ref, sinks_ref,
                   kc_ref, vc_ref, wpa_ref, wpb_ref, wo_ref, gffn_ref, wr3_ref, br_ref,
                   x1_ref, h2_ref, logits_ref, kwin_ref, vwin_ref, va_ref,
                   q_s, k_s, v_s, yain_s, ga_s, gb_s, att_s):
    i = pl.program_id(0)

    @pl.when(i == 0)
    def _():
        x = x_ref[...]
        cos = jnp.broadcast_to(cos_ref[...], (DEC_BATCH, LANES))
        sin = jnp.broadcast_to(sin_ref[...], (DEC_BATCH, LANES))
        u, va, q, k, v, gate_a, gate_b = _in_projection(x, gmix_ref, win_ref, lng_ref, lnb_ref, cos, sin)
        va_ref[...] = va
        z = wdiag_ref[...].astype(F32) * va.astype(BF16).astype(F32) + bs0_ref[...]
        yain_s[...] = u * z
        q_s[...] = q
        k_s[...] = k
        v_s[...] = v
        ga_s[...] = gate_a
        gb_s[...] = gate_b

    r0 = pl.multiple_of(i * SAMPLE_STEP, SAMPLE_STEP)
    kwin = jnp.concatenate([kc_ref[:, 1:, :], k_s[pl.ds(r0, SAMPLE_STEP), :][:, None, :]], axis=1)
    vwin = jnp.concatenate([vc_ref[:, 1:, :], v_s[pl.ds(r0, SAMPLE_STEP), :][:, None, :]], axis=1)
    kwin_ref[...] = kwin
    vwin_ref[...] = vwin

    q16 = q_s[pl.ds(r0, SAMPLE_STEP), :]
    lane = lax.broadcasted_iota(I32, (SAMPLE_STEP, LANES), 1)
    heads = [q16[:, hq * LANES:(hq + 1) * LANES] for hq in range(N_HEADS)]
    qpad = pltpu.einshape("hbd->bhd", jnp.stack(heads, axis=0)).astype(BF16)
    s = jnp.einsum("bhd,bkd->bhk", qpad, kwin.astype(BF16), preferred_element_type=F32)
    hid = lax.broadcasted_iota(I32, (1, N_HEADS, 1), 1)
    sink = jnp.zeros((1, N_HEADS, 1), F32)
    for hq in range(N_HEADS):
        sink = jnp.where(hid == hq, sinks_ref[hq], sink)
    m = jnp.maximum(jnp.max(s, axis=-1, keepdims=True), sink)
    e = jnp.exp(s - m)
    inv = 1.0 / (jnp.sum(e, axis=-1, keepdims=True) + jnp.exp(sink - m))
    o = jnp.einsum("bhk,bkd->bhd", (e * inv).astype(BF16), vwin.astype(BF16), preferred_element_type=F32)
    o = pltpu.einshape("bhd->hbd", o)
    chunks = []
    for c in range(N_HEADS // 2):
        parts = []
        for p in range(2):
            hq = 2 * c + p
            oh = o[hq]
            if p != hq // GQA_GROUP:
                oh = pltpu.roll(oh, HEAD_DIM, 1)
            parts.append(oh)
        chunks.append(jnp.where(lane < HEAD_DIM, parts[0], parts[1]))
    att_s[pl.ds(r0, SAMPLE_STEP), :] = jnp.concatenate(chunks, axis=1)

    @pl.when(i == SAMPLE_STEPS - 1)
    def _():
        x1, h2, logits = _merge_and_route(
            x_ref[...], yain_s[...], att_s[...], ga_s[...], gb_s[...],
            wpa_ref, wpb_ref, wo_ref, gffn_ref, wr3_ref, br_ref)
        _store_token_major(x1_ref, x1)
        _store_token_major(h2_ref, h2)
        logits_ref[...] = logits


def _sample_front(x, cos, sin, gmix, win, lng, lnb, wdiag, bs0, sinks, kc, vc, wpa, wpb, wo, gffn, wrt, br):
    n = DEC_BATCH
    cache_spec = pl.BlockSpec((SAMPLE_STEP, WINDOW, KV_WIDTH), lambda i: (i, 0, 0))
    in_specs = [
        _full((n, D_MODEL)),
        _full((1, LANES)),
        _full((1, LANES)),
        _full((1, D_MODEL)),
        _full((D_MODEL, IN_COLS)),
        _full((1, A_WIDTH)),
        _full((1, A_WIDTH)),
        _full((1, A_WIDTH)),
        _full((1, A_WIDTH)),
        pl.BlockSpec(memory_space=pltpu.SMEM),
        cache_spec,
        cache_spec,
        _full((A_WIDTH, D_MODEL)),
        _full((Q_WIDTH, D_MODEL)),
        _full((D_MODEL, D_MODEL)),
        _full((1, D_MODEL)),
        _full((3 * D_MODEL, LANES)),
        _full((N_EXPERTS, 1)),
    ]
    out_shape = [
        jax.ShapeDtypeStruct((n * LANE_CHUNKS, LANES), F32),
        jax.ShapeDtypeStruct((n * LANE_CHUNKS, LANES), F32),
        jax.ShapeDtypeStruct((N_EXPERTS, n), F32),
        jax.ShapeDtypeStruct((n, WINDOW, KV_WIDTH), F32),
        jax.ShapeDtypeStruct((n, WINDOW, KV_WIDTH), F32),
        jax.ShapeDtypeStruct((n, A_WIDTH), F32),
    ]
    out_specs = [
        _full((n * LANE_CHUNKS, LANES)),
        _full((n * LANE_CHUNKS, LANES)),
        _full((N_EXPERTS, n)),
        cache_spec,
        cache_spec,
        _full((n, A_WIDTH)),
    ]
    scratch = [
        pltpu.VMEM((n, QPAD_WIDTH), F32), pltpu.VMEM((n, KV_WIDTH), F32), pltpu.VMEM((n, KV_WIDTH), F32),
        pltpu.VMEM((n, A_WIDTH), F32), pltpu.VMEM((n, D_MODEL), F32), pltpu.VMEM((n, D_MODEL), F32),
        pltpu.VMEM((n, Q_WIDTH), F32),
    ]
    return pl.pallas_call(
        _sample_kernel,
        grid=(SAMPLE_STEPS,),
        in_specs=in_specs,
        out_specs=out_specs,
        out_shape=out_shape,
        scratch_shapes=scratch,
        compiler_params=pltpu.CompilerParams(dimension_semantics=("arbitrary",), vmem_limit_bytes=VMEM_LIMIT),
        name="sample_front",
    )(x, cos, sin, gmix, win, lng, lnb, wdiag, bs0, sinks, kc, vc, wpa, wpb, wo, gffn, wrt, br)


def _route_plan_kernel(lp_ref, ls_ref, dest_ref, wts_ref, off_ref):
    g = pl.program_id(0)
    topi, topw = _top4_softmax(jnp.concatenate([lp_ref[...], ls_ref[...]], axis=1))
    slot = lax.broadcasted_iota(I32, (TOP_K, GROUP_SLOTS), 1)
    eall = jnp.where(jnp.logical_or(slot < GROUP_PROMPT, g == N_GROUPS - 1), topi, N_EXPERTS)
    wts_ref[:, 0:GROUP_SLOTS] = topw
    wts_ref[:, GROUP_SLOTS:] = jnp.zeros((TOP_K, K_STRIDE - GROUP_SLOTS), F32)
    dest_ref[:, GROUP_SLOTS:] = jnp.zeros((TOP_K, K_STRIDE - GROUP_SLOTS), I32)
    eid = lax.broadcasted_iota(I32, (N_EXPERTS, GROUP_SLOTS), 0)
    onehots = [eall[k:k + 1, :] == eid for k in range(TOP_K)]
    count = jnp.zeros((N_EXPERTS, GROUP_SLOTS), F32)
    for oh in onehots:
        count = count + oh.astype(F32)
    total = jnp.broadcast_to(jnp.sum(count, axis=1, keepdims=True), (N_EXPERTS, LANES))
    padded = total + (MOE_ROWS - 1)
    nblk = jnp.floor(padded * (1.0 / MOE_ROWS))
    rem = padded - nblk * MOE_ROWS
    nblk = jnp.where(rem >= MOE_ROWS, nblk + 1.0, jnp.where(rem < 0.0, nblk - 1.0, nblk))
    r = lax.broadcasted_iota(I32, (N_EXPERTS, N_EXPERTS), 0)
    c = lax.broadcasted_iota(I32, (N_EXPERTS, N_EXPERTS), 1)
    first_blk = lax.dot_general((c < r).astype(F32), nblk, (((1,), (0,)), ((), ())),
                                precision=lax.Precision.HIGHEST, preferred_element_type=F32)
    start = (first_blk + 1.0) * MOE_ROWS
    lane = lax.broadcasted_iota(I32, (N_EXPERTS, LANES), 1)
    info = jnp.where(lane == 0, start, jnp.where(lane == 1, start + total, jnp.where(lane == 2, nblk, first_blk)))
    off_ref[...] = info.astype(I32)
    ti = lax.broadcasted_iota(I32, (LANES, LANES), 0)
    tj = lax.broadcasted_iota(I32, (LANES, LANES), 1)
    before = (ti < tj).astype(BF16)
    ones = jnp.ones((LANES, LANES), BF16)
    running = start
    for t in range(SLOT_TILES):
        sl = slice(t * LANES, (t + 1) * LANES)
        cb = count[:, sl].astype(BF16)
        pos = running + _bdot(cb, before)
        rows = [jnp.sum(jnp.where(oh[:, sl], pos, 0.0), axis=0, keepdims=True) for oh in onehots]
        dest_ref[:, sl] = jnp.concatenate(rows, axis=0).astype(I32)
        running = running + _bdot(cb, ones)


def _route_plan(logits_p, logits_s):
    in_specs = [
        pl.BlockSpec((N_EXPERTS, GROUP_PROMPT), lambda g: (0, g)),
        pl.BlockSpec((N_EXPERTS, DEC_BATCH), lambda g: (0, 0)),
    ]
    out_shape = [
        jax.ShapeDtypeStruct((N_GROUPS, TOP_K, K_STRIDE), I32),
        jax.ShapeDtypeStruct((N_GROUPS, TOP_K, K_STRIDE), F32),
        jax.ShapeDtypeStruct((N_GROUPS, N_EXPERTS, LANES), I32),
    ]
    out_specs = [
        pl.BlockSpec((None, TOP_K, K_STRIDE), lambda g: (g, 0, 0)),
        pl.BlockSpec((None, TOP_K, K_STRIDE), lambda g: (g, 0, 0)),
        pl.BlockSpec((None, N_EXPERTS, LANES), lambda g: (g, 0, 0)),
    ]
    return pl.pallas_call(
        _route_plan_kernel,
        grid=(N_GROUPS,),
        in_specs=in_specs,
        out_specs=out_specs,
        out_shape=out_shape,
        compiler_params=pltpu.CompilerParams(dimension_semantics=("arbitrary",)),
        name="route_plan",
    )(logits_p, logits_s)


GROUP_ROWS = GROUP_PROMPT * LANE_CHUNKS
SAMPLE_ROWS = DEC_BATCH * LANE_CHUNKS
TRASH_SLOT = GROUP_SLOTS
BUF_ROWS = (GROUP_SLOTS + 1) * LANE_CHUNKS
SCATTER_BATCH = 8
DMA_SPLIT = 8


def _moe_kernel(off_ref, desth_ref, wtsh_ref, h2p_ref, h2s_ref, x1p_ref, x1s_ref, wgu_ref, bgu_ref, wdn_ref, bdn_ref,
                x2p_ref, x2s_ref,
                h2buf, acc, wgubuf, bgubuf, wdnbuf, bdnbuf, xs0, xs1, ys0, ys1,
                dest_ref, wts_ref, src_ref, seg_expert, seg_first, blk_seg, act_sem, w_sem):
    g = pl.program_id(0)
    last = g == N_GROUPS - 1
    row0 = pl.multiple_of(g * GROUP_ROWS, GROUP_ROWS)

    def prompt_copies():
        cps = []
        for j in range(DMA_SPLIT):
            src = pl.ds(row0 + j * (GROUP_ROWS // DMA_SPLIT), GROUP_ROWS // DMA_SPLIT)
            dst = pl.ds(j * (GROUP_ROWS // DMA_SPLIT), GROUP_ROWS // DMA_SPLIT)
            cps.append(pltpu.make_async_copy(h2p_ref.at[src], h2buf.at[dst], act_sem.at[0]))
            cps.append(pltpu.make_async_copy(x1p_ref.at[src], acc.at[dst], act_sem.at[1]))
        return cps

    def sample_copies():
        return (pltpu.make_async_copy(h2s_ref, h2buf.at[pl.ds(GROUP_ROWS, SAMPLE_ROWS)], act_sem.at[2]),
                pltpu.make_async_copy(x1s_ref, acc.at[pl.ds(GROUP_ROWS, SAMPLE_ROWS)], act_sem.at[3]))

    def weight_copies(e, slot):
        cps = [pltpu.make_async_copy(bgu_ref.at[e], bgubuf.at[slot], w_sem.at[1, slot]),
               pltpu.make_async_copy(bdn_ref.at[e], bdnbuf.at[slot], w_sem.at[3, slot])]
        for j in range(DMA_SPLIT):
            rg = pl.ds(j * (D_MODEL // DMA_SPLIT), D_MODEL // DMA_SPLIT)
            rd = pl.ds(j * (D_EXPERT // DMA_SPLIT), D_EXPERT // DMA_SPLIT)
            cps.append(pltpu.make_async_copy(wgu_ref.at[e, rg], wgubuf.at[slot, rg], w_sem.at[0, slot]))
            cps.append(pltpu.make_async_copy(wdn_ref.at[e, rd], wdnbuf.at[slot, rd], w_sem.at[2, slot]))
        return cps

    def output_copies():
        return [pltpu.make_async_copy(
            acc.at[pl.ds(j * (GROUP_ROWS // DMA_SPLIT), GROUP_ROWS // DMA_SPLIT)],
            x2p_ref.at[pl.ds(row0 + j * (GROUP_ROWS // DMA_SPLIT), GROUP_ROWS // DMA_SPLIT)], act_sem.at[0])
            for j in range(DMA_SPLIT)]

    tab0 = pl.multiple_of(g * (TOP_K * K_STRIDE), TOP_K * K_STRIDE)
    table_copies = (
        pltpu.make_async_copy(desth_ref.at[pl.ds(tab0, TOP_K * K_STRIDE)], dest_ref, act_sem.at[4]),
        pltpu.make_async_copy(wtsh_ref.at[pl.ds(tab0, TOP_K * K_STRIDE)], wts_ref, act_sem.at[5]))
    for cp in table_copies:
        cp.start()

    for cp in prompt_copies():
        cp.start()

    @pl.when(last)
    def _():
        for cp in sample_copies():
            cp.start()

    trash = pl.ds(TRASH_SLOT * LANE_CHUNKS, LANE_CHUNKS)
    h2buf[trash, :] = jnp.zeros((LANE_CHUNKS, LANES), F32)
    acc[trash, :] = jnp.zeros((LANE_CHUNKS, LANES), F32)
    ys1[...] = jnp.zeros_like(ys1)

    def pad_block(pos0):
        def body(j, carry):
            for d in range(SUBLANES):
                src_ref[pos0 + j * SUBLANES + d] = TRASH_SLOT
            return carry
        lax.fori_loop(0, MOE_ROWS // SUBLANES, body, 0)

    def scan_expert(e, carry):
        nseg, nblocks = carry
        nblk = off_ref[e, 2]
        first = off_ref[e, 3]

        @pl.when(nblk > 0)
        def _():
            seg_expert[nseg] = e
            seg_first[nseg] = first
            pad_block(off_ref[e, 0] + (nblk - 1) * MOE_ROWS)

            def mark(b, c2):
                blk_seg[first + b] = nseg
                return c2
            lax.fori_loop(0, nblk, mark, 0)

        return nseg + jnp.where(nblk > 0, 1, 0), nblocks + nblk

    nseg, nblocks = lax.fori_loop(0, N_EXPERTS, scan_expert, (jnp.int32(0), jnp.int32(0)))
    pad_block(0)
    pad_block((nblocks + 1) * MOE_ROWS)
    pad_block((nblocks + 2) * MOE_ROWS)
    blk_seg[nblocks] = nseg - 1
    blk_seg[nblocks + 1] = nseg - 1

    for cp in weight_copies(seg_expert[0], 0):
        cp.start()

    for cp in table_copies:
        cp.wait()

    nvalid = jnp.where(last, GROUP_SLOTS, GROUP_PROMPT)
    for k in range(TOP_K):
        def fill(j, carry, k=k):
            c0 = k * K_STRIDE + j * SUBLANES
            for d in range(SUBLANES):
                src_ref[dest_ref[c0 + d]] = c0 + d
            return carry
        lax.fori_loop(0, nvalid // SUBLANES, fill, 0)

    for cp in prompt_copies():
        cp.wait()

    @pl.when(last)
    def _():
        for cp in sample_copies():
            cp.wait()

    def token_rows(code):
        slot_id = code & (K_STRIDE - 1)
        return pl.ds(pl.multiple_of(slot_id * LANE_CHUNKS, LANE_CHUNKS), LANE_CHUNKS)

    def gather(b, xs):
        base = (b + 1) * MOE_ROWS
        for m in range(MOE_ROWS):
            xs[pl.ds(m, LANE_CHUNKS, stride=XS_STRIDE), :] = h2buf[token_rows(src_ref[base + m]), :]

    def scatter_add(b, ys):
        base = (b + 1) * MOE_ROWS
        for m0 in range(0, MOE_ROWS, SCATTER_BATCH):
            pending = []
            for m in range(m0, m0 + SCATTER_BATCH):
                code = src_ref[base + m]
                rows = token_rows(code)
                pending.append((rows, acc[rows, :] + wts_ref[code] * ys[pl.ds(m, LANE_CHUNKS, stride=XS_STRIDE), :]))
            for rows, val in pending:
                acc[rows, :] = val

    def expert_ffn(xs, ys, slot):
        x = jnp.concatenate(
            [xs[c * XS_STRIDE:c * XS_STRIDE + MOE_ROWS, :] for c in range(LANE_CHUNKS)], axis=1).astype(BF16)
        gu = _bdot(x, wgubuf[slot]) + bgubuf[slot]
        gl = jnp.minimum(gu[:, :D_EXPERT], SWIGLU_LIMIT)
        ul = jnp.clip(gu[:, D_EXPERT:], -SWIGLU_LIMIT, SWIGLU_LIMIT)
        a = (ul + 1.0) * (gl * jax.nn.sigmoid(SWIGLU_ALPHA * gl))
        y = _bdot(a.astype(BF16), wdnbuf[slot]) + bdnbuf[slot]
        for c in range(LANE_CHUNKS):
            ys[c * XS_STRIDE:c * XS_STRIDE + MOE_ROWS, :] = y[:, c * LANES:(c + 1) * LANES]

    def step(b, xs_cur, xs_next, ys_cur, ys_prev):
        seg = blk_seg[b]
        slot = seg & 1

        @pl.when(jnp.logical_and(b == seg_first[seg], b < nblocks))
        def _():
            for cp in weight_copies(seg_expert[seg], slot):
                cp.wait()

            @pl.when(seg + 1 < nseg)
            def _():
                for cp in weight_copies(seg_expert[seg + 1], 1 - slot):
                    cp.start()

        gather(b + 1, xs_next)
        expert_ffn(xs_cur, ys_cur, slot)
        scatter_add(b - 1, ys_prev)

    gather(0, xs0)
    npairs = (nblocks + 1) // 2

    def pair(t, carry):
        step(2 * t, xs0, xs1, ys0, ys1)
        step(2 * t + 1, xs1, xs0, ys1, ys0)
        return carry

    lax.fori_loop(0, npairs, pair, 0)
    scatter_add(2 * npairs - 1, ys1)

    for cp in output_copies():
        cp.start()

    @pl.when(last)
    def _():
        out_s = pltpu.make_async_copy(acc.at[pl.ds(GROUP_ROWS, SAMPLE_ROWS)], x2s_ref, act_sem.at[2])
        out_s.start()
        out_s.wait()

    for cp in output_copies():
        cp.wait()


def _moe(dest, wts, off, h2p, h2s, x1p, x1s, wgu, bgu, wdn, bdn):
    anyspec = pl.BlockSpec(memory_space=pl.ANY)
    dest = dest.reshape(N_GROUPS * TOP_K * K_STRIDE)
    wts = wts.reshape(N_GROUPS * TOP_K * K_STRIDE)
    in_specs = [
        pl.BlockSpec((None, N_EXPERTS, LANES), lambda g: (g, 0, 0), memory_space=pltpu.SMEM),
        anyspec, anyspec, anyspec, anyspec, anyspec, anyspec, anyspec, anyspec, anyspec, anyspec,
    ]
    scratch = [
        pltpu.VMEM((BUF_ROWS, LANES), F32),
        pltpu.VMEM((BUF_ROWS, LANES), F32),
        pltpu.VMEM((2, D_MODEL, 2 * D_EXPERT), BF16),
        pltpu.VMEM((2, 1, 2 * D_EXPERT), F32),
        pltpu.VMEM((2, D_EXPERT, D_MODEL), BF16),
        pltpu.VMEM((2, 1, D_MODEL), F32),
        pltpu.VMEM((LANE_CHUNKS * XS_STRIDE, LANES), F32),
        pltpu.VMEM((LANE_CHUNKS * XS_STRIDE, LANES), F32),
        pltpu.VMEM((LANE_CHUNKS * XS_STRIDE, LANES), F32),
        pltpu.VMEM((LANE_CHUNKS * XS_STRIDE, LANES), F32),
        pltpu.SMEM((TOP_K * K_STRIDE,), I32),
        pltpu.SMEM((TOP_K * K_STRIDE,), F32),
        pltpu.SMEM((POS_TABLE,), I32),
        pltpu.SMEM((N_EXPERTS,), I32),
        pltpu.SMEM((N_EXPERTS,), I32),
        pltpu.SMEM((LANES,), I32),
        pltpu.SemaphoreType.DMA((6,)),
        pltpu.SemaphoreType.DMA((4, 2)),
    ]
    return pl.pallas_call(
        _moe_kernel,
        grid=(N_GROUPS,),
        in_specs=in_specs,
        out_specs=[anyspec, anyspec],
        out_shape=[jax.ShapeDtypeStruct(x1p.shape, F32), jax.ShapeDtypeStruct(x1s.shape, F32)],
        scratch_shapes=scratch,
        compiler_params=pltpu.CompilerParams(dimension_semantics=("arbitrary",), vmem_limit_bytes=VMEM_LIMIT),
        name="moe",
    )(off, dest, wts, h2p, h2s, x1p, x1s, wgu, bgu, wdn, bdn)


def _ple_final_kernel(x2_ref, ple_ref, wple_ref, gple_ref, wpg_ref, gfin_ref, y_ref):
    rows = y_ref.shape[0]
    x2 = _load_token_major(x2_ref, rows)
    e = _rmsnorm(_bdot(ple_ref[...].astype(BF16), wple_ref[...]), gple_ref[...])
    x3 = x2 + jax.nn.sigmoid(_bdot(x2.astype(BF16), wpg_ref[...])) * e
    y_ref[...] = _rmsnorm(x3, gfin_ref[...])


def _ple_final(x2_tm, ple, wple, gple, wpg, gfin, tile):
    n = ple.shape[0]
    return pl.pallas_call(
        _ple_final_kernel,
        grid=(n // tile,),
        in_specs=[
            pl.BlockSpec((tile * LANE_CHUNKS, LANES), lambda i: (i, 0)),
            pl.BlockSpec((tile, PLE_DIM), lambda i: (i, 0)),
            _full((PLE_DIM, D_MODEL)),
            _full((1, D_MODEL)),
            _full((D_MODEL, D_MODEL)),
            _full((1, D_MODEL)),
        ],
        out_specs=pl.BlockSpec((tile, D_MODEL), lambda i: (i, 0)),
        out_shape=jax.ShapeDtypeStruct((n, D_MODEL), F32),
        compiler_params=pltpu.CompilerParams(dimension_semantics=("arbitrary",), vmem_limit_bytes=VMEM_LIMIT),
        name="ple_final",
    )(x2_tm, ple, wple, gple, wpg, gfin)


def _rope_tables(pos):
    half = HEAD_DIM // 2
    inv = ROPE_THETA ** (-jnp.arange(half, dtype=F32) / half)
    ang = pos.astype(F32)[:, None] * inv[None, :]
    cos, sin = jnp.cos(ang), jnp.sin(ang)
    cos2 = jnp.concatenate([cos, cos, cos, cos], axis=1)
    sin2 = jnp.concatenate([-sin, sin, -sin, sin], axis=1)
    return cos2, sin2


def _layout_w_in(w_in):
    o_q = 2 * A_WIDTH
    wq = w_in[:, o_q:o_q + Q_WIDTH].reshape(D_MODEL, N_HEADS, HEAD_DIM) * (HEAD_DIM ** -0.5)
    kv_head = (jnp.arange(N_HEADS) // GQA_GROUP)[None, :, None]
    wq_pad = jnp.concatenate([jnp.where(kv_head == h, wq, 0.0) for h in range(N_KV_HEADS)], axis=-1)
    return jnp.concatenate([w_in[:, :o_q], wq_pad.reshape(D_MODEL, QPAD_WIDTH), w_in[:, o_q + Q_WIDTH:]], axis=1)


def _router_passes(w_router):
    hi = w_router.astype(BF16)
    lo = (w_router - hi.astype(F32)).astype(BF16)
    w3 = jnp.concatenate([hi, lo, hi], axis=0)
    return jnp.pad(w3, ((0, 0), (0, LANES - N_EXPERTS)))


def _prep_weights(g_mix, w_in, a_ln_g, a_ln_b, a_ws, a_bs, w_pa, w_pb, w_o, g_ffn, w_router, b_router):
    causal = jnp.tril(jnp.ones((CHUNK, CHUNK), dtype=bool))
    return dict(
        gmix=g_mix.reshape(1, D_MODEL),
        win=_layout_w_in(w_in).astype(BF16),
        lng=a_ln_g.reshape(1, A_WIDTH),
        lnb=a_ln_b.reshape(1, A_WIDTH),
        ws=jnp.where(causal[None], a_ws, 0.0).astype(BF16),
        bsf=jnp.repeat(jnp.transpose(a_bs), A_GROUP_DIM, axis=1),
        wpa=w_pa.astype(BF16),
        wpb=w_pb.astype(BF16),
        wo=w_o.astype(BF16),
        gffn=g_ffn.reshape(1, D_MODEL),
        wrt=_router_passes(w_router),
        br=b_router.reshape(N_EXPERTS, 1),
    )


def kernel(x_prompt, x_sample, cache_win_k, cache_win_v, p_prompt, p_sample, g_mix, w_in, a_ln_g, a_ln_b, a_ws, a_bs, sinks, w_pa, w_pb, w_o, g_ffn, w_router, b_router, w_gu, b_gu, w_down, b_down, w_ple, g_ple, w_ple_gate, g_final):
    W = _prep_weights(g_mix[0], w_in[0], a_ln_g[0], a_ln_b[0], a_ws[0], a_bs[0], w_pa[0], w_pb[0], w_o[0],
                      g_ffn[0], w_router[0], b_router[0])
    cos_p, sin_p = _rope_tables(jnp.arange(SEQ, dtype=I32))
    cos_s, sin_s = _rope_tables(jnp.full((1,), PAST_LEN, I32))
    x1p, h2p, logits_p, kwin_p, vwin_p, wgu16, wdn16 = _prompt_front(
        x_prompt.reshape(N_PROMPT, D_MODEL), cos_p, sin_p, W["gmix"], W["win"], W["lng"], W["lnb"],
        W["ws"], W["bsf"], sinks[0], _band_bias(), W["wpa"], W["wpb"], W["wo"], W["gffn"], W["wrt"], W["br"],
        w_gu[0].reshape(N_EXPERTS * D_MODEL, 2 * D_EXPERT), w_down[0].reshape(N_EXPERTS * D_EXPERT, D_MODEL))

    wdiag = jnp.repeat(a_ws[0, :, 0, 0], A_GROUP_DIM)[None, :].astype(BF16)
    bs0 = jnp.repeat(a_bs[0, :, 0], A_GROUP_DIM)[None, :]
    x1s, h2s, logits_s, kwin_s, vwin_s, va_s = _sample_front(
        x_sample.reshape(DEC_BATCH, D_MODEL), cos_s, sin_s, W["gmix"], W["win"], W["lng"], W["lnb"], wdiag, bs0,
        sinks[0], cache_win_k[0].reshape(DEC_BATCH, WINDOW, KV_WIDTH), cache_win_v[0].reshape(DEC_BATCH, WINDOW, KV_WIDTH),
        W["wpa"], W["wpb"], W["wo"], W["gffn"], W["wrt"], W["br"])

    dest, wts, off = _route_plan(logits_p, logits_s)
    x2p, x2s = _moe(dest, wts, off, h2p, h2s, x1p, x1s,
                    wgu16.reshape(N_EXPERTS, D_MODEL, 2 * D_EXPERT), b_gu[0].reshape(N_EXPERTS, 1, 2 * D_EXPERT),
                    wdn16.reshape(N_EXPERTS, D_EXPERT, D_MODEL), b_down[0].reshape(N_EXPERTS, 1, D_MODEL))

    wple = w_ple[0].astype(BF16)
    gple = g_ple[0].reshape(1, D_MODEL)
    wpg = w_ple_gate[0].astype(BF16)
    gfin = g_final.reshape(1, D_MODEL)
    y_p = _ple_final(x2p, p_prompt[0].reshape(N_PROMPT, PLE_DIM), wple, gple, wpg, gfin, TM)
    y_s = _ple_final(x2s, p_sample[0].reshape(DEC_BATCH, PLE_DIM), wple, gple, wpg, gfin, DEC_BATCH)

    return (
        y_p.reshape(BATCH, SEQ, D_MODEL),
        y_s.reshape(DEC_BATCH, 1, D_MODEL),
        kwin_p.reshape(1, BATCH, WINDOW, N_KV_HEADS, HEAD_DIM),
        vwin_p.reshape(1, BATCH, WINDOW, N_KV_HEADS, HEAD_DIM),
        kwin_s.reshape(1, DEC_BATCH, WINDOW, N_KV_HEADS, HEAD_DIM),
        vwin_s.reshape(1, DEC_BATCH, WINDOW, N_KV_HEADS, HEAD_DIM),
        va_s.reshape(1, DEC_BATCH, 1, A_WIDTH),
    )
```

```python
import functools

import jax
import jax.numpy as jnp
from jax import lax
from jax.experimental import pallas as pl
from jax.experimental.pallas import tpu as pltpu

F32 = jnp.float32
BF16 = jnp.bfloat16
I32 = jnp.int32

D_MODEL = 1024
BATCH = 4
SEQ = 4096
DEC_BATCH = 128
PAST_LEN = 8192
CHUNK = 128
A_GROUPS = 4
A_GROUP_DIM = 128
A_WIDTH = A_GROUPS * A_GROUP_DIM
N_HEADS = 8
N_KV_HEADS = 2
HEAD_DIM = 64
Q_WIDTH = N_HEADS * HEAD_DIM
KV_WIDTH = N_KV_HEADS * HEAD_DIM
GQA_GROUP = N_HEADS // N_KV_HEADS
WINDOW = 128
ROPE_THETA = 10000.0
N_EXPERTS = 32
TOP_K = 4
D_EXPERT = D_MODEL
SWIGLU_ALPHA = 1.702
SWIGLU_LIMIT = 7.0
PLE_DIM = 256
RMS_EPS = 1e-5
LN_EPS = 1e-5

LANES = 128

QPAD_WIDTH = N_HEADS * LANES
O_Q = 2 * A_WIDTH
O_K = O_Q + QPAD_WIDTH
O_V = O_K + KV_WIDTH
O_GA = O_V + KV_WIDTH
O_GB = O_GA + D_MODEL
IN_COLS = O_GB + D_MODEL
SUBLANES = 8
LANE_CHUNKS = D_MODEL // LANES
VMEM_LIMIT = 56 * 1024 * 1024

N_PROMPT = BATCH * SEQ
TM = 256
TILES_PER_SEQ = SEQ // TM
BLOCKS_PER_TILE = TM // WINDOW
FRONT_STEPS = N_PROMPT // TM
CAST_ROWS = N_EXPERTS * D_MODEL // FRONT_STEPS
CAST_SPLIT = 4

N_GROUPS = 4
GROUP_PROMPT = N_PROMPT // N_GROUPS
GROUP_SLOTS = GROUP_PROMPT + DEC_BATCH
GROUP_ASSIGN = GROUP_SLOTS * TOP_K
SLOT_TILES = GROUP_SLOTS // LANES
MOE_ROWS = 256
XS_STRIDE = MOE_ROWS + SUBLANES
SLOT_BITS = 13
K_STRIDE = 1 << SLOT_BITS
assert GROUP_SLOTS < K_STRIDE
MAX_BLOCKS = GROUP_ASSIGN // MOE_ROWS + N_EXPERTS
POS_TABLE = 1 << 15
assert (MAX_BLOCKS + 3) * MOE_ROWS <= POS_TABLE
assert MAX_BLOCKS + 2 <= LANES


def _bdot(a, b):
    return jnp.dot(a, b, preferred_element_type=F32)


def _rmsnorm(x, g):
    return x * lax.rsqrt(jnp.mean(x * x, axis=-1, keepdims=True) + RMS_EPS) * g


def _gelu(x):
    return 0.5 * x * (1.0 + lax.erf(x * (0.5 ** 0.5)))


def _group_layernorm(v, g, b):
    cols = []
    for gi in range(A_GROUPS):
        s = slice(gi * A_GROUP_DIM, (gi + 1) * A_GROUP_DIM)
        vg = v[:, s]
        mu = jnp.mean(vg, axis=-1, keepdims=True)
        d = vg - mu
        var = jnp.mean(d * d, axis=-1, keepdims=True)
        cols.append(d * lax.rsqrt(var + LN_EPS) * g[:, s] + b[:, s])
    return jnp.concatenate(cols, axis=1)


def _rope(x, cos, sin_signed):
    width = x.shape[1]
    reps = width // LANES
    cosf = jnp.concatenate([cos] * reps, axis=1) if reps > 1 else cos
    sinf = jnp.concatenate([sin_signed] * reps, axis=1) if reps > 1 else sin_signed
    half = HEAD_DIM // 2
    lane = lax.broadcasted_iota(I32, x.shape, 1)
    up = pltpu.roll(x, width - half, 1)
    down = pltpu.roll(x, half, 1)
    partner = jnp.where((lane & (HEAD_DIM - 1)) < half, up, down)
    return x * cosf + partner * sinf


def _in_projection(x, gmix_ref, win_ref, lng_ref, lnb_ref, cos, sin_signed):
    hb = _rmsnorm(x, gmix_ref[...]).astype(BF16)
    zuv = _gelu(_bdot(hb, win_ref[:, 0:O_Q]))
    u = zuv[:, :A_WIDTH]
    va = _group_layernorm(zuv[:, A_WIDTH:], lng_ref[...], lnb_ref[...])
    zqkv = _bdot(hb, win_ref[:, O_Q:O_GA])
    q = _rope(zqkv[:, :QPAD_WIDTH], cos, sin_signed)
    k = _rope(zqkv[:, QPAD_WIDTH:QPAD_WIDTH + KV_WIDTH], cos, sin_signed)
    v = zqkv[:, QPAD_WIDTH + KV_WIDTH:]
    zg = _bdot(hb, win_ref[:, O_GA:IN_COLS])
    gate_a = jax.nn.sigmoid(zg[:, :D_MODEL])
    gate_b = jax.nn.sigmoid(zg[:, D_MODEL:])
    return u, va, q, k, v, gate_a, gate_b


def _merge_and_route(x, ya_in, att, gate_a, gate_b, wpa_ref, wpb_ref, wo_ref, gffn_ref, wr3_ref, br_ref):
    ya = _bdot(ya_in.astype(BF16), wpa_ref[...])
    yb = _bdot(att.astype(BF16), wpb_ref[...])
    mix = (gate_a * ya + gate_b * yb).astype(BF16)
    x1 = x + _bdot(mix, wo_ref[...])
    h2 = _rmsnorm(x1, gffn_ref[...])
    hi = h2.astype(BF16)
    lo = (h2 - hi.astype(F32)).astype(BF16)
    logits = _bdot(jnp.concatenate([hi, hi, lo], axis=1), wr3_ref[...])
    return x1, h2, jnp.transpose(logits)[:N_EXPERTS, :] + br_ref[...]


def _top4_softmax(logits):
    eid = lax.broadcasted_iota(I32, logits.shape, 0)
    vals, idxs = [], []
    for _ in range(TOP_K):
        m = jnp.max(logits, axis=0, keepdims=True)
        idx = jnp.min(jnp.where(logits == m, eid, N_EXPERTS), axis=0, keepdims=True)
        logits = jnp.where(eid == idx, -jnp.inf, logits)
        vals.append(m)
        idxs.append(idx)
    es = [jnp.exp(v - vals[0]) for v in vals]
    inv = 1.0 / (es[0] + es[1] + es[2] + es[3])
    return jnp.concatenate(idxs, axis=0), jnp.concatenate([e * inv for e in es], axis=0)


def _store_token_major(ref, val):
    rows = val.shape[0]
    for c in range(LANE_CHUNKS):
        ref[pl.ds(c, rows, stride=LANE_CHUNKS), :] = val[:, c * LANES:(c + 1) * LANES]


def _load_token_major(ref, rows):
    return jnp.concatenate([ref[pl.ds(c, rows, stride=LANE_CHUNKS), :] for c in range(LANE_CHUNKS)], axis=1)


def _band_attention(qpad, k, v, k_prev, v_prev, sinks_ref, bias_ref, seq_start):
    kb = jnp.concatenate([k_prev, k], axis=0).astype(BF16)
    vt = jnp.transpose(jnp.concatenate([v_prev, v], axis=0)).astype(BF16)
    qb = qpad.astype(BF16)
    lane = lax.broadcasted_iota(I32, (1, GQA_GROUP * WINDOW), 1)
    blocks = []
    for b in range(BLOCKS_PER_TILE):
        bias = bias_ref[jnp.where(seq_start, 1, 0)] if b == 0 else bias_ref[0]
        keys = kb[b * WINDOW:(b + 2) * WINDOW, :]
        pieces = []
        for h in range(N_KV_HEADS):
            qh = jnp.concatenate(
                [qb[b * WINDOW:(b + 1) * WINDOW, (h * GQA_GROUP + j) * LANES:(h * GQA_GROUP + j + 1) * LANES]
                 for j in range(GQA_GROUP)], axis=0)
            st = lax.dot_general(keys, qh, (((1,), (1,)), ((), ())), preferred_element_type=F32) + bias
            sink = jnp.zeros((1, GQA_GROUP * WINDOW), F32)
            for j in range(GQA_GROUP):
                sink = jnp.where(lane // WINDOW == j, sinks_ref[h * GQA_GROUP + j], sink)
            m = jnp.maximum(jnp.max(st, axis=0, keepdims=True), sink)
            e = jnp.exp(st - m)
            inv = 1.0 / (jnp.sum(e, axis=0, keepdims=True) + jnp.exp(sink - m))
            ot = _bdot(vt[h * HEAD_DIM:(h + 1) * HEAD_DIM, b * WINDOW:(b + 2) * WINDOW], (e * inv).astype(BF16))
            pieces.extend(ot[:, j * WINDOW:(j + 1) * WINDOW] for j in range(GQA_GROUP))
        blocks.append(jnp.transpose(jnp.concatenate(pieces, axis=0)))
    return jnp.concatenate(blocks, axis=0)


def _band_bias():
    kj = lax.broadcasted_iota(I32, (2, 2 * WINDOW, GQA_GROUP * WINDOW), 1)
    qi = lax.broadcasted_iota(I32, (2, 2 * WINDOW, GQA_GROUP * WINDOW), 2) % WINDOW
    lo = lax.broadcasted_iota(I32, (2, 2 * WINDOW, GQA_GROUP * WINDOW), 0) * WINDOW
    valid = (kj > qi) & (kj <= qi + WINDOW) & (kj >= lo)
    return jnp.where(valid, 0.0, -jnp.inf).astype(F32)


def _prompt_front_kernel(x_ref, cos_ref, sin_ref, gmix_ref, win_ref, lng_ref, lnb_ref, ws_ref, bsf_ref,
                         sinks_ref, bias_ref, wpa_ref, wpb_ref, wo_ref, gffn_ref, wr3_ref, br_ref, wgu32_ref, wdn32_ref,
                         x1_ref, h2_ref, logits_ref, kwin_ref, vwin_ref, wgu16_ref, wdn16_ref,
                         kprev_ref, vprev_ref, gu_in, dn_in, gu_out, dn_out, cast_sem):
    i = pl.program_id(0)
    seq_start = (i % TILES_PER_SEQ) == 0

    def cast_rows(c, j):
        part = CAST_ROWS // CAST_SPLIT
        return pl.ds(pl.multiple_of(c * CAST_ROWS + j * part, part), part), pl.ds(j * part, part)

    def cast_in(c, slot):
        cps = []
        for j in range(CAST_SPLIT):
            hbm, loc = cast_rows(c, j)
            cps.append(pltpu.make_async_copy(wgu32_ref.at[hbm], gu_in.at[slot, loc], cast_sem.at[0, slot]))
            cps.append(pltpu.make_async_copy(wdn32_ref.at[hbm], dn_in.at[slot, loc], cast_sem.at[1, slot]))
        return cps

    def cast_out(c, slot):
        cps = []
        for j in range(CAST_SPLIT):
            hbm, loc = cast_rows(c, j)
            cps.append(pltpu.make_async_copy(gu_out.at[slot, loc], wgu16_ref.at[hbm], cast_sem.at[2, slot]))
            cps.append(pltpu.make_async_copy(dn_out.at[slot, loc], wdn16_ref.at[hbm], cast_sem.at[3, slot]))
        return cps

    slot = i & 1

    @pl.when(i == 0)
    def _():
        for cp in cast_in(0, 0):
            cp.start()

    @pl.when(i + 1 < FRONT_STEPS)
    def _():
        for cp in cast_in(i + 1, 1 - slot):
            cp.start()

    for cp in cast_in(i, slot):
        cp.wait()

    @pl.when(i >= 2)
    def _():
        for cp in cast_out(i - 2, slot):
            cp.wait()


    @pl.when(seq_start)
    def _():
        kprev_ref[...] = jnp.zeros_like(kprev_ref)
        vprev_ref[...] = jnp.zeros_like(vprev_ref)

    x = x_ref[...]
    u, va, q, k, v, gate_a, gate_b = _in_projection(
        x, gmix_ref, win_ref, lng_ref, lnb_ref, cos_ref[...], sin_ref[...])

    att = _band_attention(q, k, v, kprev_ref[...], vprev_ref[...], sinks_ref, bias_ref, seq_start)

    gu_out[slot] = gu_in[slot].astype(BF16)
    dn_out[slot] = dn_in[slot].astype(BF16)
    k_last, v_last = k[TM - WINDOW:], v[TM - WINDOW:]
    kprev_ref[...] = k_last
    vprev_ref[...] = v_last
    kwin_ref[0] = k_last
    vwin_ref[0] = v_last

    vab = va.astype(BF16)
    zc = jnp.concatenate(
        [jnp.concatenate(
            [_bdot(ws_ref[g], vab[b * CHUNK:(b + 1) * CHUNK, g * A_GROUP_DIM:(g + 1) * A_GROUP_DIM])
             for g in range(A_GROUPS)], axis=1) + bsf_ref[...]
         for b in range(BLOCKS_PER_TILE)], axis=0)

    x1, h2, logits = _merge_and_route(x, u * zc, att, gate_a, gate_b,
                                      wpa_ref, wpb_ref, wo_ref, gffn_ref, wr3_ref, br_ref)
    _store_token_major(x1_ref, x1)
    _store_token_major(h2_ref, h2)
    logits_ref[...] = logits

    for cp in cast_out(i, slot):
        cp.start()

    @pl.when(i == FRONT_STEPS - 1)
    def _():
        for cp in cast_out(i - 1, 1 - slot) + cast_out(i, slot):
            cp.wait()


def _full(shape):
    return pl.BlockSpec(shape, lambda i: (0,) * len(shape))


def _prompt_front(x, cos, sin, gmix, win, lng, lnb, ws, bsf, sinks, bias, wpa, wpb, wo, gffn, wrt, br, wgu32, wdn32):
    n = x.shape[0]
    assert n == N_PROMPT
    grid = (FRONT_STEPS,)
    anyspec = pl.BlockSpec(memory_space=pl.ANY)
    in_specs = [
        pl.BlockSpec((TM, D_MODEL), lambda i: (i, 0)),
        pl.BlockSpec((TM, LANES), lambda i: (i % TILES_PER_SEQ, 0)),
        pl.BlockSpec((TM, LANES), lambda i: (i % TILES_PER_SEQ, 0)),
        _full((1, D_MODEL)),
        _full((D_MODEL, IN_COLS)),
        _full((1, A_WIDTH)),
        _full((1, A_WIDTH)),
        _full((A_GROUPS, CHUNK, CHUNK)),
        _full((CHUNK, A_WIDTH)),
        pl.BlockSpec(memory_space=pltpu.SMEM),
        _full((2, 2 * WINDOW, GQA_GROUP * WINDOW)),
        _full((A_WIDTH, D_MODEL)),
        _full((Q_WIDTH, D_MODEL)),
        _full((D_MODEL, D_MODEL)),
        _full((1, D_MODEL)),
        _full((3 * D_MODEL, LANES)),
        _full((N_EXPERTS, 1)),
        anyspec,
        anyspec,
    ]
    out_shape = [
        jax.ShapeDtypeStruct((n * LANE_CHUNKS, LANES), F32),
        jax.ShapeDtypeStruct((n * LANE_CHUNKS, LANES), F32),
        jax.ShapeDtypeStruct((N_EXPERTS, n), F32),
        jax.ShapeDtypeStruct((n // SEQ, WINDOW, KV_WIDTH), F32),
        jax.ShapeDtypeStruct((n // SEQ, WINDOW, KV_WIDTH), F32),
        jax.ShapeDtypeStruct(wgu32.shape, BF16),
        jax.ShapeDtypeStruct(wdn32.shape, BF16),
    ]
    out_specs = [
        pl.BlockSpec((TM * LANE_CHUNKS, LANES), lambda i: (i, 0)),
        pl.BlockSpec((TM * LANE_CHUNKS, LANES), lambda i: (i, 0)),
        pl.BlockSpec((N_EXPERTS, TM), lambda i: (0, i)),
        pl.BlockSpec((1, WINDOW, KV_WIDTH), lambda i: (i // TILES_PER_SEQ, 0, 0)),
        pl.BlockSpec((1, WINDOW, KV_WIDTH), lambda i: (i // TILES_PER_SEQ, 0, 0)),
        anyspec,
        anyspec,
    ]
    scratch = [
        pltpu.VMEM((WINDOW, KV_WIDTH), F32),
        pltpu.VMEM((WINDOW, KV_WIDTH), F32),
        pltpu.VMEM((2, CAST_ROWS, 2 * D_EXPERT), F32),
        pltpu.VMEM((2, CAST_ROWS, D_MODEL), F32),
        pltpu.VMEM((2, CAST_ROWS, 2 * D_EXPERT), BF16),
        pltpu.VMEM((2, CAST_ROWS, D_MODEL), BF16),
        pltpu.SemaphoreType.DMA((4, 2)),
    ]
    return pl.pallas_call(
        _prompt_front_kernel,
        grid=grid,
        in_specs=in_specs,
        out_specs=out_specs,
        out_shape=out_shape,
        scratch_shapes=scratch,
        compiler_params=pltpu.CompilerParams(dimension_semantics=("arbitrary",), vmem_limit_bytes=VMEM_LIMIT),
        name="prompt_front",
    )(x, cos, sin, gmix, win, lng, lnb, ws, bsf, sinks, bias, wpa, wpb, wo, gffn, wrt, br, wgu32, wdn32)


SAMPLE_STEP = 16
SAMPLE_STEPS = DEC_BATCH // SAMPLE_STEP


def _sample_kernel(x_ref, cos_ref, sin_ref, gmix_ref, win_ref, lng_ref, lnb_ref, wdiag_ref, bs0_ref, sinks_ref,
                   kc_ref, vc_ref, wpa_ref, wpb_ref, wo_ref, gffn_ref, wr3_ref, br_ref,
                   x1_ref, h2_ref, logits_ref, kwin_ref, vwin_ref, va_ref,
                   q_s, k_s, v_s, yain_s, ga_s, gb_s, att_s):
    i = pl.program_id(0)

    @pl.when(i == 0)
    def _():
        x = x_ref[...]
        cos = jnp.broadcast_to(cos_ref[...], (DEC_BATCH, LANES))
        sin = jnp.broadcast_to(sin_ref[...], (DEC_BATCH, LANES))
        u, va, q, k, v, gate_a, gate_b = _in_projection(x, gmix_ref, win_ref, lng_ref, lnb_ref, cos, sin)
        va_ref[...] = va
        z = wdiag_ref[...].astype(F32) * va.astype(BF16).astype(F32) + bs0_ref[...]
        yain_s[...] = u * z
        q_s[...] = q
        k_s[...] = k
        v_s[...] = v
        ga_s[...] = gate_a
        gb_s[...] = gate_b

    r0 = pl.multiple_of(i * SAMPLE_STEP, SAMPLE_STEP)
    kwin = jnp.concatenate([kc_ref[:, 1:, :], k_s[pl.ds(r0, SAMPLE_STEP), :][:, None, :]], axis=1)
    vwin = jnp.concatenate([vc_ref[:, 1:, :], v_s[pl.ds(r0, SAMPLE_STEP), :][:, None, :]], axis=1)
    kwin_ref[...] = kwin
    vwin_ref[...] = vwin

    q16 = q_s[pl.ds(r0, SAMPLE_STEP), :]
    lane = lax.broadcasted_iota(I32, (SAMPLE_STEP, LANES), 1)
    heads = [q16[:, hq * LANES:(hq + 1) * LANES] for hq in range(N_HEADS)]
    qpad = pltpu.einshape("hbd->bhd", jnp.stack(heads, axis=0)).astype(BF16)
    s = jnp.einsum("bhd,bkd->bhk", qpad, kwin.astype(BF16), preferred_element_type=F32)
    hid = lax.broadcasted_iota(I32, (1, N_HEADS, 1), 1)
    sink = jnp.zeros((1, N_HEADS, 1), F32)
    for hq in range(N_HEADS):
        sink = jnp.where(hid == hq, sinks_ref[hq], sink)
    m = jnp.maximum(jnp.max(s, axis=-1, keepdims=True), sink)
    e = jnp.exp(s - m)
    inv = 1.0 / (jnp.sum(e, axis=-1, keepdims=True) + jnp.exp(sink - m))
    o = jnp.einsum("bhk,bkd->bhd", (e * inv).astype(BF16), vwin.astype(BF16), preferred_element_type=F32)
    o = pltpu.einshape("bhd->hbd", o)
    chunks = []
    for c in range(N_HEADS // 2):
        parts = []
        for p in range(2):
            hq = 2 * c + p
            oh = o[hq]
            if p != hq // GQA_GROUP:
                oh = pltpu.roll(oh, HEAD_DIM, 1)
            parts.append(oh)
        chunks.append(jnp.where(lane < HEAD_DIM, parts[0], parts[1]))
    att_s[pl.ds(r0, SAMPLE_STEP), :] = jnp.concatenate(chunks, axis=1)

    @pl.when(i == SAMPLE_STEPS - 1)
    def _():
        x1, h2, logits = _merge_and_route(
            x_ref[...], yain_s[...], att_s[...], ga_s[...], gb_s[...],
            wpa_ref, wpb_ref, wo_ref, gffn_ref, wr3_ref, br_ref)
        _store_token_major(x1_ref, x1)
        _store_token_major(h2_ref, h2)
        logits_ref[...] = logits


def _sample_front(x, cos, sin, gmix, win, lng, lnb, wdiag, bs0, sinks, kc, vc, wpa, wpb, wo, gffn, wrt, br):
    n = DEC_BATCH
    cache_spec = pl.BlockSpec((SAMPLE_STEP, WINDOW, KV_WIDTH), lambda i: (i, 0, 0))
    in_specs = [
        _full((n, D_MODEL)),
        _full((1, LANES)),
        _full((1, LANES)),
        _full((1, D_MODEL)),
        _full((D_MODEL, IN_COLS)),
        _full((1, A_WIDTH)),
        _full((1, A_WIDTH)),
        _full((1, A_WIDTH)),
        _full((1, A_WIDTH)),
        pl.BlockSpec(memory_space=pltpu.SMEM),
        cache_spec,
        cache_spec,
        _full((A_WIDTH, D_MODEL)),
        _full((Q_WIDTH, D_MODEL)),
        _full((D_MODEL, D_MODEL)),
        _full((1, D_MODEL)),
        _full((3 * D_MODEL, LANES)),
        _full((N_EXPERTS, 1)),
    ]
    out_shape = [
        jax.ShapeDtypeStruct((n * LANE_CHUNKS, LANES), F32),
        jax.ShapeDtypeStruct((n * LANE_CHUNKS, LANES), F32),
        jax.ShapeDtypeStruct((N_EXPERTS, n), F32),
        jax.ShapeDtypeStruct((n, WINDOW, KV_WIDTH), F32),
        jax.ShapeDtypeStruct((n, WINDOW, KV_WIDTH), F32),
        jax.ShapeDtypeStruct((n, A_WIDTH), F32),
    ]
    out_specs = [
        _full((n * LANE_CHUNKS, LANES)),
        _full((n * LANE_CHUNKS, LANES)),
        _full((N_EXPERTS, n)),
        cache_spec,
        cache_spec,
        _full((n, A_WIDTH)),
    ]
    scratch = [
        pltpu.VMEM((n, QPAD_WIDTH), F32), pltpu.VMEM((n, KV_WIDTH), F32), pltpu.VMEM((n, KV_WIDTH), F32),
        pltpu.VMEM((n, A_WIDTH), F32), pltpu.VMEM((n, D_MODEL), F32), pltpu.VMEM((n, D_MODEL), F32),
        pltpu.VMEM((n, Q_WIDTH), F32),
    ]
    return pl.pallas_call(
        _sample_kernel,
        grid=(SAMPLE_STEPS,),
        in_specs=in_specs,
        out_specs=out_specs,
        out_shape=out_shape,
        scratch_shapes=scratch,
        compiler_params=pltpu.CompilerParams(dimension_semantics=("arbitrary",), vmem_limit_bytes=VMEM_LIMIT),
        name="sample_front",
    )(x, cos, sin, gmix, win, lng, lnb, wdiag, bs0, sinks, kc, vc, wpa, wpb, wo, gffn, wrt, br)


def _route_plan_kernel(lp_ref, ls_ref, dest_ref, wts_ref, off_ref):
    g = pl.program_id(0)
    topi, topw = _top4_softmax(jnp.concatenate([lp_ref[...], ls_ref[...]], axis=1))
    slot = lax.broadcasted_iota(I32, (TOP_K, GROUP_SLOTS), 1)
    eall = jnp.where(jnp.logical_or(slot < GROUP_PROMPT, g == N_GROUPS - 1), topi, N_EXPERTS)
    wts_ref[:, 0:GROUP_SLOTS] = topw
    wts_ref[:, GROUP_SLOTS:] = jnp.zeros((TOP_K, K_STRIDE - GROUP_SLOTS), F32)
    dest_ref[:, GROUP_SLOTS:] = jnp.zeros((TOP_K, K_STRIDE - GROUP_SLOTS), I32)
    eid = lax.broadcasted_iota(I32, (N_EXPERTS, GROUP_SLOTS), 0)
    onehots = [eall[k:k + 1, :] == eid for k in range(TOP_K)]
    count = jnp.zeros((N_EXPERTS, GROUP_SLOTS), F32)
    for oh in onehots:
        count = count + oh.astype(F32)
    total = jnp.broadcast_to(jnp.sum(count, axis=1, keepdims=True), (N_EXPERTS, LANES))
    padded = total + (MOE_ROWS - 1)
    nblk = jnp.floor(padded * (1.0 / MOE_ROWS))
    rem = padded - nblk * MOE_ROWS
    nblk = jnp.where(rem >= MOE_ROWS, nblk + 1.0, jnp.where(rem < 0.0, nblk - 1.0, nblk))
    r = lax.broadcasted_iota(I32, (N_EXPERTS, N_EXPERTS), 0)
    c = lax.broadcasted_iota(I32, (N_EXPERTS, N_EXPERTS), 1)
    first_blk = lax.dot_general((c < r).astype(F32), nblk, (((1,), (0,)), ((), ())),
                                precision=lax.Precision.HIGHEST, preferred_element_type=F32)
    start = (first_blk + 1.0) * MOE_ROWS
    lane = lax.broadcasted_iota(I32, (N_EXPERTS, LANES), 1)
    info = jnp.where(lane == 0, start, jnp.where(lane == 1, start + total, jnp.where(lane == 2, nblk, first_blk)))
    off_ref[...] = info.astype(I32)
    ti = lax.broadcasted_iota(I32, (LANES, LANES), 0)
    tj = lax.broadcasted_iota(I32, (LANES, LANES), 1)
    before = (ti < tj).astype(BF16)
    ones = jnp.ones((LANES, LANES), BF16)
    running = start
    for t in range(SLOT_TILES):
        sl = slice(t * LANES, (t + 1) * LANES)
        cb = count[:, sl].astype(BF16)
        pos = running + _bdot(cb, before)
        rows = [jnp.sum(jnp.where(oh[:, sl], pos, 0.0), axis=0, keepdims=True) for oh in onehots]
        dest_ref[:, sl] = jnp.concatenate(rows, axis=0).astype(I32)
        running = running + _bdot(cb, ones)


def _route_plan(logits_p, logits_s):
    in_specs = [
        pl.BlockSpec((N_EXPERTS, GROUP_PROMPT), lambda g: (0, g)),
        pl.BlockSpec((N_EXPERTS, DEC_BATCH), lambda g: (0, 0)),
    ]
    out_shape = [
        jax.ShapeDtypeStruct((N_GROUPS, TOP_K, K_STRIDE), I32),
        jax.ShapeDtypeStruct((N_GROUPS, TOP_K, K_STRIDE), F32),
        jax.ShapeDtypeStruct((N_GROUPS, N_EXPERTS, LANES), I32),
    ]
    out_specs = [
        pl.BlockSpec((None, TOP_K, K_STRIDE), lambda g: (g, 0, 0)),
        pl.BlockSpec((None, TOP_K, K_STRIDE), lambda g: (g, 0, 0)),
        pl.BlockSpec((None, N_EXPERTS, LANES), lambda g: (g, 0, 0)),
    ]
    return pl.pallas_call(
        _route_plan_kernel,
        grid=(N_GROUPS,),
        in_specs=in_specs,
        out_specs=out_specs,
        out_shape=out_shape,
        compiler_params=pltpu.CompilerParams(dimension_semantics=("arbitrary",)),
        name="route_plan",
    )(logits_p, logits_s)


GROUP_ROWS = GROUP_PROMPT * LANE_CHUNKS
SAMPLE_ROWS = DEC_BATCH * LANE_CHUNKS
TRASH_SLOT = GROUP_SLOTS
BUF_ROWS = (GROUP_SLOTS + 1) * LANE_CHUNKS
SCATTER_BATCH = 8
DMA_SPLIT = 8


def _moe_kernel(off_ref, desth_ref, wtsh_ref, h2p_ref, h2s_ref, x1p_ref, x1s_ref, wgu_ref, bgu_ref, wdn_ref, bdn_ref,
                x2p_ref, x2s_ref,
                h2buf, acc, wgubuf, bgubuf, wdnbuf, bdnbuf, xs0, xs1, ys0, ys1,
                dest_ref, wts_ref, src_ref, seg_expert, seg_first, blk_seg, act_sem, w_sem):
    g = pl.program_id(0)
    last = g == N_GROUPS - 1
    row0 = pl.multiple_of(g * GROUP_ROWS, GROUP_ROWS)

    def prompt_copies():
        cps = []
        for j in range(DMA_SPLIT):
            src = pl.ds(row0 + j * (GROUP_ROWS // DMA_SPLIT), GROUP_ROWS // DMA_SPLIT)
            dst = pl.ds(j * (GROUP_ROWS // DMA_SPLIT), GROUP_ROWS // DMA_SPLIT)
            cps.append(pltpu.make_async_copy(h2p_ref.at[src], h2buf.at[dst], act_sem.at[0]))
            cps.append(pltpu.make_async_copy(x1p_ref.at[src], acc.at[dst], act_sem.at[1]))
        return cps

    def sample_copies():
        return (pltpu.make_async_copy(h2s_ref, h2buf.at[pl.ds(GROUP_ROWS, SAMPLE_ROWS)], act_sem.at[2]),
                pltpu.make_async_copy(x1s_ref, acc.at[pl.ds(GROUP_ROWS, SAMPLE_ROWS)], act_sem.at[3]))

    def weight_copies(e, slot):
        cps = [pltpu.make_async_copy(bgu_ref.at[e], bgubuf.at[slot], w_sem.at[1, slot]),
               pltpu.make_async_copy(bdn_ref.at[e], bdnbuf.at[slot], w_sem.at[3, slot])]
        for j in range(DMA_SPLIT):
            rg = pl.ds(j * (D_MODEL // DMA_SPLIT), D_MODEL // DMA_SPLIT)
            rd = pl.ds(j * (D_EXPERT // DMA_SPLIT), D_EXPERT // DMA_SPLIT)
            cps.append(pltpu.make_async_copy(wgu_ref.at[e, rg], wgubuf.at[slot, rg], w_sem.at[0, slot]))
            cps.append(pltpu.make_async_copy(wdn_ref.at[e, rd], wdnbuf.at[slot, rd], w_sem.at[2, slot]))
        return cps

    def output_copies():
        return [pltpu.make_async_copy(
            acc.at[pl.ds(j * (GROUP_ROWS // DMA_SPLIT), GROUP_ROWS // DMA_SPLIT)],
            x2p_ref.at[pl.ds(row0 + j * (GROUP_ROWS // DMA_SPLIT), GROUP_ROWS // DMA_SPLIT)], act_sem.at[0])
            for j in range(DMA_SPLIT)]

    tab0 = pl.multiple_of(g * (TOP_K * K_STRIDE), TOP_K * K_STRIDE)
    table_copies = (
        pltpu.make_async_copy(desth_ref.at[pl.ds(tab0, TOP_K * K_STRIDE)], dest_ref, act_sem.at[4]),
        pltpu.make_async_copy(wtsh_ref.at[pl.ds(tab0, TOP_K * K_STRIDE)], wts_ref, act_sem.at[5]))
    for cp in table_copies:
        cp.start()

    for cp in prompt_copies():
        cp.start()

    @pl.when(last)
    def _():
        for cp in sample_copies():
            cp.start()

    trash = pl.ds(TRASH_SLOT * LANE_CHUNKS, LANE_CHUNKS)
    h2buf[trash, :] = jnp.zeros((LANE_CHUNKS, LANES), F32)
    acc[trash, :] = jnp.zeros((LANE_CHUNKS, LANES), F32)
    ys1[...] = jnp.zeros_like(ys1)

    def pad_block(pos0):
        def body(j, carry):
            for d in range(SUBLANES):
                src_ref[pos0 + j * SUBLANES + d] = TRASH_SLOT
            return carry
        lax.fori_loop(0, MOE_ROWS // SUBLANES, body, 0)

    def scan_expert(e, carry):
        nseg, nblocks = carry
        nblk = off_ref[e, 2]
        first = off_ref[e, 3]

        @pl.when(nblk > 0)
        def _():
            seg_expert[nseg] = e
            seg_first[nseg] = first
            pad_block(off_ref[e, 0] + (nblk - 1) * MOE_ROWS)

            def mark(b, c2):
                blk_seg[first + b] = nseg
                return c2
            lax.fori_loop(0, nblk, mark, 0)

        return nseg + jnp.where(nblk > 0, 1, 0), nblocks + nblk

    nseg, nblocks = lax.fori_loop(0, N_EXPERTS, scan_expert, (jnp.int32(0), jnp.int32(0)))
    pad_block(0)
    pad_block((nblocks + 1) * MOE_ROWS)
    pad_block((nblocks + 2) * MOE_ROWS)
    blk_seg[nblocks] = nseg - 1
    blk_seg[nblocks + 1] = nseg - 1

    for cp in weight_copies(seg_expert[0], 0):
        cp.start()

    for cp in table_copies:
        cp.wait()

    nvalid = jnp.where(last, GROUP_SLOTS, GROUP_PROMPT)
    for k in range(TOP_K):
        def fill(j, carry, k=k):
            c0 = k * K_STRIDE + j * SUBLANES
            for d in range(SUBLANES):
                src_ref[dest_ref[c0 + d]] = c0 + d
            return carry
        lax.fori_loop(0, nvalid // SUBLANES, fill, 0)

    for cp in prompt_copies():
        cp.wait()

    @pl.when(last)
    def _():
        for cp in sample_copies():
            cp.wait()

    def token_rows(code):
        slot_id = code & (K_STRIDE - 1)
        return pl.ds(pl.multiple_of(slot_id * LANE_CHUNKS, LANE_CHUNKS), LANE_CHUNKS)

    def gather(b, xs):
        base = (b + 1) * MOE_ROWS
        for m in range(MOE_ROWS):
            xs[pl.ds(m, LANE_CHUNKS, stride=XS_STRIDE), :] = h2buf[token_rows(src_ref[base + m]), :]

    def scatter_add(b, ys):
        base = (b + 1) * MOE_ROWS
        for m0 in range(0, MOE_ROWS, SCATTER_BATCH):
            pending = []
            for m in range(m0, m0 + SCATTER_BATCH):
                code = src_ref[base + m]
                rows = token_rows(code)
                pending.append((rows, acc[rows, :] + wts_ref[code] * ys[pl.ds(m, LANE_CHUNKS, stride=XS_STRIDE), :]))
            for rows, val in pending:
                acc[rows, :] = val

    def expert_ffn(xs, ys, slot):
        x = jnp.concatenate(
            [xs[c * XS_STRIDE:c * XS_STRIDE + MOE_ROWS, :] for c in range(LANE_CHUNKS)], axis=1).astype(BF16)
        gu = _bdot(x, wgubuf[slot]) + bgubuf[slot]
        gl = jnp.minimum(gu[:, :D_EXPERT], SWIGLU_LIMIT)
        ul = jnp.clip(gu[:, D_EXPERT:], -SWIGLU_LIMIT, SWIGLU_LIMIT)
        a = (ul + 1.0) * (gl * jax.nn.sigmoid(SWIGLU_ALPHA * gl))
        y = _bdot(a.astype(BF16), wdnbuf[slot]) + bdnbuf[slot]
        for c in range(LANE_CHUNKS):
            ys[c * XS_STRIDE:c * XS_STRIDE + MOE_ROWS, :] = y[:, c * LANES:(c + 1) * LANES]

    def segment_of(b):
        seg = blk_seg[b]
        return seg, seg & 1, jnp.logical_and(b == seg_first[seg], b < nblocks)

    def prefetch_after(seg, slot):
        @pl.when(seg + 1 < nseg)
        def _():
            for cp in weight_copies(seg_expert[seg + 1], 1 - slot):
                cp.start()

    gather(0, xs0)
    npairs = (nblocks + 1) // 2

    def pair(t, carry):
        b0, b1 = 2 * t, 2 * t + 1
        seg0, slot0, new0 = segment_of(b0)
        seg1, slot1, new1 = segment_of(b1)

        @pl.when(new0)
        def _():
            for cp in weight_copies(seg_expert[seg0], slot0):
                cp.wait()
            prefetch_after(seg0, slot0)

        @pl.when(new1)
        def _():
            for cp in weight_copies(seg_expert[seg1], slot1):
                cp.wait()

        gather(b1, xs1)
        expert_ffn(xs0, ys0, slot0)
        scatter_add(b0 - 1, ys1)
        gather(b1 + 1, xs0)
        expert_ffn(xs1, ys1, slot1)
        scatter_add(b0, ys0)

        @pl.when(new1)
        def _():
            prefetch_after(seg1, slot1)

        return carry

    lax.fori_loop(0, npairs, pair, 0)
    scatter_add(2 * npairs - 1, ys1)

    for cp in output_copies():
        cp.start()

    @pl.when(last)
    def _():
        out_s = pltpu.make_async_copy(acc.at[pl.ds(GROUP_ROWS, SAMPLE_ROWS)], x2s_ref, act_sem.at[2])
        out_s.start()
        out_s.wait()

    for cp in output_copies():
        cp.wait()


def _moe(dest, wts, off, h2p, h2s, x1p, x1s, wgu, bgu, wdn, bdn):
    anyspec = pl.BlockSpec(memory_space=pl.ANY)
    dest = dest.reshape(N_GROUPS * TOP_K * K_STRIDE)
    wts = wts.reshape(N_GROUPS * TOP_K * K_STRIDE)
    in_specs = [
        pl.BlockSpec((None, N_EXPERTS, LANES), lambda g: (g, 0, 0), memory_space=pltpu.SMEM),
        anyspec, anyspec, anyspec, anyspec, anyspec, anyspec, anyspec, anyspec, anyspec, anyspec,
    ]
    scratch = [
        pltpu.VMEM((BUF_ROWS, LANES), F32),
        pltpu.VMEM((BUF_ROWS, LANES), F32),
        pltpu.VMEM((2, D_MODEL, 2 * D_EXPERT), BF16),
        pltpu.VMEM((2, 1, 2 * D_EXPERT), F32),
        pltpu.VMEM((2, D_EXPERT, D_MODEL), BF16),
        pltpu.VMEM((2, 1, D_MODEL), F32),
        pltpu.VMEM((LANE_CHUNKS * XS_STRIDE, LANES), F32),
        pltpu.VMEM((LANE_CHUNKS * XS_STRIDE, LANES), F32),
        pltpu.VMEM((LANE_CHUNKS * XS_STRIDE, LANES), F32),
        pltpu.VMEM((LANE_CHUNKS * XS_STRIDE, LANES), F32),
        pltpu.SMEM((TOP_K * K_STRIDE,), I32),
        pltpu.SMEM((TOP_K * K_STRIDE,), F32),
        pltpu.SMEM((POS_TABLE,), I32),
        pltpu.SMEM((N_EXPERTS,), I32),
        pltpu.SMEM((N_EXPERTS,), I32),
        pltpu.SMEM((LANES,), I32),
        pltpu.SemaphoreType.DMA((6,)),
        pltpu.SemaphoreType.DMA((4, 2)),
    ]
    return pl.pallas_call(
        _moe_kernel,
        grid=(N_GROUPS,),
        in_specs=in_specs,
        out_specs=[anyspec, anyspec],
        out_shape=[jax.ShapeDtypeStruct(x1p.shape, F32), jax.ShapeDtypeStruct(x1s.shape, F32)],
        scratch_shapes=scratch,
        compiler_params=pltpu.CompilerParams(dimension_semantics=("arbitrary",), vmem_limit_bytes=VMEM_LIMIT),
        name="moe",
    )(off, dest, wts, h2p, h2s, x1p, x1s, wgu, bgu, wdn, bdn)


def _ple_final_kernel(x2_ref, ple_ref, wple_ref, gple_ref, wpg_ref, gfin_ref, y_ref):
    rows = y_ref.shape[0]
    x2 = _load_token_major(x2_ref, rows)
    e = _rmsnorm(_bdot(ple_ref[...].astype(BF16), wple_ref[...]), gple_ref[...])
    x3 = x2 + jax.nn.sigmoid(_bdot(x2.astype(BF16), wpg_ref[...])) * e
    y_ref[...] = _rmsnorm(x3, gfin_ref[...])


def _ple_final(x2_tm, ple, wple, gple, wpg, gfin, tile):
    n = ple.shape[0]
    return pl.pallas_call(
        _ple_final_kernel,
        grid=(n // tile,),
        in_specs=[
            pl.BlockSpec((tile * LANE_CHUNKS, LANES), lambda i: (i, 0)),
            pl.BlockSpec((tile, PLE_DIM), lambda i: (i, 0)),
            _full((PLE_DIM, D_MODEL)),
            _full((1, D_MODEL)),
            _full((D_MODEL, D_MODEL)),
            _full((1, D_MODEL)),
        ],
        out_specs=pl.BlockSpec((tile, D_MODEL), lambda i: (i, 0)),
        out_shape=jax.ShapeDtypeStruct((n, D_MODEL), F32),
        compiler_params=pltpu.CompilerParams(dimension_semantics=("arbitrary",), vmem_limit_bytes=VMEM_LIMIT),
        name="ple_final",
    )(x2_tm, ple, wple, gple, wpg, gfin)


def _rope_tables(pos):
    half = HEAD_DIM // 2
    inv = ROPE_THETA ** (-jnp.arange(half, dtype=F32) / half)
    ang = pos.astype(F32)[:, None] * inv[None, :]
    cos, sin = jnp.cos(ang), jnp.sin(ang)
    cos2 = jnp.concatenate([cos, cos, cos, cos], axis=1)
    sin2 = jnp.concatenate([-sin, sin, -sin, sin], axis=1)
    return cos2, sin2


def _layout_w_in(w_in):
    o_q = 2 * A_WIDTH
    wq = w_in[:, o_q:o_q + Q_WIDTH].reshape(D_MODEL, N_HEADS, HEAD_DIM) * (HEAD_DIM ** -0.5)
    kv_head = (jnp.arange(N_HEADS) // GQA_GROUP)[None, :, None]
    wq_pad = jnp.concatenate([jnp.where(kv_head == h, wq, 0.0) for h in range(N_KV_HEADS)], axis=-1)
    return jnp.concatenate([w_in[:, :o_q], wq_pad.reshape(D_MODEL, QPAD_WIDTH), w_in[:, o_q + Q_WIDTH:]], axis=1)


def _router_passes(w_router):
    hi = w_router.astype(BF16)
    lo = (w_router - hi.astype(F32)).astype(BF16)
    w3 = jnp.concatenate([hi, lo, hi], axis=0)
    return jnp.pad(w3, ((0, 0), (0, LANES - N_EXPERTS)))


def _prep_weights(g_mix, w_in, a_ln_g, a_ln_b, a_ws, a_bs, w_pa, w_pb, w_o, g_ffn, w_router, b_router):
    causal = jnp.tril(jnp.ones((CHUNK, CHUNK), dtype=bool))
    return dict(
        gmix=g_mix.reshape(1, D_MODEL),
        win=_layout_w_in(w_in).astype(BF16),
        lng=a_ln_g.reshape(1, A_WIDTH),
        lnb=a_ln_b.reshape(1, A_WIDTH),
        ws=jnp.where(causal[None], a_ws, 0.0).astype(BF16),
        bsf=jnp.repeat(jnp.transpose(a_bs), A_GROUP_DIM, axis=1),
        wpa=w_pa.astype(BF16),
        wpb=w_pb.astype(BF16),
        wo=w_o.astype(BF16),
        gffn=g_ffn.reshape(1, D_MODEL),
        wrt=_router_passes(w_router),
        br=b_router.reshape(N_EXPERTS, 1),
    )


def kernel(x_prompt, x_sample, cache_win_k, cache_win_v, p_prompt, p_sample, g_mix, w_in, a_ln_g, a_ln_b, a_ws, a_bs, sinks, w_pa, w_pb, w_o, g_ffn, w_router, b_router, w_gu, b_gu, w_down, b_down, w_ple, g_ple, w_ple_gate, g_final):
    W = _prep_weights(g_mix[0], w_in[0], a_ln_g[0], a_ln_b[0], a_ws[0], a_bs[0], w_pa[0], w_pb[0], w_o[0],
                      g_ffn[0], w_router[0], b_router[0])
    cos_p, sin_p = _rope_tables(jnp.arange(SEQ, dtype=I32))
    cos_s, sin_s = _rope_tables(jnp.full((1,), PAST_LEN, I32))
    x1p, h2p, logits_p, kwin_p, vwin_p, wgu16, wdn16 = _prompt_front(
        x_prompt.reshape(N_PROMPT, D_MODEL), cos_p, sin_p, W["gmix"], W["win"], W["lng"], W["lnb"],
        W["ws"], W["bsf"], sinks[0], _band_bias(), W["wpa"], W["wpb"], W["wo"], W["gffn"], W["wrt"], W["br"],
        w_gu[0].reshape(N_EXPERTS * D_MODEL, 2 * D_EXPERT), w_down[0].reshape(N_EXPERTS * D_EXPERT, D_MODEL))

    wdiag = jnp.repeat(a_ws[0, :, 0, 0], A_GROUP_DIM)[None, :].astype(BF16)
    bs0 = jnp.repeat(a_bs[0, :, 0], A_GROUP_DIM)[None, :]
    x1s, h2s, logits_s, kwin_s, vwin_s, va_s = _sample_front(
        x_sample.reshape(DEC_BATCH, D_MODEL), cos_s, sin_s, W["gmix"], W["win"], W["lng"], W["lnb"], wdiag, bs0,
        sinks[0], cache_win_k[0].reshape(DEC_BATCH, WINDOW, KV_WIDTH), cache_win_v[0].reshape(DEC_BATCH, WINDOW, KV_WIDTH),
        W["wpa"], W["wpb"], W["wo"], W["gffn"], W["wrt"], W["br"])

    dest, wts, off = _route_plan(logits_p, logits_s)
    x2p, x2s = _moe(dest, wts, off, h2p, h2s, x1p, x1s,
                    wgu16.reshape(N_EXPERTS, D_MODEL, 2 * D_EXPERT), b_gu[0].reshape(N_EXPERTS, 1, 2 * D_EXPERT),
                    wdn16.reshape(N_EXPERTS, D_EXPERT, D_MODEL), b_down[0].reshape(N_EXPERTS, 1, D_MODEL))

    wple = w_ple[0].astype(BF16)
    gple = g_ple[0].reshape(1, D_MODEL)
    wpg = w_ple_gate[0].astype(BF16)
    gfin = g_final.reshape(1, D_MODEL)
    y_p = _ple_final(x2p, p_prompt[0].reshape(N_PROMPT, PLE_DIM), wple, gple, wpg, gfin, 2 * TM)
    y_s = _ple_final(x2s, p_sample[0].reshape(DEC_BATCH, PLE_DIM), wple, gple, wpg, gfin, DEC_BATCH)

    return (
        y_p.reshape(BATCH, SEQ, D_MODEL),
        y_s.reshape(DEC_BATCH, 1, D_MODEL),
        kwin_p.reshape(1, BATCH, WINDOW, N_KV_HEADS, HEAD_DIM),
        vwin_p.reshape(1, BATCH, WINDOW, N_KV_HEADS, HEAD_DIM),
        kwin_s.reshape(1, DEC_BATCH, WINDOW, N_KV_HEADS, HEAD_DIM),
        vwin_s.reshape(1, DEC_BATCH, WINDOW, N_KV_HEADS, HEAD_DIM),
        va_s.reshape(1, DEC_BATCH, 1, A_WIDTH),
    )
```

```python
import functools

import jax
import jax.numpy as jnp
from jax import lax
from jax.experimental import pallas as pl
from jax.experimental.pallas import tpu as pltpu

F32 = jnp.float32
BF16 = jnp.bfloat16
I32 = jnp.int32

D_MODEL = 1024
BATCH = 4
SEQ = 4096
DEC_BATCH = 128
PAST_LEN = 8192
CHUNK = 128
A_GROUPS = 4
A_GROUP_DIM = 128
A_WIDTH = A_GROUPS * A_GROUP_DIM
N_HEADS = 8
N_KV_HEADS = 2
HEAD_DIM = 64
Q_WIDTH = N_HEADS * HEAD_DIM
KV_WIDTH = N_KV_HEADS * HEAD_DIM
GQA_GROUP = N_HEADS // N_KV_HEADS
WINDOW = 128
ROPE_THETA = 10000.0
N_EXPERTS = 32
TOP_K = 4
D_EXPERT = D_MODEL
SWIGLU_ALPHA = 1.702
SWIGLU_LIMIT = 7.0
PLE_DIM = 256
RMS_EPS = 1e-5
LN_EPS = 1e-5

LANES = 128

QPAD_WIDTH = N_HEADS * LANES
O_Q = 2 * A_WIDTH
O_K = O_Q + QPAD_WIDTH
O_V = O_K + KV_WIDTH
O_GA = O_V + KV_WIDTH
O_GB = O_GA + D_MODEL
IN_COLS = O_GB + D_MODEL
SUBLANES = 8
LANE_CHUNKS = D_MODEL // LANES
VMEM_LIMIT = 56 * 1024 * 1024

N_PROMPT = BATCH * SEQ
TM = 256
TILES_PER_SEQ = SEQ // TM
BLOCKS_PER_TILE = TM // WINDOW
FRONT_STEPS = N_PROMPT // TM
CAST_ROWS = N_EXPERTS * D_MODEL // FRONT_STEPS
CAST_SPLIT = 4

N_GROUPS = 4
GROUP_PROMPT = N_PROMPT // N_GROUPS
GROUP_SLOTS = GROUP_PROMPT + DEC_BATCH
GROUP_ASSIGN = GROUP_SLOTS * TOP_K
SLOT_TILES = GROUP_SLOTS // LANES
MOE_ROWS = 256
XS_STRIDE = MOE_ROWS + SUBLANES
SLOT_BITS = 13
K_STRIDE = 1 << SLOT_BITS
assert GROUP_SLOTS < K_STRIDE
MAX_BLOCKS = GROUP_ASSIGN // MOE_ROWS + N_EXPERTS
POS_TABLE = 1 << 15
assert (MAX_BLOCKS + 3) * MOE_ROWS <= POS_TABLE
assert MAX_BLOCKS + 2 <= LANES


def _bdot(a, b):
    return jnp.dot(a, b, preferred_element_type=F32)


def _rmsnorm(x, g):
    return x * lax.rsqrt(jnp.mean(x * x, axis=-1, keepdims=True) + RMS_EPS) * g


def _gelu(x):
    return 0.5 * x * (1.0 + lax.erf(x * (0.5 ** 0.5)))


def _group_layernorm(v, g, b):
    cols = []
    for gi in range(A_GROUPS):
        s = slice(gi * A_GROUP_DIM, (gi + 1) * A_GROUP_DIM)
        vg = v[:, s]
        mu = jnp.mean(vg, axis=-1, keepdims=True)
        d = vg - mu
        var = jnp.mean(d * d, axis=-1, keepdims=True)
        cols.append(d * lax.rsqrt(var + LN_EPS) * g[:, s] + b[:, s])
    return jnp.concatenate(cols, axis=1)


def _rope(x, cos, sin_signed):
    width = x.shape[1]
    reps = width // LANES
    cosf = jnp.concatenate([cos] * reps, axis=1) if reps > 1 else cos
    sinf = jnp.concatenate([sin_signed] * reps, axis=1) if reps > 1 else sin_signed
    half = HEAD_DIM // 2
    lane = lax.broadcasted_iota(I32, x.shape, 1)
    up = pltpu.roll(x, width - half, 1)
    down = pltpu.roll(x, half, 1)
    partner = jnp.where((lane & (HEAD_DIM - 1)) < half, up, down)
    return x * cosf + partner * sinf


def _in_projection(x, gmix_ref, win_ref, lng_ref, lnb_ref, cos, sin_signed):
    hb = _rmsnorm(x, gmix_ref[...]).astype(BF16)
    zuv = _gelu(_bdot(hb, win_ref[:, 0:O_Q]))
    u = zuv[:, :A_WIDTH]
    va = _group_layernorm(zuv[:, A_WIDTH:], lng_ref[...], lnb_ref[...])
    zqkv = _bdot(hb, win_ref[:, O_Q:O_GA])
    q = _rope(zqkv[:, :QPAD_WIDTH], cos, sin_signed)
    k = _rope(zqkv[:, QPAD_WIDTH:QPAD_WIDTH + KV_WIDTH], cos, sin_signed)
    v = zqkv[:, QPAD_WIDTH + KV_WIDTH:]
    zg = _bdot(hb, win_ref[:, O_GA:IN_COLS])
    gate_a = jax.nn.sigmoid(zg[:, :D_MODEL])
    gate_b = jax.nn.sigmoid(zg[:, D_MODEL:])
    return u, va, q, k, v, gate_a, gate_b


def _merge_and_route(x, ya_in, att, gate_a, gate_b, wpa_ref, wpb_ref, wo_ref, gffn_ref, wr3_ref, br_ref):
    ya = _bdot(ya_in.astype(BF16), wpa_ref[...])
    yb = _bdot(att.astype(BF16), wpb_ref[...])
    mix = (gate_a * ya + gate_b * yb).astype(BF16)
    x1 = x + _bdot(mix, wo_ref[...])
    h2 = _rmsnorm(x1, gffn_ref[...])
    hi = h2.astype(BF16)
    lo = (h2 - hi.astype(F32)).astype(BF16)
    logits = _bdot(jnp.concatenate([hi, hi, lo], axis=1), wr3_ref[...])
    return x1, h2, jnp.transpose(logits)[:N_EXPERTS, :] + br_ref[...]


def _top4_softmax(logits):
    eid = lax.broadcasted_iota(I32, logits.shape, 0)
    vals, idxs = [], []
    for _ in range(TOP_K):
        m = jnp.max(logits, axis=0, keepdims=True)
        idx = jnp.min(jnp.where(logits == m, eid, N_EXPERTS), axis=0, keepdims=True)
        logits = jnp.where(eid == idx, -jnp.inf, logits)
        vals.append(m)
        idxs.append(idx)
    es = [jnp.exp(v - vals[0]) for v in vals]
    inv = 1.0 / (es[0] + es[1] + es[2] + es[3])
    return jnp.concatenate(idxs, axis=0), jnp.concatenate([e * inv for e in es], axis=0)


def _store_token_major(ref, val):
    rows = val.shape[0]
    for c in range(LANE_CHUNKS):
        ref[pl.ds(c, rows, stride=LANE_CHUNKS), :] = val[:, c * LANES:(c + 1) * LANES]


def _load_token_major(ref, rows):
    return jnp.concatenate([ref[pl.ds(c, rows, stride=LANE_CHUNKS), :] for c in range(LANE_CHUNKS)], axis=1)


def _band_attention(qpad, k, v, k_prev, v_prev, sinks_ref, bias_ref, seq_start):
    kb = jnp.concatenate([k_prev, k], axis=0).astype(BF16)
    vt = jnp.transpose(jnp.concatenate([v_prev, v], axis=0)).astype(BF16)
    qb = qpad.astype(BF16)
    lane = lax.broadcasted_iota(I32, (1, GQA_GROUP * WINDOW), 1)
    blocks = []
    for b in range(BLOCKS_PER_TILE):
        bias = bias_ref[jnp.where(seq_start, 1, 0)] if b == 0 else bias_ref[0]
        keys = kb[b * WINDOW:(b + 2) * WINDOW, :]
        pieces = []
        for h in range(N_KV_HEADS):
            qh = jnp.concatenate(
                [qb[b * WINDOW:(b + 1) * WINDOW, (h * GQA_GROUP + j) * LANES:(h * GQA_GROUP + j + 1) * LANES]
                 for j in range(GQA_GROUP)], axis=0)
            st = lax.dot_general(keys, qh, (((1,), (1,)), ((), ())), preferred_element_type=F32) + bias
            sink = jnp.zeros((1, GQA_GROUP * WINDOW), F32)
            for j in range(GQA_GROUP):
                sink = jnp.where(lane // WINDOW == j, sinks_ref[h * GQA_GROUP + j], sink)
            m = jnp.maximum(jnp.max(st, axis=0, keepdims=True), sink)
            e = jnp.exp(st - m)
            inv = 1.0 / (jnp.sum(e, axis=0, keepdims=True) + jnp.exp(sink - m))
            ot = _bdot(vt[h * HEAD_DIM:(h + 1) * HEAD_DIM, b * WINDOW:(b + 2) * WINDOW], (e * inv).astype(BF16))
            pieces.extend(ot[:, j * WINDOW:(j + 1) * WINDOW] for j in range(GQA_GROUP))
        blocks.append(jnp.transpose(jnp.concatenate(pieces, axis=0)))
    return jnp.concatenate(blocks, axis=0)


def _band_bias():
    kj = lax.broadcasted_iota(I32, (2, 2 * WINDOW, GQA_GROUP * WINDOW), 1)
    qi = lax.broadcasted_iota(I32, (2, 2 * WINDOW, GQA_GROUP * WINDOW), 2) % WINDOW
    lo = lax.broadcasted_iota(I32, (2, 2 * WINDOW, GQA_GROUP * WINDOW), 0) * WINDOW
    valid = (kj > qi) & (kj <= qi + WINDOW) & (kj >= lo)
    return jnp.where(valid, 0.0, -jnp.inf).astype(F32)


def _prompt_front_kernel(x_ref, cos_ref, sin_ref, gmix_ref, win_ref, lng_ref, lnb_ref, ws_ref, bsf_ref,
                         sinks_ref, bias_ref, wpa_ref, wpb_ref, wo_ref, gffn_ref, wr3_ref, br_ref, wgu32_ref, wdn32_ref,
                         x1_ref, h2_ref, logits_ref, kwin_ref, vwin_ref, wgu16_ref, wdn16_ref,
                         kprev_ref, vprev_ref, gu_in, dn_in, gu_out, dn_out, cast_sem):
    i = pl.program_id(0)
    seq_start = (i % TILES_PER_SEQ) == 0

    def cast_rows(c, j):
        part = CAST_ROWS // CAST_SPLIT
        return pl.ds(pl.multiple_of(c * CAST_ROWS + j * part, part), part), pl.ds(j * part, part)

    def cast_in(c, slot):
        cps = []
        for j in range(CAST_SPLIT):
            hbm, loc = cast_rows(c, j)
            cps.append(pltpu.make_async_copy(wgu32_ref.at[hbm], gu_in.at[slot, loc], cast_sem.at[0, slot]))
            cps.append(pltpu.make_async_copy(wdn32_ref.at[hbm], dn_in.at[slot, loc], cast_sem.at[1, slot]))
        return cps

    def cast_out(c, slot):
        cps = []
        for j in range(CAST_SPLIT):
            hbm, loc = cast_rows(c, j)
            cps.append(pltpu.make_async_copy(gu_out.at[slot, loc], wgu16_ref.at[hbm], cast_sem.at[2, slot]))
            cps.append(pltpu.make_async_copy(dn_out.at[slot, loc], wdn16_ref.at[hbm], cast_sem.at[3, slot]))
        return cps

    slot = i & 1

    @pl.when(i == 0)
    def _():
        for cp in cast_in(0, 0):
            cp.start()

    @pl.when(i + 1 < FRONT_STEPS)
    def _():
        for cp in cast_in(i + 1, 1 - slot):
            cp.start()

    for cp in cast_in(i, slot):
        cp.wait()

    @pl.when(i >= 2)
    def _():
        for cp in cast_out(i - 2, slot):
            cp.wait()


    @pl.when(seq_start)
    def _():
        kprev_ref[...] = jnp.zeros_like(kprev_ref)
        vprev_ref[...] = jnp.zeros_like(vprev_ref)

    x = x_ref[...]
    u, va, q, k, v, gate_a, gate_b = _in_projection(
        x, gmix_ref, win_ref, lng_ref, lnb_ref, cos_ref[...], sin_ref[...])

    att = _band_attention(q, k, v, kprev_ref[...], vprev_ref[...], sinks_ref, bias_ref, seq_start)

    gu_out[slot] = gu_in[slot].astype(BF16)
    dn_out[slot] = dn_in[slot].astype(BF16)
    k_last, v_last = k[TM - WINDOW:], v[TM - WINDOW:]
    kprev_ref[...] = k_last
    vprev_ref[...] = v_last
    kwin_ref[0] = k_last
    vwin_ref[0] = v_last

    vab = va.astype(BF16)
    zc = jnp.concatenate(
        [jnp.concatenate(
            [_bdot(ws_ref[g], vab[b * CHUNK:(b + 1) * CHUNK, g * A_GROUP_DIM:(g + 1) * A_GROUP_DIM])
             for g in range(A_GROUPS)], axis=1) + bsf_ref[...]
         for b in range(BLOCKS_PER_TILE)], axis=0)

    x1, h2, logits = _merge_and_route(x, u * zc, att, gate_a, gate_b,
                                      wpa_ref, wpb_ref, wo_ref, gffn_ref, wr3_ref, br_ref)
    _store_token_major(x1_ref, x1)
    _store_token_major(h2_ref, h2)
    logits_ref[...] = logits

    for cp in cast_out(i, slot):
        cp.start()

    @pl.when(i == FRONT_STEPS - 1)
    def _():
        for cp in cast_out(i - 1, 1 - slot) + cast_out(i, slot):
            cp.wait()


def _full(shape):
    return pl.BlockSpec(shape, lambda i: (0,) * len(shape))


def _prompt_front(x, cos, sin, gmix, win, lng, lnb, ws, bsf, sinks, bias, wpa, wpb, wo, gffn, wrt, br, wgu32, wdn32):
    n = x.shape[0]
    assert n == N_PROMPT
    grid = (FRONT_STEPS,)
    anyspec = pl.BlockSpec(memory_space=pl.ANY)
    in_specs = [
        pl.BlockSpec((TM, D_MODEL), lambda i: (i, 0)),
        pl.BlockSpec((TM, LANES), lambda i: (i % TILES_PER_SEQ, 0)),
        pl.BlockSpec((TM, LANES), lambda i: (i % TILES_PER_SEQ, 0)),
        _full((1, D_MODEL)),
        _full((D_MODEL, IN_COLS)),
        _full((1, A_WIDTH)),
        _full((1, A_WIDTH)),
        _full((A_GROUPS, CHUNK, CHUNK)),
        _full((CHUNK, A_WIDTH)),
        pl.BlockSpec(memory_space=pltpu.SMEM),
        _full((2, 2 * WINDOW, GQA_GROUP * WINDOW)),
        _full((A_WIDTH, D_MODEL)),
        _full((Q_WIDTH, D_MODEL)),
        _full((D_MODEL, D_MODEL)),
        _full((1, D_MODEL)),
        _full((3 * D_MODEL, LANES)),
        _full((N_EXPERTS, 1)),
        anyspec,
        anyspec,
    ]
    out_shape = [
        jax.ShapeDtypeStruct((n * LANE_CHUNKS, LANES), F32),
        jax.ShapeDtypeStruct((n * LANE_CHUNKS, LANES), F32),
        jax.ShapeDtypeStruct((N_EXPERTS, n), F32),
        jax.ShapeDtypeStruct((n // SEQ, WINDOW, KV_WIDTH), F32),
        jax.ShapeDtypeStruct((n // SEQ, WINDOW, KV_WIDTH), F32),
        jax.ShapeDtypeStruct(wgu32.shape, BF16),
        jax.ShapeDtypeStruct(wdn32.shape, BF16),
    ]
    out_specs = [
        pl.BlockSpec((TM * LANE_CHUNKS, LANES), lambda i: (i, 0)),
        pl.BlockSpec((TM * LANE_CHUNKS, LANES), lambda i: (i, 0)),
        pl.BlockSpec((N_EXPERTS, TM), lambda i: (0, i)),
        pl.BlockSpec((1, WINDOW, KV_WIDTH), lambda i: (i // TILES_PER_SEQ, 0, 0)),
        pl.BlockSpec((1, WINDOW, KV_WIDTH), lambda i: (i // TILES_PER_SEQ, 0, 0)),
        anyspec,
        anyspec,
    ]
    scratch = [
        pltpu.VMEM((WINDOW, KV_WIDTH), F32),
        pltpu.VMEM((WINDOW, KV_WIDTH), F32),
        pltpu.VMEM((2, CAST_ROWS, 2 * D_EXPERT), F32),
        pltpu.VMEM((2, CAST_ROWS, D_MODEL), F32),
        pltpu.VMEM((2, CAST_ROWS, 2 * D_EXPERT), BF16),
        pltpu.VMEM((2, CAST_ROWS, D_MODEL), BF16),
        pltpu.SemaphoreType.DMA((4, 2)),
    ]
    return pl.pallas_call(
        _prompt_front_kernel,
        grid=grid,
        in_specs=in_specs,
        out_specs=out_specs,
        out_shape=out_shape,
        scratch_shapes=scratch,
        compiler_params=pltpu.CompilerParams(dimension_semantics=("arbitrary",), vmem_limit_bytes=VMEM_LIMIT),
        name="prompt_front",
    )(x, cos, sin, gmix, win, lng, lnb, ws, bsf, sinks, bias, wpa, wpb, wo, gffn, wrt, br, wgu32, wdn32)


SAMPLE_STEP = 16
SAMPLE_STEPS = DEC_BATCH // SAMPLE_STEP


def _sample_kernel(x_ref, cos_ref, sin_ref, gmix_ref, win_ref, lng_ref, lnb_ref, wdiag_ref, bs0_ref, sinks_ref,
                   kc_ref, vc_ref, wpa_ref, wpb_ref, wo_ref, gffn_ref, wr3_ref, br_ref,
                   x1_ref, h2_ref, logits_ref, kwin_ref, vwin_ref, va_ref,
                   q_s, k_s, v_s, yain_s, ga_s, gb_s, att_s):
    i = pl.program_id(0)

    @pl.when(i == 0)
    def _():
        x = x_ref[...]
        cos = jnp.broadcast_to(cos_ref[...], (DEC_BATCH, LANES))
        sin = jnp.broadcast_to(sin_ref[...], (DEC_BATCH, LANES))
        u, va, q, k, v, gate_a, gate_b = _in_projection(x, gmix_ref, win_ref, lng_ref, lnb_ref, cos, sin)
        va_ref[...] = va
        z = wdiag_ref[...].astype(F32) * va.astype(BF16).astype(F32) + bs0_ref[...]
        yain_s[...] = u * z
        q_s[...] = q
        k_s[...] = k
        v_s[...] = v
        ga_s[...] = gate_a
        gb_s[...] = gate_b

    r0 = pl.multiple_of(i * SAMPLE_STEP, SAMPLE_STEP)
    kwin = jnp.concatenate([kc_ref[:, 1:, :], k_s[pl.ds(r0, SAMPLE_STEP), :][:, None, :]], axis=1)
    vwin = jnp.concatenate([vc_ref[:, 1:, :], v_s[pl.ds(r0, SAMPLE_STEP), :][:, None, :]], axis=1)
    kwin_ref[...] = kwin
    vwin_ref[...] = vwin

    q16 = q_s[pl.ds(r0, SAMPLE_STEP), :]
    lane = lax.broadcasted_iota(I32, (SAMPLE_STEP, LANES), 1)
    heads = [q16[:, hq * LANES:(hq + 1) * LANES] for hq in range(N_HEADS)]
    qpad = pltpu.einshape("hbd->bhd", jnp.stack(heads, axis=0)).astype(BF16)
    s = jnp.einsum("bhd,bkd->bhk", qpad, kwin.astype(BF16), preferred_element_type=F32)
    hid = lax.broadcasted_iota(I32, (1, N_HEADS, 1), 1)
    sink = jnp.zeros((1, N_HEADS, 1), F32)
    for hq in range(N_HEADS):
        sink = jnp.where(hid == hq, sinks_ref[hq], sink)
    m = jnp.maximum(jnp.max(s, axis=-1, keepdims=True), sink)
    e = jnp.exp(s - m)
    inv = 1.0 / (jnp.sum(e, axis=-1, keepdims=True) + jnp.exp(sink - m))
    o = jnp.einsum("bhk,bkd->bhd", (e * inv).astype(BF16), vwin.astype(BF16), preferred_element_type=F32)
    o = pltpu.einshape("bhd->hbd", o)
    chunks = []
    for c in range(N_HEADS // 2):
        parts = []
        for p in range(2):
            hq = 2 * c + p
            oh = o[hq]
            if p != hq // GQA_GROUP:
                oh = pltpu.roll(oh, HEAD_DIM, 1)
            parts.append(oh)
        chunks.append(jnp.where(lane < HEAD_DIM, parts[0], parts[1]))
    att_s[pl.ds(r0, SAMPLE_STEP), :] = jnp.concatenate(chunks, axis=1)

    @pl.when(i == SAMPLE_STEPS - 1)
    def _():
        x1, h2, logits = _merge_and_route(
            x_ref[...], yain_s[...], att_s[...], ga_s[...], gb_s[...],
            wpa_ref, wpb_ref, wo_ref, gffn_ref, wr3_ref, br_ref)
        _store_token_major(x1_ref, x1)
        _store_token_major(h2_ref, h2)
        logits_ref[...] = logits


def _sample_front(x, cos, sin, gmix, win, lng, lnb, wdiag, bs0, sinks, kc, vc, wpa, wpb, wo, gffn, wrt, br):
    n = DEC_BATCH
    cache_spec = pl.BlockSpec((SAMPLE_STEP, WINDOW, KV_WIDTH), lambda i: (i, 0, 0))
    in_specs = [
        _full((n, D_MODEL)),
        _full((1, LANES)),
        _full((1, LANES)),
        _full((1, D_MODEL)),
        _full((D_MODEL, IN_COLS)),
        _full((1, A_WIDTH)),
        _full((1, A_WIDTH)),
        _full((1, A_WIDTH)),
        _full((1, A_WIDTH)),
        pl.BlockSpec(memory_space=pltpu.SMEM),
        cache_spec,
        cache_spec,
        _full((A_WIDTH, D_MODEL)),
        _full((Q_WIDTH, D_MODEL)),
        _full((D_MODEL, D_MODEL)),
        _full((1, D_MODEL)),
        _full((3 * D_MODEL, LANES)),
        _full((N_EXPERTS, 1)),
    ]
    out_shape = [
        jax.ShapeDtypeStruct((n * LANE_CHUNKS, LANES), F32),
        jax.ShapeDtypeStruct((n * LANE_CHUNKS, LANES), F32),
        jax.ShapeDtypeStruct((N_EXPERTS, n), F32),
        jax.ShapeDtypeStruct((n, WINDOW, KV_WIDTH), F32),
        jax.ShapeDtypeStruct((n, WINDOW, KV_WIDTH), F32),
        jax.ShapeDtypeStruct((n, A_WIDTH), F32),
    ]
    out_specs = [
        _full((n * LANE_CHUNKS, LANES)),
        _full((n * LANE_CHUNKS, LANES)),
        _full((N_EXPERTS, n)),
        cache_spec,
        cache_spec,
        _full((n, A_WIDTH)),
    ]
    scratch = [
        pltpu.VMEM((n, QPAD_WIDTH), F32), pltpu.VMEM((n, KV_WIDTH), F32), pltpu.VMEM((n, KV_WIDTH), F32),
        pltpu.VMEM((n, A_WIDTH), F32), pltpu.VMEM((n, D_MODEL), F32), pltpu.VMEM((n, D_MODEL), F32),
        pltpu.VMEM((n, Q_WIDTH), F32),
    ]
    return pl.pallas_call(
        _sample_kernel,
        grid=(SAMPLE_STEPS,),
        in_specs=in_specs,
        out_specs=out_specs,
        out_shape=out_shape,
        scratch_shapes=scratch,
        compiler_params=pltpu.CompilerParams(dimension_semantics=("arbitrary",), vmem_limit_bytes=VMEM_LIMIT),
        name="sample_front",
    )(x, cos, sin, gmix, win, lng, lnb, wdiag, bs0, sinks, kc, vc, wpa, wpb, wo, gffn, wrt, br)


def _route_plan_kernel(lp_ref, ls_ref, dest_ref, wts_ref, off_ref):
    g = pl.program_id(0)
    topi, topw = _top4_softmax(jnp.concatenate([lp_ref[...], ls_ref[...]], axis=1))
    slot = lax.broadcasted_iota(I32, (TOP_K, GROUP_SLOTS), 1)
    eall = jnp.where(jnp.logical_or(slot < GROUP_PROMPT, g == N_GROUPS - 1), topi, N_EXPERTS)
    wts_ref[:, 0:GROUP_SLOTS] = topw
    wts_ref[:, GROUP_SLOTS:] = jnp.zeros((TOP_K, K_STRIDE - GROUP_SLOTS), F32)
    dest_ref[:, GROUP_SLOTS:] = jnp.zeros((TOP_K, K_STRIDE - GROUP_SLOTS), I32)
    eid = lax.broadcasted_iota(I32, (N_EXPERTS, GROUP_SLOTS), 0)
    onehots = [eall[k:k + 1, :] == eid for k in range(TOP_K)]
    count = jnp.zeros((N_EXPERTS, GROUP_SLOTS), F32)
    for oh in onehots:
        count = count + oh.astype(F32)
    total = jnp.broadcast_to(jnp.sum(count, axis=1, keepdims=True), (N_EXPERTS, LANES))
    padded = total + (MOE_ROWS - 1)
    nblk = jnp.floor(padded * (1.0 / MOE_ROWS))
    rem = padded - nblk * MOE_ROWS
    nblk = jnp.where(rem >= MOE_ROWS, nblk + 1.0, jnp.where(rem < 0.0, nblk - 1.0, nblk))
    r = lax.broadcasted_iota(I32, (N_EXPERTS, N_EXPERTS), 0)
    c = lax.broadcasted_iota(I32, (N_EXPERTS, N_EXPERTS), 1)
    first_blk = lax.dot_general((c < r).astype(F32), nblk, (((1,), (0,)), ((), ())),
                                precision=lax.Precision.HIGHEST, preferred_element_type=F32)
    start = (first_blk + 1.0) * MOE_ROWS
    lane = lax.broadcasted_iota(I32, (N_EXPERTS, LANES), 1)
    info = jnp.where(lane == 0, start, jnp.where(lane == 1, start + total, jnp.where(lane == 2, nblk, first_blk)))
    off_ref[...] = info.astype(I32)
    ti = lax.broadcasted_iota(I32, (LANES, LANES), 0)
    tj = lax.broadcasted_iota(I32, (LANES, LANES), 1)
    before = (ti < tj).astype(BF16)
    ones = jnp.ones((LANES, LANES), BF16)
    running = start
    for t in range(SLOT_TILES):
        sl = slice(t * LANES, (t + 1) * LANES)
        cb = count[:, sl].astype(BF16)
        pos = running + _bdot(cb, before)
        rows = [jnp.sum(jnp.where(oh[:, sl], pos, 0.0), axis=0, keepdims=True) for oh in onehots]
        dest_ref[:, sl] = jnp.concatenate(rows, axis=0).astype(I32)
        running = running + _bdot(cb, ones)


def _route_plan(logits_p, logits_s):
    in_specs = [
        pl.BlockSpec((N_EXPERTS, GROUP_PROMPT), lambda g: (0, g)),
        pl.BlockSpec((N_EXPERTS, DEC_BATCH), lambda g: (0, 0)),
    ]
    out_shape = [
        jax.ShapeDtypeStruct((N_GROUPS, TOP_K, K_STRIDE), I32),
        jax.ShapeDtypeStruct((N_GROUPS, TOP_K, K_STRIDE), F32),
        jax.ShapeDtypeStruct((N_GROUPS, N_EXPERTS, LANES), I32),
    ]
    out_specs = [
        pl.BlockSpec((None, TOP_K, K_STRIDE), lambda g: (g, 0, 0)),
        pl.BlockSpec((None, TOP_K, K_STRIDE), lambda g: (g, 0, 0)),
        pl.BlockSpec((None, N_EXPERTS, LANES), lambda g: (g, 0, 0)),
    ]
    return pl.pallas_call(
        _route_plan_kernel,
        grid=(N_GROUPS,),
        in_specs=in_specs,
        out_specs=out_specs,
        out_shape=out_shape,
        compiler_params=pltpu.CompilerParams(dimension_semantics=("arbitrary",)),
        name="route_plan",
    )(logits_p, logits_s)


GROUP_ROWS = GROUP_PROMPT * LANE_CHUNKS
SAMPLE_ROWS = DEC_BATCH * LANE_CHUNKS
TRASH_SLOT = GROUP_SLOTS
BUF_ROWS = (GROUP_SLOTS + 1) * LANE_CHUNKS
SCATTER_BATCH = 8
DMA_SPLIT = 8


def _moe_kernel(off_ref, desth_ref, wtsh_ref, h2p_ref, h2s_ref, x1p_ref, x1s_ref, wgu_ref, bgu_ref, wdn_ref, bdn_ref,
                x2p_ref, x2s_ref,
                h2buf, acc, wgubuf, bgubuf, wdnbuf, bdnbuf, xs0, xs1, ys0, ys1,
                dest_ref, wts_ref, src_ref, seg_expert, seg_first, blk_seg, act_sem, w_sem):
    g = pl.program_id(0)
    last = g == N_GROUPS - 1
    row0 = pl.multiple_of(g * GROUP_ROWS, GROUP_ROWS)

    def prompt_copies():
        cps = []
        for j in range(DMA_SPLIT):
            src = pl.ds(row0 + j * (GROUP_ROWS // DMA_SPLIT), GROUP_ROWS // DMA_SPLIT)
            dst = pl.ds(j * (GROUP_ROWS // DMA_SPLIT), GROUP_ROWS // DMA_SPLIT)
            cps.append(pltpu.make_async_copy(h2p_ref.at[src], h2buf.at[dst], act_sem.at[0]))
            cps.append(pltpu.make_async_copy(x1p_ref.at[src], acc.at[dst], act_sem.at[1]))
        return cps

    def sample_copies():
        return (pltpu.make_async_copy(h2s_ref, h2buf.at[pl.ds(GROUP_ROWS, SAMPLE_ROWS)], act_sem.at[2]),
                pltpu.make_async_copy(x1s_ref, acc.at[pl.ds(GROUP_ROWS, SAMPLE_ROWS)], act_sem.at[3]))

    def weight_copies(e, slot):
        cps = [pltpu.make_async_copy(bgu_ref.at[e], bgubuf.at[slot], w_sem.at[1, slot]),
               pltpu.make_async_copy(bdn_ref.at[e], bdnbuf.at[slot], w_sem.at[3, slot])]
        for j in range(DMA_SPLIT):
            rg = pl.ds(j * (D_MODEL // DMA_SPLIT), D_MODEL // DMA_SPLIT)
            rd = pl.ds(j * (D_EXPERT // DMA_SPLIT), D_EXPERT // DMA_SPLIT)
            cps.append(pltpu.make_async_copy(wgu_ref.at[e, rg], wgubuf.at[slot, rg], w_sem.at[0, slot]))
            cps.append(pltpu.make_async_copy(wdn_ref.at[e, rd], wdnbuf.at[slot, rd], w_sem.at[2, slot]))
        return cps

    def output_copies():
        return [pltpu.make_async_copy(
            acc.at[pl.ds(j * (GROUP_ROWS // DMA_SPLIT), GROUP_ROWS // DMA_SPLIT)],
            x2p_ref.at[pl.ds(row0 + j * (GROUP_ROWS // DMA_SPLIT), GROUP_ROWS // DMA_SPLIT)], act_sem.at[0])
            for j in range(DMA_SPLIT)]

    tab0 = pl.multiple_of(g * (TOP_K * K_STRIDE), TOP_K * K_STRIDE)
    table_copies = (
        pltpu.make_async_copy(desth_ref.at[pl.ds(tab0, TOP_K * K_STRIDE)], dest_ref, act_sem.at[4]),
        pltpu.make_async_copy(wtsh_ref.at[pl.ds(tab0, TOP_K * K_STRIDE)], wts_ref, act_sem.at[5]))
    for cp in table_copies:
        cp.start()

    for cp in prompt_copies():
        cp.start()

    @pl.when(last)
    def _():
        for cp in sample_copies():
            cp.start()

    trash = pl.ds(TRASH_SLOT * LANE_CHUNKS, LANE_CHUNKS)
    h2buf[trash, :] = jnp.zeros((LANE_CHUNKS, LANES), F32)
    acc[trash, :] = jnp.zeros((LANE_CHUNKS, LANES), F32)
    ys1[...] = jnp.zeros_like(ys1)

    def pad_block(pos0):
        def body(j, carry):
            for d in range(SUBLANES):
                src_ref[pos0 + j * SUBLANES + d] = TRASH_SLOT
            return carry
        lax.fori_loop(0, MOE_ROWS // SUBLANES, body, 0)

    def scan_expert(e, carry):
        nseg, nblocks = carry
        nblk = off_ref[e, 2]
        first = off_ref[e, 3]

        @pl.when(nblk > 0)
        def _():
            seg_expert[nseg] = e
            seg_first[nseg] = first
            pad_block(off_ref[e, 0] + (nblk - 1) * MOE_ROWS)

            def mark(b, c2):
                blk_seg[first + b] = nseg
                return c2
            lax.fori_loop(0, nblk, mark, 0)

        return nseg + jnp.where(nblk > 0, 1, 0), nblocks + nblk

    nseg, nblocks = lax.fori_loop(0, N_EXPERTS, scan_expert, (jnp.int32(0), jnp.int32(0)))
    pad_block(0)
    pad_block((nblocks + 1) * MOE_ROWS)
    pad_block((nblocks + 2) * MOE_ROWS)
    blk_seg[nblocks] = nseg - 1
    blk_seg[nblocks + 1] = nseg - 1

    for cp in weight_copies(seg_expert[0], 0):
        cp.start()

    for cp in table_copies:
        cp.wait()

    nvalid = jnp.where(last, GROUP_SLOTS, GROUP_PROMPT)
    for k in range(TOP_K):
        def fill(j, carry, k=k):
            c0 = k * K_STRIDE + j * SUBLANES
            for d in range(SUBLANES):
                src_ref[dest_ref[c0 + d]] = c0 + d
            return carry
        lax.fori_loop(0, nvalid // SUBLANES, fill, 0)

    for cp in prompt_copies():
        cp.wait()

    @pl.when(last)
    def _():
        for cp in sample_copies():
            cp.wait()

    def token_rows(code):
        slot_id = code & (K_STRIDE - 1)
        return pl.ds(pl.multiple_of(slot_id * LANE_CHUNKS, LANE_CHUNKS), LANE_CHUNKS)

    def gather(b, xs):
        base = (b + 1) * MOE_ROWS
        for m in range(MOE_ROWS):
            xs[pl.ds(m, LANE_CHUNKS, stride=XS_STRIDE), :] = h2buf[token_rows(src_ref[base + m]), :]

    def scatter_add(b, ys):
        base = (b + 1) * MOE_ROWS
        for m0 in range(0, MOE_ROWS, SCATTER_BATCH):
            pending = []
            for m in range(m0, m0 + SCATTER_BATCH):
                code = src_ref[base + m]
                rows = token_rows(code)
                pending.append((rows, acc[rows, :] + wts_ref[code] * ys[pl.ds(m, LANE_CHUNKS, stride=XS_STRIDE), :]))
            for rows, val in pending:
                acc[rows, :] = val

    def expert_ffn(xs, ys, slot):
        x = jnp.concatenate(
            [xs[c * XS_STRIDE:c * XS_STRIDE + MOE_ROWS, :] for c in range(LANE_CHUNKS)], axis=1).astype(BF16)
        gu = _bdot(x, wgubuf[slot]) + bgubuf[slot]
        gl = jnp.minimum(gu[:, :D_EXPERT], SWIGLU_LIMIT)
        ul = jnp.clip(gu[:, D_EXPERT:], -SWIGLU_LIMIT, SWIGLU_LIMIT)
        a = (ul + 1.0) * (gl * jax.nn.sigmoid(SWIGLU_ALPHA * gl))
        y = _bdot(a.astype(BF16), wdnbuf[slot]) + bdnbuf[slot]
        for c in range(LANE_CHUNKS):
            ys[c * XS_STRIDE:c * XS_STRIDE + MOE_ROWS, :] = y[:, c * LANES:(c + 1) * LANES]

    def step(b, xs_cur, xs_next, ys_cur, ys_prev):
        seg = blk_seg[b]
        slot = seg & 1

        @pl.when(jnp.logical_and(b == seg_first[seg], b < nblocks))
        def _():
            for cp in weight_copies(seg_expert[seg], slot):
                cp.wait()

            @pl.when(seg + 1 < nseg)
            def _():
                for cp in weight_copies(seg_expert[seg + 1], 1 - slot):
                    cp.start()

        gather(b + 1, xs_next)
        expert_ffn(xs_cur, ys_cur, slot)
        scatter_add(b - 1, ys_prev)

    gather(0, xs0)
    npairs = (nblocks + 1) // 2

    def pair(t, carry):
        step(2 * t, xs0, xs1, ys0, ys1)
        step(2 * t + 1, xs1, xs0, ys1, ys0)
        return carry

    lax.fori_loop(0, npairs, pair, 0)
    scatter_add(2 * npairs - 1, ys1)

    for cp in output_copies():
        cp.start()

    @pl.when(last)
    def _():
        out_s = pltpu.make_async_copy(acc.at[pl.ds(GROUP_ROWS, SAMPLE_ROWS)], x2s_ref, act_sem.at[2])
        out_s.start()
        out_s.wait()

    for cp in output_copies():
        cp.wait()


def _moe(dest, wts, off, h2p, h2s, x1p, x1s, wgu, bgu, wdn, bdn):
    anyspec = pl.BlockSpec(memory_space=pl.ANY)
    dest = dest.reshape(N_GROUPS * TOP_K * K_STRIDE)
    wts = wts.reshape(N_GROUPS * TOP_K * K_STRIDE)
    in_specs = [
        pl.BlockSpec((None, N_EXPERTS, LANES), lambda g: (g, 0, 0), memory_space=pltpu.SMEM),
        anyspec, anyspec, anyspec, anyspec, anyspec, anyspec, anyspec, anyspec, anyspec, anyspec,
    ]
    scratch = [
        pltpu.VMEM((BUF_ROWS, LANES), F32),
        pltpu.VMEM((BUF_ROWS, LANES), F32),
        pltpu.VMEM((2, D_MODEL, 2 * D_EXPERT), BF16),
        pltpu.VMEM((2, 1, 2 * D_EXPERT), F32),
        pltpu.VMEM((2, D_EXPERT, D_MODEL), BF16),
        pltpu.VMEM((2, 1, D_MODEL), F32),
        pltpu.VMEM((LANE_CHUNKS * XS_STRIDE, LANES), F32),
        pltpu.VMEM((LANE_CHUNKS * XS_STRIDE, LANES), F32),
        pltpu.VMEM((LANE_CHUNKS * XS_STRIDE, LANES), F32),
        pltpu.VMEM((LANE_CHUNKS * XS_STRIDE, LANES), F32),
        pltpu.SMEM((TOP_K * K_STRIDE,), I32),
        pltpu.SMEM((TOP_K * K_STRIDE,), F32),
        pltpu.SMEM((POS_TABLE,), I32),
        pltpu.SMEM((N_EXPERTS,), I32),
        pltpu.SMEM((N_EXPERTS,), I32),
        pltpu.SMEM((LANES,), I32),
        pltpu.SemaphoreType.DMA((6,)),
        pltpu.SemaphoreType.DMA((4, 2)),
    ]
    return pl.pallas_call(
        _moe_kernel,
        grid=(N_GROUPS,),
        in_specs=in_specs,
        out_specs=[anyspec, anyspec],
        out_shape=[jax.ShapeDtypeStruct(x1p.shape, F32), jax.ShapeDtypeStruct(x1s.shape, F32)],
        scratch_shapes=scratch,
        compiler_params=pltpu.CompilerParams(dimension_semantics=("arbitrary",), vmem_limit_bytes=VMEM_LIMIT),
        name="moe",
    )(off, dest, wts, h2p, h2s, x1p, x1s, wgu, bgu, wdn, bdn)


def _ple_final_kernel(x2_ref, ple_ref, wple_ref, gple_ref, wpg_ref, gfin_ref, y_ref):
    rows = y_ref.shape[0]
    x2 = _load_token_major(x2_ref, rows)
    e = _rmsnorm(_bdot(ple_ref[...].astype(BF16), wple_ref[...]), gple_ref[...])
    x3 = x2 + jax.nn.sigmoid(_bdot(x2.astype(BF16), wpg_ref[...])) * e
    y_ref[...] = _rmsnorm(x3, gfin_ref[...])


def _ple_final(x2_tm, ple, wple, gple, wpg, gfin, tile):
    n = ple.shape[0]
    return pl.pallas_call(
        _ple_final_kernel,
        grid=(n // tile,),
        in_specs=[
            pl.BlockSpec((tile * LANE_CHUNKS, LANES), lambda i: (i, 0)),
            pl.BlockSpec((tile, PLE_DIM), lambda i: (i, 0)),
            _full((PLE_DIM, D_MODEL)),
            _full((1, D_MODEL)),
            _full((D_MODEL, D_MODEL)),
            _full((1, D_MODEL)),
        ],
        out_specs=pl.BlockSpec((tile, D_MODEL), lambda i: (i, 0)),
        out_shape=jax.ShapeDtypeStruct((n, D_MODEL), F32),
        compiler_params=pltpu.CompilerParams(dimension_semantics=("arbitrary",), vmem_limit_bytes=VMEM_LIMIT),
        name="ple_final",
    )(x2_tm, ple, wple, gple, wpg, gfin)


def _rope_tables(pos):
    half = HEAD_DIM // 2
    inv = ROPE_THETA ** (-jnp.arange(half, dtype=F32) / half)
    ang = pos.astype(F32)[:, None] * inv[None, :]
    cos, sin = jnp.cos(ang), jnp.sin(ang)
    cos2 = jnp.concatenate([cos, cos, cos, cos], axis=1)
    sin2 = jnp.concatenate([-sin, sin, -sin, sin], axis=1)
    return cos2, sin2


def _layout_w_in(w_in):
    o_q = 2 * A_WIDTH
    wq = w_in[:, o_q:o_q + Q_WIDTH].reshape(D_MODEL, N_HEADS, HEAD_DIM) * (HEAD_DIM ** -0.5)
    kv_head = (jnp.arange(N_HEADS) // GQA_GROUP)[None, :, None]
    wq_pad = jnp.concatenate([jnp.where(kv_head == h, wq, 0.0) for h in range(N_KV_HEADS)], axis=-1)
    return jnp.concatenate([w_in[:, :o_q], wq_pad.reshape(D_MODEL, QPAD_WIDTH), w_in[:, o_q + Q_WIDTH:]], axis=1)


def _router_passes(w_router):
    hi = w_router.astype(BF16)
    lo = (w_router - hi.astype(F32)).astype(BF16)
    w3 = jnp.concatenate([hi, lo, hi], axis=0)
    return jnp.pad(w3, ((0, 0), (0, LANES - N_EXPERTS)))


def _prep_weights(g_mix, w_in, a_ln_g, a_ln_b, a_ws, a_bs, w_pa, w_pb, w_o, g_ffn, w_router, b_router):
    causal = jnp.tril(jnp.ones((CHUNK, CHUNK), dtype=bool))
    return dict(
        gmix=g_mix.reshape(1, D_MODEL),
        win=_layout_w_in(w_in).astype(BF16),
        lng=a_ln_g.reshape(1, A_WIDTH),
        lnb=a_ln_b.reshape(1, A_WIDTH),
        ws=jnp.where(causal[None], a_ws, 0.0).astype(BF16),
        bsf=jnp.repeat(jnp.transpose(a_bs), A_GROUP_DIM, axis=1),
        wpa=w_pa.astype(BF16),
        wpb=w_pb.astype(BF16),
        wo=w_o.astype(BF16),
        gffn=g_ffn.reshape(1, D_MODEL),
        wrt=_router_passes(w_router),
        br=b_router.reshape(N_EXPERTS, 1),
    )


def kernel(x_prompt, x_sample, cache_win_k, cache_win_v, p_prompt, p_sample, g_mix, w_in, a_ln_g, a_ln_b, a_ws, a_bs, sinks, w_pa, w_pb, w_o, g_ffn, w_router, b_router, w_gu, b_gu, w_down, b_down, w_ple, g_ple, w_ple_gate, g_final):
    W = _prep_weights(g_mix[0], w_in[0], a_ln_g[0], a_ln_b[0], a_ws[0], a_bs[0], w_pa[0], w_pb[0], w_o[0],
                      g_ffn[0], w_router[0], b_router[0])
    cos_p, sin_p = _rope_tables(jnp.arange(SEQ, dtype=I32))
    cos_s, sin_s = _rope_tables(jnp.full((1,), PAST_LEN, I32))
    x1p, h2p, logits_p, kwin_p, vwin_p, wgu16, wdn16 = _prompt_front(
        x_prompt.reshape(N_PROMPT, D_MODEL), cos_p, sin_p, W["gmix"], W["win"], W["lng"], W["lnb"],
        W["ws"], W["bsf"], sinks[0], _band_bias(), W["wpa"], W["wpb"], W["wo"], W["gffn"], W["wrt"], W["br"],
        w_gu[0].reshape(N_EXPERTS * D_MODEL, 2 * D_EXPERT), w_down[0].reshape(N_EXPERTS * D_EXPERT, D_MODEL))

    wdiag = jnp.repeat(a_ws[0, :, 0, 0], A_GROUP_DIM)[None, :].astype(BF16)
    bs0 = jnp.repeat(a_bs[0, :, 0], A_GROUP_DIM)[None, :]
    x1s, h2s, logits_s, kwin_s, vwin_s, va_s = _sample_front(
        x_sample.reshape(DEC_BATCH, D_MODEL), cos_s, sin_s, W["gmix"], W["win"], W["lng"], W["lnb"], wdiag, bs0,
        sinks[0], cache_win_k[0].reshape(DEC_BATCH, WINDOW, KV_WIDTH), cache_win_v[0].reshape(DEC_BATCH, WINDOW, KV_WIDTH),
        W["wpa"], W["wpb"], W["wo"], W["gffn"], W["wrt"], W["br"])

    dest, wts, off = _route_plan(logits_p, logits_s)
    x2p, x2s = _moe(dest, wts, off, h2p, h2s, x1p, x1s,
                    wgu16.reshape(N_EXPERTS, D_MODEL, 2 * D_EXPERT), b_gu[0].reshape(N_EXPERTS, 1, 2 * D_EXPERT),
                    wdn16.reshape(N_EXPERTS, D_EXPERT, D_MODEL), b_down[0].reshape(N_EXPERTS, 1, D_MODEL))

    wple = w_ple[0].astype(BF16)
    gple = g_ple[0].reshape(1, D_MODEL)
    wpg = w_ple_gate[0].astype(BF16)
    gfin = g_final.reshape(1, D_MODEL)
    y_p = _ple_final(x2p, p_prompt[0].reshape(N_PROMPT, PLE_DIM), wple, gple, wpg, gfin, 2 * TM)
    y_s = _ple_final(x2s, p_sample[0].reshape(DEC_BATCH, PLE_DIM), wple, gple, wpg, gfin, DEC_BATCH)

    return (
        y_p.reshape(BATCH, SEQ, D_MODEL),
        y_s.reshape(DEC_BATCH, 1, D_MODEL),
        kwin_p.reshape(1, BATCH, WINDOW, N_KV_HEADS, HEAD_DIM),
        vwin_p.reshape(1, BATCH, WINDOW, N_KV_HEADS, HEAD_DIM),
        kwin_s.reshape(1, DEC_BATCH, WINDOW, N_KV_HEADS, HEAD_DIM),
        vwin_s.reshape(1, DEC_BATCH, WINDOW, N_KV_HEADS, HEAD_DIM),
        va_s.reshape(1, DEC_BATCH, 1, A_WIDTH),
    )
```

```python
import functools

import jax
import jax.numpy as jnp
from jax import lax
from jax.experimental import pallas as pl
from jax.experimental.pallas import tpu as pltpu

F32 = jnp.float32
BF16 = jnp.bfloat16
I32 = jnp.int32

D_MODEL = 1024
BATCH = 4
SEQ = 4096
DEC_BATCH = 128
PAST_LEN = 8192
CHUNK = 128
A_GROUPS = 4
A_GROUP_DIM = 128
A_WIDTH = A_GROUPS * A_GROUP_DIM
N_HEADS = 8
N_KV_HEADS = 2
HEAD_DIM = 64
Q_WIDTH = N_HEADS * HEAD_DIM
KV_WIDTH = N_KV_HEADS * HEAD_DIM
GQA_GROUP = N_HEADS // N_KV_HEADS
WINDOW = 128
ROPE_THETA = 10000.0
N_EXPERTS = 32
TOP_K = 4
D_EXPERT = D_MODEL
SWIGLU_ALPHA = 1.702
SWIGLU_LIMIT = 7.0
PLE_DIM = 256
RMS_EPS = 1e-5
LN_EPS = 1e-5

LANES = 128

QPAD_WIDTH = N_HEADS * LANES
O_Q = 2 * A_WIDTH
O_K = O_Q + QPAD_WIDTH
O_V = O_K + KV_WIDTH
O_GA = O_V + KV_WIDTH
O_GB = O_GA + D_MODEL
IN_COLS = O_GB + D_MODEL
SUBLANES = 8
LANE_CHUNKS = D_MODEL // LANES
VMEM_LIMIT = 56 * 1024 * 1024

N_PROMPT = BATCH * SEQ
TM = 256
TILES_PER_SEQ = SEQ // TM
BLOCKS_PER_TILE = TM // WINDOW
FRONT_STEPS = N_PROMPT // TM
CAST_ROWS = N_EXPERTS * D_MODEL // FRONT_STEPS
CAST_SPLIT = 4

N_GROUPS = 4
GROUP_PROMPT = N_PROMPT // N_GROUPS
GROUP_SLOTS = GROUP_PROMPT + DEC_BATCH
GROUP_ASSIGN = GROUP_SLOTS * TOP_K
SLOT_TILES = GROUP_SLOTS // LANES
MOE_ROWS = 256
XS_STRIDE = MOE_ROWS + SUBLANES
SLOT_BITS = 13
K_STRIDE = 1 << SLOT_BITS
assert GROUP_SLOTS < K_STRIDE
MAX_BLOCKS = GROUP_ASSIGN // MOE_ROWS + N_EXPERTS
POS_TABLE = 1 << 15
assert (MAX_BLOCKS + 3) * MOE_ROWS <= POS_TABLE
assert MAX_BLOCKS + 2 <= LANES


def _bdot(a, b):
    return jnp.dot(a, b, preferred_element_type=F32)


def _rmsnorm(x, g):
    return x * lax.rsqrt(jnp.mean(x * x, axis=-1, keepdims=True) + RMS_EPS) * g


def _gelu(x):
    return 0.5 * x * (1.0 + lax.erf(x * (0.5 ** 0.5)))


def _group_layernorm(v, g, b):
    cols = []
    for gi in range(A_GROUPS):
        s = slice(gi * A_GROUP_DIM, (gi + 1) * A_GROUP_DIM)
        vg = v[:, s]
        mu = jnp.mean(vg, axis=-1, keepdims=True)
        d = vg - mu
        var = jnp.mean(d * d, axis=-1, keepdims=True)
        cols.append(d * lax.rsqrt(var + LN_EPS) * g[:, s] + b[:, s])
    return jnp.concatenate(cols, axis=1)


def _rope(x, cos, sin_signed):
    width = x.shape[1]
    reps = width // LANES
    cosf = jnp.concatenate([cos] * reps, axis=1) if reps > 1 else cos
    sinf = jnp.concatenate([sin_signed] * reps, axis=1) if reps > 1 else sin_signed
    half = HEAD_DIM // 2
    lane = lax.broadcasted_iota(I32, x.shape, 1)
    up = pltpu.roll(x, width - half, 1)
    down = pltpu.roll(x, half, 1)
    partner = jnp.where((lane & (HEAD_DIM - 1)) < half, up, down)
    return x * cosf + partner * sinf


def _in_projection(x, gmix_ref, win_ref, lng_ref, lnb_ref, cos, sin_signed):
    hb = _rmsnorm(x, gmix_ref[...]).astype(BF16)
    zuv = _gelu(_bdot(hb, win_ref[:, 0:O_Q]))
    u = zuv[:, :A_WIDTH]
    va = _group_layernorm(zuv[:, A_WIDTH:], lng_ref[...], lnb_ref[...])
    zqkv = _bdot(hb, win_ref[:, O_Q:O_GA])
    q = _rope(zqkv[:, :QPAD_WIDTH], cos, sin_signed)
    k = _rope(zqkv[:, QPAD_WIDTH:QPAD_WIDTH + KV_WIDTH], cos, sin_signed)
    v = zqkv[:, QPAD_WIDTH + KV_WIDTH:]
    zg = _bdot(hb, win_ref[:, O_GA:IN_COLS])
    gate_a = jax.nn.sigmoid(zg[:, :D_MODEL])
    gate_b = jax.nn.sigmoid(zg[:, D_MODEL:])
    return u, va, q, k, v, gate_a, gate_b


def _merge_and_route(x, ya_in, att, gate_a, gate_b, wpa_ref, wpb_ref, wo_ref, gffn_ref, wr3_ref, br_ref):
    ya = _bdot(ya_in.astype(BF16), wpa_ref[...])
    yb = _bdot(att.astype(BF16), wpb_ref[...])
    mix = (gate_a * ya + gate_b * yb).astype(BF16)
    x1 = x + _bdot(mix, wo_ref[...])
    h2 = _rmsnorm(x1, gffn_ref[...])
    hi = h2.astype(BF16)
    lo = (h2 - hi.astype(F32)).astype(BF16)
    logits = _bdot(jnp.concatenate([hi, hi, lo], axis=1), wr3_ref[...])
    return x1, h2, jnp.transpose(logits)[:N_EXPERTS, :] + br_ref[...]


def _top4_softmax(logits):
    eid = lax.broadcasted_iota(I32, logits.shape, 0)
    vals, idxs = [], []
    for _ in range(TOP_K):
        m = jnp.max(logits, axis=0, keepdims=True)
        idx = jnp.min(jnp.where(logits == m, eid, N_EXPERTS), axis=0, keepdims=True)
        logits = jnp.where(eid == idx, -jnp.inf, logits)
        vals.append(m)
        idxs.append(idx)
    es = [jnp.exp(v - vals[0]) for v in vals]
    inv = 1.0 / (es[0] + es[1] + es[2] + es[3])
    return jnp.concatenate(idxs, axis=0), jnp.concatenate([e * inv for e in es], axis=0)


def _store_token_major(ref, val):
    rows = val.shape[0]
    for c in range(LANE_CHUNKS):
        ref[pl.ds(c, rows, stride=LANE_CHUNKS), :] = val[:, c * LANES:(c + 1) * LANES]


def _load_token_major(ref, rows):
    return jnp.concatenate([ref[pl.ds(c, rows, stride=LANE_CHUNKS), :] for c in range(LANE_CHUNKS)], axis=1)


def _band_attention(qpad, k, v, k_prev, v_prev, sinks_ref, bias_ref, seq_start):
    kb = jnp.concatenate([k_prev, k], axis=0).astype(BF16)
    vt = jnp.transpose(jnp.concatenate([v_prev, v], axis=0)).astype(BF16)
    qb = qpad.astype(BF16)
    lane = lax.broadcasted_iota(I32, (1, GQA_GROUP * WINDOW), 1)
    blocks = []
    for b in range(BLOCKS_PER_TILE):
        bias = bias_ref[jnp.where(seq_start, 1, 0)] if b == 0 else bias_ref[0]
        keys = kb[b * WINDOW:(b + 2) * WINDOW, :]
        pieces = []
        for h in range(N_KV_HEADS):
            qh = jnp.concatenate(
                [qb[b * WINDOW:(b + 1) * WINDOW, (h * GQA_GROUP + j) * LANES:(h * GQA_GROUP + j + 1) * LANES]
                 for j in range(GQA_GROUP)], axis=0)
            st = lax.dot_general(keys, qh, (((1,), (1,)), ((), ())), preferred_element_type=F32) + bias
            sink = jnp.zeros((1, GQA_GROUP * WINDOW), F32)
            for j in range(GQA_GROUP):
                sink = jnp.where(lane // WINDOW == j, sinks_ref[h * GQA_GROUP + j], sink)
            m = jnp.maximum(jnp.max(st, axis=0, keepdims=True), sink)
            e = jnp.exp(st - m)
            inv = 1.0 / (jnp.sum(e, axis=0, keepdims=True) + jnp.exp(sink - m))
            ot = _bdot(vt[h * HEAD_DIM:(h + 1) * HEAD_DIM, b * WINDOW:(b + 2) * WINDOW], (e * inv).astype(BF16))
            pieces.extend(ot[:, j * WINDOW:(j + 1) * WINDOW] for j in range(GQA_GROUP))
        blocks.append(jnp.transpose(jnp.concatenate(pieces, axis=0)))
    return jnp.concatenate(blocks, axis=0)


def _band_bias():
    kj = lax.broadcasted_iota(I32, (2, 2 * WINDOW, GQA_GROUP * WINDOW), 1)
    qi = lax.broadcasted_iota(I32, (2, 2 * WINDOW, GQA_GROUP * WINDOW), 2) % WINDOW
    lo = lax.broadcasted_iota(I32, (2, 2 * WINDOW, GQA_GROUP * WINDOW), 0) * WINDOW
    valid = (kj > qi) & (kj <= qi + WINDOW) & (kj >= lo)
    return jnp.where(valid, 0.0, -jnp.inf).astype(F32)


def _prompt_front_kernel(x_ref, cos_ref, sin_ref, gmix_ref, win_ref, lng_ref, lnb_ref, ws_ref, bsf_ref,
                         sinks_ref, bias_ref, wpa_ref, wpb_ref, wo_ref, gffn_ref, wr3_ref, br_ref, wgu32_ref, wdn32_ref,
                         x1_ref, h2_ref, logits_ref, kwin_ref, vwin_ref, wgu16_ref, wdn16_ref,
                         kprev_ref, vprev_ref, gu_in, dn_in, gu_out, dn_out, cast_sem):
    i = pl.program_id(0)
    seq_start = (i % TILES_PER_SEQ) == 0

    def cast_rows(c, j):
        part = CAST_ROWS // CAST_SPLIT
        return pl.ds(pl.multiple_of(c * CAST_ROWS + j * part, part), part), pl.ds(j * part, part)

    def cast_in(c, slot):
        cps = []
        for j in range(CAST_SPLIT):
            hbm, loc = cast_rows(c, j)
            cps.append(pltpu.make_async_copy(wgu32_ref.at[hbm], gu_in.at[slot, loc], cast_sem.at[0, slot]))
            cps.append(pltpu.make_async_copy(wdn32_ref.at[hbm], dn_in.at[slot, loc], cast_sem.at[1, slot]))
        return cps

    def cast_out(c, slot):
        cps = []
        for j in range(CAST_SPLIT):
            hbm, loc = cast_rows(c, j)
            cps.append(pltpu.make_async_copy(gu_out.at[slot, loc], wgu16_ref.at[hbm], cast_sem.at[2, slot]))
            cps.append(pltpu.make_async_copy(dn_out.at[slot, loc], wdn16_ref.at[hbm], cast_sem.at[3, slot]))
        return cps

    slot = i & 1

    @pl.when(i == 0)
    def _():
        for cp in cast_in(0, 0):
            cp.start()

    @pl.when(i + 1 < FRONT_STEPS)
    def _():
        for cp in cast_in(i + 1, 1 - slot):
            cp.start()

    for cp in cast_in(i, slot):
        cp.wait()

    @pl.when(i >= 2)
    def _():
        for cp in cast_out(i - 2, slot):
            cp.wait()


    @pl.when(seq_start)
    def _():
        kprev_ref[...] = jnp.zeros_like(kprev_ref)
        vprev_ref[...] = jnp.zeros_like(vprev_ref)

    x = x_ref[...]
    u, va, q, k, v, gate_a, gate_b = _in_projection(
        x, gmix_ref, win_ref, lng_ref, lnb_ref, cos_ref[...], sin_ref[...])

    att = _band_attention(q, k, v, kprev_ref[...], vprev_ref[...], sinks_ref, bias_ref, seq_start)

    gu_out[slot] = gu_in[slot].astype(BF16)
    dn_out[slot] = dn_in[slot].astype(BF16)
    k_last, v_last = k[TM - WINDOW:], v[TM - WINDOW:]
    kprev_ref[...] = k_last
    vprev_ref[...] = v_last
    kwin_ref[0] = k_last
    vwin_ref[0] = v_last

    vab = va.astype(BF16)
    zc = jnp.concatenate(
        [jnp.concatenate(
            [_bdot(ws_ref[g], vab[b * CHUNK:(b + 1) * CHUNK, g * A_GROUP_DIM:(g + 1) * A_GROUP_DIM])
             for g in range(A_GROUPS)], axis=1) + bsf_ref[...]
         for b in range(BLOCKS_PER_TILE)], axis=0)

    x1, h2, logits = _merge_and_route(x, u * zc, att, gate_a, gate_b,
                                      wpa_ref, wpb_ref, wo_ref, gffn_ref, wr3_ref, br_ref)
    _store_token_major(x1_ref, x1)
    _store_token_major(h2_ref, h2)
    logits_ref[...] = logits

    for cp in cast_out(i, slot):
        cp.start()

    @pl.when(i == FRONT_STEPS - 1)
    def _():
        for cp in cast_out(i - 1, 1 - slot) + cast_out(i, slot):
            cp.wait()


def _full(shape):
    return pl.BlockSpec(shape, lambda i: (0,) * len(shape))


def _prompt_front(x, cos, sin, gmix, win, lng, lnb, ws, bsf, sinks, bias, wpa, wpb, wo, gffn, wrt, br, wgu32, wdn32):
    n = x.shape[0]
    assert n == N_PROMPT
    grid = (FRONT_STEPS,)
    anyspec = pl.BlockSpec(memory_space=pl.ANY)
    in_specs = [
        pl.BlockSpec((TM, D_MODEL), lambda i: (i, 0)),
        pl.BlockSpec((TM, LANES), lambda i: (i % TILES_PER_SEQ, 0)),
        pl.BlockSpec((TM, LANES), lambda i: (i % TILES_PER_SEQ, 0)),
        _full((1, D_MODEL)),
        _full((D_MODEL, IN_COLS)),
        _full((1, A_WIDTH)),
        _full((1, A_WIDTH)),
        _full((A_GROUPS, CHUNK, CHUNK)),
        _full((CHUNK, A_WIDTH)),
        pl.BlockSpec(memory_space=pltpu.SMEM),
        _full((2, 2 * WINDOW, GQA_GROUP * WINDOW)),
        _full((A_WIDTH, D_MODEL)),
        _full((Q_WIDTH, D_MODEL)),
        _full((D_MODEL, D_MODEL)),
        _full((1, D_MODEL)),
        _full((3 * D_MODEL, LANES)),
        _full((N_EXPERTS, 1)),
        anyspec,
        anyspec,
    ]
    out_shape = [
        jax.ShapeDtypeStruct((n * LANE_CHUNKS, LANES), F32),
        jax.ShapeDtypeStruct((n * LANE_CHUNKS, LANES), F32),
        jax.ShapeDtypeStruct((N_EXPERTS, n), F32),
        jax.ShapeDtypeStruct((n // SEQ, WINDOW, KV_WIDTH), F32),
        jax.ShapeDtypeStruct((n // SEQ, WINDOW, KV_WIDTH), F32),
        jax.ShapeDtypeStruct(wgu32.shape, BF16),
        jax.ShapeDtypeStruct(wdn32.shape, BF16),
    ]
    out_specs = [
        pl.BlockSpec((TM * LANE_CHUNKS, LANES), lambda i: (i, 0)),
        pl.BlockSpec((TM * LANE_CHUNKS, LANES), lambda i: (i, 0)),
        pl.BlockSpec((N_EXPERTS, TM), lambda i: (0, i)),
        pl.BlockSpec((1, WINDOW, KV_WIDTH), lambda i: (i // TILES_PER_SEQ, 0, 0)),
        pl.BlockSpec((1, WINDOW, KV_WIDTH), lambda i: (i // TILES_PER_SEQ, 0, 0)),
        anyspec,
        anyspec,
    ]
    scratch = [
        pltpu.VMEM((WINDOW, KV_WIDTH), F32),
        pltpu.VMEM((WINDOW, KV_WIDTH), F32),
        pltpu.VMEM((2, CAST_ROWS, 2 * D_EXPERT), F32),
        pltpu.VMEM((2, CAST_ROWS, D_MODEL), F32),
        pltpu.VMEM((2, CAST_ROWS, 2 * D_EXPERT), BF16),
        pltpu.VMEM((2, CAST_ROWS, D_MODEL), BF16),
        pltpu.SemaphoreType.DMA((4, 2)),
    ]
    return pl.pallas_call(
        _prompt_front_kernel,
        grid=grid,
        in_specs=in_specs,
        out_specs=out_specs,
        out_shape=out_shape,
        scratch_shapes=scratch,
        compiler_params=pltpu.CompilerParams(dimension_semantics=("arbitrary",), vmem_limit_bytes=VMEM_LIMIT),
        name="prompt_front",
    )(x, cos, sin, gmix, win, lng, lnb, ws, bsf, sinks, bias, wpa, wpb, wo, gffn, wrt, br, wgu32, wdn32)


SAMPLE_STEP = 16
SAMPLE_STEPS = DEC_BATCH // SAMPLE_STEP


def _sample_kernel(x_ref, cos_ref, sin_ref, gmix_ref, win_ref, lng_ref, lnb_ref, wdiag_ref, bs0_ref, sinks_ref,
                   kc_ref, vc_ref, wpa_ref, wpb_ref, wo_ref, gffn_ref, wr3_ref, br_ref,
                   x1_ref, h2_ref, logits_ref, kwin_ref, vwin_ref, va_ref,
                   q_s, k_s, v_s, yain_s, ga_s, gb_s, att_s):
    i = pl.program_id(0)

    @pl.when(i == 0)
    def _():
        x = x_ref[...]
        cos = jnp.broadcast_to(cos_ref[...], (DEC_BATCH, LANES))
        sin = jnp.broadcast_to(sin_ref[...], (DEC_BATCH, LANES))
        u, va, q, k, v, gate_a, gate_b = _in_projection(x, gmix_ref, win_ref, lng_ref, lnb_ref, cos, sin)
        va_ref[...] = va
        z = wdiag_ref[...].astype(F32) * va.astype(BF16).astype(F32) + bs0_ref[...]
        yain_s[...] = u * z
        q_s[...] = q
        k_s[...] = k
        v_s[...] = v
        ga_s[...] = gate_a
        gb_s[...] = gate_b

    r0 = pl.multiple_of(i * SAMPLE_STEP, SAMPLE_STEP)
    kwin = jnp.concatenate([kc_ref[:, 1:, :], k_s[pl.ds(r0, SAMPLE_STEP), :][:, None, :]], axis=1)
    vwin = jnp.concatenate([vc_ref[:, 1:, :], v_s[pl.ds(r0, SAMPLE_STEP), :][:, None, :]], axis=1)
    kwin_ref[...] = kwin
    vwin_ref[...] = vwin

    q16 = q_s[pl.ds(r0, SAMPLE_STEP), :]
    lane = lax.broadcasted_iota(I32, (SAMPLE_STEP, LANES), 1)
    heads = [q16[:, hq * LANES:(hq + 1) * LANES] for hq in range(N_HEADS)]
    qpad = pltpu.einshape("hbd->bhd", jnp.stack(heads, axis=0)).astype(BF16)
    s = jnp.einsum("bhd,bkd->bhk", qpad, kwin.astype(BF16), preferred_element_type=F32)
    hid = lax.broadcasted_iota(I32, (1, N_HEADS, 1), 1)
    sink = jnp.zeros((1, N_HEADS, 1), F32)
    for hq in range(N_HEADS):
        sink = jnp.where(hid == hq, sinks_ref[hq], sink)
    m = jnp.maximum(jnp.max(s, axis=-1, keepdims=True), sink)
    e = jnp.exp(s - m)
    inv = 1.0 / (jnp.sum(e, axis=-1, keepdims=True) + jnp.exp(sink - m))
    o = jnp.einsum("bhk,bkd->bhd", (e * inv).astype(BF16), vwin.astype(BF16), preferred_element_type=F32)
    o = pltpu.einshape("bhd->hbd", o)
    chunks = []
    for c in range(N_HEADS // 2):
        parts = []
        for p in range(2):
            hq = 2 * c + p
            oh = o[hq]
            if p != hq // GQA_GROUP:
                oh = pltpu.roll(oh, HEAD_DIM, 1)
            parts.append(oh)
        chunks.append(jnp.where(lane < HEAD_DIM, parts[0], parts[1]))
    att_s[pl.ds(r0, SAMPLE_STEP), :] = jnp.concatenate(chunks, axis=1)

    @pl.when(i == SAMPLE_STEPS - 1)
    def _():
        x1, h2, logits = _merge_and_route(
            x_ref[...], yain_s[...], att_s[...], ga_s[...], gb_s[...],
            wpa_ref, wpb_ref, wo_ref, gffn_ref, wr3_ref, br_ref)
        _store_token_major(x1_ref, x1)
        _store_token_major(h2_ref, h2)
        logits_ref[...] = logits


def _sample_front(x, cos, sin, gmix, win, lng, lnb, wdiag, bs0, sinks, kc, vc, wpa, wpb, wo, gffn, wrt, br):
    n = DEC_BATCH
    cache_spec = pl.BlockSpec((SAMPLE_STEP, WINDOW, KV_WIDTH), lambda i: (i, 0, 0))
    in_specs = [
        _full((n, D_MODEL)),
        _full((1, LANES)),
        _full((1, LANES)),
        _full((1, D_MODEL)),
        _full((D_MODEL, IN_COLS)),
        _full((1, A_WIDTH)),
        _full((1, A_WIDTH)),
        _full((1, A_WIDTH)),
        _full((1, A_WIDTH)),
        pl.BlockSpec(memory_space=pltpu.SMEM),
        cache_spec,
        cache_spec,
        _full((A_WIDTH, D_MODEL)),
        _full((Q_WIDTH, D_MODEL)),
        _full((D_MODEL, D_MODEL)),
        _full((1, D_MODEL)),
        _full((3 * D_MODEL, LANES)),
        _full((N_EXPERTS, 1)),
    ]
    out_shape = [
        jax.ShapeDtypeStruct((n * LANE_CHUNKS, LANES), F32),
        jax.ShapeDtypeStruct((n * LANE_CHUNKS, LANES), F32),
        jax.ShapeDtypeStruct((N_EXPERTS, n), F32),
        jax.ShapeDtypeStruct((n, WINDOW, KV_WIDTH), F32),
        jax.ShapeDtypeStruct((n, WINDOW, KV_WIDTH), F32),
        jax.ShapeDtypeStruct((n, A_WIDTH), F32),
    ]
    out_specs = [
        _full((n * LANE_CHUNKS, LANES)),
        _full((n * LANE_CHUNKS, LANES)),
        _full((N_EXPERTS, n)),
        cache_spec,
        cache_spec,
        _full((n, A_WIDTH)),
    ]
    scratch = [
        pltpu.VMEM((n, QPAD_WIDTH), F32), pltpu.VMEM((n, KV_WIDTH), F32), pltpu.VMEM((n, KV_WIDTH), F32),
        pltpu.VMEM((n, A_WIDTH), F32), pltpu.VMEM((n, D_MODEL), F32), pltpu.VMEM((n, D_MODEL), F32),
        pltpu.VMEM((n, Q_WIDTH), F32),
    ]
    return pl.pallas_call(
        _sample_kernel,
        grid=(SAMPLE_STEPS,),
        in_specs=in_specs,
        out_specs=out_specs,
        out_shape=out_shape,
        scratch_shapes=scratch,
        compiler_params=pltpu.CompilerParams(dimension_semantics=("arbitrary",), vmem_limit_bytes=VMEM_LIMIT),
        name="sample_front",
    )(x, cos, sin, gmix, win, lng, lnb, wdiag, bs0, sinks, kc, vc, wpa, wpb, wo, gffn, wrt, br)


def _route_plan_kernel(lp_ref, ls_ref, dest_ref, wts_ref, off_ref):
    g = pl.program_id(0)
    topi, topw = _top4_softmax(jnp.concatenate([lp_ref[...], ls_ref[...]], axis=1))
    slot = lax.broadcasted_iota(I32, (TOP_K, GROUP_SLOTS), 1)
    eall = jnp.where(jnp.logical_or(slot < GROUP_PROMPT, g == N_GROUPS - 1), topi, N_EXPERTS)
    wts_ref[:, 0:GROUP_SLOTS] = topw
    wts_ref[:, GROUP_SLOTS:] = jnp.zeros((TOP_K, K_STRIDE - GROUP_SLOTS), F32)
    dest_ref[:, GROUP_SLOTS:] = jnp.zeros((TOP_K, K_STRIDE - GROUP_SLOTS), I32)
    eid = lax.broadcasted_iota(I32, (N_EXPERTS, GROUP_SLOTS), 0)
    onehots = [eall[k:k + 1, :] == eid for k in range(TOP_K)]
    count = jnp.zeros((N_EXPERTS, GROUP_SLOTS), F32)
    for oh in onehots:
        count = count + oh.astype(F32)
    total = jnp.broadcast_to(jnp.sum(count, axis=1, keepdims=True), (N_EXPERTS, LANES))
    padded = total + (MOE_ROWS - 1)
    nblk = jnp.floor(padded * (1.0 / MOE_ROWS))
    rem = padded - nblk * MOE_ROWS
    nblk = jnp.where(rem >= MOE_ROWS, nblk + 1.0, jnp.where(rem < 0.0, nblk - 1.0, nblk))
    r = lax.broadcasted_iota(I32, (N_EXPERTS, N_EXPERTS), 0)
    c = lax.broadcasted_iota(I32, (N_EXPERTS, N_EXPERTS), 1)
    first_blk = lax.dot_general((c < r).astype(F32), nblk, (((1,), (0,)), ((), ())),
                                precision=lax.Precision.HIGHEST, preferred_element_type=F32)
    start = (first_blk + 1.0) * MOE_ROWS
    lane = lax.broadcasted_iota(I32, (N_EXPERTS, LANES), 1)
    info = jnp.where(lane == 0, start, jnp.where(lane == 1, start + total, jnp.where(lane == 2, nblk, first_blk)))
    off_ref[...] = info.astype(I32)
    ti = lax.broadcasted_iota(I32, (LANES, LANES), 0)
    tj = lax.broadcasted_iota(I32, (LANES, LANES), 1)
    before = (ti < tj).astype(BF16)
    ones = jnp.ones((LANES, LANES), BF16)
    running = start
    for t in range(SLOT_TILES):
        sl = slice(t * LANES, (t + 1) * LANES)
        cb = count[:, sl].astype(BF16)
        pos = running + _bdot(cb, before)
        rows = [jnp.sum(jnp.where(oh[:, sl], pos, 0.0), axis=0, keepdims=True) for oh in onehots]
        dest_ref[:, sl] = jnp.concatenate(rows, axis=0).astype(I32)
        running = running + _bdot(cb, ones)


def _route_plan(logits_p, logits_s):
    in_specs = [
        pl.BlockSpec((N_EXPERTS, GROUP_PROMPT), lambda g: (0, g)),
        pl.BlockSpec((N_EXPERTS, DEC_BATCH), lambda g: (0, 0)),
    ]
    out_shape = [
        jax.ShapeDtypeStruct((N_GROUPS, TOP_K, K_STRIDE), I32),
        jax.ShapeDtypeStruct((N_GROUPS, TOP_K, K_STRIDE), F32),
        jax.ShapeDtypeStruct((N_GROUPS, N_EXPERTS, LANES), I32),
    ]
    out_specs = [
        pl.BlockSpec((None, TOP_K, K_STRIDE), lambda g: (g, 0, 0)),
        pl.BlockSpec((None, TOP_K, K_STRIDE), lambda g: (g, 0, 0)),
        pl.BlockSpec((None, N_EXPERTS, LANES), lambda g: (g, 0, 0)),
    ]
    return pl.pallas_call(
        _route_plan_kernel,
        grid=(N_GROUPS,),
        in_specs=in_specs,
        out_specs=out_specs,
        out_shape=out_shape,
        compiler_params=pltpu.CompilerParams(dimension_semantics=("arbitrary",)),
        name="route_plan",
    )(logits_p, logits_s)


GROUP_ROWS = GROUP_PROMPT * LANE_CHUNKS
SAMPLE_ROWS = DEC_BATCH * LANE_CHUNKS
TRASH_SLOT = GROUP_SLOTS
BUF_ROWS = (GROUP_SLOTS + 1) * LANE_CHUNKS
SCATTER_BATCH = 8
DMA_SPLIT = 8
FFN_COLS = 256
FFN_DOTS = (2 * D_EXPERT + D_MODEL) // FFN_COLS


def _moe_kernel(off_ref, desth_ref, wtsh_ref, h2p_ref, h2s_ref, x1p_ref, x1s_ref, wgu_ref, bgu_ref, wdn_ref, bdn_ref,
                x2p_ref, x2s_ref,
                h2buf, acc, wgubuf, bgubuf, wdnbuf, bdnbuf, xs0, xs1, ys0, ys1,
                dest_ref, wts_ref, src_ref, seg_expert, seg_first, blk_seg, act_sem, w_sem):
    g = pl.program_id(0)
    last = g == N_GROUPS - 1
    row0 = pl.multiple_of(g * GROUP_ROWS, GROUP_ROWS)

    def prompt_copies():
        cps = []
        for j in range(DMA_SPLIT):
            src = pl.ds(row0 + j * (GROUP_ROWS // DMA_SPLIT), GROUP_ROWS // DMA_SPLIT)
            dst = pl.ds(j * (GROUP_ROWS // DMA_SPLIT), GROUP_ROWS // DMA_SPLIT)
            cps.append(pltpu.make_async_copy(h2p_ref.at[src], h2buf.at[dst], act_sem.at[0]))
            cps.append(pltpu.make_async_copy(x1p_ref.at[src], acc.at[dst], act_sem.at[1]))
        return cps

    def sample_copies():
        return (pltpu.make_async_copy(h2s_ref, h2buf.at[pl.ds(GROUP_ROWS, SAMPLE_ROWS)], act_sem.at[2]),
                pltpu.make_async_copy(x1s_ref, acc.at[pl.ds(GROUP_ROWS, SAMPLE_ROWS)], act_sem.at[3]))

    def weight_copies(e, slot):
        cps = [pltpu.make_async_copy(bgu_ref.at[e], bgubuf.at[slot], w_sem.at[1, slot]),
               pltpu.make_async_copy(bdn_ref.at[e], bdnbuf.at[slot], w_sem.at[3, slot])]
        for j in range(DMA_SPLIT):
            rg = pl.ds(j * (D_MODEL // DMA_SPLIT), D_MODEL // DMA_SPLIT)
            rd = pl.ds(j * (D_EXPERT // DMA_SPLIT), D_EXPERT // DMA_SPLIT)
            cps.append(pltpu.make_async_copy(wgu_ref.at[e, rg], wgubuf.at[slot, rg], w_sem.at[0, slot]))
            cps.append(pltpu.make_async_copy(wdn_ref.at[e, rd], wdnbuf.at[slot, rd], w_sem.at[2, slot]))
        return cps

    def output_copies():
        return [pltpu.make_async_copy(
            acc.at[pl.ds(j * (GROUP_ROWS // DMA_SPLIT), GROUP_ROWS // DMA_SPLIT)],
            x2p_ref.at[pl.ds(row0 + j * (GROUP_ROWS // DMA_SPLIT), GROUP_ROWS // DMA_SPLIT)], act_sem.at[0])
            for j in range(DMA_SPLIT)]

    tab0 = pl.multiple_of(g * (TOP_K * K_STRIDE), TOP_K * K_STRIDE)
    table_copies = (
        pltpu.make_async_copy(desth_ref.at[pl.ds(tab0, TOP_K * K_STRIDE)], dest_ref, act_sem.at[4]),
        pltpu.make_async_copy(wtsh_ref.at[pl.ds(tab0, TOP_K * K_STRIDE)], wts_ref, act_sem.at[5]))
    for cp in table_copies:
        cp.start()

    for cp in prompt_copies():
        cp.start()

    @pl.when(last)
    def _():
        for cp in sample_copies():
            cp.start()

    trash = pl.ds(TRASH_SLOT * LANE_CHUNKS, LANE_CHUNKS)
    h2buf[trash, :] = jnp.zeros((LANE_CHUNKS, LANES), F32)
    acc[trash, :] = jnp.zeros((LANE_CHUNKS, LANES), F32)
    ys1[...] = jnp.zeros_like(ys1)

    def pad_block(pos0):
        def body(j, carry):
            for d in range(SUBLANES):
                src_ref[pos0 + j * SUBLANES + d] = TRASH_SLOT
            return carry
        lax.fori_loop(0, MOE_ROWS // SUBLANES, body, 0)

    def scan_expert(e, carry):
        nseg, nblocks = carry
        nblk = off_ref[e, 2]
        first = off_ref[e, 3]

        @pl.when(nblk > 0)
        def _():
            seg_expert[nseg] = e
            seg_first[nseg] = first
            pad_block(off_ref[e, 0] + (nblk - 1) * MOE_ROWS)

            def mark(b, c2):
                blk_seg[first + b] = nseg
                return c2
            lax.fori_loop(0, nblk, mark, 0)

        return nseg + jnp.where(nblk > 0, 1, 0), nblocks + nblk

    nseg, nblocks = lax.fori_loop(0, N_EXPERTS, scan_expert, (jnp.int32(0), jnp.int32(0)))
    pad_block(0)
    pad_block((nblocks + 1) * MOE_ROWS)
    pad_block((nblocks + 2) * MOE_ROWS)
    blk_seg[nblocks] = nseg - 1
    blk_seg[nblocks + 1] = nseg - 1

    for cp in weight_copies(seg_expert[0], 0):
        cp.start()

    for cp in table_copies:
        cp.wait()

    nvalid = jnp.where(last, GROUP_SLOTS, GROUP_PROMPT)
    for k in range(TOP_K):
        def fill(j, carry, k=k):
            c0 = k * K_STRIDE + j * SUBLANES
            for d in range(SUBLANES):
                src_ref[dest_ref[c0 + d]] = c0 + d
            return carry
        lax.fori_loop(0, nvalid // SUBLANES, fill, 0)

    for cp in prompt_copies():
        cp.wait()

    @pl.when(last)
    def _():
        for cp in sample_copies():
            cp.wait()

    def token_rows(code):
        slot_id = code & (K_STRIDE - 1)
        return pl.ds(pl.multiple_of(slot_id * LANE_CHUNKS, LANE_CHUNKS), LANE_CHUNKS)

    def gather(b, xs, lo=0, hi=MOE_ROWS):
        base = (b + 1) * MOE_ROWS
        for m in range(lo, hi):
            xs[pl.ds(m, LANE_CHUNKS, stride=XS_STRIDE), :] = h2buf[token_rows(src_ref[base + m]), :]

    def scatter_add(b, ys, lo=0, hi=MOE_ROWS):
        base = (b + 1) * MOE_ROWS
        for m0 in range(lo, hi, SCATTER_BATCH):
            pending = []
            for m in range(m0, m0 + SCATTER_BATCH):
                code = src_ref[base + m]
                rows = token_rows(code)
                pending.append((rows, acc[rows, :] + wts_ref[code] * ys[pl.ds(m, LANE_CHUNKS, stride=XS_STRIDE), :]))
            for rows, val in pending:
                acc[rows, :] = val

    gather_cuts = [round(i * MOE_ROWS / FFN_DOTS) for i in range(FFN_DOTS + 1)]
    scatter_cuts = [SCATTER_BATCH * round(i * (MOE_ROWS // SCATTER_BATCH) / FFN_DOTS) for i in range(FFN_DOTS + 1)]

    def step(b, xs_cur, xs_next, ys_cur, ys_prev):
        seg = blk_seg[b]
        slot = seg & 1

        @pl.when(jnp.logical_and(b == seg_first[seg], b < nblocks))
        def _():
            for cp in weight_copies(seg_expert[seg], slot):
                cp.wait()

            @pl.when(seg + 1 < nseg)
            def _():
                for cp in weight_copies(seg_expert[seg + 1], 1 - slot):
                    cp.start()

        done = [0]

        def row_traffic():
            i = done[0]
            gather(b + 1, xs_next, gather_cuts[i], gather_cuts[i + 1])
            scatter_add(b - 1, ys_prev, scatter_cuts[i], scatter_cuts[i + 1])
            done[0] = i + 1

        x = jnp.concatenate(
            [xs_cur[c * XS_STRIDE:c * XS_STRIDE + MOE_ROWS, :] for c in range(LANE_CHUNKS)], axis=1).astype(BF16)
        acts = []
        for c in range(D_EXPERT // FFN_COLS):
            gc = pl.ds(c * FFN_COLS, FFN_COLS)
            uc = pl.ds(D_EXPERT + c * FFN_COLS, FFN_COLS)
            g = _bdot(x, wgubuf[slot, :, gc]) + bgubuf[slot, :, gc]
            row_traffic()
            u = _bdot(x, wgubuf[slot, :, uc]) + bgubuf[slot, :, uc]
            row_traffic()
            gl = jnp.minimum(g, SWIGLU_LIMIT)
            ul = jnp.clip(u, -SWIGLU_LIMIT, SWIGLU_LIMIT)
            acts.append((ul + 1.0) * (gl * jax.nn.sigmoid(SWIGLU_ALPHA * gl)))
        a = jnp.concatenate(acts, axis=1).astype(BF16)
        for c in range(D_MODEL // FFN_COLS):
            oc = pl.ds(c * FFN_COLS, FFN_COLS)
            y = _bdot(a, wdnbuf[slot, :, oc]) + bdnbuf[slot, :, oc]
            for j in range(FFN_COLS // LANES):
                lc = c * (FFN_COLS // LANES) + j
                ys_cur[lc * XS_STRIDE:lc * XS_STRIDE + MOE_ROWS, :] = y[:, j * LANES:(j + 1) * LANES]
            row_traffic()

    gather(0, xs0)
    npairs = (nblocks + 1) // 2

    def pair(t, carry):
        step(2 * t, xs0, xs1, ys0, ys1)
        step(2 * t + 1, xs1, xs0, ys1, ys0)
        return carry

    lax.fori_loop(0, npairs, pair, 0)
    scatter_add(2 * npairs - 1, ys1)

    for cp in output_copies():
        cp.start()

    @pl.when(last)
    def _():
        out_s = pltpu.make_async_copy(acc.at[pl.ds(GROUP_ROWS, SAMPLE_ROWS)], x2s_ref, act_sem.at[2])
        out_s.start()
        out_s.wait()

    for cp in output_copies():
        cp.wait()


def _moe(dest, wts, off, h2p, h2s, x1p, x1s, wgu, bgu, wdn, bdn):
    anyspec = pl.BlockSpec(memory_space=pl.ANY)
    dest = dest.reshape(N_GROUPS * TOP_K * K_STRIDE)
    wts = wts.reshape(N_GROUPS * TOP_K * K_STRIDE)
    in_specs = [
        pl.BlockSpec((None, N_EXPERTS, LANES), lambda g: (g, 0, 0), memory_space=pltpu.SMEM),
        anyspec, anyspec, anyspec, anyspec, anyspec, anyspec, anyspec, anyspec, anyspec, anyspec,
    ]
    scratch = [
        pltpu.VMEM((BUF_ROWS, LANES), F32),
        pltpu.VMEM((BUF_ROWS, LANES), F32),
        pltpu.VMEM((2, D_MODEL, 2 * D_EXPERT), BF16),
        pltpu.VMEM((2, 1, 2 * D_EXPERT), F32),
        pltpu.VMEM((2, D_EXPERT, D_MODEL), BF16),
        pltpu.VMEM((2, 1, D_MODEL), F32),
        pltpu.VMEM((LANE_CHUNKS * XS_STRIDE, LANES), F32),
        pltpu.VMEM((LANE_CHUNKS * XS_STRIDE, LANES), F32),
        pltpu.VMEM((LANE_CHUNKS * XS_STRIDE, LANES), F32),
        pltpu.VMEM((LANE_CHUNKS * XS_STRIDE, LANES), F32),
        pltpu.SMEM((TOP_K * K_STRIDE,), I32),
        pltpu.SMEM((TOP_K * K_STRIDE,), F32),
        pltpu.SMEM((POS_TABLE,), I32),
        pltpu.SMEM((N_EXPERTS,), I32),
        pltpu.SMEM((N_EXPERTS,), I32),
        pltpu.SMEM((LANES,), I32),
        pltpu.SemaphoreType.DMA((6,)),
        pltpu.SemaphoreType.DMA((4, 2)),
    ]
    return pl.pallas_call(
        _moe_kernel,
        grid=(N_GROUPS,),
        in_specs=in_specs,
        out_specs=[anyspec, anyspec],
        out_shape=[jax.ShapeDtypeStruct(x1p.shape, F32), jax.ShapeDtypeStruct(x1s.shape, F32)],
        scratch_shapes=scratch,
        compiler_params=pltpu.CompilerParams(dimension_semantics=("arbitrary",), vmem_limit_bytes=VMEM_LIMIT),
        name="moe",
    )(off, dest, wts, h2p, h2s, x1p, x1s, wgu, bgu, wdn, bdn)


def _ple_final_kernel(x2_ref, ple_ref, wple_ref, gple_ref, wpg_ref, gfin_ref, y_ref):
    rows = y_ref.shape[0]
    x2 = _load_token_major(x2_ref, rows)
    e = _rmsnorm(_bdot(ple_ref[...].astype(BF16), wple_ref[...]), gple_ref[...])
    x3 = x2 + jax.nn.sigmoid(_bdot(x2.astype(BF16), wpg_ref[...])) * e
    y_ref[...] = _rmsnorm(x3, gfin_ref[...])


def _ple_final(x2_tm, ple, wple, gple, wpg, gfin, tile):
    n = ple.shape[0]
    return pl.pallas_call(
        _ple_final_kernel,
        grid=(n // tile,),
        in_specs=[
            pl.BlockSpec((tile * LANE_CHUNKS, LANES), lambda i: (i, 0)),
            pl.BlockSpec((tile, PLE_DIM), lambda i: (i, 0)),
            _full((PLE_DIM, D_MODEL)),
            _full((1, D_MODEL)),
            _full((D_MODEL, D_MODEL)),
            _full((1, D_MODEL)),
        ],
        out_specs=pl.BlockSpec((tile, D_MODEL), lambda i: (i, 0)),
        out_shape=jax.ShapeDtypeStruct((n, D_MODEL), F32),
        compiler_params=pltpu.CompilerParams(dimension_semantics=("arbitrary",), vmem_limit_bytes=VMEM_LIMIT),
        name="ple_final",
    )(x2_tm, ple, wple, gple, wpg, gfin)


def _rope_tables(pos):
    half = HEAD_DIM // 2
    inv = ROPE_THETA ** (-jnp.arange(half, dtype=F32) / half)
    ang = pos.astype(F32)[:, None] * inv[None, :]
    cos, sin = jnp.cos(ang), jnp.sin(ang)
    cos2 = jnp.concatenate([cos, cos, cos, cos], axis=1)
    sin2 = jnp.concatenate([-sin, sin, -sin, sin], axis=1)
    return cos2, sin2


def _layout_w_in(w_in):
    o_q = 2 * A_WIDTH
    wq = w_in[:, o_q:o_q + Q_WIDTH].reshape(D_MODEL, N_HEADS, HEAD_DIM) * (HEAD_DIM ** -0.5)
    kv_head = (jnp.arange(N_HEADS) // GQA_GROUP)[None, :, None]
    wq_pad = jnp.concatenate([jnp.where(kv_head == h, wq, 0.0) for h in range(N_KV_HEADS)], axis=-1)
    return jnp.concatenate([w_in[:, :o_q], wq_pad.reshape(D_MODEL, QPAD_WIDTH), w_in[:, o_q + Q_WIDTH:]], axis=1)


def _router_passes(w_router):
    hi = w_router.astype(BF16)
    lo = (w_router - hi.astype(F32)).astype(BF16)
    w3 = jnp.concatenate([hi, lo, hi], axis=0)
    return jnp.pad(w3, ((0, 0), (0, LANES - N_EXPERTS)))


def _prep_weights(g_mix, w_in, a_ln_g, a_ln_b, a_ws, a_bs, w_pa, w_pb, w_o, g_ffn, w_router, b_router):
    causal = jnp.tril(jnp.ones((CHUNK, CHUNK), dtype=bool))
    return dict(
        gmix=g_mix.reshape(1, D_MODEL),
        win=_layout_w_in(w_in).astype(BF16),
        lng=a_ln_g.reshape(1, A_WIDTH),
        lnb=a_ln_b.reshape(1, A_WIDTH),
        ws=jnp.where(causal[None], a_ws, 0.0).astype(BF16),
        bsf=jnp.repeat(jnp.transpose(a_bs), A_GROUP_DIM, axis=1),
        wpa=w_pa.astype(BF16),
        wpb=w_pb.astype(BF16),
        wo=w_o.astype(BF16),
        gffn=g_ffn.reshape(1, D_MODEL),
        wrt=_router_passes(w_router),
        br=b_router.reshape(N_EXPERTS, 1),
    )


def kernel(x_prompt, x_sample, cache_win_k, cache_win_v, p_prompt, p_sample, g_mix, w_in, a_ln_g, a_ln_b, a_ws, a_bs, sinks, w_pa, w_pb, w_o, g_ffn, w_router, b_router, w_gu, b_gu, w_down, b_down, w_ple, g_ple, w_ple_gate, g_final):
    W = _prep_weights(g_mix[0], w_in[0], a_ln_g[0], a_ln_b[0], a_ws[0], a_bs[0], w_pa[0], w_pb[0], w_o[0],
                      g_ffn[0], w_router[0], b_router[0])
    cos_p, sin_p = _rope_tables(jnp.arange(SEQ, dtype=I32))
    cos_s, sin_s = _rope_tables(jnp.full((1,), PAST_LEN, I32))
    x1p, h2p, logits_p, kwin_p, vwin_p, wgu16, wdn16 = _prompt_front(
        x_prompt.reshape(N_PROMPT, D_MODEL), cos_p, sin_p, W["gmix"], W["win"], W["lng"], W["lnb"],
        W["ws"], W["bsf"], sinks[0], _band_bias(), W["wpa"], W["wpb"], W["wo"], W["gffn"], W["wrt"], W["br"],
        w_gu[0].reshape(N_EXPERTS * D_MODEL, 2 * D_EXPERT), w_down[0].reshape(N_EXPERTS * D_EXPERT, D_MODEL))

    wdiag = jnp.repeat(a_ws[0, :, 0, 0], A_GROUP_DIM)[None, :].astype(BF16)
    bs0 = jnp.repeat(a_bs[0, :, 0], A_GROUP_DIM)[None, :]
    x1s, h2s, logits_s, kwin_s, vwin_s, va_s = _sample_front(
        x_sample.reshape(DEC_BATCH, D_MODEL), cos_s, sin_s, W["gmix"], W["win"], W["lng"], W["lnb"], wdiag, bs0,
        sinks[0], cache_win_k[0].reshape(DEC_BATCH, WINDOW, KV_WIDTH), cache_win_v[0].reshape(DEC_BATCH, WINDOW, KV_WIDTH),
        W["wpa"], W["wpb"], W["wo"], W["gffn"], W["wrt"], W["br"])

    dest, wts, off = _route_plan(logits_p, logits_s)
    x2p, x2s = _moe(dest, wts, off, h2p, h2s, x1p, x1s,
                    wgu16.reshape(N_EXPERTS, D_MODEL, 2 * D_EXPERT), b_gu[0].reshape(N_EXPERTS, 1, 2 * D_EXPERT),
                    wdn16.reshape(N_EXPERTS, D_EXPERT, D_MODEL), b_down[0].reshape(N_EXPERTS, 1, D_MODEL))

    wple = w_ple[0].astype(BF16)
    gple = g_ple[0].reshape(1, D_MODEL)
    wpg = w_ple_gate[0].astype(BF16)
    gfin = g_final.reshape(1, D_MODEL)
    y_p = _ple_final(x2p, p_prompt[0].reshape(N_PROMPT, PLE_DIM), wple, gple, wpg, gfin, 2 * TM)
    y_s = _ple_final(x2s, p_sample[0].reshape(DEC_BATCH, PLE_DIM), wple, gple, wpg, gfin, DEC_BATCH)

    return (
        y_p.reshape(BATCH, SEQ, D_MODEL),
        y_s.reshape(DEC_BATCH, 1, D_MODEL),
        kwin_p.reshape(1, BATCH, WINDOW, N_KV_HEADS, HEAD_DIM),
        vwin_p.reshape(1, BATCH, WINDOW, N_KV_HEADS, HEAD_DIM),
        kwin_s.reshape(1, DEC_BATCH, WINDOW, N_KV_HEADS, HEAD_DIM),
        vwin_s.reshape(1, DEC_BATCH, WINDOW, N_KV_HEADS, HEAD_DIM),
        va_s.reshape(1, DEC_BATCH, 1, A_WIDTH),
    )
```

```python
import functools

import jax
import jax.numpy as jnp
from jax import lax
from jax.experimental import pallas as pl
from jax.experimental.pallas import tpu as pltpu

F32 = jnp.float32
BF16 = jnp.bfloat16
I32 = jnp.int32

D_MODEL = 1024
BATCH = 4
SEQ = 4096
DEC_BATCH = 128
PAST_LEN = 8192
CHUNK = 128
A_GROUPS = 4
A_GROUP_DIM = 128
A_WIDTH = A_GROUPS * A_GROUP_DIM
N_HEADS = 8
N_KV_HEADS = 2
HEAD_DIM = 64
Q_WIDTH = N_HEADS * HEAD_DIM
KV_WIDTH = N_KV_HEADS * HEAD_DIM
GQA_GROUP = N_HEADS // N_KV_HEADS
WINDOW = 128
ROPE_THETA = 10000.0
N_EXPERTS = 32
TOP_K = 4
D_EXPERT = D_MODEL
SWIGLU_ALPHA = 1.702
SWIGLU_LIMIT = 7.0
PLE_DIM = 256
RMS_EPS = 1e-5
LN_EPS = 1e-5

LANES = 128

QPAD_WIDTH = N_HEADS * LANES
O_Q = 2 * A_WIDTH
O_K = O_Q + QPAD_WIDTH
O_V = O_K + KV_WIDTH
O_GA = O_V + KV_WIDTH
O_GB = O_GA + D_MODEL
IN_COLS = O_GB + D_MODEL
SUBLANES = 8
LANE_CHUNKS = D_MODEL // LANES
VMEM_LIMIT = 56 * 1024 * 1024

N_PROMPT = BATCH * SEQ
TM = 256
TILES_PER_SEQ = SEQ // TM
BLOCKS_PER_TILE = TM // WINDOW
FRONT_STEPS = N_PROMPT // TM
CAST_ROWS = N_EXPERTS * D_MODEL // FRONT_STEPS
CAST_SPLIT = 4

N_GROUPS = 4
GROUP_PROMPT = N_PROMPT // N_GROUPS
GROUP_SLOTS = GROUP_PROMPT + DEC_BATCH
GROUP_ASSIGN = GROUP_SLOTS * TOP_K
SLOT_TILES = GROUP_SLOTS // LANES
MOE_ROWS = 256
XS_STRIDE = MOE_ROWS + SUBLANES
SLOT_BITS = 13
K_STRIDE = 1 << SLOT_BITS
assert GROUP_SLOTS < K_STRIDE
MAX_BLOCKS = GROUP_ASSIGN // MOE_ROWS + N_EXPERTS
POS_TABLE = 1 << 15
LEAD_BLOCKS = 2
TAIL_BLOCKS = 4
assert (MAX_BLOCKS + LEAD_BLOCKS + TAIL_BLOCKS) * MOE_ROWS <= POS_TABLE
assert MAX_BLOCKS + TAIL_BLOCKS <= LANES


def _bdot(a, b):
    return jnp.dot(a, b, preferred_element_type=F32)


def _rmsnorm(x, g):
    return x * lax.rsqrt(jnp.mean(x * x, axis=-1, keepdims=True) + RMS_EPS) * g


def _gelu(x):
    return 0.5 * x * (1.0 + lax.erf(x * (0.5 ** 0.5)))


def _group_layernorm(v, g, b):
    cols = []
    for gi in range(A_GROUPS):
        s = slice(gi * A_GROUP_DIM, (gi + 1) * A_GROUP_DIM)
        vg = v[:, s]
        mu = jnp.mean(vg, axis=-1, keepdims=True)
        d = vg - mu
        var = jnp.mean(d * d, axis=-1, keepdims=True)
        cols.append(d * lax.rsqrt(var + LN_EPS) * g[:, s] + b[:, s])
    return jnp.concatenate(cols, axis=1)


def _rope(x, cos, sin_signed):
    width = x.shape[1]
    reps = width // LANES
    cosf = jnp.concatenate([cos] * reps, axis=1) if reps > 1 else cos
    sinf = jnp.concatenate([sin_signed] * reps, axis=1) if reps > 1 else sin_signed
    half = HEAD_DIM // 2
    lane = lax.broadcasted_iota(I32, x.shape, 1)
    up = pltpu.roll(x, width - half, 1)
    down = pltpu.roll(x, half, 1)
    partner = jnp.where((lane & (HEAD_DIM - 1)) < half, up, down)
    return x * cosf + partner * sinf


def _in_projection(x, gmix_ref, win_ref, lng_ref, lnb_ref, cos, sin_signed):
    hb = _rmsnorm(x, gmix_ref[...]).astype(BF16)
    zuv = _gelu(_bdot(hb, win_ref[:, 0:O_Q]))
    u = zuv[:, :A_WIDTH]
    va = _group_layernorm(zuv[:, A_WIDTH:], lng_ref[...], lnb_ref[...])
    zqkv = _bdot(hb, win_ref[:, O_Q:O_GA])
    q = _rope(zqkv[:, :QPAD_WIDTH], cos, sin_signed)
    k = _rope(zqkv[:, QPAD_WIDTH:QPAD_WIDTH + KV_WIDTH], cos, sin_signed)
    v = zqkv[:, QPAD_WIDTH + KV_WIDTH:]
    zg = _bdot(hb, win_ref[:, O_GA:IN_COLS])
    gate_a = jax.nn.sigmoid(zg[:, :D_MODEL])
    gate_b = jax.nn.sigmoid(zg[:, D_MODEL:])
    return u, va, q, k, v, gate_a, gate_b


def _merge_and_route(x, ya_in, att, gate_a, gate_b, wpa_ref, wpb_ref, wo_ref, gffn_ref, wr3_ref, br_ref):
    ya = _bdot(ya_in.astype(BF16), wpa_ref[...])
    yb = _bdot(att.astype(BF16), wpb_ref[...])
    mix = (gate_a * ya + gate_b * yb).astype(BF16)
    x1 = x + _bdot(mix, wo_ref[...])
    h2 = _rmsnorm(x1, gffn_ref[...])
    hi = h2.astype(BF16)
    lo = (h2 - hi.astype(F32)).astype(BF16)
    logits = _bdot(jnp.concatenate([hi, hi, lo], axis=1), wr3_ref[...])
    return x1, h2, jnp.transpose(logits)[:N_EXPERTS, :] + br_ref[...]


def _top4_softmax(logits):
    eid = lax.broadcasted_iota(I32, logits.shape, 0)
    vals, idxs = [], []
    for _ in range(TOP_K):
        m = jnp.max(logits, axis=0, keepdims=True)
        idx = jnp.min(jnp.where(logits == m, eid, N_EXPERTS), axis=0, keepdims=True)
        logits = jnp.where(eid == idx, -jnp.inf, logits)
        vals.append(m)
        idxs.append(idx)
    es = [jnp.exp(v - vals[0]) for v in vals]
    inv = 1.0 / (es[0] + es[1] + es[2] + es[3])
    return jnp.concatenate(idxs, axis=0), jnp.concatenate([e * inv for e in es], axis=0)


def _store_token_major(ref, val):
    rows = val.shape[0]
    for c in range(LANE_CHUNKS):
        ref[pl.ds(c, rows, stride=LANE_CHUNKS), :] = val[:, c * LANES:(c + 1) * LANES]


def _load_token_major(ref, rows):
    return jnp.concatenate([ref[pl.ds(c, rows, stride=LANE_CHUNKS), :] for c in range(LANE_CHUNKS)], axis=1)


def _band_attention(qpad, k, v, k_prev, v_prev, sinks_ref, bias_ref, seq_start):
    kb = jnp.concatenate([k_prev, k], axis=0).astype(BF16)
    vt = jnp.transpose(jnp.concatenate([v_prev, v], axis=0)).astype(BF16)
    qb = qpad.astype(BF16)
    lane = lax.broadcasted_iota(I32, (1, GQA_GROUP * WINDOW), 1)
    blocks = []
    for b in range(BLOCKS_PER_TILE):
        bias = bias_ref[jnp.where(seq_start, 1, 0)] if b == 0 else bias_ref[0]
        keys = kb[b * WINDOW:(b + 2) * WINDOW, :]
        pieces = []
        for h in range(N_KV_HEADS):
            qh = jnp.concatenate(
                [qb[b * WINDOW:(b + 1) * WINDOW, (h * GQA_GROUP + j) * LANES:(h * GQA_GROUP + j + 1) * LANES]
                 for j in range(GQA_GROUP)], axis=0)
            st = lax.dot_general(keys, qh, (((1,), (1,)), ((), ())), preferred_element_type=F32) + bias
            sink = jnp.zeros((1, GQA_GROUP * WINDOW), F32)
            for j in range(GQA_GROUP):
                sink = jnp.where(lane // WINDOW == j, sinks_ref[h * GQA_GROUP + j], sink)
            m = jnp.maximum(jnp.max(st, axis=0, keepdims=True), sink)
            e = jnp.exp(st - m)
            inv = 1.0 / (jnp.sum(e, axis=0, keepdims=True) + jnp.exp(sink - m))
            ot = _bdot(vt[h * HEAD_DIM:(h + 1) * HEAD_DIM, b * WINDOW:(b + 2) * WINDOW], (e * inv).astype(BF16))
            pieces.extend(ot[:, j * WINDOW:(j + 1) * WINDOW] for j in range(GQA_GROUP))
        blocks.append(jnp.transpose(jnp.concatenate(pieces, axis=0)))
    return jnp.concatenate(blocks, axis=0)


def _band_bias():
    kj = lax.broadcasted_iota(I32, (2, 2 * WINDOW, GQA_GROUP * WINDOW), 1)
    qi = lax.broadcasted_iota(I32, (2, 2 * WINDOW, GQA_GROUP * WINDOW), 2) % WINDOW
    lo = lax.broadcasted_iota(I32, (2, 2 * WINDOW, GQA_GROUP * WINDOW), 0) * WINDOW
    valid = (kj > qi) & (kj <= qi + WINDOW) & (kj >= lo)
    return jnp.where(valid, 0.0, -jnp.inf).astype(F32)


def _prompt_front_kernel(x_ref, cos_ref, sin_ref, gmix_ref, win_ref, lng_ref, lnb_ref, ws_ref, bsf_ref,
                         sinks_ref, bias_ref, wpa_ref, wpb_ref, wo_ref, gffn_ref, wr3_ref, br_ref, wgu32_ref, wdn32_ref,
                         x1_ref, h2_ref, logits_ref, kwin_ref, vwin_ref, wgu16_ref, wdn16_ref,
                         kprev_ref, vprev_ref, gu_in, dn_in, gu_out, dn_out, cast_sem):
    i = pl.program_id(0)
    seq_start = (i % TILES_PER_SEQ) == 0

    def cast_rows(c, j):
        part = CAST_ROWS // CAST_SPLIT
        return pl.ds(pl.multiple_of(c * CAST_ROWS + j * part, part), part), pl.ds(j * part, part)

    def cast_in(c, slot):
        cps = []
        for j in range(CAST_SPLIT):
            hbm, loc = cast_rows(c, j)
            cps.append(pltpu.make_async_copy(wgu32_ref.at[hbm], gu_in.at[slot, loc], cast_sem.at[0, slot]))
            cps.append(pltpu.make_async_copy(wdn32_ref.at[hbm], dn_in.at[slot, loc], cast_sem.at[1, slot]))
        return cps

    def cast_out(c, slot):
        cps = []
        for j in range(CAST_SPLIT):
            hbm, loc = cast_rows(c, j)
            cps.append(pltpu.make_async_copy(gu_out.at[slot, loc], wgu16_ref.at[hbm], cast_sem.at[2, slot]))
            cps.append(pltpu.make_async_copy(dn_out.at[slot, loc], wdn16_ref.at[hbm], cast_sem.at[3, slot]))
        return cps

    slot = i & 1

    @pl.when(i == 0)
    def _():
        for cp in cast_in(0, 0):
            cp.start()

    @pl.when(i + 1 < FRONT_STEPS)
    def _():
        for cp in cast_in(i + 1, 1 - slot):
            cp.start()

    for cp in cast_in(i, slot):
        cp.wait()

    @pl.when(i >= 2)
    def _():
        for cp in cast_out(i - 2, slot):
            cp.wait()


    @pl.when(seq_start)
    def _():
        kprev_ref[...] = jnp.zeros_like(kprev_ref)
        vprev_ref[...] = jnp.zeros_like(vprev_ref)

    x = x_ref[...]
    u, va, q, k, v, gate_a, gate_b = _in_projection(
        x, gmix_ref, win_ref, lng_ref, lnb_ref, cos_ref[...], sin_ref[...])

    att = _band_attention(q, k, v, kprev_ref[...], vprev_ref[...], sinks_ref, bias_ref, seq_start)

    gu_out[slot] = gu_in[slot].astype(BF16)
    dn_out[slot] = dn_in[slot].astype(BF16)
    k_last, v_last = k[TM - WINDOW:], v[TM - WINDOW:]
    kprev_ref[...] = k_last
    vprev_ref[...] = v_last
    kwin_ref[0] = k_last
    vwin_ref[0] = v_last

    vab = va.astype(BF16)
    zc = jnp.concatenate(
        [jnp.concatenate(
            [_bdot(ws_ref[g], vab[b * CHUNK:(b + 1) * CHUNK, g * A_GROUP_DIM:(g + 1) * A_GROUP_DIM])
             for g in range(A_GROUPS)], axis=1) + bsf_ref[...]
         for b in range(BLOCKS_PER_TILE)], axis=0)

    x1, h2, logits = _merge_and_route(x, u * zc, att, gate_a, gate_b,
                                      wpa_ref, wpb_ref, wo_ref, gffn_ref, wr3_ref, br_ref)
    _store_token_major(x1_ref, x1)
    _store_token_major(h2_ref, h2)
    logits_ref[...] = logits

    for cp in cast_out(i, slot):
        cp.start()

    @pl.when(i == FRONT_STEPS - 1)
    def _():
        for cp in cast_out(i - 1, 1 - slot) + cast_out(i, slot):
            cp.wait()


def _full(shape):
    return pl.BlockSpec(shape, lambda i: (0,) * len(shape))


def _prompt_front(x, cos, sin, gmix, win, lng, lnb, ws, bsf, sinks, bias, wpa, wpb, wo, gffn, wrt, br, wgu32, wdn32):
    n = x.shape[0]
    assert n == N_PROMPT
    grid = (FRONT_STEPS,)
    anyspec = pl.BlockSpec(memory_space=pl.ANY)
    in_specs = [
        pl.BlockSpec((TM, D_MODEL), lambda i: (i, 0)),
        pl.BlockSpec((TM, LANES), lambda i: (i % TILES_PER_SEQ, 0)),
        pl.BlockSpec((TM, LANES), lambda i: (i % TILES_PER_SEQ, 0)),
        _full((1, D_MODEL)),
        _full((D_MODEL, IN_COLS)),
        _full((1, A_WIDTH)),
        _full((1, A_WIDTH)),
        _full((A_GROUPS, CHUNK, CHUNK)),
        _full((CHUNK, A_WIDTH)),
        pl.BlockSpec(memory_space=pltpu.SMEM),
        _full((2, 2 * WINDOW, GQA_GROUP * WINDOW)),
        _full((A_WIDTH, D_MODEL)),
        _full((Q_WIDTH, D_MODEL)),
        _full((D_MODEL, D_MODEL)),
        _full((1, D_MODEL)),
        _full((3 * D_MODEL, LANES)),
        _full((N_EXPERTS, 1)),
        anyspec,
        anyspec,
    ]
    out_shape = [
        jax.ShapeDtypeStruct((n * LANE_CHUNKS, LANES), F32),
        jax.ShapeDtypeStruct((n * LANE_CHUNKS, LANES), F32),
        jax.ShapeDtypeStruct((N_EXPERTS, n), F32),
        jax.ShapeDtypeStruct((n // SEQ, WINDOW, KV_WIDTH), F32),
        jax.ShapeDtypeStruct((n // SEQ, WINDOW, KV_WIDTH), F32),
        jax.ShapeDtypeStruct(wgu32.shape, BF16),
        jax.ShapeDtypeStruct(wdn32.shape, BF16),
    ]
    out_specs = [
        pl.BlockSpec((TM * LANE_CHUNKS, LANES), lambda i: (i, 0)),
        pl.BlockSpec((TM * LANE_CHUNKS, LANES), lambda i: (i, 0)),
        pl.BlockSpec((N_EXPERTS, TM), lambda i: (0, i)),
        pl.BlockSpec((1, WINDOW, KV_WIDTH), lambda i: (i // TILES_PER_SEQ, 0, 0)),
        pl.BlockSpec((1, WINDOW, KV_WIDTH), lambda i: (i // TILES_PER_SEQ, 0, 0)),
        anyspec,
        anyspec,
    ]
    scratch = [
        pltpu.VMEM((WINDOW, KV_WIDTH), F32),
        pltpu.VMEM((WINDOW, KV_WIDTH), F32),
        pltpu.VMEM((2, CAST_ROWS, 2 * D_EXPERT), F32),
        pltpu.VMEM((2, CAST_ROWS, D_MODEL), F32),
        pltpu.VMEM((2, CAST_ROWS, 2 * D_EXPERT), BF16),
        pltpu.VMEM((2, CAST_ROWS, D_MODEL), BF16),
        pltpu.SemaphoreType.DMA((4, 2)),
    ]
    return pl.pallas_call(
        _prompt_front_kernel,
        grid=grid,
        in_specs=in_specs,
        out_specs=out_specs,
        out_shape=out_shape,
        scratch_shapes=scratch,
        compiler_params=pltpu.CompilerParams(dimension_semantics=("arbitrary",), vmem_limit_bytes=VMEM_LIMIT),
        name="prompt_front",
    )(x, cos, sin, gmix, win, lng, lnb, ws, bsf, sinks, bias, wpa, wpb, wo, gffn, wrt, br, wgu32, wdn32)


SAMPLE_STEP = 16
SAMPLE_STEPS = DEC_BATCH // SAMPLE_STEP


def _sample_kernel(x_ref, cos_ref, sin_ref, gmix_ref, win_ref, lng_ref, lnb_ref, wdiag_ref, bs0_ref, sinks_ref,
                   kc_ref, vc_ref, wpa_ref, wpb_ref, wo_ref, gffn_ref, wr3_ref, br_ref,
                   x1_ref, h2_ref, logits_ref, kwin_ref, vwin_ref, va_ref,
                   q_s, k_s, v_s, yain_s, ga_s, gb_s, att_s):
    i = pl.program_id(0)

    @pl.when(i == 0)
    def _():
        x = x_ref[...]
        cos = jnp.broadcast_to(cos_ref[...], (DEC_BATCH, LANES))
        sin = jnp.broadcast_to(sin_ref[...], (DEC_BATCH, LANES))
        u, va, q, k, v, gate_a, gate_b = _in_projection(x, gmix_ref, win_ref, lng_ref, lnb_ref, cos, sin)
        va_ref[...] = va
        z = wdiag_ref[...].astype(F32) * va.astype(BF16).astype(F32) + bs0_ref[...]
        yain_s[...] = u * z
        q_s[...] = q
        k_s[...] = k
        v_s[...] = v
        ga_s[...] = gate_a
        gb_s[...] = gate_b

    r0 = pl.multiple_of(i * SAMPLE_STEP, SAMPLE_STEP)
    kwin = jnp.concatenate([kc_ref[:, 1:, :], k_s[pl.ds(r0, SAMPLE_STEP), :][:, None, :]], axis=1)
    vwin = jnp.concatenate([vc_ref[:, 1:, :], v_s[pl.ds(r0, SAMPLE_STEP), :][:, None, :]], axis=1)
    kwin_ref[...] = kwin
    vwin_ref[...] = vwin

    q16 = q_s[pl.ds(r0, SAMPLE_STEP), :]
    lane = lax.broadcasted_iota(I32, (SAMPLE_STEP, LANES), 1)
    heads = [q16[:, hq * LANES:(hq + 1) * LANES] for hq in range(N_HEADS)]
    qpad = pltpu.einshape("hbd->bhd", jnp.stack(heads, axis=0)).astype(BF16)
    s = jnp.einsum("bhd,bkd->bhk", qpad, kwin.astype(BF16), preferred_element_type=F32)
    hid = lax.broadcasted_iota(I32, (1, N_HEADS, 1), 1)
    sink = jnp.zeros((1, N_HEADS, 1), F32)
    for hq in range(N_HEADS):
        sink = jnp.where(hid == hq, sinks_ref[hq], sink)
    m = jnp.maximum(jnp.max(s, axis=-1, keepdims=True), sink)
    e = jnp.exp(s - m)
    inv = 1.0 / (jnp.sum(e, axis=-1, keepdims=True) + jnp.exp(sink - m))
    o = jnp.einsum("bhk,bkd->bhd", (e * inv).astype(BF16), vwin.astype(BF16), preferred_element_type=F32)
    o = pltpu.einshape("bhd->hbd", o)
    chunks = []
    for c in range(N_HEADS // 2):
        parts = []
        for p in range(2):
            hq = 2 * c + p
            oh = o[hq]
            if p != hq // GQA_GROUP:
                oh = pltpu.roll(oh, HEAD_DIM, 1)
            parts.append(oh)
        chunks.append(jnp.where(lane < HEAD_DIM, parts[0], parts[1]))
    att_s[pl.ds(r0, SAMPLE_STEP), :] = jnp.concatenate(chunks, axis=1)

    @pl.when(i == SAMPLE_STEPS - 1)
    def _():
        x1, h2, logits = _merge_and_route(
            x_ref[...], yain_s[...], att_s[...], ga_s[...], gb_s[...],
            wpa_ref, wpb_ref, wo_ref, gffn_ref, wr3_ref, br_ref)
        _store_token_major(x1_ref, x1)
        _store_token_major(h2_ref, h2)
        logits_ref[...] = logits


def _sample_front(x, cos, sin, gmix, win, lng, lnb, wdiag, bs0, sinks, kc, vc, wpa, wpb, wo, gffn, wrt, br):
    n = DEC_BATCH
    cache_spec = pl.BlockSpec((SAMPLE_STEP, WINDOW, KV_WIDTH), lambda i: (i, 0, 0))
    in_specs = [
        _full((n, D_MODEL)),
        _full((1, LANES)),
        _full((1, LANES)),
        _full((1, D_MODEL)),
        _full((D_MODEL, IN_COLS)),
        _full((1, A_WIDTH)),
        _full((1, A_WIDTH)),
        _full((1, A_WIDTH)),
        _full((1, A_WIDTH)),
        pl.BlockSpec(memory_space=pltpu.SMEM),
        cache_spec,
        cache_spec,
        _full((A_WIDTH, D_MODEL)),
        _full((Q_WIDTH, D_MODEL)),
        _full((D_MODEL, D_MODEL)),
        _full((1, D_MODEL)),
        _full((3 * D_MODEL, LANES)),
        _full((N_EXPERTS, 1)),
    ]
    out_shape = [
        jax.ShapeDtypeStruct((n * LANE_CHUNKS, LANES), F32),
        jax.ShapeDtypeStruct((n * LANE_CHUNKS, LANES), F32),
        jax.ShapeDtypeStruct((N_EXPERTS, n), F32),
        jax.ShapeDtypeStruct((n, WINDOW, KV_WIDTH), F32),
        jax.ShapeDtypeStruct((n, WINDOW, KV_WIDTH), F32),
        jax.ShapeDtypeStruct((n, A_WIDTH), F32),
    ]
    out_specs = [
        _full((n * LANE_CHUNKS, LANES)),
        _full((n * LANE_CHUNKS, LANES)),
        _full((N_EXPERTS, n)),
        cache_spec,
        cache_spec,
        _full((n, A_WIDTH)),
    ]
    scratch = [
        pltpu.VMEM((n, QPAD_WIDTH), F32), pltpu.VMEM((n, KV_WIDTH), F32), pltpu.VMEM((n, KV_WIDTH), F32),
        pltpu.VMEM((n, A_WIDTH), F32), pltpu.VMEM((n, D_MODEL), F32), pltpu.VMEM((n, D_MODEL), F32),
        pltpu.VMEM((n, Q_WIDTH), F32),
    ]
    return pl.pallas_call(
        _sample_kernel,
        grid=(SAMPLE_STEPS,),
        in_specs=in_specs,
        out_specs=out_specs,
        out_shape=out_shape,
        scratch_shapes=scratch,
        compiler_params=pltpu.CompilerParams(dimension_semantics=("arbitrary",), vmem_limit_bytes=VMEM_LIMIT),
        name="sample_front",
    )(x, cos, sin, gmix, win, lng, lnb, wdiag, bs0, sinks, kc, vc, wpa, wpb, wo, gffn, wrt, br)


def _route_plan_kernel(lp_ref, ls_ref, dest_ref, wts_ref, off_ref):
    g = pl.program_id(0)
    topi, topw = _top4_softmax(jnp.concatenate([lp_ref[...], ls_ref[...]], axis=1))
    slot = lax.broadcasted_iota(I32, (TOP_K, GROUP_SLOTS), 1)
    eall = jnp.where(jnp.logical_or(slot < GROUP_PROMPT, g == N_GROUPS - 1), topi, N_EXPERTS)
    wts_ref[:, 0:GROUP_SLOTS] = topw
    wts_ref[:, GROUP_SLOTS:] = jnp.zeros((TOP_K, K_STRIDE - GROUP_SLOTS), F32)
    dest_ref[:, GROUP_SLOTS:] = jnp.zeros((TOP_K, K_STRIDE - GROUP_SLOTS), I32)
    eid = lax.broadcasted_iota(I32, (N_EXPERTS, GROUP_SLOTS), 0)
    onehots = [eall[k:k + 1, :] == eid for k in range(TOP_K)]
    count = jnp.zeros((N_EXPERTS, GROUP_SLOTS), F32)
    for oh in onehots:
        count = count + oh.astype(F32)
    total = jnp.broadcast_to(jnp.sum(count, axis=1, keepdims=True), (N_EXPERTS, LANES))
    padded = total + (MOE_ROWS - 1)
    nblk = jnp.floor(padded * (1.0 / MOE_ROWS))
    rem = padded - nblk * MOE_ROWS
    nblk = jnp.where(rem >= MOE_ROWS, nblk + 1.0, jnp.where(rem < 0.0, nblk - 1.0, nblk))
    r = lax.broadcasted_iota(I32, (N_EXPERTS, N_EXPERTS), 0)
    c = lax.broadcasted_iota(I32, (N_EXPERTS, N_EXPERTS), 1)
    first_blk = lax.dot_general((c < r).astype(F32), nblk, (((1,), (0,)), ((), ())),
                                precision=lax.Precision.HIGHEST, preferred_element_type=F32)
    start = (first_blk + LEAD_BLOCKS) * MOE_ROWS
    lane = lax.broadcasted_iota(I32, (N_EXPERTS, LANES), 1)
    info = jnp.where(lane == 0, start, jnp.where(lane == 1, start + total, jnp.where(lane == 2, nblk, first_blk)))
    off_ref[...] = info.astype(I32)
    ti = lax.broadcasted_iota(I32, (LANES, LANES), 0)
    tj = lax.broadcasted_iota(I32, (LANES, LANES), 1)
    before = (ti < tj).astype(BF16)
    ones = jnp.ones((LANES, LANES), BF16)
    running = start
    for t in range(SLOT_TILES):
        sl = slice(t * LANES, (t + 1) * LANES)
        cb = count[:, sl].astype(BF16)
        pos = running + _bdot(cb, before)
        rows = [jnp.sum(jnp.where(oh[:, sl], pos, 0.0), axis=0, keepdims=True) for oh in onehots]
        dest_ref[:, sl] = jnp.concatenate(rows, axis=0).astype(I32)
        running = running + _bdot(cb, ones)


def _route_plan(logits_p, logits_s):
    in_specs = [
        pl.BlockSpec((N_EXPERTS, GROUP_PROMPT), lambda g: (0, g)),
        pl.BlockSpec((N_EXPERTS, DEC_BATCH), lambda g: (0, 0)),
    ]
    out_shape = [
        jax.ShapeDtypeStruct((N_GROUPS, TOP_K, K_STRIDE), I32),
        jax.ShapeDtypeStruct((N_GROUPS, TOP_K, K_STRIDE), F32),
        jax.ShapeDtypeStruct((N_GROUPS, N_EXPERTS, LANES), I32),
    ]
    out_specs = [
        pl.BlockSpec((None, TOP_K, K_STRIDE), lambda g: (g, 0, 0)),
        pl.BlockSpec((None, TOP_K, K_STRIDE), lambda g: (g, 0, 0)),
        pl.BlockSpec((None, N_EXPERTS, LANES), lambda g: (g, 0, 0)),
    ]
    return pl.pallas_call(
        _route_plan_kernel,
        grid=(N_GROUPS,),
        in_specs=in_specs,
        out_specs=out_specs,
        out_shape=out_shape,
        compiler_params=pltpu.CompilerParams(dimension_semantics=("arbitrary",)),
        name="route_plan",
    )(logits_p, logits_s)


GROUP_ROWS = GROUP_PROMPT * LANE_CHUNKS
SAMPLE_ROWS = DEC_BATCH * LANE_CHUNKS
TRASH_SLOT = GROUP_SLOTS
BUF_ROWS = (GROUP_SLOTS + 1) * LANE_CHUNKS
SCATTER_BATCH = 8
DMA_SPLIT = 8
FFN_COLS = 256
FFN_DOTS = (2 * D_EXPERT + D_MODEL) // FFN_COLS
GATHER_DOTS = 9
assert D_EXPERT == D_MODEL and GATHER_DOTS < FFN_DOTS


def _moe_kernel(off_ref, desth_ref, wtsh_ref, h2p_ref, h2s_ref, x1p_ref, x1s_ref, wgu_ref, bgu_ref, wdn_ref, bdn_ref,
                x2p_ref, x2s_ref,
                h2buf, acc, wgubuf, bgubuf, wdnbuf, bdnbuf, xs, xb0, xb1, ab0, ab1, ys0, ys1,
                dest_ref, wts_ref, src_ref, seg_expert, seg_first, blk_seg, act_sem, w_sem):
    g = pl.program_id(0)
    last = g == N_GROUPS - 1
    row0 = pl.multiple_of(g * GROUP_ROWS, GROUP_ROWS)

    def prompt_copies():
        cps = []
        for j in range(DMA_SPLIT):
            src = pl.ds(row0 + j * (GROUP_ROWS // DMA_SPLIT), GROUP_ROWS // DMA_SPLIT)
            dst = pl.ds(j * (GROUP_ROWS // DMA_SPLIT), GROUP_ROWS // DMA_SPLIT)
            cps.append(pltpu.make_async_copy(h2p_ref.at[src], h2buf.at[dst], act_sem.at[0]))
            cps.append(pltpu.make_async_copy(x1p_ref.at[src], acc.at[dst], act_sem.at[1]))
        return cps

    def sample_copies():
        return (pltpu.make_async_copy(h2s_ref, h2buf.at[pl.ds(GROUP_ROWS, SAMPLE_ROWS)], act_sem.at[2]),
                pltpu.make_async_copy(x1s_ref, acc.at[pl.ds(GROUP_ROWS, SAMPLE_ROWS)], act_sem.at[3]))

    def gate_up_copies(e, slot):
        cps = [pltpu.make_async_copy(bgu_ref.at[e], bgubuf.at[slot], w_sem.at[1, slot])]
        for j in range(DMA_SPLIT):
            rg = pl.ds(j * (D_MODEL // DMA_SPLIT), D_MODEL // DMA_SPLIT)
            cps.append(pltpu.make_async_copy(wgu_ref.at[e, rg], wgubuf.at[slot, rg], w_sem.at[0, slot]))
        return cps

    def down_copies(e, slot):
        cps = [pltpu.make_async_copy(bdn_ref.at[e], bdnbuf.at[slot], w_sem.at[3, slot])]
        for j in range(DMA_SPLIT):
            rd = pl.ds(j * (D_EXPERT // DMA_SPLIT), D_EXPERT // DMA_SPLIT)
            cps.append(pltpu.make_async_copy(wdn_ref.at[e, rd], wdnbuf.at[slot, rd], w_sem.at[2, slot]))
        return cps

    def output_copies():
        return [pltpu.make_async_copy(
            acc.at[pl.ds(j * (GROUP_ROWS // DMA_SPLIT), GROUP_ROWS // DMA_SPLIT)],
            x2p_ref.at[pl.ds(row0 + j * (GROUP_ROWS // DMA_SPLIT), GROUP_ROWS // DMA_SPLIT)], act_sem.at[0])
            for j in range(DMA_SPLIT)]

    tab0 = pl.multiple_of(g * (TOP_K * K_STRIDE), TOP_K * K_STRIDE)
    table_copies = (
        pltpu.make_async_copy(desth_ref.at[pl.ds(tab0, TOP_K * K_STRIDE)], dest_ref, act_sem.at[4]),
        pltpu.make_async_copy(wtsh_ref.at[pl.ds(tab0, TOP_K * K_STRIDE)], wts_ref, act_sem.at[5]))
    for cp in table_copies:
        cp.start()

    for cp in prompt_copies():
        cp.start()

    @pl.when(last)
    def _():
        for cp in sample_copies():
            cp.start()

    trash = pl.ds(TRASH_SLOT * LANE_CHUNKS, LANE_CHUNKS)
    h2buf[trash, :] = jnp.zeros((LANE_CHUNKS, LANES), F32)
    acc[trash, :] = jnp.zeros((LANE_CHUNKS, LANES), F32)
    ys0[...] = jnp.zeros_like(ys0)
    ys1[...] = jnp.zeros_like(ys1)
    ab1[...] = jnp.zeros_like(ab1)

    def pad_block(pos0):
        def body(j, carry):
            for d in range(SUBLANES):
                src_ref[pos0 + j * SUBLANES + d] = TRASH_SLOT
            return carry
        lax.fori_loop(0, MOE_ROWS // SUBLANES, body, 0)

    def scan_expert(e, carry):
        nseg, nblocks = carry
        nblk = off_ref[e, 2]
        first = off_ref[e, 3]

        @pl.when(nblk > 0)
        def _():
            seg_expert[nseg] = e
            seg_first[nseg] = first
            pad_block(off_ref[e, 0] + (nblk - 1) * MOE_ROWS)

            def mark(b, c2):
                blk_seg[first + b] = nseg
                return c2
            lax.fori_loop(0, nblk, mark, 0)

        return nseg + jnp.where(nblk > 0, 1, 0), nblocks + nblk

    nseg, nblocks = lax.fori_loop(0, N_EXPERTS, scan_expert, (jnp.int32(0), jnp.int32(0)))
    for j in range(LEAD_BLOCKS):
        pad_block(j * MOE_ROWS)
    for j in range(TAIL_BLOCKS):
        pad_block((nblocks + LEAD_BLOCKS + j) * MOE_ROWS)
        blk_seg[nblocks + j] = nseg - 1

    for cp in gate_up_copies(seg_expert[0], 0) + down_copies(seg_expert[0], 0):
        cp.start()

    for cp in table_copies:
        cp.wait()

    nvalid = jnp.where(last, GROUP_SLOTS, GROUP_PROMPT)
    for k in range(TOP_K):
        def fill(j, carry, k=k):
            c0 = k * K_STRIDE + j * SUBLANES
            for d in range(SUBLANES):
                src_ref[dest_ref[c0 + d]] = c0 + d
            return carry
        lax.fori_loop(0, nvalid // SUBLANES, fill, 0)

    for cp in prompt_copies():
        cp.wait()

    @pl.when(last)
    def _():
        for cp in sample_copies():
            cp.wait()

    def token_rows(code):
        slot_id = code & (K_STRIDE - 1)
        return pl.ds(pl.multiple_of(slot_id * LANE_CHUNKS, LANE_CHUNKS), LANE_CHUNKS)

    def gather(b, lo=0, hi=MOE_ROWS):
        base = (b + LEAD_BLOCKS) * MOE_ROWS
        for m in range(lo, hi):
            xs[pl.ds(m, LANE_CHUNKS, stride=XS_STRIDE), :] = h2buf[token_rows(src_ref[base + m]), :]

    def pack(xb, lo=0, hi=LANE_CHUNKS):
        for c in range(lo, hi):
            xb[:, c * LANES:(c + 1) * LANES] = xs[c * XS_STRIDE:c * XS_STRIDE + MOE_ROWS, :].astype(BF16)

    def scatter_add(b, ys, lo=0, hi=MOE_ROWS):
        base = (b + LEAD_BLOCKS) * MOE_ROWS
        for m0 in range(lo, hi, SCATTER_BATCH):
            pending = []
            for m in range(m0, m0 + SCATTER_BATCH):
                code = src_ref[base + m]
                rows = token_rows(code)
                pending.append((rows, acc[rows, :] + wts_ref[code] * ys[pl.ds(m, LANE_CHUNKS, stride=XS_STRIDE), :]))
            for rows, val in pending:
                acc[rows, :] = val

    gather_cuts = [round(i * MOE_ROWS / GATHER_DOTS) for i in range(GATHER_DOTS + 1)]
    pack_cuts = [round(i * LANE_CHUNKS / (FFN_DOTS - GATHER_DOTS)) for i in range(FFN_DOTS - GATHER_DOTS + 1)]
    scatter_cuts = [SCATTER_BATCH * round(i * (MOE_ROWS // SCATTER_BATCH) / FFN_DOTS) for i in range(FFN_DOTS + 1)]

    def step(t, xb_cur, xb_next, ab_cur, ab_prev, ys_down, ys_scatter):
        seg_g = blk_seg[t]
        slot_g = seg_g & 1
        tp = jnp.maximum(t - 1, 0)
        seg_d = blk_seg[tp]
        slot_d = seg_d & 1

        @pl.when(jnp.logical_and(t == seg_first[seg_g], t < nblocks))
        def _():
            for cp in gate_up_copies(seg_expert[seg_g], slot_g):
                cp.wait()

            @pl.when(seg_g + 1 < nseg)
            def _():
                for cp in gate_up_copies(seg_expert[seg_g + 1], 1 - slot_g):
                    cp.start()

        new_down = jnp.logical_and(jnp.logical_and(t >= 2, tp == seg_first[seg_d]), tp < nblocks)

        @pl.when(jnp.logical_or(t == 0, new_down))
        def _():
            for cp in down_copies(seg_expert[seg_d], slot_d):
                cp.wait()

            @pl.when(seg_d + 1 < nseg)
            def _():
                for cp in down_copies(seg_expert[seg_d + 1], 1 - slot_d):
                    cp.start()

        done = [0]

        def row_traffic():
            i = done[0]
            if i < GATHER_DOTS:
                gather(t + 1, gather_cuts[i], gather_cuts[i + 1])
            else:
                pack(xb_next, pack_cuts[i - GATHER_DOTS], pack_cuts[i - GATHER_DOTS + 1])
            scatter_add(t - 2, ys_scatter, scatter_cuts[i], scatter_cuts[i + 1])
            done[0] = i + 1

        x = xb_cur[...]
        a_prev = ab_prev[...]
        for c in range(D_EXPERT // FFN_COLS):
            gc = pl.ds(c * FFN_COLS, FFN_COLS)
            uc = pl.ds(D_EXPERT + c * FFN_COLS, FFN_COLS)
            g = _bdot(x, wgubuf[slot_g, :, gc]) + bgubuf[slot_g, :, gc]
            row_traffic()
            u = _bdot(x, wgubuf[slot_g, :, uc]) + bgubuf[slot_g, :, uc]
            row_traffic()
            gl = jnp.minimum(g, SWIGLU_LIMIT)
            ul = jnp.clip(u, -SWIGLU_LIMIT, SWIGLU_LIMIT)
            ab_cur[:, c * FFN_COLS:(c + 1) * FFN_COLS] = (
                (ul + 1.0) * (gl * jax.nn.sigmoid(SWIGLU_ALPHA * gl))).astype(BF16)
            y = _bdot(a_prev, wdnbuf[slot_d, :, gc]) + bdnbuf[slot_d, :, gc]
            for j in range(FFN_COLS // LANES):
                lc = c * (FFN_COLS // LANES) + j
                ys_down[lc * XS_STRIDE:lc * XS_STRIDE + MOE_ROWS, :] = y[:, j * LANES:(j + 1) * LANES]
            row_traffic()

    gather(0)
    pack(xb0)
    npairs = (nblocks + 3) // 2

    def pair(k, carry):
        step(2 * k, xb0, xb1, ab0, ab1, ys1, ys0)
        step(2 * k + 1, xb1, xb0, ab1, ab0, ys0, ys1)
        return carry

    lax.fori_loop(0, npairs, pair, 0)

    for cp in output_copies():
        cp.start()

    @pl.when(last)
    def _():
        out_s = pltpu.make_async_copy(acc.at[pl.ds(GROUP_ROWS, SAMPLE_ROWS)], x2s_ref, act_sem.at[2])
        out_s.start()
        out_s.wait()

    for cp in output_copies():
        cp.wait()


def _moe(dest, wts, off, h2p, h2s, x1p, x1s, wgu, bgu, wdn, bdn):
    anyspec = pl.BlockSpec(memory_space=pl.ANY)
    dest = dest.reshape(N_GROUPS * TOP_K * K_STRIDE)
    wts = wts.reshape(N_GROUPS * TOP_K * K_STRIDE)
    in_specs = [
        pl.BlockSpec((None, N_EXPERTS, LANES), lambda g: (g, 0, 0), memory_space=pltpu.SMEM),
        anyspec, anyspec, anyspec, anyspec, anyspec, anyspec, anyspec, anyspec, anyspec, anyspec,
    ]
    scratch = [
        pltpu.VMEM((BUF_ROWS, LANES), F32),
        pltpu.VMEM((BUF_ROWS, LANES), F32),
        pltpu.VMEM((2, D_MODEL, 2 * D_EXPERT), BF16),
        pltpu.VMEM((2, 1, 2 * D_EXPERT), F32),
        pltpu.VMEM((2, D_EXPERT, D_MODEL), BF16),
        pltpu.VMEM((2, 1, D_MODEL), F32),
        pltpu.VMEM((LANE_CHUNKS * XS_STRIDE, LANES), F32),
        pltpu.VMEM((MOE_ROWS, D_MODEL), BF16),
        pltpu.VMEM((MOE_ROWS, D_MODEL), BF16),
        pltpu.VMEM((MOE_ROWS, D_EXPERT), BF16),
        pltpu.VMEM((MOE_ROWS, D_EXPERT), BF16),
        pltpu.VMEM((LANE_CHUNKS * XS_STRIDE, LANES), F32),
        pltpu.VMEM((LANE_CHUNKS * XS_STRIDE, LANES), F32),
        pltpu.SMEM((TOP_K * K_STRIDE,), I32),
        pltpu.SMEM((TOP_K * K_STRIDE,), F32),
        pltpu.SMEM((POS_TABLE,), I32),
        pltpu.SMEM((N_EXPERTS,), I32),
        pltpu.SMEM((N_EXPERTS,), I32),
        pltpu.SMEM((LANES,), I32),
        pltpu.SemaphoreType.DMA((6,)),
        pltpu.SemaphoreType.DMA((4, 2)),
    ]
    return pl.pallas_call(
        _moe_kernel,
        grid=(N_GROUPS,),
        in_specs=in_specs,
        out_specs=[anyspec, anyspec],
        out_shape=[jax.ShapeDtypeStruct(x1p.shape, F32), jax.ShapeDtypeStruct(x1s.shape, F32)],
        scratch_shapes=scratch,
        compiler_params=pltpu.CompilerParams(dimension_semantics=("arbitrary",), vmem_limit_bytes=VMEM_LIMIT),
        name="moe",
    )(off, dest, wts, h2p, h2s, x1p, x1s, wgu, bgu, wdn, bdn)


def _ple_final_kernel(x2_ref, ple_ref, wple_ref, gple_ref, wpg_ref, gfin_ref, y_ref):
    rows = y_ref.shape[0]
    x2 = _load_token_major(x2_ref, rows)
    e = _rmsnorm(_bdot(ple_ref[...].astype(BF16), wple_ref[...]), gple_ref[...])
    x3 = x2 + jax.nn.sigmoid(_bdot(x2.astype(BF16), wpg_ref[...])) * e
    y_ref[...] = _rmsnorm(x3, gfin_ref[...])


def _ple_final(x2_tm, ple, wple, gple, wpg, gfin, tile):
    n = ple.shape[0]
    return pl.pallas_call(
        _ple_final_kernel,
        grid=(n // tile,),
        in_specs=[
            pl.BlockSpec((tile * LANE_CHUNKS, LANES), lambda i: (i, 0)),
            pl.BlockSpec((tile, PLE_DIM), lambda i: (i, 0)),
            _full((PLE_DIM, D_MODEL)),
            _full((1, D_MODEL)),
            _full((D_MODEL, D_MODEL)),
            _full((1, D_MODEL)),
        ],
        out_specs=pl.BlockSpec((tile, D_MODEL), lambda i: (i, 0)),
        out_shape=jax.ShapeDtypeStruct((n, D_MODEL), F32),
        compiler_params=pltpu.CompilerParams(dimension_semantics=("arbitrary",), vmem_limit_bytes=VMEM_LIMIT),
        name="ple_final",
    )(x2_tm, ple, wple, gple, wpg, gfin)


def _rope_tables(pos):
    half = HEAD_DIM // 2
    inv = ROPE_THETA ** (-jnp.arange(half, dtype=F32) / half)
    ang = pos.astype(F32)[:, None] * inv[None, :]
    cos, sin = jnp.cos(ang), jnp.sin(ang)
    cos2 = jnp.concatenate([cos, cos, cos, cos], axis=1)
    sin2 = jnp.concatenate([-sin, sin, -sin, sin], axis=1)
    return cos2, sin2


def _layout_w_in(w_in):
    o_q = 2 * A_WIDTH
    wq = w_in[:, o_q:o_q + Q_WIDTH].reshape(D_MODEL, N_HEADS, HEAD_DIM) * (HEAD_DIM ** -0.5)
    kv_head = (jnp.arange(N_HEADS) // GQA_GROUP)[None, :, None]
    wq_pad = jnp.concatenate([jnp.where(kv_head == h, wq, 0.0) for h in range(N_KV_HEADS)], axis=-1)
    return jnp.concatenate([w_in[:, :o_q], wq_pad.reshape(D_MODEL, QPAD_WIDTH), w_in[:, o_q + Q_WIDTH:]], axis=1)


def _router_passes(w_router):
    hi = w_router.astype(BF16)
    lo = (w_router - hi.astype(F32)).astype(BF16)
    w3 = jnp.concatenate([hi, lo, hi], axis=0)
    return jnp.pad(w3, ((0, 0), (0, LANES - N_EXPERTS)))


def _prep_weights(g_mix, w_in, a_ln_g, a_ln_b, a_ws, a_bs, w_pa, w_pb, w_o, g_ffn, w_router, b_router):
    causal = jnp.tril(jnp.ones((CHUNK, CHUNK), dtype=bool))
    return dict(
        gmix=g_mix.reshape(1, D_MODEL),
        win=_layout_w_in(w_in).astype(BF16),
        lng=a_ln_g.reshape(1, A_WIDTH),
        lnb=a_ln_b.reshape(1, A_WIDTH),
        ws=jnp.where(causal[None], a_ws, 0.0).astype(BF16),
        bsf=jnp.repeat(jnp.transpose(a_bs), A_GROUP_DIM, axis=1),
        wpa=w_pa.astype(BF16),
        wpb=w_pb.astype(BF16),
        wo=w_o.astype(BF16),
        gffn=g_ffn.reshape(1, D_MODEL),
        wrt=_router_passes(w_router),
        br=b_router.reshape(N_EXPERTS, 1),
    )


def kernel(x_prompt, x_sample, cache_win_k, cache_win_v, p_prompt, p_sample, g_mix, w_in, a_ln_g, a_ln_b, a_ws, a_bs, sinks, w_pa, w_pb, w_o, g_ffn, w_router, b_router, w_gu, b_gu, w_down, b_down, w_ple, g_ple, w_ple_gate, g_final):
    W = _prep_weights(g_mix[0], w_in[0], a_ln_g[0], a_ln_b[0], a_ws[0], a_bs[0], w_pa[0], w_pb[0], w_o[0],
                      g_ffn[0], w_router[0], b_router[0])
    cos_p, sin_p = _rope_tables(jnp.arange(SEQ, dtype=I32))
    cos_s, sin_s = _rope_tables(jnp.full((1,), PAST_LEN, I32))
    x1p, h2p, logits_p, kwin_p, vwin_p, wgu16, wdn16 = _prompt_front(
        x_prompt.reshape(N_PROMPT, D_MODEL), cos_p, sin_p, W["gmix"], W["win"], W["lng"], W["lnb"],
        W["ws"], W["bsf"], sinks[0], _band_bias(), W["wpa"], W["wpb"], W["wo"], W["gffn"], W["wrt"], W["br"],
        w_gu[0].reshape(N_EXPERTS * D_MODEL, 2 * D_EXPERT), w_down[0].reshape(N_EXPERTS * D_EXPERT, D_MODEL))

    wdiag = jnp.repeat(a_ws[0, :, 0, 0], A_GROUP_DIM)[None, :].astype(BF16)
    bs0 = jnp.repeat(a_bs[0, :, 0], A_GROUP_DIM)[None, :]
    x1s, h2s, logits_s, kwin_s, vwin_s, va_s = _sample_front(
        x_sample.reshape(DEC_BATCH, D_MODEL), cos_s, sin_s, W["gmix"], W["win"], W["lng"], W["lnb"], wdiag, bs0,
        sinks[0], cache_win_k[0].reshape(DEC_BATCH, WINDOW, KV_WIDTH), cache_win_v[0].reshape(DEC_BATCH, WINDOW, KV_WIDTH),
        W["wpa"], W["wpb"], W["wo"], W["gffn"], W["wrt"], W["br"])

    dest, wts, off = _route_plan(logits_p, logits_s)
    x2p, x2s = _moe(dest, wts, off, h2p, h2s, x1p, x1s,
                    wgu16.reshape(N_EXPERTS, D_MODEL, 2 * D_EXPERT), b_gu[0].reshape(N_EXPERTS, 1, 2 * D_EXPERT),
                    wdn16.reshape(N_EXPERTS, D_EXPERT, D_MODEL), b_down[0].reshape(N_EXPERTS, 1, D_MODEL))

    wple = w_ple[0].astype(BF16)
    gple = g_ple[0].reshape(1, D_MODEL)
    wpg = w_ple_gate[0].astype(BF16)
    gfin = g_final.reshape(1, D_MODEL)
    y_p = _ple_final(x2p, p_prompt[0].reshape(N_PROMPT, PLE_DIM), wple, gple, wpg, gfin, 2 * TM)
    y_s = _ple_final(x2s, p_sample[0].reshape(DEC_BATCH, PLE_DIM), wple, gple, wpg, gfin, DEC_BATCH)

    return (
        y_p.reshape(BATCH, SEQ, D_MODEL),
        y_s.reshape(DEC_BATCH, 1, D_MODEL),
        kwin_p.reshape(1, BATCH, WINDOW, N_KV_HEADS, HEAD_DIM),
        vwin_p.reshape(1, BATCH, WINDOW, N_KV_HEADS, HEAD_DIM),
        kwin_s.reshape(1, DEC_BATCH, WINDOW, N_KV_HEADS, HEAD_DIM),
        vwin_s.reshape(1, DEC_BATCH, WINDOW, N_KV_HEADS, HEAD_DIM),
        va_s.reshape(1, DEC_BATCH, 1, A_WIDTH),
    )
```

```python
import functools

import jax
import jax.numpy as jnp
from jax import lax
from jax.experimental import pallas as pl
from jax.experimental.pallas import tpu as pltpu

F32 = jnp.float32
BF16 = jnp.bfloat16
I32 = jnp.int32

D_MODEL = 1024
BATCH = 4
SEQ = 4096
DEC_BATCH = 128
PAST_LEN = 8192
CHUNK = 128
A_GROUPS = 4
A_GROUP_DIM = 128
A_WIDTH = A_GROUPS * A_GROUP_DIM
N_HEADS = 8
N_KV_HEADS = 2
HEAD_DIM = 64
Q_WIDTH = N_HEADS * HEAD_DIM
KV_WIDTH = N_KV_HEADS * HEAD_DIM
GQA_GROUP = N_HEADS // N_KV_HEADS
WINDOW = 128
ROPE_THETA = 10000.0
N_EXPERTS = 32
TOP_K = 4
D_EXPERT = D_MODEL
SWIGLU_ALPHA = 1.702
SWIGLU_LIMIT = 7.0
PLE_DIM = 256
RMS_EPS = 1e-5
LN_EPS = 1e-5

LANES = 128

QPAD_WIDTH = N_HEADS * LANES
O_Q = 2 * A_WIDTH
O_K = O_Q + QPAD_WIDTH
O_V = O_K + KV_WIDTH
O_GA = O_V + KV_WIDTH
O_GB = O_GA + D_MODEL
IN_COLS = O_GB + D_MODEL
SUBLANES = 8
LANE_CHUNKS = D_MODEL // LANES
VMEM_LIMIT = 56 * 1024 * 1024

N_PROMPT = BATCH * SEQ
TM = 256
TILES_PER_SEQ = SEQ // TM
BLOCKS_PER_TILE = TM // WINDOW
FRONT_STEPS = N_PROMPT // TM
CAST_ROWS = N_EXPERTS * D_MODEL // FRONT_STEPS
CAST_SPLIT = 4

N_GROUPS = 4
GROUP_PROMPT = N_PROMPT // N_GROUPS
GROUP_SLOTS = GROUP_PROMPT + DEC_BATCH
GROUP_ASSIGN = GROUP_SLOTS * TOP_K
SLOT_TILES = GROUP_SLOTS // LANES
MOE_ROWS = 256
XS_STRIDE = MOE_ROWS + SUBLANES
SLOT_BITS = 13
K_STRIDE = 1 << SLOT_BITS
assert GROUP_SLOTS < K_STRIDE
MAX_BLOCKS = GROUP_ASSIGN // MOE_ROWS + N_EXPERTS
POS_TABLE = 1 << 15
LEAD_BLOCKS = 2
TAIL_BLOCKS = 4
assert (MAX_BLOCKS + LEAD_BLOCKS + TAIL_BLOCKS) * MOE_ROWS <= POS_TABLE
assert MAX_BLOCKS + TAIL_BLOCKS <= LANES


def _bdot(a, b):
    return jnp.dot(a, b, preferred_element_type=F32)


def _rmsnorm(x, g):
    return x * lax.rsqrt(jnp.mean(x * x, axis=-1, keepdims=True) + RMS_EPS) * g


def _gelu(x):
    return 0.5 * x * (1.0 + lax.erf(x * (0.5 ** 0.5)))


def _group_layernorm(v, g, b):
    cols = []
    for gi in range(A_GROUPS):
        s = slice(gi * A_GROUP_DIM, (gi + 1) * A_GROUP_DIM)
        vg = v[:, s]
        mu = jnp.mean(vg, axis=-1, keepdims=True)
        d = vg - mu
        var = jnp.mean(d * d, axis=-1, keepdims=True)
        cols.append(d * lax.rsqrt(var + LN_EPS) * g[:, s] + b[:, s])
    return jnp.concatenate(cols, axis=1)


def _rope(x, cos, sin_signed):
    width = x.shape[1]
    reps = width // LANES
    cosf = jnp.concatenate([cos] * reps, axis=1) if reps > 1 else cos
    sinf = jnp.concatenate([sin_signed] * reps, axis=1) if reps > 1 else sin_signed
    half = HEAD_DIM // 2
    lane = lax.broadcasted_iota(I32, x.shape, 1)
    up = pltpu.roll(x, width - half, 1)
    down = pltpu.roll(x, half, 1)
    partner = jnp.where((lane & (HEAD_DIM - 1)) < half, up, down)
    return x * cosf + partner * sinf


def _in_projection(x, gmix_ref, win_ref, lng_ref, lnb_ref, cos, sin_signed):
    hb = _rmsnorm(x, gmix_ref[...]).astype(BF16)
    zuv = _gelu(_bdot(hb, win_ref[:, 0:O_Q]))
    u = zuv[:, :A_WIDTH]
    va = _group_layernorm(zuv[:, A_WIDTH:], lng_ref[...], lnb_ref[...])
    zqkv = _bdot(hb, win_ref[:, O_Q:O_GA])
    q = _rope(zqkv[:, :QPAD_WIDTH], cos, sin_signed)
    k = _rope(zqkv[:, QPAD_WIDTH:QPAD_WIDTH + KV_WIDTH], cos, sin_signed)
    v = zqkv[:, QPAD_WIDTH + KV_WIDTH:]
    zg = _bdot(hb, win_ref[:, O_GA:IN_COLS])
    gate_a = jax.nn.sigmoid(zg[:, :D_MODEL])
    gate_b = jax.nn.sigmoid(zg[:, D_MODEL:])
    return u, va, q, k, v, gate_a, gate_b


def _merge_and_route(x, ya_in, att, gate_a, gate_b, wpa_ref, wpb_ref, wo_ref, gffn_ref, wr3_ref, br_ref):
    ya = _bdot(ya_in.astype(BF16), wpa_ref[...])
    yb = _bdot(att.astype(BF16), wpb_ref[...])
    mix = (gate_a * ya + gate_b * yb).astype(BF16)
    x1 = x + _bdot(mix, wo_ref[...])
    h2 = _rmsnorm(x1, gffn_ref[...])
    hi = h2.astype(BF16)
    lo = (h2 - hi.astype(F32)).astype(BF16)
    logits = _bdot(jnp.concatenate([hi, hi, lo], axis=1), wr3_ref[...])
    return x1, h2, jnp.transpose(logits)[:N_EXPERTS, :] + br_ref[...]


def _top4_softmax(logits):
    eid = lax.broadcasted_iota(I32, logits.shape, 0)
    vals, idxs = [], []
    for _ in range(TOP_K):
        m = jnp.max(logits, axis=0, keepdims=True)
        idx = jnp.min(jnp.where(logits == m, eid, N_EXPERTS), axis=0, keepdims=True)
        logits = jnp.where(eid == idx, -jnp.inf, logits)
        vals.append(m)
        idxs.append(idx)
    es = [jnp.exp(v - vals[0]) for v in vals]
    inv = 1.0 / (es[0] + es[1] + es[2] + es[3])
    return jnp.concatenate(idxs, axis=0), jnp.concatenate([e * inv for e in es], axis=0)


def _store_token_major(ref, val):
    rows = val.shape[0]
    for c in range(LANE_CHUNKS):
        ref[pl.ds(c, rows, stride=LANE_CHUNKS), :] = val[:, c * LANES:(c + 1) * LANES]


def _load_token_major(ref, rows):
    return jnp.concatenate([ref[pl.ds(c, rows, stride=LANE_CHUNKS), :] for c in range(LANE_CHUNKS)], axis=1)


def _band_attention(qpad, k, v, k_prev, v_prev, sinks_ref, bias_ref, seq_start):
    kb = jnp.concatenate([k_prev, k], axis=0).astype(BF16)
    vt = jnp.transpose(jnp.concatenate([v_prev, v], axis=0)).astype(BF16)
    qb = qpad.astype(BF16)
    lane = lax.broadcasted_iota(I32, (1, GQA_GROUP * WINDOW), 1)
    blocks = []
    for b in range(BLOCKS_PER_TILE):
        bias = bias_ref[jnp.where(seq_start, 1, 0)] if b == 0 else bias_ref[0]
        keys = kb[b * WINDOW:(b + 2) * WINDOW, :]
        pieces = []
        for h in range(N_KV_HEADS):
            qh = jnp.concatenate(
                [qb[b * WINDOW:(b + 1) * WINDOW, (h * GQA_GROUP + j) * LANES:(h * GQA_GROUP + j + 1) * LANES]
                 for j in range(GQA_GROUP)], axis=0)
            st = lax.dot_general(keys, qh, (((1,), (1,)), ((), ())), preferred_element_type=F32) + bias
            sink = jnp.zeros((1, GQA_GROUP * WINDOW), F32)
            for j in range(GQA_GROUP):
                sink = jnp.where(lane // WINDOW == j, sinks_ref[h * GQA_GROUP + j], sink)
            m = jnp.maximum(jnp.max(st, axis=0, keepdims=True), sink)
            e = jnp.exp(st - m)
            inv = 1.0 / (jnp.sum(e, axis=0, keepdims=True) + jnp.exp(sink - m))
            ot = _bdot(vt[h * HEAD_DIM:(h + 1) * HEAD_DIM, b * WINDOW:(b + 2) * WINDOW], (e * inv).astype(BF16))
            pieces.extend(ot[:, j * WINDOW:(j + 1) * WINDOW] for j in range(GQA_GROUP))
        blocks.append(jnp.transpose(jnp.concatenate(pieces, axis=0)))
    return jnp.concatenate(blocks, axis=0)


def _band_bias():
    kj = lax.broadcasted_iota(I32, (2, 2 * WINDOW, GQA_GROUP * WINDOW), 1)
    qi = lax.broadcasted_iota(I32, (2, 2 * WINDOW, GQA_GROUP * WINDOW), 2) % WINDOW
    lo = lax.broadcasted_iota(I32, (2, 2 * WINDOW, GQA_GROUP * WINDOW), 0) * WINDOW
    valid = (kj > qi) & (kj <= qi + WINDOW) & (kj >= lo)
    return jnp.where(valid, 0.0, -jnp.inf).astype(F32)


def _prompt_front_kernel(x_ref, cos_ref, sin_ref, gmix_ref, win_ref, lng_ref, lnb_ref, ws_ref, bsf_ref,
                         sinks_ref, bias_ref, wpa_ref, wpb_ref, wo_ref, gffn_ref, wr3_ref, br_ref, wgu32_ref, wdn32_ref,
                         x1_ref, h2_ref, logits_ref, kwin_ref, vwin_ref, wgu16_ref, wdn16_ref,
                         kprev_ref, vprev_ref, gu_in, dn_in, gu_out, dn_out, cast_sem):
    i = pl.program_id(0)
    seq_start = (i % TILES_PER_SEQ) == 0

    def cast_rows(c, j):
        part = CAST_ROWS // CAST_SPLIT
        return pl.ds(pl.multiple_of(c * CAST_ROWS + j * part, part), part), pl.ds(j * part, part)

    def cast_in(c, slot):
        cps = []
        for j in range(CAST_SPLIT):
            hbm, loc = cast_rows(c, j)
            cps.append(pltpu.make_async_copy(wgu32_ref.at[hbm], gu_in.at[slot, loc], cast_sem.at[0, slot]))
            cps.append(pltpu.make_async_copy(wdn32_ref.at[hbm], dn_in.at[slot, loc], cast_sem.at[1, slot]))
        return cps

    def cast_out(c, slot):
        cps = []
        for j in range(CAST_SPLIT):
            hbm, loc = cast_rows(c, j)
            cps.append(pltpu.make_async_copy(gu_out.at[slot, loc], wgu16_ref.at[hbm], cast_sem.at[2, slot]))
            cps.append(pltpu.make_async_copy(dn_out.at[slot, loc], wdn16_ref.at[hbm], cast_sem.at[3, slot]))
        return cps

    slot = i & 1

    @pl.when(i == 0)
    def _():
        for cp in cast_in(0, 0):
            cp.start()

    @pl.when(i + 1 < FRONT_STEPS)
    def _():
        for cp in cast_in(i + 1, 1 - slot):
            cp.start()

    for cp in cast_in(i, slot):
        cp.wait()

    @pl.when(i >= 2)
    def _():
        for cp in cast_out(i - 2, slot):
            cp.wait()


    @pl.when(seq_start)
    def _():
        kprev_ref[...] = jnp.zeros_like(kprev_ref)
        vprev_ref[...] = jnp.zeros_like(vprev_ref)

    x = x_ref[...]
    u, va, q, k, v, gate_a, gate_b = _in_projection(
        x, gmix_ref, win_ref, lng_ref, lnb_ref, cos_ref[...], sin_ref[...])

    att = _band_attention(q, k, v, kprev_ref[...], vprev_ref[...], sinks_ref, bias_ref, seq_start)

    gu_out[slot] = gu_in[slot].astype(BF16)
    dn_out[slot] = dn_in[slot].astype(BF16)
    k_last, v_last = k[TM - WINDOW:], v[TM - WINDOW:]
    kprev_ref[...] = k_last
    vprev_ref[...] = v_last
    kwin_ref[0] = k_last
    vwin_ref[0] = v_last

    vab = va.astype(BF16)
    zc = jnp.concatenate(
        [jnp.concatenate(
            [_bdot(ws_ref[g], vab[b * CHUNK:(b + 1) * CHUNK, g * A_GROUP_DIM:(g + 1) * A_GROUP_DIM])
             for g in range(A_GROUPS)], axis=1) + bsf_ref[...]
         for b in range(BLOCKS_PER_TILE)], axis=0)

    x1, h2, logits = _merge_and_route(x, u * zc, att, gate_a, gate_b,
                                      wpa_ref, wpb_ref, wo_ref, gffn_ref, wr3_ref, br_ref)
    _store_token_major(x1_ref, x1)
    _store_token_major(h2_ref, h2)
    logits_ref[...] = logits

    for cp in cast_out(i, slot):
        cp.start()

    @pl.when(i == FRONT_STEPS - 1)
    def _():
        for cp in cast_out(i - 1, 1 - slot) + cast_out(i, slot):
            cp.wait()


def _full(shape):
    return pl.BlockSpec(shape, lambda i: (0,) * len(shape))


def _prompt_front(x, cos, sin, gmix, win, lng, lnb, ws, bsf, sinks, bias, wpa, wpb, wo, gffn, wrt, br, wgu32, wdn32):
    n = x.shape[0]
    assert n == N_PROMPT
    grid = (FRONT_STEPS,)
    anyspec = pl.BlockSpec(memory_space=pl.ANY)
    in_specs = [
        pl.BlockSpec((TM, D_MODEL), lambda i: (i, 0)),
        pl.BlockSpec((TM, LANES), lambda i: (i % TILES_PER_SEQ, 0)),
        pl.BlockSpec((TM, LANES), lambda i: (i % TILES_PER_SEQ, 0)),
        _full((1, D_MODEL)),
        _full((D_MODEL, IN_COLS)),
        _full((1, A_WIDTH)),
        _full((1, A_WIDTH)),
        _full((A_GROUPS, CHUNK, CHUNK)),
        _full((CHUNK, A_WIDTH)),
        pl.BlockSpec(memory_space=pltpu.SMEM),
        _full((2, 2 * WINDOW, GQA_GROUP * WINDOW)),
        _full((A_WIDTH, D_MODEL)),
        _full((Q_WIDTH, D_MODEL)),
        _full((D_MODEL, D_MODEL)),
        _full((1, D_MODEL)),
        _full((3 * D_MODEL, LANES)),
        _full((N_EXPERTS, 1)),
        anyspec,
        anyspec,
    ]
    out_shape = [
        jax.ShapeDtypeStruct((n * LANE_CHUNKS, LANES), F32),
        jax.ShapeDtypeStruct((n * LANE_CHUNKS, LANES), F32),
        jax.ShapeDtypeStruct((N_EXPERTS, n), F32),
        jax.ShapeDtypeStruct((n // SEQ, WINDOW, KV_WIDTH), F32),
        jax.ShapeDtypeStruct((n // SEQ, WINDOW, KV_WIDTH), F32),
        jax.ShapeDtypeStruct(wgu32.shape, BF16),
        jax.ShapeDtypeStruct(wdn32.shape, BF16),
    ]
    out_specs = [
        pl.BlockSpec((TM * LANE_CHUNKS, LANES), lambda i: (i, 0)),
        pl.BlockSpec((TM * LANE_CHUNKS, LANES), lambda i: (i, 0)),
        pl.BlockSpec((N_EXPERTS, TM), lambda i: (0, i)),
        pl.BlockSpec((1, WINDOW, KV_WIDTH), lambda i: (i // TILES_PER_SEQ, 0, 0)),
        pl.BlockSpec((1, WINDOW, KV_WIDTH), lambda i: (i // TILES_PER_SEQ, 0, 0)),
        anyspec,
        anyspec,
    ]
    scratch = [
        pltpu.VMEM((WINDOW, KV_WIDTH), F32),
        pltpu.VMEM((WINDOW, KV_WIDTH), F32),
        pltpu.VMEM((2, CAST_ROWS, 2 * D_EXPERT), F32),
        pltpu.VMEM((2, CAST_ROWS, D_MODEL), F32),
        pltpu.VMEM((2, CAST_ROWS, 2 * D_EXPERT), BF16),
        pltpu.VMEM((2, CAST_ROWS, D_MODEL), BF16),
        pltpu.SemaphoreType.DMA((4, 2)),
    ]
    return pl.pallas_call(
        _prompt_front_kernel,
        grid=grid,
        in_specs=in_specs,
        out_specs=out_specs,
        out_shape=out_shape,
        scratch_shapes=scratch,
        compiler_params=pltpu.CompilerParams(dimension_semantics=("arbitrary",), vmem_limit_bytes=VMEM_LIMIT),
        name="prompt_front",
    )(x, cos, sin, gmix, win, lng, lnb, ws, bsf, sinks, bias, wpa, wpb, wo, gffn, wrt, br, wgu32, wdn32)


SAMPLE_STEP = 16
SAMPLE_STEPS = DEC_BATCH // SAMPLE_STEP


def _sample_kernel(x_ref, cos_ref, sin_ref, gmix_ref, win_ref, lng_ref, lnb_ref, wdiag_ref, bs0_ref, sinks_ref,
                   kc_ref, vc_ref, wpa_ref, wpb_ref, wo_ref, gffn_ref, wr3_ref, br_ref,
                   x1_ref, h2_ref, logits_ref, kwin_ref, vwin_ref, va_ref,
                   q_s, k_s, v_s, yain_s, ga_s, gb_s, att_s):
    i = pl.program_id(0)

    @pl.when(i == 0)
    def _():
        x = x_ref[...]
        cos = jnp.broadcast_to(cos_ref[...], (DEC_BATCH, LANES))
        sin = jnp.broadcast_to(sin_ref[...], (DEC_BATCH, LANES))
        u, va, q, k, v, gate_a, gate_b = _in_projection(x, gmix_ref, win_ref, lng_ref, lnb_ref, cos, sin)
        va_ref[...] = va
        z = wdiag_ref[...].astype(F32) * va.astype(BF16).astype(F32) + bs0_ref[...]
        yain_s[...] = u * z
        q_s[...] = q
        k_s[...] = k
        v_s[...] = v
        ga_s[...] = gate_a
        gb_s[...] = gate_b

    r0 = pl.multiple_of(i * SAMPLE_STEP, SAMPLE_STEP)
    kwin = jnp.concatenate([kc_ref[:, 1:, :], k_s[pl.ds(r0, SAMPLE_STEP), :][:, None, :]], axis=1)
    vwin = jnp.concatenate([vc_ref[:, 1:, :], v_s[pl.ds(r0, SAMPLE_STEP), :][:, None, :]], axis=1)
    kwin_ref[...] = kwin
    vwin_ref[...] = vwin

    q16 = q_s[pl.ds(r0, SAMPLE_STEP), :]
    lane = lax.broadcasted_iota(I32, (SAMPLE_STEP, LANES), 1)
    heads = [q16[:, hq * LANES:(hq + 1) * LANES] for hq in range(N_HEADS)]
    qpad = pltpu.einshape("hbd->bhd", jnp.stack(heads, axis=0)).astype(BF16)
    s = jnp.einsum("bhd,bkd->bhk", qpad, kwin.astype(BF16), preferred_element_type=F32)
    hid = lax.broadcasted_iota(I32, (1, N_HEADS, 1), 1)
    sink = jnp.zeros((1, N_HEADS, 1), F32)
    for hq in range(N_HEADS):
        sink = jnp.where(hid == hq, sinks_ref[hq], sink)
    m = jnp.maximum(jnp.max(s, axis=-1, keepdims=True), sink)
    e = jnp.exp(s - m)
    inv = 1.0 / (jnp.sum(e, axis=-1, keepdims=True) + jnp.exp(sink - m))
    o = jnp.einsum("bhk,bkd->bhd", (e * inv).astype(BF16), vwin.astype(BF16), preferred_element_type=F32)
    o = pltpu.einshape("bhd->hbd", o)
    chunks = []
    for c in range(N_HEADS // 2):
        parts = []
        for p in range(2):
            hq = 2 * c + p
            oh = o[hq]
            if p != hq // GQA_GROUP:
                oh = pltpu.roll(oh, HEAD_DIM, 1)
            parts.append(oh)
        chunks.append(jnp.where(lane < HEAD_DIM, parts[0], parts[1]))
    att_s[pl.ds(r0, SAMPLE_STEP), :] = jnp.concatenate(chunks, axis=1)

    @pl.when(i == SAMPLE_STEPS - 1)
    def _():
        x1, h2, logits = _merge_and_route(
            x_ref[...], yain_s[...], att_s[...], ga_s[...], gb_s[...],
            wpa_ref, wpb_ref, wo_ref, gffn_ref, wr3_ref, br_ref)
        _store_token_major(x1_ref, x1)
        _store_token_major(h2_ref, h2)
        logits_ref[...] = logits


def _sample_front(x, cos, sin, gmix, win, lng, lnb, wdiag, bs0, sinks, kc, vc, wpa, wpb, wo, gffn, wrt, br):
    n = DEC_BATCH
    cache_spec = pl.BlockSpec((SAMPLE_STEP, WINDOW, KV_WIDTH), lambda i: (i, 0, 0))
    in_specs = [
        _full((n, D_MODEL)),
        _full((1, LANES)),
        _full((1, LANES)),
        _full((1, D_MODEL)),
        _full((D_MODEL, IN_COLS)),
        _full((1, A_WIDTH)),
        _full((1, A_WIDTH)),
        _full((1, A_WIDTH)),
        _full((1, A_WIDTH)),
        pl.BlockSpec(memory_space=pltpu.SMEM),
        cache_spec,
        cache_spec,
        _full((A_WIDTH, D_MODEL)),
        _full((Q_WIDTH, D_MODEL)),
        _full((D_MODEL, D_MODEL)),
        _full((1, D_MODEL)),
        _full((3 * D_MODEL, LANES)),
        _full((N_EXPERTS, 1)),
    ]
    out_shape = [
        jax.ShapeDtypeStruct((n * LANE_CHUNKS, LANES), F32),
        jax.ShapeDtypeStruct((n * LANE_CHUNKS, LANES), F32),
        jax.ShapeDtypeStruct((N_EXPERTS, n), F32),
        jax.ShapeDtypeStruct((n, WINDOW, KV_WIDTH), F32),
        jax.ShapeDtypeStruct((n, WINDOW, KV_WIDTH), F32),
        jax.ShapeDtypeStruct((n, A_WIDTH), F32),
    ]
    out_specs = [
        _full((n * LANE_CHUNKS, LANES)),
        _full((n * LANE_CHUNKS, LANES)),
        _full((N_EXPERTS, n)),
        cache_spec,
        cache_spec,
        _full((n, A_WIDTH)),
    ]
    scratch = [
        pltpu.VMEM((n, QPAD_WIDTH), F32), pltpu.VMEM((n, KV_WIDTH), F32), pltpu.VMEM((n, KV_WIDTH), F32),
        pltpu.VMEM((n, A_WIDTH), F32), pltpu.VMEM((n, D_MODEL), F32), pltpu.VMEM((n, D_MODEL), F32),
        pltpu.VMEM((n, Q_WIDTH), F32),
    ]
    return pl.pallas_call(
        _sample_kernel,
        grid=(SAMPLE_STEPS,),
        in_specs=in_specs,
        out_specs=out_specs,
        out_shape=out_shape,
        scratch_shapes=scratch,
        compiler_params=pltpu.CompilerParams(dimension_semantics=("arbitrary",), vmem_limit_bytes=VMEM_LIMIT),
        name="sample_front",
    )(x, cos, sin, gmix, win, lng, lnb, wdiag, bs0, sinks, kc, vc, wpa, wpb, wo, gffn, wrt, br)


def _route_plan_kernel(lp_ref, ls_ref, dest_ref, wts_ref, off_ref):
    g = pl.program_id(0)
    topi, topw = _top4_softmax(jnp.concatenate([lp_ref[...], ls_ref[...]], axis=1))
    slot = lax.broadcasted_iota(I32, (TOP_K, GROUP_SLOTS), 1)
    eall = jnp.where(jnp.logical_or(slot < GROUP_PROMPT, g == N_GROUPS - 1), topi, N_EXPERTS)
    wts_ref[:, 0:GROUP_SLOTS] = topw
    wts_ref[:, GROUP_SLOTS:] = jnp.zeros((TOP_K, K_STRIDE - GROUP_SLOTS), F32)
    dest_ref[:, GROUP_SLOTS:] = jnp.zeros((TOP_K, K_STRIDE - GROUP_SLOTS), I32)
    eid = lax.broadcasted_iota(I32, (N_EXPERTS, GROUP_SLOTS), 0)
    onehots = [eall[k:k + 1, :] == eid for k in range(TOP_K)]
    count = jnp.zeros((N_EXPERTS, GROUP_SLOTS), F32)
    for oh in onehots:
        count = count + oh.astype(F32)
    total = jnp.broadcast_to(jnp.sum(count, axis=1, keepdims=True), (N_EXPERTS, LANES))
    padded = total + (MOE_ROWS - 1)
    nblk = jnp.floor(padded * (1.0 / MOE_ROWS))
    rem = padded - nblk * MOE_ROWS
    nblk = jnp.where(rem >= MOE_ROWS, nblk + 1.0, jnp.where(rem < 0.0, nblk - 1.0, nblk))
    r = lax.broadcasted_iota(I32, (N_EXPERTS, N_EXPERTS), 0)
    c = lax.broadcasted_iota(I32, (N_EXPERTS, N_EXPERTS), 1)
    first_blk = lax.dot_general((c < r).astype(F32), nblk, (((1,), (0,)), ((), ())),
                                precision=lax.Precision.HIGHEST, preferred_element_type=F32)
    start = (first_blk + LEAD_BLOCKS) * MOE_ROWS
    lane = lax.broadcasted_iota(I32, (N_EXPERTS, LANES), 1)
    info = jnp.where(lane == 0, start, jnp.where(lane == 1, start + total, jnp.where(lane == 2, nblk, first_blk)))
    off_ref[...] = info.astype(I32)
    ti = lax.broadcasted_iota(I32, (LANES, LANES), 0)
    tj = lax.broadcasted_iota(I32, (LANES, LANES), 1)
    before = (ti < tj).astype(BF16)
    ones = jnp.ones((LANES, LANES), BF16)
    running = start
    for t in range(SLOT_TILES):
        sl = slice(t * LANES, (t + 1) * LANES)
        cb = count[:, sl].astype(BF16)
        pos = running + _bdot(cb, before)
        rows = [jnp.sum(jnp.where(oh[:, sl], pos, 0.0), axis=0, keepdims=True) for oh in onehots]
        dest_ref[:, sl] = jnp.concatenate(rows, axis=0).astype(I32)
        running = running + _bdot(cb, ones)


def _route_plan(logits_p, logits_s):
    in_specs = [
        pl.BlockSpec((N_EXPERTS, GROUP_PROMPT), lambda g: (0, g)),
        pl.BlockSpec((N_EXPERTS, DEC_BATCH), lambda g: (0, 0)),
    ]
    out_shape = [
        jax.ShapeDtypeStruct((N_GROUPS, TOP_K, K_STRIDE), I32),
        jax.ShapeDtypeStruct((N_GROUPS, TOP_K, K_STRIDE), F32),
        jax.ShapeDtypeStruct((N_GROUPS, N_EXPERTS, LANES), I32),
    ]
    out_specs = [
        pl.BlockSpec((None, TOP_K, K_STRIDE), lambda g: (g, 0, 0)),
        pl.BlockSpec((None, TOP_K, K_STRIDE), lambda g: (g, 0, 0)),
        pl.BlockSpec((None, N_EXPERTS, LANES), lambda g: (g, 0, 0)),
    ]
    return pl.pallas_call(
        _route_plan_kernel,
        grid=(N_GROUPS,),
        in_specs=in_specs,
        out_specs=out_specs,
        out_shape=out_shape,
        compiler_params=pltpu.CompilerParams(dimension_semantics=("arbitrary",)),
        name="route_plan",
    )(logits_p, logits_s)


GROUP_ROWS = GROUP_PROMPT * LANE_CHUNKS
SAMPLE_ROWS = DEC_BATCH * LANE_CHUNKS
TRASH_SLOT = GROUP_SLOTS
BUF_ROWS = (GROUP_SLOTS + 1) * LANE_CHUNKS
SCATTER_BATCH = 8
DMA_SPLIT = 8
FFN_COLS = 256
FFN_DOTS = (2 * D_EXPERT + D_MODEL) // FFN_COLS
GATHER_DOTS = 9
assert D_EXPERT == D_MODEL and GATHER_DOTS < FFN_DOTS


def _moe_kernel(off_ref, desth_ref, wtsh_ref, h2p_ref, h2s_ref, x1p_ref, x1s_ref, wgu_ref, bgu_ref, wdn_ref, bdn_ref,
                x2p_ref, x2s_ref,
                h2buf, acc, wgubuf, bgubuf, wdnbuf, bdnbuf, xs, xb0, xb1, ab0, ab1, ys0, ys1,
                dest_ref, wts_ref, src_ref, seg_expert, seg_first, blk_seg, act_sem, w_sem):
    g = pl.program_id(0)
    last = g == N_GROUPS - 1
    row0 = pl.multiple_of(g * GROUP_ROWS, GROUP_ROWS)

    def prompt_copies():
        cps = []
        for j in range(DMA_SPLIT):
            src = pl.ds(row0 + j * (GROUP_ROWS // DMA_SPLIT), GROUP_ROWS // DMA_SPLIT)
            dst = pl.ds(j * (GROUP_ROWS // DMA_SPLIT), GROUP_ROWS // DMA_SPLIT)
            cps.append(pltpu.make_async_copy(h2p_ref.at[src], h2buf.at[dst], act_sem.at[0]))
            cps.append(pltpu.make_async_copy(x1p_ref.at[src], acc.at[dst], act_sem.at[1]))
        return cps

    def sample_copies():
        return (pltpu.make_async_copy(h2s_ref, h2buf.at[pl.ds(GROUP_ROWS, SAMPLE_ROWS)], act_sem.at[2]),
                pltpu.make_async_copy(x1s_ref, acc.at[pl.ds(GROUP_ROWS, SAMPLE_ROWS)], act_sem.at[3]))

    def gate_up_copies(e, slot):
        cps = [pltpu.make_async_copy(bgu_ref.at[e], bgubuf.at[slot], w_sem.at[1, slot])]
        for j in range(DMA_SPLIT):
            rg = pl.ds(j * (D_MODEL // DMA_SPLIT), D_MODEL // DMA_SPLIT)
            cps.append(pltpu.make_async_copy(wgu_ref.at[e, rg], wgubuf.at[slot, rg], w_sem.at[0, slot]))
        return cps

    def down_copies(e, slot):
        cps = [pltpu.make_async_copy(bdn_ref.at[e], bdnbuf.at[slot], w_sem.at[3, slot])]
        for j in range(DMA_SPLIT):
            rd = pl.ds(j * (D_EXPERT // DMA_SPLIT), D_EXPERT // DMA_SPLIT)
            cps.append(pltpu.make_async_copy(wdn_ref.at[e, rd], wdnbuf.at[slot, rd], w_sem.at[2, slot]))
        return cps

    def output_copies():
        return [pltpu.make_async_copy(
            acc.at[pl.ds(j * (GROUP_ROWS // DMA_SPLIT), GROUP_ROWS // DMA_SPLIT)],
            x2p_ref.at[pl.ds(row0 + j * (GROUP_ROWS // DMA_SPLIT), GROUP_ROWS // DMA_SPLIT)], act_sem.at[0])
            for j in range(DMA_SPLIT)]

    tab0 = pl.multiple_of(g * (TOP_K * K_STRIDE), TOP_K * K_STRIDE)
    table_copies = (
        pltpu.make_async_copy(desth_ref.at[pl.ds(tab0, TOP_K * K_STRIDE)], dest_ref, act_sem.at[4]),
        pltpu.make_async_copy(wtsh_ref.at[pl.ds(tab0, TOP_K * K_STRIDE)], wts_ref, act_sem.at[5]))
    for cp in table_copies:
        cp.start()

    for cp in prompt_copies():
        cp.start()

    @pl.when(last)
    def _():
        for cp in sample_copies():
            cp.start()

    trash = pl.ds(TRASH_SLOT * LANE_CHUNKS, LANE_CHUNKS)
    h2buf[trash, :] = jnp.zeros((LANE_CHUNKS, LANES), F32)
    acc[trash, :] = jnp.zeros((LANE_CHUNKS, LANES), F32)
    ys0[...] = jnp.zeros_like(ys0)
    ys1[...] = jnp.zeros_like(ys1)
    ab1[...] = jnp.zeros_like(ab1)

    def pad_block(pos0):
        def body(j, carry):
            for d in range(SUBLANES):
                src_ref[pos0 + j * SUBLANES + d] = TRASH_SLOT
            return carry
        lax.fori_loop(0, MOE_ROWS // SUBLANES, body, 0)

    def scan_expert(e, carry):
        nseg, nblocks = carry
        nblk = off_ref[e, 2]
        first = off_ref[e, 3]

        @pl.when(nblk > 0)
        def _():
            seg_expert[nseg] = e
            seg_first[nseg] = first
            pad_block(off_ref[e, 0] + (nblk - 1) * MOE_ROWS)

            def mark(b, c2):
                blk_seg[first + b] = nseg
                return c2
            lax.fori_loop(0, nblk, mark, 0)

        return nseg + jnp.where(nblk > 0, 1, 0), nblocks + nblk

    nseg, nblocks = lax.fori_loop(0, N_EXPERTS, scan_expert, (jnp.int32(0), jnp.int32(0)))
    for j in range(LEAD_BLOCKS):
        pad_block(j * MOE_ROWS)
    for j in range(TAIL_BLOCKS):
        pad_block((nblocks + LEAD_BLOCKS + j) * MOE_ROWS)
        blk_seg[nblocks + j] = nseg - 1

    for cp in gate_up_copies(seg_expert[0], 0) + down_copies(seg_expert[0], 0):
        cp.start()

    for cp in table_copies:
        cp.wait()

    nvalid = jnp.where(last, GROUP_SLOTS, GROUP_PROMPT)
    for k in range(TOP_K):
        def fill(j, carry, k=k):
            c0 = k * K_STRIDE + j * SUBLANES
            for d in range(SUBLANES):
                src_ref[dest_ref[c0 + d]] = c0 + d
            return carry
        lax.fori_loop(0, nvalid // SUBLANES, fill, 0)

    for cp in prompt_copies():
        cp.wait()

    @pl.when(last)
    def _():
        for cp in sample_copies():
            cp.wait()

    def token_rows(code):
        slot_id = code & (K_STRIDE - 1)
        return pl.ds(pl.multiple_of(slot_id * LANE_CHUNKS, LANE_CHUNKS), LANE_CHUNKS)

    def gather(b, lo=0, hi=MOE_ROWS):
        base = (b + LEAD_BLOCKS) * MOE_ROWS
        for m in range(lo, hi):
            xs[pl.ds(m, LANE_CHUNKS, stride=XS_STRIDE), :] = h2buf[token_rows(src_ref[base + m]), :]

    def pack(xb, lo=0, hi=LANE_CHUNKS):
        for c in range(lo, hi):
            xb[:, c * LANES:(c + 1) * LANES] = xs[c * XS_STRIDE:c * XS_STRIDE + MOE_ROWS, :].astype(BF16)

    def scatter_add(b, ys, lo=0, hi=MOE_ROWS):
        base = (b + LEAD_BLOCKS) * MOE_ROWS
        for m0 in range(lo, hi, SCATTER_BATCH):
            pending = []
            for m in range(m0, m0 + SCATTER_BATCH):
                code = src_ref[base + m]
                rows = token_rows(code)
                pending.append((rows, acc[rows, :] + wts_ref[code] * ys[pl.ds(m, LANE_CHUNKS, stride=XS_STRIDE), :]))
            for rows, val in pending:
                acc[rows, :] = val

    gather_cuts = [round(i * MOE_ROWS / GATHER_DOTS) for i in range(GATHER_DOTS + 1)]
    pack_cuts = [round(i * LANE_CHUNKS / (FFN_DOTS - GATHER_DOTS)) for i in range(FFN_DOTS - GATHER_DOTS + 1)]
    scatter_cuts = [SCATTER_BATCH * round(i * (MOE_ROWS // SCATTER_BATCH) / FFN_DOTS) for i in range(FFN_DOTS + 1)]

    def step(t, xb_cur, xb_next, ab_cur, ab_prev, ys_down, ys_scatter):
        seg_g = blk_seg[t]
        slot_g = seg_g & 1
        tp = jnp.maximum(t - 1, 0)
        seg_d = blk_seg[tp]
        slot_d = seg_d & 1

        @pl.when(jnp.logical_and(t == seg_first[seg_g], t < nblocks))
        def _():
            for cp in gate_up_copies(seg_expert[seg_g], slot_g):
                cp.wait()

            @pl.when(seg_g + 1 < nseg)
            def _():
                for cp in gate_up_copies(seg_expert[seg_g + 1], 1 - slot_g):
                    cp.start()

        new_down = jnp.logical_and(jnp.logical_and(t >= 2, tp == seg_first[seg_d]), tp < nblocks)

        @pl.when(jnp.logical_or(t == 0, new_down))
        def _():
            for cp in down_copies(seg_expert[seg_d], slot_d):
                cp.wait()

            @pl.when(seg_d + 1 < nseg)
            def _():
                for cp in down_copies(seg_expert[seg_d + 1], 1 - slot_d):
                    cp.start()

        done = [0]

        def row_traffic():
            i = done[0]
            if i < GATHER_DOTS:
                gather(t + 1, gather_cuts[i], gather_cuts[i + 1])
            else:
                pack(xb_next, pack_cuts[i - GATHER_DOTS], pack_cuts[i - GATHER_DOTS + 1])
            scatter_add(t - 2, ys_scatter, scatter_cuts[i], scatter_cuts[i + 1])
            done[0] = i + 1

        x = xb_cur[...]
        a_prev = ab_prev[...]
        for c in range(D_EXPERT // FFN_COLS):
            gc = pl.ds(c * FFN_COLS, FFN_COLS)
            uc = pl.ds(D_EXPERT + c * FFN_COLS, FFN_COLS)
            g = _bdot(x, wgubuf[slot_g, :, gc]) + bgubuf[slot_g, :, gc]
            row_traffic()
            u = _bdot(x, wgubuf[slot_g, :, uc]) + bgubuf[slot_g, :, uc]
            row_traffic()
            gl = jnp.minimum(g, SWIGLU_LIMIT)
            ul = jnp.clip(u, -SWIGLU_LIMIT, SWIGLU_LIMIT)
            ab_cur[:, c * FFN_COLS:(c + 1) * FFN_COLS] = (
                (ul + 1.0) * (gl * jax.nn.sigmoid(SWIGLU_ALPHA * gl))).astype(BF16)
        for c in range(D_MODEL // FFN_COLS):
            gc = pl.ds(c * FFN_COLS, FFN_COLS)
            y = _bdot(a_prev, wdnbuf[slot_d, :, gc]) + bdnbuf[slot_d, :, gc]
            for j in range(FFN_COLS // LANES):
                lc = c * (FFN_COLS // LANES) + j
                ys_down[lc * XS_STRIDE:lc * XS_STRIDE + MOE_ROWS, :] = y[:, j * LANES:(j + 1) * LANES]
            row_traffic()

    gather(0)
    pack(xb0)
    npairs = (nblocks + 3) // 2

    def pair(k, carry):
        step(2 * k, xb0, xb1, ab0, ab1, ys1, ys0)
        step(2 * k + 1, xb1, xb0, ab1, ab0, ys0, ys1)
        return carry

    lax.fori_loop(0, npairs, pair, 0)

    for cp in output_copies():
        cp.start()

    @pl.when(last)
    def _():
        out_s = pltpu.make_async_copy(acc.at[pl.ds(GROUP_ROWS, SAMPLE_ROWS)], x2s_ref, act_sem.at[2])
        out_s.start()
        out_s.wait()

    for cp in output_copies():
        cp.wait()


def _moe(dest, wts, off, h2p, h2s, x1p, x1s, wgu, bgu, wdn, bdn):
    anyspec = pl.BlockSpec(memory_space=pl.ANY)
    dest = dest.reshape(N_GROUPS * TOP_K * K_STRIDE)
    wts = wts.reshape(N_GROUPS * TOP_K * K_STRIDE)
    in_specs = [
        pl.BlockSpec((None, N_EXPERTS, LANES), lambda g: (g, 0, 0), memory_space=pltpu.SMEM),
        anyspec, anyspec, anyspec, anyspec, anyspec, anyspec, anyspec, anyspec, anyspec, anyspec,
    ]
    scratch = [
        pltpu.VMEM((BUF_ROWS, LANES), F32),
        pltpu.VMEM((BUF_ROWS, LANES), F32),
        pltpu.VMEM((2, D_MODEL, 2 * D_EXPERT), BF16),
        pltpu.VMEM((2, 1, 2 * D_EXPERT), F32),
        pltpu.VMEM((2, D_EXPERT, D_MODEL), BF16),
        pltpu.VMEM((2, 1, D_MODEL), F32),
        pltpu.VMEM((LANE_CHUNKS * XS_STRIDE, LANES), F32),
        pltpu.VMEM((MOE_ROWS, D_MODEL), BF16),
        pltpu.VMEM((MOE_ROWS, D_MODEL), BF16),
        pltpu.VMEM((MOE_ROWS, D_EXPERT), BF16),
        pltpu.VMEM((MOE_ROWS, D_EXPERT), BF16),
        pltpu.VMEM((LANE_CHUNKS * XS_STRIDE, LANES), F32),
        pltpu.VMEM((LANE_CHUNKS * XS_STRIDE, LANES), F32),
        pltpu.SMEM((TOP_K * K_STRIDE,), I32),
        pltpu.SMEM((TOP_K * K_STRIDE,), F32),
        pltpu.SMEM((POS_TABLE,), I32),
        pltpu.SMEM((N_EXPERTS,), I32),
        pltpu.SMEM((N_EXPERTS,), I32),
        pltpu.SMEM((LANES,), I32),
        pltpu.SemaphoreType.DMA((6,)),
        pltpu.SemaphoreType.DMA((4, 2)),
    ]
    return pl.pallas_call(
        _moe_kernel,
        grid=(N_GROUPS,),
        in_specs=in_specs,
        out_specs=[anyspec, anyspec],
        out_shape=[jax.ShapeDtypeStruct(x1p.shape, F32), jax.ShapeDtypeStruct(x1s.shape, F32)],
        scratch_shapes=scratch,
        compiler_params=pltpu.CompilerParams(dimension_semantics=("arbitrary",), vmem_limit_bytes=VMEM_LIMIT),
        name="moe",
    )(off, dest, wts, h2p, h2s, x1p, x1s, wgu, bgu, wdn, bdn)


def _ple_final_kernel(x2_ref, ple_ref, wple_ref, gple_ref, wpg_ref, gfin_ref, y_ref):
    rows = y_ref.shape[0]
    x2 = _load_token_major(x2_ref, rows)
    e = _rmsnorm(_bdot(ple_ref[...].astype(BF16), wple_ref[...]), gple_ref[...])
    x3 = x2 + jax.nn.sigmoid(_bdot(x2.astype(BF16), wpg_ref[...])) * e
    y_ref[...] = _rmsnorm(x3, gfin_ref[...])


def _ple_final(x2_tm, ple, wple, gple, wpg, gfin, tile):
    n = ple.shape[0]
    return pl.pallas_call(
        _ple_final_kernel,
        grid=(n // tile,),
        in_specs=[
            pl.BlockSpec((tile * LANE_CHUNKS, LANES), lambda i: (i, 0)),
            pl.BlockSpec((tile, PLE_DIM), lambda i: (i, 0)),
            _full((PLE_DIM, D_MODEL)),
            _full((1, D_MODEL)),
            _full((D_MODEL, D_MODEL)),
            _full((1, D_MODEL)),
        ],
        out_specs=pl.BlockSpec((tile, D_MODEL), lambda i: (i, 0)),
        out_shape=jax.ShapeDtypeStruct((n, D_MODEL), F32),
        compiler_params=pltpu.CompilerParams(dimension_semantics=("arbitrary",), vmem_limit_bytes=VMEM_LIMIT),
        name="ple_final",
    )(x2_tm, ple, wple, gple, wpg, gfin)


def _rope_tables(pos):
    half = HEAD_DIM // 2
    inv = ROPE_THETA ** (-jnp.arange(half, dtype=F32) / half)
    ang = pos.astype(F32)[:, None] * inv[None, :]
    cos, sin = jnp.cos(ang), jnp.sin(ang)
    cos2 = jnp.concatenate([cos, cos, cos, cos], axis=1)
    sin2 = jnp.concatenate([-sin, sin, -sin, sin], axis=1)
    return cos2, sin2


def _layout_w_in(w_in):
    o_q = 2 * A_WIDTH
    wq = w_in[:, o_q:o_q + Q_WIDTH].reshape(D_MODEL, N_HEADS, HEAD_DIM) * (HEAD_DIM ** -0.5)
    kv_head = (jnp.arange(N_HEADS) // GQA_GROUP)[None, :, None]
    wq_pad = jnp.concatenate([jnp.where(kv_head == h, wq, 0.0) for h in range(N_KV_HEADS)], axis=-1)
    return jnp.concatenate([w_in[:, :o_q], wq_pad.reshape(D_MODEL, QPAD_WIDTH), w_in[:, o_q + Q_WIDTH:]], axis=1)


def _router_passes(w_router):
    hi = w_router.astype(BF16)
    lo = (w_router - hi.astype(F32)).astype(BF16)
    w3 = jnp.concatenate([hi, lo, hi], axis=0)
    return jnp.pad(w3, ((0, 0), (0, LANES - N_EXPERTS)))


def _prep_weights(g_mix, w_in, a_ln_g, a_ln_b, a_ws, a_bs, w_pa, w_pb, w_o, g_ffn, w_router, b_router):
    causal = jnp.tril(jnp.ones((CHUNK, CHUNK), dtype=bool))
    return dict(
        gmix=g_mix.reshape(1, D_MODEL),
        win=_layout_w_in(w_in).astype(BF16),
        lng=a_ln_g.reshape(1, A_WIDTH),
        lnb=a_ln_b.reshape(1, A_WIDTH),
        ws=jnp.where(causal[None], a_ws, 0.0).astype(BF16),
        bsf=jnp.repeat(jnp.transpose(a_bs), A_GROUP_DIM, axis=1),
        wpa=w_pa.astype(BF16),
        wpb=w_pb.astype(BF16),
        wo=w_o.astype(BF16),
        gffn=g_ffn.reshape(1, D_MODEL),
        wrt=_router_passes(w_router),
        br=b_router.reshape(N_EXPERTS, 1),
    )


def kernel(x_prompt, x_sample, cache_win_k, cache_win_v, p_prompt, p_sample, g_mix, w_in, a_ln_g, a_ln_b, a_ws, a_bs, sinks, w_pa, w_pb, w_o, g_ffn, w_router, b_router, w_gu, b_gu, w_down, b_down, w_ple, g_ple, w_ple_gate, g_final):
    W = _prep_weights(g_mix[0], w_in[0], a_ln_g[0], a_ln_b[0], a_ws[0], a_bs[0], w_pa[0], w_pb[0], w_o[0],
                      g_ffn[0], w_router[0], b_router[0])
    cos_p, sin_p = _rope_tables(jnp.arange(SEQ, dtype=I32))
    cos_s, sin_s = _rope_tables(jnp.full((1,), PAST_LEN, I32))
    x1p, h2p, logits_p, kwin_p, vwin_p, wgu16, wdn16 = _prompt_front(
        x_prompt.reshape(N_PROMPT, D_MODEL), cos_p, sin_p, W["gmix"], W["win"], W["lng"], W["lnb"],
        W["ws"], W["bsf"], sinks[0], _band_bias(), W["wpa"], W["wpb"], W["wo"], W["gffn"], W["wrt"], W["br"],
        w_gu[0].reshape(N_EXPERTS * D_MODEL, 2 * D_EXPERT), w_down[0].reshape(N_EXPERTS * D_EXPERT, D_MODEL))

    wdiag = jnp.repeat(a_ws[0, :, 0, 0], A_GROUP_DIM)[None, :].astype(BF16)
    bs0 = jnp.repeat(a_bs[0, :, 0], A_GROUP_DIM)[None, :]
    x1s, h2s, logits_s, kwin_s, vwin_s, va_s = _sample_front(
        x_sample.reshape(DEC_BATCH, D_MODEL), cos_s, sin_s, W["gmix"], W["win"], W["lng"], W["lnb"], wdiag, bs0,
        sinks[0], cache_win_k[0].reshape(DEC_BATCH, WINDOW, KV_WIDTH), cache_win_v[0].reshape(DEC_BATCH, WINDOW, KV_WIDTH),
        W["wpa"], W["wpb"], W["wo"], W["gffn"], W["wrt"], W["br"])

    dest, wts, off = _route_plan(logits_p, logits_s)
    x2p, x2s = _moe(dest, wts, off, h2p, h2s, x1p, x1s,
                    wgu16.reshape(N_EXPERTS, D_MODEL, 2 * D_EXPERT), b_gu[0].reshape(N_EXPERTS, 1, 2 * D_EXPERT),
                    wdn16.reshape(N_EXPERTS, D_EXPERT, D_MODEL), b_down[0].reshape(N_EXPERTS, 1, D_MODEL))

    wple = w_ple[0].astype(BF16)
    gple = g_ple[0].reshape(1, D_MODEL)
    wpg = w_ple_gate[0].astype(BF16)
    gfin = g_final.reshape(1, D_MODEL)
    y_p = _ple_final(x2p, p_prompt[0].reshape(N_PROMPT, PLE_DIM), wple, gple, wpg, gfin, 2 * TM)
    y_s = _ple_final(x2s, p_sample[0].reshape(DEC_BATCH, PLE_DIM), wple, gple, wpg, gfin, DEC_BATCH)

    return (
        y_p.reshape(BATCH, SEQ, D_MODEL),
        y_s.reshape(DEC_BATCH, 1, D_MODEL),
        kwin_p.reshape(1, BATCH, WINDOW, N_KV_HEADS, HEAD_DIM),
        vwin_p.reshape(1, BATCH, WINDOW, N_KV_HEADS, HEAD_DIM),
        kwin_s.reshape(1, DEC_BATCH, WINDOW, N_KV_HEADS, HEAD_DIM),
        vwin_s.reshape(1, DEC_BATCH, WINDOW, N_KV_HEADS, HEAD_DIM),
        va_s.reshape(1, DEC_BATCH, 1, A_WIDTH),
    )
```

```python
import functools

import jax
import jax.numpy as jnp
from jax import lax
from jax.experimental import pallas as pl
from jax.experimental.pallas import tpu as pltpu

F32 = jnp.float32
BF16 = jnp.bfloat16
I32 = jnp.int32

D_MODEL = 1024
BATCH = 4
SEQ = 4096
DEC_BATCH = 128
PAST_LEN = 8192
CHUNK = 128
A_GROUPS = 4
A_GROUP_DIM = 128
A_WIDTH = A_GROUPS * A_GROUP_DIM
N_HEADS = 8
N_KV_HEADS = 2
HEAD_DIM = 64
Q_WIDTH = N_HEADS * HEAD_DIM
KV_WIDTH = N_KV_HEADS * HEAD_DIM
GQA_GROUP = N_HEADS // N_KV_HEADS
WINDOW = 128
ROPE_THETA = 10000.0
N_EXPERTS = 32
TOP_K = 4
D_EXPERT = D_MODEL
SWIGLU_ALPHA = 1.702
SWIGLU_LIMIT = 7.0
PLE_DIM = 256
RMS_EPS = 1e-5
LN_EPS = 1e-5

LANES = 128

QPAD_WIDTH = N_HEADS * LANES
O_Q = 2 * A_WIDTH
O_K = O_Q + QPAD_WIDTH
O_V = O_K + KV_WIDTH
O_GA = O_V + KV_WIDTH
O_GB = O_GA + D_MODEL
IN_COLS = O_GB + D_MODEL
SUBLANES = 8
LANE_CHUNKS = D_MODEL // LANES
VMEM_LIMIT = 56 * 1024 * 1024

N_PROMPT = BATCH * SEQ
TM = 256
TILES_PER_SEQ = SEQ // TM
BLOCKS_PER_TILE = TM // WINDOW
FRONT_STEPS = N_PROMPT // TM
CAST_ROWS = N_EXPERTS * D_MODEL // FRONT_STEPS
CAST_SPLIT = 4
IN_PROJ_DOTS = 5
BF16_ROWS = 16
CAST_CUTS = [BF16_ROWS * round(i * (CAST_ROWS // BF16_ROWS) / IN_PROJ_DOTS) for i in range(IN_PROJ_DOTS + 1)]

N_GROUPS = 4
GROUP_PROMPT = N_PROMPT // N_GROUPS
GROUP_SLOTS = GROUP_PROMPT + DEC_BATCH
GROUP_ASSIGN = GROUP_SLOTS * TOP_K
SLOT_TILES = GROUP_SLOTS // LANES
MOE_ROWS = 256
XS_STRIDE = MOE_ROWS + SUBLANES
SLOT_BITS = 13
K_STRIDE = 1 << SLOT_BITS
assert GROUP_SLOTS < K_STRIDE
MAX_BLOCKS = GROUP_ASSIGN // MOE_ROWS + N_EXPERTS
POS_TABLE = 1 << 15
assert (MAX_BLOCKS + 3) * MOE_ROWS <= POS_TABLE
assert MAX_BLOCKS + 2 <= LANES


def _bdot(a, b):
    return jnp.dot(a, b, preferred_element_type=F32)


def _rmsnorm(x, g):
    return x * lax.rsqrt(jnp.mean(x * x, axis=-1, keepdims=True) + RMS_EPS) * g


def _gelu(x):
    return 0.5 * x * (1.0 + lax.erf(x * (0.5 ** 0.5)))


def _group_layernorm(v, g, b):
    cols = []
    for gi in range(A_GROUPS):
        s = slice(gi * A_GROUP_DIM, (gi + 1) * A_GROUP_DIM)
        vg = v[:, s]
        mu = jnp.mean(vg, axis=-1, keepdims=True)
        d = vg - mu
        var = jnp.mean(d * d, axis=-1, keepdims=True)
        cols.append(d * lax.rsqrt(var + LN_EPS) * g[:, s] + b[:, s])
    return jnp.concatenate(cols, axis=1)


def _rope(x, cos, sin_signed):
    width = x.shape[1]
    reps = width // LANES
    cosf = jnp.concatenate([cos] * reps, axis=1) if reps > 1 else cos
    sinf = jnp.concatenate([sin_signed] * reps, axis=1) if reps > 1 else sin_signed
    half = HEAD_DIM // 2
    lane = lax.broadcasted_iota(I32, x.shape, 1)
    up = pltpu.roll(x, width - half, 1)
    down = pltpu.roll(x, half, 1)
    partner = jnp.where((lane & (HEAD_DIM - 1)) < half, up, down)
    return x * cosf + partner * sinf


def _in_projection(x, gmix_ref, win_ref, lng_ref, lnb_ref, cos, sin_signed, after_dot=lambda: None):
    hb = _rmsnorm(x, gmix_ref[...]).astype(BF16)

    def dot_cols(lo, hi):
        z = _bdot(hb, win_ref[:, lo:hi])
        after_dot()
        return z

    u = _gelu(dot_cols(0, A_WIDTH))
    va = _group_layernorm(_gelu(dot_cols(A_WIDTH, O_Q)), lng_ref[...], lnb_ref[...])
    zq = jnp.concatenate([dot_cols(O_Q, O_Q + QPAD_WIDTH // 2), dot_cols(O_Q + QPAD_WIDTH // 2, O_K)], axis=1)
    zkv = dot_cols(O_K, O_GA)
    q = _rope(zq, cos, sin_signed)
    k = _rope(zkv[:, :KV_WIDTH], cos, sin_signed)
    v = zkv[:, KV_WIDTH:]
    return u, va, q, k, v, hb


def _gate_preact(hb, win_ref, part, parts):
    width = 2 * D_MODEL // parts
    return _bdot(hb, win_ref[:, O_GA + part * width:O_GA + (part + 1) * width])


def _gates(preacts):
    zg = jnp.concatenate(preacts, axis=1)
    return jax.nn.sigmoid(zg[:, :D_MODEL]), jax.nn.sigmoid(zg[:, D_MODEL:])


def _merge_and_route(x, ya_in, att, gate_a, gate_b, wpa_ref, wpb_ref, wo_ref, gffn_ref, wr3_ref, br_ref):
    ya = _bdot(ya_in.astype(BF16), wpa_ref[...])
    yb = _bdot(att.astype(BF16), wpb_ref[...])
    mix = (gate_a * ya + gate_b * yb).astype(BF16)
    x1 = x + _bdot(mix, wo_ref[...])
    h2 = _rmsnorm(x1, gffn_ref[...])
    hi = h2.astype(BF16)
    lo = (h2 - hi.astype(F32)).astype(BF16)
    logits = _bdot(jnp.concatenate([hi, hi, lo], axis=1), wr3_ref[...])
    return x1, h2, jnp.transpose(logits)[:N_EXPERTS, :] + br_ref[...]


def _top4_softmax(logits):
    eid = lax.broadcasted_iota(I32, logits.shape, 0)
    vals, idxs = [], []
    for _ in range(TOP_K):
        m = jnp.max(logits, axis=0, keepdims=True)
        idx = jnp.min(jnp.where(logits == m, eid, N_EXPERTS), axis=0, keepdims=True)
        logits = jnp.where(eid == idx, -jnp.inf, logits)
        vals.append(m)
        idxs.append(idx)
    es = [jnp.exp(v - vals[0]) for v in vals]
    inv = 1.0 / (es[0] + es[1] + es[2] + es[3])
    return jnp.concatenate(idxs, axis=0), jnp.concatenate([e * inv for e in es], axis=0)


def _store_token_major(ref, val):
    rows = val.shape[0]
    for c in range(LANE_CHUNKS):
        ref[pl.ds(c, rows, stride=LANE_CHUNKS), :] = val[:, c * LANES:(c + 1) * LANES]


def _load_token_major(ref, rows):
    return jnp.concatenate([ref[pl.ds(c, rows, stride=LANE_CHUNKS), :] for c in range(LANE_CHUNKS)], axis=1)


def _band_attention(qpad, k, v, k_prev, v_prev, sinks_ref, bias_ref, seq_start, after_scores):
    kb = jnp.concatenate([k_prev, k], axis=0).astype(BF16)
    vt = jnp.transpose(jnp.concatenate([v_prev, v], axis=0)).astype(BF16)
    qb = qpad.astype(BF16)
    lane = lax.broadcasted_iota(I32, (1, GQA_GROUP * WINDOW), 1)
    blocks = []
    for b in range(BLOCKS_PER_TILE):
        bias = bias_ref[jnp.where(seq_start, 1, 0)] if b == 0 else bias_ref[0]
        keys = kb[b * WINDOW:(b + 2) * WINDOW, :]
        pieces = []
        for h in range(N_KV_HEADS):
            qh = jnp.concatenate(
                [qb[b * WINDOW:(b + 1) * WINDOW, (h * GQA_GROUP + j) * LANES:(h * GQA_GROUP + j + 1) * LANES]
                 for j in range(GQA_GROUP)], axis=0)
            st = lax.dot_general(keys, qh, (((1,), (1,)), ((), ())), preferred_element_type=F32) + bias
            after_scores(b * N_KV_HEADS + h)
            sink = jnp.zeros((1, GQA_GROUP * WINDOW), F32)
            for j in range(GQA_GROUP):
                sink = jnp.where(lane // WINDOW == j, sinks_ref[h * GQA_GROUP + j], sink)
            m = jnp.maximum(jnp.max(st, axis=0, keepdims=True), sink)
            e = jnp.exp(st - m)
            inv = 1.0 / (jnp.sum(e, axis=0, keepdims=True) + jnp.exp(sink - m))
            ot = _bdot(vt[h * HEAD_DIM:(h + 1) * HEAD_DIM, b * WINDOW:(b + 2) * WINDOW], (e * inv).astype(BF16))
            pieces.extend(ot[:, j * WINDOW:(j + 1) * WINDOW] for j in range(GQA_GROUP))
        blocks.append(jnp.transpose(jnp.concatenate(pieces, axis=0)))
    return jnp.concatenate(blocks, axis=0)


def _band_bias():
    kj = lax.broadcasted_iota(I32, (2, 2 * WINDOW, GQA_GROUP * WINDOW), 1)
    qi = lax.broadcasted_iota(I32, (2, 2 * WINDOW, GQA_GROUP * WINDOW), 2) % WINDOW
    lo = lax.broadcasted_iota(I32, (2, 2 * WINDOW, GQA_GROUP * WINDOW), 0) * WINDOW
    valid = (kj > qi) & (kj <= qi + WINDOW) & (kj >= lo)
    return jnp.where(valid, 0.0, -jnp.inf).astype(F32)


def _prompt_front_kernel(x_ref, cos_ref, sin_ref, gmix_ref, win_ref, lng_ref, lnb_ref, ws_ref, bsf_ref,
                         sinks_ref, bias_ref, wpa_ref, wpb_ref, wo_ref, gffn_ref, wr3_ref, br_ref, wgu32_ref, wdn32_ref,
                         x1_ref, h2_ref, logits_ref, kwin_ref, vwin_ref, wgu16_ref, wdn16_ref,
                         kprev_ref, vprev_ref, gu_in, dn_in, gu_out, dn_out, cast_sem):
    i = pl.program_id(0)
    seq_start = (i % TILES_PER_SEQ) == 0

    def cast_rows(c, j):
        part = CAST_ROWS // CAST_SPLIT
        return pl.ds(pl.multiple_of(c * CAST_ROWS + j * part, part), part), pl.ds(j * part, part)

    def cast_in(c, slot):
        cps = []
        for j in range(CAST_SPLIT):
            hbm, loc = cast_rows(c, j)
            cps.append(pltpu.make_async_copy(wgu32_ref.at[hbm], gu_in.at[slot, loc], cast_sem.at[0, slot]))
            cps.append(pltpu.make_async_copy(wdn32_ref.at[hbm], dn_in.at[slot, loc], cast_sem.at[1, slot]))
        return cps

    def cast_out(c, slot):
        cps = []
        for j in range(CAST_SPLIT):
            hbm, loc = cast_rows(c, j)
            cps.append(pltpu.make_async_copy(gu_out.at[slot, loc], wgu16_ref.at[hbm], cast_sem.at[2, slot]))
            cps.append(pltpu.make_async_copy(dn_out.at[slot, loc], wdn16_ref.at[hbm], cast_sem.at[3, slot]))
        return cps

    slot = i & 1

    @pl.when(i == 0)
    def _():
        for cp in cast_in(0, 0):
            cp.start()

    @pl.when(i + 1 < FRONT_STEPS)
    def _():
        for cp in cast_in(i + 1, 1 - slot):
            cp.start()

    for cp in cast_in(i, slot):
        cp.wait()

    @pl.when(i >= 2)
    def _():
        for cp in cast_out(i - 2, slot):
            cp.wait()


    @pl.when(seq_start)
    def _():
        kprev_ref[...] = jnp.zeros_like(kprev_ref)
        vprev_ref[...] = jnp.zeros_like(vprev_ref)

    x = x_ref[...]
    cast_done = [0]

    def cast_slice():
        i = cast_done[0]
        r = pl.ds(CAST_CUTS[i], CAST_CUTS[i + 1] - CAST_CUTS[i])
        gu_out[slot, r, :] = gu_in[slot, r, :].astype(BF16)
        dn_out[slot, r, :] = dn_in[slot, r, :].astype(BF16)
        cast_done[0] += 1

    u, va, q, k, v, hb = _in_projection(
        x, gmix_ref, win_ref, lng_ref, lnb_ref, cos_ref[...], sin_ref[...], cast_slice)
    assert cast_done[0] == IN_PROJ_DOTS
    for cp in cast_out(i, slot):
        cp.start()

    units = BLOCKS_PER_TILE * N_KV_HEADS
    preacts = []
    att = _band_attention(q, k, v, kprev_ref[...], vprev_ref[...], sinks_ref, bias_ref, seq_start,
                          lambda unit: preacts.append(_gate_preact(hb, win_ref, unit, units)))
    gate_a, gate_b = _gates(preacts)

    k_last, v_last = k[TM - WINDOW:], v[TM - WINDOW:]
    kprev_ref[...] = k_last
    vprev_ref[...] = v_last
    kwin_ref[0] = k_last
    vwin_ref[0] = v_last

    vab = va.astype(BF16)
    zc = jnp.concatenate(
        [jnp.concatenate(
            [_bdot(ws_ref[g], vab[b * CHUNK:(b + 1) * CHUNK, g * A_GROUP_DIM:(g + 1) * A_GROUP_DIM])
             for g in range(A_GROUPS)], axis=1) + bsf_ref[...]
         for b in range(BLOCKS_PER_TILE)], axis=0)

    x1, h2, logits = _merge_and_route(x, u * zc, att, gate_a, gate_b,
                                      wpa_ref, wpb_ref, wo_ref, gffn_ref, wr3_ref, br_ref)
    _store_token_major(x1_ref, x1)
    _store_token_major(h2_ref, h2)
    logits_ref[...] = logits

    @pl.when(i == FRONT_STEPS - 1)
    def _():
        for cp in cast_out(i - 1, 1 - slot) + cast_out(i, slot):
            cp.wait()


def _full(shape):
    return pl.BlockSpec(shape, lambda i: (0,) * len(shape))


def _prompt_front(x, cos, sin, gmix, win, lng, lnb, ws, bsf, sinks, bias, wpa, wpb, wo, gffn, wrt, br, wgu32, wdn32):
    n = x.shape[0]
    assert n == N_PROMPT
    grid = (FRONT_STEPS,)
    anyspec = pl.BlockSpec(memory_space=pl.ANY)
    in_specs = [
        pl.BlockSpec((TM, D_MODEL), lambda i: (i, 0)),
        pl.BlockSpec((TM, LANES), lambda i: (i % TILES_PER_SEQ, 0)),
        pl.BlockSpec((TM, LANES), lambda i: (i % TILES_PER_SEQ, 0)),
        _full((1, D_MODEL)),
        _full((D_MODEL, IN_COLS)),
        _full((1, A_WIDTH)),
        _full((1, A_WIDTH)),
        _full((A_GROUPS, CHUNK, CHUNK)),
        _full((CHUNK, A_WIDTH)),
        pl.BlockSpec(memory_space=pltpu.SMEM),
        _full((2, 2 * WINDOW, GQA_GROUP * WINDOW)),
        _full((A_WIDTH, D_MODEL)),
        _full((Q_WIDTH, D_MODEL)),
        _full((D_MODEL, D_MODEL)),
        _full((1, D_MODEL)),
        _full((3 * D_MODEL, LANES)),
        _full((N_EXPERTS, 1)),
        anyspec,
        anyspec,
    ]
    out_shape = [
        jax.ShapeDtypeStruct((n * LANE_CHUNKS, LANES), F32),
        jax.ShapeDtypeStruct((n * LANE_CHUNKS, LANES), F32),
        jax.ShapeDtypeStruct((N_EXPERTS, n), F32),
        jax.ShapeDtypeStruct((n // SEQ, WINDOW, KV_WIDTH), F32),
        jax.ShapeDtypeStruct((n // SEQ, WINDOW, KV_WIDTH), F32),
        jax.ShapeDtypeStruct(wgu32.shape, BF16),
        jax.ShapeDtypeStruct(wdn32.shape, BF16),
    ]
    out_specs = [
        pl.BlockSpec((TM * LANE_CHUNKS, LANES), lambda i: (i, 0)),
        pl.BlockSpec((TM * LANE_CHUNKS, LANES), lambda i: (i, 0)),
        pl.BlockSpec((N_EXPERTS, TM), lambda i: (0, i)),
        pl.BlockSpec((1, WINDOW, KV_WIDTH), lambda i: (i // TILES_PER_SEQ, 0, 0)),
        pl.BlockSpec((1, WINDOW, KV_WIDTH), lambda i: (i // TILES_PER_SEQ, 0, 0)),
        anyspec,
        anyspec,
    ]
    scratch = [
        pltpu.VMEM((WINDOW, KV_WIDTH), F32),
        pltpu.VMEM((WINDOW, KV_WIDTH), F32),
        pltpu.VMEM((2, CAST_ROWS, 2 * D_EXPERT), F32),
        pltpu.VMEM((2, CAST_ROWS, D_MODEL), F32),
        pltpu.VMEM((2, CAST_ROWS, 2 * D_EXPERT), BF16),
        pltpu.VMEM((2, CAST_ROWS, D_MODEL), BF16),
        pltpu.SemaphoreType.DMA((4, 2)),
    ]
    return pl.pallas_call(
        _prompt_front_kernel,
        grid=grid,
        in_specs=in_specs,
        out_specs=out_specs,
        out_shape=out_shape,
        scratch_shapes=scratch,
        compiler_params=pltpu.CompilerParams(dimension_semantics=("arbitrary",), vmem_limit_bytes=VMEM_LIMIT),
        name="prompt_front",
    )(x, cos, sin, gmix, win, lng, lnb, ws, bsf, sinks, bias, wpa, wpb, wo, gffn, wrt, br, wgu32, wdn32)


SAMPLE_STEP = 16
SAMPLE_STEPS = DEC_BATCH // SAMPLE_STEP


def _sample_kernel(x_ref, cos_ref, sin_ref, gmix_ref, win_ref, lng_ref, lnb_ref, wdiag_ref, bs0_ref, sinks_ref,
                   kc_ref, vc_ref, wpa_ref, wpb_ref, wo_ref, gffn_ref, wr3_ref, br_ref,
                   x1_ref, h2_ref, logits_ref, kwin_ref, vwin_ref, va_ref,
                   q_s, k_s, v_s, yain_s, ga_s, gb_s, att_s):
    i = pl.program_id(0)

    @pl.when(i == 0)
    def _():
        x = x_ref[...]
        cos = jnp.broadcast_to(cos_ref[...], (DEC_BATCH, LANES))
        sin = jnp.broadcast_to(sin_ref[...], (DEC_BATCH, LANES))
        u, va, q, k, v, hb = _in_projection(x, gmix_ref, win_ref, lng_ref, lnb_ref, cos, sin)
        gate_a, gate_b = _gates([_gate_preact(hb, win_ref, 0, 1)])
        va_ref[...] = va
        z = wdiag_ref[...].astype(F32) * va.astype(BF16).astype(F32) + bs0_ref[...]
        yain_s[...] = u * z
        q_s[...] = q
        k_s[...] = k
        v_s[...] = v
        ga_s[...] = gate_a
        gb_s[...] = gate_b

    r0 = pl.multiple_of(i * SAMPLE_STEP, SAMPLE_STEP)
    kwin = jnp.concatenate([kc_ref[:, 1:, :], k_s[pl.ds(r0, SAMPLE_STEP), :][:, None, :]], axis=1)
    vwin = jnp.concatenate([vc_ref[:, 1:, :], v_s[pl.ds(r0, SAMPLE_STEP), :][:, None, :]], axis=1)
    kwin_ref[...] = kwin
    vwin_ref[...] = vwin

    q16 = q_s[pl.ds(r0, SAMPLE_STEP), :]
    lane = lax.broadcasted_iota(I32, (SAMPLE_STEP, LANES), 1)
    heads = [q16[:, hq * LANES:(hq + 1) * LANES] for hq in range(N_HEADS)]
    qpad = pltpu.einshape("hbd->bhd", jnp.stack(heads, axis=0)).astype(BF16)
    s = jnp.einsum("bhd,bkd->bhk", qpad, kwin.astype(BF16), preferred_element_type=F32)
    hid = lax.broadcasted_iota(I32, (1, N_HEADS, 1), 1)
    sink = jnp.zeros((1, N_HEADS, 1), F32)
    for hq in range(N_HEADS):
        sink = jnp.where(hid == hq, sinks_ref[hq], sink)
    m = jnp.maximum(jnp.max(s, axis=-1, keepdims=True), sink)
    e = jnp.exp(s - m)
    inv = 1.0 / (jnp.sum(e, axis=-1, keepdims=True) + jnp.exp(sink - m))
    o = jnp.einsum("bhk,bkd->bhd", (e * inv).astype(BF16), vwin.astype(BF16), preferred_element_type=F32)
    o = pltpu.einshape("bhd->hbd", o)
    chunks = []
    for c in range(N_HEADS // 2):
        parts = []
        for p in range(2):
            hq = 2 * c + p
            oh = o[hq]
            if p != hq // GQA_GROUP:
                oh = pltpu.roll(oh, HEAD_DIM, 1)
            parts.append(oh)
        chunks.append(jnp.where(lane < HEAD_DIM, parts[0], parts[1]))
    att_s[pl.ds(r0, SAMPLE_STEP), :] = jnp.concatenate(chunks, axis=1)

    @pl.when(i == SAMPLE_STEPS - 1)
    def _():
        x1, h2, logits = _merge_and_route(
            x_ref[...], yain_s[...], att_s[...], ga_s[...], gb_s[...],
            wpa_ref, wpb_ref, wo_ref, gffn_ref, wr3_ref, br_ref)
        _store_token_major(x1_ref, x1)
        _store_token_major(h2_ref, h2)
        logits_ref[...] = logits


def _sample_front(x, cos, sin, gmix, win, lng, lnb, wdiag, bs0, sinks, kc, vc, wpa, wpb, wo, gffn, wrt, br):
    n = DEC_BATCH
    cache_spec = pl.BlockSpec((SAMPLE_STEP, WINDOW, KV_WIDTH), lambda i: (i, 0, 0))
    in_specs = [
        _full((n, D_MODEL)),
        _full((1, LANES)),
        _full((1, LANES)),
        _full((1, D_MODEL)),
        _full((D_MODEL, IN_COLS)),
        _full((1, A_WIDTH)),
        _full((1, A_WIDTH)),
        _full((1, A_WIDTH)),
        _full((1, A_WIDTH)),
        pl.BlockSpec(memory_space=pltpu.SMEM),
        cache_spec,
        cache_spec,
        _full((A_WIDTH, D_MODEL)),
        _full((Q_WIDTH, D_MODEL)),
        _full((D_MODEL, D_MODEL)),
        _full((1, D_MODEL)),
        _full((3 * D_MODEL, LANES)),
        _full((N_EXPERTS, 1)),
    ]
    out_shape = [
        jax.ShapeDtypeStruct((n * LANE_CHUNKS, LANES), F32),
        jax.ShapeDtypeStruct((n * LANE_CHUNKS, LANES), F32),
        jax.ShapeDtypeStruct((N_EXPERTS, n), F32),
        jax.ShapeDtypeStruct((n, WINDOW, KV_WIDTH), F32),
        jax.ShapeDtypeStruct((n, WINDOW, KV_WIDTH), F32),
        jax.ShapeDtypeStruct((n, A_WIDTH), F32),
    ]
    out_specs = [
        _full((n * LANE_CHUNKS, LANES)),
        _full((n * LANE_CHUNKS, LANES)),
        _full((N_EXPERTS, n)),
        cache_spec,
        cache_spec,
        _full((n, A_WIDTH)),
    ]
    scratch = [
        pltpu.VMEM((n, QPAD_WIDTH), F32), pltpu.VMEM((n, KV_WIDTH), F32), pltpu.VMEM((n, KV_WIDTH), F32),
        pltpu.VMEM((n, A_WIDTH), F32), pltpu.VMEM((n, D_MODEL), F32), pltpu.VMEM((n, D_MODEL), F32),
        pltpu.VMEM((n, Q_WIDTH), F32),
    ]
    return pl.pallas_call(
        _sample_kernel,
        grid=(SAMPLE_STEPS,),
        in_specs=in_specs,
        out_specs=out_specs,
        out_shape=out_shape,
        scratch_shapes=scratch,
        compiler_params=pltpu.CompilerParams(dimension_semantics=("arbitrary",), vmem_limit_bytes=VMEM_LIMIT),
        name="sample_front",
    )(x, cos, sin, gmix, win, lng, lnb, wdiag, bs0, sinks, kc, vc, wpa, wpb, wo, gffn, wrt, br)


def _route_plan_kernel(lp_ref, ls_ref, dest_ref, wts_ref, off_ref):
    g = pl.program_id(0)
    topi, topw = _top4_softmax(jnp.concatenate([lp_ref[...], ls_ref[...]], axis=1))
    slot = lax.broadcasted_iota(I32, (TOP_K, GROUP_SLOTS), 1)
    eall = jnp.where(jnp.logical_or(slot < GROUP_PROMPT, g == N_GROUPS - 1), topi, N_EXPERTS)
    wts_ref[:, 0:GROUP_SLOTS] = topw
    wts_ref[:, GROUP_SLOTS:] = jnp.zeros((TOP_K, K_STRIDE - GROUP_SLOTS), F32)
    dest_ref[:, GROUP_SLOTS:] = jnp.zeros((TOP_K, K_STRIDE - GROUP_SLOTS), I32)
    eid = lax.broadcasted_iota(I32, (N_EXPERTS, GROUP_SLOTS), 0)
    onehots = [eall[k:k + 1, :] == eid for k in range(TOP_K)]
    count = jnp.zeros((N_EXPERTS, GROUP_SLOTS), F32)
    for oh in onehots:
        count = count + oh.astype(F32)
    total = jnp.broadcast_to(jnp.sum(count, axis=1, keepdims=True), (N_EXPERTS, LANES))
    padded = total + (MOE_ROWS - 1)
    nblk = jnp.floor(padded * (1.0 / MOE_ROWS))
    rem = padded - nblk * MOE_ROWS
    nblk = jnp.where(rem >= MOE_ROWS, nblk + 1.0, jnp.where(rem < 0.0, nblk - 1.0, nblk))
    r = lax.broadcasted_iota(I32, (N_EXPERTS, N_EXPERTS), 0)
    c = lax.broadcasted_iota(I32, (N_EXPERTS, N_EXPERTS), 1)
    first_blk = lax.dot_general((c < r).astype(F32), nblk, (((1,), (0,)), ((), ())),
                                precision=lax.Precision.HIGHEST, preferred_element_type=F32)
    start = (first_blk + 1.0) * MOE_ROWS
    lane = lax.broadcasted_iota(I32, (N_EXPERTS, LANES), 1)
    info = jnp.where(lane == 0, start, jnp.where(lane == 1, start + total, jnp.where(lane == 2, nblk, first_blk)))
    off_ref[...] = info.astype(I32)
    ti = lax.broadcasted_iota(I32, (LANES, LANES), 0)
    tj = lax.broadcasted_iota(I32, (LANES, LANES), 1)
    before = (ti < tj).astype(BF16)
    ones = jnp.ones((LANES, LANES), BF16)
    running = start
    for t in range(SLOT_TILES):
        sl = slice(t * LANES, (t + 1) * LANES)
        cb = count[:, sl].astype(BF16)
        pos = running + _bdot(cb, before)
        rows = [jnp.sum(jnp.where(oh[:, sl], pos, 0.0), axis=0, keepdims=True) for oh in onehots]
        dest_ref[:, sl] = jnp.concatenate(rows, axis=0).astype(I32)
        running = running + _bdot(cb, ones)


def _route_plan(logits_p, logits_s):
    in_specs = [
        pl.BlockSpec((N_EXPERTS, GROUP_PROMPT), lambda g: (0, g)),
        pl.BlockSpec((N_EXPERTS, DEC_BATCH), lambda g: (0, 0)),
    ]
    out_shape = [
        jax.ShapeDtypeStruct((N_GROUPS, TOP_K, K_STRIDE), I32),
        jax.ShapeDtypeStruct((N_GROUPS, TOP_K, K_STRIDE), F32),
        jax.ShapeDtypeStruct((N_GROUPS, N_EXPERTS, LANES), I32),
    ]
    out_specs = [
        pl.BlockSpec((None, TOP_K, K_STRIDE), lambda g: (g, 0, 0)),
        pl.BlockSpec((None, TOP_K, K_STRIDE), lambda g: (g, 0, 0)),
        pl.BlockSpec((None, N_EXPERTS, LANES), lambda g: (g, 0, 0)),
    ]
    return pl.pallas_call(
        _route_plan_kernel,
        grid=(N_GROUPS,),
        in_specs=in_specs,
        out_specs=out_specs,
        out_shape=out_shape,
        compiler_params=pltpu.CompilerParams(dimension_semantics=("arbitrary",)),
        name="route_plan",
    )(logits_p, logits_s)


GROUP_ROWS = GROUP_PROMPT * LANE_CHUNKS
SAMPLE_ROWS = DEC_BATCH * LANE_CHUNKS
TRASH_SLOT = GROUP_SLOTS
BUF_ROWS = (GROUP_SLOTS + 1) * LANE_CHUNKS
SCATTER_BATCH = 8
DMA_SPLIT = 8
FFN_COLS = 256
FFN_DOTS = (2 * D_EXPERT + D_MODEL) // FFN_COLS


def _moe_kernel(off_ref, desth_ref, wtsh_ref, h2p_ref, h2s_ref, x1p_ref, x1s_ref, wgu_ref, bgu_ref, wdn_ref, bdn_ref,
                x2p_ref, x2s_ref,
                h2buf, acc, wgubuf, bgubuf, wdnbuf, bdnbuf, xs0, xs1, ys0, ys1,
                dest_ref, wts_ref, src_ref, seg_expert, seg_first, blk_seg, act_sem, w_sem):
    g = pl.program_id(0)
    last = g == N_GROUPS - 1
    row0 = pl.multiple_of(g * GROUP_ROWS, GROUP_ROWS)

    def prompt_copies():
        cps = []
        for j in range(DMA_SPLIT):
            src = pl.ds(row0 + j * (GROUP_ROWS // DMA_SPLIT), GROUP_ROWS // DMA_SPLIT)
            dst = pl.ds(j * (GROUP_ROWS // DMA_SPLIT), GROUP_ROWS // DMA_SPLIT)
            cps.append(pltpu.make_async_copy(h2p_ref.at[src], h2buf.at[dst], act_sem.at[0]))
            cps.append(pltpu.make_async_copy(x1p_ref.at[src], acc.at[dst], act_sem.at[1]))
        return cps

    def sample_copies():
        return (pltpu.make_async_copy(h2s_ref, h2buf.at[pl.ds(GROUP_ROWS, SAMPLE_ROWS)], act_sem.at[2]),
                pltpu.make_async_copy(x1s_ref, acc.at[pl.ds(GROUP_ROWS, SAMPLE_ROWS)], act_sem.at[3]))

    def weight_copies(e, slot):
        cps = [pltpu.make_async_copy(bgu_ref.at[e], bgubuf.at[slot], w_sem.at[1, slot]),
               pltpu.make_async_copy(bdn_ref.at[e], bdnbuf.at[slot], w_sem.at[3, slot])]
        for j in range(DMA_SPLIT):
            rg = pl.ds(j * (D_MODEL // DMA_SPLIT), D_MODEL // DMA_SPLIT)
            rd = pl.ds(j * (D_EXPERT // DMA_SPLIT), D_EXPERT // DMA_SPLIT)
            cps.append(pltpu.make_async_copy(wgu_ref.at[e, rg], wgubuf.at[slot, rg], w_sem.at[0, slot]))
            cps.append(pltpu.make_async_copy(wdn_ref.at[e, rd], wdnbuf.at[slot, rd], w_sem.at[2, slot]))
        return cps

    def output_copies():
        return [pltpu.make_async_copy(
            acc.at[pl.ds(j * (GROUP_ROWS // DMA_SPLIT), GROUP_ROWS // DMA_SPLIT)],
            x2p_ref.at[pl.ds(row0 + j * (GROUP_ROWS // DMA_SPLIT), GROUP_ROWS // DMA_SPLIT)], act_sem.at[0])
            for j in range(DMA_SPLIT)]

    tab0 = pl.multiple_of(g * (TOP_K * K_STRIDE), TOP_K * K_STRIDE)
    table_copies = (
        pltpu.make_async_copy(desth_ref.at[pl.ds(tab0, TOP_K * K_STRIDE)], dest_ref, act_sem.at[4]),
        pltpu.make_async_copy(wtsh_ref.at[pl.ds(tab0, TOP_K * K_STRIDE)], wts_ref, act_sem.at[5]))
    for cp in table_copies:
        cp.start()

    for cp in prompt_copies():
        cp.start()

    @pl.when(last)
    def _():
        for cp in sample_copies():
            cp.start()

    trash = pl.ds(TRASH_SLOT * LANE_CHUNKS, LANE_CHUNKS)
    h2buf[trash, :] = jnp.zeros((LANE_CHUNKS, LANES), F32)
    acc[trash, :] = jnp.zeros((LANE_CHUNKS, LANES), F32)
    ys1[...] = jnp.zeros_like(ys1)

    def pad_block(pos0):
        def body(j, carry):
            for d in range(SUBLANES):
                src_ref[pos0 + j * SUBLANES + d] = TRASH_SLOT
            return carry
        lax.fori_loop(0, MOE_ROWS // SUBLANES, body, 0)

    def scan_expert(e, carry):
        nseg, nblocks = carry
        nblk = off_ref[e, 2]
        first = off_ref[e, 3]

        @pl.when(nblk > 0)
        def _():
            seg_expert[nseg] = e
            seg_first[nseg] = first
            pad_block(off_ref[e, 0] + (nblk - 1) * MOE_ROWS)

            def mark(b, c2):
                blk_seg[first + b] = nseg
                return c2
            lax.fori_loop(0, nblk, mark, 0)

        return nseg + jnp.where(nblk > 0, 1, 0), nblocks + nblk

    nseg, nblocks = lax.fori_loop(0, N_EXPERTS, scan_expert, (jnp.int32(0), jnp.int32(0)))
    pad_block(0)
    pad_block((nblocks + 1) * MOE_ROWS)
    pad_block((nblocks + 2) * MOE_ROWS)
    blk_seg[nblocks] = nseg - 1
    blk_seg[nblocks + 1] = nseg - 1

    for cp in weight_copies(seg_expert[0], 0):
        cp.start()

    for cp in table_copies:
        cp.wait()

    nvalid = jnp.where(last, GROUP_SLOTS, GROUP_PROMPT)
    for k in range(TOP_K):
        def fill(j, carry, k=k):
            c0 = k * K_STRIDE + j * SUBLANES
            for d in range(SUBLANES):
                src_ref[dest_ref[c0 + d]] = c0 + d
            return carry
        lax.fori_loop(0, nvalid // SUBLANES, fill, 0)

    for cp in prompt_copies():
        cp.wait()

    @pl.when(last)
    def _():
        for cp in sample_copies():
            cp.wait()

    def token_rows(code):
        slot_id = code & (K_STRIDE - 1)
        return pl.ds(pl.multiple_of(slot_id * LANE_CHUNKS, LANE_CHUNKS), LANE_CHUNKS)

    def gather(b, xs, lo=0, hi=MOE_ROWS):
        base = (b + 1) * MOE_ROWS
        for m in range(lo, hi):
            xs[pl.ds(m, LANE_CHUNKS, stride=XS_STRIDE), :] = h2buf[token_rows(src_ref[base + m]), :]

    def scatter_add(b, ys, lo=0, hi=MOE_ROWS):
        base = (b + 1) * MOE_ROWS
        for m0 in range(lo, hi, SCATTER_BATCH):
            pending = []
            for m in range(m0, m0 + SCATTER_BATCH):
                code = src_ref[base + m]
                rows = token_rows(code)
                pending.append((rows, acc[rows, :] + wts_ref[code] * ys[pl.ds(m, LANE_CHUNKS, stride=XS_STRIDE), :]))
            for rows, val in pending:
                acc[rows, :] = val

    gather_cuts = [round(i * MOE_ROWS / FFN_DOTS) for i in range(FFN_DOTS + 1)]
    scatter_cuts = [SCATTER_BATCH * round(i * (MOE_ROWS // SCATTER_BATCH) / FFN_DOTS) for i in range(FFN_DOTS + 1)]

    def step(b, xs_cur, xs_next, ys_cur, ys_prev):
        seg = blk_seg[b]
        slot = seg & 1

        @pl.when(jnp.logical_and(b == seg_first[seg], b < nblocks))
        def _():
            for cp in weight_copies(seg_expert[seg], slot):
                cp.wait()

            @pl.when(seg + 1 < nseg)
            def _():
                for cp in weight_copies(seg_expert[seg + 1], 1 - slot):
                    cp.start()

        done = [0]

        def row_traffic():
            i = done[0]
            gather(b + 1, xs_next, gather_cuts[i], gather_cuts[i + 1])
            scatter_add(b - 1, ys_prev, scatter_cuts[i], scatter_cuts[i + 1])
            done[0] = i + 1

        x = jnp.concatenate(
            [xs_cur[c * XS_STRIDE:c * XS_STRIDE + MOE_ROWS, :] for c in range(LANE_CHUNKS)], axis=1).astype(BF16)
        acts = []
        for c in range(D_EXPERT // FFN_COLS):
            gc = pl.ds(c * FFN_COLS, FFN_COLS)
            uc = pl.ds(D_EXPERT + c * FFN_COLS, FFN_COLS)
            g = _bdot(x, wgubuf[slot, :, gc]) + bgubuf[slot, :, gc]
            row_traffic()
            u = _bdot(x, wgubuf[slot, :, uc]) + bgubuf[slot, :, uc]
            row_traffic()
            gl = jnp.minimum(g, SWIGLU_LIMIT)
            ul = jnp.clip(u, -SWIGLU_LIMIT, SWIGLU_LIMIT)
            acts.append((ul + 1.0) * (gl * jax.nn.sigmoid(SWIGLU_ALPHA * gl)))
        a = jnp.concatenate(acts, axis=1).astype(BF16)
        for c in range(D_MODEL // FFN_COLS):
            oc = pl.ds(c * FFN_COLS, FFN_COLS)
            y = _bdot(a, wdnbuf[slot, :, oc]) + bdnbuf[slot, :, oc]
            for j in range(FFN_COLS // LANES):
                lc = c * (FFN_COLS // LANES) + j
                ys_cur[lc * XS_STRIDE:lc * XS_STRIDE + MOE_ROWS, :] = y[:, j * LANES:(j + 1) * LANES]
            row_traffic()

    gather(0, xs0)
    npairs = (nblocks + 1) // 2

    def pair(t, carry):
        step(2 * t, xs0, xs1, ys0, ys1)
        step(2 * t + 1, xs1, xs0, ys1, ys0)
        return carry

    lax.fori_loop(0, npairs, pair, 0)
    scatter_add(2 * npairs - 1, ys1)

    for cp in output_copies():
        cp.start()

    @pl.when(last)
    def _():
        out_s = pltpu.make_async_copy(acc.at[pl.ds(GROUP_ROWS, SAMPLE_ROWS)], x2s_ref, act_sem.at[2])
        out_s.start()
        out_s.wait()

    for cp in output_copies():
        cp.wait()


def _moe(dest, wts, off, h2p, h2s, x1p, x1s, wgu, bgu, wdn, bdn):
    anyspec = pl.BlockSpec(memory_space=pl.ANY)
    dest = dest.reshape(N_GROUPS * TOP_K * K_STRIDE)
    wts = wts.reshape(N_GROUPS * TOP_K * K_STRIDE)
    in_specs = [
        pl.BlockSpec((None, N_EXPERTS, LANES), lambda g: (g, 0, 0), memory_space=pltpu.SMEM),
        anyspec, anyspec, anyspec, anyspec, anyspec, anyspec, anyspec, anyspec, anyspec, anyspec,
    ]
    scratch = [
        pltpu.VMEM((BUF_ROWS, LANES), F32),
        pltpu.VMEM((BUF_ROWS, LANES), F32),
        pltpu.VMEM((2, D_MODEL, 2 * D_EXPERT), BF16),
        pltpu.VMEM((2, 1, 2 * D_EXPERT), F32),
        pltpu.VMEM((2, D_EXPERT, D_MODEL), BF16),
        pltpu.VMEM((2, 1, D_MODEL), F32),
        pltpu.VMEM((LANE_CHUNKS * XS_STRIDE, LANES), F32),
        pltpu.VMEM((LANE_CHUNKS * XS_STRIDE, LANES), F32),
        pltpu.VMEM((LANE_CHUNKS * XS_STRIDE, LANES), F32),
        pltpu.VMEM((LANE_CHUNKS * XS_STRIDE, LANES), F32),
        pltpu.SMEM((TOP_K * K_STRIDE,), I32),
        pltpu.SMEM((TOP_K * K_STRIDE,), F32),
        pltpu.SMEM((POS_TABLE,), I32),
        pltpu.SMEM((N_EXPERTS,), I32),
        pltpu.SMEM((N_EXPERTS,), I32),
        pltpu.SMEM((LANES,), I32),
        pltpu.SemaphoreType.DMA((6,)),
        pltpu.SemaphoreType.DMA((4, 2)),
    ]
    return pl.pallas_call(
        _moe_kernel,
        grid=(N_GROUPS,),
        in_specs=in_specs,
        out_specs=[anyspec, anyspec],
        out_shape=[jax.ShapeDtypeStruct(x1p.shape, F32), jax.ShapeDtypeStruct(x1s.shape, F32)],
        scratch_shapes=scratch,
        compiler_params=pltpu.CompilerParams(dimension_semantics=("arbitrary",), vmem_limit_bytes=VMEM_LIMIT),
        name="moe",
    )(off, dest, wts, h2p, h2s, x1p, x1s, wgu, bgu, wdn, bdn)


def _ple_final_kernel(x2_ref, ple_ref, wple_ref, gple_ref, wpg_ref, gfin_ref, y_ref):
    rows = y_ref.shape[0]
    x2 = _load_token_major(x2_ref, rows)
    e = _rmsnorm(_bdot(ple_ref[...].astype(BF16), wple_ref[...]), gple_ref[...])
    x3 = x2 + jax.nn.sigmoid(_bdot(x2.astype(BF16), wpg_ref[...])) * e
    y_ref[...] = _rmsnorm(x3, gfin_ref[...])


def _ple_final(x2_tm, ple, wple, gple, wpg, gfin, tile):
    n = ple.shape[0]
    return pl.pallas_call(
        _ple_final_kernel,
        grid=(n // tile,),
        in_specs=[
            pl.BlockSpec((tile * LANE_CHUNKS, LANES), lambda i: (i, 0)),
            pl.BlockSpec((tile, PLE_DIM), lambda i: (i, 0)),
            _full((PLE_DIM, D_MODEL)),
            _full((1, D_MODEL)),
            _full((D_MODEL, D_MODEL)),
            _full((1, D_MODEL)),
        ],
        out_specs=pl.BlockSpec((tile, D_MODEL), lambda i: (i, 0)),
        out_shape=jax.ShapeDtypeStruct((n, D_MODEL), F32),
        compiler_params=pltpu.CompilerParams(dimension_semantics=("arbitrary",), vmem_limit_bytes=VMEM_LIMIT),
        name="ple_final",
    )(x2_tm, ple, wple, gple, wpg, gfin)


def _rope_tables(pos):
    half = HEAD_DIM // 2
    inv = ROPE_THETA ** (-jnp.arange(half, dtype=F32) / half)
    ang = pos.astype(F32)[:, None] * inv[None, :]
    cos, sin = jnp.cos(ang), jnp.sin(ang)
    cos2 = jnp.concatenate([cos, cos, cos, cos], axis=1)
    sin2 = jnp.concatenate([-sin, sin, -sin, sin], axis=1)
    return cos2, sin2


def _layout_w_in(w_in):
    o_q = 2 * A_WIDTH
    wq = w_in[:, o_q:o_q + Q_WIDTH].reshape(D_MODEL, N_HEADS, HEAD_DIM) * (HEAD_DIM ** -0.5)
    kv_head = (jnp.arange(N_HEADS) // GQA_GROUP)[None, :, None]
    wq_pad = jnp.concatenate([jnp.where(kv_head == h, wq, 0.0) for h in range(N_KV_HEADS)], axis=-1)
    return jnp.concatenate([w_in[:, :o_q], wq_pad.reshape(D_MODEL, QPAD_WIDTH), w_in[:, o_q + Q_WIDTH:]], axis=1)


def _router_passes(w_router):
    hi = w_router.astype(BF16)
    lo = (w_router - hi.astype(F32)).astype(BF16)
    w3 = jnp.concatenate([hi, lo, hi], axis=0)
    return jnp.pad(w3, ((0, 0), (0, LANES - N_EXPERTS)))


def _prep_weights(g_mix, w_in, a_ln_g, a_ln_b, a_ws, a_bs, w_pa, w_pb, w_o, g_ffn, w_router, b_router):
    causal = jnp.tril(jnp.ones((CHUNK, CHUNK), dtype=bool))
    return dict(
        gmix=g_mix.reshape(1, D_MODEL),
        win=_layout_w_in(w_in).astype(BF16),
        lng=a_ln_g.reshape(1, A_WIDTH),
        lnb=a_ln_b.reshape(1, A_WIDTH),
        ws=jnp.where(causal[None], a_ws, 0.0).astype(BF16),
        bsf=jnp.repeat(jnp.transpose(a_bs), A_GROUP_DIM, axis=1),
        wpa=w_pa.astype(BF16),
        wpb=w_pb.astype(BF16),
        wo=w_o.astype(BF16),
        gffn=g_ffn.reshape(1, D_MODEL),
        wrt=_router_passes(w_router),
        br=b_router.reshape(N_EXPERTS, 1),
    )


def kernel(x_prompt, x_sample, cache_win_k, cache_win_v, p_prompt, p_sample, g_mix, w_in, a_ln_g, a_ln_b, a_ws, a_bs, sinks, w_pa, w_pb, w_o, g_ffn, w_router, b_router, w_gu, b_gu, w_down, b_down, w_ple, g_ple, w_ple_gate, g_final):
    W = _prep_weights(g_mix[0], w_in[0], a_ln_g[0], a_ln_b[0], a_ws[0], a_bs[0], w_pa[0], w_pb[0], w_o[0],
                      g_ffn[0], w_router[0], b_router[0])
    cos_p, sin_p = _rope_tables(jnp.arange(SEQ, dtype=I32))
    cos_s, sin_s = _rope_tables(jnp.full((1,), PAST_LEN, I32))
    x1p, h2p, logits_p, kwin_p, vwin_p, wgu16, wdn16 = _prompt_front(
        x_prompt.reshape(N_PROMPT, D_MODEL), cos_p, sin_p, W["gmix"], W["win"], W["lng"], W["lnb"],
        W["ws"], W["bsf"], sinks[0], _band_bias(), W["wpa"], W["wpb"], W["wo"], W["gffn"], W["wrt"], W["br"],
        w_gu[0].reshape(N_EXPERTS * D_MODEL, 2 * D_EXPERT), w_down[0].reshape(N_EXPERTS * D_EXPERT, D_MODEL))

    wdiag = jnp.repeat(a_ws[0, :, 0, 0], A_GROUP_DIM)[None, :].astype(BF16)
    bs0 = jnp.repeat(a_bs[0, :, 0], A_GROUP_DIM)[None, :]
    x1s, h2s, logits_s, kwin_s, vwin_s, va_s = _sample_front(
        x_sample.reshape(DEC_BATCH, D_MODEL), cos_s, sin_s, W["gmix"], W["win"], W["lng"], W["lnb"], wdiag, bs0,
        sinks[0], cache_win_k[0].reshape(DEC_BATCH, WINDOW, KV_WIDTH), cache_win_v[0].reshape(DEC_BATCH, WINDOW, KV_WIDTH),
        W["wpa"], W["wpb"], W["wo"], W["gffn"], W["wrt"], W["br"])

    dest, wts, off = _route_plan(logits_p, logits_s)
    x2p, x2s = _moe(dest, wts, off, h2p, h2s, x1p, x1s,
                    wgu16.reshape(N_EXPERTS, D_MODEL, 2 * D_EXPERT), b_gu[0].reshape(N_EXPERTS, 1, 2 * D_EXPERT),
                    wdn16.reshape(N_EXPERTS, D_EXPERT, D_MODEL), b_down[0].reshape(N_EXPERTS, 1, D_MODEL))

    wple = w_ple[0].astype(BF16)
    gple = g_ple[0].reshape(1, D_MODEL)
    wpg = w_ple_gate[0].astype(BF16)
    gfin = g_final.reshape(1, D_MODEL)
    y_p = _ple_final(x2p, p_prompt[0].reshape(N_PROMPT, PLE_DIM), wple, gple, wpg, gfin, 2 * TM)
    y_s = _ple_final(x2s, p_sample[0].reshape(DEC_BATCH, PLE_DIM), wple, gple, wpg, gfin, DEC_BATCH)

    return (
        y_p.reshape(BATCH, SEQ, D_MODEL),
        y_s.reshape(DEC_BATCH, 1, D_MODEL),
        kwin_p.reshape(1, BATCH, WINDOW, N_KV_HEADS, HEAD_DIM),
        vwin_p.reshape(1, BATCH, WINDOW, N_KV_HEADS, HEAD_DIM),
        kwin_s.reshape(1, DEC_BATCH, WINDOW, N_KV_HEADS, HEAD_DIM),
        vwin_s.reshape(1, DEC_BATCH, WINDOW, N_KV_HEADS, HEAD_DIM),
        va_s.reshape(1, DEC_BATCH, 1, A_WIDTH),
    )
```

```python
import functools

import jax
import jax.numpy as jnp
from jax import lax
from jax.experimental import pallas as pl
from jax.experimental.pallas import tpu as pltpu

F32 = jnp.float32
BF16 = jnp.bfloat16
I32 = jnp.int32

D_MODEL = 1024
BATCH = 4
SEQ = 4096
DEC_BATCH = 128
PAST_LEN = 8192
CHUNK = 128
A_GROUPS = 4
A_GROUP_DIM = 128
A_WIDTH = A_GROUPS * A_GROUP_DIM
N_HEADS = 8
N_KV_HEADS = 2
HEAD_DIM = 64
Q_WIDTH = N_HEADS * HEAD_DIM
KV_WIDTH = N_KV_HEADS * HEAD_DIM
GQA_GROUP = N_HEADS // N_KV_HEADS
WINDOW = 128
ROPE_THETA = 10000.0
N_EXPERTS = 32
TOP_K = 4
D_EXPERT = D_MODEL
SWIGLU_ALPHA = 1.702
SWIGLU_LIMIT = 7.0
PLE_DIM = 256
RMS_EPS = 1e-5
LN_EPS = 1e-5

LANES = 128

QPAD_WIDTH = N_HEADS * LANES
O_Q = 2 * A_WIDTH
O_K = O_Q + QPAD_WIDTH
O_V = O_K + KV_WIDTH
O_GA = O_V + KV_WIDTH
O_GB = O_GA + D_MODEL
IN_COLS = O_GB + D_MODEL
SUBLANES = 8
LANE_CHUNKS = D_MODEL // LANES
VMEM_LIMIT = 56 * 1024 * 1024

N_PROMPT = BATCH * SEQ
TM = 256
TILES_PER_SEQ = SEQ // TM
BLOCKS_PER_TILE = TM // WINDOW
FRONT_STEPS = N_PROMPT // TM
CAST_ROWS = N_EXPERTS * D_MODEL // FRONT_STEPS
CAST_SPLIT = 4
IN_PROJ_DOTS = 5
BF16_ROWS = 16
CAST_CUTS = [BF16_ROWS * round(i * (CAST_ROWS // BF16_ROWS) / IN_PROJ_DOTS) for i in range(IN_PROJ_DOTS + 1)]

N_GROUPS = 4
GROUP_PROMPT = N_PROMPT // N_GROUPS
GROUP_SLOTS = GROUP_PROMPT + DEC_BATCH
GROUP_ASSIGN = GROUP_SLOTS * TOP_K
SLOT_TILES = GROUP_SLOTS // LANES
MOE_ROWS = 256
XS_STRIDE = MOE_ROWS + SUBLANES
HALF_ROWS = MOE_ROWS // 2
HALF_FLAG = 64
assert N_EXPERTS <= HALF_FLAG
SLOT_BITS = 13
K_STRIDE = 1 << SLOT_BITS
assert GROUP_SLOTS < K_STRIDE
MAX_BLOCKS = GROUP_ASSIGN // MOE_ROWS + N_EXPERTS
POS_TABLE = 1 << 15
assert (MAX_BLOCKS + 3) * MOE_ROWS <= POS_TABLE
assert MAX_BLOCKS + 2 <= LANES


def _bdot(a, b):
    return jnp.dot(a, b, preferred_element_type=F32)


def _rmsnorm(x, g):
    return x * lax.rsqrt(jnp.mean(x * x, axis=-1, keepdims=True) + RMS_EPS) * g


def _gelu(x):
    return 0.5 * x * (1.0 + lax.erf(x * (0.5 ** 0.5)))


def _group_layernorm(v, g, b):
    cols = []
    for gi in range(A_GROUPS):
        s = slice(gi * A_GROUP_DIM, (gi + 1) * A_GROUP_DIM)
        vg = v[:, s]
        mu = jnp.mean(vg, axis=-1, keepdims=True)
        d = vg - mu
        var = jnp.mean(d * d, axis=-1, keepdims=True)
        cols.append(d * lax.rsqrt(var + LN_EPS) * g[:, s] + b[:, s])
    return jnp.concatenate(cols, axis=1)


def _rope(x, cos, sin_signed):
    width = x.shape[1]
    reps = width // LANES
    cosf = jnp.concatenate([cos] * reps, axis=1) if reps > 1 else cos
    sinf = jnp.concatenate([sin_signed] * reps, axis=1) if reps > 1 else sin_signed
    half = HEAD_DIM // 2
    lane = lax.broadcasted_iota(I32, x.shape, 1)
    up = pltpu.roll(x, width - half, 1)
    down = pltpu.roll(x, half, 1)
    partner = jnp.where((lane & (HEAD_DIM - 1)) < half, up, down)
    return x * cosf + partner * sinf


def _in_projection(x, gmix_ref, win_ref, lng_ref, lnb_ref, cos, sin_signed, after_dot=lambda: None):
    hb = _rmsnorm(x, gmix_ref[...]).astype(BF16)

    def dot_cols(lo, hi):
        z = _bdot(hb, win_ref[:, lo:hi])
        after_dot()
        return z

    u = _gelu(dot_cols(0, A_WIDTH))
    va = _group_layernorm(_gelu(dot_cols(A_WIDTH, O_Q)), lng_ref[...], lnb_ref[...])
    zq = jnp.concatenate([dot_cols(O_Q, O_Q + QPAD_WIDTH // 2), dot_cols(O_Q + QPAD_WIDTH // 2, O_K)], axis=1)
    zkv = dot_cols(O_K, O_GA)
    q = _rope(zq, cos, sin_signed)
    k = _rope(zkv[:, :KV_WIDTH], cos, sin_signed)
    v = zkv[:, KV_WIDTH:]
    return u, va, q, k, v, hb


def _gate_preact(hb, win_ref, part, parts):
    width = 2 * D_MODEL // parts
    return _bdot(hb, win_ref[:, O_GA + part * width:O_GA + (part + 1) * width])


def _gates(preacts):
    zg = jnp.concatenate(preacts, axis=1)
    return jax.nn.sigmoid(zg[:, :D_MODEL]), jax.nn.sigmoid(zg[:, D_MODEL:])


def _merge_and_route(x, ya_in, att, gate_a, gate_b, wpa_ref, wpb_ref, wo_ref, gffn_ref, wr3_ref, br_ref):
    ya = _bdot(ya_in.astype(BF16), wpa_ref[...])
    yb = _bdot(att.astype(BF16), wpb_ref[...])
    mix = (gate_a * ya + gate_b * yb).astype(BF16)
    x1 = x + _bdot(mix, wo_ref[...])
    h2 = _rmsnorm(x1, gffn_ref[...])
    hi = h2.astype(BF16)
    lo = (h2 - hi.astype(F32)).astype(BF16)
    logits = _bdot(jnp.concatenate([hi, hi, lo], axis=1), wr3_ref[...])
    return x1, h2, jnp.transpose(logits)[:N_EXPERTS, :] + br_ref[...]


def _top4_softmax(logits):
    eid = lax.broadcasted_iota(I32, logits.shape, 0)
    vals, idxs = [], []
    for _ in range(TOP_K):
        m = jnp.max(logits, axis=0, keepdims=True)
        idx = jnp.min(jnp.where(logits == m, eid, N_EXPERTS), axis=0, keepdims=True)
        logits = jnp.where(eid == idx, -jnp.inf, logits)
        vals.append(m)
        idxs.append(idx)
    es = [jnp.exp(v - vals[0]) for v in vals]
    inv = 1.0 / (es[0] + es[1] + es[2] + es[3])
    return jnp.concatenate(idxs, axis=0), jnp.concatenate([e * inv for e in es], axis=0)


def _store_token_major(ref, val):
    rows = val.shape[0]
    for c in range(LANE_CHUNKS):
        ref[pl.ds(c, rows, stride=LANE_CHUNKS), :] = val[:, c * LANES:(c + 1) * LANES]


def _load_token_major(ref, rows):
    return jnp.concatenate([ref[pl.ds(c, rows, stride=LANE_CHUNKS), :] for c in range(LANE_CHUNKS)], axis=1)


def _band_attention(qpad, k, v, k_prev, v_prev, sinks_ref, bias_ref, seq_start, after_scores):
    kb = jnp.concatenate([k_prev, k], axis=0).astype(BF16)
    vt = jnp.transpose(jnp.concatenate([v_prev, v], axis=0)).astype(BF16)
    qb = qpad.astype(BF16)
    lane = lax.broadcasted_iota(I32, (1, GQA_GROUP * WINDOW), 1)
    blocks = []
    for b in range(BLOCKS_PER_TILE):
        bias = bias_ref[jnp.where(seq_start, 1, 0)] if b == 0 else bias_ref[0]
        keys = kb[b * WINDOW:(b + 2) * WINDOW, :]
        pieces = []
        for h in range(N_KV_HEADS):
            qh = jnp.concatenate(
                [qb[b * WINDOW:(b + 1) * WINDOW, (h * GQA_GROUP + j) * LANES:(h * GQA_GROUP + j + 1) * LANES]
                 for j in range(GQA_GROUP)], axis=0)
            st = lax.dot_general(keys, qh, (((1,), (1,)), ((), ())), preferred_element_type=F32) + bias
            after_scores(b * N_KV_HEADS + h)
            sink = jnp.zeros((1, GQA_GROUP * WINDOW), F32)
            for j in range(GQA_GROUP):
                sink = jnp.where(lane // WINDOW == j, sinks_ref[h * GQA_GROUP + j], sink)
            m = jnp.maximum(jnp.max(st, axis=0, keepdims=True), sink)
            e = jnp.exp(st - m)
            inv = 1.0 / (jnp.sum(e, axis=0, keepdims=True) + jnp.exp(sink - m))
            ot = _bdot(vt[h * HEAD_DIM:(h + 1) * HEAD_DIM, b * WINDOW:(b + 2) * WINDOW], (e * inv).astype(BF16))
            pieces.extend(ot[:, j * WINDOW:(j + 1) * WINDOW] for j in range(GQA_GROUP))
        blocks.append(jnp.transpose(jnp.concatenate(pieces, axis=0)))
    return jnp.concatenate(blocks, axis=0)


def _band_bias():
    kj = lax.broadcasted_iota(I32, (2, 2 * WINDOW, GQA_GROUP * WINDOW), 1)
    qi = lax.broadcasted_iota(I32, (2, 2 * WINDOW, GQA_GROUP * WINDOW), 2) % WINDOW
    lo = lax.broadcasted_iota(I32, (2, 2 * WINDOW, GQA_GROUP * WINDOW), 0) * WINDOW
    valid = (kj > qi) & (kj <= qi + WINDOW) & (kj >= lo)
    return jnp.where(valid, 0.0, -jnp.inf).astype(F32)


def _prompt_front_kernel(x_ref, cos_ref, sin_ref, gmix_ref, win_ref, lng_ref, lnb_ref, ws_ref, bsf_ref,
                         sinks_ref, bias_ref, wpa_ref, wpb_ref, wo_ref, gffn_ref, wr3_ref, br_ref, wgu32_ref, wdn32_ref,
                         x1_ref, h2_ref, logits_ref, kwin_ref, vwin_ref, wgu16_ref, wdn16_ref,
                         kprev_ref, vprev_ref, gu_in, dn_in, gu_out, dn_out, cast_sem):
    i = pl.program_id(0)
    seq_start = (i % TILES_PER_SEQ) == 0

    def cast_rows(c, j):
        part = CAST_ROWS // CAST_SPLIT
        return pl.ds(pl.multiple_of(c * CAST_ROWS + j * part, part), part), pl.ds(j * part, part)

    def cast_in(c, slot):
        cps = []
        for j in range(CAST_SPLIT):
            hbm, loc = cast_rows(c, j)
            cps.append(pltpu.make_async_copy(wgu32_ref.at[hbm], gu_in.at[slot, loc], cast_sem.at[0, slot]))
            cps.append(pltpu.make_async_copy(wdn32_ref.at[hbm], dn_in.at[slot, loc], cast_sem.at[1, slot]))
        return cps

    def cast_out(c, slot):
        cps = []
        for j in range(CAST_SPLIT):
            hbm, loc = cast_rows(c, j)
            cps.append(pltpu.make_async_copy(gu_out.at[slot, loc], wgu16_ref.at[hbm], cast_sem.at[2, slot]))
            cps.append(pltpu.make_async_copy(dn_out.at[slot, loc], wdn16_ref.at[hbm], cast_sem.at[3, slot]))
        return cps

    slot = i & 1

    @pl.when(i == 0)
    def _():
        for cp in cast_in(0, 0):
            cp.start()

    @pl.when(i + 1 < FRONT_STEPS)
    def _():
        for cp in cast_in(i + 1, 1 - slot):
            cp.start()

    for cp in cast_in(i, slot):
        cp.wait()

    @pl.when(i >= 2)
    def _():
        for cp in cast_out(i - 2, slot):
            cp.wait()


    @pl.when(seq_start)
    def _():
        kprev_ref[...] = jnp.zeros_like(kprev_ref)
        vprev_ref[...] = jnp.zeros_like(vprev_ref)

    x = x_ref[...]
    cast_done = [0]

    def cast_slice():
        i = cast_done[0]
        r = pl.ds(CAST_CUTS[i], CAST_CUTS[i + 1] - CAST_CUTS[i])
        gu_out[slot, r, :] = gu_in[slot, r, :].astype(BF16)
        dn_out[slot, r, :] = dn_in[slot, r, :].astype(BF16)
        cast_done[0] += 1

    u, va, q, k, v, hb = _in_projection(
        x, gmix_ref, win_ref, lng_ref, lnb_ref, cos_ref[...], sin_ref[...], cast_slice)
    assert cast_done[0] == IN_PROJ_DOTS
    for cp in cast_out(i, slot):
        cp.start()

    units = BLOCKS_PER_TILE * N_KV_HEADS
    preacts = []
    att = _band_attention(q, k, v, kprev_ref[...], vprev_ref[...], sinks_ref, bias_ref, seq_start,
                          lambda unit: preacts.append(_gate_preact(hb, win_ref, unit, units)))
    gate_a, gate_b = _gates(preacts)

    k_last, v_last = k[TM - WINDOW:], v[TM - WINDOW:]
    kprev_ref[...] = k_last
    vprev_ref[...] = v_last
    kwin_ref[0] = k_last
    vwin_ref[0] = v_last

    vab = va.astype(BF16)
    zc = jnp.concatenate(
        [jnp.concatenate(
            [_bdot(ws_ref[g], vab[b * CHUNK:(b + 1) * CHUNK, g * A_GROUP_DIM:(g + 1) * A_GROUP_DIM])
             for g in range(A_GROUPS)], axis=1) + bsf_ref[...]
         for b in range(BLOCKS_PER_TILE)], axis=0)

    x1, h2, logits = _merge_and_route(x, u * zc, att, gate_a, gate_b,
                                      wpa_ref, wpb_ref, wo_ref, gffn_ref, wr3_ref, br_ref)
    _store_token_major(x1_ref, x1)
    _store_token_major(h2_ref, h2)
    logits_ref[...] = logits

    @pl.when(i == FRONT_STEPS - 1)
    def _():
        for cp in cast_out(i - 1, 1 - slot) + cast_out(i, slot):
            cp.wait()


def _full(shape):
    return pl.BlockSpec(shape, lambda i: (0,) * len(shape))


def _prompt_front(x, cos, sin, gmix, win, lng, lnb, ws, bsf, sinks, bias, wpa, wpb, wo, gffn, wrt, br, wgu32, wdn32):
    n = x.shape[0]
    assert n == N_PROMPT
    grid = (FRONT_STEPS,)
    anyspec = pl.BlockSpec(memory_space=pl.ANY)
    in_specs = [
        pl.BlockSpec((TM, D_MODEL), lambda i: (i, 0)),
        pl.BlockSpec((TM, LANES), lambda i: (i % TILES_PER_SEQ, 0)),
        pl.BlockSpec((TM, LANES), lambda i: (i % TILES_PER_SEQ, 0)),
        _full((1, D_MODEL)),
        _full((D_MODEL, IN_COLS)),
        _full((1, A_WIDTH)),
        _full((1, A_WIDTH)),
        _full((A_GROUPS, CHUNK, CHUNK)),
        _full((CHUNK, A_WIDTH)),
        pl.BlockSpec(memory_space=pltpu.SMEM),
        _full((2, 2 * WINDOW, GQA_GROUP * WINDOW)),
        _full((A_WIDTH, D_MODEL)),
        _full((Q_WIDTH, D_MODEL)),
        _full((D_MODEL, D_MODEL)),
        _full((1, D_MODEL)),
        _full((3 * D_MODEL, LANES)),
        _full((N_EXPERTS, 1)),
        anyspec,
        anyspec,
    ]
    out_shape = [
        jax.ShapeDtypeStruct((n * LANE_CHUNKS, LANES), F32),
        jax.ShapeDtypeStruct((n * LANE_CHUNKS, LANES), F32),
        jax.ShapeDtypeStruct((N_EXPERTS, n), F32),
        jax.ShapeDtypeStruct((n // SEQ, WINDOW, KV_WIDTH), F32),
        jax.ShapeDtypeStruct((n // SEQ, WINDOW, KV_WIDTH), F32),
        jax.ShapeDtypeStruct(wgu32.shape, BF16),
        jax.ShapeDtypeStruct(wdn32.shape, BF16),
    ]
    out_specs = [
        pl.BlockSpec((TM * LANE_CHUNKS, LANES), lambda i: (i, 0)),
        pl.BlockSpec((TM * LANE_CHUNKS, LANES), lambda i: (i, 0)),
        pl.BlockSpec((N_EXPERTS, TM), lambda i: (0, i)),
        pl.BlockSpec((1, WINDOW, KV_WIDTH), lambda i: (i // TILES_PER_SEQ, 0, 0)),
        pl.BlockSpec((1, WINDOW, KV_WIDTH), lambda i: (i // TILES_PER_SEQ, 0, 0)),
        anyspec,
        anyspec,
    ]
    scratch = [
        pltpu.VMEM((WINDOW, KV_WIDTH), F32),
        pltpu.VMEM((WINDOW, KV_WIDTH), F32),
        pltpu.VMEM((2, CAST_ROWS, 2 * D_EXPERT), F32),
        pltpu.VMEM((2, CAST_ROWS, D_MODEL), F32),
        pltpu.VMEM((2, CAST_ROWS, 2 * D_EXPERT), BF16),
        pltpu.VMEM((2, CAST_ROWS, D_MODEL), BF16),
        pltpu.SemaphoreType.DMA((4, 2)),
    ]
    return pl.pallas_call(
        _prompt_front_kernel,
        grid=grid,
        in_specs=in_specs,
        out_specs=out_specs,
        out_shape=out_shape,
        scratch_shapes=scratch,
        compiler_params=pltpu.CompilerParams(dimension_semantics=("arbitrary",), vmem_limit_bytes=VMEM_LIMIT),
        name="prompt_front",
    )(x, cos, sin, gmix, win, lng, lnb, ws, bsf, sinks, bias, wpa, wpb, wo, gffn, wrt, br, wgu32, wdn32)


SAMPLE_STEP = 16
SAMPLE_STEPS = DEC_BATCH // SAMPLE_STEP


def _sample_kernel(x_ref, cos_ref, sin_ref, gmix_ref, win_ref, lng_ref, lnb_ref, wdiag_ref, bs0_ref, sinks_ref,
                   kc_ref, vc_ref, wpa_ref, wpb_ref, wo_ref, gffn_ref, wr3_ref, br_ref,
                   x1_ref, h2_ref, logits_ref, kwin_ref, vwin_ref, va_ref,
                   q_s, k_s, v_s, yain_s, ga_s, gb_s, att_s):
    i = pl.program_id(0)

    @pl.when(i == 0)
    def _():
        x = x_ref[...]
        cos = jnp.broadcast_to(cos_ref[...], (DEC_BATCH, LANES))
        sin = jnp.broadcast_to(sin_ref[...], (DEC_BATCH, LANES))
        u, va, q, k, v, hb = _in_projection(x, gmix_ref, win_ref, lng_ref, lnb_ref, cos, sin)
        gate_a, gate_b = _gates([_gate_preact(hb, win_ref, 0, 1)])
        va_ref[...] = va
        z = wdiag_ref[...].astype(F32) * va.astype(BF16).astype(F32) + bs0_ref[...]
        yain_s[...] = u * z
        q_s[...] = q
        k_s[...] = k
        v_s[...] = v
        ga_s[...] = gate_a
        gb_s[...] = gate_b

    r0 = pl.multiple_of(i * SAMPLE_STEP, SAMPLE_STEP)
    kwin = jnp.concatenate([kc_ref[:, 1:, :], k_s[pl.ds(r0, SAMPLE_STEP), :][:, None, :]], axis=1)
    vwin = jnp.concatenate([vc_ref[:, 1:, :], v_s[pl.ds(r0, SAMPLE_STEP), :][:, None, :]], axis=1)
    kwin_ref[...] = kwin
    vwin_ref[...] = vwin

    q16 = q_s[pl.ds(r0, SAMPLE_STEP), :]
    lane = lax.broadcasted_iota(I32, (SAMPLE_STEP, LANES), 1)
    heads = [q16[:, hq * LANES:(hq + 1) * LANES] for hq in range(N_HEADS)]
    qpad = pltpu.einshape("hbd->bhd", jnp.stack(heads, axis=0)).astype(BF16)
    s = jnp.einsum("bhd,bkd->bhk", qpad, kwin.astype(BF16), preferred_element_type=F32)
    hid = lax.broadcasted_iota(I32, (1, N_HEADS, 1), 1)
    sink = jnp.zeros((1, N_HEADS, 1), F32)
    for hq in range(N_HEADS):
        sink = jnp.where(hid == hq, sinks_ref[hq], sink)
    m = jnp.maximum(jnp.max(s, axis=-1, keepdims=True), sink)
    e = jnp.exp(s - m)
    inv = 1.0 / (jnp.sum(e, axis=-1, keepdims=True) + jnp.exp(sink - m))
    o = jnp.einsum("bhk,bkd->bhd", (e * inv).astype(BF16), vwin.astype(BF16), preferred_element_type=F32)
    o = pltpu.einshape("bhd->hbd", o)
    chunks = []
    for c in range(N_HEADS // 2):
        parts = []
        for p in range(2):
            hq = 2 * c + p
            oh = o[hq]
            if p != hq // GQA_GROUP:
                oh = pltpu.roll(oh, HEAD_DIM, 1)
            parts.append(oh)
        chunks.append(jnp.where(lane < HEAD_DIM, parts[0], parts[1]))
    att_s[pl.ds(r0, SAMPLE_STEP), :] = jnp.concatenate(chunks, axis=1)

    @pl.when(i == SAMPLE_STEPS - 1)
    def _():
        x1, h2, logits = _merge_and_route(
            x_ref[...], yain_s[...], att_s[...], ga_s[...], gb_s[...],
            wpa_ref, wpb_ref, wo_ref, gffn_ref, wr3_ref, br_ref)
        _store_token_major(x1_ref, x1)
        _store_token_major(h2_ref, h2)
        logits_ref[...] = logits


def _sample_front(x, cos, sin, gmix, win, lng, lnb, wdiag, bs0, sinks, kc, vc, wpa, wpb, wo, gffn, wrt, br):
    n = DEC_BATCH
    cache_spec = pl.BlockSpec((SAMPLE_STEP, WINDOW, KV_WIDTH), lambda i: (i, 0, 0))
    in_specs = [
        _full((n, D_MODEL)),
        _full((1, LANES)),
        _full((1, LANES)),
        _full((1, D_MODEL)),
        _full((D_MODEL, IN_COLS)),
        _full((1, A_WIDTH)),
        _full((1, A_WIDTH)),
        _full((1, A_WIDTH)),
        _full((1, A_WIDTH)),
        pl.BlockSpec(memory_space=pltpu.SMEM),
        cache_spec,
        cache_spec,
        _full((A_WIDTH, D_MODEL)),
        _full((Q_WIDTH, D_MODEL)),
        _full((D_MODEL, D_MODEL)),
        _full((1, D_MODEL)),
        _full((3 * D_MODEL, LANES)),
        _full((N_EXPERTS, 1)),
    ]
    out_shape = [
        jax.ShapeDtypeStruct((n * LANE_CHUNKS, LANES), F32),
        jax.ShapeDtypeStruct((n * LANE_CHUNKS, LANES), F32),
        jax.ShapeDtypeStruct((N_EXPERTS, n), F32),
        jax.ShapeDtypeStruct((n, WINDOW, KV_WIDTH), F32),
        jax.ShapeDtypeStruct((n, WINDOW, KV_WIDTH), F32),
        jax.ShapeDtypeStruct((n, A_WIDTH), F32),
    ]
    out_specs = [
        _full((n * LANE_CHUNKS, LANES)),
        _full((n * LANE_CHUNKS, LANES)),
        _full((N_EXPERTS, n)),
        cache_spec,
        cache_spec,
        _full((n, A_WIDTH)),
    ]
    scratch = [
        pltpu.VMEM((n, QPAD_WIDTH), F32), pltpu.VMEM((n, KV_WIDTH), F32), pltpu.VMEM((n, KV_WIDTH), F32),
        pltpu.VMEM((n, A_WIDTH), F32), pltpu.VMEM((n, D_MODEL), F32), pltpu.VMEM((n, D_MODEL), F32),
        pltpu.VMEM((n, Q_WIDTH), F32),
    ]
    return pl.pallas_call(
        _sample_kernel,
        grid=(SAMPLE_STEPS,),
        in_specs=in_specs,
        out_specs=out_specs,
        out_shape=out_shape,
        scratch_shapes=scratch,
        compiler_params=pltpu.CompilerParams(dimension_semantics=("arbitrary",), vmem_limit_bytes=VMEM_LIMIT),
        name="sample_front",
    )(x, cos, sin, gmix, win, lng, lnb, wdiag, bs0, sinks, kc, vc, wpa, wpb, wo, gffn, wrt, br)


def _route_plan_kernel(lp_ref, ls_ref, dest_ref, wts_ref, off_ref):
    g = pl.program_id(0)
    topi, topw = _top4_softmax(jnp.concatenate([lp_ref[...], ls_ref[...]], axis=1))
    slot = lax.broadcasted_iota(I32, (TOP_K, GROUP_SLOTS), 1)
    eall = jnp.where(jnp.logical_or(slot < GROUP_PROMPT, g == N_GROUPS - 1), topi, N_EXPERTS)
    wts_ref[:, 0:GROUP_SLOTS] = topw
    wts_ref[:, GROUP_SLOTS:] = jnp.zeros((TOP_K, K_STRIDE - GROUP_SLOTS), F32)
    dest_ref[:, GROUP_SLOTS:] = jnp.zeros((TOP_K, K_STRIDE - GROUP_SLOTS), I32)
    eid = lax.broadcasted_iota(I32, (N_EXPERTS, GROUP_SLOTS), 0)
    onehots = [eall[k:k + 1, :] == eid for k in range(TOP_K)]
    count = jnp.zeros((N_EXPERTS, GROUP_SLOTS), F32)
    for oh in onehots:
        count = count + oh.astype(F32)
    total = jnp.broadcast_to(jnp.sum(count, axis=1, keepdims=True), (N_EXPERTS, LANES))
    padded = total + (MOE_ROWS - 1)
    nblk = jnp.floor(padded * (1.0 / MOE_ROWS))
    rem = padded - nblk * MOE_ROWS
    nblk = jnp.where(rem >= MOE_ROWS, nblk + 1.0, jnp.where(rem < 0.0, nblk - 1.0, nblk))
    r = lax.broadcasted_iota(I32, (N_EXPERTS, N_EXPERTS), 0)
    c = lax.broadcasted_iota(I32, (N_EXPERTS, N_EXPERTS), 1)
    first_blk = lax.dot_general((c < r).astype(F32), nblk, (((1,), (0,)), ((), ())),
                                precision=lax.Precision.HIGHEST, preferred_element_type=F32)
    start = (first_blk + 1.0) * MOE_ROWS
    lane = lax.broadcasted_iota(I32, (N_EXPERTS, LANES), 1)
    info = jnp.where(lane == 0, start, jnp.where(lane == 1, start + total, jnp.where(lane == 2, nblk, first_blk)))
    off_ref[...] = info.astype(I32)
    ti = lax.broadcasted_iota(I32, (LANES, LANES), 0)
    tj = lax.broadcasted_iota(I32, (LANES, LANES), 1)
    before = (ti < tj).astype(BF16)
    ones = jnp.ones((LANES, LANES), BF16)
    running = start
    for t in range(SLOT_TILES):
        sl = slice(t * LANES, (t + 1) * LANES)
        cb = count[:, sl].astype(BF16)
        pos = running + _bdot(cb, before)
        rows = [jnp.sum(jnp.where(oh[:, sl], pos, 0.0), axis=0, keepdims=True) for oh in onehots]
        dest_ref[:, sl] = jnp.concatenate(rows, axis=0).astype(I32)
        running = running + _bdot(cb, ones)


def _route_plan(logits_p, logits_s):
    in_specs = [
        pl.BlockSpec((N_EXPERTS, GROUP_PROMPT), lambda g: (0, g)),
        pl.BlockSpec((N_EXPERTS, DEC_BATCH), lambda g: (0, 0)),
    ]
    out_shape = [
        jax.ShapeDtypeStruct((N_GROUPS, TOP_K, K_STRIDE), I32),
        jax.ShapeDtypeStruct((N_GROUPS, TOP_K, K_STRIDE), F32),
        jax.ShapeDtypeStruct((N_GROUPS, N_EXPERTS, LANES), I32),
    ]
    out_specs = [
        pl.BlockSpec((None, TOP_K, K_STRIDE), lambda g: (g, 0, 0)),
        pl.BlockSpec((None, TOP_K, K_STRIDE), lambda g: (g, 0, 0)),
        pl.BlockSpec((None, N_EXPERTS, LANES), lambda g: (g, 0, 0)),
    ]
    return pl.pallas_call(
        _route_plan_kernel,
        grid=(N_GROUPS,),
        in_specs=in_specs,
        out_specs=out_specs,
        out_shape=out_shape,
        compiler_params=pltpu.CompilerParams(dimension_semantics=("arbitrary",)),
        name="route_plan",
    )(logits_p, logits_s)


GROUP_ROWS = GROUP_PROMPT * LANE_CHUNKS
SAMPLE_ROWS = DEC_BATCH * LANE_CHUNKS
TRASH_SLOT = GROUP_SLOTS
BUF_ROWS = (GROUP_SLOTS + 1) * LANE_CHUNKS
SCATTER_BATCH = 8
DMA_SPLIT = 8
FFN_COLS = 256
FFN_DOTS = (2 * D_EXPERT + D_MODEL) // FFN_COLS


def _moe_kernel(off_ref, desth_ref, wtsh_ref, h2p_ref, h2s_ref, x1p_ref, x1s_ref, wgu_ref, bgu_ref, wdn_ref, bdn_ref,
                x2p_ref, x2s_ref,
                h2buf, acc, wgubuf, bgubuf, wdnbuf, bdnbuf, xs0, xs1, ys0, ys1,
                dest_ref, wts_ref, src_ref, seg_expert, seg_first, blk_seg, act_sem, w_sem):
    g = pl.program_id(0)
    last = g == N_GROUPS - 1
    row0 = pl.multiple_of(g * GROUP_ROWS, GROUP_ROWS)

    def prompt_copies():
        cps = []
        for j in range(DMA_SPLIT):
            src = pl.ds(row0 + j * (GROUP_ROWS // DMA_SPLIT), GROUP_ROWS // DMA_SPLIT)
            dst = pl.ds(j * (GROUP_ROWS // DMA_SPLIT), GROUP_ROWS // DMA_SPLIT)
            cps.append(pltpu.make_async_copy(h2p_ref.at[src], h2buf.at[dst], act_sem.at[0]))
            cps.append(pltpu.make_async_copy(x1p_ref.at[src], acc.at[dst], act_sem.at[1]))
        return cps

    def sample_copies():
        return (pltpu.make_async_copy(h2s_ref, h2buf.at[pl.ds(GROUP_ROWS, SAMPLE_ROWS)], act_sem.at[2]),
                pltpu.make_async_copy(x1s_ref, acc.at[pl.ds(GROUP_ROWS, SAMPLE_ROWS)], act_sem.at[3]))

    def weight_copies(e, slot):
        cps = [pltpu.make_async_copy(bgu_ref.at[e], bgubuf.at[slot], w_sem.at[1, slot]),
               pltpu.make_async_copy(bdn_ref.at[e], bdnbuf.at[slot], w_sem.at[3, slot])]
        for j in range(DMA_SPLIT):
            rg = pl.ds(j * (D_MODEL // DMA_SPLIT), D_MODEL // DMA_SPLIT)
            rd = pl.ds(j * (D_EXPERT // DMA_SPLIT), D_EXPERT // DMA_SPLIT)
            cps.append(pltpu.make_async_copy(wgu_ref.at[e, rg], wgubuf.at[slot, rg], w_sem.at[0, slot]))
            cps.append(pltpu.make_async_copy(wdn_ref.at[e, rd], wdnbuf.at[slot, rd], w_sem.at[2, slot]))
        return cps

    def output_copies():
        return [pltpu.make_async_copy(
            acc.at[pl.ds(j * (GROUP_ROWS // DMA_SPLIT), GROUP_ROWS // DMA_SPLIT)],
            x2p_ref.at[pl.ds(row0 + j * (GROUP_ROWS // DMA_SPLIT), GROUP_ROWS // DMA_SPLIT)], act_sem.at[0])
            for j in range(DMA_SPLIT)]

    tab0 = pl.multiple_of(g * (TOP_K * K_STRIDE), TOP_K * K_STRIDE)
    table_copies = (
        pltpu.make_async_copy(desth_ref.at[pl.ds(tab0, TOP_K * K_STRIDE)], dest_ref, act_sem.at[4]),
        pltpu.make_async_copy(wtsh_ref.at[pl.ds(tab0, TOP_K * K_STRIDE)], wts_ref, act_sem.at[5]))
    for cp in table_copies:
        cp.start()

    for cp in prompt_copies():
        cp.start()

    @pl.when(last)
    def _():
        for cp in sample_copies():
            cp.start()

    trash = pl.ds(TRASH_SLOT * LANE_CHUNKS, LANE_CHUNKS)
    h2buf[trash, :] = jnp.zeros((LANE_CHUNKS, LANES), F32)
    acc[trash, :] = jnp.zeros((LANE_CHUNKS, LANES), F32)
    ys0[...] = jnp.zeros_like(ys0)
    ys1[...] = jnp.zeros_like(ys1)

    def pad_block(pos0):
        def body(j, carry):
            for d in range(SUBLANES):
                src_ref[pos0 + j * SUBLANES + d] = TRASH_SLOT
            return carry
        lax.fori_loop(0, MOE_ROWS // SUBLANES, body, 0)

    def scan_expert(e, carry):
        nseg, nblocks = carry
        nblk = off_ref[e, 2]
        first = off_ref[e, 3]

        @pl.when(nblk > 0)
        def _():
            seg_expert[nseg] = e
            seg_first[nseg] = first
            pad_block(off_ref[e, 0] + (nblk - 1) * MOE_ROWS)

            def mark(b, c2):
                blk_seg[first + b] = nseg
                return c2
            lax.fori_loop(0, nblk, mark, 0)

            tail_rows = off_ref[e, 1] - off_ref[e, 0] - (nblk - 1) * MOE_ROWS
            blk_seg[first + nblk - 1] = nseg + jnp.where(tail_rows <= HALF_ROWS, HALF_FLAG, 0)

        return nseg + jnp.where(nblk > 0, 1, 0), nblocks + nblk

    nseg, nblocks = lax.fori_loop(0, N_EXPERTS, scan_expert, (jnp.int32(0), jnp.int32(0)))
    pad_block(0)
    pad_block((nblocks + 1) * MOE_ROWS)
    pad_block((nblocks + 2) * MOE_ROWS)
    blk_seg[nblocks] = nseg - 1 + HALF_FLAG
    blk_seg[nblocks + 1] = nseg - 1 + HALF_FLAG

    for cp in weight_copies(seg_expert[0], 0):
        cp.start()

    for cp in table_copies:
        cp.wait()

    nvalid = jnp.where(last, GROUP_SLOTS, GROUP_PROMPT)
    for k in range(TOP_K):
        def fill(j, carry, k=k):
            c0 = k * K_STRIDE + j * SUBLANES
            for d in range(SUBLANES):
                src_ref[dest_ref[c0 + d]] = c0 + d
            return carry
        lax.fori_loop(0, nvalid // SUBLANES, fill, 0)

    for cp in prompt_copies():
        cp.wait()

    @pl.when(last)
    def _():
        for cp in sample_copies():
            cp.wait()

    def token_rows(code):
        slot_id = code & (K_STRIDE - 1)
        return pl.ds(pl.multiple_of(slot_id * LANE_CHUNKS, LANE_CHUNKS), LANE_CHUNKS)

    def gather(b, xs, lo=0, hi=MOE_ROWS):
        base = (b + 1) * MOE_ROWS
        for m in range(lo, hi):
            xs[pl.ds(m, LANE_CHUNKS, stride=XS_STRIDE), :] = h2buf[token_rows(src_ref[base + m]), :]

    def scatter_add(b, ys, lo=0, hi=MOE_ROWS):
        base = (b + 1) * MOE_ROWS
        for m0 in range(lo, hi, SCATTER_BATCH):
            pending = []
            for m in range(m0, m0 + SCATTER_BATCH):
                code = src_ref[base + m]
                rows = token_rows(code)
                pending.append((rows, acc[rows, :] + wts_ref[code] * ys[pl.ds(m, LANE_CHUNKS, stride=XS_STRIDE), :]))
            for rows, val in pending:
                acc[rows, :] = val

    gather_cuts = [round(i * MOE_ROWS / FFN_DOTS) for i in range(FFN_DOTS + 1)]
    scatter_cuts = [SCATTER_BATCH * round(i * (MOE_ROWS // SCATTER_BATCH) / FFN_DOTS) for i in range(FFN_DOTS + 1)]

    def step(b, xs_cur, xs_next, ys_cur, ys_prev):
        tag = blk_seg[b]
        seg = tag & (HALF_FLAG - 1)
        half = tag >= HALF_FLAG
        slot = seg & 1

        @pl.when(jnp.logical_and(b == seg_first[seg], b < nblocks))
        def _():
            for cp in weight_copies(seg_expert[seg], slot):
                cp.wait()

            @pl.when(seg + 1 < nseg)
            def _():
                for cp in weight_copies(seg_expert[seg + 1], 1 - slot):
                    cp.start()

        def work(rows):
            done = [0]

            def row_traffic():
                i = done[0]
                gather(b + 1, xs_next, gather_cuts[i], gather_cuts[i + 1])
                scatter_add(b - 1, ys_prev, scatter_cuts[i], scatter_cuts[i + 1])
                done[0] = i + 1

            x = jnp.concatenate(
                [xs_cur[c * XS_STRIDE:c * XS_STRIDE + rows, :] for c in range(LANE_CHUNKS)], axis=1).astype(BF16)
            acts = []
            for c in range(D_EXPERT // FFN_COLS):
                gc = pl.ds(c * FFN_COLS, FFN_COLS)
                uc = pl.ds(D_EXPERT + c * FFN_COLS, FFN_COLS)
                g = _bdot(x, wgubuf[slot, :, gc]) + bgubuf[slot, :, gc]
                row_traffic()
                u = _bdot(x, wgubuf[slot, :, uc]) + bgubuf[slot, :, uc]
                row_traffic()
                gl = jnp.minimum(g, SWIGLU_LIMIT)
                ul = jnp.clip(u, -SWIGLU_LIMIT, SWIGLU_LIMIT)
                acts.append((ul + 1.0) * (gl * jax.nn.sigmoid(SWIGLU_ALPHA * gl)))
            a = jnp.concatenate(acts, axis=1).astype(BF16)
            for c in range(D_MODEL // FFN_COLS):
                oc = pl.ds(c * FFN_COLS, FFN_COLS)
                y = _bdot(a, wdnbuf[slot, :, oc]) + bdnbuf[slot, :, oc]
                for j in range(FFN_COLS // LANES):
                    lc = c * (FFN_COLS // LANES) + j
                    ys_cur[lc * XS_STRIDE:lc * XS_STRIDE + rows, :] = y[:, j * LANES:(j + 1) * LANES]
                row_traffic()

        pl.when(half)(lambda: work(HALF_ROWS))
        pl.when(jnp.logical_not(half))(lambda: work(MOE_ROWS))

    gather(0, xs0)
    npairs = (nblocks + 1) // 2

    def pair(t, carry):
        step(2 * t, xs0, xs1, ys0, ys1)
        step(2 * t + 1, xs1, xs0, ys1, ys0)
        return carry

    lax.fori_loop(0, npairs, pair, 0)
    scatter_add(2 * npairs - 1, ys1)

    for cp in output_copies():
        cp.start()

    @pl.when(last)
    def _():
        out_s = pltpu.make_async_copy(acc.at[pl.ds(GROUP_ROWS, SAMPLE_ROWS)], x2s_ref, act_sem.at[2])
        out_s.start()
        out_s.wait()

    for cp in output_copies():
        cp.wait()


def _moe(dest, wts, off, h2p, h2s, x1p, x1s, wgu, bgu, wdn, bdn):
    anyspec = pl.BlockSpec(memory_space=pl.ANY)
    dest = dest.reshape(N_GROUPS * TOP_K * K_STRIDE)
    wts = wts.reshape(N_GROUPS * TOP_K * K_STRIDE)
    in_specs = [
        pl.BlockSpec((None, N_EXPERTS, LANES), lambda g: (g, 0, 0), memory_space=pltpu.SMEM),
        anyspec, anyspec, anyspec, anyspec, anyspec, anyspec, anyspec, anyspec, anyspec, anyspec,
    ]
    scratch = [
        pltpu.VMEM((BUF_ROWS, LANES), F32),
        pltpu.VMEM((BUF_ROWS, LANES), F32),
        pltpu.VMEM((2, D_MODEL, 2 * D_EXPERT), BF16),
        pltpu.VMEM((2, 1, 2 * D_EXPERT), F32),
        pltpu.VMEM((2, D_EXPERT, D_MODEL), BF16),
        pltpu.VMEM((2, 1, D_MODEL), F32),
        pltpu.VMEM((LANE_CHUNKS * XS_STRIDE, LANES), F32),
        pltpu.VMEM((LANE_CHUNKS * XS_STRIDE, LANES), F32),
        pltpu.VMEM((LANE_CHUNKS * XS_STRIDE, LANES), F32),
        pltpu.VMEM((LANE_CHUNKS * XS_STRIDE, LANES), F32),
        pltpu.SMEM((TOP_K * K_STRIDE,), I32),
        pltpu.SMEM((TOP_K * K_STRIDE,), F32),
        pltpu.SMEM((POS_TABLE,), I32),
        pltpu.SMEM((N_EXPERTS,), I32),
        pltpu.SMEM((N_EXPERTS,), I32),
        pltpu.SMEM((LANES,), I32),
        pltpu.SemaphoreType.DMA((6,)),
        pltpu.SemaphoreType.DMA((4, 2)),
    ]
    return pl.pallas_call(
        _moe_kernel,
        grid=(N_GROUPS,),
        in_specs=in_specs,
        out_specs=[anyspec, anyspec],
        out_shape=[jax.ShapeDtypeStruct(x1p.shape, F32), jax.ShapeDtypeStruct(x1s.shape, F32)],
        scratch_shapes=scratch,
        compiler_params=pltpu.CompilerParams(dimension_semantics=("arbitrary",), vmem_limit_bytes=VMEM_LIMIT),
        name="moe",
    )(off, dest, wts, h2p, h2s, x1p, x1s, wgu, bgu, wdn, bdn)


def _ple_final_kernel(x2_ref, ple_ref, wple_ref, gple_ref, wpg_ref, gfin_ref, y_ref):
    rows = y_ref.shape[0]
    x2 = _load_token_major(x2_ref, rows)
    e = _rmsnorm(_bdot(ple_ref[...].astype(BF16), wple_ref[...]), gple_ref[...])
    x3 = x2 + jax.nn.sigmoid(_bdot(x2.astype(BF16), wpg_ref[...])) * e
    y_ref[...] = _rmsnorm(x3, gfin_ref[...])


def _ple_final(x2_tm, ple, wple, gple, wpg, gfin, tile):
    n = ple.shape[0]
    return pl.pallas_call(
        _ple_final_kernel,
        grid=(n // tile,),
        in_specs=[
            pl.BlockSpec((tile * LANE_CHUNKS, LANES), lambda i: (i, 0)),
            pl.BlockSpec((tile, PLE_DIM), lambda i: (i, 0)),
            _full((PLE_DIM, D_MODEL)),
            _full((1, D_MODEL)),
            _full((D_MODEL, D_MODEL)),
            _full((1, D_MODEL)),
        ],
        out_specs=pl.BlockSpec((tile, D_MODEL), lambda i: (i, 0)),
        out_shape=jax.ShapeDtypeStruct((n, D_MODEL), F32),
        compiler_params=pltpu.CompilerParams(dimension_semantics=("arbitrary",), vmem_limit_bytes=VMEM_LIMIT),
        name="ple_final",
    )(x2_tm, ple, wple, gple, wpg, gfin)


def _rope_tables(pos):
    half = HEAD_DIM // 2
    inv = ROPE_THETA ** (-jnp.arange(half, dtype=F32) / half)
    ang = pos.astype(F32)[:, None] * inv[None, :]
    cos, sin = jnp.cos(ang), jnp.sin(ang)
    cos2 = jnp.concatenate([cos, cos, cos, cos], axis=1)
    sin2 = jnp.concatenate([-sin, sin, -sin, sin], axis=1)
    return cos2, sin2


def _layout_w_in(w_in):
    o_q = 2 * A_WIDTH
    wq = w_in[:, o_q:o_q + Q_WIDTH].reshape(D_MODEL, N_HEADS, HEAD_DIM) * (HEAD_DIM ** -0.5)
    kv_head = (jnp.arange(N_HEADS) // GQA_GROUP)[None, :, None]
    wq_pad = jnp.concatenate([jnp.where(kv_head == h, wq, 0.0) for h in range(N_KV_HEADS)], axis=-1)
    return jnp.concatenate([w_in[:, :o_q], wq_pad.reshape(D_MODEL, QPAD_WIDTH), w_in[:, o_q + Q_WIDTH:]], axis=1)


def _router_passes(w_router):
    hi = w_router.astype(BF16)
    lo = (w_router - hi.astype(F32)).astype(BF16)
    w3 = jnp.concatenate([hi, lo, hi], axis=0)
    return jnp.pad(w3, ((0, 0), (0, LANES - N_EXPERTS)))


def _prep_weights(g_mix, w_in, a_ln_g, a_ln_b, a_ws, a_bs, w_pa, w_pb, w_o, g_ffn, w_router, b_router):
    causal = jnp.tril(jnp.ones((CHUNK, CHUNK), dtype=bool))
    return dict(
        gmix=g_mix.reshape(1, D_MODEL),
        win=_layout_w_in(w_in).astype(BF16),
        lng=a_ln_g.reshape(1, A_WIDTH),
        lnb=a_ln_b.reshape(1, A_WIDTH),
        ws=jnp.where(causal[None], a_ws, 0.0).astype(BF16),
        bsf=jnp.repeat(jnp.transpose(a_bs), A_GROUP_DIM, axis=1),
        wpa=w_pa.astype(BF16),
        wpb=w_pb.astype(BF16),
        wo=w_o.astype(BF16),
        gffn=g_ffn.reshape(1, D_MODEL),
        wrt=_router_passes(w_router),
        br=b_router.reshape(N_EXPERTS, 1),
    )


def kernel(x_prompt, x_sample, cache_win_k, cache_win_v, p_prompt, p_sample, g_mix, w_in, a_ln_g, a_ln_b, a_ws, a_bs, sinks, w_pa, w_pb, w_o, g_ffn, w_router, b_router, w_gu, b_gu, w_down, b_down, w_ple, g_ple, w_ple_gate, g_final):
    W = _prep_weights(g_mix[0], w_in[0], a_ln_g[0], a_ln_b[0], a_ws[0], a_bs[0], w_pa[0], w_pb[0], w_o[0],
                      g_ffn[0], w_router[0], b_router[0])
    cos_p, sin_p = _rope_tables(jnp.arange(SEQ, dtype=I32))
    cos_s, sin_s = _rope_tables(jnp.full((1,), PAST_LEN, I32))
    x1p, h2p, logits_p, kwin_p, vwin_p, wgu16, wdn16 = _prompt_front(
        x_prompt.reshape(N_PROMPT, D_MODEL), cos_p, sin_p, W["gmix"], W["win"], W["lng"], W["lnb"],
        W["ws"], W["bsf"], sinks[0], _band_bias(), W["wpa"], W["wpb"], W["wo"], W["gffn"], W["wrt"], W["br"],
        w_gu[0].reshape(N_EXPERTS * D_MODEL, 2 * D_EXPERT), w_down[0].reshape(N_EXPERTS * D_EXPERT, D_MODEL))

    wdiag = jnp.repeat(a_ws[0, :, 0, 0], A_GROUP_DIM)[None, :].astype(BF16)
    bs0 = jnp.repeat(a_bs[0, :, 0], A_GROUP_DIM)[None, :]
    x1s, h2s, logits_s, kwin_s, vwin_s, va_s = _sample_front(
        x_sample.reshape(DEC_BATCH, D_MODEL), cos_s, sin_s, W["gmix"], W["win"], W["lng"], W["lnb"], wdiag, bs0,
        sinks[0], cache_win_k[0].reshape(DEC_BATCH, WINDOW, KV_WIDTH), cache_win_v[0].reshape(DEC_BATCH, WINDOW, KV_WIDTH),
        W["wpa"], W["wpb"], W["wo"], W["gffn"], W["wrt"], W["br"])

    dest, wts, off = _route_plan(logits_p, logits_s)
    x2p, x2s = _moe(dest, wts, off, h2p, h2s, x1p, x1s,
                    wgu16.reshape(N_EXPERTS, D_MODEL, 2 * D_EXPERT), b_gu[0].reshape(N_EXPERTS, 1, 2 * D_EXPERT),
                    wdn16.reshape(N_EXPERTS, D_EXPERT, D_MODEL), b_down[0].reshape(N_EXPERTS, 1, D_MODEL))

    wple = w_ple[0].astype(BF16)
    gple = g_ple[0].reshape(1, D_MODEL)
    wpg = w_ple_gate[0].astype(BF16)
    gfin = g_final.reshape(1, D_MODEL)
    y_p = _ple_final(x2p, p_prompt[0].reshape(N_PROMPT, PLE_DIM), wple, gple, wpg, gfin, 2 * TM)
    y_s = _ple_final(x2s, p_sample[0].reshape(DEC_BATCH, PLE_DIM), wple, gple, wpg, gfin, DEC_BATCH)

    return (
        y_p.reshape(BATCH, SEQ, D_MODEL),
        y_s.reshape(DEC_BATCH, 1, D_MODEL),
        kwin_p.reshape(1, BATCH, WINDOW, N_KV_HEADS, HEAD_DIM),
        vwin_p.reshape(1, BATCH, WINDOW, N_KV_HEADS, HEAD_DIM),
        kwin_s.reshape(1, DEC_BATCH, WINDOW, N_KV_HEADS, HEAD_DIM),
        vwin_s.reshape(1, DEC_BATCH, WINDOW, N_KV_HEADS, HEAD_DIM),
        va_s.reshape(1, DEC_BATCH, 1, A_WIDTH),
    )
```

```python
import functools

import jax
import jax.numpy as jnp
from jax import lax
from jax.experimental import pallas as pl
from jax.experimental.pallas import tpu as pltpu

F32 = jnp.float32
BF16 = jnp.bfloat16
I32 = jnp.int32

D_MODEL = 1024
BATCH = 4
SEQ = 4096
DEC_BATCH = 128
PAST_LEN = 8192
CHUNK = 128
A_GROUPS = 4
A_GROUP_DIM = 128
A_WIDTH = A_GROUPS * A_GROUP_DIM
N_HEADS = 8
N_KV_HEADS = 2
HEAD_DIM = 64
Q_WIDTH = N_HEADS * HEAD_DIM
KV_WIDTH = N_KV_HEADS * HEAD_DIM
GQA_GROUP = N_HEADS // N_KV_HEADS
WINDOW = 128
ROPE_THETA = 10000.0
N_EXPERTS = 32
TOP_K = 4
D_EXPERT = D_MODEL
SWIGLU_ALPHA = 1.702
SWIGLU_LIMIT = 7.0
PLE_DIM = 256
RMS_EPS = 1e-5
LN_EPS = 1e-5

LANES = 128

QPAD_WIDTH = N_HEADS * LANES
O_Q = 2 * A_WIDTH
O_K = O_Q + QPAD_WIDTH
O_V = O_K + KV_WIDTH
O_GA = O_V + KV_WIDTH
O_GB = O_GA + D_MODEL
IN_COLS = O_GB + D_MODEL
SUBLANES = 8
LANE_CHUNKS = D_MODEL // LANES
VMEM_LIMIT = 56 * 1024 * 1024

N_PROMPT = BATCH * SEQ
TM = 256
TILES_PER_SEQ = SEQ // TM
BLOCKS_PER_TILE = TM // WINDOW
FRONT_STEPS = N_PROMPT // TM
CAST_ROWS = N_EXPERTS * D_MODEL // FRONT_STEPS
CAST_SPLIT = 4
IN_PROJ_DOTS = 5
BF16_ROWS = 16
CAST_CUTS = [BF16_ROWS * round(i * (CAST_ROWS // BF16_ROWS) / IN_PROJ_DOTS) for i in range(IN_PROJ_DOTS + 1)]

N_GROUPS = 4
GROUP_PROMPT = N_PROMPT // N_GROUPS
GROUP_SLOTS = GROUP_PROMPT + DEC_BATCH
GROUP_ASSIGN = GROUP_SLOTS * TOP_K
SLOT_TILES = GROUP_SLOTS // LANES
MOE_ROWS = 256
XS_STRIDE = MOE_ROWS + SUBLANES
HALF_ROWS = MOE_ROWS // 2
HALF_FLAG = 64
assert N_EXPERTS <= HALF_FLAG
SLOT_BITS = 13
K_STRIDE = 1 << SLOT_BITS
assert GROUP_SLOTS < K_STRIDE
MAX_BLOCKS = GROUP_ASSIGN // MOE_ROWS + N_EXPERTS
POS_TABLE = 1 << 15
assert (MAX_BLOCKS + 3) * MOE_ROWS <= POS_TABLE
assert MAX_BLOCKS + 2 <= LANES


def _bdot(a, b):
    return jnp.dot(a, b, preferred_element_type=F32)


def _rmsnorm(x, g):
    return x * lax.rsqrt(jnp.mean(x * x, axis=-1, keepdims=True) + RMS_EPS) * g


def _gelu(x):
    return 0.5 * x * (1.0 + lax.erf(x * (0.5 ** 0.5)))


def _group_layernorm(v, g, b):
    cols = []
    for gi in range(A_GROUPS):
        s = slice(gi * A_GROUP_DIM, (gi + 1) * A_GROUP_DIM)
        vg = v[:, s]
        mu = jnp.mean(vg, axis=-1, keepdims=True)
        d = vg - mu
        var = jnp.mean(d * d, axis=-1, keepdims=True)
        cols.append(d * lax.rsqrt(var + LN_EPS) * g[:, s] + b[:, s])
    return jnp.concatenate(cols, axis=1)


def _rope(x, cos, sin_signed):
    width = x.shape[1]
    reps = width // LANES
    cosf = jnp.concatenate([cos] * reps, axis=1) if reps > 1 else cos
    sinf = jnp.concatenate([sin_signed] * reps, axis=1) if reps > 1 else sin_signed
    half = HEAD_DIM // 2
    lane = lax.broadcasted_iota(I32, x.shape, 1)
    up = pltpu.roll(x, width - half, 1)
    down = pltpu.roll(x, half, 1)
    partner = jnp.where((lane & (HEAD_DIM - 1)) < half, up, down)
    return x * cosf + partner * sinf


def _in_projection(x, gmix_ref, win_ref, lng_ref, lnb_ref, cos, sin_signed, after_dot=lambda: None):
    hb = _rmsnorm(x, gmix_ref[...]).astype(BF16)

    def dot_cols(lo, hi):
        z = _bdot(hb, win_ref[:, lo:hi])
        after_dot()
        return z

    u = _gelu(dot_cols(0, A_WIDTH))
    va = _group_layernorm(_gelu(dot_cols(A_WIDTH, O_Q)), lng_ref[...], lnb_ref[...])
    zq = jnp.concatenate([dot_cols(O_Q, O_Q + QPAD_WIDTH // 2), dot_cols(O_Q + QPAD_WIDTH // 2, O_K)], axis=1)
    zkv = dot_cols(O_K, O_GA)
    q = _rope(zq, cos, sin_signed)
    k = _rope(zkv[:, :KV_WIDTH], cos, sin_signed)
    v = zkv[:, KV_WIDTH:]
    return u, va, q, k, v, hb


def _gate_preact(hb, win_ref, part, parts):
    width = 2 * D_MODEL // parts
    return _bdot(hb, win_ref[:, O_GA + part * width:O_GA + (part + 1) * width])


def _gates(preacts):
    zg = jnp.concatenate(preacts, axis=1)
    return jax.nn.sigmoid(zg[:, :D_MODEL]), jax.nn.sigmoid(zg[:, D_MODEL:])


def _merge_and_route(x, ya_in, att, gate_a, gate_b, wpa_ref, wpb_ref, wo_ref, gffn_ref, wr3_ref, br_ref,
                     x1_ref, h2_ref):
    ya = _bdot(ya_in.astype(BF16), wpa_ref[...])
    yb = _bdot(att.astype(BF16), wpb_ref[...])
    mix = (gate_a * ya + gate_b * yb).astype(BF16)
    x1 = x + _bdot(mix, wo_ref[...])
    _store_token_major(x1_ref, x1)
    h2 = _rmsnorm(x1, gffn_ref[...])
    _store_token_major(h2_ref, h2)
    hi = h2.astype(BF16)
    lo = (h2 - hi.astype(F32)).astype(BF16)
    logits = _bdot(jnp.concatenate([hi, hi, lo], axis=1), wr3_ref[...])
    return jnp.transpose(logits)[:N_EXPERTS, :] + br_ref[...]


def _top4_softmax(logits):
    eid = lax.broadcasted_iota(I32, logits.shape, 0)
    vals, idxs = [], []
    for _ in range(TOP_K):
        m = jnp.max(logits, axis=0, keepdims=True)
        idx = jnp.min(jnp.where(logits == m, eid, N_EXPERTS), axis=0, keepdims=True)
        logits = jnp.where(eid == idx, -jnp.inf, logits)
        vals.append(m)
        idxs.append(idx)
    es = [jnp.exp(v - vals[0]) for v in vals]
    inv = 1.0 / (es[0] + es[1] + es[2] + es[3])
    return jnp.concatenate(idxs, axis=0), jnp.concatenate([e * inv for e in es], axis=0)


def _store_token_major(ref, val):
    rows = val.shape[0]
    for c in range(LANE_CHUNKS):
        ref[pl.ds(c, rows, stride=LANE_CHUNKS), :] = val[:, c * LANES:(c + 1) * LANES]


def _load_token_major(ref, rows):
    return jnp.concatenate([ref[pl.ds(c, rows, stride=LANE_CHUNKS), :] for c in range(LANE_CHUNKS)], axis=1)


def _band_attention(qpad, k, v, k_prev, v_prev, sinks_ref, bias_ref, seq_start, after_scores):
    kb = jnp.concatenate([k_prev, k], axis=0).astype(BF16)
    vt = jnp.transpose(jnp.concatenate([v_prev, v], axis=0)).astype(BF16)
    qb = qpad.astype(BF16)
    lane = lax.broadcasted_iota(I32, (1, GQA_GROUP * WINDOW), 1)
    blocks = []
    for b in range(BLOCKS_PER_TILE):
        bias = bias_ref[jnp.where(seq_start, 1, 0)] if b == 0 else bias_ref[0]
        keys = kb[b * WINDOW:(b + 2) * WINDOW, :]
        pieces = []
        for h in range(N_KV_HEADS):
            qh = jnp.concatenate(
                [qb[b * WINDOW:(b + 1) * WINDOW, (h * GQA_GROUP + j) * LANES:(h * GQA_GROUP + j + 1) * LANES]
                 for j in range(GQA_GROUP)], axis=0)
            st = lax.dot_general(keys, qh, (((1,), (1,)), ((), ())), preferred_element_type=F32) + bias
            after_scores(b * N_KV_HEADS + h)
            sink = jnp.zeros((1, GQA_GROUP * WINDOW), F32)
            for j in range(GQA_GROUP):
                sink = jnp.where(lane // WINDOW == j, sinks_ref[h * GQA_GROUP + j], sink)
            m = jnp.maximum(jnp.max(st, axis=0, keepdims=True), sink)
            e = jnp.exp(st - m)
            inv = 1.0 / (jnp.sum(e, axis=0, keepdims=True) + jnp.exp(sink - m))
            ot = _bdot(vt[h * HEAD_DIM:(h + 1) * HEAD_DIM, b * WINDOW:(b + 2) * WINDOW], (e * inv).astype(BF16))
            pieces.extend(ot[:, j * WINDOW:(j + 1) * WINDOW] for j in range(GQA_GROUP))
        blocks.append(jnp.transpose(jnp.concatenate(pieces, axis=0)))
    return jnp.concatenate(blocks, axis=0)


def _band_bias():
    kj = lax.broadcasted_iota(I32, (2, 2 * WINDOW, GQA_GROUP * WINDOW), 1)
    qi = lax.broadcasted_iota(I32, (2, 2 * WINDOW, GQA_GROUP * WINDOW), 2) % WINDOW
    lo = lax.broadcasted_iota(I32, (2, 2 * WINDOW, GQA_GROUP * WINDOW), 0) * WINDOW
    valid = (kj > qi) & (kj <= qi + WINDOW) & (kj >= lo)
    return jnp.where(valid, 0.0, -jnp.inf).astype(F32)


def _prompt_front_kernel(x_ref, cos_ref, sin_ref, gmix_ref, win_ref, lng_ref, lnb_ref, ws_ref, bsf_ref,
                         sinks_ref, bias_ref, wpa_ref, wpb_ref, wo_ref, gffn_ref, wr3_ref, br_ref, wgu32_ref, wdn32_ref,
                         x1_ref, h2_ref, logits_ref, kwin_ref, vwin_ref, wgu16_ref, wdn16_ref,
                         kprev_ref, vprev_ref, gu_in, dn_in, gu_out, dn_out, cast_sem):
    i = pl.program_id(0)
    seq_start = (i % TILES_PER_SEQ) == 0

    def cast_rows(c, j):
        part = CAST_ROWS // CAST_SPLIT
        return pl.ds(pl.multiple_of(c * CAST_ROWS + j * part, part), part), pl.ds(j * part, part)

    def cast_in(c, slot):
        cps = []
        for j in range(CAST_SPLIT):
            hbm, loc = cast_rows(c, j)
            cps.append(pltpu.make_async_copy(wgu32_ref.at[hbm], gu_in.at[slot, loc], cast_sem.at[0, slot]))
            cps.append(pltpu.make_async_copy(wdn32_ref.at[hbm], dn_in.at[slot, loc], cast_sem.at[1, slot]))
        return cps

    def cast_out(c, slot):
        cps = []
        for j in range(CAST_SPLIT):
            hbm, loc = cast_rows(c, j)
            cps.append(pltpu.make_async_copy(gu_out.at[slot, loc], wgu16_ref.at[hbm], cast_sem.at[2, slot]))
            cps.append(pltpu.make_async_copy(dn_out.at[slot, loc], wdn16_ref.at[hbm], cast_sem.at[3, slot]))
        return cps

    slot = i & 1

    @pl.when(i == 0)
    def _():
        for cp in cast_in(0, 0):
            cp.start()

    @pl.when(i + 1 < FRONT_STEPS)
    def _():
        for cp in cast_in(i + 1, 1 - slot):
            cp.start()

    for cp in cast_in(i, slot):
        cp.wait()

    @pl.when(i >= 2)
    def _():
        for cp in cast_out(i - 2, slot):
            cp.wait()


    @pl.when(seq_start)
    def _():
        kprev_ref[...] = jnp.zeros_like(kprev_ref)
        vprev_ref[...] = jnp.zeros_like(vprev_ref)

    x = x_ref[...]
    cast_done = [0]

    def cast_slice():
        i = cast_done[0]
        r = pl.ds(CAST_CUTS[i], CAST_CUTS[i + 1] - CAST_CUTS[i])
        gu_out[slot, r, :] = gu_in[slot, r, :].astype(BF16)
        dn_out[slot, r, :] = dn_in[slot, r, :].astype(BF16)
        cast_done[0] += 1

    u, va, q, k, v, hb = _in_projection(
        x, gmix_ref, win_ref, lng_ref, lnb_ref, cos_ref[...], sin_ref[...], cast_slice)
    assert cast_done[0] == IN_PROJ_DOTS
    for cp in cast_out(i, slot):
        cp.start()

    units = BLOCKS_PER_TILE * N_KV_HEADS
    preacts = []
    att = _band_attention(q, k, v, kprev_ref[...], vprev_ref[...], sinks_ref, bias_ref, seq_start,
                          lambda unit: preacts.append(_gate_preact(hb, win_ref, unit, units)))
    gate_a, gate_b = _gates(preacts)

    k_last, v_last = k[TM - WINDOW:], v[TM - WINDOW:]
    kprev_ref[...] = k_last
    vprev_ref[...] = v_last
    kwin_ref[0] = k_last
    vwin_ref[0] = v_last

    vab = va.astype(BF16)
    zc = jnp.concatenate(
        [jnp.concatenate(
            [_bdot(ws_ref[g], vab[b * CHUNK:(b + 1) * CHUNK, g * A_GROUP_DIM:(g + 1) * A_GROUP_DIM])
             for g in range(A_GROUPS)], axis=1) + bsf_ref[...]
         for b in range(BLOCKS_PER_TILE)], axis=0)

    logits_ref[...] = _merge_and_route(x, u * zc, att, gate_a, gate_b,
                                       wpa_ref, wpb_ref, wo_ref, gffn_ref, wr3_ref, br_ref, x1_ref, h2_ref)

    @pl.when(i == FRONT_STEPS - 1)
    def _():
        for cp in cast_out(i - 1, 1 - slot) + cast_out(i, slot):
            cp.wait()


def _full(shape):
    return pl.BlockSpec(shape, lambda i: (0,) * len(shape))


def _prompt_front(x, cos, sin, gmix, win, lng, lnb, ws, bsf, sinks, bias, wpa, wpb, wo, gffn, wrt, br, wgu32, wdn32):
    n = x.shape[0]
    assert n == N_PROMPT
    grid = (FRONT_STEPS,)
    anyspec = pl.BlockSpec(memory_space=pl.ANY)
    in_specs = [
        pl.BlockSpec((TM, D_MODEL), lambda i: (i, 0)),
        pl.BlockSpec((TM, LANES), lambda i: (i % TILES_PER_SEQ, 0)),
        pl.BlockSpec((TM, LANES), lambda i: (i % TILES_PER_SEQ, 0)),
        _full((1, D_MODEL)),
        _full((D_MODEL, IN_COLS)),
        _full((1, A_WIDTH)),
        _full((1, A_WIDTH)),
        _full((A_GROUPS, CHUNK, CHUNK)),
        _full((CHUNK, A_WIDTH)),
        pl.BlockSpec(memory_space=pltpu.SMEM),
        _full((2, 2 * WINDOW, GQA_GROUP * WINDOW)),
        _full((A_WIDTH, D_MODEL)),
        _full((Q_WIDTH, D_MODEL)),
        _full((D_MODEL, D_MODEL)),
        _full((1, D_MODEL)),
        _full((3 * D_MODEL, LANES)),
        _full((N_EXPERTS, 1)),
        anyspec,
        anyspec,
    ]
    out_shape = [
        jax.ShapeDtypeStruct((n * LANE_CHUNKS, LANES), F32),
        jax.ShapeDtypeStruct((n * LANE_CHUNKS, LANES), F32),
        jax.ShapeDtypeStruct((N_EXPERTS, n), F32),
        jax.ShapeDtypeStruct((n // SEQ, WINDOW, KV_WIDTH), F32),
        jax.ShapeDtypeStruct((n // SEQ, WINDOW, KV_WIDTH), F32),
        jax.ShapeDtypeStruct(wgu32.shape, BF16),
        jax.ShapeDtypeStruct(wdn32.shape, BF16),
    ]
    out_specs = [
        pl.BlockSpec((TM * LANE_CHUNKS, LANES), lambda i: (i, 0)),
        pl.BlockSpec((TM * LANE_CHUNKS, LANES), lambda i: (i, 0)),
        pl.BlockSpec((N_EXPERTS, TM), lambda i: (0, i)),
        pl.BlockSpec((1, WINDOW, KV_WIDTH), lambda i: (i // TILES_PER_SEQ, 0, 0)),
        pl.BlockSpec((1, WINDOW, KV_WIDTH), lambda i: (i // TILES_PER_SEQ, 0, 0)),
        anyspec,
        anyspec,
    ]
    scratch = [
        pltpu.VMEM((WINDOW, KV_WIDTH), F32),
        pltpu.VMEM((WINDOW, KV_WIDTH), F32),
        pltpu.VMEM((2, CAST_ROWS, 2 * D_EXPERT), F32),
        pltpu.VMEM((2, CAST_ROWS, D_MODEL), F32),
        pltpu.VMEM((2, CAST_ROWS, 2 * D_EXPERT), BF16),
        pltpu.VMEM((2, CAST_ROWS, D_MODEL), BF16),
        pltpu.SemaphoreType.DMA((4, 2)),
    ]
    return pl.pallas_call(
        _prompt_front_kernel,
        grid=grid,
        in_specs=in_specs,
        out_specs=out_specs,
        out_shape=out_shape,
        scratch_shapes=scratch,
        compiler_params=pltpu.CompilerParams(dimension_semantics=("arbitrary",), vmem_limit_bytes=VMEM_LIMIT),
        name="prompt_front",
    )(x, cos, sin, gmix, win, lng, lnb, ws, bsf, sinks, bias, wpa, wpb, wo, gffn, wrt, br, wgu32, wdn32)


SAMPLE_STEP = 16
SAMPLE_STEPS = DEC_BATCH // SAMPLE_STEP


def _sample_kernel(x_ref, cos_ref, sin_ref, gmix_ref, win_ref, lng_ref, lnb_ref, wdiag_ref, bs0_ref, sinks_ref,
                   kc_ref, vc_ref, wpa_ref, wpb_ref, wo_ref, gffn_ref, wr3_ref, br_ref,
                   x1_ref, h2_ref, logits_ref, kwin_ref, vwin_ref, va_ref,
                   q_s, k_s, v_s, yain_s, ga_s, gb_s, att_s):
    i = pl.program_id(0)

    @pl.when(i == 0)
    def _():
        x = x_ref[...]
        cos = jnp.broadcast_to(cos_ref[...], (DEC_BATCH, LANES))
        sin = jnp.broadcast_to(sin_ref[...], (DEC_BATCH, LANES))
        u, va, q, k, v, hb = _in_projection(x, gmix_ref, win_ref, lng_ref, lnb_ref, cos, sin)
        gate_a, gate_b = _gates([_gate_preact(hb, win_ref, 0, 1)])
        va_ref[...] = va
        z = wdiag_ref[...].astype(F32) * va.astype(BF16).astype(F32) + bs0_ref[...]
        yain_s[...] = u * z
        q_s[...] = q
        k_s[...] = k
        v_s[...] = v
        ga_s[...] = gate_a
        gb_s[...] = gate_b

    r0 = pl.multiple_of(i * SAMPLE_STEP, SAMPLE_STEP)
    kwin = jnp.concatenate([kc_ref[:, 1:, :], k_s[pl.ds(r0, SAMPLE_STEP), :][:, None, :]], axis=1)
    vwin = jnp.concatenate([vc_ref[:, 1:, :], v_s[pl.ds(r0, SAMPLE_STEP), :][:, None, :]], axis=1)
    kwin_ref[...] = kwin
    vwin_ref[...] = vwin

    q16 = q_s[pl.ds(r0, SAMPLE_STEP), :]
    lane = lax.broadcasted_iota(I32, (SAMPLE_STEP, LANES), 1)
    heads = [q16[:, hq * LANES:(hq + 1) * LANES] for hq in range(N_HEADS)]
    qpad = pltpu.einshape("hbd->bhd", jnp.stack(heads, axis=0)).astype(BF16)
    s = jnp.einsum("bhd,bkd->bhk", qpad, kwin.astype(BF16), preferred_element_type=F32)
    hid = lax.broadcasted_iota(I32, (1, N_HEADS, 1), 1)
    sink = jnp.zeros((1, N_HEADS, 1), F32)
    for hq in range(N_HEADS):
        sink = jnp.where(hid == hq, sinks_ref[hq], sink)
    m = jnp.maximum(jnp.max(s, axis=-1, keepdims=True), sink)
    e = jnp.exp(s - m)
    inv = 1.0 / (jnp.sum(e, axis=-1, keepdims=True) + jnp.exp(sink - m))
    o = jnp.einsum("bhk,bkd->bhd", (e * inv).astype(BF16), vwin.astype(BF16), preferred_element_type=F32)
    o = pltpu.einshape("bhd->hbd", o)
    chunks = []
    for c in range(N_HEADS // 2):
        parts = []
        for p in range(2):
            hq = 2 * c + p
            oh = o[hq]
            if p != hq // GQA_GROUP:
                oh = pltpu.roll(oh, HEAD_DIM, 1)
            parts.append(oh)
        chunks.append(jnp.where(lane < HEAD_DIM, parts[0], parts[1]))
    att_s[pl.ds(r0, SAMPLE_STEP), :] = jnp.concatenate(chunks, axis=1)

    @pl.when(i == SAMPLE_STEPS - 1)
    def _():
        logits_ref[...] = _merge_and_route(
            x_ref[...], yain_s[...], att_s[...], ga_s[...], gb_s[...],
            wpa_ref, wpb_ref, wo_ref, gffn_ref, wr3_ref, br_ref, x1_ref, h2_ref)


def _sample_front(x, cos, sin, gmix, win, lng, lnb, wdiag, bs0, sinks, kc, vc, wpa, wpb, wo, gffn, wrt, br):
    n = DEC_BATCH
    cache_spec = pl.BlockSpec((SAMPLE_STEP, WINDOW, KV_WIDTH), lambda i: (i, 0, 0))
    in_specs = [
        _full((n, D_MODEL)),
        _full((1, LANES)),
        _full((1, LANES)),
        _full((1, D_MODEL)),
        _full((D_MODEL, IN_COLS)),
        _full((1, A_WIDTH)),
        _full((1, A_WIDTH)),
        _full((1, A_WIDTH)),
        _full((1, A_WIDTH)),
        pl.BlockSpec(memory_space=pltpu.SMEM),
        cache_spec,
        cache_spec,
        _full((A_WIDTH, D_MODEL)),
        _full((Q_WIDTH, D_MODEL)),
        _full((D_MODEL, D_MODEL)),
        _full((1, D_MODEL)),
        _full((3 * D_MODEL, LANES)),
        _full((N_EXPERTS, 1)),
    ]
    out_shape = [
        jax.ShapeDtypeStruct((n * LANE_CHUNKS, LANES), F32),
        jax.ShapeDtypeStruct((n * LANE_CHUNKS, LANES), F32),
        jax.ShapeDtypeStruct((N_EXPERTS, n), F32),
        jax.ShapeDtypeStruct((n, WINDOW, KV_WIDTH), F32),
        jax.ShapeDtypeStruct((n, WINDOW, KV_WIDTH), F32),
        jax.ShapeDtypeStruct((n, A_WIDTH), F32),
    ]
    out_specs = [
        _full((n * LANE_CHUNKS, LANES)),
        _full((n * LANE_CHUNKS, LANES)),
        _full((N_EXPERTS, n)),
        cache_spec,
        cache_spec,
        _full((n, A_WIDTH)),
    ]
    scratch = [
        pltpu.VMEM((n, QPAD_WIDTH), F32), pltpu.VMEM((n, KV_WIDTH), F32), pltpu.VMEM((n, KV_WIDTH), F32),
        pltpu.VMEM((n, A_WIDTH), F32), pltpu.VMEM((n, D_MODEL), F32), pltpu.VMEM((n, D_MODEL), F32),
        pltpu.VMEM((n, Q_WIDTH), F32),
    ]
    return pl.pallas_call(
        _sample_kernel,
        grid=(SAMPLE_STEPS,),
        in_specs=in_specs,
        out_specs=out_specs,
        out_shape=out_shape,
        scratch_shapes=scratch,
        compiler_params=pltpu.CompilerParams(dimension_semantics=("arbitrary",), vmem_limit_bytes=VMEM_LIMIT),
        name="sample_front",
    )(x, cos, sin, gmix, win, lng, lnb, wdiag, bs0, sinks, kc, vc, wpa, wpb, wo, gffn, wrt, br)


def _route_plan_kernel(lp_ref, ls_ref, dest_ref, wts_ref, off_ref):
    g = pl.program_id(0)
    topi, topw = _top4_softmax(jnp.concatenate([lp_ref[...], ls_ref[...]], axis=1))
    slot = lax.broadcasted_iota(I32, (TOP_K, GROUP_SLOTS), 1)
    eall = jnp.where(jnp.logical_or(slot < GROUP_PROMPT, g == N_GROUPS - 1), topi, N_EXPERTS)
    wts_ref[:, 0:GROUP_SLOTS] = topw
    wts_ref[:, GROUP_SLOTS:] = jnp.zeros((TOP_K, K_STRIDE - GROUP_SLOTS), F32)
    dest_ref[:, GROUP_SLOTS:] = jnp.zeros((TOP_K, K_STRIDE - GROUP_SLOTS), I32)
    eid = lax.broadcasted_iota(I32, (N_EXPERTS, GROUP_SLOTS), 0)
    onehots = [eall[k:k + 1, :] == eid for k in range(TOP_K)]
    count = jnp.zeros((N_EXPERTS, GROUP_SLOTS), F32)
    for oh in onehots:
        count = count + oh.astype(F32)
    total = jnp.broadcast_to(jnp.sum(count, axis=1, keepdims=True), (N_EXPERTS, LANES))
    padded = total + (MOE_ROWS - 1)
    nblk = jnp.floor(padded * (1.0 / MOE_ROWS))
    rem = padded - nblk * MOE_ROWS
    nblk = jnp.where(rem >= MOE_ROWS, nblk + 1.0, jnp.where(rem < 0.0, nblk - 1.0, nblk))
    r = lax.broadcasted_iota(I32, (N_EXPERTS, N_EXPERTS), 0)
    c = lax.broadcasted_iota(I32, (N_EXPERTS, N_EXPERTS), 1)
    first_blk = lax.dot_general((c < r).astype(F32), nblk, (((1,), (0,)), ((), ())),
                                precision=lax.Precision.HIGHEST, preferred_element_type=F32)
    start = (first_blk + 1.0) * MOE_ROWS
    lane = lax.broadcasted_iota(I32, (N_EXPERTS, LANES), 1)
    info = jnp.where(lane == 0, start, jnp.where(lane == 1, start + total, jnp.where(lane == 2, nblk, first_blk)))
    off_ref[...] = info.astype(I32)
    ti = lax.broadcasted_iota(I32, (LANES, LANES), 0)
    tj = lax.broadcasted_iota(I32, (LANES, LANES), 1)
    before = (ti < tj).astype(BF16)
    ones = jnp.ones((LANES, LANES), BF16)
    running = start
    for t in range(SLOT_TILES):
        sl = slice(t * LANES, (t + 1) * LANES)
        cb = count[:, sl].astype(BF16)
        pos = running + _bdot(cb, before)
        rows = [jnp.sum(jnp.where(oh[:, sl], pos, 0.0), axis=0, keepdims=True) for oh in onehots]
        dest_ref[:, sl] = jnp.concatenate(rows, axis=0).astype(I32)
        running = running + _bdot(cb, ones)


def _route_plan(logits_p, logits_s):
    in_specs = [
        pl.BlockSpec((N_EXPERTS, GROUP_PROMPT), lambda g: (0, g)),
        pl.BlockSpec((N_EXPERTS, DEC_BATCH), lambda g: (0, 0)),
    ]
    out_shape = [
        jax.ShapeDtypeStruct((N_GROUPS, TOP_K, K_STRIDE), I32),
        jax.ShapeDtypeStruct((N_GROUPS, TOP_K, K_STRIDE), F32),
        jax.ShapeDtypeStruct((N_GROUPS, N_EXPERTS, LANES), I32),
    ]
    out_specs = [
        pl.BlockSpec((None, TOP_K, K_STRIDE), lambda g: (g, 0, 0)),
        pl.BlockSpec((None, TOP_K, K_STRIDE), lambda g: (g, 0, 0)),
        pl.BlockSpec((None, N_EXPERTS, LANES), lambda g: (g, 0, 0)),
    ]
    return pl.pallas_call(
        _route_plan_kernel,
        grid=(N_GROUPS,),
        in_specs=in_specs,
        out_specs=out_specs,
        out_shape=out_shape,
        compiler_params=pltpu.CompilerParams(dimension_semantics=("arbitrary",)),
        name="route_plan",
    )(logits_p, logits_s)


GROUP_ROWS = GROUP_PROMPT * LANE_CHUNKS
SAMPLE_ROWS = DEC_BATCH * LANE_CHUNKS
TRASH_SLOT = GROUP_SLOTS
BUF_ROWS = (GROUP_SLOTS + 1) * LANE_CHUNKS
SCATTER_BATCH = 8
DMA_SPLIT = 8
TABLE_USED = GROUP_SLOTS + LANES
assert TRASH_SLOT < TABLE_USED <= K_STRIDE
FFN_COLS = 256
FFN_DOTS = (2 * D_EXPERT + D_MODEL) // FFN_COLS


def _moe_kernel(off_ref, desth_ref, wtsh_ref, h2p_ref, h2s_ref, x1p_ref, x1s_ref, wgu_ref, bgu_ref, wdn_ref, bdn_ref,
                x2p_ref, x2s_ref,
                h2buf, acc, wgubuf, bgubuf, wdnbuf, bdnbuf, xs0, xs1, ys0, ys1,
                dest_ref, wts_ref, src_ref, seg_expert, seg_first, blk_seg, act_sem, w_sem):
    g = pl.program_id(0)
    last = g == N_GROUPS - 1
    row0 = pl.multiple_of(g * GROUP_ROWS, GROUP_ROWS)

    def prompt_copies():
        cps = []
        for j in range(DMA_SPLIT):
            src = pl.ds(row0 + j * (GROUP_ROWS // DMA_SPLIT), GROUP_ROWS // DMA_SPLIT)
            dst = pl.ds(j * (GROUP_ROWS // DMA_SPLIT), GROUP_ROWS // DMA_SPLIT)
            cps.append(pltpu.make_async_copy(h2p_ref.at[src], h2buf.at[dst], act_sem.at[0]))
            cps.append(pltpu.make_async_copy(x1p_ref.at[src], acc.at[dst], act_sem.at[1]))
        return cps

    def sample_copies():
        return (pltpu.make_async_copy(h2s_ref, h2buf.at[pl.ds(GROUP_ROWS, SAMPLE_ROWS)], act_sem.at[2]),
                pltpu.make_async_copy(x1s_ref, acc.at[pl.ds(GROUP_ROWS, SAMPLE_ROWS)], act_sem.at[3]))

    def weight_copies(e, slot):
        cps = [pltpu.make_async_copy(bgu_ref.at[e], bgubuf.at[slot], w_sem.at[1, slot]),
               pltpu.make_async_copy(bdn_ref.at[e], bdnbuf.at[slot], w_sem.at[3, slot])]
        for j in range(DMA_SPLIT):
            rg = pl.ds(j * (D_MODEL // DMA_SPLIT), D_MODEL // DMA_SPLIT)
            rd = pl.ds(j * (D_EXPERT // DMA_SPLIT), D_EXPERT // DMA_SPLIT)
            cps.append(pltpu.make_async_copy(wgu_ref.at[e, rg], wgubuf.at[slot, rg], w_sem.at[0, slot]))
            cps.append(pltpu.make_async_copy(wdn_ref.at[e, rd], wdnbuf.at[slot, rd], w_sem.at[2, slot]))
        return cps

    def output_copies():
        return [pltpu.make_async_copy(
            acc.at[pl.ds(j * (GROUP_ROWS // DMA_SPLIT), GROUP_ROWS // DMA_SPLIT)],
            x2p_ref.at[pl.ds(row0 + j * (GROUP_ROWS // DMA_SPLIT), GROUP_ROWS // DMA_SPLIT)], act_sem.at[0])
            for j in range(DMA_SPLIT)]

    tab0 = pl.multiple_of(g * (TOP_K * K_STRIDE), TOP_K * K_STRIDE)
    table_copies = []
    for k in range(TOP_K):
        used = pl.ds(k * K_STRIDE, TABLE_USED)
        table_copies.append(pltpu.make_async_copy(
            desth_ref.at[pl.ds(tab0 + k * K_STRIDE, TABLE_USED)], dest_ref.at[used], act_sem.at[4]))
        table_copies.append(pltpu.make_async_copy(
            wtsh_ref.at[pl.ds(tab0 + k * K_STRIDE, TABLE_USED)], wts_ref.at[used], act_sem.at[5]))
    for cp in table_copies:
        cp.start()

    for cp in prompt_copies():
        cp.start()

    @pl.when(last)
    def _():
        for cp in sample_copies():
            cp.start()

    trash = pl.ds(TRASH_SLOT * LANE_CHUNKS, LANE_CHUNKS)
    h2buf[trash, :] = jnp.zeros((LANE_CHUNKS, LANES), F32)
    acc[trash, :] = jnp.zeros((LANE_CHUNKS, LANES), F32)
    ys0[...] = jnp.zeros_like(ys0)
    ys1[...] = jnp.zeros_like(ys1)

    def pad_block(pos0):
        def body(j, carry):
            for d in range(SUBLANES):
                src_ref[pos0 + j * SUBLANES + d] = TRASH_SLOT
            return carry
        lax.fori_loop(0, MOE_ROWS // SUBLANES, body, 0)

    def scan_expert(e, carry):
        nseg, nblocks = carry
        nblk = off_ref[e, 2]
        first = off_ref[e, 3]

        @pl.when(nblk > 0)
        def _():
            seg_expert[nseg] = e
            seg_first[nseg] = first
            pad_block(off_ref[e, 0] + (nblk - 1) * MOE_ROWS)

            def mark(b, c2):
                blk_seg[first + b] = nseg
                return c2
            lax.fori_loop(0, nblk, mark, 0)

            tail_rows = off_ref[e, 1] - off_ref[e, 0] - (nblk - 1) * MOE_ROWS
            blk_seg[first + nblk - 1] = nseg + jnp.where(tail_rows <= HALF_ROWS, HALF_FLAG, 0)

        return nseg + jnp.where(nblk > 0, 1, 0), nblocks + nblk

    nseg, nblocks = lax.fori_loop(0, N_EXPERTS, scan_expert, (jnp.int32(0), jnp.int32(0)))
    pad_block(0)
    pad_block((nblocks + 1) * MOE_ROWS)
    pad_block((nblocks + 2) * MOE_ROWS)
    blk_seg[nblocks] = nseg - 1 + HALF_FLAG
    blk_seg[nblocks + 1] = nseg - 1 + HALF_FLAG

    for cp in weight_copies(seg_expert[0], 0):
        cp.start()

    for cp in table_copies:
        cp.wait()

    nvalid = jnp.where(last, GROUP_SLOTS, GROUP_PROMPT)
    for k in range(TOP_K):
        def fill(j, carry, k=k):
            c0 = k * K_STRIDE + j * SUBLANES
            for d in range(SUBLANES):
                src_ref[dest_ref[c0 + d]] = c0 + d
            return carry
        lax.fori_loop(0, nvalid // SUBLANES, fill, 0)

    for cp in prompt_copies():
        cp.wait()

    @pl.when(last)
    def _():
        for cp in sample_copies():
            cp.wait()

    def token_rows(code):
        slot_id = code & (K_STRIDE - 1)
        return pl.ds(pl.multiple_of(slot_id * LANE_CHUNKS, LANE_CHUNKS), LANE_CHUNKS)

    def gather(b, xs, lo=0, hi=MOE_ROWS):
        base = (b + 1) * MOE_ROWS
        for m in range(lo, hi):
            xs[pl.ds(m, LANE_CHUNKS, stride=XS_STRIDE), :] = h2buf[token_rows(src_ref[base + m]), :]

    def scatter_add(b, ys, lo=0, hi=MOE_ROWS):
        base = (b + 1) * MOE_ROWS
        for m0 in range(lo, hi, SCATTER_BATCH):
            pending = []
            for m in range(m0, m0 + SCATTER_BATCH):
                code = src_ref[base + m]
                rows = token_rows(code)
                pending.append((rows, acc[rows, :] + wts_ref[code] * ys[pl.ds(m, LANE_CHUNKS, stride=XS_STRIDE), :]))
            for rows, val in pending:
                acc[rows, :] = val

    gather_cuts = [round(i * MOE_ROWS / FFN_DOTS) for i in range(FFN_DOTS + 1)]
    scatter_cuts = [SCATTER_BATCH * round(i * (MOE_ROWS // SCATTER_BATCH) / FFN_DOTS) for i in range(FFN_DOTS + 1)]

    def step(b, xs_cur, xs_next, ys_cur, ys_prev):
        tag = blk_seg[b]
        seg = tag & (HALF_FLAG - 1)
        half = tag >= HALF_FLAG
        slot = seg & 1

        @pl.when(jnp.logical_and(b == seg_first[seg], b < nblocks))
        def _():
            for cp in weight_copies(seg_expert[seg], slot):
                cp.wait()

            @pl.when(seg + 1 < nseg)
            def _():
                for cp in weight_copies(seg_expert[seg + 1], 1 - slot):
                    cp.start()

        def work(rows):
            done = [0]

            def row_traffic():
                i = done[0]
                gather(b + 1, xs_next, gather_cuts[i], gather_cuts[i + 1])
                scatter_add(b - 1, ys_prev, scatter_cuts[i], scatter_cuts[i + 1])
                done[0] = i + 1

            x = jnp.concatenate(
                [xs_cur[c * XS_STRIDE:c * XS_STRIDE + rows, :] for c in range(LANE_CHUNKS)], axis=1).astype(BF16)
            acts = []
            for c in range(D_EXPERT // FFN_COLS):
                gc = pl.ds(c * FFN_COLS, FFN_COLS)
                uc = pl.ds(D_EXPERT + c * FFN_COLS, FFN_COLS)
                g = _bdot(x, wgubuf[slot, :, gc]) + bgubuf[slot, :, gc]
                row_traffic()
                u = _bdot(x, wgubuf[slot, :, uc]) + bgubuf[slot, :, uc]
                row_traffic()
                gl = jnp.minimum(g, SWIGLU_LIMIT)
                ul = jnp.clip(u, -SWIGLU_LIMIT, SWIGLU_LIMIT)
                acts.append((ul + 1.0) * (gl * jax.nn.sigmoid(SWIGLU_ALPHA * gl)))
            a = jnp.concatenate(acts, axis=1).astype(BF16)
            for c in range(D_MODEL // FFN_COLS):
                oc = pl.ds(c * FFN_COLS, FFN_COLS)
                y = _bdot(a, wdnbuf[slot, :, oc]) + bdnbuf[slot, :, oc]
                for j in range(FFN_COLS // LANES):
                    lc = c * (FFN_COLS // LANES) + j
                    ys_cur[lc * XS_STRIDE:lc * XS_STRIDE + rows, :] = y[:, j * LANES:(j + 1) * LANES]
                row_traffic()

        pl.when(half)(lambda: work(HALF_ROWS))
        pl.when(jnp.logical_not(half))(lambda: work(MOE_ROWS))

    gather(0, xs0)
    npairs = (nblocks + 1) // 2

    def pair(t, carry):
        step(2 * t, xs0, xs1, ys0, ys1)
        step(2 * t + 1, xs1, xs0, ys1, ys0)
        return carry

    lax.fori_loop(0, npairs, pair, 0)
    scatter_add(2 * npairs - 1, ys1)

    for cp in output_copies():
        cp.start()

    @pl.when(last)
    def _():
        out_s = pltpu.make_async_copy(acc.at[pl.ds(GROUP_ROWS, SAMPLE_ROWS)], x2s_ref, act_sem.at[2])
        out_s.start()
        out_s.wait()

    for cp in output_copies():
        cp.wait()


def _moe(dest, wts, off, h2p, h2s, x1p, x1s, wgu, bgu, wdn, bdn):
    anyspec = pl.BlockSpec(memory_space=pl.ANY)
    dest = dest.reshape(N_GROUPS * TOP_K * K_STRIDE)
    wts = wts.reshape(N_GROUPS * TOP_K * K_STRIDE)
    in_specs = [
        pl.BlockSpec((None, N_EXPERTS, LANES), lambda g: (g, 0, 0), memory_space=pltpu.SMEM),
        anyspec, anyspec, anyspec, anyspec, anyspec, anyspec, anyspec, anyspec, anyspec, anyspec,
    ]
    scratch = [
        pltpu.VMEM((BUF_ROWS, LANES), F32),
        pltpu.VMEM((BUF_ROWS, LANES), F32),
        pltpu.VMEM((2, D_MODEL, 2 * D_EXPERT), BF16),
        pltpu.VMEM((2, 1, 2 * D_EXPERT), F32),
        pltpu.VMEM((2, D_EXPERT, D_MODEL), BF16),
        pltpu.VMEM((2, 1, D_MODEL), F32),
        pltpu.VMEM((LANE_CHUNKS * XS_STRIDE, LANES), F32),
        pltpu.VMEM((LANE_CHUNKS * XS_STRIDE, LANES), F32),
        pltpu.VMEM((LANE_CHUNKS * XS_STRIDE, LANES), F32),
        pltpu.VMEM((LANE_CHUNKS * XS_STRIDE, LANES), F32),
        pltpu.SMEM((TOP_K * K_STRIDE,), I32),
        pltpu.SMEM((TOP_K * K_STRIDE,), F32),
        pltpu.SMEM((POS_TABLE,), I32),
        pltpu.SMEM((N_EXPERTS,), I32),
        pltpu.SMEM((N_EXPERTS,), I32),
        pltpu.SMEM((LANES,), I32),
        pltpu.SemaphoreType.DMA((6,)),
        pltpu.SemaphoreType.DMA((4, 2)),
    ]
    return pl.pallas_call(
        _moe_kernel,
        grid=(N_GROUPS,),
        in_specs=in_specs,
        out_specs=[anyspec, anyspec],
        out_shape=[jax.ShapeDtypeStruct(x1p.shape, F32), jax.ShapeDtypeStruct(x1s.shape, F32)],
        scratch_shapes=scratch,
        compiler_params=pltpu.CompilerParams(dimension_semantics=("arbitrary",), vmem_limit_bytes=VMEM_LIMIT),
        name="moe",
    )(off, dest, wts, h2p, h2s, x1p, x1s, wgu, bgu, wdn, bdn)


def _ple_final_kernel(x2_ref, ple_ref, wple_ref, gple_ref, wpg_ref, gfin_ref, y_ref):
    rows = y_ref.shape[0]
    x2 = _load_token_major(x2_ref, rows)
    e = _rmsnorm(_bdot(ple_ref[...].astype(BF16), wple_ref[...]), gple_ref[...])
    x3 = x2 + jax.nn.sigmoid(_bdot(x2.astype(BF16), wpg_ref[...])) * e
    y_ref[...] = _rmsnorm(x3, gfin_ref[...])


def _ple_final(x2_tm, ple, wple, gple, wpg, gfin, tile):
    n = ple.shape[0]
    return pl.pallas_call(
        _ple_final_kernel,
        grid=(n // tile,),
        in_specs=[
            pl.BlockSpec((tile * LANE_CHUNKS, LANES), lambda i: (i, 0)),
            pl.BlockSpec((tile, PLE_DIM), lambda i: (i, 0)),
            _full((PLE_DIM, D_MODEL)),
            _full((1, D_MODEL)),
            _full((D_MODEL, D_MODEL)),
            _full((1, D_MODEL)),
        ],
        out_specs=pl.BlockSpec((tile, D_MODEL), lambda i: (i, 0)),
        out_shape=jax.ShapeDtypeStruct((n, D_MODEL), F32),
        compiler_params=pltpu.CompilerParams(dimension_semantics=("arbitrary",), vmem_limit_bytes=VMEM_LIMIT),
        name="ple_final",
    )(x2_tm, ple, wple, gple, wpg, gfin)


def _rope_tables(pos):
    half = HEAD_DIM // 2
    inv = ROPE_THETA ** (-jnp.arange(half, dtype=F32) / half)
    ang = pos.astype(F32)[:, None] * inv[None, :]
    cos, sin = jnp.cos(ang), jnp.sin(ang)
    cos2 = jnp.concatenate([cos, cos, cos, cos], axis=1)
    sin2 = jnp.concatenate([-sin, sin, -sin, sin], axis=1)
    return cos2, sin2


def _layout_w_in(w_in):
    o_q = 2 * A_WIDTH
    wq = w_in[:, o_q:o_q + Q_WIDTH].reshape(D_MODEL, N_HEADS, HEAD_DIM) * (HEAD_DIM ** -0.5)
    kv_head = (jnp.arange(N_HEADS) // GQA_GROUP)[None, :, None]
    wq_pad = jnp.concatenate([jnp.where(kv_head == h, wq, 0.0) for h in range(N_KV_HEADS)], axis=-1)
    return jnp.concatenate([w_in[:, :o_q], wq_pad.reshape(D_MODEL, QPAD_WIDTH), w_in[:, o_q + Q_WIDTH:]], axis=1)


def _router_passes(w_router):
    hi = w_router.astype(BF16)
    lo = (w_router - hi.astype(F32)).astype(BF16)
    w3 = jnp.concatenate([hi, lo, hi], axis=0)
    return jnp.pad(w3, ((0, 0), (0, LANES - N_EXPERTS)))


def _prep_weights(g_mix, w_in, a_ln_g, a_ln_b, a_ws, a_bs, w_pa, w_pb, w_o, g_ffn, w_router, b_router):
    causal = jnp.tril(jnp.ones((CHUNK, CHUNK), dtype=bool))
    return dict(
        gmix=g_mix.reshape(1, D_MODEL),
        win=_layout_w_in(w_in).astype(BF16),
        lng=a_ln_g.reshape(1, A_WIDTH),
        lnb=a_ln_b.reshape(1, A_WIDTH),
        ws=jnp.where(causal[None], a_ws, 0.0).astype(BF16),
        bsf=jnp.repeat(jnp.transpose(a_bs), A_GROUP_DIM, axis=1),
        wpa=w_pa.astype(BF16),
        wpb=w_pb.astype(BF16),
        wo=w_o.astype(BF16),
        gffn=g_ffn.reshape(1, D_MODEL),
        wrt=_router_passes(w_router),
        br=b_router.reshape(N_EXPERTS, 1),
    )


def kernel(x_prompt, x_sample, cache_win_k, cache_win_v, p_prompt, p_sample, g_mix, w_in, a_ln_g, a_ln_b, a_ws, a_bs, sinks, w_pa, w_pb, w_o, g_ffn, w_router, b_router, w_gu, b_gu, w_down, b_down, w_ple, g_ple, w_ple_gate, g_final):
    W = _prep_weights(g_mix[0], w_in[0], a_ln_g[0], a_ln_b[0], a_ws[0], a_bs[0], w_pa[0], w_pb[0], w_o[0],
                      g_ffn[0], w_router[0], b_router[0])
    cos_p, sin_p = _rope_tables(jnp.arange(SEQ, dtype=I32))
    cos_s, sin_s = _rope_tables(jnp.full((1,), PAST_LEN, I32))
    x1p, h2p, logits_p, kwin_p, vwin_p, wgu16, wdn16 = _prompt_front(
        x_prompt.reshape(N_PROMPT, D_MODEL), cos_p, sin_p, W["gmix"], W["win"], W["lng"], W["lnb"],
        W["ws"], W["bsf"], sinks[0], _band_bias(), W["wpa"], W["wpb"], W["wo"], W["gffn"], W["wrt"], W["br"],
        w_gu[0].reshape(N_EXPERTS * D_MODEL, 2 * D_EXPERT), w_down[0].reshape(N_EXPERTS * D_EXPERT, D_MODEL))

    wdiag = jnp.repeat(a_ws[0, :, 0, 0], A_GROUP_DIM)[None, :].astype(BF16)
    bs0 = jnp.repeat(a_bs[0, :, 0], A_GROUP_DIM)[None, :]
    x1s, h2s, logits_s, kwin_s, vwin_s, va_s = _sample_front(
        x_sample.reshape(DEC_BATCH, D_MODEL), cos_s, sin_s, W["gmix"], W["win"], W["lng"], W["lnb"], wdiag, bs0,
        sinks[0], cache_win_k[0].reshape(DEC_BATCH, WINDOW, KV_WIDTH), cache_win_v[0].reshape(DEC_BATCH, WINDOW, KV_WIDTH),
        W["wpa"], W["wpb"], W["wo"], W["gffn"], W["wrt"], W["br"])

    dest, wts, off = _route_plan(logits_p, logits_s)
    x2p, x2s = _moe(dest, wts, off, h2p, h2s, x1p, x1s,
                    wgu16.reshape(N_EXPERTS, D_MODEL, 2 * D_EXPERT), b_gu[0].reshape(N_EXPERTS, 1, 2 * D_EXPERT),
                    wdn16.reshape(N_EXPERTS, D_EXPERT, D_MODEL), b_down[0].reshape(N_EXPERTS, 1, D_MODEL))

    wple = w_ple[0].astype(BF16)
    gple = g_ple[0].reshape(1, D_MODEL)
    wpg = w_ple_gate[0].astype(BF16)
    gfin = g_final.reshape(1, D_MODEL)
    y_p = _ple_final(x2p, p_prompt[0].reshape(N_PROMPT, PLE_DIM), wple, gple, wpg, gfin, 2 * TM)
    y_s = _ple_final(x2s, p_sample[0].reshape(DEC_BATCH, PLE_DIM), wple, gple, wpg, gfin, DEC_BATCH)

    return (
        y_p.reshape(BATCH, SEQ, D_MODEL),
        y_s.reshape(DEC_BATCH, 1, D_MODEL),
        kwin_p.reshape(1, BATCH, WINDOW, N_KV_HEADS, HEAD_DIM),
        vwin_p.reshape(1, BATCH, WINDOW, N_KV_HEADS, HEAD_DIM),
        kwin_s.reshape(1, DEC_BATCH, WINDOW, N_KV_HEADS, HEAD_DIM),
        vwin_s.reshape(1, DEC_BATCH, WINDOW, N_KV_HEADS, HEAD_DIM),
        va_s.reshape(1, DEC_BATCH, 1, A_WIDTH),
    )
```

```python
import numpy as np

import jax
import jax.numpy as jnp
from jax import lax
from jax.experimental import pallas as pl
from jax.experimental.pallas import tpu as pltpu

F32 = jnp.float32
BF16 = jnp.bfloat16
I32 = jnp.int32

D_MODEL = 1024
BATCH = 4
SEQ = 4096
DEC_BATCH = 128
PAST_LEN = 8192
CHUNK = 128
A_GROUPS = 4
A_GROUP_DIM = 128
A_WIDTH = A_GROUPS * A_GROUP_DIM
N_HEADS = 8
N_KV_HEADS = 2
HEAD_DIM = 64
Q_WIDTH = N_HEADS * HEAD_DIM
KV_WIDTH = N_KV_HEADS * HEAD_DIM
GQA_GROUP = N_HEADS // N_KV_HEADS
WINDOW = 128
ROPE_THETA = 10000.0
N_EXPERTS = 32
TOP_K = 4
D_EXPERT = D_MODEL
SWIGLU_ALPHA = 1.702
SWIGLU_LIMIT = 7.0
PLE_DIM = 256
RMS_EPS = 1e-5
LN_EPS = 1e-5

LANES = 128

QPAD_WIDTH = N_HEADS * LANES
O_Q = 2 * A_WIDTH
O_K = O_Q + QPAD_WIDTH
O_V = O_K + KV_WIDTH
O_GA = O_V + KV_WIDTH
O_GB = O_GA + D_MODEL
IN_COLS = O_GB + D_MODEL
SUBLANES = 8
LANE_CHUNKS = D_MODEL // LANES
VMEM_LIMIT = 56 * 1024 * 1024

N_PROMPT = BATCH * SEQ
TM = 256
TILES_PER_SEQ = SEQ // TM
BLOCKS_PER_TILE = TM // WINDOW
FRONT_STEPS = N_PROMPT // TM
CAST_ROWS = N_EXPERTS * D_MODEL // FRONT_STEPS
CAST_SPLIT = 4
IN_PROJ_DOTS = 5
BF16_ROWS = 16
CAST_CUTS = [BF16_ROWS * round(i * (CAST_ROWS // BF16_ROWS) / IN_PROJ_DOTS) for i in range(IN_PROJ_DOTS + 1)]

N_GROUPS = 4
GROUP_PROMPT = N_PROMPT // N_GROUPS
GROUP_SLOTS = GROUP_PROMPT + DEC_BATCH
GROUP_ASSIGN = GROUP_SLOTS * TOP_K
SLOT_TILES = GROUP_SLOTS // LANES
MOE_ROWS = 384
XS_STRIDE = MOE_ROWS + SUBLANES
ROW_STEP = 128
ROW_VARIANTS = MOE_ROWS // ROW_STEP
SEG_BITS = 6
assert N_EXPERTS <= (1 << SEG_BITS) and MOE_ROWS % ROW_STEP == 0
SLOT_BITS = 13
K_STRIDE = 1 << SLOT_BITS
assert GROUP_SLOTS < K_STRIDE
MAX_BLOCKS = GROUP_ASSIGN // MOE_ROWS + N_EXPERTS
POS_TABLE = 1 << 15
assert (MAX_BLOCKS + 3) * MOE_ROWS <= POS_TABLE
assert MAX_BLOCKS + 2 <= LANES


def _bdot(a, b):
    return jnp.dot(a, b, preferred_element_type=F32)


def _rmsnorm(x, g):
    return x * lax.rsqrt(jnp.mean(x * x, axis=-1, keepdims=True) + RMS_EPS) * g


def _gelu(x):
    return 0.5 * x * (1.0 + lax.erf(x * (0.5 ** 0.5)))


def _group_layernorm(v, g, b):
    cols = []
    for gi in range(A_GROUPS):
        s = slice(gi * A_GROUP_DIM, (gi + 1) * A_GROUP_DIM)
        vg = v[:, s]
        mu = jnp.mean(vg, axis=-1, keepdims=True)
        d = vg - mu
        var = jnp.mean(d * d, axis=-1, keepdims=True)
        cols.append(d * lax.rsqrt(var + LN_EPS) * g[:, s] + b[:, s])
    return jnp.concatenate(cols, axis=1)


def _rope(x, cos, sin_signed):
    width = x.shape[1]
    reps = width // LANES
    cosf = jnp.concatenate([cos] * reps, axis=1) if reps > 1 else cos
    sinf = jnp.concatenate([sin_signed] * reps, axis=1) if reps > 1 else sin_signed
    half = HEAD_DIM // 2
    lane = lax.broadcasted_iota(I32, x.shape, 1)
    up = pltpu.roll(x, width - half, 1)
    down = pltpu.roll(x, half, 1)
    partner = jnp.where((lane & (HEAD_DIM - 1)) < half, up, down)
    return x * cosf + partner * sinf


def _in_projection(x, gmix_ref, win_ref, lng_ref, lnb_ref, cos, sin_signed, after_dot=lambda: None):
    hb = _rmsnorm(x, gmix_ref[...]).astype(BF16)

    def dot_cols(lo, hi):
        z = _bdot(hb, win_ref[:, lo:hi])
        after_dot()
        return z

    u = _gelu(dot_cols(0, A_WIDTH))
    va = _group_layernorm(_gelu(dot_cols(A_WIDTH, O_Q)), lng_ref[...], lnb_ref[...])
    zq = jnp.concatenate([dot_cols(O_Q, O_Q + QPAD_WIDTH // 2), dot_cols(O_Q + QPAD_WIDTH // 2, O_K)], axis=1)
    zkv = dot_cols(O_K, O_GA)
    q = _rope(zq, cos, sin_signed)
    k = _rope(zkv[:, :KV_WIDTH], cos, sin_signed)
    v = zkv[:, KV_WIDTH:]
    return u, va, q, k, v, hb


def _gate_preact(hb, win_ref, part, parts):
    width = 2 * D_MODEL // parts
    return _bdot(hb, win_ref[:, O_GA + part * width:O_GA + (part + 1) * width])


def _gates(preacts):
    zg = jnp.concatenate(preacts, axis=1)
    return jax.nn.sigmoid(zg[:, :D_MODEL]), jax.nn.sigmoid(zg[:, D_MODEL:])


def _merge_and_route(x, ya_in, att, gate_a, gate_b, wpa_ref, wpb_ref, wo_ref, gffn_ref, wr3_ref, br_ref,
                     x1_ref, h2_ref):
    ya = _bdot(ya_in.astype(BF16), wpa_ref[...])
    yb = _bdot(att.astype(BF16), wpb_ref[...])
    mix = (gate_a * ya + gate_b * yb).astype(BF16)
    x1 = x + _bdot(mix, wo_ref[...])
    _store_token_major(x1_ref, x1)
    h2 = _rmsnorm(x1, gffn_ref[...])
    _store_token_major(h2_ref, h2)
    hi = h2.astype(BF16)
    lo = (h2 - hi.astype(F32)).astype(BF16)
    logits = _bdot(jnp.concatenate([hi, hi, lo], axis=1), wr3_ref[...])
    return jnp.transpose(logits)[:N_EXPERTS, :] + br_ref[...]


def _top4_softmax(logits):
    eid = lax.broadcasted_iota(I32, logits.shape, 0)
    vals, idxs = [], []
    for _ in range(TOP_K):
        m = jnp.max(logits, axis=0, keepdims=True)
        idx = jnp.min(jnp.where(logits == m, eid, N_EXPERTS), axis=0, keepdims=True)
        logits = jnp.where(eid == idx, -jnp.inf, logits)
        vals.append(m)
        idxs.append(idx)
    es = [jnp.exp(v - vals[0]) for v in vals]
    inv = 1.0 / (es[0] + es[1] + es[2] + es[3])
    return jnp.concatenate(idxs, axis=0), jnp.concatenate([e * inv for e in es], axis=0)


def _store_token_major(ref, val):
    rows = val.shape[0]
    for c in range(LANE_CHUNKS):
        ref[pl.ds(c, rows, stride=LANE_CHUNKS), :] = val[:, c * LANES:(c + 1) * LANES]


def _load_token_major(ref, rows):
    return jnp.concatenate([ref[pl.ds(c, rows, stride=LANE_CHUNKS), :] for c in range(LANE_CHUNKS)], axis=1)


def _band_attention(qpad, k, v, k_prev, v_prev, sinks_ref, bias_ref, seq_start, after_scores):
    kb = jnp.concatenate([k_prev, k], axis=0).astype(BF16)
    vt = jnp.transpose(jnp.concatenate([v_prev, v], axis=0)).astype(BF16)
    qb = qpad.astype(BF16)
    lane = lax.broadcasted_iota(I32, (1, GQA_GROUP * WINDOW), 1)
    blocks = []
    for b in range(BLOCKS_PER_TILE):
        bias = bias_ref[jnp.where(seq_start, 1, 0)] if b == 0 else bias_ref[0]
        keys = kb[b * WINDOW:(b + 2) * WINDOW, :]
        pieces = []
        for h in range(N_KV_HEADS):
            qh = jnp.concatenate(
                [qb[b * WINDOW:(b + 1) * WINDOW, (h * GQA_GROUP + j) * LANES:(h * GQA_GROUP + j + 1) * LANES]
                 for j in range(GQA_GROUP)], axis=0)
            st = lax.dot_general(keys, qh, (((1,), (1,)), ((), ())), preferred_element_type=F32) + bias
            after_scores(b * N_KV_HEADS + h)
            sink = jnp.zeros((1, GQA_GROUP * WINDOW), F32)
            for j in range(GQA_GROUP):
                sink = jnp.where(lane // WINDOW == j, sinks_ref[h * GQA_GROUP + j], sink)
            m = jnp.maximum(jnp.max(st, axis=0, keepdims=True), sink)
            e = jnp.exp(st - m)
            inv = 1.0 / (jnp.sum(e, axis=0, keepdims=True) + jnp.exp(sink - m))
            ot = _bdot(vt[h * HEAD_DIM:(h + 1) * HEAD_DIM, b * WINDOW:(b + 2) * WINDOW], (e * inv).astype(BF16))
            pieces.extend(ot[:, j * WINDOW:(j + 1) * WINDOW] for j in range(GQA_GROUP))
        blocks.append(jnp.transpose(jnp.concatenate(pieces, axis=0)))
    return jnp.concatenate(blocks, axis=0)


def _band_bias():
    kj = np.arange(2 * WINDOW)[None, :, None]
    qi = (np.arange(GQA_GROUP * WINDOW) % WINDOW)[None, None, :]
    lo = (np.arange(2) * WINDOW)[:, None, None]
    valid = (kj > qi) & (kj <= qi + WINDOW) & (kj >= lo)
    return jnp.asarray(np.where(valid, 0.0, -np.inf).astype(np.float32))


def _prompt_front_kernel(x_ref, cos_ref, sin_ref, gmix_ref, win_ref, lng_ref, lnb_ref, ws_ref, bsf_ref,
                         sinks_ref, bias_ref, wpa_ref, wpb_ref, wo_ref, gffn_ref, wr3_ref, br_ref, wgu32_ref, wdn32_ref,
                         x1_ref, h2_ref, logits_ref, kwin_ref, vwin_ref, wgu16_ref, wdn16_ref,
                         kprev_ref, vprev_ref, gu_in, dn_in, gu_out, dn_out, cast_sem):
    i = pl.program_id(0)
    seq_start = (i % TILES_PER_SEQ) == 0

    def cast_rows(c, j):
        part = CAST_ROWS // CAST_SPLIT
        return pl.ds(pl.multiple_of(c * CAST_ROWS + j * part, part), part), pl.ds(j * part, part)

    def cast_in(c, slot):
        cps = []
        for j in range(CAST_SPLIT):
            hbm, loc = cast_rows(c, j)
            cps.append(pltpu.make_async_copy(wgu32_ref.at[hbm], gu_in.at[slot, loc], cast_sem.at[0, slot]))
            cps.append(pltpu.make_async_copy(wdn32_ref.at[hbm], dn_in.at[slot, loc], cast_sem.at[1, slot]))
        return cps

    def cast_out(c, slot):
        cps = []
        for j in range(CAST_SPLIT):
            hbm, loc = cast_rows(c, j)
            cps.append(pltpu.make_async_copy(gu_out.at[slot, loc], wgu16_ref.at[hbm], cast_sem.at[2, slot]))
            cps.append(pltpu.make_async_copy(dn_out.at[slot, loc], wdn16_ref.at[hbm], cast_sem.at[3, slot]))
        return cps

    slot = i & 1

    @pl.when(i == 0)
    def _():
        for cp in cast_in(0, 0):
            cp.start()

    @pl.when(i + 1 < FRONT_STEPS)
    def _():
        for cp in cast_in(i + 1, 1 - slot):
            cp.start()

    for cp in cast_in(i, slot):
        cp.wait()

    @pl.when(i >= 2)
    def _():
        for cp in cast_out(i - 2, slot):
            cp.wait()


    @pl.when(seq_start)
    def _():
        kprev_ref[...] = jnp.zeros_like(kprev_ref)
        vprev_ref[...] = jnp.zeros_like(vprev_ref)

    x = x_ref[...]
    cast_done = [0]

    def cast_slice():
        i = cast_done[0]
        r = pl.ds(CAST_CUTS[i], CAST_CUTS[i + 1] - CAST_CUTS[i])
        gu_out[slot, r, :] = gu_in[slot, r, :].astype(BF16)
        dn_out[slot, r, :] = dn_in[slot, r, :].astype(BF16)
        cast_done[0] += 1

    u, va, q, k, v, hb = _in_projection(
        x, gmix_ref, win_ref, lng_ref, lnb_ref, cos_ref[...], sin_ref[...], cast_slice)
    assert cast_done[0] == IN_PROJ_DOTS
    for cp in cast_out(i, slot):
        cp.start()

    units = BLOCKS_PER_TILE * N_KV_HEADS
    preacts = []
    att = _band_attention(q, k, v, kprev_ref[...], vprev_ref[...], sinks_ref, bias_ref, seq_start,
                          lambda unit: preacts.append(_gate_preact(hb, win_ref, unit, units)))
    gate_a, gate_b = _gates(preacts)

    k_last, v_last = k[TM - WINDOW:], v[TM - WINDOW:]
    kprev_ref[...] = k_last
    vprev_ref[...] = v_last
    kwin_ref[0] = k_last
    vwin_ref[0] = v_last

    vab = va.astype(BF16)
    zc = jnp.concatenate(
        [jnp.concatenate(
            [_bdot(ws_ref[g], vab[b * CHUNK:(b + 1) * CHUNK, g * A_GROUP_DIM:(g + 1) * A_GROUP_DIM])
             for g in range(A_GROUPS)], axis=1) + bsf_ref[...]
         for b in range(BLOCKS_PER_TILE)], axis=0)

    logits_ref[...] = _merge_and_route(x, u * zc, att, gate_a, gate_b,
                                       wpa_ref, wpb_ref, wo_ref, gffn_ref, wr3_ref, br_ref, x1_ref, h2_ref)

    @pl.when(i == FRONT_STEPS - 1)
    def _():
        for cp in cast_out(i - 1, 1 - slot) + cast_out(i, slot):
            cp.wait()


def _full(shape):
    return pl.BlockSpec(shape, lambda i: (0,) * len(shape))


def _prompt_front(x, cos, sin, gmix, win, lng, lnb, ws, bsf, sinks, bias, wpa, wpb, wo, gffn, wrt, br, wgu32, wdn32):
    n = x.shape[0]
    assert n == N_PROMPT
    grid = (FRONT_STEPS,)
    anyspec = pl.BlockSpec(memory_space=pl.ANY)
    in_specs = [
        pl.BlockSpec((TM, D_MODEL), lambda i: (i, 0)),
        pl.BlockSpec((TM, LANES), lambda i: (i % TILES_PER_SEQ, 0)),
        pl.BlockSpec((TM, LANES), lambda i: (i % TILES_PER_SEQ, 0)),
        _full((1, D_MODEL)),
        _full((D_MODEL, IN_COLS)),
        _full((1, A_WIDTH)),
        _full((1, A_WIDTH)),
        _full((A_GROUPS, CHUNK, CHUNK)),
        _full((CHUNK, A_WIDTH)),
        pl.BlockSpec(memory_space=pltpu.SMEM),
        _full((2, 2 * WINDOW, GQA_GROUP * WINDOW)),
        _full((A_WIDTH, D_MODEL)),
        _full((Q_WIDTH, D_MODEL)),
        _full((D_MODEL, D_MODEL)),
        _full((1, D_MODEL)),
        _full((3 * D_MODEL, LANES)),
        _full((N_EXPERTS, 1)),
        anyspec,
        anyspec,
    ]
    out_shape = [
        jax.ShapeDtypeStruct((n * LANE_CHUNKS, LANES), F32),
        jax.ShapeDtypeStruct((n * LANE_CHUNKS, LANES), F32),
        jax.ShapeDtypeStruct((N_EXPERTS, n), F32),
        jax.ShapeDtypeStruct((n // SEQ, WINDOW, KV_WIDTH), F32),
        jax.ShapeDtypeStruct((n // SEQ, WINDOW, KV_WIDTH), F32),
        jax.ShapeDtypeStruct(wgu32.shape, BF16),
        jax.ShapeDtypeStruct(wdn32.shape, BF16),
    ]
    out_specs = [
        pl.BlockSpec((TM * LANE_CHUNKS, LANES), lambda i: (i, 0)),
        pl.BlockSpec((TM * LANE_CHUNKS, LANES), lambda i: (i, 0)),
        pl.BlockSpec((N_EXPERTS, TM), lambda i: (0, i)),
        pl.BlockSpec((1, WINDOW, KV_WIDTH), lambda i: (i // TILES_PER_SEQ, 0, 0)),
        pl.BlockSpec((1, WINDOW, KV_WIDTH), lambda i: (i // TILES_PER_SEQ, 0, 0)),
        anyspec,
        anyspec,
    ]
    scratch = [
        pltpu.VMEM((WINDOW, KV_WIDTH), F32),
        pltpu.VMEM((WINDOW, KV_WIDTH), F32),
        pltpu.VMEM((2, CAST_ROWS, 2 * D_EXPERT), F32),
        pltpu.VMEM((2, CAST_ROWS, D_MODEL), F32),
        pltpu.VMEM((2, CAST_ROWS, 2 * D_EXPERT), BF16),
        pltpu.VMEM((2, CAST_ROWS, D_MODEL), BF16),
        pltpu.SemaphoreType.DMA((4, 2)),
    ]
    return pl.pallas_call(
        _prompt_front_kernel,
        grid=grid,
        in_specs=in_specs,
        out_specs=out_specs,
        out_shape=out_shape,
        scratch_shapes=scratch,
        compiler_params=pltpu.CompilerParams(dimension_semantics=("arbitrary",), vmem_limit_bytes=VMEM_LIMIT),
        name="prompt_front",
    )(x, cos, sin, gmix, win, lng, lnb, ws, bsf, sinks, bias, wpa, wpb, wo, gffn, wrt, br, wgu32, wdn32)


SAMPLE_STEP = 16
SAMPLE_STEPS = DEC_BATCH // SAMPLE_STEP


def _sample_kernel(x_ref, cos_ref, sin_ref, gmix_ref, win_ref, lng_ref, lnb_ref, wdiag_ref, bs0_ref, sinks_ref,
                   kc_ref, vc_ref, wpa_ref, wpb_ref, wo_ref, gffn_ref, wr3_ref, br_ref,
                   x1_ref, h2_ref, logits_ref, kwin_ref, vwin_ref, va_ref,
                   q_s, k_s, v_s, yain_s, ga_s, gb_s, att_s):
    i = pl.program_id(0)

    @pl.when(i == 0)
    def _():
        x = x_ref[...]
        cos = jnp.broadcast_to(cos_ref[...], (DEC_BATCH, LANES))
        sin = jnp.broadcast_to(sin_ref[...], (DEC_BATCH, LANES))
        u, va, q, k, v, hb = _in_projection(x, gmix_ref, win_ref, lng_ref, lnb_ref, cos, sin)
        gate_a, gate_b = _gates([_gate_preact(hb, win_ref, 0, 1)])
        va_ref[...] = va
        z = wdiag_ref[...].astype(F32) * va.astype(BF16).astype(F32) + bs0_ref[...]
        yain_s[...] = u * z
        q_s[...] = q
        k_s[...] = k
        v_s[...] = v
        ga_s[...] = gate_a
        gb_s[...] = gate_b

    r0 = pl.multiple_of(i * SAMPLE_STEP, SAMPLE_STEP)
    kwin = jnp.concatenate([kc_ref[:, 1:, :], k_s[pl.ds(r0, SAMPLE_STEP), :][:, None, :]], axis=1)
    vwin = jnp.concatenate([vc_ref[:, 1:, :], v_s[pl.ds(r0, SAMPLE_STEP), :][:, None, :]], axis=1)
    kwin_ref[...] = kwin
    vwin_ref[...] = vwin

    q16 = q_s[pl.ds(r0, SAMPLE_STEP), :]
    lane = lax.broadcasted_iota(I32, (SAMPLE_STEP, LANES), 1)
    heads = [q16[:, hq * LANES:(hq + 1) * LANES] for hq in range(N_HEADS)]
    qpad = pltpu.einshape("hbd->bhd", jnp.stack(heads, axis=0)).astype(BF16)
    s = jnp.einsum("bhd,bkd->bhk", qpad, kwin.astype(BF16), preferred_element_type=F32)
    hid = lax.broadcasted_iota(I32, (1, N_HEADS, 1), 1)
    sink = jnp.zeros((1, N_HEADS, 1), F32)
    for hq in range(N_HEADS):
        sink = jnp.where(hid == hq, sinks_ref[hq], sink)
    m = jnp.maximum(jnp.max(s, axis=-1, keepdims=True), sink)
    e = jnp.exp(s - m)
    inv = 1.0 / (jnp.sum(e, axis=-1, keepdims=True) + jnp.exp(sink - m))
    o = jnp.einsum("bhk,bkd->bhd", (e * inv).astype(BF16), vwin.astype(BF16), preferred_element_type=F32)
    o = pltpu.einshape("bhd->hbd", o)
    chunks = []
    for c in range(N_HEADS // 2):
        parts = []
        for p in range(2):
            hq = 2 * c + p
            oh = o[hq]
            if p != hq // GQA_GROUP:
                oh = pltpu.roll(oh, HEAD_DIM, 1)
            parts.append(oh)
        chunks.append(jnp.where(lane < HEAD_DIM, parts[0], parts[1]))
    att_s[pl.ds(r0, SAMPLE_STEP), :] = jnp.concatenate(chunks, axis=1)

    @pl.when(i == SAMPLE_STEPS - 1)
    def _():
        logits_ref[...] = _merge_and_route(
            x_ref[...], yain_s[...], att_s[...], ga_s[...], gb_s[...],
            wpa_ref, wpb_ref, wo_ref, gffn_ref, wr3_ref, br_ref, x1_ref, h2_ref)


def _sample_front(x, cos, sin, gmix, win, lng, lnb, wdiag, bs0, sinks, kc, vc, wpa, wpb, wo, gffn, wrt, br):
    n = DEC_BATCH
    cache_spec = pl.BlockSpec((SAMPLE_STEP, WINDOW, KV_WIDTH), lambda i: (i, 0, 0))
    in_specs = [
        _full((n, D_MODEL)),
        _full((1, LANES)),
        _full((1, LANES)),
        _full((1, D_MODEL)),
        _full((D_MODEL, IN_COLS)),
        _full((1, A_WIDTH)),
        _full((1, A_WIDTH)),
        _full((1, A_WIDTH)),
        _full((1, A_WIDTH)),
        pl.BlockSpec(memory_space=pltpu.SMEM),
        cache_spec,
        cache_spec,
        _full((A_WIDTH, D_MODEL)),
        _full((Q_WIDTH, D_MODEL)),
        _full((D_MODEL, D_MODEL)),
        _full((1, D_MODEL)),
        _full((3 * D_MODEL, LANES)),
        _full((N_EXPERTS, 1)),
    ]
    out_shape = [
        jax.ShapeDtypeStruct((n * LANE_CHUNKS, LANES), F32),
        jax.ShapeDtypeStruct((n * LANE_CHUNKS, LANES), F32),
        jax.ShapeDtypeStruct((N_EXPERTS, n), F32),
        jax.ShapeDtypeStruct((n, WINDOW, KV_WIDTH), F32),
        jax.ShapeDtypeStruct((n, WINDOW, KV_WIDTH), F32),
        jax.ShapeDtypeStruct((n, A_WIDTH), F32),
    ]
    out_specs = [
        _full((n * LANE_CHUNKS, LANES)),
        _full((n * LANE_CHUNKS, LANES)),
        _full((N_EXPERTS, n)),
        cache_spec,
        cache_spec,
        _full((n, A_WIDTH)),
    ]
    scratch = [
        pltpu.VMEM((n, QPAD_WIDTH), F32), pltpu.VMEM((n, KV_WIDTH), F32), pltpu.VMEM((n, KV_WIDTH), F32),
        pltpu.VMEM((n, A_WIDTH), F32), pltpu.VMEM((n, D_MODEL), F32), pltpu.VMEM((n, D_MODEL), F32),
        pltpu.VMEM((n, Q_WIDTH), F32),
    ]
    return pl.pallas_call(
        _sample_kernel,
        grid=(SAMPLE_STEPS,),
        in_specs=in_specs,
        out_specs=out_specs,
        out_shape=out_shape,
        scratch_shapes=scratch,
        compiler_params=pltpu.CompilerParams(dimension_semantics=("arbitrary",), vmem_limit_bytes=VMEM_LIMIT),
        name="sample_front",
    )(x, cos, sin, gmix, win, lng, lnb, wdiag, bs0, sinks, kc, vc, wpa, wpb, wo, gffn, wrt, br)


def _route_plan_kernel(lp_ref, ls_ref, dest_ref, wts_ref, off_ref):
    g = pl.program_id(0)
    topi, topw = _top4_softmax(jnp.concatenate([lp_ref[...], ls_ref[...]], axis=1))
    slot = lax.broadcasted_iota(I32, (TOP_K, GROUP_SLOTS), 1)
    eall = jnp.where(jnp.logical_or(slot < GROUP_PROMPT, g == N_GROUPS - 1), topi, N_EXPERTS)
    wts_ref[:, 0:GROUP_SLOTS] = topw
    wts_ref[:, GROUP_SLOTS:] = jnp.zeros((TOP_K, K_STRIDE - GROUP_SLOTS), F32)
    dest_ref[:, GROUP_SLOTS:] = jnp.zeros((TOP_K, K_STRIDE - GROUP_SLOTS), I32)
    eid = lax.broadcasted_iota(I32, (N_EXPERTS, GROUP_SLOTS), 0)
    onehots = [eall[k:k + 1, :] == eid for k in range(TOP_K)]
    count = jnp.zeros((N_EXPERTS, GROUP_SLOTS), F32)
    for oh in onehots:
        count = count + oh.astype(F32)
    total = jnp.broadcast_to(jnp.sum(count, axis=1, keepdims=True), (N_EXPERTS, LANES))
    padded = total + (MOE_ROWS - 1)
    nblk = jnp.floor(padded * (1.0 / MOE_ROWS))
    rem = padded - nblk * MOE_ROWS
    nblk = jnp.where(rem >= MOE_ROWS, nblk + 1.0, jnp.where(rem < 0.0, nblk - 1.0, nblk))
    r = lax.broadcasted_iota(I32, (N_EXPERTS, N_EXPERTS), 0)
    c = lax.broadcasted_iota(I32, (N_EXPERTS, N_EXPERTS), 1)
    first_blk = lax.dot_general((c < r).astype(F32), nblk, (((1,), (0,)), ((), ())),
                                precision=lax.Precision.HIGHEST, preferred_element_type=F32)
    start = (first_blk + 1.0) * MOE_ROWS
    lane = lax.broadcasted_iota(I32, (N_EXPERTS, LANES), 1)
    info = jnp.where(lane == 0, start, jnp.where(lane == 1, start + total, jnp.where(lane == 2, nblk, first_blk)))
    off_ref[...] = info.astype(I32)
    ti = lax.broadcasted_iota(I32, (LANES, LANES), 0)
    tj = lax.broadcasted_iota(I32, (LANES, LANES), 1)
    before = (ti < tj).astype(BF16)
    ones = jnp.ones((LANES, LANES), BF16)
    running = start
    for t in range(SLOT_TILES):
        sl = slice(t * LANES, (t + 1) * LANES)
        cb = count[:, sl].astype(BF16)
        pos = running + _bdot(cb, before)
        rows = [jnp.sum(jnp.where(oh[:, sl], pos, 0.0), axis=0, keepdims=True) for oh in onehots]
        dest_ref[:, sl] = jnp.concatenate(rows, axis=0).astype(I32)
        running = running + _bdot(cb, ones)


def _route_plan(logits_p, logits_s):
    in_specs = [
        pl.BlockSpec((N_EXPERTS, GROUP_PROMPT), lambda g: (0, g)),
        pl.BlockSpec((N_EXPERTS, DEC_BATCH), lambda g: (0, 0)),
    ]
    out_shape = [
        jax.ShapeDtypeStruct((N_GROUPS, TOP_K, K_STRIDE), I32),
        jax.ShapeDtypeStruct((N_GROUPS, TOP_K, K_STRIDE), F32),
        jax.ShapeDtypeStruct((N_GROUPS, N_EXPERTS, LANES), I32),
    ]
    out_specs = [
        pl.BlockSpec((None, TOP_K, K_STRIDE), lambda g: (g, 0, 0)),
        pl.BlockSpec((None, TOP_K, K_STRIDE), lambda g: (g, 0, 0)),
        pl.BlockSpec((None, N_EXPERTS, LANES), lambda g: (g, 0, 0)),
    ]
    return pl.pallas_call(
        _route_plan_kernel,
        grid=(N_GROUPS,),
        in_specs=in_specs,
        out_specs=out_specs,
        out_shape=out_shape,
        compiler_params=pltpu.CompilerParams(dimension_semantics=("arbitrary",)),
        name="route_plan",
    )(logits_p, logits_s)


GROUP_ROWS = GROUP_PROMPT * LANE_CHUNKS
SAMPLE_ROWS = DEC_BATCH * LANE_CHUNKS
TRASH_SLOT = GROUP_SLOTS
BUF_ROWS = (GROUP_SLOTS + 1) * LANE_CHUNKS
SCATTER_BATCH = 8
DMA_SPLIT = 8
TABLE_USED = GROUP_SLOTS + LANES
assert TRASH_SLOT < TABLE_USED <= K_STRIDE
FFN_COLS = 256
FFN_DOTS = (2 * D_EXPERT + D_MODEL) // FFN_COLS


def _moe_kernel(off_ref, desth_ref, wtsh_ref, h2p_ref, h2s_ref, x1p_ref, x1s_ref, wgu_ref, bgu_ref, wdn_ref, bdn_ref,
                x2p_ref, x2s_ref,
                h2buf, acc, wgubuf, bgubuf, wdnbuf, bdnbuf, xs0, xs1, ys0, ys1,
                dest_ref, wts_ref, src_ref, seg_expert, seg_first, blk_seg, act_sem, w_sem):
    g = pl.program_id(0)
    last = g == N_GROUPS - 1
    row0 = pl.multiple_of(g * GROUP_ROWS, GROUP_ROWS)

    def prompt_copies():
        cps = []
        for j in range(DMA_SPLIT):
            src = pl.ds(row0 + j * (GROUP_ROWS // DMA_SPLIT), GROUP_ROWS // DMA_SPLIT)
            dst = pl.ds(j * (GROUP_ROWS // DMA_SPLIT), GROUP_ROWS // DMA_SPLIT)
            cps.append(pltpu.make_async_copy(h2p_ref.at[src], h2buf.at[dst], act_sem.at[0]))
            cps.append(pltpu.make_async_copy(x1p_ref.at[src], acc.at[dst], act_sem.at[1]))
        return cps

    def sample_copies():
        return (pltpu.make_async_copy(h2s_ref, h2buf.at[pl.ds(GROUP_ROWS, SAMPLE_ROWS)], act_sem.at[2]),
                pltpu.make_async_copy(x1s_ref, acc.at[pl.ds(GROUP_ROWS, SAMPLE_ROWS)], act_sem.at[3]))

    def weight_copies(e, slot):
        cps = [pltpu.make_async_copy(bgu_ref.at[e], bgubuf.at[slot], w_sem.at[1, slot]),
               pltpu.make_async_copy(bdn_ref.at[e], bdnbuf.at[slot], w_sem.at[3, slot])]
        for j in range(DMA_SPLIT):
            rg = pl.ds(j * (D_MODEL // DMA_SPLIT), D_MODEL // DMA_SPLIT)
            rd = pl.ds(j * (D_EXPERT // DMA_SPLIT), D_EXPERT // DMA_SPLIT)
            cps.append(pltpu.make_async_copy(wgu_ref.at[e, rg], wgubuf.at[slot, rg], w_sem.at[0, slot]))
            cps.append(pltpu.make_async_copy(wdn_ref.at[e, rd], wdnbuf.at[slot, rd], w_sem.at[2, slot]))
        return cps

    def output_copies():
        return [pltpu.make_async_copy(
            acc.at[pl.ds(j * (GROUP_ROWS // DMA_SPLIT), GROUP_ROWS // DMA_SPLIT)],
            x2p_ref.at[pl.ds(row0 + j * (GROUP_ROWS // DMA_SPLIT), GROUP_ROWS // DMA_SPLIT)], act_sem.at[0])
            for j in range(DMA_SPLIT)]

    tab0 = pl.multiple_of(g * (TOP_K * K_STRIDE), TOP_K * K_STRIDE)
    table_copies = []
    for k in range(TOP_K):
        used = pl.ds(k * K_STRIDE, TABLE_USED)
        table_copies.append(pltpu.make_async_copy(
            desth_ref.at[pl.ds(tab0 + k * K_STRIDE, TABLE_USED)], dest_ref.at[used], act_sem.at[4]))
        table_copies.append(pltpu.make_async_copy(
            wtsh_ref.at[pl.ds(tab0 + k * K_STRIDE, TABLE_USED)], wts_ref.at[used], act_sem.at[5]))
    for cp in table_copies:
        cp.start()

    for cp in prompt_copies():
        cp.start()

    @pl.when(last)
    def _():
        for cp in sample_copies():
            cp.start()

    trash = pl.ds(TRASH_SLOT * LANE_CHUNKS, LANE_CHUNKS)
    h2buf[trash, :] = jnp.zeros((LANE_CHUNKS, LANES), F32)
    acc[trash, :] = jnp.zeros((LANE_CHUNKS, LANES), F32)
    ys0[...] = jnp.zeros_like(ys0)
    ys1[...] = jnp.zeros_like(ys1)

    def pad_block(pos0, real_rows=0):
        def body(j, carry):
            for d in range(SUBLANES):
                src_ref[pos0 + MOE_ROWS - SUBLANES - j * SUBLANES + d] = TRASH_SLOT
            return carry
        lax.fori_loop(0, (MOE_ROWS - real_rows + SUBLANES - 1) // SUBLANES, body, 0)

    def scan_expert(e, carry):
        nseg, nblocks = carry
        nblk = off_ref[e, 2]
        first = off_ref[e, 3]

        @pl.when(nblk > 0)
        def _():
            seg_expert[nseg] = e
            seg_first[nseg] = first
            tail_rows = off_ref[e, 1] - off_ref[e, 0] - (nblk - 1) * MOE_ROWS
            pad_block(off_ref[e, 0] + (nblk - 1) * MOE_ROWS, tail_rows)

            def mark(b, c2):
                blk_seg[first + b] = nseg
                return c2
            lax.fori_loop(0, nblk, mark, 0)

            blk_seg[first + nblk - 1] = nseg + (((MOE_ROWS - tail_rows) // ROW_STEP) << SEG_BITS)

        return nseg + jnp.where(nblk > 0, 1, 0), nblocks + nblk

    nseg, nblocks = lax.fori_loop(0, N_EXPERTS, scan_expert, (jnp.int32(0), jnp.int32(0)))
    pad_block(0)
    pad_block((nblocks + 1) * MOE_ROWS)
    pad_block((nblocks + 2) * MOE_ROWS)
    blk_seg[nblocks] = nseg - 1 + ((ROW_VARIANTS - 1) << SEG_BITS)
    blk_seg[nblocks + 1] = nseg - 1 + ((ROW_VARIANTS - 1) << SEG_BITS)

    for cp in weight_copies(seg_expert[0], 0):
        cp.start()

    for cp in table_copies:
        cp.wait()

    nvalid = jnp.where(last, GROUP_SLOTS, GROUP_PROMPT)
    for k in range(TOP_K):
        def fill(j, carry, k=k):
            c0 = k * K_STRIDE + j * SUBLANES
            for d in range(SUBLANES):
                src_ref[dest_ref[c0 + d]] = c0 + d
            return carry
        lax.fori_loop(0, nvalid // SUBLANES, fill, 0)

    for cp in prompt_copies():
        cp.wait()

    @pl.when(last)
    def _():
        for cp in sample_copies():
            cp.wait()

    def token_rows(code):
        slot_id = code & (K_STRIDE - 1)
        return pl.ds(pl.multiple_of(slot_id * LANE_CHUNKS, LANE_CHUNKS), LANE_CHUNKS)

    def gather(b, xs, lo=0, hi=MOE_ROWS):
        base = (b + 1) * MOE_ROWS
        for m in range(lo, hi):
            xs[pl.ds(m, LANE_CHUNKS, stride=XS_STRIDE), :] = h2buf[token_rows(src_ref[base + m]), :]

    def scatter_add(b, ys, lo=0, hi=MOE_ROWS):
        base = (b + 1) * MOE_ROWS
        for m0 in range(lo, hi, SCATTER_BATCH):
            pending = []
            for m in range(m0, m0 + SCATTER_BATCH):
                code = src_ref[base + m]
                rows = token_rows(code)
                pending.append((rows, acc[rows, :] + wts_ref[code] * ys[pl.ds(m, LANE_CHUNKS, stride=XS_STRIDE), :]))
            for rows, val in pending:
                acc[rows, :] = val

    gather_cuts = [round(i * MOE_ROWS / FFN_DOTS) for i in range(FFN_DOTS + 1)]
    scatter_cuts = [SCATTER_BATCH * round(i * (MOE_ROWS // SCATTER_BATCH) / FFN_DOTS) for i in range(FFN_DOTS + 1)]

    def step(b, xs_cur, xs_next, ys_cur, ys_prev):
        tag = blk_seg[b]
        seg = tag & ((1 << SEG_BITS) - 1)
        dropped = tag >> SEG_BITS
        slot = seg & 1

        @pl.when(jnp.logical_and(b == seg_first[seg], b < nblocks))
        def _():
            for cp in weight_copies(seg_expert[seg], slot):
                cp.wait()

            @pl.when(seg + 1 < nseg)
            def _():
                for cp in weight_copies(seg_expert[seg + 1], 1 - slot):
                    cp.start()

        def work(rows):
            done = [0]

            def row_traffic():
                i = done[0]
                gather(b + 1, xs_next, gather_cuts[i], gather_cuts[i + 1])
                scatter_add(b - 1, ys_prev, scatter_cuts[i], scatter_cuts[i + 1])
                done[0] = i + 1

            x = jnp.concatenate(
                [xs_cur[c * XS_STRIDE:c * XS_STRIDE + rows, :] for c in range(LANE_CHUNKS)], axis=1).astype(BF16)
            acts = []
            for c in range(D_EXPERT // FFN_COLS):
                gc = pl.ds(c * FFN_COLS, FFN_COLS)
                uc = pl.ds(D_EXPERT + c * FFN_COLS, FFN_COLS)
                g = _bdot(x, wgubuf[slot, :, gc]) + bgubuf[slot, :, gc]
                row_traffic()
                u = _bdot(x, wgubuf[slot, :, uc]) + bgubuf[slot, :, uc]
                row_traffic()
                gl = jnp.minimum(g, SWIGLU_LIMIT)
                ul = jnp.clip(u, -SWIGLU_LIMIT, SWIGLU_LIMIT)
                acts.append((ul + 1.0) * (gl * jax.nn.sigmoid(SWIGLU_ALPHA * gl)))
            a = jnp.concatenate(acts, axis=1).astype(BF16)
            for c in range(D_MODEL // FFN_COLS):
                oc = pl.ds(c * FFN_COLS, FFN_COLS)
                y = _bdot(a, wdnbuf[slot, :, oc]) + bdnbuf[slot, :, oc]
                for j in range(FFN_COLS // LANES):
                    lc = c * (FFN_COLS // LANES) + j
                    ys_cur[lc * XS_STRIDE:lc * XS_STRIDE + rows, :] = y[:, j * LANES:(j + 1) * LANES]
                row_traffic()

        def choose(v):
            if v == ROW_VARIANTS - 1:
                return lambda: work(MOE_ROWS - v * ROW_STEP)
            return lambda: lax.cond(dropped == v, lambda: work(MOE_ROWS - v * ROW_STEP), choose(v + 1))

        choose(0)()

    gather(0, xs0)
    npairs = (nblocks + 1) // 2

    def pair(t, carry):
        step(2 * t, xs0, xs1, ys0, ys1)
        step(2 * t + 1, xs1, xs0, ys1, ys0)
        return carry

    lax.fori_loop(0, npairs, pair, 0)
    scatter_add(2 * npairs - 1, ys1)

    for cp in output_copies():
        cp.start()

    @pl.when(last)
    def _():
        out_s = pltpu.make_async_copy(acc.at[pl.ds(GROUP_ROWS, SAMPLE_ROWS)], x2s_ref, act_sem.at[2])
        out_s.start()
        out_s.wait()

    for cp in output_copies():
        cp.wait()


def _moe(dest, wts, off, h2p, h2s, x1p, x1s, wgu, bgu, wdn, bdn):
    anyspec = pl.BlockSpec(memory_space=pl.ANY)
    dest = dest.reshape(N_GROUPS * TOP_K * K_STRIDE)
    wts = wts.reshape(N_GROUPS * TOP_K * K_STRIDE)
    in_specs = [
        pl.BlockSpec((None, N_EXPERTS, LANES), lambda g: (g, 0, 0), memory_space=pltpu.SMEM),
        anyspec, anyspec, anyspec, anyspec, anyspec, anyspec, anyspec, anyspec, anyspec, anyspec,
    ]
    scratch = [
        pltpu.VMEM((BUF_ROWS, LANES), F32),
        pltpu.VMEM((BUF_ROWS, LANES), F32),
        pltpu.VMEM((2, D_MODEL, 2 * D_EXPERT), BF16),
        pltpu.VMEM((2, 1, 2 * D_EXPERT), F32),
        pltpu.VMEM((2, D_EXPERT, D_MODEL), BF16),
        pltpu.VMEM((2, 1, D_MODEL), F32),
        pltpu.VMEM((LANE_CHUNKS * XS_STRIDE, LANES), F32),
        pltpu.VMEM((LANE_CHUNKS * XS_STRIDE, LANES), F32),
        pltpu.VMEM((LANE_CHUNKS * XS_STRIDE, LANES), F32),
        pltpu.VMEM((LANE_CHUNKS * XS_STRIDE, LANES), F32),
        pltpu.SMEM((TOP_K * K_STRIDE,), I32),
        pltpu.SMEM((TOP_K * K_STRIDE,), F32),
        pltpu.SMEM((POS_TABLE,), I32),
        pltpu.SMEM((N_EXPERTS,), I32),
        pltpu.SMEM((N_EXPERTS,), I32),
        pltpu.SMEM((LANES,), I32),
        pltpu.SemaphoreType.DMA((6,)),
        pltpu.SemaphoreType.DMA((4, 2)),
    ]
    return pl.pallas_call(
        _moe_kernel,
        grid=(N_GROUPS,),
        in_specs=in_specs,
        out_specs=[anyspec, anyspec],
        out_shape=[jax.ShapeDtypeStruct(x1p.shape, F32), jax.ShapeDtypeStruct(x1s.shape, F32)],
        scratch_shapes=scratch,
        compiler_params=pltpu.CompilerParams(dimension_semantics=("arbitrary",), vmem_limit_bytes=VMEM_LIMIT),
        name="moe",
    )(off, dest, wts, h2p, h2s, x1p, x1s, wgu, bgu, wdn, bdn)


def _ple_final_kernel(x2_ref, ple_ref, wple_ref, gple_ref, wpg_ref, gfin_ref, y_ref):
    rows = y_ref.shape[0]
    x2 = _load_token_major(x2_ref, rows)
    e = _rmsnorm(_bdot(ple_ref[...].astype(BF16), wple_ref[...]), gple_ref[...])
    x3 = x2 + jax.nn.sigmoid(_bdot(x2.astype(BF16), wpg_ref[...])) * e
    y_ref[...] = _rmsnorm(x3, gfin_ref[...])


def _ple_final(x2_tm, ple, wple, gple, wpg, gfin, tile):
    n = ple.shape[0]
    return pl.pallas_call(
        _ple_final_kernel,
        grid=(n // tile,),
        in_specs=[
            pl.BlockSpec((tile * LANE_CHUNKS, LANES), lambda i: (i, 0)),
            pl.BlockSpec((tile, PLE_DIM), lambda i: (i, 0)),
            _full((PLE_DIM, D_MODEL)),
            _full((1, D_MODEL)),
            _full((D_MODEL, D_MODEL)),
            _full((1, D_MODEL)),
        ],
        out_specs=pl.BlockSpec((tile, D_MODEL), lambda i: (i, 0)),
        out_shape=jax.ShapeDtypeStruct((n, D_MODEL), F32),
        compiler_params=pltpu.CompilerParams(dimension_semantics=("arbitrary",), vmem_limit_bytes=VMEM_LIMIT),
        name="ple_final",
    )(x2_tm, ple, wple, gple, wpg, gfin)


def _rope_tables(pos):
    half = HEAD_DIM // 2
    inv = ROPE_THETA ** (-jnp.arange(half, dtype=F32) / half)
    ang = pos.astype(F32)[:, None] * inv[None, :]
    cos, sin = jnp.cos(ang), jnp.sin(ang)
    cos2 = jnp.concatenate([cos, cos, cos, cos], axis=1)
    sin2 = jnp.concatenate([-sin, sin, -sin, sin], axis=1)
    return cos2, sin2


def _layout_w_in(w_in):
    o_q = 2 * A_WIDTH
    wq = w_in[:, o_q:o_q + Q_WIDTH].reshape(D_MODEL, N_HEADS, HEAD_DIM) * (HEAD_DIM ** -0.5)
    kv_head = (jnp.arange(N_HEADS) // GQA_GROUP)[None, :, None]
    wq_pad = jnp.concatenate([jnp.where(kv_head == h, wq, 0.0) for h in range(N_KV_HEADS)], axis=-1)
    return jnp.concatenate([w_in[:, :o_q], wq_pad.reshape(D_MODEL, QPAD_WIDTH), w_in[:, o_q + Q_WIDTH:]], axis=1)


def _router_passes(w_router):
    hi = w_router.astype(BF16)
    lo = (w_router - hi.astype(F32)).astype(BF16)
    w3 = jnp.concatenate([hi, lo, hi], axis=0)
    return jnp.pad(w3, ((0, 0), (0, LANES - N_EXPERTS)))


def _prep_weights(g_mix, w_in, a_ln_g, a_ln_b, a_ws, a_bs, w_pa, w_pb, w_o, g_ffn, w_router, b_router):
    causal = jnp.tril(jnp.ones((CHUNK, CHUNK), dtype=bool))
    return dict(
        gmix=g_mix.reshape(1, D_MODEL),
        win=_layout_w_in(w_in).astype(BF16),
        lng=a_ln_g.reshape(1, A_WIDTH),
        lnb=a_ln_b.reshape(1, A_WIDTH),
        ws=jnp.where(causal[None], a_ws, 0.0).astype(BF16),
        bsf=jnp.repeat(jnp.transpose(a_bs), A_GROUP_DIM, axis=1),
        wpa=w_pa.astype(BF16),
        wpb=w_pb.astype(BF16),
        wo=w_o.astype(BF16),
        gffn=g_ffn.reshape(1, D_MODEL),
        wrt=_router_passes(w_router),
        br=b_router.reshape(N_EXPERTS, 1),
    )


def kernel(x_prompt, x_sample, cache_win_k, cache_win_v, p_prompt, p_sample, g_mix, w_in, a_ln_g, a_ln_b, a_ws, a_bs, sinks, w_pa, w_pb, w_o, g_ffn, w_router, b_router, w_gu, b_gu, w_down, b_down, w_ple, g_ple, w_ple_gate, g_final):
    W = _prep_weights(g_mix[0], w_in[0], a_ln_g[0], a_ln_b[0], a_ws[0], a_bs[0], w_pa[0], w_pb[0], w_o[0],
                      g_ffn[0], w_router[0], b_router[0])
    cos_p, sin_p = _rope_tables(jnp.arange(SEQ, dtype=I32))
    cos_s, sin_s = _rope_tables(jnp.full((1,), PAST_LEN, I32))
    x1p, h2p, logits_p, kwin_p, vwin_p, wgu16, wdn16 = _prompt_front(
        x_prompt.reshape(N_PROMPT, D_MODEL), cos_p, sin_p, W["gmix"], W["win"], W["lng"], W["lnb"],
        W["ws"], W["bsf"], sinks[0], _band_bias(), W["wpa"], W["wpb"], W["wo"], W["gffn"], W["wrt"], W["br"],
        w_gu[0].reshape(N_EXPERTS * D_MODEL, 2 * D_EXPERT), w_down[0].reshape(N_EXPERTS * D_EXPERT, D_MODEL))

    wdiag = jnp.repeat(a_ws[0, :, 0, 0], A_GROUP_DIM)[None, :].astype(BF16)
    bs0 = jnp.repeat(a_bs[0, :, 0], A_GROUP_DIM)[None, :]
    x1s, h2s, logits_s, kwin_s, vwin_s, va_s = _sample_front(
        x_sample.reshape(DEC_BATCH, D_MODEL), cos_s, sin_s, W["gmix"], W["win"], W["lng"], W["lnb"], wdiag, bs0,
        sinks[0], cache_win_k[0].reshape(DEC_BATCH, WINDOW, KV_WIDTH), cache_win_v[0].reshape(DEC_BATCH, WINDOW, KV_WIDTH),
        W["wpa"], W["wpb"], W["wo"], W["gffn"], W["wrt"], W["br"])

    dest, wts, off = _route_plan(logits_p, logits_s)
    x2p, x2s = _moe(dest, wts, off, h2p, h2s, x1p, x1s,
                    wgu16.reshape(N_EXPERTS, D_MODEL, 2 * D_EXPERT), b_gu[0].reshape(N_EXPERTS, 1, 2 * D_EXPERT),
                    wdn16.reshape(N_EXPERTS, D_EXPERT, D_MODEL), b_down[0].reshape(N_EXPERTS, 1, D_MODEL))

    wple = w_ple[0].astype(BF16)
    gple = g_ple[0].reshape(1, D_MODEL)
    wpg = w_ple_gate[0].astype(BF16)
    gfin = g_final.reshape(1, D_MODEL)
    y_p = _ple_final(x2p, p_prompt[0].reshape(N_PROMPT, PLE_DIM), wple, gple, wpg, gfin, 2 * TM)
    y_s = _ple_final(x2s, p_sample[0].reshape(DEC_BATCH, PLE_DIM), wple, gple, wpg, gfin, DEC_BATCH)

    return (
        y_p.reshape(BATCH, SEQ, D_MODEL),
        y_s.reshape(DEC_BATCH, 1, D_MODEL),
        kwin_p.reshape(1, BATCH, WINDOW, N_KV_HEADS, HEAD_DIM),
        vwin_p.reshape(1, BATCH, WINDOW, N_KV_HEADS, HEAD_DIM),
        kwin_s.reshape(1, DEC_BATCH, WINDOW, N_KV_HEADS, HEAD_DIM),
        vwin_s.reshape(1, DEC_BATCH, WINDOW, N_KV_HEADS, HEAD_DIM),
        va_s.reshape(1, DEC_BATCH, 1, A_WIDTH),
    )
```

```python
import numpy as np

import jax
import jax.numpy as jnp
from jax import lax
from jax.experimental import pallas as pl
from jax.experimental.pallas import tpu as pltpu

F32 = jnp.float32
BF16 = jnp.bfloat16
I32 = jnp.int32

D_MODEL = 1024
BATCH = 4
SEQ = 4096
DEC_BATCH = 128
PAST_LEN = 8192
CHUNK = 128
A_GROUPS = 4
A_GROUP_DIM = 128
A_WIDTH = A_GROUPS * A_GROUP_DIM
N_HEADS = 8
N_KV_HEADS = 2
HEAD_DIM = 64
Q_WIDTH = N_HEADS * HEAD_DIM
KV_WIDTH = N_KV_HEADS * HEAD_DIM
GQA_GROUP = N_HEADS // N_KV_HEADS
WINDOW = 128
ROPE_THETA = 10000.0
N_EXPERTS = 32
TOP_K = 4
D_EXPERT = D_MODEL
SWIGLU_ALPHA = 1.702
SWIGLU_LIMIT = 7.0
PLE_DIM = 256
RMS_EPS = 1e-5
LN_EPS = 1e-5

LANES = 128

QPAD_WIDTH = N_HEADS * LANES
O_Q = 2 * A_WIDTH
O_K = O_Q + QPAD_WIDTH
O_V = O_K + KV_WIDTH
O_GA = O_V + KV_WIDTH
O_GB = O_GA + D_MODEL
IN_COLS = O_GB + D_MODEL
SUBLANES = 8
LANE_CHUNKS = D_MODEL // LANES
VMEM_LIMIT = 56 * 1024 * 1024

N_PROMPT = BATCH * SEQ
TM = 256
TILES_PER_SEQ = SEQ // TM
BLOCKS_PER_TILE = TM // WINDOW
FRONT_STEPS = N_PROMPT // TM
CAST_ROWS = N_EXPERTS * D_MODEL // FRONT_STEPS
CAST_SPLIT = 2
IN_PROJ_DOTS = 5
BF16_ROWS = 16
CAST_CUTS = [BF16_ROWS * round(i * (CAST_ROWS // BF16_ROWS) / IN_PROJ_DOTS) for i in range(IN_PROJ_DOTS + 1)]

N_GROUPS = 4
GROUP_PROMPT = N_PROMPT // N_GROUPS
GROUP_SLOTS = GROUP_PROMPT + DEC_BATCH
GROUP_ASSIGN = GROUP_SLOTS * TOP_K
SLOT_TILES = GROUP_SLOTS // LANES
MOE_ROWS = 256
XS_STRIDE = MOE_ROWS + SUBLANES
HALF_ROWS = MOE_ROWS // 2
HALF_FLAG = 64
assert N_EXPERTS <= HALF_FLAG
SLOT_BITS = 13
K_STRIDE = 1 << SLOT_BITS
assert GROUP_SLOTS < K_STRIDE
MAX_BLOCKS = GROUP_ASSIGN // MOE_ROWS + N_EXPERTS
POS_TABLE = 1 << 15
assert (MAX_BLOCKS + 3) * MOE_ROWS <= POS_TABLE
assert MAX_BLOCKS + 2 <= LANES


def _bdot(a, b):
    return jnp.dot(a, b, preferred_element_type=F32)


def _rmsnorm(x, g):
    return x * lax.rsqrt(jnp.mean(x * x, axis=-1, keepdims=True) + RMS_EPS) * g


def _gelu(x):
    return 0.5 * x * (1.0 + lax.erf(x * (0.5 ** 0.5)))


def _group_layernorm(v, g, b):
    cols = []
    for gi in range(A_GROUPS):
        s = slice(gi * A_GROUP_DIM, (gi + 1) * A_GROUP_DIM)
        vg = v[:, s]
        mu = jnp.mean(vg, axis=-1, keepdims=True)
        d = vg - mu
        var = jnp.mean(d * d, axis=-1, keepdims=True)
        cols.append(d * lax.rsqrt(var + LN_EPS) * g[:, s] + b[:, s])
    return jnp.concatenate(cols, axis=1)


def _rope(x, cos, sin_signed):
    width = x.shape[1]
    reps = width // LANES
    cosf = jnp.concatenate([cos] * reps, axis=1) if reps > 1 else cos
    sinf = jnp.concatenate([sin_signed] * reps, axis=1) if reps > 1 else sin_signed
    half = HEAD_DIM // 2
    lane = lax.broadcasted_iota(I32, x.shape, 1)
    up = pltpu.roll(x, width - half, 1)
    down = pltpu.roll(x, half, 1)
    partner = jnp.where((lane & (HEAD_DIM - 1)) < half, up, down)
    return x * cosf + partner * sinf


def _in_projection(x, gmix_ref, win_ref, lng_ref, lnb_ref, cos, sin_signed, after_dot=lambda: None):
    hb = _rmsnorm(x, gmix_ref[...]).astype(BF16)

    def dot_cols(lo, hi):
        z = _bdot(hb, win_ref[:, lo:hi])
        after_dot()
        return z

    u = _gelu(dot_cols(0, A_WIDTH))
    va = _group_layernorm(_gelu(dot_cols(A_WIDTH, O_Q)), lng_ref[...], lnb_ref[...])
    zq = jnp.concatenate([dot_cols(O_Q, O_Q + QPAD_WIDTH // 2), dot_cols(O_Q + QPAD_WIDTH // 2, O_K)], axis=1)
    zkv = dot_cols(O_K, O_GA)
    q = _rope(zq, cos, sin_signed)
    k = _rope(zkv[:, :KV_WIDTH], cos, sin_signed)
    v = zkv[:, KV_WIDTH:]
    return u, va, q, k, v, hb


def _gate_preact(hb, win_ref, part, parts):
    width = 2 * D_MODEL // parts
    return _bdot(hb, win_ref[:, O_GA + part * width:O_GA + (part + 1) * width])


def _gates(preacts):
    zg = jnp.concatenate(preacts, axis=1)
    return jax.nn.sigmoid(zg[:, :D_MODEL]), jax.nn.sigmoid(zg[:, D_MODEL:])


def _merge_and_route(x, ya_in, att, gate_a, gate_b, wpa_ref, wpb_ref, wo_ref, gffn_ref, wr3_ref, br_ref,
                     x1_ref, h2_ref):
    ya = _bdot(ya_in.astype(BF16), wpa_ref[...])
    yb = _bdot(att.astype(BF16), wpb_ref[...])
    mix = (gate_a * ya + gate_b * yb).astype(BF16)
    x1 = x + _bdot(mix, wo_ref[...])
    _store_token_major(x1_ref, x1)
    h2 = _rmsnorm(x1, gffn_ref[...])
    _store_token_major(h2_ref, h2)
    hi = h2.astype(BF16)
    lo = (h2 - hi.astype(F32)).astype(BF16)
    logits = _bdot(jnp.concatenate([hi, hi, lo], axis=1), wr3_ref[...])
    return jnp.transpose(logits)[:N_EXPERTS, :] + br_ref[...]


def _top4_softmax(logits):
    eid = lax.broadcasted_iota(I32, logits.shape, 0)
    vals, idxs = [], []
    for _ in range(TOP_K):
        m = jnp.max(logits, axis=0, keepdims=True)
        idx = jnp.min(jnp.where(logits == m, eid, N_EXPERTS), axis=0, keepdims=True)
        logits = jnp.where(eid == idx, -jnp.inf, logits)
        vals.append(m)
        idxs.append(idx)
    es = [jnp.exp(v - vals[0]) for v in vals]
    inv = 1.0 / (es[0] + es[1] + es[2] + es[3])
    return jnp.concatenate(idxs, axis=0), jnp.concatenate([e * inv for e in es], axis=0)


def _store_token_major(ref, val):
    rows = val.shape[0]
    for c in range(LANE_CHUNKS):
        ref[pl.ds(c, rows, stride=LANE_CHUNKS), :] = val[:, c * LANES:(c + 1) * LANES]


def _band_attention(qpad, k, v, k_prev, v_prev, sinks_ref, bias_ref, seq_start, after_scores):
    kb = jnp.concatenate([k_prev, k], axis=0).astype(BF16)
    vt = jnp.transpose(jnp.concatenate([v_prev, v], axis=0)).astype(BF16)
    qb = qpad.astype(BF16)
    lane = lax.broadcasted_iota(I32, (1, GQA_GROUP * WINDOW), 1)
    blocks = []
    for b in range(BLOCKS_PER_TILE):
        bias = bias_ref[jnp.where(seq_start, 1, 0)] if b == 0 else bias_ref[0]
        keys = kb[b * WINDOW:(b + 2) * WINDOW, :]
        pieces = []
        for h in range(N_KV_HEADS):
            qh = jnp.concatenate(
                [qb[b * WINDOW:(b + 1) * WINDOW, (h * GQA_GROUP + j) * LANES:(h * GQA_GROUP + j + 1) * LANES]
                 for j in range(GQA_GROUP)], axis=0)
            st = lax.dot_general(keys, qh, (((1,), (1,)), ((), ())), preferred_element_type=F32) + bias
            after_scores(b * N_KV_HEADS + h)
            sink = jnp.zeros((1, GQA_GROUP * WINDOW), F32)
            for j in range(GQA_GROUP):
                sink = jnp.where(lane // WINDOW == j, sinks_ref[h * GQA_GROUP + j], sink)
            m = jnp.maximum(jnp.max(st, axis=0, keepdims=True), sink)
            e = jnp.exp(st - m)
            inv = 1.0 / (jnp.sum(e, axis=0, keepdims=True) + jnp.exp(sink - m))
            ot = _bdot(vt[h * HEAD_DIM:(h + 1) * HEAD_DIM, b * WINDOW:(b + 2) * WINDOW], (e * inv).astype(BF16))
            pieces.extend(ot[:, j * WINDOW:(j + 1) * WINDOW] for j in range(GQA_GROUP))
        blocks.append(jnp.transpose(jnp.concatenate(pieces, axis=0)))
    return jnp.concatenate(blocks, axis=0)


def _band_bias():
    kj = np.arange(2 * WINDOW)[None, :, None]
    qi = (np.arange(GQA_GROUP * WINDOW) % WINDOW)[None, None, :]
    lo = (np.arange(2) * WINDOW)[:, None, None]
    valid = (kj > qi) & (kj <= qi + WINDOW) & (kj >= lo)
    return jnp.asarray(np.where(valid, 0.0, -np.inf).astype(np.float32))


def _prompt_front_kernel(x_ref, cos_ref, sin_ref, gmix_ref, win_ref, lng_ref, lnb_ref, ws_ref, bsf_ref,
                         sinks_ref, bias_ref, wpa_ref, wpb_ref, wo_ref, gffn_ref, wr3_ref, br_ref, wgu32_ref, wdn32_ref,
                         x1_ref, h2_ref, logits_ref, kwin_ref, vwin_ref, wgu16_ref, wdn16_ref,
                         kprev_ref, vprev_ref, gu_in, dn_in, gu_out, dn_out, cast_sem):
    i = pl.program_id(0)
    seq_start = (i % TILES_PER_SEQ) == 0

    def cast_rows(c, j):
        part = CAST_ROWS // CAST_SPLIT
        return pl.ds(pl.multiple_of(c * CAST_ROWS + j * part, part), part), pl.ds(j * part, part)

    def cast_in(c, slot):
        cps = []
        for j in range(CAST_SPLIT):
            hbm, loc = cast_rows(c, j)
            cps.append(pltpu.make_async_copy(wgu32_ref.at[hbm], gu_in.at[slot, loc], cast_sem.at[0, slot]))
            cps.append(pltpu.make_async_copy(wdn32_ref.at[hbm], dn_in.at[slot, loc], cast_sem.at[1, slot]))
        return cps

    def cast_out(c, slot):
        cps = []
        for j in range(CAST_SPLIT):
            hbm, loc = cast_rows(c, j)
            cps.append(pltpu.make_async_copy(gu_out.at[slot, loc], wgu16_ref.at[hbm], cast_sem.at[2, slot]))
            cps.append(pltpu.make_async_copy(dn_out.at[slot, loc], wdn16_ref.at[hbm], cast_sem.at[3, slot]))
        return cps

    slot = i & 1

    @pl.when(i == 0)
    def _():
        for cp in cast_in(0, 0):
            cp.start()

    @pl.when(i + 1 < FRONT_STEPS)
    def _():
        for cp in cast_in(i + 1, 1 - slot):
            cp.start()

    for cp in cast_in(i, slot):
        cp.wait()

    @pl.when(i >= 2)
    def _():
        for cp in cast_out(i - 2, slot):
            cp.wait()


    @pl.when(seq_start)
    def _():
        kprev_ref[...] = jnp.zeros_like(kprev_ref)
        vprev_ref[...] = jnp.zeros_like(vprev_ref)

    x = x_ref[...]
    cast_done = [0]

    def cast_slice():
        i = cast_done[0]
        r = pl.ds(CAST_CUTS[i], CAST_CUTS[i + 1] - CAST_CUTS[i])
        gu_out[slot, r, :] = gu_in[slot, r, :].astype(BF16)
        dn_out[slot, r, :] = dn_in[slot, r, :].astype(BF16)
        cast_done[0] += 1

    u, va, q, k, v, hb = _in_projection(
        x, gmix_ref, win_ref, lng_ref, lnb_ref, cos_ref[...], sin_ref[...], cast_slice)
    assert cast_done[0] == IN_PROJ_DOTS
    for cp in cast_out(i, slot):
        cp.start()

    units = BLOCKS_PER_TILE * N_KV_HEADS
    preacts = []
    att = _band_attention(q, k, v, kprev_ref[...], vprev_ref[...], sinks_ref, bias_ref, seq_start,
                          lambda unit: preacts.append(_gate_preact(hb, win_ref, unit, units)))
    gate_a, gate_b = _gates(preacts)

    k_last, v_last = k[TM - WINDOW:], v[TM - WINDOW:]
    kprev_ref[...] = k_last
    vprev_ref[...] = v_last
    kwin_ref[0] = k_last
    vwin_ref[0] = v_last

    vab = va.astype(BF16)
    zc = jnp.concatenate(
        [jnp.concatenate(
            [_bdot(ws_ref[g], vab[b * CHUNK:(b + 1) * CHUNK, g * A_GROUP_DIM:(g + 1) * A_GROUP_DIM])
             for g in range(A_GROUPS)], axis=1) + bsf_ref[...]
         for b in range(BLOCKS_PER_TILE)], axis=0)

    logits_ref[...] = _merge_and_route(x, u * zc, att, gate_a, gate_b,
                                       wpa_ref, wpb_ref, wo_ref, gffn_ref, wr3_ref, br_ref, x1_ref, h2_ref)

    @pl.when(i == FRONT_STEPS - 1)
    def _():
        for cp in cast_out(i - 1, 1 - slot) + cast_out(i, slot):
            cp.wait()


def _full(shape):
    return pl.BlockSpec(shape, lambda i: (0,) * len(shape))


def _prompt_front(x, cos, sin, gmix, win, lng, lnb, ws, bsf, sinks, bias, wpa, wpb, wo, gffn, wrt, br, wgu32, wdn32):
    n = x.shape[0]
    assert n == N_PROMPT
    grid = (FRONT_STEPS,)
    anyspec = pl.BlockSpec(memory_space=pl.ANY)
    in_specs = [
        pl.BlockSpec((TM, D_MODEL), lambda i: (i, 0)),
        pl.BlockSpec((TM, LANES), lambda i: (i % TILES_PER_SEQ, 0)),
        pl.BlockSpec((TM, LANES), lambda i: (i % TILES_PER_SEQ, 0)),
        _full((1, D_MODEL)),
        _full((D_MODEL, IN_COLS)),
        _full((1, A_WIDTH)),
        _full((1, A_WIDTH)),
        _full((A_GROUPS, CHUNK, CHUNK)),
        _full((CHUNK, A_WIDTH)),
        pl.BlockSpec(memory_space=pltpu.SMEM),
        _full((2, 2 * WINDOW, GQA_GROUP * WINDOW)),
        _full((A_WIDTH, D_MODEL)),
        _full((Q_WIDTH, D_MODEL)),
        _full((D_MODEL, D_MODEL)),
        _full((1, D_MODEL)),
        _full((3 * D_MODEL, LANES)),
        _full((N_EXPERTS, 1)),
        anyspec,
        anyspec,
    ]
    out_shape = [
        jax.ShapeDtypeStruct((n * LANE_CHUNKS, LANES), F32),
        jax.ShapeDtypeStruct((n * LANE_CHUNKS, LANES), F32),
        jax.ShapeDtypeStruct((N_EXPERTS, n), F32),
        jax.ShapeDtypeStruct((n // SEQ, WINDOW, KV_WIDTH), F32),
        jax.ShapeDtypeStruct((n // SEQ, WINDOW, KV_WIDTH), F32),
        jax.ShapeDtypeStruct(wgu32.shape, BF16),
        jax.ShapeDtypeStruct(wdn32.shape, BF16),
    ]
    out_specs = [
        pl.BlockSpec((TM * LANE_CHUNKS, LANES), lambda i: (i, 0)),
        pl.BlockSpec((TM * LANE_CHUNKS, LANES), lambda i: (i, 0)),
        pl.BlockSpec((N_EXPERTS, TM), lambda i: (0, i)),
        pl.BlockSpec((1, WINDOW, KV_WIDTH), lambda i: (i // TILES_PER_SEQ, 0, 0)),
        pl.BlockSpec((1, WINDOW, KV_WIDTH), lambda i: (i // TILES_PER_SEQ, 0, 0)),
        anyspec,
        anyspec,
    ]
    scratch = [
        pltpu.VMEM((WINDOW, KV_WIDTH), F32),
        pltpu.VMEM((WINDOW, KV_WIDTH), F32),
        pltpu.VMEM((2, CAST_ROWS, 2 * D_EXPERT), F32),
        pltpu.VMEM((2, CAST_ROWS, D_MODEL), F32),
        pltpu.VMEM((2, CAST_ROWS, 2 * D_EXPERT), BF16),
        pltpu.VMEM((2, CAST_ROWS, D_MODEL), BF16),
        pltpu.SemaphoreType.DMA((4, 2)),
    ]
    return pl.pallas_call(
        _prompt_front_kernel,
        grid=grid,
        in_specs=in_specs,
        out_specs=out_specs,
        out_shape=out_shape,
        scratch_shapes=scratch,
        compiler_params=pltpu.CompilerParams(dimension_semantics=("arbitrary",), vmem_limit_bytes=VMEM_LIMIT),
        name="prompt_front",
    )(x, cos, sin, gmix, win, lng, lnb, ws, bsf, sinks, bias, wpa, wpb, wo, gffn, wrt, br, wgu32, wdn32)


SAMPLE_STEP = 32
SAMPLE_STEPS = DEC_BATCH // SAMPLE_STEP


def _sample_kernel(x_ref, cos_ref, sin_ref, gmix_ref, win_ref, lng_ref, lnb_ref, wdiag_ref, bs0_ref, sinks_ref,
                   kc_ref, vc_ref, wpa_ref, wpb_ref, wo_ref, gffn_ref, wr3_ref, br_ref,
                   x1_ref, h2_ref, logits_ref, kwin_ref, vwin_ref, va_ref,
                   q_s, k_s, v_s, yain_s, ga_s, gb_s, att_s):
    i = pl.program_id(0)

    @pl.when(i == 0)
    def _():
        x = x_ref[...]
        cos = jnp.broadcast_to(cos_ref[...], (DEC_BATCH, LANES))
        sin = jnp.broadcast_to(sin_ref[...], (DEC_BATCH, LANES))
        u, va, q, k, v, hb = _in_projection(x, gmix_ref, win_ref, lng_ref, lnb_ref, cos, sin)
        gate_a, gate_b = _gates([_gate_preact(hb, win_ref, 0, 1)])
        va_ref[...] = va
        z = wdiag_ref[...].astype(F32) * va.astype(BF16).astype(F32) + bs0_ref[...]
        yain_s[...] = u * z
        q_s[...] = q
        k_s[...] = k
        v_s[...] = v
        ga_s[...] = gate_a
        gb_s[...] = gate_b

    r0 = pl.multiple_of(i * SAMPLE_STEP, SAMPLE_STEP)
    kwin = jnp.concatenate([kc_ref[:, 1:, :], k_s[pl.ds(r0, SAMPLE_STEP), :][:, None, :]], axis=1)
    vwin = jnp.concatenate([vc_ref[:, 1:, :], v_s[pl.ds(r0, SAMPLE_STEP), :][:, None, :]], axis=1)
    kwin_ref[...] = kwin
    vwin_ref[...] = vwin

    q16 = q_s[pl.ds(r0, SAMPLE_STEP), :]
    lane = lax.broadcasted_iota(I32, (SAMPLE_STEP, LANES), 1)
    heads = [q16[:, hq * LANES:(hq + 1) * LANES] for hq in range(N_HEADS)]
    qpad = pltpu.einshape("hbd->bhd", jnp.stack(heads, axis=0)).astype(BF16)
    s = jnp.einsum("bhd,bkd->bhk", qpad, kwin.astype(BF16), preferred_element_type=F32)
    hid = lax.broadcasted_iota(I32, (1, N_HEADS, 1), 1)
    sink = jnp.zeros((1, N_HEADS, 1), F32)
    for hq in range(N_HEADS):
        sink = jnp.where(hid == hq, sinks_ref[hq], sink)
    m = jnp.maximum(jnp.max(s, axis=-1, keepdims=True), sink)
    e = jnp.exp(s - m)
    inv = 1.0 / (jnp.sum(e, axis=-1, keepdims=True) + jnp.exp(sink - m))
    o = jnp.einsum("bhk,bkd->bhd", (e * inv).astype(BF16), vwin.astype(BF16), preferred_element_type=F32)
    o = pltpu.einshape("bhd->hbd", o)
    chunks = []
    for c in range(N_HEADS // 2):
        parts = []
        for p in range(2):
            hq = 2 * c + p
            oh = o[hq]
            if p != hq // GQA_GROUP:
                oh = pltpu.roll(oh, HEAD_DIM, 1)
            parts.append(oh)
        chunks.append(jnp.where(lane < HEAD_DIM, parts[0], parts[1]))
    att_s[pl.ds(r0, SAMPLE_STEP), :] = jnp.concatenate(chunks, axis=1)

    @pl.when(i == SAMPLE_STEPS - 1)
    def _():
        logits_ref[...] = _merge_and_route(
            x_ref[...], yain_s[...], att_s[...], ga_s[...], gb_s[...],
            wpa_ref, wpb_ref, wo_ref, gffn_ref, wr3_ref, br_ref, x1_ref, h2_ref)


def _sample_front(x, cos, sin, gmix, win, lng, lnb, wdiag, bs0, sinks, kc, vc, wpa, wpb, wo, gffn, wrt, br):
    n = DEC_BATCH
    cache_spec = pl.BlockSpec((SAMPLE_STEP, WINDOW, KV_WIDTH), lambda i: (i, 0, 0))
    in_specs = [
        _full((n, D_MODEL)),
        _full((1, LANES)),
        _full((1, LANES)),
        _full((1, D_MODEL)),
        _full((D_MODEL, IN_COLS)),
        _full((1, A_WIDTH)),
        _full((1, A_WIDTH)),
        _full((1, A_WIDTH)),
        _full((1, A_WIDTH)),
        pl.BlockSpec(memory_space=pltpu.SMEM),
        cache_spec,
        cache_spec,
        _full((A_WIDTH, D_MODEL)),
        _full((Q_WIDTH, D_MODEL)),
        _full((D_MODEL, D_MODEL)),
        _full((1, D_MODEL)),
        _full((3 * D_MODEL, LANES)),
        _full((N_EXPERTS, 1)),
    ]
    out_shape = [
        jax.ShapeDtypeStruct((n * LANE_CHUNKS, LANES), F32),
        jax.ShapeDtypeStruct((n * LANE_CHUNKS, LANES), F32),
        jax.ShapeDtypeStruct((N_EXPERTS, n), F32),
        jax.ShapeDtypeStruct((n, WINDOW, KV_WIDTH), F32),
        jax.ShapeDtypeStruct((n, WINDOW, KV_WIDTH), F32),
        jax.ShapeDtypeStruct((n, A_WIDTH), F32),
    ]
    out_specs = [
        _full((n * LANE_CHUNKS, LANES)),
        _full((n * LANE_CHUNKS, LANES)),
        _full((N_EXPERTS, n)),
        cache_spec,
        cache_spec,
        _full((n, A_WIDTH)),
    ]
    scratch = [
        pltpu.VMEM((n, QPAD_WIDTH), F32), pltpu.VMEM((n, KV_WIDTH), F32), pltpu.VMEM((n, KV_WIDTH), F32),
        pltpu.VMEM((n, A_WIDTH), F32), pltpu.VMEM((n, D_MODEL), F32), pltpu.VMEM((n, D_MODEL), F32),
        pltpu.VMEM((n, Q_WIDTH), F32),
    ]
    return pl.pallas_call(
        _sample_kernel,
        grid=(SAMPLE_STEPS,),
        in_specs=in_specs,
        out_specs=out_specs,
        out_shape=out_shape,
        scratch_shapes=scratch,
        compiler_params=pltpu.CompilerParams(dimension_semantics=("arbitrary",), vmem_limit_bytes=VMEM_LIMIT),
        name="sample_front",
    )(x, cos, sin, gmix, win, lng, lnb, wdiag, bs0, sinks, kc, vc, wpa, wpb, wo, gffn, wrt, br)


def _route_plan_kernel(lp_ref, ls_ref, dest_ref, wts_ref, off_ref):
    g = pl.program_id(0)
    topi, topw = _top4_softmax(jnp.concatenate([lp_ref[...], ls_ref[...]], axis=1))
    slot = lax.broadcasted_iota(I32, (TOP_K, GROUP_SLOTS), 1)
    eall = jnp.where(jnp.logical_or(slot < GROUP_PROMPT, g == N_GROUPS - 1), topi, N_EXPERTS)
    wts_ref[:, 0:GROUP_SLOTS] = topw
    wts_ref[:, GROUP_SLOTS:] = jnp.zeros((TOP_K, K_STRIDE - GROUP_SLOTS), F32)
    dest_ref[:, GROUP_SLOTS:] = jnp.zeros((TOP_K, K_STRIDE - GROUP_SLOTS), I32)
    eid = lax.broadcasted_iota(I32, (N_EXPERTS, GROUP_SLOTS), 0)
    onehots = [eall[k:k + 1, :] == eid for k in range(TOP_K)]
    count = jnp.zeros((N_EXPERTS, GROUP_SLOTS), F32)
    for oh in onehots:
        count = count + oh.astype(F32)
    total = jnp.broadcast_to(jnp.sum(count, axis=1, keepdims=True), (N_EXPERTS, LANES))
    padded = total + (MOE_ROWS - 1)
    nblk = jnp.floor(padded * (1.0 / MOE_ROWS))
    rem = padded - nblk * MOE_ROWS
    nblk = jnp.where(rem >= MOE_ROWS, nblk + 1.0, jnp.where(rem < 0.0, nblk - 1.0, nblk))
    r = lax.broadcasted_iota(I32, (N_EXPERTS, N_EXPERTS), 0)
    c = lax.broadcasted_iota(I32, (N_EXPERTS, N_EXPERTS), 1)
    first_blk = lax.dot_general((c < r).astype(F32), nblk, (((1,), (0,)), ((), ())),
                                precision=lax.Precision.HIGHEST, preferred_element_type=F32)
    start = (first_blk + 1.0) * MOE_ROWS
    lane = lax.broadcasted_iota(I32, (N_EXPERTS, LANES), 1)
    info = jnp.where(lane == 0, start, jnp.where(lane == 1, start + total, jnp.where(lane == 2, nblk, first_blk)))
    off_ref[...] = info.astype(I32)
    ti = lax.broadcasted_iota(I32, (LANES, LANES), 0)
    tj = lax.broadcasted_iota(I32, (LANES, LANES), 1)
    before = (ti < tj).astype(BF16)
    ones = jnp.ones((LANES, LANES), BF16)
    running = start
    for t in range(SLOT_TILES):
        sl = slice(t * LANES, (t + 1) * LANES)
        cb = count[:, sl].astype(BF16)
        pos = running + _bdot(cb, before)
        rows = [jnp.sum(jnp.where(oh[:, sl], pos, 0.0), axis=0, keepdims=True) for oh in onehots]
        dest_ref[:, sl] = jnp.concatenate(rows, axis=0).astype(I32)
        running = running + _bdot(cb, ones)


def _route_plan(logits_p, logits_s):
    in_specs = [
        pl.BlockSpec((N_EXPERTS, GROUP_PROMPT), lambda g: (0, g)),
        pl.BlockSpec((N_EXPERTS, DEC_BATCH), lambda g: (0, 0)),
    ]
    out_shape = [
        jax.ShapeDtypeStruct((N_GROUPS, TOP_K, K_STRIDE), I32),
        jax.ShapeDtypeStruct((N_GROUPS, TOP_K, K_STRIDE), F32),
        jax.ShapeDtypeStruct((N_GROUPS, N_EXPERTS, LANES), I32),
    ]
    out_specs = [
        pl.BlockSpec((None, TOP_K, K_STRIDE), lambda g: (g, 0, 0)),
        pl.BlockSpec((None, TOP_K, K_STRIDE), lambda g: (g, 0, 0)),
        pl.BlockSpec((None, N_EXPERTS, LANES), lambda g: (g, 0, 0)),
    ]
    return pl.pallas_call(
        _route_plan_kernel,
        grid=(N_GROUPS,),
        in_specs=in_specs,
        out_specs=out_specs,
        out_shape=out_shape,
        compiler_params=pltpu.CompilerParams(dimension_semantics=("arbitrary",)),
        name="route_plan",
    )(logits_p, logits_s)


GROUP_ROWS = GROUP_PROMPT * LANE_CHUNKS
SAMPLE_ROWS = DEC_BATCH * LANE_CHUNKS
TRASH_SLOT = GROUP_SLOTS
BUF_ROWS = (GROUP_SLOTS + 1) * LANE_CHUNKS
SCATTER_BATCH = 8
DMA_SPLIT = 8
TABLE_USED = GROUP_SLOTS + LANES
assert TRASH_SLOT < TABLE_USED <= K_STRIDE
FFN_COLS = 256
FFN_DOTS = (2 * D_EXPERT + D_MODEL) // FFN_COLS


def _moe_kernel(off_ref, desth_ref, wtsh_ref, h2p_ref, h2s_ref, x1p_ref, x1s_ref, wgu_ref, bgu_ref, wdn_ref, bdn_ref,
                x2p_ref, x2s_ref,
                h2buf, acc, wgubuf, bgubuf, wdnbuf, bdnbuf, xs0, xs1, ys0, ys1,
                dest_ref, wts_ref, src_ref, seg_expert, seg_first, blk_seg, act_sem, w_sem):
    g = pl.program_id(0)
    last = g == N_GROUPS - 1
    row0 = pl.multiple_of(g * GROUP_ROWS, GROUP_ROWS)

    def prompt_copies():
        cps = []
        for j in range(DMA_SPLIT):
            src = pl.ds(row0 + j * (GROUP_ROWS // DMA_SPLIT), GROUP_ROWS // DMA_SPLIT)
            dst = pl.ds(j * (GROUP_ROWS // DMA_SPLIT), GROUP_ROWS // DMA_SPLIT)
            cps.append(pltpu.make_async_copy(h2p_ref.at[src], h2buf.at[dst], act_sem.at[0]))
            cps.append(pltpu.make_async_copy(x1p_ref.at[src], acc.at[dst], act_sem.at[1]))
        return cps

    def sample_copies():
        return (pltpu.make_async_copy(h2s_ref, h2buf.at[pl.ds(GROUP_ROWS, SAMPLE_ROWS)], act_sem.at[2]),
                pltpu.make_async_copy(x1s_ref, acc.at[pl.ds(GROUP_ROWS, SAMPLE_ROWS)], act_sem.at[3]))

    def weight_copies(e, slot):
        cps = [pltpu.make_async_copy(bgu_ref.at[e], bgubuf.at[slot], w_sem.at[1, slot]),
               pltpu.make_async_copy(bdn_ref.at[e], bdnbuf.at[slot], w_sem.at[3, slot])]
        for j in range(DMA_SPLIT):
            rg = pl.ds(j * (D_MODEL // DMA_SPLIT), D_MODEL // DMA_SPLIT)
            rd = pl.ds(j * (D_EXPERT // DMA_SPLIT), D_EXPERT // DMA_SPLIT)
            cps.append(pltpu.make_async_copy(wgu_ref.at[e, rg], wgubuf.at[slot, rg], w_sem.at[0, slot]))
            cps.append(pltpu.make_async_copy(wdn_ref.at[e, rd], wdnbuf.at[slot, rd], w_sem.at[2, slot]))
        return cps

    def output_copies():
        return [pltpu.make_async_copy(
            acc.at[pl.ds(j * (GROUP_ROWS // DMA_SPLIT), GROUP_ROWS // DMA_SPLIT)],
            x2p_ref.at[pl.ds(row0 + j * (GROUP_ROWS // DMA_SPLIT), GROUP_ROWS // DMA_SPLIT)], act_sem.at[0])
            for j in range(DMA_SPLIT)]

    tab0 = pl.multiple_of(g * (TOP_K * K_STRIDE), TOP_K * K_STRIDE)
    table_copies = []
    for k in range(TOP_K):
        used = pl.ds(k * K_STRIDE, TABLE_USED)
        table_copies.append(pltpu.make_async_copy(
            desth_ref.at[pl.ds(tab0 + k * K_STRIDE, TABLE_USED)], dest_ref.at[used], act_sem.at[4]))
        table_copies.append(pltpu.make_async_copy(
            wtsh_ref.at[pl.ds(tab0 + k * K_STRIDE, TABLE_USED)], wts_ref.at[used], act_sem.at[5]))
    for cp in table_copies:
        cp.start()

    for cp in prompt_copies():
        cp.start()

    @pl.when(last)
    def _():
        for cp in sample_copies():
            cp.start()

    trash = pl.ds(TRASH_SLOT * LANE_CHUNKS, LANE_CHUNKS)
    h2buf[trash, :] = jnp.zeros((LANE_CHUNKS, LANES), F32)
    acc[trash, :] = jnp.zeros((LANE_CHUNKS, LANES), F32)
    ys0[...] = jnp.zeros_like(ys0)
    ys1[...] = jnp.zeros_like(ys1)

    def pad_block(pos0, real_rows=0):
        def body(j, carry):
            for d in range(SUBLANES):
                src_ref[pos0 + MOE_ROWS - SUBLANES - j * SUBLANES + d] = TRASH_SLOT
            return carry
        lax.fori_loop(0, (MOE_ROWS - real_rows + SUBLANES - 1) // SUBLANES, body, 0)

    def scan_expert(e, carry):
        nseg, nblocks = carry
        nblk = off_ref[e, 2]
        first = off_ref[e, 3]

        @pl.when(nblk > 0)
        def _():
            seg_expert[nseg] = e
            seg_first[nseg] = first
            tail_rows = off_ref[e, 1] - off_ref[e, 0] - (nblk - 1) * MOE_ROWS
            pad_block(off_ref[e, 0] + (nblk - 1) * MOE_ROWS, tail_rows)

            def mark(b, c2):
                blk_seg[first + b] = nseg
                return c2
            lax.fori_loop(0, nblk, mark, 0)

            blk_seg[first + nblk - 1] = nseg + jnp.where(tail_rows <= HALF_ROWS, HALF_FLAG, 0)

        return nseg + jnp.where(nblk > 0, 1, 0), nblocks + nblk

    nseg, nblocks = lax.fori_loop(0, N_EXPERTS, scan_expert, (jnp.int32(0), jnp.int32(0)))
    pad_block(0)
    pad_block((nblocks + 1) * MOE_ROWS)
    pad_block((nblocks + 2) * MOE_ROWS)
    blk_seg[nblocks] = nseg - 1 + HALF_FLAG
    blk_seg[nblocks + 1] = nseg - 1 + HALF_FLAG

    for cp in weight_copies(seg_expert[0], 0):
        cp.start()

    for cp in table_copies:
        cp.wait()

    nvalid = jnp.where(last, GROUP_SLOTS, GROUP_PROMPT)
    for k in range(TOP_K):
        def fill(j, carry, k=k):
            c0 = k * K_STRIDE + j * SUBLANES
            for d in range(SUBLANES):
                src_ref[dest_ref[c0 + d]] = c0 + d
            return carry
        lax.fori_loop(0, nvalid // SUBLANES, fill, 0)

    for cp in prompt_copies():
        cp.wait()

    @pl.when(last)
    def _():
        for cp in sample_copies():
            cp.wait()

    def token_rows(code):
        slot_id = code & (K_STRIDE - 1)
        return pl.ds(pl.multiple_of(slot_id * LANE_CHUNKS, LANE_CHUNKS), LANE_CHUNKS)

    def gather(b, xs, lo=0, hi=MOE_ROWS):
        base = (b + 1) * MOE_ROWS
        for m in range(lo, hi):
            xs[pl.ds(m, LANE_CHUNKS, stride=XS_STRIDE), :] = h2buf[token_rows(src_ref[base + m]), :]

    def scatter_add(b, ys, lo=0, hi=MOE_ROWS):
        base = (b + 1) * MOE_ROWS
        for m0 in range(lo, hi, SCATTER_BATCH):
            pending = []
            for m in range(m0, m0 + SCATTER_BATCH):
                code = src_ref[base + m]
                rows = token_rows(code)
                pending.append((rows, acc[rows, :] + wts_ref[code] * ys[pl.ds(m, LANE_CHUNKS, stride=XS_STRIDE), :]))
            for rows, val in pending:
                acc[rows, :] = val

    gather_cuts = [round(i * MOE_ROWS / FFN_DOTS) for i in range(FFN_DOTS + 1)]
    scatter_cuts = [SCATTER_BATCH * round(i * (MOE_ROWS // SCATTER_BATCH) / FFN_DOTS) for i in range(FFN_DOTS + 1)]

    def step(b, xs_cur, xs_next, ys_cur, ys_prev):
        tag = blk_seg[b]
        seg = tag & (HALF_FLAG - 1)
        half = tag >= HALF_FLAG
        slot = seg & 1

        @pl.when(jnp.logical_and(b == seg_first[seg], b < nblocks))
        def _():
            for cp in weight_copies(seg_expert[seg], slot):
                cp.wait()

            @pl.when(seg + 1 < nseg)
            def _():
                for cp in weight_copies(seg_expert[seg + 1], 1 - slot):
                    cp.start()

        def work(rows):
            done = [0]

            def row_traffic():
                i = done[0]
                gather(b + 1, xs_next, gather_cuts[i], gather_cuts[i + 1])
                scatter_add(b - 1, ys_prev, scatter_cuts[i], scatter_cuts[i + 1])
                done[0] = i + 1

            x = jnp.concatenate(
                [xs_cur[c * XS_STRIDE:c * XS_STRIDE + rows, :] for c in range(LANE_CHUNKS)], axis=1).astype(BF16)
            acts = []
            for c in range(D_EXPERT // FFN_COLS):
                gc = pl.ds(c * FFN_COLS, FFN_COLS)
                uc = pl.ds(D_EXPERT + c * FFN_COLS, FFN_COLS)
                g = _bdot(x, wgubuf[slot, :, gc]) + bgubuf[slot, :, gc]
                row_traffic()
                u = _bdot(x, wgubuf[slot, :, uc]) + bgubuf[slot, :, uc]
                row_traffic()
                gl = jnp.minimum(g, SWIGLU_LIMIT)
                ul = jnp.clip(u, -SWIGLU_LIMIT, SWIGLU_LIMIT)
                acts.append((ul + 1.0) * (gl * jax.nn.sigmoid(SWIGLU_ALPHA * gl)))
            a = jnp.concatenate(acts, axis=1).astype(BF16)
            for c in range(D_MODEL // FFN_COLS):
                oc = pl.ds(c * FFN_COLS, FFN_COLS)
                y = _bdot(a, wdnbuf[slot, :, oc]) + bdnbuf[slot, :, oc]
                for j in range(FFN_COLS // LANES):
                    lc = c * (FFN_COLS // LANES) + j
                    ys_cur[lc * XS_STRIDE:lc * XS_STRIDE + rows, :] = y[:, j * LANES:(j + 1) * LANES]
                row_traffic()

        lax.cond(half, lambda: work(HALF_ROWS), lambda: work(MOE_ROWS))

    gather(0, xs0)
    npairs = (nblocks + 1) // 2

    def pair(t, carry):
        step(2 * t, xs0, xs1, ys0, ys1)
        step(2 * t + 1, xs1, xs0, ys1, ys0)
        return carry

    lax.fori_loop(0, npairs, pair, 0)
    scatter_add(2 * npairs - 1, ys1)

    for cp in output_copies():
        cp.start()

    @pl.when(last)
    def _():
        out_s = pltpu.make_async_copy(acc.at[pl.ds(GROUP_ROWS, SAMPLE_ROWS)], x2s_ref, act_sem.at[2])
        out_s.start()
        out_s.wait()

    for cp in output_copies():
        cp.wait()


def _moe(dest, wts, off, h2p, h2s, x1p, x1s, wgu, bgu, wdn, bdn):
    anyspec = pl.BlockSpec(memory_space=pl.ANY)
    dest = dest.reshape(N_GROUPS * TOP_K * K_STRIDE)
    wts = wts.reshape(N_GROUPS * TOP_K * K_STRIDE)
    in_specs = [
        pl.BlockSpec((None, N_EXPERTS, LANES), lambda g: (g, 0, 0), memory_space=pltpu.SMEM),
        anyspec, anyspec, anyspec, anyspec, anyspec, anyspec, anyspec, anyspec, anyspec, anyspec,
    ]
    scratch = [
        pltpu.VMEM((BUF_ROWS, LANES), F32),
        pltpu.VMEM((BUF_ROWS, LANES), F32),
        pltpu.VMEM((2, D_MODEL, 2 * D_EXPERT), BF16),
        pltpu.VMEM((2, 1, 2 * D_EXPERT), F32),
        pltpu.VMEM((2, D_EXPERT, D_MODEL), BF16),
        pltpu.VMEM((2, 1, D_MODEL), F32),
        pltpu.VMEM((LANE_CHUNKS * XS_STRIDE, LANES), F32),
        pltpu.VMEM((LANE_CHUNKS * XS_STRIDE, LANES), F32),
        pltpu.VMEM((LANE_CHUNKS * XS_STRIDE, LANES), F32),
        pltpu.VMEM((LANE_CHUNKS * XS_STRIDE, LANES), F32),
        pltpu.SMEM((TOP_K * K_STRIDE,), I32),
        pltpu.SMEM((TOP_K * K_STRIDE,), F32),
        pltpu.SMEM((POS_TABLE,), I32),
        pltpu.SMEM((N_EXPERTS,), I32),
        pltpu.SMEM((N_EXPERTS,), I32),
        pltpu.SMEM((LANES,), I32),
        pltpu.SemaphoreType.DMA((6,)),
        pltpu.SemaphoreType.DMA((4, 2)),
    ]
    return pl.pallas_call(
        _moe_kernel,
        grid=(N_GROUPS,),
        in_specs=in_specs,
        out_specs=[anyspec, anyspec],
        out_shape=[jax.ShapeDtypeStruct(x1p.shape, F32), jax.ShapeDtypeStruct(x1s.shape, F32)],
        scratch_shapes=scratch,
        compiler_params=pltpu.CompilerParams(dimension_semantics=("arbitrary",), vmem_limit_bytes=VMEM_LIMIT),
        name="moe",
    )(off, dest, wts, h2p, h2s, x1p, x1s, wgu, bgu, wdn, bdn)


def _ple_final_kernel(x2_ref, ple_ref, wple_ref, gple_ref, wpg_ref, gfin_ref, y_ref):
    rows = y_ref.shape[0]
    parts = 2 if rows % (2 * SUBLANES * BF16_ROWS) == 0 else 1
    r = rows // parts
    x2s, es, zs = [], [], []
    for h in range(parts):
        x2 = jnp.concatenate(
            [x2_ref[pl.ds(h * r * LANE_CHUNKS + c, r, stride=LANE_CHUNKS), :] for c in range(LANE_CHUNKS)], axis=1)
        x2s.append(x2)
        es.append(_bdot(ple_ref[h * r:(h + 1) * r, :].astype(BF16), wple_ref[...]))
        zs.append(_bdot(x2.astype(BF16), wpg_ref[...]))
    for h in range(parts):
        x3 = x2s[h] + jax.nn.sigmoid(zs[h]) * _rmsnorm(es[h], gple_ref[...])
        y_ref[h * r:(h + 1) * r, :] = _rmsnorm(x3, gfin_ref[...])


def _ple_final(x2_tm, ple, wple, gple, wpg, gfin, tile):
    n = ple.shape[0]
    return pl.pallas_call(
        _ple_final_kernel,
        grid=(n // tile,),
        in_specs=[
            pl.BlockSpec((tile * LANE_CHUNKS, LANES), lambda i: (i, 0)),
            pl.BlockSpec((tile, PLE_DIM), lambda i: (i, 0)),
            _full((PLE_DIM, D_MODEL)),
            _full((1, D_MODEL)),
            _full((D_MODEL, D_MODEL)),
            _full((1, D_MODEL)),
        ],
        out_specs=pl.BlockSpec((tile, D_MODEL), lambda i: (i, 0)),
        out_shape=jax.ShapeDtypeStruct((n, D_MODEL), F32),
        compiler_params=pltpu.CompilerParams(dimension_semantics=("arbitrary",), vmem_limit_bytes=VMEM_LIMIT),
        name="ple_final",
    )(x2_tm, ple, wple, gple, wpg, gfin)


def _rope_tables(pos):
    half = HEAD_DIM // 2
    inv = ROPE_THETA ** (-jnp.arange(half, dtype=F32) / half)
    ang = pos.astype(F32)[:, None] * inv[None, :]
    cos, sin = jnp.cos(ang), jnp.sin(ang)
    cos2 = jnp.concatenate([cos, cos, cos, cos], axis=1)
    sin2 = jnp.concatenate([-sin, sin, -sin, sin], axis=1)
    return cos2, sin2


def _layout_w_in(w_in):
    o_q = 2 * A_WIDTH
    wq = w_in[:, o_q:o_q + Q_WIDTH].reshape(D_MODEL, N_HEADS, HEAD_DIM) * (HEAD_DIM ** -0.5)
    kv_head = (jnp.arange(N_HEADS) // GQA_GROUP)[None, :, None]
    wq_pad = jnp.concatenate([jnp.where(kv_head == h, wq, 0.0) for h in range(N_KV_HEADS)], axis=-1)
    return jnp.concatenate([w_in[:, :o_q], wq_pad.reshape(D_MODEL, QPAD_WIDTH), w_in[:, o_q + Q_WIDTH:]], axis=1)


def _router_passes(w_router):
    hi = w_router.astype(BF16)
    lo = (w_router - hi.astype(F32)).astype(BF16)
    w3 = jnp.concatenate([hi, lo, hi], axis=0)
    return jnp.pad(w3, ((0, 0), (0, LANES - N_EXPERTS)))


def _prep_weights(g_mix, w_in, a_ln_g, a_ln_b, a_ws, a_bs, w_pa, w_pb, w_o, g_ffn, w_router, b_router):
    causal = jnp.tril(jnp.ones((CHUNK, CHUNK), dtype=bool))
    return dict(
        gmix=g_mix.reshape(1, D_MODEL),
        win=_layout_w_in(w_in).astype(BF16),
        lng=a_ln_g.reshape(1, A_WIDTH),
        lnb=a_ln_b.reshape(1, A_WIDTH),
        ws=jnp.where(causal[None], a_ws, 0.0).astype(BF16),
        bsf=jnp.repeat(jnp.transpose(a_bs), A_GROUP_DIM, axis=1),
        wpa=w_pa.astype(BF16),
        wpb=w_pb.astype(BF16),
        wo=w_o.astype(BF16),
        gffn=g_ffn.reshape(1, D_MODEL),
        wrt=_router_passes(w_router),
        br=b_router.reshape(N_EXPERTS, 1),
    )


def kernel(x_prompt, x_sample, cache_win_k, cache_win_v, p_prompt, p_sample, g_mix, w_in, a_ln_g, a_ln_b, a_ws, a_bs, sinks, w_pa, w_pb, w_o, g_ffn, w_router, b_router, w_gu, b_gu, w_down, b_down, w_ple, g_ple, w_ple_gate, g_final):
    W = _prep_weights(g_mix[0], w_in[0], a_ln_g[0], a_ln_b[0], a_ws[0], a_bs[0], w_pa[0], w_pb[0], w_o[0],
                      g_ffn[0], w_router[0], b_router[0])
    cos_p, sin_p = _rope_tables(jnp.arange(SEQ, dtype=I32))
    cos_s, sin_s = _rope_tables(jnp.full((1,), PAST_LEN, I32))
    x1p, h2p, logits_p, kwin_p, vwin_p, wgu16, wdn16 = _prompt_front(
        x_prompt.reshape(N_PROMPT, D_MODEL), cos_p, sin_p, W["gmix"], W["win"], W["lng"], W["lnb"],
        W["ws"], W["bsf"], sinks[0], _band_bias(), W["wpa"], W["wpb"], W["wo"], W["gffn"], W["wrt"], W["br"],
        w_gu[0].reshape(N_EXPERTS * D_MODEL, 2 * D_EXPERT), w_down[0].reshape(N_EXPERTS * D_EXPERT, D_MODEL))

    wdiag = jnp.repeat(a_ws[0, :, 0, 0], A_GROUP_DIM)[None, :].astype(BF16)
    bs0 = jnp.repeat(a_bs[0, :, 0], A_GROUP_DIM)[None, :]
    x1s, h2s, logits_s, kwin_s, vwin_s, va_s = _sample_front(
        x_sample.reshape(DEC_BATCH, D_MODEL), cos_s, sin_s, W["gmix"], W["win"], W["lng"], W["lnb"], wdiag, bs0,
        sinks[0], cache_win_k[0].reshape(DEC_BATCH, WINDOW, KV_WIDTH), cache_win_v[0].reshape(DEC_BATCH, WINDOW, KV_WIDTH),
        W["wpa"], W["wpb"], W["wo"], W["gffn"], W["wrt"], W["br"])

    dest, wts, off = _route_plan(logits_p, logits_s)
    x2p, x2s = _moe(dest, wts, off, h2p, h2s, x1p, x1s,
                    wgu16.reshape(N_EXPERTS, D_MODEL, 2 * D_EXPERT), b_gu[0].reshape(N_EXPERTS, 1, 2 * D_EXPERT),
                    wdn16.reshape(N_EXPERTS, D_EXPERT, D_MODEL), b_down[0].reshape(N_EXPERTS, 1, D_MODEL))

    wple = w_ple[0].astype(BF16)
    gple = g_ple[0].reshape(1, D_MODEL)
    wpg = w_ple_gate[0].astype(BF16)
    gfin = g_final.reshape(1, D_MODEL)
    y_p = _ple_final(x2p, p_prompt[0].reshape(N_PROMPT, PLE_DIM), wple, gple, wpg, gfin, 2 * TM)
    y_s = _ple_final(x2s, p_sample[0].reshape(DEC_BATCH, PLE_DIM), wple, gple, wpg, gfin, DEC_BATCH)

    return (
        y_p.reshape(BATCH, SEQ, D_MODEL),
        y_s.reshape(DEC_BATCH, 1, D_MODEL),
        kwin_p.reshape(1, BATCH, WINDOW, N_KV_HEADS, HEAD_DIM),
        vwin_p.reshape(1, BATCH, WINDOW, N_KV_HEADS, HEAD_DIM),
        kwin_s.reshape(1, DEC_BATCH, WINDOW, N_KV_HEADS, HEAD_DIM),
        vwin_s.reshape(1, DEC_BATCH, WINDOW, N_KV_HEADS, HEAD_DIM),
        va_s.reshape(1, DEC_BATCH, 1, A_WIDTH),
    )
```

```python
import numpy as np

import jax
import jax.numpy as jnp
from jax import lax
from jax.experimental import pallas as pl
from jax.experimental.pallas import tpu as pltpu

F32 = jnp.float32
BF16 = jnp.bfloat16
I32 = jnp.int32

D_MODEL = 1024
BATCH = 4
SEQ = 4096
DEC_BATCH = 128
PAST_LEN = 8192
CHUNK = 128
A_GROUPS = 4
A_GROUP_DIM = 128
A_WIDTH = A_GROUPS * A_GROUP_DIM
N_HEADS = 8
N_KV_HEADS = 2
HEAD_DIM = 64
Q_WIDTH = N_HEADS * HEAD_DIM
KV_WIDTH = N_KV_HEADS * HEAD_DIM
GQA_GROUP = N_HEADS // N_KV_HEADS
WINDOW = 128
ROPE_THETA = 10000.0
N_EXPERTS = 32
TOP_K = 4
D_EXPERT = D_MODEL
SWIGLU_ALPHA = 1.702
SWIGLU_LIMIT = 7.0
PLE_DIM = 256
RMS_EPS = 1e-5
LN_EPS = 1e-5

LANES = 128

QPAD_WIDTH = N_HEADS * LANES
O_Q = 2 * A_WIDTH
O_K = O_Q + Q_WIDTH
O_V = O_K + KV_WIDTH
O_GA = O_V + KV_WIDTH
O_GB = O_GA + D_MODEL
IN_COLS = O_GB + D_MODEL
SUBLANES = 8
LANE_CHUNKS = D_MODEL // LANES
VMEM_LIMIT = 56 * 1024 * 1024

N_PROMPT = BATCH * SEQ
TM = 256
TILES_PER_SEQ = SEQ // TM
BLOCKS_PER_TILE = TM // WINDOW
FRONT_STEPS = N_PROMPT // TM
CAST_ROWS = N_EXPERTS * D_MODEL // FRONT_STEPS
CAST_SPLIT = 4
IN_PROJ_DOTS = 4
BF16_ROWS = 16
CAST_CUTS = [BF16_ROWS * round(i * (CAST_ROWS // BF16_ROWS) / IN_PROJ_DOTS) for i in range(IN_PROJ_DOTS + 1)]

N_GROUPS = 4
GROUP_PROMPT = N_PROMPT // N_GROUPS
GROUP_SLOTS = GROUP_PROMPT + DEC_BATCH
GROUP_ASSIGN = GROUP_SLOTS * TOP_K
SLOT_TILES = GROUP_SLOTS // LANES
MOE_ROWS = 256
XS_STRIDE = MOE_ROWS + SUBLANES
HALF_ROWS = MOE_ROWS // 2
HALF_FLAG = 64
assert N_EXPERTS <= HALF_FLAG
SLOT_BITS = 13
K_STRIDE = 1 << SLOT_BITS
assert GROUP_SLOTS < K_STRIDE
MAX_BLOCKS = GROUP_ASSIGN // MOE_ROWS + N_EXPERTS
POS_TABLE = 1 << 15
assert (MAX_BLOCKS + 3) * MOE_ROWS <= POS_TABLE
assert MAX_BLOCKS + 2 <= LANES


def _bdot(a, b):
    return jnp.dot(a, b, preferred_element_type=F32)


def _rmsnorm(x, g):
    return x * lax.rsqrt(jnp.mean(x * x, axis=-1, keepdims=True) + RMS_EPS) * g


def _gelu(x):
    return 0.5 * x * (1.0 + lax.erf(x * (0.5 ** 0.5)))


def _group_layernorm(v, g, b):
    cols = []
    for gi in range(A_GROUPS):
        s = slice(gi * A_GROUP_DIM, (gi + 1) * A_GROUP_DIM)
        vg = v[:, s]
        mu = jnp.mean(vg, axis=-1, keepdims=True)
        d = vg - mu
        var = jnp.mean(d * d, axis=-1, keepdims=True)
        cols.append(d * lax.rsqrt(var + LN_EPS) * g[:, s] + b[:, s])
    return jnp.concatenate(cols, axis=1)


def _rope(x, cos, sin_signed):
    width = x.shape[1]
    reps = width // LANES
    cosf = jnp.concatenate([cos] * reps, axis=1) if reps > 1 else cos
    sinf = jnp.concatenate([sin_signed] * reps, axis=1) if reps > 1 else sin_signed
    half = HEAD_DIM // 2
    lane = lax.broadcasted_iota(I32, x.shape, 1)
    up = pltpu.roll(x, width - half, 1)
    down = pltpu.roll(x, half, 1)
    partner = jnp.where((lane & (HEAD_DIM - 1)) < half, up, down)
    return x * cosf + partner * sinf


def _pad_heads(q):
    lane = lax.broadcasted_iota(I32, (q.shape[0], LANES), 1)
    chunks = []
    for hq in range(N_HEADS):
        pair = q[:, (hq // 2) * LANES:(hq // 2 + 1) * LANES]
        kv_head = hq // GQA_GROUP
        if hq % 2 != kv_head:
            pair = pltpu.roll(pair, HEAD_DIM, 1)
        keep = (lane < HEAD_DIM) if kv_head == 0 else (lane >= HEAD_DIM)
        chunks.append(jnp.where(keep, pair, 0.0))
    return jnp.concatenate(chunks, axis=1)


def _in_projection(x, gmix_ref, win_ref, lng_ref, lnb_ref, cos, sin_signed, after_dot=lambda: None):
    hb = _rmsnorm(x, gmix_ref[...]).astype(BF16)

    def dot_cols(lo, hi):
        z = _bdot(hb, win_ref[:, lo:hi])
        after_dot()
        return z

    u = _gelu(dot_cols(0, A_WIDTH))
    va = _group_layernorm(_gelu(dot_cols(A_WIDTH, O_Q)), lng_ref[...], lnb_ref[...])
    zq = dot_cols(O_Q, O_K)
    zkv = dot_cols(O_K, O_GA)
    q = _pad_heads(_rope(zq, cos, sin_signed))
    k = _rope(zkv[:, :KV_WIDTH], cos, sin_signed)
    v = zkv[:, KV_WIDTH:]
    return u, va, q, k, v, hb


def _gate_preact(hb, win_ref, part, parts):
    width = 2 * D_MODEL // parts
    return _bdot(hb, win_ref[:, O_GA + part * width:O_GA + (part + 1) * width])


def _gates(preacts):
    zg = jnp.concatenate(preacts, axis=1)
    return jax.nn.sigmoid(zg[:, :D_MODEL]), jax.nn.sigmoid(zg[:, D_MODEL:])


def _merge_and_route(x, ya_in, att, gate_a, gate_b, wpa_ref, wpb_ref, wo_ref, gffn_ref, wr3_ref, br_ref,
                     x1_ref, h2_ref):
    ya = _bdot(ya_in.astype(BF16), wpa_ref[...])
    yb = _bdot(att.astype(BF16), wpb_ref[...])
    mix = (gate_a * ya + gate_b * yb).astype(BF16)
    x1 = x + _bdot(mix, wo_ref[...])
    _store_token_major(x1_ref, x1)
    h2 = _rmsnorm(x1, gffn_ref[...])
    _store_token_major(h2_ref, h2)
    hi = h2.astype(BF16)
    lo = (h2 - hi.astype(F32)).astype(BF16)
    logits = _bdot(jnp.concatenate([hi, hi, lo], axis=1), wr3_ref[...])
    return jnp.transpose(logits)[:N_EXPERTS, :] + br_ref[...]


def _top4_softmax(logits):
    eid = lax.broadcasted_iota(I32, logits.shape, 0)
    vals, idxs = [], []
    for _ in range(TOP_K):
        m = jnp.max(logits, axis=0, keepdims=True)
        idx = jnp.min(jnp.where(logits == m, eid, N_EXPERTS), axis=0, keepdims=True)
        logits = jnp.where(eid == idx, -jnp.inf, logits)
        vals.append(m)
        idxs.append(idx)
    es = [jnp.exp(v - vals[0]) for v in vals]
    inv = 1.0 / (es[0] + es[1] + es[2] + es[3])
    return jnp.concatenate(idxs, axis=0), jnp.concatenate([e * inv for e in es], axis=0)


def _store_token_major(ref, val):
    rows = val.shape[0]
    for c in range(LANE_CHUNKS):
        ref[pl.ds(c, rows, stride=LANE_CHUNKS), :] = val[:, c * LANES:(c + 1) * LANES]


def _load_token_major(ref, rows):
    return jnp.concatenate([ref[pl.ds(c, rows, stride=LANE_CHUNKS), :] for c in range(LANE_CHUNKS)], axis=1)


def _band_attention(qpad, k, v, k_prev, v_prev, sinks_ref, bias_ref, seq_start, after_scores):
    kb = jnp.concatenate([k_prev, k], axis=0).astype(BF16)
    vt = jnp.transpose(jnp.concatenate([v_prev, v], axis=0)).astype(BF16)
    qb = qpad.astype(BF16)
    lane = lax.broadcasted_iota(I32, (1, GQA_GROUP * WINDOW), 1)
    blocks = []
    for b in range(BLOCKS_PER_TILE):
        bias = bias_ref[jnp.where(seq_start, 1, 0)] if b == 0 else bias_ref[0]
        keys = kb[b * WINDOW:(b + 2) * WINDOW, :]
        pieces = []
        for h in range(N_KV_HEADS):
            qh = jnp.concatenate(
                [qb[b * WINDOW:(b + 1) * WINDOW, (h * GQA_GROUP + j) * LANES:(h * GQA_GROUP + j + 1) * LANES]
                 for j in range(GQA_GROUP)], axis=0)
            st = lax.dot_general(keys, qh, (((1,), (1,)), ((), ())), preferred_element_type=F32) + bias
            after_scores(b * N_KV_HEADS + h)
            sink = jnp.zeros((1, GQA_GROUP * WINDOW), F32)
            for j in range(GQA_GROUP):
                sink = jnp.where(lane // WINDOW == j, sinks_ref[h * GQA_GROUP + j], sink)
            m = jnp.maximum(jnp.max(st, axis=0, keepdims=True), sink)
            e = jnp.exp(st - m)
            inv = 1.0 / (jnp.sum(e, axis=0, keepdims=True) + jnp.exp(sink - m))
            ot = _bdot(vt[h * HEAD_DIM:(h + 1) * HEAD_DIM, b * WINDOW:(b + 2) * WINDOW], (e * inv).astype(BF16))
            pieces.extend(ot[:, j * WINDOW:(j + 1) * WINDOW] for j in range(GQA_GROUP))
        blocks.append(jnp.transpose(jnp.concatenate(pieces, axis=0)))
    return jnp.concatenate(blocks, axis=0)


def _band_bias():
    kj = np.arange(2 * WINDOW)[None, :, None]
    qi = (np.arange(GQA_GROUP * WINDOW) % WINDOW)[None, None, :]
    lo = (np.arange(2) * WINDOW)[:, None, None]
    valid = (kj > qi) & (kj <= qi + WINDOW) & (kj >= lo)
    return jnp.asarray(np.where(valid, 0.0, -np.inf).astype(np.float32))


def _prompt_front_kernel(x_ref, cos_ref, sin_ref, gmix_ref, win_ref, lng_ref, lnb_ref, ws_ref, bsf_ref,
                         sinks_ref, bias_ref, wpa_ref, wpb_ref, wo_ref, gffn_ref, wr3_ref, br_ref, wgu32_ref, wdn32_ref,
                         x1_ref, h2_ref, logits_ref, kwin_ref, vwin_ref, wgu16_ref, wdn16_ref,
                         kprev_ref, vprev_ref, gu_in, dn_in, gu_out, dn_out, cast_sem):
    i = pl.program_id(0)
    seq_start = (i % TILES_PER_SEQ) == 0

    def cast_rows(c, j):
        part = CAST_ROWS // CAST_SPLIT
        return pl.ds(pl.multiple_of(c * CAST_ROWS + j * part, part), part), pl.ds(j * part, part)

    def cast_in(c, slot):
        cps = []
        for j in range(CAST_SPLIT):
            hbm, loc = cast_rows(c, j)
            cps.append(pltpu.make_async_copy(wgu32_ref.at[hbm], gu_in.at[slot, loc], cast_sem.at[0, slot]))
            cps.append(pltpu.make_async_copy(wdn32_ref.at[hbm], dn_in.at[slot, loc], cast_sem.at[1, slot]))
        return cps

    def cast_out(c, slot):
        cps = []
        for j in range(CAST_SPLIT):
            hbm, loc = cast_rows(c, j)
            cps.append(pltpu.make_async_copy(gu_out.at[slot, loc], wgu16_ref.at[hbm], cast_sem.at[2, slot]))
            cps.append(pltpu.make_async_copy(dn_out.at[slot, loc], wdn16_ref.at[hbm], cast_sem.at[3, slot]))
        return cps

    slot = i & 1

    @pl.when(i == 0)
    def _():
        for cp in cast_in(0, 0):
            cp.start()

    @pl.when(i + 1 < FRONT_STEPS)
    def _():
        for cp in cast_in(i + 1, 1 - slot):
            cp.start()

    for cp in cast_in(i, slot):
        cp.wait()

    @pl.when(i >= 2)
    def _():
        for cp in cast_out(i - 2, slot):
            cp.wait()


    @pl.when(seq_start)
    def _():
        kprev_ref[...] = jnp.zeros_like(kprev_ref)
        vprev_ref[...] = jnp.zeros_like(vprev_ref)

    x = x_ref[...]
    cast_done = [0]

    def cast_slice():
        i = cast_done[0]
        r = pl.ds(CAST_CUTS[i], CAST_CUTS[i + 1] - CAST_CUTS[i])
        gu_out[slot, r, :] = gu_in[slot, r, :].astype(BF16)
        dn_out[slot, r, :] = dn_in[slot, r, :].astype(BF16)
        cast_done[0] += 1

    u, va, q, k, v, hb = _in_projection(
        x, gmix_ref, win_ref, lng_ref, lnb_ref, cos_ref[...], sin_ref[...], cast_slice)
    assert cast_done[0] == IN_PROJ_DOTS
    for cp in cast_out(i, slot):
        cp.start()

    units = BLOCKS_PER_TILE * N_KV_HEADS
    preacts = []
    att = _band_attention(q, k, v, kprev_ref[...], vprev_ref[...], sinks_ref, bias_ref, seq_start,
                          lambda unit: preacts.append(_gate_preact(hb, win_ref, unit, units)))
    gate_a, gate_b = _gates(preacts)

    k_last, v_last = k[TM - WINDOW:], v[TM - WINDOW:]
    kprev_ref[...] = k_last
    vprev_ref[...] = v_last
    kwin_ref[0] = k_last
    vwin_ref[0] = v_last

    vab = va.astype(BF16)
    zc = jnp.concatenate(
        [jnp.concatenate(
            [_bdot(ws_ref[g], vab[b * CHUNK:(b + 1) * CHUNK, g * A_GROUP_DIM:(g + 1) * A_GROUP_DIM])
             for g in range(A_GROUPS)], axis=1) + bsf_ref[...]
         for b in range(BLOCKS_PER_TILE)], axis=0)

    logits_ref[...] = _merge_and_route(x, u * zc, att, gate_a, gate_b,
                                       wpa_ref, wpb_ref, wo_ref, gffn_ref, wr3_ref, br_ref, x1_ref, h2_ref)

    @pl.when(i == FRONT_STEPS - 1)
    def _():
        for cp in cast_out(i - 1, 1 - slot) + cast_out(i, slot):
            cp.wait()


def _full(shape):
    return pl.BlockSpec(shape, lambda i: (0,) * len(shape))


def _prompt_front(x, cos, sin, gmix, win, lng, lnb, ws, bsf, sinks, bias, wpa, wpb, wo, gffn, wrt, br, wgu32, wdn32):
    n = x.shape[0]
    assert n == N_PROMPT
    grid = (FRONT_STEPS,)
    anyspec = pl.BlockSpec(memory_space=pl.ANY)
    in_specs = [
        pl.BlockSpec((TM, D_MODEL), lambda i: (i, 0)),
        pl.BlockSpec((TM, LANES), lambda i: (i % TILES_PER_SEQ, 0)),
        pl.BlockSpec((TM, LANES), lambda i: (i % TILES_PER_SEQ, 0)),
        _full((1, D_MODEL)),
        _full((D_MODEL, IN_COLS)),
        _full((1, A_WIDTH)),
        _full((1, A_WIDTH)),
        _full((A_GROUPS, CHUNK, CHUNK)),
        _full((CHUNK, A_WIDTH)),
        pl.BlockSpec(memory_space=pltpu.SMEM),
        _full((2, 2 * WINDOW, GQA_GROUP * WINDOW)),
        _full((A_WIDTH, D_MODEL)),
        _full((Q_WIDTH, D_MODEL)),
        _full((D_MODEL, D_MODEL)),
        _full((1, D_MODEL)),
        _full((3 * D_MODEL, LANES)),
        _full((N_EXPERTS, 1)),
        anyspec,
        anyspec,
    ]
    out_shape = [
        jax.ShapeDtypeStruct((n * LANE_CHUNKS, LANES), F32),
        jax.ShapeDtypeStruct((n * LANE_CHUNKS, LANES), F32),
        jax.ShapeDtypeStruct((N_EXPERTS, n), F32),
        jax.ShapeDtypeStruct((n // SEQ, WINDOW, KV_WIDTH), F32),
        jax.ShapeDtypeStruct((n // SEQ, WINDOW, KV_WIDTH), F32),
        jax.ShapeDtypeStruct(wgu32.shape, BF16),
        jax.ShapeDtypeStruct(wdn32.shape, BF16),
    ]
    out_specs = [
        pl.BlockSpec((TM * LANE_CHUNKS, LANES), lambda i: (i, 0)),
        pl.BlockSpec((TM * LANE_CHUNKS, LANES), lambda i: (i, 0)),
        pl.BlockSpec((N_EXPERTS, TM), lambda i: (0, i)),
        pl.BlockSpec((1, WINDOW, KV_WIDTH), lambda i: (i // TILES_PER_SEQ, 0, 0)),
        pl.BlockSpec((1, WINDOW, KV_WIDTH), lambda i: (i // TILES_PER_SEQ, 0, 0)),
        anyspec,
        anyspec,
    ]
    scratch = [
        pltpu.VMEM((WINDOW, KV_WIDTH), F32),
        pltpu.VMEM((WINDOW, KV_WIDTH), F32),
        pltpu.VMEM((2, CAST_ROWS, 2 * D_EXPERT), F32),
        pltpu.VMEM((2, CAST_ROWS, D_MODEL), F32),
        pltpu.VMEM((2, CAST_ROWS, 2 * D_EXPERT), BF16),
        pltpu.VMEM((2, CAST_ROWS, D_MODEL), BF16),
        pltpu.SemaphoreType.DMA((4, 2)),
    ]
    return pl.pallas_call(
        _prompt_front_kernel,
        grid=grid,
        in_specs=in_specs,
        out_specs=out_specs,
        out_shape=out_shape,
        scratch_shapes=scratch,
        compiler_params=pltpu.CompilerParams(dimension_semantics=("arbitrary",), vmem_limit_bytes=VMEM_LIMIT),
        name="prompt_front",
    )(x, cos, sin, gmix, win, lng, lnb, ws, bsf, sinks, bias, wpa, wpb, wo, gffn, wrt, br, wgu32, wdn32)


SAMPLE_STEP = 16
SAMPLE_STEPS = DEC_BATCH // SAMPLE_STEP


def _sample_kernel(x_ref, cos_ref, sin_ref, gmix_ref, win_ref, lng_ref, lnb_ref, wdiag_ref, bs0_ref, sinks_ref,
                   kc_ref, vc_ref, wpa_ref, wpb_ref, wo_ref, gffn_ref, wr3_ref, br_ref,
                   x1_ref, h2_ref, logits_ref, kwin_ref, vwin_ref, va_ref,
                   q_s, k_s, v_s, yain_s, ga_s, gb_s, att_s):
    i = pl.program_id(0)

    @pl.when(i == 0)
    def _():
        x = x_ref[...]
        cos = jnp.broadcast_to(cos_ref[...], (DEC_BATCH, LANES))
        sin = jnp.broadcast_to(sin_ref[...], (DEC_BATCH, LANES))
        u, va, q, k, v, hb = _in_projection(x, gmix_ref, win_ref, lng_ref, lnb_ref, cos, sin)
        gate_a, gate_b = _gates([_gate_preact(hb, win_ref, 0, 1)])
        va_ref[...] = va
        z = wdiag_ref[...].astype(F32) * va.astype(BF16).astype(F32) + bs0_ref[...]
        yain_s[...] = u * z
        q_s[...] = q
        k_s[...] = k
        v_s[...] = v
        ga_s[...] = gate_a
        gb_s[...] = gate_b

    r0 = pl.multiple_of(i * SAMPLE_STEP, SAMPLE_STEP)
    kwin = jnp.concatenate([kc_ref[:, 1:, :], k_s[pl.ds(r0, SAMPLE_STEP), :][:, None, :]], axis=1)
    vwin = jnp.concatenate([vc_ref[:, 1:, :], v_s[pl.ds(r0, SAMPLE_STEP), :][:, None, :]], axis=1)
    kwin_ref[...] = kwin
    vwin_ref[...] = vwin

    q16 = q_s[pl.ds(r0, SAMPLE_STEP), :]
    lane = lax.broadcasted_iota(I32, (SAMPLE_STEP, LANES), 1)
    heads = [q16[:, hq * LANES:(hq + 1) * LANES] for hq in range(N_HEADS)]
    qpad = pltpu.einshape("hbd->bhd", jnp.stack(heads, axis=0)).astype(BF16)
    s = jnp.einsum("bhd,bkd->bhk", qpad, kwin.astype(BF16), preferred_element_type=F32)
    hid = lax.broadcasted_iota(I32, (1, N_HEADS, 1), 1)
    sink = jnp.zeros((1, N_HEADS, 1), F32)
    for hq in range(N_HEADS):
        sink = jnp.where(hid == hq, sinks_ref[hq], sink)
    m = jnp.maximum(jnp.max(s, axis=-1, keepdims=True), sink)
    e = jnp.exp(s - m)
    inv = 1.0 / (jnp.sum(e, axis=-1, keepdims=True) + jnp.exp(sink - m))
    o = jnp.einsum("bhk,bkd->bhd", (e * inv).astype(BF16), vwin.astype(BF16), preferred_element_type=F32)
    o = pltpu.einshape("bhd->hbd", o)
    chunks = []
    for c in range(N_HEADS // 2):
        parts = []
        for p in range(2):
            hq = 2 * c + p
            oh = o[hq]
            if p != hq // GQA_GROUP:
                oh = pltpu.roll(oh, HEAD_DIM, 1)
            parts.append(oh)
        chunks.append(jnp.where(lane < HEAD_DIM, parts[0], parts[1]))
    att_s[pl.ds(r0, SAMPLE_STEP), :] = jnp.concatenate(chunks, axis=1)

    @pl.when(i == SAMPLE_STEPS - 1)
    def _():
        logits_ref[...] = _merge_and_route(
            x_ref[...], yain_s[...], att_s[...], ga_s[...], gb_s[...],
            wpa_ref, wpb_ref, wo_ref, gffn_ref, wr3_ref, br_ref, x1_ref, h2_ref)


def _sample_front(x, cos, sin, gmix, win, lng, lnb, wdiag, bs0, sinks, kc, vc, wpa, wpb, wo, gffn, wrt, br):
    n = DEC_BATCH
    cache_spec = pl.BlockSpec((SAMPLE_STEP, WINDOW, KV_WIDTH), lambda i: (i, 0, 0))
    in_specs = [
        _full((n, D_MODEL)),
        _full((1, LANES)),
        _full((1, LANES)),
        _full((1, D_MODEL)),
        _full((D_MODEL, IN_COLS)),
        _full((1, A_WIDTH)),
        _full((1, A_WIDTH)),
        _full((1, A_WIDTH)),
        _full((1, A_WIDTH)),
        pl.BlockSpec(memory_space=pltpu.SMEM),
        cache_spec,
        cache_spec,
        _full((A_WIDTH, D_MODEL)),
        _full((Q_WIDTH, D_MODEL)),
        _full((D_MODEL, D_MODEL)),
        _full((1, D_MODEL)),
        _full((3 * D_MODEL, LANES)),
        _full((N_EXPERTS, 1)),
    ]
    out_shape = [
        jax.ShapeDtypeStruct((n * LANE_CHUNKS, LANES), F32),
        jax.ShapeDtypeStruct((n * LANE_CHUNKS, LANES), F32),
        jax.ShapeDtypeStruct((N_EXPERTS, n), F32),
        jax.ShapeDtypeStruct((n, WINDOW, KV_WIDTH), F32),
        jax.ShapeDtypeStruct((n, WINDOW, KV_WIDTH), F32),
        jax.ShapeDtypeStruct((n, A_WIDTH), F32),
    ]
    out_specs = [
        _full((n * LANE_CHUNKS, LANES)),
        _full((n * LANE_CHUNKS, LANES)),
        _full((N_EXPERTS, n)),
        cache_spec,
        cache_spec,
        _full((n, A_WIDTH)),
    ]
    scratch = [
        pltpu.VMEM((n, QPAD_WIDTH), F32), pltpu.VMEM((n, KV_WIDTH), F32), pltpu.VMEM((n, KV_WIDTH), F32),
        pltpu.VMEM((n, A_WIDTH), F32), pltpu.VMEM((n, D_MODEL), F32), pltpu.VMEM((n, D_MODEL), F32),
        pltpu.VMEM((n, Q_WIDTH), F32),
    ]
    return pl.pallas_call(
        _sample_kernel,
        grid=(SAMPLE_STEPS,),
        in_specs=in_specs,
        out_specs=out_specs,
        out_shape=out_shape,
        scratch_shapes=scratch,
        compiler_params=pltpu.CompilerParams(dimension_semantics=("arbitrary",), vmem_limit_bytes=VMEM_LIMIT),
        name="sample_front",
    )(x, cos, sin, gmix, win, lng, lnb, wdiag, bs0, sinks, kc, vc, wpa, wpb, wo, gffn, wrt, br)


def _route_plan_kernel(lp_ref, ls_ref, dest_ref, wts_ref, off_ref):
    g = pl.program_id(0)
    topi, topw = _top4_softmax(jnp.concatenate([lp_ref[...], ls_ref[...]], axis=1))
    slot = lax.broadcasted_iota(I32, (TOP_K, GROUP_SLOTS), 1)
    eall = jnp.where(jnp.logical_or(slot < GROUP_PROMPT, g == N_GROUPS - 1), topi, N_EXPERTS)
    wts_ref[:, 0:GROUP_SLOTS] = topw
    wts_ref[:, GROUP_SLOTS:] = jnp.zeros((TOP_K, K_STRIDE - GROUP_SLOTS), F32)
    dest_ref[:, GROUP_SLOTS:] = jnp.zeros((TOP_K, K_STRIDE - GROUP_SLOTS), I32)
    eid = lax.broadcasted_iota(I32, (N_EXPERTS, GROUP_SLOTS), 0)
    onehots = [eall[k:k + 1, :] == eid for k in range(TOP_K)]
    count = jnp.zeros((N_EXPERTS, GROUP_SLOTS), F32)
    for oh in onehots:
        count = count + oh.astype(F32)
    total = jnp.broadcast_to(jnp.sum(count, axis=1, keepdims=True), (N_EXPERTS, LANES))
    padded = total + (MOE_ROWS - 1)
    nblk = jnp.floor(padded * (1.0 / MOE_ROWS))
    rem = padded - nblk * MOE_ROWS
    nblk = jnp.where(rem >= MOE_ROWS, nblk + 1.0, jnp.where(rem < 0.0, nblk - 1.0, nblk))
    r = lax.broadcasted_iota(I32, (N_EXPERTS, N_EXPERTS), 0)
    c = lax.broadcasted_iota(I32, (N_EXPERTS, N_EXPERTS), 1)
    first_blk = lax.dot_general((c < r).astype(F32), nblk, (((1,), (0,)), ((), ())),
                                precision=lax.Precision.HIGHEST, preferred_element_type=F32)
    start = (first_blk + 1.0) * MOE_ROWS
    lane = lax.broadcasted_iota(I32, (N_EXPERTS, LANES), 1)
    info = jnp.where(lane == 0, start, jnp.where(lane == 1, start + total, jnp.where(lane == 2, nblk, first_blk)))
    off_ref[...] = info.astype(I32)
    ti = lax.broadcasted_iota(I32, (LANES, LANES), 0)
    tj = lax.broadcasted_iota(I32, (LANES, LANES), 1)
    before = (ti < tj).astype(BF16)
    ones = jnp.ones((LANES, LANES), BF16)
    running = start
    for t in range(SLOT_TILES):
        sl = slice(t * LANES, (t + 1) * LANES)
        cb = count[:, sl].astype(BF16)
        pos = running + _bdot(cb, before)
        rows = [jnp.sum(jnp.where(oh[:, sl], pos, 0.0), axis=0, keepdims=True) for oh in onehots]
        dest_ref[:, sl] = jnp.concatenate(rows, axis=0).astype(I32)
        running = running + _bdot(cb, ones)


def _route_plan(logits_p, logits_s):
    in_specs = [
        pl.BlockSpec((N_EXPERTS, GROUP_PROMPT), lambda g: (0, g)),
        pl.BlockSpec((N_EXPERTS, DEC_BATCH), lambda g: (0, 0)),
    ]
    out_shape = [
        jax.ShapeDtypeStruct((N_GROUPS, TOP_K, K_STRIDE), I32),
        jax.ShapeDtypeStruct((N_GROUPS, TOP_K, K_STRIDE), F32),
        jax.ShapeDtypeStruct((N_GROUPS, N_EXPERTS, LANES), I32),
    ]
    out_specs = [
        pl.BlockSpec((None, TOP_K, K_STRIDE), lambda g: (g, 0, 0)),
        pl.BlockSpec((None, TOP_K, K_STRIDE), lambda g: (g, 0, 0)),
        pl.BlockSpec((None, N_EXPERTS, LANES), lambda g: (g, 0, 0)),
    ]
    return pl.pallas_call(
        _route_plan_kernel,
        grid=(N_GROUPS,),
        in_specs=in_specs,
        out_specs=out_specs,
        out_shape=out_shape,
        compiler_params=pltpu.CompilerParams(dimension_semantics=("arbitrary",)),
        name="route_plan",
    )(logits_p, logits_s)


GROUP_ROWS = GROUP_PROMPT * LANE_CHUNKS
SAMPLE_ROWS = DEC_BATCH * LANE_CHUNKS
TRASH_SLOT = GROUP_SLOTS
BUF_ROWS = (GROUP_SLOTS + 1) * LANE_CHUNKS
SCATTER_BATCH = 8
DMA_SPLIT = 8
TABLE_USED = GROUP_SLOTS + LANES
assert TRASH_SLOT < TABLE_USED <= K_STRIDE
FFN_COLS = 256
FFN_DOTS = (2 * D_EXPERT + D_MODEL) // FFN_COLS


def _moe_kernel(off_ref, desth_ref, wtsh_ref, h2p_ref, h2s_ref, x1p_ref, x1s_ref, wgu_ref, bgu_ref, wdn_ref, bdn_ref,
                x2p_ref, x2s_ref,
                h2buf, acc, wgubuf, bgubuf, wdnbuf, bdnbuf, xs0, xs1, ys0, ys1,
                dest_ref, wts_ref, src_ref, seg_expert, seg_first, blk_seg, act_sem, w_sem):
    g = pl.program_id(0)
    last = g == N_GROUPS - 1
    row0 = pl.multiple_of(g * GROUP_ROWS, GROUP_ROWS)

    def prompt_copies():
        cps = []
        for j in range(DMA_SPLIT):
            src = pl.ds(row0 + j * (GROUP_ROWS // DMA_SPLIT), GROUP_ROWS // DMA_SPLIT)
            dst = pl.ds(j * (GROUP_ROWS // DMA_SPLIT), GROUP_ROWS // DMA_SPLIT)
            cps.append(pltpu.make_async_copy(h2p_ref.at[src], h2buf.at[dst], act_sem.at[0]))
            cps.append(pltpu.make_async_copy(x1p_ref.at[src], acc.at[dst], act_sem.at[1]))
        return cps

    def sample_copies():
        return (pltpu.make_async_copy(h2s_ref, h2buf.at[pl.ds(GROUP_ROWS, SAMPLE_ROWS)], act_sem.at[2]),
                pltpu.make_async_copy(x1s_ref, acc.at[pl.ds(GROUP_ROWS, SAMPLE_ROWS)], act_sem.at[3]))

    def weight_copies(e, slot):
        cps = [pltpu.make_async_copy(bgu_ref.at[e], bgubuf.at[slot], w_sem.at[1, slot]),
               pltpu.make_async_copy(bdn_ref.at[e], bdnbuf.at[slot], w_sem.at[3, slot])]
        for j in range(DMA_SPLIT):
            rg = pl.ds(j * (D_MODEL // DMA_SPLIT), D_MODEL // DMA_SPLIT)
            rd = pl.ds(j * (D_EXPERT // DMA_SPLIT), D_EXPERT // DMA_SPLIT)
            cps.append(pltpu.make_async_copy(wgu_ref.at[e, rg], wgubuf.at[slot, rg], w_sem.at[0, slot]))
            cps.append(pltpu.make_async_copy(wdn_ref.at[e, rd], wdnbuf.at[slot, rd], w_sem.at[2, slot]))
        return cps

    def output_copies():
        return [pltpu.make_async_copy(
            acc.at[pl.ds(j * (GROUP_ROWS // DMA_SPLIT), GROUP_ROWS // DMA_SPLIT)],
            x2p_ref.at[pl.ds(row0 + j * (GROUP_ROWS // DMA_SPLIT), GROUP_ROWS // DMA_SPLIT)], act_sem.at[0])
            for j in range(DMA_SPLIT)]

    tab0 = pl.multiple_of(g * (TOP_K * K_STRIDE), TOP_K * K_STRIDE)
    table_copies = []
    for k in range(TOP_K):
        used = pl.ds(k * K_STRIDE, TABLE_USED)
        table_copies.append(pltpu.make_async_copy(
            desth_ref.at[pl.ds(tab0 + k * K_STRIDE, TABLE_USED)], dest_ref.at[used], act_sem.at[4]))
        table_copies.append(pltpu.make_async_copy(
            wtsh_ref.at[pl.ds(tab0 + k * K_STRIDE, TABLE_USED)], wts_ref.at[used], act_sem.at[5]))
    for cp in table_copies:
        cp.start()

    for cp in prompt_copies():
        cp.start()

    @pl.when(last)
    def _():
        for cp in sample_copies():
            cp.start()

    trash = pl.ds(TRASH_SLOT * LANE_CHUNKS, LANE_CHUNKS)
    h2buf[trash, :] = jnp.zeros((LANE_CHUNKS, LANES), F32)
    acc[trash, :] = jnp.zeros((LANE_CHUNKS, LANES), F32)
    ys0[...] = jnp.zeros_like(ys0)
    ys1[...] = jnp.zeros_like(ys1)

    def pad_block(pos0, real_rows=0):
        def body(j, carry):
            for d in range(SUBLANES):
                src_ref[pos0 + MOE_ROWS - SUBLANES - j * SUBLANES + d] = TRASH_SLOT
            return carry
        lax.fori_loop(0, (MOE_ROWS - real_rows + SUBLANES - 1) // SUBLANES, body, 0)

    def scan_expert(e, carry):
        nseg, nblocks = carry
        nblk = off_ref[e, 2]
        first = off_ref[e, 3]

        @pl.when(nblk > 0)
        def _():
            seg_expert[nseg] = e
            seg_first[nseg] = first
            tail_rows = off_ref[e, 1] - off_ref[e, 0] - (nblk - 1) * MOE_ROWS
            pad_block(off_ref[e, 0] + (nblk - 1) * MOE_ROWS, tail_rows)

            def mark(b, c2):
                blk_seg[first + b] = nseg
                return c2
            lax.fori_loop(0, nblk, mark, 0)

            blk_seg[first + nblk - 1] = nseg + jnp.where(tail_rows <= HALF_ROWS, HALF_FLAG, 0)

        return nseg + jnp.where(nblk > 0, 1, 0), nblocks + nblk

    nseg, nblocks = lax.fori_loop(0, N_EXPERTS, scan_expert, (jnp.int32(0), jnp.int32(0)))
    pad_block(0)
    pad_block((nblocks + 1) * MOE_ROWS)
    pad_block((nblocks + 2) * MOE_ROWS)
    blk_seg[nblocks] = nseg - 1 + HALF_FLAG
    blk_seg[nblocks + 1] = nseg - 1 + HALF_FLAG

    for cp in weight_copies(seg_expert[0], 0):
        cp.start()

    for cp in table_copies:
        cp.wait()

    nvalid = jnp.where(last, GROUP_SLOTS, GROUP_PROMPT)
    for k in range(TOP_K):
        def fill(j, carry, k=k):
            c0 = k * K_STRIDE + j * SUBLANES
            for d in range(SUBLANES):
                src_ref[dest_ref[c0 + d]] = c0 + d
            return carry
        lax.fori_loop(0, nvalid // SUBLANES, fill, 0)

    for cp in prompt_copies():
        cp.wait()

    @pl.when(last)
    def _():
        for cp in sample_copies():
            cp.wait()

    def token_rows(code):
        slot_id = code & (K_STRIDE - 1)
        return pl.ds(pl.multiple_of(slot_id * LANE_CHUNKS, LANE_CHUNKS), LANE_CHUNKS)

    def gather(b, xs, lo=0, hi=MOE_ROWS):
        base = (b + 1) * MOE_ROWS
        for m in range(lo, hi):
            xs[pl.ds(m, LANE_CHUNKS, stride=XS_STRIDE), :] = h2buf[token_rows(src_ref[base + m]), :]

    def scatter_add(b, ys, lo=0, hi=MOE_ROWS):
        base = (b + 1) * MOE_ROWS
        for m0 in range(lo, hi, SCATTER_BATCH):
            pending = []
            for m in range(m0, m0 + SCATTER_BATCH):
                code = src_ref[base + m]
                rows = token_rows(code)
                pending.append((rows, acc[rows, :] + wts_ref[code] * ys[pl.ds(m, LANE_CHUNKS, stride=XS_STRIDE), :]))
            for rows, val in pending:
                acc[rows, :] = val

    gather_cuts = [round(i * MOE_ROWS / FFN_DOTS) for i in range(FFN_DOTS + 1)]
    scatter_cuts = [SCATTER_BATCH * round(i * (MOE_ROWS // SCATTER_BATCH) / FFN_DOTS) for i in range(FFN_DOTS + 1)]

    def step(b, xs_cur, xs_next, ys_cur, ys_prev):
        tag = blk_seg[b]
        seg = tag & (HALF_FLAG - 1)
        half = tag >= HALF_FLAG
        slot = seg & 1

        @pl.when(jnp.logical_and(b == seg_first[seg], b < nblocks))
        def _():
            for cp in weight_copies(seg_expert[seg], slot):
                cp.wait()

            @pl.when(seg + 1 < nseg)
            def _():
                for cp in weight_copies(seg_expert[seg + 1], 1 - slot):
                    cp.start()

        def work(rows):
            done = [0]

            def row_traffic():
                i = done[0]
                gather(b + 1, xs_next, gather_cuts[i], gather_cuts[i + 1])
                scatter_add(b - 1, ys_prev, scatter_cuts[i], scatter_cuts[i + 1])
                done[0] = i + 1

            x = jnp.concatenate(
                [xs_cur[c * XS_STRIDE:c * XS_STRIDE + rows, :] for c in range(LANE_CHUNKS)], axis=1).astype(BF16)
            acts = []
            for c in range(D_EXPERT // FFN_COLS):
                gc = pl.ds(c * FFN_COLS, FFN_COLS)
                uc = pl.ds(D_EXPERT + c * FFN_COLS, FFN_COLS)
                g = _bdot(x, wgubuf[slot, :, gc]) + bgubuf[slot, :, gc]
                row_traffic()
                u = _bdot(x, wgubuf[slot, :, uc]) + bgubuf[slot, :, uc]
                row_traffic()
                gl = jnp.minimum(g, SWIGLU_LIMIT)
                ul = jnp.clip(u, -SWIGLU_LIMIT, SWIGLU_LIMIT)
                acts.append((ul + 1.0) * (gl * jax.nn.sigmoid(SWIGLU_ALPHA * gl)))
            a = jnp.concatenate(acts, axis=1).astype(BF16)
            for c in range(D_MODEL // FFN_COLS):
                oc = pl.ds(c * FFN_COLS, FFN_COLS)
                y = _bdot(a, wdnbuf[slot, :, oc]) + bdnbuf[slot, :, oc]
                for j in range(FFN_COLS // LANES):
                    lc = c * (FFN_COLS // LANES) + j
                    ys_cur[lc * XS_STRIDE:lc * XS_STRIDE + rows, :] = y[:, j * LANES:(j + 1) * LANES]
                row_traffic()

        lax.cond(half, lambda: work(HALF_ROWS), lambda: work(MOE_ROWS))

    gather(0, xs0)
    npairs = (nblocks + 1) // 2

    def pair(t, carry):
        step(2 * t, xs0, xs1, ys0, ys1)
        step(2 * t + 1, xs1, xs0, ys1, ys0)
        return carry

    lax.fori_loop(0, npairs, pair, 0)
    scatter_add(2 * npairs - 1, ys1)

    for cp in output_copies():
        cp.start()

    @pl.when(last)
    def _():
        out_s = pltpu.make_async_copy(acc.at[pl.ds(GROUP_ROWS, SAMPLE_ROWS)], x2s_ref, act_sem.at[2])
        out_s.start()
        out_s.wait()

    for cp in output_copies():
        cp.wait()


def _moe(dest, wts, off, h2p, h2s, x1p, x1s, wgu, bgu, wdn, bdn):
    anyspec = pl.BlockSpec(memory_space=pl.ANY)
    dest = dest.reshape(N_GROUPS * TOP_K * K_STRIDE)
    wts = wts.reshape(N_GROUPS * TOP_K * K_STRIDE)
    in_specs = [
        pl.BlockSpec((None, N_EXPERTS, LANES), lambda g: (g, 0, 0), memory_space=pltpu.SMEM),
        anyspec, anyspec, anyspec, anyspec, anyspec, anyspec, anyspec, anyspec, anyspec, anyspec,
    ]
    scratch = [
        pltpu.VMEM((BUF_ROWS, LANES), F32),
        pltpu.VMEM((BUF_ROWS, LANES), F32),
        pltpu.VMEM((2, D_MODEL, 2 * D_EXPERT), BF16),
        pltpu.VMEM((2, 1, 2 * D_EXPERT), F32),
        pltpu.VMEM((2, D_EXPERT, D_MODEL), BF16),
        pltpu.VMEM((2, 1, D_MODEL), F32),
        pltpu.VMEM((LANE_CHUNKS * XS_STRIDE, LANES), F32),
        pltpu.VMEM((LANE_CHUNKS * XS_STRIDE, LANES), F32),
        pltpu.VMEM((LANE_CHUNKS * XS_STRIDE, LANES), F32),
        pltpu.VMEM((LANE_CHUNKS * XS_STRIDE, LANES), F32),
        pltpu.SMEM((TOP_K * K_STRIDE,), I32),
        pltpu.SMEM((TOP_K * K_STRIDE,), F32),
        pltpu.SMEM((POS_TABLE,), I32),
        pltpu.SMEM((N_EXPERTS,), I32),
        pltpu.SMEM((N_EXPERTS,), I32),
        pltpu.SMEM((LANES,), I32),
        pltpu.SemaphoreType.DMA((6,)),
        pltpu.SemaphoreType.DMA((4, 2)),
    ]
    return pl.pallas_call(
        _moe_kernel,
        grid=(N_GROUPS,),
        in_specs=in_specs,
        out_specs=[anyspec, anyspec],
        out_shape=[jax.ShapeDtypeStruct(x1p.shape, F32), jax.ShapeDtypeStruct(x1s.shape, F32)],
        scratch_shapes=scratch,
        compiler_params=pltpu.CompilerParams(dimension_semantics=("arbitrary",), vmem_limit_bytes=VMEM_LIMIT),
        name="moe",
    )(off, dest, wts, h2p, h2s, x1p, x1s, wgu, bgu, wdn, bdn)


def _ple_final_kernel(x2_ref, ple_ref, wple_ref, gple_ref, wpg_ref, gfin_ref, y_ref):
    rows = y_ref.shape[0]
    x2 = _load_token_major(x2_ref, rows)
    e = _rmsnorm(_bdot(ple_ref[...].astype(BF16), wple_ref[...]), gple_ref[...])
    x3 = x2 + jax.nn.sigmoid(_bdot(x2.astype(BF16), wpg_ref[...])) * e
    y_ref[...] = _rmsnorm(x3, gfin_ref[...])


def _ple_final(x2_tm, ple, wple, gple, wpg, gfin, tile):
    n = ple.shape[0]
    return pl.pallas_call(
        _ple_final_kernel,
        grid=(n // tile,),
        in_specs=[
            pl.BlockSpec((tile * LANE_CHUNKS, LANES), lambda i: (i, 0)),
            pl.BlockSpec((tile, PLE_DIM), lambda i: (i, 0)),
            _full((PLE_DIM, D_MODEL)),
            _full((1, D_MODEL)),
            _full((D_MODEL, D_MODEL)),
            _full((1, D_MODEL)),
        ],
        out_specs=pl.BlockSpec((tile, D_MODEL), lambda i: (i, 0)),
        out_shape=jax.ShapeDtypeStruct((n, D_MODEL), F32),
        compiler_params=pltpu.CompilerParams(dimension_semantics=("arbitrary",), vmem_limit_bytes=VMEM_LIMIT),
        name="ple_final",
    )(x2_tm, ple, wple, gple, wpg, gfin)


def _rope_tables(pos):
    half = HEAD_DIM // 2
    inv = ROPE_THETA ** (-jnp.arange(half, dtype=F32) / half)
    ang = pos.astype(F32)[:, None] * inv[None, :]
    cos, sin = jnp.cos(ang), jnp.sin(ang)
    cos2 = jnp.concatenate([cos, cos, cos, cos], axis=1)
    sin2 = jnp.concatenate([-sin, sin, -sin, sin], axis=1)
    return cos2, sin2


def _layout_w_in(w_in):
    col = jnp.arange(IN_COLS)
    return jnp.where((col >= O_Q) & (col < O_K), w_in * (HEAD_DIM ** -0.5), w_in)


def _router_passes(w_router):
    hi = w_router.astype(BF16)
    lo = (w_router - hi.astype(F32)).astype(BF16)
    w3 = jnp.concatenate([hi, lo, hi], axis=0)
    return jnp.pad(w3, ((0, 0), (0, LANES - N_EXPERTS)))


def _prep_weights(g_mix, w_in, a_ln_g, a_ln_b, a_ws, a_bs, w_pa, w_pb, w_o, g_ffn, w_router, b_router):
    causal = jnp.tril(jnp.ones((CHUNK, CHUNK), dtype=bool))
    return dict(
        gmix=g_mix.reshape(1, D_MODEL),
        win=_layout_w_in(w_in).astype(BF16),
        lng=a_ln_g.reshape(1, A_WIDTH),
        lnb=a_ln_b.reshape(1, A_WIDTH),
        ws=jnp.where(causal[None], a_ws, 0.0).astype(BF16),
        bsf=jnp.repeat(jnp.transpose(a_bs), A_GROUP_DIM, axis=1),
        wpa=w_pa.astype(BF16),
        wpb=w_pb.astype(BF16),
        wo=w_o.astype(BF16),
        gffn=g_ffn.reshape(1, D_MODEL),
        wrt=_router_passes(w_router),
        br=b_router.reshape(N_EXPERTS, 1),
    )


def kernel(x_prompt, x_sample, cache_win_k, cache_win_v, p_prompt, p_sample, g_mix, w_in, a_ln_g, a_ln_b, a_ws, a_bs, sinks, w_pa, w_pb, w_o, g_ffn, w_router, b_router, w_gu, b_gu, w_down, b_down, w_ple, g_ple, w_ple_gate, g_final):
    W = _prep_weights(g_mix[0], w_in[0], a_ln_g[0], a_ln_b[0], a_ws[0], a_bs[0], w_pa[0], w_pb[0], w_o[0],
                      g_ffn[0], w_router[0], b_router[0])
    cos_p, sin_p = _rope_tables(jnp.arange(SEQ, dtype=I32))
    cos_s, sin_s = _rope_tables(jnp.full((1,), PAST_LEN, I32))
    x1p, h2p, logits_p, kwin_p, vwin_p, wgu16, wdn16 = _prompt_front(
        x_prompt.reshape(N_PROMPT, D_MODEL), cos_p, sin_p, W["gmix"], W["win"], W["lng"], W["lnb"],
        W["ws"], W["bsf"], sinks[0], _band_bias(), W["wpa"], W["wpb"], W["wo"], W["gffn"], W["wrt"], W["br"],
        w_gu[0].reshape(N_EXPERTS * D_MODEL, 2 * D_EXPERT), w_down[0].reshape(N_EXPERTS * D_EXPERT, D_MODEL))

    wdiag = jnp.repeat(a_ws[0, :, 0, 0], A_GROUP_DIM)[None, :].astype(BF16)
    bs0 = jnp.repeat(a_bs[0, :, 0], A_GROUP_DIM)[None, :]
    x1s, h2s, logits_s, kwin_s, vwin_s, va_s = _sample_front(
        x_sample.reshape(DEC_BATCH, D_MODEL), cos_s, sin_s, W["gmix"], W["win"], W["lng"], W["lnb"], wdiag, bs0,
        sinks[0], cache_win_k[0].reshape(DEC_BATCH, WINDOW, KV_WIDTH), cache_win_v[0].reshape(DEC_BATCH, WINDOW, KV_WIDTH),
        W["wpa"], W["wpb"], W["wo"], W["gffn"], W["wrt"], W["br"])

    dest, wts, off = _route_plan(logits_p, logits_s)
    x2p, x2s = _moe(dest, wts, off, h2p, h2s, x1p, x1s,
                    wgu16.reshape(N_EXPERTS, D_MODEL, 2 * D_EXPERT), b_gu[0].reshape(N_EXPERTS, 1, 2 * D_EXPERT),
                    wdn16.reshape(N_EXPERTS, D_EXPERT, D_MODEL), b_down[0].reshape(N_EXPERTS, 1, D_MODEL))

    wple = w_ple[0].astype(BF16)
    gple = g_ple[0].reshape(1, D_MODEL)
    wpg = w_ple_gate[0].astype(BF16)
    gfin = g_final.reshape(1, D_MODEL)
    y_p = _ple_final(x2p, p_prompt[0].reshape(N_PROMPT, PLE_DIM), wple, gple, wpg, gfin, 2 * TM)
    y_s = _ple_final(x2s, p_sample[0].reshape(DEC_BATCH, PLE_DIM), wple, gple, wpg, gfin, DEC_BATCH)

    return (
        y_p.reshape(BATCH, SEQ, D_MODEL),
        y_s.reshape(DEC_BATCH, 1, D_MODEL),
        kwin_p.reshape(1, BATCH, WINDOW, N_KV_HEADS, HEAD_DIM),
        vwin_p.reshape(1, BATCH, WINDOW, N_KV_HEADS, HEAD_DIM),
        kwin_s.reshape(1, DEC_BATCH, WINDOW, N_KV_HEADS, HEAD_DIM),
        vwin_s.reshape(1, DEC_BATCH, WINDOW, N_KV_HEADS, HEAD_DIM),
        va_s.reshape(1, DEC_BATCH, 1, A_WIDTH),
    )
```

```python
import numpy as np

import jax
import jax.numpy as jnp
from jax import lax
from jax.experimental import pallas as pl
from jax.experimental.pallas import tpu as pltpu

F32 = jnp.float32
BF16 = jnp.bfloat16
I32 = jnp.int32

D_MODEL = 1024
BATCH = 4
SEQ = 4096
DEC_BATCH = 128
PAST_LEN = 8192
CHUNK = 128
A_GROUPS = 4
A_GROUP_DIM = 128
A_WIDTH = A_GROUPS * A_GROUP_DIM
N_HEADS = 8
N_KV_HEADS = 2
HEAD_DIM = 64
Q_WIDTH = N_HEADS * HEAD_DIM
KV_WIDTH = N_KV_HEADS * HEAD_DIM
GQA_GROUP = N_HEADS // N_KV_HEADS
WINDOW = 128
ROPE_THETA = 10000.0
N_EXPERTS = 32
TOP_K = 4
D_EXPERT = D_MODEL
SWIGLU_ALPHA = 1.702
SWIGLU_LIMIT = 7.0
PLE_DIM = 256
RMS_EPS = 1e-5
LN_EPS = 1e-5

LANES = 128

QPAD_WIDTH = N_HEADS * LANES
O_Q = 2 * A_WIDTH
O_K = O_Q + Q_WIDTH
O_V = O_K + KV_WIDTH
O_GA = O_V + KV_WIDTH
O_GB = O_GA + D_MODEL
IN_COLS = O_GB + D_MODEL
SUBLANES = 8
LANE_CHUNKS = D_MODEL // LANES
VMEM_LIMIT = 56 * 1024 * 1024

N_PROMPT = BATCH * SEQ
TM = 256
TILES_PER_SEQ = SEQ // TM
BLOCKS_PER_TILE = TM // WINDOW
FRONT_STEPS = N_PROMPT // TM
CAST_ROWS = N_EXPERTS * D_MODEL // FRONT_STEPS
CAST_SPLIT = 4
IN_PROJ_DOTS = 4
BF16_ROWS = 16
CAST_CUTS = [BF16_ROWS * round(i * (CAST_ROWS // BF16_ROWS) / IN_PROJ_DOTS) for i in range(IN_PROJ_DOTS + 1)]

N_GROUPS = 4
GROUP_PROMPT = N_PROMPT // N_GROUPS
GROUP_SLOTS = GROUP_PROMPT + DEC_BATCH
GROUP_ASSIGN = GROUP_SLOTS * TOP_K
SLOT_TILES = GROUP_SLOTS // LANES
MOE_ROWS = 256
XS_STRIDE = MOE_ROWS + SUBLANES
HALF_ROWS = MOE_ROWS // 2
HALF_FLAG = 64
assert N_EXPERTS <= HALF_FLAG
SLOT_BITS = 13
K_STRIDE = 1 << SLOT_BITS
assert GROUP_SLOTS < K_STRIDE
MAX_BLOCKS = GROUP_ASSIGN // MOE_ROWS + N_EXPERTS
POS_TABLE = 1 << 15
assert (MAX_BLOCKS + 3) * MOE_ROWS <= POS_TABLE
assert MAX_BLOCKS + 2 <= LANES


def _bdot(a, b):
    return jnp.dot(a, b, preferred_element_type=F32)


def _rmsnorm(x, g):
    return x * lax.rsqrt(jnp.mean(x * x, axis=-1, keepdims=True) + RMS_EPS) * g


def _gelu(x):
    return 0.5 * x * (1.0 + lax.erf(x * (0.5 ** 0.5)))


def _group_layernorm(v, g, b):
    cols = []
    for gi in range(A_GROUPS):
        s = slice(gi * A_GROUP_DIM, (gi + 1) * A_GROUP_DIM)
        vg = v[:, s]
        mu = jnp.mean(vg, axis=-1, keepdims=True)
        d = vg - mu
        var = jnp.mean(d * d, axis=-1, keepdims=True)
        cols.append(d * lax.rsqrt(var + LN_EPS) * g[:, s] + b[:, s])
    return jnp.concatenate(cols, axis=1)


def _rope(x, cos, sin_signed):
    width = x.shape[1]
    reps = width // LANES
    cosf = jnp.concatenate([cos] * reps, axis=1) if reps > 1 else cos
    sinf = jnp.concatenate([sin_signed] * reps, axis=1) if reps > 1 else sin_signed
    half = HEAD_DIM // 2
    lane = lax.broadcasted_iota(I32, x.shape, 1)
    up = pltpu.roll(x, width - half, 1)
    down = pltpu.roll(x, half, 1)
    partner = jnp.where((lane & (HEAD_DIM - 1)) < half, up, down)
    return x * cosf + partner * sinf


def _pad_heads(q):
    lane = lax.broadcasted_iota(I32, (q.shape[0], LANES), 1)
    chunks = []
    for hq in range(N_HEADS):
        pair = q[:, (hq // 2) * LANES:(hq // 2 + 1) * LANES]
        kv_head = hq // GQA_GROUP
        if hq % 2 != kv_head:
            pair = pltpu.roll(pair, HEAD_DIM, 1)
        keep = (lane < HEAD_DIM) if kv_head == 0 else (lane >= HEAD_DIM)
        chunks.append(jnp.where(keep, pair, 0.0))
    return jnp.concatenate(chunks, axis=1)


def _in_projection(x, gmix_ref, win_ref, lng_ref, lnb_ref, cos, sin_signed, after_dot=lambda: None):
    hb = _rmsnorm(x, gmix_ref[...]).astype(BF16)

    def dot_cols(lo, hi):
        z = _bdot(hb, win_ref[:, lo:hi])
        after_dot()
        return z

    u = _gelu(dot_cols(0, A_WIDTH))
    va = _group_layernorm(_gelu(dot_cols(A_WIDTH, O_Q)), lng_ref[...], lnb_ref[...])
    zq = dot_cols(O_Q, O_K)
    zkv = dot_cols(O_K, O_GA)
    q = _pad_heads(_rope(zq, cos, sin_signed))
    k = _rope(zkv[:, :KV_WIDTH], cos, sin_signed)
    v = zkv[:, KV_WIDTH:]
    return u, va, q, k, v, hb


def _gate_preact(hb, win_ref, part, parts):
    width = 2 * D_MODEL // parts
    return _bdot(hb, win_ref[:, O_GA + part * width:O_GA + (part + 1) * width])


def _gates(preacts):
    zg = jnp.concatenate(preacts, axis=1)
    return jax.nn.sigmoid(zg[:, :D_MODEL]), jax.nn.sigmoid(zg[:, D_MODEL:])


def _merge_and_route(x, ya_in, att, gate_a, gate_b, wpa_ref, wpb_ref, wo_ref, gffn_ref, wr3_ref, br_ref,
                     x1_ref, h2_ref):
    ya = _bdot(ya_in.astype(BF16), wpa_ref[...])
    yb = _bdot(att.astype(BF16), wpb_ref[...])
    mix = (gate_a * ya + gate_b * yb).astype(BF16)
    x1 = x + _bdot(mix, wo_ref[...])
    _store_token_major(x1_ref, x1)
    h2 = _rmsnorm(x1, gffn_ref[...])
    _store_token_major(h2_ref, h2)
    hi = h2.astype(BF16)
    lo = (h2 - hi.astype(F32)).astype(BF16)
    logits = _bdot(jnp.concatenate([hi, hi, lo], axis=1), wr3_ref[...])
    return jnp.transpose(logits)[:N_EXPERTS, :] + br_ref[...]


def _top4_softmax(logits):
    eid = lax.broadcasted_iota(I32, logits.shape, 0)
    vals, idxs = [], []
    for _ in range(TOP_K):
        m = jnp.max(logits, axis=0, keepdims=True)
        idx = jnp.min(jnp.where(logits == m, eid, N_EXPERTS), axis=0, keepdims=True)
        logits = jnp.where(eid == idx, -jnp.inf, logits)
        vals.append(m)
        idxs.append(idx)
    es = [jnp.exp(v - vals[0]) for v in vals]
    inv = 1.0 / (es[0] + es[1] + es[2] + es[3])
    return jnp.concatenate(idxs, axis=0), jnp.concatenate([e * inv for e in es], axis=0)


def _store_token_major(ref, val):
    rows = val.shape[0]
    for c in range(LANE_CHUNKS):
        ref[pl.ds(c, rows, stride=LANE_CHUNKS), :] = val[:, c * LANES:(c + 1) * LANES]


def _load_token_major(ref, rows):
    return jnp.concatenate([ref[pl.ds(c, rows, stride=LANE_CHUNKS), :] for c in range(LANE_CHUNKS)], axis=1)


def _band_attention(qpad, k, v, k_prev, v_prev, sinks_ref, bias_ref, seq_start, after_scores):
    kb = jnp.concatenate([k_prev, k], axis=0).astype(BF16)
    vt = jnp.transpose(jnp.concatenate([v_prev, v], axis=0)).astype(BF16)
    qb = qpad.astype(BF16)
    lane = lax.broadcasted_iota(I32, (1, GQA_GROUP * WINDOW), 1)
    blocks = []
    for b in range(BLOCKS_PER_TILE):
        bias = bias_ref[jnp.where(seq_start, 1, 0)] if b == 0 else bias_ref[0]
        keys = kb[b * WINDOW:(b + 2) * WINDOW, :]
        pieces = []
        for h in range(N_KV_HEADS):
            qh = jnp.concatenate(
                [qb[b * WINDOW:(b + 1) * WINDOW, (h * GQA_GROUP + j) * LANES:(h * GQA_GROUP + j + 1) * LANES]
                 for j in range(GQA_GROUP)], axis=0)
            st = lax.dot_general(keys, qh, (((1,), (1,)), ((), ())), preferred_element_type=F32) + bias
            after_scores(b * N_KV_HEADS + h)
            sink = jnp.zeros((1, GQA_GROUP * WINDOW), F32)
            for j in range(GQA_GROUP):
                sink = jnp.where(lane // WINDOW == j, sinks_ref[h * GQA_GROUP + j], sink)
            m = jnp.maximum(jnp.max(st, axis=0, keepdims=True), sink)
            e = jnp.exp(st - m)
            inv = 1.0 / (jnp.sum(e, axis=0, keepdims=True) + jnp.exp(sink - m))
            ot = _bdot(vt[h * HEAD_DIM:(h + 1) * HEAD_DIM, b * WINDOW:(b + 2) * WINDOW], (e * inv).astype(BF16))
            pieces.extend(ot[:, j * WINDOW:(j + 1) * WINDOW] for j in range(GQA_GROUP))
        blocks.append(jnp.transpose(jnp.concatenate(pieces, axis=0)))
    return jnp.concatenate(blocks, axis=0)


def _band_bias():
    kj = np.arange(2 * WINDOW)[None, :, None]
    qi = (np.arange(GQA_GROUP * WINDOW) % WINDOW)[None, None, :]
    lo = (np.arange(2) * WINDOW)[:, None, None]
    valid = (kj > qi) & (kj <= qi + WINDOW) & (kj >= lo)
    return jnp.asarray(np.where(valid, 0.0, -np.inf).astype(np.float32))


def _prompt_front_kernel(x_ref, cos_ref, sin_ref, gmix_ref, win_ref, lng_ref, lnb_ref, ws_ref, bsf_ref,
                         sinks_ref, bias_ref, wpa_ref, wpb_ref, wo_ref, gffn_ref, wr3_ref, br_ref, wgu32_ref, wdn32_ref,
                         x1_ref, h2_ref, logits_ref, kwin_ref, vwin_ref, wgu16_ref, wdn16_ref,
                         kprev_ref, vprev_ref, gu_in, dn_in, gu_out, dn_out, cast_sem):
    i = pl.program_id(0)
    seq_start = (i % TILES_PER_SEQ) == 0

    def cast_rows(c, j):
        part = CAST_ROWS // CAST_SPLIT
        return pl.ds(pl.multiple_of(c * CAST_ROWS + j * part, part), part), pl.ds(j * part, part)

    def cast_in(c, slot):
        cps = []
        for j in range(CAST_SPLIT):
            hbm, loc = cast_rows(c, j)
            cps.append(pltpu.make_async_copy(wgu32_ref.at[hbm], gu_in.at[slot, loc], cast_sem.at[0, slot]))
            cps.append(pltpu.make_async_copy(wdn32_ref.at[hbm], dn_in.at[slot, loc], cast_sem.at[1, slot]))
        return cps

    def cast_out(c, slot):
        cps = []
        for j in range(CAST_SPLIT):
            hbm, loc = cast_rows(c, j)
            cps.append(pltpu.make_async_copy(gu_out.at[slot, loc], wgu16_ref.at[hbm], cast_sem.at[2, slot]))
            cps.append(pltpu.make_async_copy(dn_out.at[slot, loc], wdn16_ref.at[hbm], cast_sem.at[3, slot]))
        return cps

    slot = i & 1

    @pl.when(i == 0)
    def _():
        for cp in cast_in(0, 0):
            cp.start()

    @pl.when(i + 1 < FRONT_STEPS)
    def _():
        for cp in cast_in(i + 1, 1 - slot):
            cp.start()

    for cp in cast_in(i, slot):
        cp.wait()

    @pl.when(i >= 2)
    def _():
        for cp in cast_out(i - 2, slot):
            cp.wait()


    @pl.when(seq_start)
    def _():
        kprev_ref[...] = jnp.zeros_like(kprev_ref)
        vprev_ref[...] = jnp.zeros_like(vprev_ref)

    x = x_ref[...]
    cast_done = [0]

    def cast_slice():
        i = cast_done[0]
        r = pl.ds(CAST_CUTS[i], CAST_CUTS[i + 1] - CAST_CUTS[i])
        gu_out[slot, r, :] = gu_in[slot, r, :].astype(BF16)
        dn_out[slot, r, :] = dn_in[slot, r, :].astype(BF16)
        cast_done[0] += 1

    u, va, q, k, v, hb = _in_projection(
        x, gmix_ref, win_ref, lng_ref, lnb_ref, cos_ref[...], sin_ref[...], cast_slice)
    assert cast_done[0] == IN_PROJ_DOTS
    for cp in cast_out(i, slot):
        cp.start()

    units = BLOCKS_PER_TILE * N_KV_HEADS
    preacts = []
    att = _band_attention(q, k, v, kprev_ref[...], vprev_ref[...], sinks_ref, bias_ref, seq_start,
                          lambda unit: preacts.append(_gate_preact(hb, win_ref, unit, units)))
    gate_a, gate_b = _gates(preacts)

    k_last, v_last = k[TM - WINDOW:], v[TM - WINDOW:]
    kprev_ref[...] = k_last
    vprev_ref[...] = v_last
    kwin_ref[0] = k_last
    vwin_ref[0] = v_last

    vab = va.astype(BF16)
    zc = jnp.concatenate(
        [jnp.concatenate(
            [_bdot(ws_ref[g], vab[b * CHUNK:(b + 1) * CHUNK, g * A_GROUP_DIM:(g + 1) * A_GROUP_DIM])
             for g in range(A_GROUPS)], axis=1) + bsf_ref[...]
         for b in range(BLOCKS_PER_TILE)], axis=0)

    logits_ref[...] = _merge_and_route(x, u * zc, att, gate_a, gate_b,
                                       wpa_ref, wpb_ref, wo_ref, gffn_ref, wr3_ref, br_ref, x1_ref, h2_ref)

    @pl.when(i == FRONT_STEPS - 1)
    def _():
        for cp in cast_out(i - 1, 1 - slot) + cast_out(i, slot):
            cp.wait()


def _full(shape):
    return pl.BlockSpec(shape, lambda i: (0,) * len(shape))


def _prompt_front(x, cos, sin, gmix, win, lng, lnb, ws, bsf, sinks, bias, wpa, wpb, wo, gffn, wrt, br, wgu32, wdn32):
    n = x.shape[0]
    assert n == N_PROMPT
    grid = (FRONT_STEPS,)
    anyspec = pl.BlockSpec(memory_space=pl.ANY)
    in_specs = [
        pl.BlockSpec((TM, D_MODEL), lambda i: (i, 0)),
        pl.BlockSpec((TM, LANES), lambda i: (i % TILES_PER_SEQ, 0)),
        pl.BlockSpec((TM, LANES), lambda i: (i % TILES_PER_SEQ, 0)),
        _full((1, D_MODEL)),
        _full((D_MODEL, IN_COLS)),
        _full((1, A_WIDTH)),
        _full((1, A_WIDTH)),
        _full((A_GROUPS, CHUNK, CHUNK)),
        _full((CHUNK, A_WIDTH)),
        pl.BlockSpec(memory_space=pltpu.SMEM),
        _full((2, 2 * WINDOW, GQA_GROUP * WINDOW)),
        _full((A_WIDTH, D_MODEL)),
        _full((Q_WIDTH, D_MODEL)),
        _full((D_MODEL, D_MODEL)),
        _full((1, D_MODEL)),
        _full((3 * D_MODEL, LANES)),
        _full((N_EXPERTS, 1)),
        anyspec,
        anyspec,
    ]
    out_shape = [
        jax.ShapeDtypeStruct((n * LANE_CHUNKS, LANES), F32),
        jax.ShapeDtypeStruct((n * LANE_CHUNKS, LANES), F32),
        jax.ShapeDtypeStruct((N_EXPERTS, n), F32),
        jax.ShapeDtypeStruct((n // SEQ, WINDOW, KV_WIDTH), F32),
        jax.ShapeDtypeStruct((n // SEQ, WINDOW, KV_WIDTH), F32),
        jax.ShapeDtypeStruct(wgu32.shape, BF16),
        jax.ShapeDtypeStruct(wdn32.shape, BF16),
    ]
    out_specs = [
        pl.BlockSpec((TM * LANE_CHUNKS, LANES), lambda i: (i, 0)),
        pl.BlockSpec((TM * LANE_CHUNKS, LANES), lambda i: (i, 0)),
        pl.BlockSpec((N_EXPERTS, TM), lambda i: (0, i)),
        pl.BlockSpec((1, WINDOW, KV_WIDTH), lambda i: (i // TILES_PER_SEQ, 0, 0)),
        pl.BlockSpec((1, WINDOW, KV_WIDTH), lambda i: (i // TILES_PER_SEQ, 0, 0)),
        anyspec,
        anyspec,
    ]
    scratch = [
        pltpu.VMEM((WINDOW, KV_WIDTH), F32),
        pltpu.VMEM((WINDOW, KV_WIDTH), F32),
        pltpu.VMEM((2, CAST_ROWS, 2 * D_EXPERT), F32),
        pltpu.VMEM((2, CAST_ROWS, D_MODEL), F32),
        pltpu.VMEM((2, CAST_ROWS, 2 * D_EXPERT), BF16),
        pltpu.VMEM((2, CAST_ROWS, D_MODEL), BF16),
        pltpu.SemaphoreType.DMA((4, 2)),
    ]
    return pl.pallas_call(
        _prompt_front_kernel,
        grid=grid,
        in_specs=in_specs,
        out_specs=out_specs,
        out_shape=out_shape,
        scratch_shapes=scratch,
        compiler_params=pltpu.CompilerParams(dimension_semantics=("arbitrary",), vmem_limit_bytes=VMEM_LIMIT),
        name="prompt_front",
    )(x, cos, sin, gmix, win, lng, lnb, ws, bsf, sinks, bias, wpa, wpb, wo, gffn, wrt, br, wgu32, wdn32)


SAMPLE_STEP = 16
SAMPLE_STEPS = DEC_BATCH // SAMPLE_STEP


def _sample_kernel(x_ref, cos_ref, sin_ref, gmix_ref, win_ref, lng_ref, lnb_ref, wdiag_ref, bs0_ref, sinks_ref,
                   kc_ref, vc_ref, wpa_ref, wpb_ref, wo_ref, gffn_ref, wr3_ref, br_ref,
                   x1_ref, h2_ref, logits_ref, kwin_ref, vwin_ref, va_ref,
                   q_s, k_s, v_s, yain_s, ga_s, gb_s, att_s):
    i = pl.program_id(0)

    @pl.when(i == 0)
    def _():
        x = x_ref[...]
        cos = jnp.broadcast_to(cos_ref[...], (DEC_BATCH, LANES))
        sin = jnp.broadcast_to(sin_ref[...], (DEC_BATCH, LANES))
        u, va, q, k, v, hb = _in_projection(x, gmix_ref, win_ref, lng_ref, lnb_ref, cos, sin)
        gate_a, gate_b = _gates([_gate_preact(hb, win_ref, 0, 1)])
        va_ref[...] = va
        z = wdiag_ref[...].astype(F32) * va.astype(BF16).astype(F32) + bs0_ref[...]
        yain_s[...] = u * z
        q_s[...] = q
        k_s[...] = k
        v_s[...] = v
        ga_s[...] = gate_a
        gb_s[...] = gate_b

    r0 = pl.multiple_of(i * SAMPLE_STEP, SAMPLE_STEP)
    kwin = jnp.concatenate([kc_ref[:, 1:, :], k_s[pl.ds(r0, SAMPLE_STEP), :][:, None, :]], axis=1)
    vwin = jnp.concatenate([vc_ref[:, 1:, :], v_s[pl.ds(r0, SAMPLE_STEP), :][:, None, :]], axis=1)
    kwin_ref[...] = kwin
    vwin_ref[...] = vwin

    q16 = q_s[pl.ds(r0, SAMPLE_STEP), :]
    lane = lax.broadcasted_iota(I32, (SAMPLE_STEP, LANES), 1)
    heads = [q16[:, hq * LANES:(hq + 1) * LANES] for hq in range(N_HEADS)]
    qpad = pltpu.einshape("hbd->bhd", jnp.stack(heads, axis=0)).astype(BF16)
    s = jnp.einsum("bhd,bkd->bhk", qpad, kwin.astype(BF16), preferred_element_type=F32)
    hid = lax.broadcasted_iota(I32, (1, N_HEADS, 1), 1)
    sink = jnp.zeros((1, N_HEADS, 1), F32)
    for hq in range(N_HEADS):
        sink = jnp.where(hid == hq, sinks_ref[hq], sink)
    m = jnp.maximum(jnp.max(s, axis=-1, keepdims=True), sink)
    e = jnp.exp(s - m)
    inv = 1.0 / (jnp.sum(e, axis=-1, keepdims=True) + jnp.exp(sink - m))
    o = jnp.einsum("bhk,bkd->bhd", (e * inv).astype(BF16), vwin.astype(BF16), preferred_element_type=F32)
    o = pltpu.einshape("bhd->hbd", o)
    chunks = []
    for c in range(N_HEADS // 2):
        parts = []
        for p in range(2):
            hq = 2 * c + p
            oh = o[hq]
            if p != hq // GQA_GROUP:
                oh = pltpu.roll(oh, HEAD_DIM, 1)
            parts.append(oh)
        chunks.append(jnp.where(lane < HEAD_DIM, parts[0], parts[1]))
    att_s[pl.ds(r0, SAMPLE_STEP), :] = jnp.concatenate(chunks, axis=1)

    @pl.when(i == SAMPLE_STEPS - 1)
    def _():
        logits_ref[...] = _merge_and_route(
            x_ref[...], yain_s[...], att_s[...], ga_s[...], gb_s[...],
            wpa_ref, wpb_ref, wo_ref, gffn_ref, wr3_ref, br_ref, x1_ref, h2_ref)


def _sample_front(x, cos, sin, gmix, win, lng, lnb, wdiag, bs0, sinks, kc, vc, wpa, wpb, wo, gffn, wrt, br):
    n = DEC_BATCH
    cache_spec = pl.BlockSpec((SAMPLE_STEP, WINDOW, KV_WIDTH), lambda i: (i, 0, 0))
    in_specs = [
        _full((n, D_MODEL)),
        _full((1, LANES)),
        _full((1, LANES)),
        _full((1, D_MODEL)),
        _full((D_MODEL, IN_COLS)),
        _full((1, A_WIDTH)),
        _full((1, A_WIDTH)),
        _full((1, A_WIDTH)),
        _full((1, A_WIDTH)),
        pl.BlockSpec(memory_space=pltpu.SMEM),
        cache_spec,
        cache_spec,
        _full((A_WIDTH, D_MODEL)),
        _full((Q_WIDTH, D_MODEL)),
        _full((D_MODEL, D_MODEL)),
        _full((1, D_MODEL)),
        _full((3 * D_MODEL, LANES)),
        _full((N_EXPERTS, 1)),
    ]
    out_shape = [
        jax.ShapeDtypeStruct((n * LANE_CHUNKS, LANES), F32),
        jax.ShapeDtypeStruct((n * LANE_CHUNKS, LANES), F32),
        jax.ShapeDtypeStruct((N_EXPERTS, n), F32),
        jax.ShapeDtypeStruct((n, WINDOW, KV_WIDTH), F32),
        jax.ShapeDtypeStruct((n, WINDOW, KV_WIDTH), F32),
        jax.ShapeDtypeStruct((n, A_WIDTH), F32),
    ]
    out_specs = [
        _full((n * LANE_CHUNKS, LANES)),
        _full((n * LANE_CHUNKS, LANES)),
        _full((N_EXPERTS, n)),
        cache_spec,
        cache_spec,
        _full((n, A_WIDTH)),
    ]
    scratch = [
        pltpu.VMEM((n, QPAD_WIDTH), F32), pltpu.VMEM((n, KV_WIDTH), F32), pltpu.VMEM((n, KV_WIDTH), F32),
        pltpu.VMEM((n, A_WIDTH), F32), pltpu.VMEM((n, D_MODEL), F32), pltpu.VMEM((n, D_MODEL), F32),
        pltpu.VMEM((n, Q_WIDTH), F32),
    ]
    return pl.pallas_call(
        _sample_kernel,
        grid=(SAMPLE_STEPS,),
        in_specs=in_specs,
        out_specs=out_specs,
        out_shape=out_shape,
        scratch_shapes=scratch,
        compiler_params=pltpu.CompilerParams(dimension_semantics=("arbitrary",), vmem_limit_bytes=VMEM_LIMIT),
        name="sample_front",
    )(x, cos, sin, gmix, win, lng, lnb, wdiag, bs0, sinks, kc, vc, wpa, wpb, wo, gffn, wrt, br)


def _route_plan_kernel(lp_ref, ls_ref, dest_ref, wts_ref, off_ref):
    g = pl.program_id(0)
    topi, topw = _top4_softmax(jnp.concatenate([lp_ref[...], ls_ref[...]], axis=1))
    slot = lax.broadcasted_iota(I32, (TOP_K, GROUP_SLOTS), 1)
    eall = jnp.where(jnp.logical_or(slot < GROUP_PROMPT, g == N_GROUPS - 1), topi, N_EXPERTS)
    wts_ref[:, 0:GROUP_SLOTS] = topw
    wts_ref[:, GROUP_SLOTS:] = jnp.zeros((TOP_K, K_STRIDE - GROUP_SLOTS), F32)
    dest_ref[:, GROUP_SLOTS:] = jnp.zeros((TOP_K, K_STRIDE - GROUP_SLOTS), I32)
    eid = lax.broadcasted_iota(I32, (N_EXPERTS, GROUP_SLOTS), 0)
    onehots = [eall[k:k + 1, :] == eid for k in range(TOP_K)]
    count = jnp.zeros((N_EXPERTS, GROUP_SLOTS), F32)
    for oh in onehots:
        count = count + oh.astype(F32)
    total = jnp.broadcast_to(jnp.sum(count, axis=1, keepdims=True), (N_EXPERTS, LANES))
    padded = total + (MOE_ROWS - 1)
    nblk = jnp.floor(padded * (1.0 / MOE_ROWS))
    rem = padded - nblk * MOE_ROWS
    nblk = jnp.where(rem >= MOE_ROWS, nblk + 1.0, jnp.where(rem < 0.0, nblk - 1.0, nblk))
    r = lax.broadcasted_iota(I32, (N_EXPERTS, N_EXPERTS), 0)
    c = lax.broadcasted_iota(I32, (N_EXPERTS, N_EXPERTS), 1)
    first_blk = lax.dot_general((c < r).astype(F32), nblk, (((1,), (0,)), ((), ())),
                                precision=lax.Precision.HIGHEST, preferred_element_type=F32)
    start = (first_blk + 1.0) * MOE_ROWS
    lane = lax.broadcasted_iota(I32, (N_EXPERTS, LANES), 1)
    info = jnp.where(lane == 0, start, jnp.where(lane == 1, start + total, jnp.where(lane == 2, nblk, first_blk)))
    off_ref[...] = info.astype(I32)
    ti = lax.broadcasted_iota(I32, (LANES, LANES), 0)
    tj = lax.broadcasted_iota(I32, (LANES, LANES), 1)
    before = (ti < tj).astype(BF16)
    ones = jnp.ones((LANES, LANES), BF16)
    running = start
    for t in range(SLOT_TILES):
        sl = slice(t * LANES, (t + 1) * LANES)
        cb = count[:, sl].astype(BF16)
        pos = running + _bdot(cb, before)
        rows = [jnp.sum(jnp.where(oh[:, sl], pos, 0.0), axis=0, keepdims=True) for oh in onehots]
        dest_ref[:, sl] = jnp.concatenate(rows, axis=0).astype(I32)
        running = running + _bdot(cb, ones)


def _route_plan(logits_p, logits_s):
    in_specs = [
        pl.BlockSpec((N_EXPERTS, GROUP_PROMPT), lambda g: (0, g)),
        pl.BlockSpec((N_EXPERTS, DEC_BATCH), lambda g: (0, 0)),
    ]
    out_shape = [
        jax.ShapeDtypeStruct((N_GROUPS, TOP_K, K_STRIDE), I32),
        jax.ShapeDtypeStruct((N_GROUPS, TOP_K, K_STRIDE), F32),
        jax.ShapeDtypeStruct((N_GROUPS, N_EXPERTS, LANES), I32),
    ]
    out_specs = [
        pl.BlockSpec((None, TOP_K, K_STRIDE), lambda g: (g, 0, 0)),
        pl.BlockSpec((None, TOP_K, K_STRIDE), lambda g: (g, 0, 0)),
        pl.BlockSpec((None, N_EXPERTS, LANES), lambda g: (g, 0, 0)),
    ]
    return pl.pallas_call(
        _route_plan_kernel,
        grid=(N_GROUPS,),
        in_specs=in_specs,
        out_specs=out_specs,
        out_shape=out_shape,
        compiler_params=pltpu.CompilerParams(dimension_semantics=("arbitrary",)),
        name="route_plan",
    )(logits_p, logits_s)


GROUP_ROWS = GROUP_PROMPT * LANE_CHUNKS
SAMPLE_ROWS = DEC_BATCH * LANE_CHUNKS
TRASH_SLOT = GROUP_SLOTS
BUF_ROWS = (GROUP_SLOTS + 1) * LANE_CHUNKS
SCATTER_BATCH = 8
DMA_SPLIT = 8
TABLE_USED = GROUP_SLOTS + LANES
assert TRASH_SLOT < TABLE_USED <= K_STRIDE
FFN_COLS = 256
FFN_DOTS = (2 * D_EXPERT + D_MODEL) // FFN_COLS


def _moe_kernel(off_ref, desth_ref, wtsh_ref, h2p_ref, h2s_ref, x1p_ref, x1s_ref, wgu_ref, bgu_ref, wdn_ref, bdn_ref,
                x2p_ref, x2s_ref,
                h2buf, acc, wgubuf, bgubuf, wdnbuf, bdnbuf, xs0, xs1, ys0, ys1,
                dest_ref, wts_ref, src_ref, seg_expert, seg_first, blk_seg, act_sem, w_sem):
    g = pl.program_id(0)
    last = g == N_GROUPS - 1
    row0 = pl.multiple_of(g * GROUP_ROWS, GROUP_ROWS)

    def prompt_copies():
        cps = []
        for j in range(DMA_SPLIT):
            src = pl.ds(row0 + j * (GROUP_ROWS // DMA_SPLIT), GROUP_ROWS // DMA_SPLIT)
            dst = pl.ds(j * (GROUP_ROWS // DMA_SPLIT), GROUP_ROWS // DMA_SPLIT)
            cps.append(pltpu.make_async_copy(h2p_ref.at[src], h2buf.at[dst], act_sem.at[0]))
            cps.append(pltpu.make_async_copy(x1p_ref.at[src], acc.at[dst], act_sem.at[1]))
        return cps

    def sample_copies():
        return (pltpu.make_async_copy(h2s_ref, h2buf.at[pl.ds(GROUP_ROWS, SAMPLE_ROWS)], act_sem.at[2]),
                pltpu.make_async_copy(x1s_ref, acc.at[pl.ds(GROUP_ROWS, SAMPLE_ROWS)], act_sem.at[3]))

    def weight_copies(e, slot):
        cps = [pltpu.make_async_copy(bgu_ref.at[e], bgubuf.at[slot], w_sem.at[1, slot]),
               pltpu.make_async_copy(bdn_ref.at[e], bdnbuf.at[slot], w_sem.at[3, slot])]
        for j in range(DMA_SPLIT):
            rg = pl.ds(j * (D_MODEL // DMA_SPLIT), D_MODEL // DMA_SPLIT)
            rd = pl.ds(j * (D_EXPERT // DMA_SPLIT), D_EXPERT // DMA_SPLIT)
            cps.append(pltpu.make_async_copy(wgu_ref.at[e, rg], wgubuf.at[slot, rg], w_sem.at[0, slot]))
            cps.append(pltpu.make_async_copy(wdn_ref.at[e, rd], wdnbuf.at[slot, rd], w_sem.at[2, slot]))
        return cps

    def output_copies():
        return [pltpu.make_async_copy(
            acc.at[pl.ds(j * (GROUP_ROWS // DMA_SPLIT), GROUP_ROWS // DMA_SPLIT)],
            x2p_ref.at[pl.ds(row0 + j * (GROUP_ROWS // DMA_SPLIT), GROUP_ROWS // DMA_SPLIT)], act_sem.at[0])
            for j in range(DMA_SPLIT)]

    tab0 = pl.multiple_of(g * (TOP_K * K_STRIDE), TOP_K * K_STRIDE)
    table_copies = []
    for k in range(TOP_K):
        used = pl.ds(k * K_STRIDE, TABLE_USED)
        table_copies.append(pltpu.make_async_copy(
            desth_ref.at[pl.ds(tab0 + k * K_STRIDE, TABLE_USED)], dest_ref.at[used], act_sem.at[4]))
        table_copies.append(pltpu.make_async_copy(
            wtsh_ref.at[pl.ds(tab0 + k * K_STRIDE, TABLE_USED)], wts_ref.at[used], act_sem.at[5]))
    for cp in table_copies:
        cp.start()

    for cp in prompt_copies():
        cp.start()

    @pl.when(last)
    def _():
        for cp in sample_copies():
            cp.start()

    trash = pl.ds(TRASH_SLOT * LANE_CHUNKS, LANE_CHUNKS)
    h2buf[trash, :] = jnp.zeros((LANE_CHUNKS, LANES), F32)
    acc[trash, :] = jnp.zeros((LANE_CHUNKS, LANES), F32)
    ys0[...] = jnp.zeros_like(ys0)
    ys1[...] = jnp.zeros_like(ys1)

    def pad_block(pos0, real_rows=0):
        def body(j, carry):
            for d in range(SUBLANES):
                src_ref[pos0 + MOE_ROWS - SUBLANES - j * SUBLANES + d] = TRASH_SLOT
            return carry
        lax.fori_loop(0, (MOE_ROWS - real_rows + SUBLANES - 1) // SUBLANES, body, 0)

    def scan_expert(e, carry):
        nseg, nblocks = carry
        nblk = off_ref[e, 2]
        first = off_ref[e, 3]

        @pl.when(nblk > 0)
        def _():
            seg_expert[nseg] = e
            seg_first[nseg] = first
            tail_rows = off_ref[e, 1] - off_ref[e, 0] - (nblk - 1) * MOE_ROWS
            pad_block(off_ref[e, 0] + (nblk - 1) * MOE_ROWS, tail_rows)

            def mark(b, c2):
                blk_seg[first + b] = nseg
                return c2
            lax.fori_loop(0, nblk, mark, 0)

            blk_seg[first + nblk - 1] = nseg + jnp.where(tail_rows <= HALF_ROWS, HALF_FLAG, 0)

        return nseg + jnp.where(nblk > 0, 1, 0), nblocks + nblk

    nseg, nblocks = lax.fori_loop(0, N_EXPERTS, scan_expert, (jnp.int32(0), jnp.int32(0)))
    pad_block(0)
    pad_block((nblocks + 1) * MOE_ROWS)
    pad_block((nblocks + 2) * MOE_ROWS)
    blk_seg[nblocks] = nseg - 1 + HALF_FLAG
    blk_seg[nblocks + 1] = nseg - 1 + HALF_FLAG

    for cp in weight_copies(seg_expert[0], 0):
        cp.start()

    for cp in table_copies:
        cp.wait()

    nvalid = jnp.where(last, GROUP_SLOTS, GROUP_PROMPT)
    for k in range(TOP_K):
        def fill(j, carry, k=k):
            c0 = k * K_STRIDE + j * SUBLANES
            for d in range(SUBLANES):
                src_ref[dest_ref[c0 + d]] = c0 + d
            return carry
        lax.fori_loop(0, nvalid // SUBLANES, fill, 0)

    for cp in prompt_copies():
        cp.wait()

    @pl.when(last)
    def _():
        for cp in sample_copies():
            cp.wait()

    def token_rows(code):
        slot_id = code & (K_STRIDE - 1)
        return pl.ds(pl.multiple_of(slot_id * LANE_CHUNKS, LANE_CHUNKS), LANE_CHUNKS)

    def gather(b, xs, lo=0, hi=MOE_ROWS):
        base = (b + 1) * MOE_ROWS
        for m in range(lo, hi):
            xs[pl.ds(m, LANE_CHUNKS, stride=XS_STRIDE), :] = h2buf[token_rows(src_ref[base + m]), :]

    def scatter_add(b, ys, lo=0, hi=MOE_ROWS):
        base = (b + 1) * MOE_ROWS
        for m0 in range(lo, hi, SCATTER_BATCH):
            pending = []
            for m in range(m0, m0 + SCATTER_BATCH):
                code = src_ref[base + m]
                rows = token_rows(code)
                pending.append((rows, acc[rows, :] + wts_ref[code] * ys[pl.ds(m, LANE_CHUNKS, stride=XS_STRIDE), :]))
            for rows, val in pending:
                acc[rows, :] = val

    gather_cuts = [round(i * MOE_ROWS / FFN_DOTS) for i in range(FFN_DOTS + 1)]
    scatter_cuts = [SCATTER_BATCH * round(i * (MOE_ROWS // SCATTER_BATCH) / FFN_DOTS) for i in range(FFN_DOTS + 1)]

    def step(b, xs_cur, xs_next, ys_cur, ys_prev):
        tag = blk_seg[b]
        seg = tag & (HALF_FLAG - 1)
        half = tag >= HALF_FLAG
        slot = seg & 1

        @pl.when(jnp.logical_and(b == seg_first[seg], b < nblocks))
        def _():
            for cp in weight_copies(seg_expert[seg], slot):
                cp.wait()

            @pl.when(seg + 1 < nseg)
            def _():
                for cp in weight_copies(seg_expert[seg + 1], 1 - slot):
                    cp.start()

        def work(rows):
            done = [0]

            def row_traffic():
                i = done[0]
                gather(b + 1, xs_next, gather_cuts[i], gather_cuts[i + 1])
                scatter_add(b - 1, ys_prev, scatter_cuts[i], scatter_cuts[i + 1])
                done[0] = i + 1

            x = jnp.concatenate(
                [xs_cur[c * XS_STRIDE:c * XS_STRIDE + rows, :] for c in range(LANE_CHUNKS)], axis=1).astype(BF16)
            acts = []
            for c in range(D_EXPERT // FFN_COLS):
                gc = pl.ds(c * FFN_COLS, FFN_COLS)
                uc = pl.ds(D_EXPERT + c * FFN_COLS, FFN_COLS)
                g = _bdot(x, wgubuf[slot, :, gc]) + bgubuf[slot, :, gc]
                row_traffic()
                u = _bdot(x, wgubuf[slot, :, uc]) + bgubuf[slot, :, uc]
                row_traffic()
                gl = jnp.minimum(g, SWIGLU_LIMIT)
                ul = jnp.clip(u, -SWIGLU_LIMIT, SWIGLU_LIMIT)
                acts.append((ul + 1.0) * (gl * jax.nn.sigmoid(SWIGLU_ALPHA * gl)))
            a = jnp.concatenate(acts, axis=1).astype(BF16)
            for c in range(D_MODEL // FFN_COLS):
                oc = pl.ds(c * FFN_COLS, FFN_COLS)
                y = _bdot(a, wdnbuf[slot, :, oc]) + bdnbuf[slot, :, oc]
                for j in range(FFN_COLS // LANES):
                    lc = c * (FFN_COLS // LANES) + j
                    ys_cur[lc * XS_STRIDE:lc * XS_STRIDE + rows, :] = y[:, j * LANES:(j + 1) * LANES]
                row_traffic()

        lax.cond(half, lambda: work(HALF_ROWS), lambda: work(MOE_ROWS))

    gather(0, xs0)
    npairs = (nblocks + 1) // 2

    def pair(t, carry):
        step(2 * t, xs0, xs1, ys0, ys1)
        step(2 * t + 1, xs1, xs0, ys1, ys0)
        return carry

    lax.fori_loop(0, npairs, pair, 0)
    scatter_add(2 * npairs - 1, ys1)

    for cp in output_copies():
        cp.start()

    @pl.when(last)
    def _():
        out_s = pltpu.make_async_copy(acc.at[pl.ds(GROUP_ROWS, SAMPLE_ROWS)], x2s_ref, act_sem.at[2])
        out_s.start()
        out_s.wait()

    for cp in output_copies():
        cp.wait()


def _moe(dest, wts, off, h2p, h2s, x1p, x1s, wgu, bgu, wdn, bdn):
    anyspec = pl.BlockSpec(memory_space=pl.ANY)
    dest = dest.reshape(N_GROUPS * TOP_K * K_STRIDE)
    wts = wts.reshape(N_GROUPS * TOP_K * K_STRIDE)
    in_specs = [
        pl.BlockSpec((None, N_EXPERTS, LANES), lambda g: (g, 0, 0), memory_space=pltpu.SMEM),
        anyspec, anyspec, anyspec, anyspec, anyspec, anyspec, anyspec, anyspec, anyspec, anyspec,
    ]
    scratch = [
        pltpu.VMEM((BUF_ROWS, LANES), F32),
        pltpu.VMEM((BUF_ROWS, LANES), F32),
        pltpu.VMEM((2, D_MODEL, 2 * D_EXPERT), BF16),
        pltpu.VMEM((2, 1, 2 * D_EXPERT), F32),
        pltpu.VMEM((2, D_EXPERT, D_MODEL), BF16),
        pltpu.VMEM((2, 1, D_MODEL), F32),
        pltpu.VMEM((LANE_CHUNKS * XS_STRIDE, LANES), F32),
        pltpu.VMEM((LANE_CHUNKS * XS_STRIDE, LANES), F32),
        pltpu.VMEM((LANE_CHUNKS * XS_STRIDE, LANES), F32),
        pltpu.VMEM((LANE_CHUNKS * XS_STRIDE, LANES), F32),
        pltpu.SMEM((TOP_K * K_STRIDE,), I32),
        pltpu.SMEM((TOP_K * K_STRIDE,), F32),
        pltpu.SMEM((POS_TABLE,), I32),
        pltpu.SMEM((N_EXPERTS,), I32),
        pltpu.SMEM((N_EXPERTS,), I32),
        pltpu.SMEM((LANES,), I32),
        pltpu.SemaphoreType.DMA((6,)),
        pltpu.SemaphoreType.DMA((4, 2)),
    ]
    return pl.pallas_call(
        _moe_kernel,
        grid=(N_GROUPS,),
        in_specs=in_specs,
        out_specs=[anyspec, anyspec],
        out_shape=[jax.ShapeDtypeStruct(x1p.shape, F32), jax.ShapeDtypeStruct(x1s.shape, F32)],
        scratch_shapes=scratch,
        compiler_params=pltpu.CompilerParams(dimension_semantics=("arbitrary",), vmem_limit_bytes=VMEM_LIMIT),
        name="moe",
    )(off, dest, wts, h2p, h2s, x1p, x1s, wgu, bgu, wdn, bdn)


def _ple_final_kernel(x2_ref, ple_ref, wple_ref, gple_ref, wpg_ref, gfin_ref, y_ref):
    rows = y_ref.shape[0]
    x2 = _load_token_major(x2_ref, rows)
    e = _rmsnorm(_bdot(ple_ref[...].astype(BF16), wple_ref[...]), gple_ref[...])
    x3 = x2 + jax.nn.sigmoid(_bdot(x2.astype(BF16), wpg_ref[...])) * e
    y_ref[...] = _rmsnorm(x3, gfin_ref[...])


def _ple_final(x2_tm, ple, wple, gple, wpg, gfin, tile):
    n = ple.shape[0]
    return pl.pallas_call(
        _ple_final_kernel,
        grid=(n // tile,),
        in_specs=[
            pl.BlockSpec((tile * LANE_CHUNKS, LANES), lambda i: (i, 0)),
            pl.BlockSpec((tile, PLE_DIM), lambda i: (i, 0)),
            _full((PLE_DIM, D_MODEL)),
            _full((1, D_MODEL)),
            _full((D_MODEL, D_MODEL)),
            _full((1, D_MODEL)),
        ],
        out_specs=pl.BlockSpec((tile, D_MODEL), lambda i: (i, 0)),
        out_shape=jax.ShapeDtypeStruct((n, D_MODEL), F32),
        compiler_params=pltpu.CompilerParams(dimension_semantics=("arbitrary",), vmem_limit_bytes=VMEM_LIMIT),
        name="ple_final",
    )(x2_tm, ple, wple, gple, wpg, gfin)


def _rope_tables(pos):
    half = HEAD_DIM // 2
    inv = ROPE_THETA ** (-jnp.arange(half, dtype=F32) / half)
    ang = pos.astype(F32)[:, None] * inv[None, :]
    cos, sin = jnp.cos(ang), jnp.sin(ang)
    cos2 = jnp.concatenate([cos, cos, cos, cos], axis=1)
    sin2 = jnp.concatenate([-sin, sin, -sin, sin], axis=1)
    return cos2, sin2


def _layout_w_in(w_in):
    col = jnp.arange(IN_COLS)
    return jnp.where((col >= O_Q) & (col < O_K), w_in * (HEAD_DIM ** -0.5), w_in)


def _router_passes(w_router):
    hi = w_router.astype(BF16)
    lo = (w_router - hi.astype(F32)).astype(BF16)
    w3 = jnp.concatenate([hi, lo, hi], axis=0)
    return jnp.pad(w3, ((0, 0), (0, LANES - N_EXPERTS)))


def _prep_weights(g_mix, w_in, a_ln_g, a_ln_b, a_ws, a_bs, w_pa, w_pb, w_o, g_ffn, w_router, b_router):
    causal = jnp.tril(jnp.ones((CHUNK, CHUNK), dtype=bool))
    return dict(
        gmix=g_mix.reshape(1, D_MODEL),
        win=_layout_w_in(w_in).astype(BF16),
        lng=a_ln_g.reshape(1, A_WIDTH),
        lnb=a_ln_b.reshape(1, A_WIDTH),
        ws=jnp.where(causal[None], a_ws, 0.0).astype(BF16),
        bsf=jnp.repeat(jnp.transpose(a_bs), A_GROUP_DIM, axis=1),
        wpa=w_pa.astype(BF16),
        wpb=w_pb.astype(BF16),
        wo=w_o.astype(BF16),
        gffn=g_ffn.reshape(1, D_MODEL),
        wrt=_router_passes(w_router),
        br=b_router.reshape(N_EXPERTS, 1),
    )


def kernel(x_prompt, x_sample, cache_win_k, cache_win_v, p_prompt, p_sample, g_mix, w_in, a_ln_g, a_ln_b, a_ws, a_bs, sinks, w_pa, w_pb, w_o, g_ffn, w_router, b_router, w_gu, b_gu, w_down, b_down, w_ple, g_ple, w_ple_gate, g_final):
    W = _prep_weights(g_mix[0], w_in[0], a_ln_g[0], a_ln_b[0], a_ws[0], a_bs[0], w_pa[0], w_pb[0], w_o[0],
                      g_ffn[0], w_router[0], b_router[0])
    cos_p, sin_p = _rope_tables(jnp.arange(SEQ, dtype=I32))
    cos_s, sin_s = _rope_tables(jnp.full((1,), PAST_LEN, I32))
    x1p, h2p, logits_p, kwin_p, vwin_p, wgu16, wdn16 = _prompt_front(
        x_prompt.reshape(N_PROMPT, D_MODEL), cos_p, sin_p, W["gmix"], W["win"], W["lng"], W["lnb"],
        W["ws"], W["bsf"], sinks[0], _band_bias(), W["wpa"], W["wpb"], W["wo"], W["gffn"], W["wrt"], W["br"],
        w_gu[0].reshape(N_EXPERTS * D_MODEL, 2 * D_EXPERT), w_down[0].reshape(N_EXPERTS * D_EXPERT, D_MODEL))

    wdiag = jnp.repeat(a_ws[0, :, 0, 0], A_GROUP_DIM)[None, :].astype(BF16)
    bs0 = jnp.repeat(a_bs[0, :, 0], A_GROUP_DIM)[None, :]
    x1s, h2s, logits_s, kwin_s, vwin_s, va_s = _sample_front(
        x_sample.reshape(DEC_BATCH, D_MODEL), cos_s, sin_s, W["gmix"], W["win"], W["lng"], W["lnb"], wdiag, bs0,
        sinks[0], cache_win_k[0].reshape(DEC_BATCH, WINDOW, KV_WIDTH), cache_win_v[0].reshape(DEC_BATCH, WINDOW, KV_WIDTH),
        W["wpa"], W["wpb"], W["wo"], W["gffn"], W["wrt"], W["br"])

    dest, wts, off = _route_plan(logits_p, logits_s)
    x2p, x2s = _moe(dest, wts, off, h2p, h2s, x1p, x1s,
                    wgu16.reshape(N_EXPERTS, D_MODEL, 2 * D_EXPERT), b_gu[0].reshape(N_EXPERTS, 1, 2 * D_EXPERT),
                    wdn16.reshape(N_EXPERTS, D_EXPERT, D_MODEL), b_down[0].reshape(N_EXPERTS, 1, D_MODEL))

    wple = w_ple[0].astype(BF16)
    gple = g_ple[0].reshape(1, D_MODEL)
    wpg = w_ple_gate[0].astype(BF16)
    gfin = g_final.reshape(1, D_MODEL)
    y_p = _ple_final(x2p, p_prompt[0].reshape(N_PROMPT, PLE_DIM), wple, gple, wpg, gfin, 4 * TM)
    y_s = _ple_final(x2s, p_sample[0].reshape(DEC_BATCH, PLE_DIM), wple, gple, wpg, gfin, DEC_BATCH)

    return (
        y_p.reshape(BATCH, SEQ, D_MODEL),
        y_s.reshape(DEC_BATCH, 1, D_MODEL),
        kwin_p.reshape(1, BATCH, WINDOW, N_KV_HEADS, HEAD_DIM),
        vwin_p.reshape(1, BATCH, WINDOW, N_KV_HEADS, HEAD_DIM),
        kwin_s.reshape(1, DEC_BATCH, WINDOW, N_KV_HEADS, HEAD_DIM),
        vwin_s.reshape(1, DEC_BATCH, WINDOW, N_KV_HEADS, HEAD_DIM),
        va_s.reshape(1, DEC_BATCH, 1, A_WIDTH),
    )
```

```python
import numpy as np

import jax
import jax.numpy as jnp
from jax import lax
from jax.experimental import pallas as pl
from jax.experimental.pallas import tpu as pltpu

F32 = jnp.float32
BF16 = jnp.bfloat16
I32 = jnp.int32

D_MODEL = 1024
BATCH = 4
SEQ = 4096
DEC_BATCH = 128
PAST_LEN = 8192
CHUNK = 128
A_GROUPS = 4
A_GROUP_DIM = 128
A_WIDTH = A_GROUPS * A_GROUP_DIM
N_HEADS = 8
N_KV_HEADS = 2
HEAD_DIM = 64
Q_WIDTH = N_HEADS * HEAD_DIM
KV_WIDTH = N_KV_HEADS * HEAD_DIM
GQA_GROUP = N_HEADS // N_KV_HEADS
WINDOW = 128
ROPE_THETA = 10000.0
N_EXPERTS = 32
TOP_K = 4
D_EXPERT = D_MODEL
SWIGLU_ALPHA = 1.702
SWIGLU_LIMIT = 7.0
PLE_DIM = 256
RMS_EPS = 1e-5
LN_EPS = 1e-5

LANES = 128

QPAD_WIDTH = N_HEADS * LANES
O_Q = 2 * A_WIDTH
O_K = O_Q + Q_WIDTH
O_V = O_K + KV_WIDTH
O_GA = O_V + KV_WIDTH
O_GB = O_GA + D_MODEL
IN_COLS = O_GB + D_MODEL
SUBLANES = 8
LANE_CHUNKS = D_MODEL // LANES
VMEM_LIMIT = 56 * 1024 * 1024

N_PROMPT = BATCH * SEQ
TM = 256
TILES_PER_SEQ = SEQ // TM
BLOCKS_PER_TILE = TM // WINDOW
FRONT_STEPS = N_PROMPT // TM
CAST_ROWS = N_EXPERTS * D_MODEL // FRONT_STEPS
CAST_SPLIT = 4
IN_PROJ_DOTS = 4
BF16_ROWS = 16
CAST_CUTS = [BF16_ROWS * round(i * (CAST_ROWS // BF16_ROWS) / IN_PROJ_DOTS) for i in range(IN_PROJ_DOTS + 1)]

N_GROUPS = 4
GROUP_PROMPT = N_PROMPT // N_GROUPS
GROUP_SLOTS = GROUP_PROMPT + DEC_BATCH
GROUP_ASSIGN = GROUP_SLOTS * TOP_K
SLOT_TILES = GROUP_SLOTS // LANES
MOE_ROWS = 256
XS_STRIDE = MOE_ROWS + SUBLANES
HALF_ROWS = MOE_ROWS // 2
HALF_FLAG = 64
assert N_EXPERTS <= HALF_FLAG
SLOT_BITS = 13
K_STRIDE = 1 << SLOT_BITS
assert GROUP_SLOTS < K_STRIDE
MAX_BLOCKS = GROUP_ASSIGN // MOE_ROWS + N_EXPERTS
POS_TABLE = 1 << 15
assert (MAX_BLOCKS + 3) * MOE_ROWS <= POS_TABLE
assert MAX_BLOCKS + 2 <= LANES


def _bdot(a, b):
    return jnp.dot(a, b, preferred_element_type=F32)


def _rmsnorm(x, g):
    return x * lax.rsqrt(jnp.mean(x * x, axis=-1, keepdims=True) + RMS_EPS) * g


def _gelu(x):
    return 0.5 * x * (1.0 + lax.erf(x * (0.5 ** 0.5)))


def _group_layernorm(v, g, b):
    cols = []
    for gi in range(A_GROUPS):
        s = slice(gi * A_GROUP_DIM, (gi + 1) * A_GROUP_DIM)
        vg = v[:, s]
        mu = jnp.mean(vg, axis=-1, keepdims=True)
        d = vg - mu
        var = jnp.mean(d * d, axis=-1, keepdims=True)
        cols.append(d * lax.rsqrt(var + LN_EPS) * g[:, s] + b[:, s])
    return jnp.concatenate(cols, axis=1)


def _rope(x, cos, sin_signed):
    width = x.shape[1]
    reps = width // LANES
    cosf = jnp.concatenate([cos] * reps, axis=1) if reps > 1 else cos
    sinf = jnp.concatenate([sin_signed] * reps, axis=1) if reps > 1 else sin_signed
    half = HEAD_DIM // 2
    lane = lax.broadcasted_iota(I32, x.shape, 1)
    up = pltpu.roll(x, width - half, 1)
    down = pltpu.roll(x, half, 1)
    partner = jnp.where((lane & (HEAD_DIM - 1)) < half, up, down)
    return x * cosf + partner * sinf


def _pad_heads(q):
    lane = lax.broadcasted_iota(I32, (q.shape[0], LANES), 1)
    chunks = []
    for hq in range(N_HEADS):
        pair = q[:, (hq // 2) * LANES:(hq // 2 + 1) * LANES]
        kv_head = hq // GQA_GROUP
        if hq % 2 != kv_head:
            pair = pltpu.roll(pair, HEAD_DIM, 1)
        keep = (lane < HEAD_DIM) if kv_head == 0 else (lane >= HEAD_DIM)
        chunks.append(jnp.where(keep, pair, 0.0))
    return jnp.concatenate(chunks, axis=1)


def _in_projection(x, gmix_ref, win_ref, lng_ref, lnb_ref, cos, sin_signed, after_dot=lambda: None):
    hb = _rmsnorm(x, gmix_ref[...]).astype(BF16)

    def dot_cols(lo, hi):
        z = _bdot(hb, win_ref[:, lo:hi])
        after_dot()
        return z

    u = _gelu(dot_cols(0, A_WIDTH))
    va = _group_layernorm(_gelu(dot_cols(A_WIDTH, O_Q)), lng_ref[...], lnb_ref[...])
    zq = dot_cols(O_Q, O_K)
    zkv = dot_cols(O_K, O_GA)
    q = _pad_heads(_rope(zq, cos, sin_signed))
    k = _rope(zkv[:, :KV_WIDTH], cos, sin_signed)
    v = zkv[:, KV_WIDTH:]
    return u, va, q, k, v, hb


def _gate_preact(hb, win_ref, part, parts):
    width = 2 * D_MODEL // parts
    return _bdot(hb, win_ref[:, O_GA + part * width:O_GA + (part + 1) * width])


def _gates(preacts):
    zg = jnp.concatenate(preacts, axis=1)
    return jax.nn.sigmoid(zg[:, :D_MODEL]), jax.nn.sigmoid(zg[:, D_MODEL:])


def _merge_and_route(x, ya_in, att, gate_a, gate_b, wpa_ref, wpb_ref, wo_ref, gffn_ref, wr3_ref, br_ref,
                     x1_ref, h2_ref):
    ya = _bdot(ya_in.astype(BF16), wpa_ref[...])
    yb = _bdot(att.astype(BF16), wpb_ref[...])
    mix = (gate_a * ya + gate_b * yb).astype(BF16)
    x1 = x + _bdot(mix, wo_ref[...])
    _store_token_major(x1_ref, x1)
    h2 = _rmsnorm(x1, gffn_ref[...])
    _store_token_major(h2_ref, h2)
    hi = h2.astype(BF16)
    lo = (h2 - hi.astype(F32)).astype(BF16)
    logits = _bdot(jnp.concatenate([hi, hi, lo], axis=1), wr3_ref[...])
    return jnp.transpose(logits)[:N_EXPERTS, :] + br_ref[...]


def _top4_softmax(logits):
    eid = lax.broadcasted_iota(I32, logits.shape, 0)
    vals, idxs = [], []
    for _ in range(TOP_K):
        m = jnp.max(logits, axis=0, keepdims=True)
        idx = jnp.min(jnp.where(logits == m, eid, N_EXPERTS), axis=0, keepdims=True)
        logits = jnp.where(eid == idx, -jnp.inf, logits)
        vals.append(m)
        idxs.append(idx)
    es = [jnp.exp(v - vals[0]) for v in vals]
    inv = 1.0 / (es[0] + es[1] + es[2] + es[3])
    return jnp.concatenate(idxs, axis=0), jnp.concatenate([e * inv for e in es], axis=0)


def _store_token_major(ref, val):
    rows = val.shape[0]
    for c in range(LANE_CHUNKS):
        ref[pl.ds(c, rows, stride=LANE_CHUNKS), :] = val[:, c * LANES:(c + 1) * LANES]


def _load_token_major(ref, rows):
    return jnp.concatenate([ref[pl.ds(c, rows, stride=LANE_CHUNKS), :] for c in range(LANE_CHUNKS)], axis=1)


def _band_attention(qpad, k, v, k_prev, v_prev, sinks_ref, bias_ref, seq_start, after_scores):
    kb = jnp.concatenate([k_prev, k], axis=0).astype(BF16)
    vt = jnp.transpose(jnp.concatenate([v_prev, v], axis=0)).astype(BF16)
    qb = qpad.astype(BF16)
    lane = lax.broadcasted_iota(I32, (1, GQA_GROUP * WINDOW), 1)
    blocks = []
    for b in range(BLOCKS_PER_TILE):
        bias = bias_ref[jnp.where(seq_start, 1, 0)] if b == 0 else bias_ref[0]
        keys = kb[b * WINDOW:(b + 2) * WINDOW, :]
        pieces = []
        for h in range(N_KV_HEADS):
            qh = jnp.concatenate(
                [qb[b * WINDOW:(b + 1) * WINDOW, (h * GQA_GROUP + j) * LANES:(h * GQA_GROUP + j + 1) * LANES]
                 for j in range(GQA_GROUP)], axis=0)
            st = lax.dot_general(keys, qh, (((1,), (1,)), ((), ())), preferred_element_type=F32) + bias
            after_scores(b * N_KV_HEADS + h)
            sink = jnp.zeros((1, GQA_GROUP * WINDOW), F32)
            for j in range(GQA_GROUP):
                sink = jnp.where(lane // WINDOW == j, sinks_ref[h * GQA_GROUP + j], sink)
            m = jnp.maximum(jnp.max(st, axis=0, keepdims=True), sink)
            e = jnp.exp(st - m)
            inv = 1.0 / (jnp.sum(e, axis=0, keepdims=True) + jnp.exp(sink - m))
            ot = _bdot(vt[h * HEAD_DIM:(h + 1) * HEAD_DIM, b * WINDOW:(b + 2) * WINDOW], (e * inv).astype(BF16))
            pieces.extend(ot[:, j * WINDOW:(j + 1) * WINDOW] for j in range(GQA_GROUP))
        blocks.append(jnp.transpose(jnp.concatenate(pieces, axis=0)))
    return jnp.concatenate(blocks, axis=0)


def _band_bias():
    kj = np.arange(2 * WINDOW)[None, :, None]
    qi = (np.arange(GQA_GROUP * WINDOW) % WINDOW)[None, None, :]
    lo = (np.arange(2) * WINDOW)[:, None, None]
    valid = (kj > qi) & (kj <= qi + WINDOW) & (kj >= lo)
    return jnp.asarray(np.where(valid, 0.0, -np.inf).astype(np.float32))


def _prompt_front_kernel(x_ref, cos_ref, sin_ref, gmix_ref, win_ref, lng_ref, lnb_ref, ws_ref, bsf_ref,
                         sinks_ref, bias_ref, wpa_ref, wpb_ref, wo_ref, gffn_ref, wr3_ref, br_ref, wgu32_ref, wdn32_ref,
                         x1_ref, h2_ref, logits_ref, kwin_ref, vwin_ref, wgu16_ref, wdn16_ref,
                         kprev_ref, vprev_ref, gu_in, dn_in, gu_out, dn_out, cast_sem):
    i = pl.program_id(0)
    seq_start = (i % TILES_PER_SEQ) == 0

    def cast_rows(c, j):
        part = CAST_ROWS // CAST_SPLIT
        return pl.ds(pl.multiple_of(c * CAST_ROWS + j * part, part), part), pl.ds(j * part, part)

    def cast_in(c, slot):
        cps = []
        for j in range(CAST_SPLIT):
            hbm, loc = cast_rows(c, j)
            cps.append(pltpu.make_async_copy(wgu32_ref.at[hbm], gu_in.at[slot, loc], cast_sem.at[0, slot]))
            cps.append(pltpu.make_async_copy(wdn32_ref.at[hbm], dn_in.at[slot, loc], cast_sem.at[1, slot]))
        return cps

    def cast_out(c, slot):
        cps = []
        for j in range(CAST_SPLIT):
            hbm, loc = cast_rows(c, j)
            cps.append(pltpu.make_async_copy(gu_out.at[slot, loc], wgu16_ref.at[hbm], cast_sem.at[2, slot]))
            cps.append(pltpu.make_async_copy(dn_out.at[slot, loc], wdn16_ref.at[hbm], cast_sem.at[3, slot]))
        return cps

    slot = i & 1

    @pl.when(i == 0)
    def _():
        for cp in cast_in(0, 0):
            cp.start()

    @pl.when(i + 1 < FRONT_STEPS)
    def _():
        for cp in cast_in(i + 1, 1 - slot):
            cp.start()

    for cp in cast_in(i, slot):
        cp.wait()

    @pl.when(i >= 2)
    def _():
        for cp in cast_out(i - 2, slot):
            cp.wait()


    @pl.when(seq_start)
    def _():
        kprev_ref[...] = jnp.zeros_like(kprev_ref)
        vprev_ref[...] = jnp.zeros_like(vprev_ref)

    x = x_ref[...]
    cast_done = [0]

    def cast_slice():
        i = cast_done[0]
        r = pl.ds(CAST_CUTS[i], CAST_CUTS[i + 1] - CAST_CUTS[i])
        gu_out[slot, r, :] = gu_in[slot, r, :].astype(BF16)
        dn_out[slot, r, :] = dn_in[slot, r, :].astype(BF16)
        cast_done[0] += 1

    u, va, q, k, v, hb = _in_projection(
        x, gmix_ref, win_ref, lng_ref, lnb_ref, cos_ref[...], sin_ref[...], cast_slice)
    assert cast_done[0] == IN_PROJ_DOTS
    for cp in cast_out(i, slot):
        cp.start()

    units = BLOCKS_PER_TILE * N_KV_HEADS
    preacts = []
    att = _band_attention(q, k, v, kprev_ref[...], vprev_ref[...], sinks_ref, bias_ref, seq_start,
                          lambda unit: preacts.append(_gate_preact(hb, win_ref, unit, units)))
    gate_a, gate_b = _gates(preacts)

    k_last, v_last = k[TM - WINDOW:], v[TM - WINDOW:]
    kprev_ref[...] = k_last
    vprev_ref[...] = v_last
    kwin_ref[0] = k_last
    vwin_ref[0] = v_last

    vab = va.astype(BF16)
    zc = jnp.concatenate(
        [jnp.concatenate(
            [_bdot(ws_ref[g], vab[b * CHUNK:(b + 1) * CHUNK, g * A_GROUP_DIM:(g + 1) * A_GROUP_DIM])
             for g in range(A_GROUPS)], axis=1) + bsf_ref[...]
         for b in range(BLOCKS_PER_TILE)], axis=0)

    logits_ref[...] = _merge_and_route(x, u * zc, att, gate_a, gate_b,
                                       wpa_ref, wpb_ref, wo_ref, gffn_ref, wr3_ref, br_ref, x1_ref, h2_ref)

    @pl.when(i == FRONT_STEPS - 1)
    def _():
        for cp in cast_out(i - 1, 1 - slot) + cast_out(i, slot):
            cp.wait()


def _full(shape):
    return pl.BlockSpec(shape, lambda i: (0,) * len(shape))


def _prompt_front(x, cos, sin, gmix, win, lng, lnb, ws, bsf, sinks, bias, wpa, wpb, wo, gffn, wrt, br, wgu32, wdn32):
    n = x.shape[0]
    assert n == N_PROMPT
    grid = (FRONT_STEPS,)
    anyspec = pl.BlockSpec(memory_space=pl.ANY)
    in_specs = [
        pl.BlockSpec((TM, D_MODEL), lambda i: (i, 0)),
        pl.BlockSpec((TM, LANES), lambda i: (i % TILES_PER_SEQ, 0)),
        pl.BlockSpec((TM, LANES), lambda i: (i % TILES_PER_SEQ, 0)),
        _full((1, D_MODEL)),
        _full((D_MODEL, IN_COLS)),
        _full((1, A_WIDTH)),
        _full((1, A_WIDTH)),
        _full((A_GROUPS, CHUNK, CHUNK)),
        _full((CHUNK, A_WIDTH)),
        pl.BlockSpec(memory_space=pltpu.SMEM),
        _full((2, 2 * WINDOW, GQA_GROUP * WINDOW)),
        _full((A_WIDTH, D_MODEL)),
        _full((Q_WIDTH, D_MODEL)),
        _full((D_MODEL, D_MODEL)),
        _full((1, D_MODEL)),
        _full((3 * D_MODEL, LANES)),
        _full((N_EXPERTS, 1)),
        anyspec,
        anyspec,
    ]
    out_shape = [
        jax.ShapeDtypeStruct((n * LANE_CHUNKS, LANES), F32),
        jax.ShapeDtypeStruct((n * LANE_CHUNKS, LANES), F32),
        jax.ShapeDtypeStruct((N_EXPERTS, n), F32),
        jax.ShapeDtypeStruct((n // SEQ, WINDOW, KV_WIDTH), F32),
        jax.ShapeDtypeStruct((n // SEQ, WINDOW, KV_WIDTH), F32),
        jax.ShapeDtypeStruct(wgu32.shape, BF16),
        jax.ShapeDtypeStruct(wdn32.shape, BF16),
    ]
    out_specs = [
        pl.BlockSpec((TM * LANE_CHUNKS, LANES), lambda i: (i, 0)),
        pl.BlockSpec((TM * LANE_CHUNKS, LANES), lambda i: (i, 0)),
        pl.BlockSpec((N_EXPERTS, TM), lambda i: (0, i)),
        pl.BlockSpec((1, WINDOW, KV_WIDTH), lambda i: (i // TILES_PER_SEQ, 0, 0)),
        pl.BlockSpec((1, WINDOW, KV_WIDTH), lambda i: (i // TILES_PER_SEQ, 0, 0)),
        anyspec,
        anyspec,
    ]
    scratch = [
        pltpu.VMEM((WINDOW, KV_WIDTH), F32),
        pltpu.VMEM((WINDOW, KV_WIDTH), F32),
        pltpu.VMEM((2, CAST_ROWS, 2 * D_EXPERT), F32),
        pltpu.VMEM((2, CAST_ROWS, D_MODEL), F32),
        pltpu.VMEM((2, CAST_ROWS, 2 * D_EXPERT), BF16),
        pltpu.VMEM((2, CAST_ROWS, D_MODEL), BF16),
        pltpu.SemaphoreType.DMA((4, 2)),
    ]
    return pl.pallas_call(
        _prompt_front_kernel,
        grid=grid,
        in_specs=in_specs,
        out_specs=out_specs,
        out_shape=out_shape,
        scratch_shapes=scratch,
        compiler_params=pltpu.CompilerParams(dimension_semantics=("arbitrary",), vmem_limit_bytes=VMEM_LIMIT),
        name="prompt_front",
    )(x, cos, sin, gmix, win, lng, lnb, ws, bsf, sinks, bias, wpa, wpb, wo, gffn, wrt, br, wgu32, wdn32)


SAMPLE_STEP = 16
SAMPLE_STEPS = DEC_BATCH // SAMPLE_STEP


def _sample_kernel(x_ref, cos_ref, sin_ref, gmix_ref, win_ref, lng_ref, lnb_ref, wdiag_ref, bs0_ref, sinks_ref,
                   kc_ref, vc_ref, wpa_ref, wpb_ref, wo_ref, gffn_ref, wr3_ref, br_ref,
                   x1_ref, h2_ref, logits_ref, kwin_ref, vwin_ref, va_ref,
                   q_s, k_s, v_s, yain_s, ga_s, gb_s, att_s):
    i = pl.program_id(0)

    @pl.when(i == 0)
    def _():
        x = x_ref[...]
        cos = jnp.broadcast_to(cos_ref[...], (DEC_BATCH, LANES))
        sin = jnp.broadcast_to(sin_ref[...], (DEC_BATCH, LANES))
        u, va, q, k, v, hb = _in_projection(x, gmix_ref, win_ref, lng_ref, lnb_ref, cos, sin)
        gate_a, gate_b = _gates([_gate_preact(hb, win_ref, 0, 1)])
        va_ref[...] = va
        z = wdiag_ref[...].astype(F32) * va.astype(BF16).astype(F32) + bs0_ref[...]
        yain_s[...] = u * z
        q_s[...] = q
        k_s[...] = k
        v_s[...] = v
        ga_s[...] = gate_a
        gb_s[...] = gate_b

    r0 = pl.multiple_of(i * SAMPLE_STEP, SAMPLE_STEP)
    kwin = jnp.concatenate([kc_ref[:, 1:, :], k_s[pl.ds(r0, SAMPLE_STEP), :][:, None, :]], axis=1)
    vwin = jnp.concatenate([vc_ref[:, 1:, :], v_s[pl.ds(r0, SAMPLE_STEP), :][:, None, :]], axis=1)
    kwin_ref[...] = kwin
    vwin_ref[...] = vwin

    q16 = q_s[pl.ds(r0, SAMPLE_STEP), :]
    lane = lax.broadcasted_iota(I32, (SAMPLE_STEP, LANES), 1)
    heads = [q16[:, hq * LANES:(hq + 1) * LANES] for hq in range(N_HEADS)]
    qpad = pltpu.einshape("hbd->bhd", jnp.stack(heads, axis=0)).astype(BF16)
    s = jnp.einsum("bhd,bkd->bhk", qpad, kwin.astype(BF16), preferred_element_type=F32)
    hid = lax.broadcasted_iota(I32, (1, N_HEADS, 1), 1)
    sink = jnp.zeros((1, N_HEADS, 1), F32)
    for hq in range(N_HEADS):
        sink = jnp.where(hid == hq, sinks_ref[hq], sink)
    m = jnp.maximum(jnp.max(s, axis=-1, keepdims=True), sink)
    e = jnp.exp(s - m)
    inv = 1.0 / (jnp.sum(e, axis=-1, keepdims=True) + jnp.exp(sink - m))
    o = jnp.einsum("bhk,bkd->bhd", (e * inv).astype(BF16), vwin.astype(BF16), preferred_element_type=F32)
    o = pltpu.einshape("bhd->hbd", o)
    chunks = []
    for c in range(N_HEADS // 2):
        parts = []
        for p in range(2):
            hq = 2 * c + p
            oh = o[hq]
            if p != hq // GQA_GROUP:
                oh = pltpu.roll(oh, HEAD_DIM, 1)
            parts.append(oh)
        chunks.append(jnp.where(lane < HEAD_DIM, parts[0], parts[1]))
    att_s[pl.ds(r0, SAMPLE_STEP), :] = jnp.concatenate(chunks, axis=1)

    @pl.when(i == SAMPLE_STEPS - 1)
    def _():
        logits_ref[...] = _merge_and_route(
            x_ref[...], yain_s[...], att_s[...], ga_s[...], gb_s[...],
            wpa_ref, wpb_ref, wo_ref, gffn_ref, wr3_ref, br_ref, x1_ref, h2_ref)


def _sample_front(x, cos, sin, gmix, win, lng, lnb, wdiag, bs0, sinks, kc, vc, wpa, wpb, wo, gffn, wrt, br):
    n = DEC_BATCH
    cache_spec = pl.BlockSpec((SAMPLE_STEP, WINDOW, KV_WIDTH), lambda i: (i, 0, 0))
    in_specs = [
        _full((n, D_MODEL)),
        _full((1, LANES)),
        _full((1, LANES)),
        _full((1, D_MODEL)),
        _full((D_MODEL, IN_COLS)),
        _full((1, A_WIDTH)),
        _full((1, A_WIDTH)),
        _full((1, A_WIDTH)),
        _full((1, A_WIDTH)),
        pl.BlockSpec(memory_space=pltpu.SMEM),
        cache_spec,
        cache_spec,
        _full((A_WIDTH, D_MODEL)),
        _full((Q_WIDTH, D_MODEL)),
        _full((D_MODEL, D_MODEL)),
        _full((1, D_MODEL)),
        _full((3 * D_MODEL, LANES)),
        _full((N_EXPERTS, 1)),
    ]
    out_shape = [
        jax.ShapeDtypeStruct((n * LANE_CHUNKS, LANES), F32),
        jax.ShapeDtypeStruct((n * LANE_CHUNKS, LANES), F32),
        jax.ShapeDtypeStruct((N_EXPERTS, n), F32),
        jax.ShapeDtypeStruct((n, WINDOW, KV_WIDTH), F32),
        jax.ShapeDtypeStruct((n, WINDOW, KV_WIDTH), F32),
        jax.ShapeDtypeStruct((n, A_WIDTH), F32),
    ]
    out_specs = [
        _full((n * LANE_CHUNKS, LANES)),
        _full((n * LANE_CHUNKS, LANES)),
        _full((N_EXPERTS, n)),
        cache_spec,
        cache_spec,
        _full((n, A_WIDTH)),
    ]
    scratch = [
        pltpu.VMEM((n, QPAD_WIDTH), F32), pltpu.VMEM((n, KV_WIDTH), F32), pltpu.VMEM((n, KV_WIDTH), F32),
        pltpu.VMEM((n, A_WIDTH), F32), pltpu.VMEM((n, D_MODEL), F32), pltpu.VMEM((n, D_MODEL), F32),
        pltpu.VMEM((n, Q_WIDTH), F32),
    ]
    return pl.pallas_call(
        _sample_kernel,
        grid=(SAMPLE_STEPS,),
        in_specs=in_specs,
        out_specs=out_specs,
        out_shape=out_shape,
        scratch_shapes=scratch,
        compiler_params=pltpu.CompilerParams(dimension_semantics=("arbitrary",), vmem_limit_bytes=VMEM_LIMIT),
        name="sample_front",
    )(x, cos, sin, gmix, win, lng, lnb, wdiag, bs0, sinks, kc, vc, wpa, wpb, wo, gffn, wrt, br)


def _route_plan_kernel(lp_ref, ls_ref, dest_ref, wts_ref, off_ref):
    g = pl.program_id(0)
    topi, topw = _top4_softmax(jnp.concatenate([lp_ref[...], ls_ref[...]], axis=1))
    slot = lax.broadcasted_iota(I32, (TOP_K, GROUP_SLOTS), 1)
    eall = jnp.where(jnp.logical_or(slot < GROUP_PROMPT, g == N_GROUPS - 1), topi, N_EXPERTS)
    wts_ref[:, 0:GROUP_SLOTS] = topw
    wts_ref[:, GROUP_SLOTS:] = jnp.zeros((TOP_K, K_STRIDE - GROUP_SLOTS), F32)
    dest_ref[:, GROUP_SLOTS:] = jnp.zeros((TOP_K, K_STRIDE - GROUP_SLOTS), I32)
    eid = lax.broadcasted_iota(I32, (N_EXPERTS, GROUP_SLOTS), 0)
    onehots = [eall[k:k + 1, :] == eid for k in range(TOP_K)]
    count = jnp.zeros((N_EXPERTS, GROUP_SLOTS), F32)
    for oh in onehots:
        count = count + oh.astype(F32)
    total = jnp.broadcast_to(jnp.sum(count, axis=1, keepdims=True), (N_EXPERTS, LANES))
    padded = total + (MOE_ROWS - 1)
    nblk = jnp.floor(padded * (1.0 / MOE_ROWS))
    rem = padded - nblk * MOE_ROWS
    nblk = jnp.where(rem >= MOE_ROWS, nblk + 1.0, jnp.where(rem < 0.0, nblk - 1.0, nblk))
    r = lax.broadcasted_iota(I32, (N_EXPERTS, N_EXPERTS), 0)
    c = lax.broadcasted_iota(I32, (N_EXPERTS, N_EXPERTS), 1)
    first_blk = lax.dot_general((c < r).astype(F32), nblk, (((1,), (0,)), ((), ())),
                                precision=lax.Precision.HIGHEST, preferred_element_type=F32)
    start = (first_blk + 1.0) * MOE_ROWS
    lane = lax.broadcasted_iota(I32, (N_EXPERTS, LANES), 1)
    info = jnp.where(lane == 0, start, jnp.where(lane == 1, start + total, jnp.where(lane == 2, nblk, first_blk)))
    off_ref[...] = info.astype(I32)
    ti = lax.broadcasted_iota(I32, (LANES, LANES), 0)
    tj = lax.broadcasted_iota(I32, (LANES, LANES), 1)
    before = (ti < tj).astype(BF16)
    ones = jnp.ones((LANES, LANES), BF16)
    running = start
    for t in range(SLOT_TILES):
        sl = slice(t * LANES, (t + 1) * LANES)
        cb = count[:, sl].astype(BF16)
        pos = running + _bdot(cb, before)
        rows = [jnp.sum(jnp.where(oh[:, sl], pos, 0.0), axis=0, keepdims=True) for oh in onehots]
        dest_ref[:, sl] = jnp.concatenate(rows, axis=0).astype(I32)
        running = running + _bdot(cb, ones)


def _route_plan(logits_p, logits_s):
    in_specs = [
        pl.BlockSpec((N_EXPERTS, GROUP_PROMPT), lambda g: (0, g)),
        pl.BlockSpec((N_EXPERTS, DEC_BATCH), lambda g: (0, 0)),
    ]
    out_shape = [
        jax.ShapeDtypeStruct((N_GROUPS, TOP_K, K_STRIDE), I32),
        jax.ShapeDtypeStruct((N_GROUPS, TOP_K, K_STRIDE), F32),
        jax.ShapeDtypeStruct((N_GROUPS, N_EXPERTS, LANES), I32),
    ]
    out_specs = [
        pl.BlockSpec((None, TOP_K, K_STRIDE), lambda g: (g, 0, 0)),
        pl.BlockSpec((None, TOP_K, K_STRIDE), lambda g: (g, 0, 0)),
        pl.BlockSpec((None, N_EXPERTS, LANES), lambda g: (g, 0, 0)),
    ]
    return pl.pallas_call(
        _route_plan_kernel,
        grid=(N_GROUPS,),
        in_specs=in_specs,
        out_specs=out_specs,
        out_shape=out_shape,
        compiler_params=pltpu.CompilerParams(dimension_semantics=("arbitrary",)),
        name="route_plan",
    )(logits_p, logits_s)


GROUP_ROWS = GROUP_PROMPT * LANE_CHUNKS
SAMPLE_ROWS = DEC_BATCH * LANE_CHUNKS
TRASH_SLOT = GROUP_SLOTS
BUF_ROWS = (GROUP_SLOTS + 1) * LANE_CHUNKS
SCATTER_BATCH = 8
DMA_SPLIT = 16
TABLE_USED = GROUP_SLOTS + LANES
assert TRASH_SLOT < TABLE_USED <= K_STRIDE
FFN_COLS = 256
FFN_DOTS = (2 * D_EXPERT + D_MODEL) // FFN_COLS


def _moe_kernel(off_ref, desth_ref, wtsh_ref, h2p_ref, h2s_ref, x1p_ref, x1s_ref, wgu_ref, bgu_ref, wdn_ref, bdn_ref,
                x2p_ref, x2s_ref,
                h2buf, acc, wgubuf, bgubuf, wdnbuf, bdnbuf, xs0, xs1, ys0, ys1,
                dest_ref, wts_ref, src_ref, seg_expert, seg_first, blk_seg, act_sem, w_sem):
    g = pl.program_id(0)
    last = g == N_GROUPS - 1
    row0 = pl.multiple_of(g * GROUP_ROWS, GROUP_ROWS)

    def prompt_copies():
        cps = []
        for j in range(DMA_SPLIT):
            src = pl.ds(row0 + j * (GROUP_ROWS // DMA_SPLIT), GROUP_ROWS // DMA_SPLIT)
            dst = pl.ds(j * (GROUP_ROWS // DMA_SPLIT), GROUP_ROWS // DMA_SPLIT)
            cps.append(pltpu.make_async_copy(h2p_ref.at[src], h2buf.at[dst], act_sem.at[0]))
            cps.append(pltpu.make_async_copy(x1p_ref.at[src], acc.at[dst], act_sem.at[1]))
        return cps

    def sample_copies():
        return (pltpu.make_async_copy(h2s_ref, h2buf.at[pl.ds(GROUP_ROWS, SAMPLE_ROWS)], act_sem.at[2]),
                pltpu.make_async_copy(x1s_ref, acc.at[pl.ds(GROUP_ROWS, SAMPLE_ROWS)], act_sem.at[3]))

    def weight_copies(e, slot):
        cps = [pltpu.make_async_copy(bgu_ref.at[e], bgubuf.at[slot], w_sem.at[1, slot]),
               pltpu.make_async_copy(bdn_ref.at[e], bdnbuf.at[slot], w_sem.at[3, slot])]
        for j in range(DMA_SPLIT):
            rg = pl.ds(j * (D_MODEL // DMA_SPLIT), D_MODEL // DMA_SPLIT)
            rd = pl.ds(j * (D_EXPERT // DMA_SPLIT), D_EXPERT // DMA_SPLIT)
            cps.append(pltpu.make_async_copy(wgu_ref.at[e, rg], wgubuf.at[slot, rg], w_sem.at[0, slot]))
            cps.append(pltpu.make_async_copy(wdn_ref.at[e, rd], wdnbuf.at[slot, rd], w_sem.at[2, slot]))
        return cps

    def output_copies():
        return [pltpu.make_async_copy(
            acc.at[pl.ds(j * (GROUP_ROWS // DMA_SPLIT), GROUP_ROWS // DMA_SPLIT)],
            x2p_ref.at[pl.ds(row0 + j * (GROUP_ROWS // DMA_SPLIT), GROUP_ROWS // DMA_SPLIT)], act_sem.at[0])
            for j in range(DMA_SPLIT)]

    tab0 = pl.multiple_of(g * (TOP_K * K_STRIDE), TOP_K * K_STRIDE)
    table_copies = []
    for k in range(TOP_K):
        used = pl.ds(k * K_STRIDE, TABLE_USED)
        table_copies.append(pltpu.make_async_copy(
            desth_ref.at[pl.ds(tab0 + k * K_STRIDE, TABLE_USED)], dest_ref.at[used], act_sem.at[4]))
        table_copies.append(pltpu.make_async_copy(
            wtsh_ref.at[pl.ds(tab0 + k * K_STRIDE, TABLE_USED)], wts_ref.at[used], act_sem.at[5]))
    for cp in table_copies:
        cp.start()

    for cp in prompt_copies():
        cp.start()

    @pl.when(last)
    def _():
        for cp in sample_copies():
            cp.start()

    trash = pl.ds(TRASH_SLOT * LANE_CHUNKS, LANE_CHUNKS)
    h2buf[trash, :] = jnp.zeros((LANE_CHUNKS, LANES), F32)
    acc[trash, :] = jnp.zeros((LANE_CHUNKS, LANES), F32)
    ys0[...] = jnp.zeros_like(ys0)
    ys1[...] = jnp.zeros_like(ys1)

    def pad_block(pos0, real_rows=0):
        def body(j, carry):
            for d in range(SUBLANES):
                src_ref[pos0 + MOE_ROWS - SUBLANES - j * SUBLANES + d] = TRASH_SLOT
            return carry
        lax.fori_loop(0, (MOE_ROWS - real_rows + SUBLANES - 1) // SUBLANES, body, 0)

    def scan_expert(e, carry):
        nseg, nblocks = carry
        nblk = off_ref[e, 2]
        first = off_ref[e, 3]

        @pl.when(nblk > 0)
        def _():
            seg_expert[nseg] = e
            seg_first[nseg] = first
            tail_rows = off_ref[e, 1] - off_ref[e, 0] - (nblk - 1) * MOE_ROWS
            pad_block(off_ref[e, 0] + (nblk - 1) * MOE_ROWS, tail_rows)

            def mark(b, c2):
                blk_seg[first + b] = nseg
                return c2
            lax.fori_loop(0, nblk, mark, 0)

            blk_seg[first + nblk - 1] = nseg + jnp.where(tail_rows <= HALF_ROWS, HALF_FLAG, 0)

        return nseg + jnp.where(nblk > 0, 1, 0), nblocks + nblk

    nseg, nblocks = lax.fori_loop(0, N_EXPERTS, scan_expert, (jnp.int32(0), jnp.int32(0)))
    pad_block(0)
    pad_block((nblocks + 1) * MOE_ROWS)
    pad_block((nblocks + 2) * MOE_ROWS)
    blk_seg[nblocks] = nseg - 1 + HALF_FLAG
    blk_seg[nblocks + 1] = nseg - 1 + HALF_FLAG

    for cp in weight_copies(seg_expert[0], 0):
        cp.start()

    for cp in table_copies:
        cp.wait()

    nvalid = jnp.where(last, GROUP_SLOTS, GROUP_PROMPT)
    for k in range(TOP_K):
        def fill(j, carry, k=k):
            c0 = k * K_STRIDE + j * SUBLANES
            for d in range(SUBLANES):
                src_ref[dest_ref[c0 + d]] = c0 + d
            return carry
        lax.fori_loop(0, nvalid // SUBLANES, fill, 0)

    for cp in prompt_copies():
        cp.wait()

    @pl.when(last)
    def _():
        for cp in sample_copies():
            cp.wait()

    def token_rows(code):
        slot_id = code & (K_STRIDE - 1)
        return pl.ds(pl.multiple_of(slot_id * LANE_CHUNKS, LANE_CHUNKS), LANE_CHUNKS)

    def gather(b, xs, lo=0, hi=MOE_ROWS):
        base = (b + 1) * MOE_ROWS
        for m in range(lo, hi):
            xs[pl.ds(m, LANE_CHUNKS, stride=XS_STRIDE), :] = h2buf[token_rows(src_ref[base + m]), :]

    def scatter_add(b, ys, lo=0, hi=MOE_ROWS):
        base = (b + 1) * MOE_ROWS
        for m0 in range(lo, hi, SCATTER_BATCH):
            pending = []
            for m in range(m0, m0 + SCATTER_BATCH):
                code = src_ref[base + m]
                rows = token_rows(code)
                pending.append((rows, acc[rows, :] + wts_ref[code] * ys[pl.ds(m, LANE_CHUNKS, stride=XS_STRIDE), :]))
            for rows, val in pending:
                acc[rows, :] = val

    gather_cuts = [round(i * MOE_ROWS / FFN_DOTS) for i in range(FFN_DOTS + 1)]
    scatter_cuts = [SCATTER_BATCH * round(i * (MOE_ROWS // SCATTER_BATCH) / FFN_DOTS) for i in range(FFN_DOTS + 1)]

    def step(b, xs_cur, xs_next, ys_cur, ys_prev):
        tag = blk_seg[b]
        seg = tag & (HALF_FLAG - 1)
        half = tag >= HALF_FLAG
        slot = seg & 1

        @pl.when(jnp.logical_and(b == seg_first[seg], b < nblocks))
        def _():
            for cp in weight_copies(seg_expert[seg], slot):
                cp.wait()

            @pl.when(seg + 1 < nseg)
            def _():
                for cp in weight_copies(seg_expert[seg + 1], 1 - slot):
                    cp.start()

        def work(rows):
            done = [0]

            def row_traffic():
                i = done[0]
                gather(b + 1, xs_next, gather_cuts[i], gather_cuts[i + 1])
                scatter_add(b - 1, ys_prev, scatter_cuts[i], scatter_cuts[i + 1])
                done[0] = i + 1

            x = jnp.concatenate(
                [xs_cur[c * XS_STRIDE:c * XS_STRIDE + rows, :] for c in range(LANE_CHUNKS)], axis=1).astype(BF16)
            acts = []
            for c in range(D_EXPERT // FFN_COLS):
                gc = pl.ds(c * FFN_COLS, FFN_COLS)
                uc = pl.ds(D_EXPERT + c * FFN_COLS, FFN_COLS)
                g = _bdot(x, wgubuf[slot, :, gc]) + bgubuf[slot, :, gc]
                row_traffic()
                u = _bdot(x, wgubuf[slot, :, uc]) + bgubuf[slot, :, uc]
                row_traffic()
                gl = jnp.minimum(g, SWIGLU_LIMIT)
                ul = jnp.clip(u, -SWIGLU_LIMIT, SWIGLU_LIMIT)
                acts.append((ul + 1.0) * (gl * jax.nn.sigmoid(SWIGLU_ALPHA * gl)))
            a = jnp.concatenate(acts, axis=1).astype(BF16)
            for c in range(D_MODEL // FFN_COLS):
                oc = pl.ds(c * FFN_COLS, FFN_COLS)
                y = _bdot(a, wdnbuf[slot, :, oc]) + bdnbuf[slot, :, oc]
                for j in range(FFN_COLS // LANES):
                    lc = c * (FFN_COLS // LANES) + j
                    ys_cur[lc * XS_STRIDE:lc * XS_STRIDE + rows, :] = y[:, j * LANES:(j + 1) * LANES]
                row_traffic()

        lax.cond(half, lambda: work(HALF_ROWS), lambda: work(MOE_ROWS))

    gather(0, xs0)
    npairs = (nblocks + 1) // 2

    def pair(t, carry):
        step(2 * t, xs0, xs1, ys0, ys1)
        step(2 * t + 1, xs1, xs0, ys1, ys0)
        return carry

    lax.fori_loop(0, npairs, pair, 0)
    scatter_add(2 * npairs - 1, ys1)

    for cp in output_copies():
        cp.start()

    @pl.when(last)
    def _():
        out_s = pltpu.make_async_copy(acc.at[pl.ds(GROUP_ROWS, SAMPLE_ROWS)], x2s_ref, act_sem.at[2])
        out_s.start()
        out_s.wait()

    for cp in output_copies():
        cp.wait()


def _moe(dest, wts, off, h2p, h2s, x1p, x1s, wgu, bgu, wdn, bdn):
    anyspec = pl.BlockSpec(memory_space=pl.ANY)
    dest = dest.reshape(N_GROUPS * TOP_K * K_STRIDE)
    wts = wts.reshape(N_GROUPS * TOP_K * K_STRIDE)
    in_specs = [
        pl.BlockSpec((None, N_EXPERTS, LANES), lambda g: (g, 0, 0), memory_space=pltpu.SMEM),
        anyspec, anyspec, anyspec, anyspec, anyspec, anyspec, anyspec, anyspec, anyspec, anyspec,
    ]
    scratch = [
        pltpu.VMEM((BUF_ROWS, LANES), F32),
        pltpu.VMEM((BUF_ROWS, LANES), F32),
        pltpu.VMEM((2, D_MODEL, 2 * D_EXPERT), BF16),
        pltpu.VMEM((2, 1, 2 * D_EXPERT), F32),
        pltpu.VMEM((2, D_EXPERT, D_MODEL), BF16),
        pltpu.VMEM((2, 1, D_MODEL), F32),
        pltpu.VMEM((LANE_CHUNKS * XS_STRIDE, LANES), F32),
        pltpu.VMEM((LANE_CHUNKS * XS_STRIDE, LANES), F32),
        pltpu.VMEM((LANE_CHUNKS * XS_STRIDE, LANES), F32),
        pltpu.VMEM((LANE_CHUNKS * XS_STRIDE, LANES), F32),
        pltpu.SMEM((TOP_K * K_STRIDE,), I32),
        pltpu.SMEM((TOP_K * K_STRIDE,), F32),
        pltpu.SMEM((POS_TABLE,), I32),
        pltpu.SMEM((N_EXPERTS,), I32),
        pltpu.SMEM((N_EXPERTS,), I32),
        pltpu.SMEM((LANES,), I32),
        pltpu.SemaphoreType.DMA((6,)),
        pltpu.SemaphoreType.DMA((4, 2)),
    ]
    return pl.pallas_call(
        _moe_kernel,
        grid=(N_GROUPS,),
        in_specs=in_specs,
        out_specs=[anyspec, anyspec],
        out_shape=[jax.ShapeDtypeStruct(x1p.shape, F32), jax.ShapeDtypeStruct(x1s.shape, F32)],
        scratch_shapes=scratch,
        compiler_params=pltpu.CompilerParams(dimension_semantics=("arbitrary",), vmem_limit_bytes=VMEM_LIMIT),
        name="moe",
    )(off, dest, wts, h2p, h2s, x1p, x1s, wgu, bgu, wdn, bdn)


def _ple_final_kernel(x2_ref, ple_ref, wple_ref, gple_ref, wpg_ref, gfin_ref, y_ref):
    rows = y_ref.shape[0]
    x2 = _load_token_major(x2_ref, rows)
    e = _rmsnorm(_bdot(ple_ref[...].astype(BF16), wple_ref[...]), gple_ref[...])
    x3 = x2 + jax.nn.sigmoid(_bdot(x2.astype(BF16), wpg_ref[...])) * e
    y_ref[...] = _rmsnorm(x3, gfin_ref[...])


def _ple_final(x2_tm, ple, wple, gple, wpg, gfin, tile):
    n = ple.shape[0]
    return pl.pallas_call(
        _ple_final_kernel,
        grid=(n // tile,),
        in_specs=[
            pl.BlockSpec((tile * LANE_CHUNKS, LANES), lambda i: (i, 0)),
            pl.BlockSpec((tile, PLE_DIM), lambda i: (i, 0)),
            _full((PLE_DIM, D_MODEL)),
            _full((1, D_MODEL)),
            _full((D_MODEL, D_MODEL)),
            _full((1, D_MODEL)),
        ],
        out_specs=pl.BlockSpec((tile, D_MODEL), lambda i: (i, 0)),
        out_shape=jax.ShapeDtypeStruct((n, D_MODEL), F32),
        compiler_params=pltpu.CompilerParams(dimension_semantics=("arbitrary",), vmem_limit_bytes=VMEM_LIMIT),
        name="ple_final",
    )(x2_tm, ple, wple, gple, wpg, gfin)


def _rope_tables(pos):
    half = HEAD_DIM // 2
    inv = ROPE_THETA ** (-jnp.arange(half, dtype=F32) / half)
    ang = pos.astype(F32)[:, None] * inv[None, :]
    cos, sin = jnp.cos(ang), jnp.sin(ang)
    cos2 = jnp.concatenate([cos, cos, cos, cos], axis=1)
    sin2 = jnp.concatenate([-sin, sin, -sin, sin], axis=1)
    return cos2, sin2


def _layout_w_in(w_in):
    col = jnp.arange(IN_COLS)
    return jnp.where((col >= O_Q) & (col < O_K), w_in * (HEAD_DIM ** -0.5), w_in)


def _router_passes(w_router):
    hi = w_router.astype(BF16)
    lo = (w_router - hi.astype(F32)).astype(BF16)
    w3 = jnp.concatenate([hi, lo, hi], axis=0)
    return jnp.pad(w3, ((0, 0), (0, LANES - N_EXPERTS)))


def _prep_weights(g_mix, w_in, a_ln_g, a_ln_b, a_ws, a_bs, w_pa, w_pb, w_o, g_ffn, w_router, b_router):
    causal = jnp.tril(jnp.ones((CHUNK, CHUNK), dtype=bool))
    return dict(
        gmix=g_mix.reshape(1, D_MODEL),
        win=_layout_w_in(w_in).astype(BF16),
        lng=a_ln_g.reshape(1, A_WIDTH),
        lnb=a_ln_b.reshape(1, A_WIDTH),
        ws=jnp.where(causal[None], a_ws, 0.0).astype(BF16),
        bsf=jnp.repeat(jnp.transpose(a_bs), A_GROUP_DIM, axis=1),
        wpa=w_pa.astype(BF16),
        wpb=w_pb.astype(BF16),
        wo=w_o.astype(BF16),
        gffn=g_ffn.reshape(1, D_MODEL),
        wrt=_router_passes(w_router),
        br=b_router.reshape(N_EXPERTS, 1),
    )


def kernel(x_prompt, x_sample, cache_win_k, cache_win_v, p_prompt, p_sample, g_mix, w_in, a_ln_g, a_ln_b, a_ws, a_bs, sinks, w_pa, w_pb, w_o, g_ffn, w_router, b_router, w_gu, b_gu, w_down, b_down, w_ple, g_ple, w_ple_gate, g_final):
    W = _prep_weights(g_mix[0], w_in[0], a_ln_g[0], a_ln_b[0], a_ws[0], a_bs[0], w_pa[0], w_pb[0], w_o[0],
                      g_ffn[0], w_router[0], b_router[0])
    cos_p, sin_p = _rope_tables(jnp.arange(SEQ, dtype=I32))
    cos_s, sin_s = _rope_tables(jnp.full((1,), PAST_LEN, I32))
    x1p, h2p, logits_p, kwin_p, vwin_p, wgu16, wdn16 = _prompt_front(
        x_prompt.reshape(N_PROMPT, D_MODEL), cos_p, sin_p, W["gmix"], W["win"], W["lng"], W["lnb"],
        W["ws"], W["bsf"], sinks[0], _band_bias(), W["wpa"], W["wpb"], W["wo"], W["gffn"], W["wrt"], W["br"],
        w_gu[0].reshape(N_EXPERTS * D_MODEL, 2 * D_EXPERT), w_down[0].reshape(N_EXPERTS * D_EXPERT, D_MODEL))

    wdiag = jnp.repeat(a_ws[0, :, 0, 0], A_GROUP_DIM)[None, :].astype(BF16)
    bs0 = jnp.repeat(a_bs[0, :, 0], A_GROUP_DIM)[None, :]
    x1s, h2s, logits_s, kwin_s, vwin_s, va_s = _sample_front(
        x_sample.reshape(DEC_BATCH, D_MODEL), cos_s, sin_s, W["gmix"], W["win"], W["lng"], W["lnb"], wdiag, bs0,
        sinks[0], cache_win_k[0].reshape(DEC_BATCH, WINDOW, KV_WIDTH), cache_win_v[0].reshape(DEC_BATCH, WINDOW, KV_WIDTH),
        W["wpa"], W["wpb"], W["wo"], W["gffn"], W["wrt"], W["br"])

    dest, wts, off = _route_plan(logits_p, logits_s)
    x2p, x2s = _moe(dest, wts, off, h2p, h2s, x1p, x1s,
                    wgu16.reshape(N_EXPERTS, D_MODEL, 2 * D_EXPERT), b_gu[0].reshape(N_EXPERTS, 1, 2 * D_EXPERT),
                    wdn16.reshape(N_EXPERTS, D_EXPERT, D_MODEL), b_down[0].reshape(N_EXPERTS, 1, D_MODEL))

    wple = w_ple[0].astype(BF16)
    gple = g_ple[0].reshape(1, D_MODEL)
    wpg = w_ple_gate[0].astype(BF16)
    gfin = g_final.reshape(1, D_MODEL)
    y_p = _ple_final(x2p, p_prompt[0].reshape(N_PROMPT, PLE_DIM), wple, gple, wpg, gfin, 4 * TM)
    y_s = _ple_final(x2s, p_sample[0].reshape(DEC_BATCH, PLE_DIM), wple, gple, wpg, gfin, DEC_BATCH)

    return (
        y_p.reshape(BATCH, SEQ, D_MODEL),
        y_s.reshape(DEC_BATCH, 1, D_MODEL),
        kwin_p.reshape(1, BATCH, WINDOW, N_KV_HEADS, HEAD_DIM),
        vwin_p.reshape(1, BATCH, WINDOW, N_KV_HEADS, HEAD_DIM),
        kwin_s.reshape(1, DEC_BATCH, WINDOW, N_KV_HEADS, HEAD_DIM),
        vwin_s.reshape(1, DEC_BATCH, WINDOW, N_KV_HEADS, HEAD_DIM),
        va_s.reshape(1, DEC_BATCH, 1, A_WIDTH),
    )
```

```python
import numpy as np

import jax
import jax.numpy as jnp
from jax import lax
from jax.experimental import pallas as pl
from jax.experimental.pallas import tpu as pltpu

F32 = jnp.float32
BF16 = jnp.bfloat16
I32 = jnp.int32

D_MODEL = 1024
BATCH = 4
SEQ = 4096
DEC_BATCH = 128
PAST_LEN = 8192
CHUNK = 128
A_GROUPS = 4
A_GROUP_DIM = 128
A_WIDTH = A_GROUPS * A_GROUP_DIM
N_HEADS = 8
N_KV_HEADS = 2
HEAD_DIM = 64
Q_WIDTH = N_HEADS * HEAD_DIM
KV_WIDTH = N_KV_HEADS * HEAD_DIM
GQA_GROUP = N_HEADS // N_KV_HEADS
WINDOW = 128
ROPE_THETA = 10000.0
N_EXPERTS = 32
TOP_K = 4
D_EXPERT = D_MODEL
SWIGLU_ALPHA = 1.702
SWIGLU_LIMIT = 7.0
PLE_DIM = 256
RMS_EPS = 1e-5
LN_EPS = 1e-5

LANES = 128

QPAD_WIDTH = N_HEADS * LANES
O_Q = 2 * A_WIDTH
O_K = O_Q + Q_WIDTH
O_V = O_K + KV_WIDTH
O_GA = O_V + KV_WIDTH
O_GB = O_GA + D_MODEL
IN_COLS = O_GB + D_MODEL
SUBLANES = 8
LANE_CHUNKS = D_MODEL // LANES
VMEM_LIMIT = 56 * 1024 * 1024

N_PROMPT = BATCH * SEQ
TM = 256
TILES_PER_SEQ = SEQ // TM
BLOCKS_PER_TILE = TM // WINDOW
FRONT_STEPS = N_PROMPT // TM
CAST_ROWS = N_EXPERTS * D_MODEL // FRONT_STEPS
CAST_SPLIT = 4
IN_PROJ_DOTS = 4
BF16_ROWS = 16
CAST_CUTS = [BF16_ROWS * round(i * (CAST_ROWS // BF16_ROWS) / IN_PROJ_DOTS) for i in range(IN_PROJ_DOTS + 1)]

N_GROUPS = 4
GROUP_PROMPT = N_PROMPT // N_GROUPS
GROUP_SLOTS = GROUP_PROMPT + DEC_BATCH
GROUP_ASSIGN = GROUP_SLOTS * TOP_K
SLOT_TILES = GROUP_SLOTS // LANES
MOE_ROWS = 256
XS_STRIDE = MOE_ROWS + SUBLANES
HALF_ROWS = MOE_ROWS // 2
HALF_FLAG = 64
assert N_EXPERTS <= HALF_FLAG
SLOT_BITS = 13
K_STRIDE = 1 << SLOT_BITS
assert GROUP_SLOTS < K_STRIDE
MAX_BLOCKS = GROUP_ASSIGN // MOE_ROWS + N_EXPERTS
POS_TABLE = 1 << 15
assert (MAX_BLOCKS + 3) * MOE_ROWS <= POS_TABLE
assert MAX_BLOCKS + 2 <= LANES


def _bdot(a, b):
    return jnp.dot(a, b, preferred_element_type=F32)


def _rmsnorm(x, g):
    return x * lax.rsqrt(jnp.mean(x * x, axis=-1, keepdims=True) + RMS_EPS) * g


def _gelu(x):
    return 0.5 * x * (1.0 + lax.erf(x * (0.5 ** 0.5)))


def _group_layernorm(v, g, b):
    cols = []
    for gi in range(A_GROUPS):
        s = slice(gi * A_GROUP_DIM, (gi + 1) * A_GROUP_DIM)
        vg = v[:, s]
        mu = jnp.mean(vg, axis=-1, keepdims=True)
        d = vg - mu
        var = jnp.mean(d * d, axis=-1, keepdims=True)
        cols.append(d * lax.rsqrt(var + LN_EPS) * g[:, s] + b[:, s])
    return jnp.concatenate(cols, axis=1)


def _rope(x, cos, sin_signed):
    width = x.shape[1]
    reps = width // LANES
    cosf = jnp.concatenate([cos] * reps, axis=1) if reps > 1 else cos
    sinf = jnp.concatenate([sin_signed] * reps, axis=1) if reps > 1 else sin_signed
    half = HEAD_DIM // 2
    lane = lax.broadcasted_iota(I32, x.shape, 1)
    up = pltpu.roll(x, width - half, 1)
    down = pltpu.roll(x, half, 1)
    partner = jnp.where((lane & (HEAD_DIM - 1)) < half, up, down)
    return x * cosf + partner * sinf


def _pad_heads(q):
    lane = lax.broadcasted_iota(I32, (q.shape[0], LANES), 1)
    chunks = []
    for hq in range(N_HEADS):
        pair = q[:, (hq // 2) * LANES:(hq // 2 + 1) * LANES]
        kv_head = hq // GQA_GROUP
        if hq % 2 != kv_head:
            pair = pltpu.roll(pair, HEAD_DIM, 1)
        keep = (lane < HEAD_DIM) if kv_head == 0 else (lane >= HEAD_DIM)
        chunks.append(jnp.where(keep, pair, 0.0))
    return jnp.concatenate(chunks, axis=1)


def _in_projection(x, gmix_ref, win_ref, lng_ref, lnb_ref, cos, sin_signed, after_dot=lambda: None):
    hb = _rmsnorm(x, gmix_ref[...]).astype(BF16)

    def dot_cols(lo, hi):
        z = _bdot(hb, win_ref[:, lo:hi])
        after_dot()
        return z

    u = _gelu(dot_cols(0, A_WIDTH))
    va = _group_layernorm(_gelu(dot_cols(A_WIDTH, O_Q)), lng_ref[...], lnb_ref[...])
    zq = dot_cols(O_Q, O_K)
    zkv = dot_cols(O_K, O_GA)
    q = _pad_heads(_rope(zq, cos, sin_signed))
    k = _rope(zkv[:, :KV_WIDTH], cos, sin_signed)
    v = zkv[:, KV_WIDTH:]
    return u, va, q, k, v, hb


def _gate_preact(hb, win_ref, part, parts):
    width = 2 * D_MODEL // parts
    return _bdot(hb, win_ref[:, O_GA + part * width:O_GA + (part + 1) * width])


def _gates(preacts):
    zg = jnp.concatenate(preacts, axis=1)
    return jax.nn.sigmoid(zg[:, :D_MODEL]), jax.nn.sigmoid(zg[:, D_MODEL:])


def _merge_and_route(x, ya_in, att, gate_a, gate_b, wpa_ref, wpb_ref, wo_ref, gffn_ref, wr3_ref, br_ref,
                     x1_ref, h2_ref):
    ya = _bdot(ya_in.astype(BF16), wpa_ref[...])
    yb = _bdot(att.astype(BF16), wpb_ref[...])
    mix = (gate_a * ya + gate_b * yb).astype(BF16)
    x1 = x + _bdot(mix, wo_ref[...])
    _store_token_major(x1_ref, x1)
    h2 = _rmsnorm(x1, gffn_ref[...])
    _store_token_major(h2_ref, h2)
    hi = h2.astype(BF16)
    lo = (h2 - hi.astype(F32)).astype(BF16)
    logits = _bdot(jnp.concatenate([hi, hi, lo], axis=1), wr3_ref[...])
    return jnp.transpose(logits)[:N_EXPERTS, :] + br_ref[...]


def _top4_softmax(logits):
    eid = lax.broadcasted_iota(I32, logits.shape, 0)
    vals, idxs = [], []
    for _ in range(TOP_K):
        m = jnp.max(logits, axis=0, keepdims=True)
        idx = jnp.min(jnp.where(logits == m, eid, N_EXPERTS), axis=0, keepdims=True)
        logits = jnp.where(eid == idx, -jnp.inf, logits)
        vals.append(m)
        idxs.append(idx)
    es = [jnp.exp(v - vals[0]) for v in vals]
    inv = 1.0 / (es[0] + es[1] + es[2] + es[3])
    return jnp.concatenate(idxs, axis=0), jnp.concatenate([e * inv for e in es], axis=0)


def _store_token_major(ref, val):
    rows = val.shape[0]
    for c in range(LANE_CHUNKS):
        ref[pl.ds(c, rows, stride=LANE_CHUNKS), :] = val[:, c * LANES:(c + 1) * LANES]


def _load_token_major(ref, rows):
    return jnp.concatenate([ref[pl.ds(c, rows, stride=LANE_CHUNKS), :] for c in range(LANE_CHUNKS)], axis=1)


def _band_attention(qpad, k, v, k_prev, v_prev, sinks_ref, bias_ref, seq_start, after_scores):
    kb = jnp.concatenate([k_prev, k], axis=0).astype(BF16)
    vt = jnp.transpose(jnp.concatenate([v_prev, v], axis=0)).astype(BF16)
    qb = qpad.astype(BF16)
    lane = lax.broadcasted_iota(I32, (1, GQA_GROUP * WINDOW), 1)
    blocks = []
    for b in range(BLOCKS_PER_TILE):
        bias = bias_ref[jnp.where(seq_start, 1, 0)] if b == 0 else bias_ref[0]
        keys = kb[b * WINDOW:(b + 2) * WINDOW, :]
        pieces = []
        for h in range(N_KV_HEADS):
            qh = jnp.concatenate(
                [qb[b * WINDOW:(b + 1) * WINDOW, (h * GQA_GROUP + j) * LANES:(h * GQA_GROUP + j + 1) * LANES]
                 for j in range(GQA_GROUP)], axis=0)
            st = lax.dot_general(keys, qh, (((1,), (1,)), ((), ())), preferred_element_type=F32) + bias
            after_scores(b * N_KV_HEADS + h)
            sink = jnp.zeros((1, GQA_GROUP * WINDOW), F32)
            for j in range(GQA_GROUP):
                sink = jnp.where(lane // WINDOW == j, sinks_ref[h * GQA_GROUP + j], sink)
            m = jnp.maximum(jnp.max(st, axis=0, keepdims=True), sink)
            e = jnp.exp(st - m)
            inv = 1.0 / (jnp.sum(e, axis=0, keepdims=True) + jnp.exp(sink - m))
            ot = _bdot(vt[h * HEAD_DIM:(h + 1) * HEAD_DIM, b * WINDOW:(b + 2) * WINDOW], (e * inv).astype(BF16))
            pieces.extend(ot[:, j * WINDOW:(j + 1) * WINDOW] for j in range(GQA_GROUP))
        blocks.append(jnp.transpose(jnp.concatenate(pieces, axis=0)))
    return jnp.concatenate(blocks, axis=0)


def _band_bias():
    kj = np.arange(2 * WINDOW)[None, :, None]
    qi = (np.arange(GQA_GROUP * WINDOW) % WINDOW)[None, None, :]
    lo = (np.arange(2) * WINDOW)[:, None, None]
    valid = (kj > qi) & (kj <= qi + WINDOW) & (kj >= lo)
    return jnp.asarray(np.where(valid, 0.0, -np.inf).astype(np.float32))


def _prompt_front_kernel(x_ref, cos_ref, sin_ref, gmix_ref, win_ref, lng_ref, lnb_ref, ws_ref, bsf_ref,
                         sinks_ref, bias_ref, wpa_ref, wpb_ref, wo_ref, gffn_ref, wr3_ref, br_ref, wgu32_ref, wdn32_ref,
                         x1_ref, h2_ref, logits_ref, kwin_ref, vwin_ref, wgu16_ref, wdn16_ref,
                         kprev_ref, vprev_ref, gu_in, dn_in, gu_out, dn_out, cast_sem):
    i = pl.program_id(0)
    seq_start = (i % TILES_PER_SEQ) == 0

    def cast_rows(c, j):
        part = CAST_ROWS // CAST_SPLIT
        return pl.ds(pl.multiple_of(c * CAST_ROWS + j * part, part), part), pl.ds(j * part, part)

    def cast_in(c, slot):
        cps = []
        for j in range(CAST_SPLIT):
            hbm, loc = cast_rows(c, j)
            cps.append(pltpu.make_async_copy(wgu32_ref.at[hbm], gu_in.at[slot, loc], cast_sem.at[0, slot]))
            cps.append(pltpu.make_async_copy(wdn32_ref.at[hbm], dn_in.at[slot, loc], cast_sem.at[1, slot]))
        return cps

    def cast_out(c, slot):
        cps = []
        for j in range(CAST_SPLIT):
            hbm, loc = cast_rows(c, j)
            cps.append(pltpu.make_async_copy(gu_out.at[slot, loc], wgu16_ref.at[hbm], cast_sem.at[2, slot]))
            cps.append(pltpu.make_async_copy(dn_out.at[slot, loc], wdn16_ref.at[hbm], cast_sem.at[3, slot]))
        return cps

    slot = i & 1

    @pl.when(i == 0)
    def _():
        for cp in cast_in(0, 0):
            cp.start()

    @pl.when(i + 1 < FRONT_STEPS)
    def _():
        for cp in cast_in(i + 1, 1 - slot):
            cp.start()

    for cp in cast_in(i, slot):
        cp.wait()

    @pl.when(i >= 2)
    def _():
        for cp in cast_out(i - 2, slot):
            cp.wait()


    @pl.when(seq_start)
    def _():
        kprev_ref[...] = jnp.zeros_like(kprev_ref)
        vprev_ref[...] = jnp.zeros_like(vprev_ref)

    x = x_ref[...]
    cast_done = [0]

    def cast_slice():
        i = cast_done[0]
        r = pl.ds(CAST_CUTS[i], CAST_CUTS[i + 1] - CAST_CUTS[i])
        gu_out[slot, r, :] = gu_in[slot, r, :].astype(BF16)
        dn_out[slot, r, :] = dn_in[slot, r, :].astype(BF16)
        cast_done[0] += 1

    u, va, q, k, v, hb = _in_projection(
        x, gmix_ref, win_ref, lng_ref, lnb_ref, cos_ref[...], sin_ref[...], cast_slice)
    assert cast_done[0] == IN_PROJ_DOTS
    for cp in cast_out(i, slot):
        cp.start()

    units = BLOCKS_PER_TILE * N_KV_HEADS
    preacts = []
    att = _band_attention(q, k, v, kprev_ref[...], vprev_ref[...], sinks_ref, bias_ref, seq_start,
                          lambda unit: preacts.append(_gate_preact(hb, win_ref, unit, units)))
    gate_a, gate_b = _gates(preacts)

    k_last, v_last = k[TM - WINDOW:], v[TM - WINDOW:]
    kprev_ref[...] = k_last
    vprev_ref[...] = v_last
    kwin_ref[0] = k_last
    vwin_ref[0] = v_last

    vab = va.astype(BF16)
    zc = jnp.concatenate(
        [jnp.concatenate(
            [_bdot(ws_ref[g], vab[b * CHUNK:(b + 1) * CHUNK, g * A_GROUP_DIM:(g + 1) * A_GROUP_DIM])
             for g in range(A_GROUPS)], axis=1) + bsf_ref[...]
         for b in range(BLOCKS_PER_TILE)], axis=0)

    logits_ref[...] = _merge_and_route(x, u * zc, att, gate_a, gate_b,
                                       wpa_ref, wpb_ref, wo_ref, gffn_ref, wr3_ref, br_ref, x1_ref, h2_ref)

    @pl.when(i == FRONT_STEPS - 1)
    def _():
        for cp in cast_out(i - 1, 1 - slot) + cast_out(i, slot):
            cp.wait()


def _full(shape):
    return pl.BlockSpec(shape, lambda i: (0,) * len(shape))


def _prompt_front(x, cos, sin, gmix, win, lng, lnb, ws, bsf, sinks, bias, wpa, wpb, wo, gffn, wrt, br, wgu32, wdn32):
    n = x.shape[0]
    assert n == N_PROMPT
    grid = (FRONT_STEPS,)
    anyspec = pl.BlockSpec(memory_space=pl.ANY)
    in_specs = [
        pl.BlockSpec((TM, D_MODEL), lambda i: (i, 0)),
        pl.BlockSpec((TM, LANES), lambda i: (i % TILES_PER_SEQ, 0)),
        pl.BlockSpec((TM, LANES), lambda i: (i % TILES_PER_SEQ, 0)),
        _full((1, D_MODEL)),
        _full((D_MODEL, IN_COLS)),
        _full((1, A_WIDTH)),
        _full((1, A_WIDTH)),
        _full((A_GROUPS, CHUNK, CHUNK)),
        _full((CHUNK, A_WIDTH)),
        pl.BlockSpec(memory_space=pltpu.SMEM),
        _full((2, 2 * WINDOW, GQA_GROUP * WINDOW)),
        _full((A_WIDTH, D_MODEL)),
        _full((Q_WIDTH, D_MODEL)),
        _full((D_MODEL, D_MODEL)),
        _full((1, D_MODEL)),
        _full((3 * D_MODEL, LANES)),
        _full((N_EXPERTS, 1)),
        anyspec,
        anyspec,
    ]
    out_shape = [
        jax.ShapeDtypeStruct((n * LANE_CHUNKS, LANES), F32),
        jax.ShapeDtypeStruct((n * LANE_CHUNKS, LANES), F32),
        jax.ShapeDtypeStruct((N_EXPERTS, n), F32),
        jax.ShapeDtypeStruct((n // SEQ, WINDOW, KV_WIDTH), F32),
        jax.ShapeDtypeStruct((n // SEQ, WINDOW, KV_WIDTH), F32),
        jax.ShapeDtypeStruct(wgu32.shape, BF16),
        jax.ShapeDtypeStruct(wdn32.shape, BF16),
    ]
    out_specs = [
        pl.BlockSpec((TM * LANE_CHUNKS, LANES), lambda i: (i, 0)),
        pl.BlockSpec((TM * LANE_CHUNKS, LANES), lambda i: (i, 0)),
        pl.BlockSpec((N_EXPERTS, TM), lambda i: (0, i)),
        pl.BlockSpec((1, WINDOW, KV_WIDTH), lambda i: (i // TILES_PER_SEQ, 0, 0)),
        pl.BlockSpec((1, WINDOW, KV_WIDTH), lambda i: (i // TILES_PER_SEQ, 0, 0)),
        anyspec,
        anyspec,
    ]
    scratch = [
        pltpu.VMEM((WINDOW, KV_WIDTH), F32),
        pltpu.VMEM((WINDOW, KV_WIDTH), F32),
        pltpu.VMEM((2, CAST_ROWS, 2 * D_EXPERT), F32),
        pltpu.VMEM((2, CAST_ROWS, D_MODEL), F32),
        pltpu.VMEM((2, CAST_ROWS, 2 * D_EXPERT), BF16),
        pltpu.VMEM((2, CAST_ROWS, D_MODEL), BF16),
        pltpu.SemaphoreType.DMA((4, 2)),
    ]
    return pl.pallas_call(
        _prompt_front_kernel,
        grid=grid,
        in_specs=in_specs,
        out_specs=out_specs,
        out_shape=out_shape,
        scratch_shapes=scratch,
        compiler_params=pltpu.CompilerParams(dimension_semantics=("arbitrary",), vmem_limit_bytes=VMEM_LIMIT),
        name="prompt_front",
    )(x, cos, sin, gmix, win, lng, lnb, ws, bsf, sinks, bias, wpa, wpb, wo, gffn, wrt, br, wgu32, wdn32)


SAMPLE_STEP = 16
SAMPLE_STEPS = DEC_BATCH // SAMPLE_STEP


def _sample_kernel(x_ref, cos_ref, sin_ref, gmix_ref, win_ref, lng_ref, lnb_ref, wdiag_ref, bs0_ref, sinks_ref,
                   kc_ref, vc_ref, wpa_ref, wpb_ref, wo_ref, gffn_ref, wr3_ref, br_ref,
                   x1_ref, h2_ref, logits_ref, kwin_ref, vwin_ref, va_ref,
                   q_s, k_s, v_s, yain_s, ga_s, gb_s, att_s):
    i = pl.program_id(0)

    @pl.when(i == 0)
    def _():
        x = x_ref[...]
        cos = jnp.broadcast_to(cos_ref[...], (DEC_BATCH, LANES))
        sin = jnp.broadcast_to(sin_ref[...], (DEC_BATCH, LANES))
        u, va, q, k, v, hb = _in_projection(x, gmix_ref, win_ref, lng_ref, lnb_ref, cos, sin)
        gate_a, gate_b = _gates([_gate_preact(hb, win_ref, 0, 1)])
        va_ref[...] = va
        z = wdiag_ref[...].astype(F32) * va.astype(BF16).astype(F32) + bs0_ref[...]
        yain_s[...] = u * z
        q_s[...] = q
        k_s[...] = k
        v_s[...] = v
        ga_s[...] = gate_a
        gb_s[...] = gate_b

    r0 = pl.multiple_of(i * SAMPLE_STEP, SAMPLE_STEP)
    kwin = jnp.concatenate([kc_ref[:, 1:, :], k_s[pl.ds(r0, SAMPLE_STEP), :][:, None, :]], axis=1)
    vwin = jnp.concatenate([vc_ref[:, 1:, :], v_s[pl.ds(r0, SAMPLE_STEP), :][:, None, :]], axis=1)
    kwin_ref[...] = kwin
    vwin_ref[...] = vwin

    q16 = q_s[pl.ds(r0, SAMPLE_STEP), :]
    lane = lax.broadcasted_iota(I32, (SAMPLE_STEP, LANES), 1)
    heads = [q16[:, hq * LANES:(hq + 1) * LANES] for hq in range(N_HEADS)]
    qpad = pltpu.einshape("hbd->bhd", jnp.stack(heads, axis=0)).astype(BF16)
    s = jnp.einsum("bhd,bkd->bhk", qpad, kwin.astype(BF16), preferred_element_type=F32)
    hid = lax.broadcasted_iota(I32, (1, N_HEADS, 1), 1)
    sink = jnp.zeros((1, N_HEADS, 1), F32)
    for hq in range(N_HEADS):
        sink = jnp.where(hid == hq, sinks_ref[hq], sink)
    m = jnp.maximum(jnp.max(s, axis=-1, keepdims=True), sink)
    e = jnp.exp(s - m)
    inv = 1.0 / (jnp.sum(e, axis=-1, keepdims=True) + jnp.exp(sink - m))
    o = jnp.einsum("bhk,bkd->bhd", (e * inv).astype(BF16), vwin.astype(BF16), preferred_element_type=F32)
    o = pltpu.einshape("bhd->hbd", o)
    chunks = []
    for c in range(N_HEADS // 2):
        parts = []
        for p in range(2):
            hq = 2 * c + p
            oh = o[hq]
            if p != hq // GQA_GROUP:
                oh = pltpu.roll(oh, HEAD_DIM, 1)
            parts.append(oh)
        chunks.append(jnp.where(lane < HEAD_DIM, parts[0], parts[1]))
    att_s[pl.ds(r0, SAMPLE_STEP), :] = jnp.concatenate(chunks, axis=1)

    @pl.when(i == SAMPLE_STEPS - 1)
    def _():
        logits_ref[...] = _merge_and_route(
            x_ref[...], yain_s[...], att_s[...], ga_s[...], gb_s[...],
            wpa_ref, wpb_ref, wo_ref, gffn_ref, wr3_ref, br_ref, x1_ref, h2_ref)


def _sample_front(x, cos, sin, gmix, win, lng, lnb, wdiag, bs0, sinks, kc, vc, wpa, wpb, wo, gffn, wrt, br):
    n = DEC_BATCH
    cache_spec = pl.BlockSpec((SAMPLE_STEP, WINDOW, KV_WIDTH), lambda i: (i, 0, 0))
    in_specs = [
        _full((n, D_MODEL)),
        _full((1, LANES)),
        _full((1, LANES)),
        _full((1, D_MODEL)),
        _full((D_MODEL, IN_COLS)),
        _full((1, A_WIDTH)),
        _full((1, A_WIDTH)),
        _full((1, A_WIDTH)),
        _full((1, A_WIDTH)),
        pl.BlockSpec(memory_space=pltpu.SMEM),
        cache_spec,
        cache_spec,
        _full((A_WIDTH, D_MODEL)),
        _full((Q_WIDTH, D_MODEL)),
        _full((D_MODEL, D_MODEL)),
        _full((1, D_MODEL)),
        _full((3 * D_MODEL, LANES)),
        _full((N_EXPERTS, 1)),
    ]
    out_shape = [
        jax.ShapeDtypeStruct((n * LANE_CHUNKS, LANES), F32),
        jax.ShapeDtypeStruct((n * LANE_CHUNKS, LANES), F32),
        jax.ShapeDtypeStruct((N_EXPERTS, n), F32),
        jax.ShapeDtypeStruct((n, WINDOW, KV_WIDTH), F32),
        jax.ShapeDtypeStruct((n, WINDOW, KV_WIDTH), F32),
        jax.ShapeDtypeStruct((n, A_WIDTH), F32),
    ]
    out_specs = [
        _full((n * LANE_CHUNKS, LANES)),
        _full((n * LANE_CHUNKS, LANES)),
        _full((N_EXPERTS, n)),
        cache_spec,
        cache_spec,
        _full((n, A_WIDTH)),
    ]
    scratch = [
        pltpu.VMEM((n, QPAD_WIDTH), F32), pltpu.VMEM((n, KV_WIDTH), F32), pltpu.VMEM((n, KV_WIDTH), F32),
        pltpu.VMEM((n, A_WIDTH), F32), pltpu.VMEM((n, D_MODEL), F32), pltpu.VMEM((n, D_MODEL), F32),
        pltpu.VMEM((n, Q_WIDTH), F32),
    ]
    return pl.pallas_call(
        _sample_kernel,
        grid=(SAMPLE_STEPS,),
        in_specs=in_specs,
        out_specs=out_specs,
        out_shape=out_shape,
        scratch_shapes=scratch,
        compiler_params=pltpu.CompilerParams(dimension_semantics=("arbitrary",), vmem_limit_bytes=VMEM_LIMIT),
        name="sample_front",
    )(x, cos, sin, gmix, win, lng, lnb, wdiag, bs0, sinks, kc, vc, wpa, wpb, wo, gffn, wrt, br)


def _route_plan_kernel(lp_ref, ls_ref, dest_ref, wts_ref, off_ref):
    g = pl.program_id(0)
    topi, topw = _top4_softmax(jnp.concatenate([lp_ref[...], ls_ref[...]], axis=1))
    slot = lax.broadcasted_iota(I32, (TOP_K, GROUP_SLOTS), 1)
    eall = jnp.where(jnp.logical_or(slot < GROUP_PROMPT, g == N_GROUPS - 1), topi, N_EXPERTS)
    wts_ref[:, 0:GROUP_SLOTS] = topw
    wts_ref[:, GROUP_SLOTS:] = jnp.zeros((TOP_K, K_STRIDE - GROUP_SLOTS), F32)
    dest_ref[:, GROUP_SLOTS:] = jnp.zeros((TOP_K, K_STRIDE - GROUP_SLOTS), I32)
    eid = lax.broadcasted_iota(I32, (N_EXPERTS, GROUP_SLOTS), 0)
    onehots = [eall[k:k + 1, :] == eid for k in range(TOP_K)]
    count = jnp.zeros((N_EXPERTS, GROUP_SLOTS), F32)
    for oh in onehots:
        count = count + oh.astype(F32)
    total = jnp.broadcast_to(jnp.sum(count, axis=1, keepdims=True), (N_EXPERTS, LANES))
    padded = total + (MOE_ROWS - 1)
    nblk = jnp.floor(padded * (1.0 / MOE_ROWS))
    rem = padded - nblk * MOE_ROWS
    nblk = jnp.where(rem >= MOE_ROWS, nblk + 1.0, jnp.where(rem < 0.0, nblk - 1.0, nblk))
    r = lax.broadcasted_iota(I32, (N_EXPERTS, N_EXPERTS), 0)
    c = lax.broadcasted_iota(I32, (N_EXPERTS, N_EXPERTS), 1)
    first_blk = lax.dot_general((c < r).astype(F32), nblk, (((1,), (0,)), ((), ())),
                                precision=lax.Precision.HIGHEST, preferred_element_type=F32)
    start = (first_blk + 1.0) * MOE_ROWS
    lane = lax.broadcasted_iota(I32, (N_EXPERTS, LANES), 1)
    info = jnp.where(lane == 0, start, jnp.where(lane == 1, start + total, jnp.where(lane == 2, nblk, first_blk)))
    off_ref[...] = info.astype(I32)
    ti = lax.broadcasted_iota(I32, (LANES, LANES), 0)
    tj = lax.broadcasted_iota(I32, (LANES, LANES), 1)
    before = (ti < tj).astype(BF16)
    ones = jnp.ones((LANES, LANES), BF16)
    running = start
    for t in range(SLOT_TILES):
        sl = slice(t * LANES, (t + 1) * LANES)
        cb = count[:, sl].astype(BF16)
        pos = running + _bdot(cb, before)
        rows = [jnp.sum(jnp.where(oh[:, sl], pos, 0.0), axis=0, keepdims=True) for oh in onehots]
        dest_ref[:, sl] = jnp.concatenate(rows, axis=0).astype(I32)
        running = running + _bdot(cb, ones)


def _route_plan(logits_p, logits_s):
    in_specs = [
        pl.BlockSpec((N_EXPERTS, GROUP_PROMPT), lambda g: (0, g)),
        pl.BlockSpec((N_EXPERTS, DEC_BATCH), lambda g: (0, 0)),
    ]
    out_shape = [
        jax.ShapeDtypeStruct((N_GROUPS, TOP_K, K_STRIDE), I32),
        jax.ShapeDtypeStruct((N_GROUPS, TOP_K, K_STRIDE), F32),
        jax.ShapeDtypeStruct((N_GROUPS, N_EXPERTS, LANES), I32),
    ]
    out_specs = [
        pl.BlockSpec((None, TOP_K, K_STRIDE), lambda g: (g, 0, 0)),
        pl.BlockSpec((None, TOP_K, K_STRIDE), lambda g: (g, 0, 0)),
        pl.BlockSpec((None, N_EXPERTS, LANES), lambda g: (g, 0, 0)),
    ]
    return pl.pallas_call(
        _route_plan_kernel,
        grid=(N_GROUPS,),
        in_specs=in_specs,
        out_specs=out_specs,
        out_shape=out_shape,
        compiler_params=pltpu.CompilerParams(dimension_semantics=("arbitrary",)),
        name="route_plan",
    )(logits_p, logits_s)


GROUP_ROWS = GROUP_PROMPT * LANE_CHUNKS
SAMPLE_ROWS = DEC_BATCH * LANE_CHUNKS
TRASH_SLOT = GROUP_SLOTS
BUF_ROWS = (GROUP_SLOTS + 1) * LANE_CHUNKS
SCATTER_BATCH = 8
DMA_SPLIT = 4
TABLE_USED = GROUP_SLOTS + LANES
assert TRASH_SLOT < TABLE_USED <= K_STRIDE
FFN_COLS = 256
FFN_DOTS = (2 * D_EXPERT + D_MODEL) // FFN_COLS


def _moe_kernel(off_ref, desth_ref, wtsh_ref, h2p_ref, h2s_ref, x1p_ref, x1s_ref, wgu_ref, bgu_ref, wdn_ref, bdn_ref,
                x2p_ref, x2s_ref,
                h2buf, acc, wgubuf, bgubuf, wdnbuf, bdnbuf, xs0, xs1, ys0, ys1,
                dest_ref, wts_ref, src_ref, seg_expert, seg_first, blk_seg, act_sem, w_sem):
    g = pl.program_id(0)
    last = g == N_GROUPS - 1
    row0 = pl.multiple_of(g * GROUP_ROWS, GROUP_ROWS)

    def prompt_copies():
        cps = []
        for j in range(DMA_SPLIT):
            src = pl.ds(row0 + j * (GROUP_ROWS // DMA_SPLIT), GROUP_ROWS // DMA_SPLIT)
            dst = pl.ds(j * (GROUP_ROWS // DMA_SPLIT), GROUP_ROWS // DMA_SPLIT)
            cps.append(pltpu.make_async_copy(h2p_ref.at[src], h2buf.at[dst], act_sem.at[0]))
            cps.append(pltpu.make_async_copy(x1p_ref.at[src], acc.at[dst], act_sem.at[1]))
        return cps

    def sample_copies():
        return (pltpu.make_async_copy(h2s_ref, h2buf.at[pl.ds(GROUP_ROWS, SAMPLE_ROWS)], act_sem.at[2]),
                pltpu.make_async_copy(x1s_ref, acc.at[pl.ds(GROUP_ROWS, SAMPLE_ROWS)], act_sem.at[3]))

    def weight_copies(e, slot):
        cps = [pltpu.make_async_copy(bgu_ref.at[e], bgubuf.at[slot], w_sem.at[1, slot]),
               pltpu.make_async_copy(bdn_ref.at[e], bdnbuf.at[slot], w_sem.at[3, slot])]
        for j in range(DMA_SPLIT):
            rg = pl.ds(j * (D_MODEL // DMA_SPLIT), D_MODEL // DMA_SPLIT)
            rd = pl.ds(j * (D_EXPERT // DMA_SPLIT), D_EXPERT // DMA_SPLIT)
            cps.append(pltpu.make_async_copy(wgu_ref.at[e, rg], wgubuf.at[slot, rg], w_sem.at[0, slot]))
            cps.append(pltpu.make_async_copy(wdn_ref.at[e, rd], wdnbuf.at[slot, rd], w_sem.at[2, slot]))
        return cps

    def output_copies():
        return [pltpu.make_async_copy(
            acc.at[pl.ds(j * (GROUP_ROWS // DMA_SPLIT), GROUP_ROWS // DMA_SPLIT)],
            x2p_ref.at[pl.ds(row0 + j * (GROUP_ROWS // DMA_SPLIT), GROUP_ROWS // DMA_SPLIT)], act_sem.at[0])
            for j in range(DMA_SPLIT)]

    tab0 = pl.multiple_of(g * (TOP_K * K_STRIDE), TOP_K * K_STRIDE)
    table_copies = []
    for k in range(TOP_K):
        used = pl.ds(k * K_STRIDE, TABLE_USED)
        table_copies.append(pltpu.make_async_copy(
            desth_ref.at[pl.ds(tab0 + k * K_STRIDE, TABLE_USED)], dest_ref.at[used], act_sem.at[4]))
        table_copies.append(pltpu.make_async_copy(
            wtsh_ref.at[pl.ds(tab0 + k * K_STRIDE, TABLE_USED)], wts_ref.at[used], act_sem.at[5]))
    for cp in table_copies:
        cp.start()

    for cp in prompt_copies():
        cp.start()

    @pl.when(last)
    def _():
        for cp in sample_copies():
            cp.start()

    trash = pl.ds(TRASH_SLOT * LANE_CHUNKS, LANE_CHUNKS)
    h2buf[trash, :] = jnp.zeros((LANE_CHUNKS, LANES), F32)
    acc[trash, :] = jnp.zeros((LANE_CHUNKS, LANES), F32)
    ys0[...] = jnp.zeros_like(ys0)
    ys1[...] = jnp.zeros_like(ys1)

    def pad_block(pos0, real_rows=0):
        def body(j, carry):
            for d in range(SUBLANES):
                src_ref[pos0 + MOE_ROWS - SUBLANES - j * SUBLANES + d] = TRASH_SLOT
            return carry
        lax.fori_loop(0, (MOE_ROWS - real_rows + SUBLANES - 1) // SUBLANES, body, 0)

    def scan_expert(e, carry):
        nseg, nblocks = carry
        nblk = off_ref[e, 2]
        first = off_ref[e, 3]

        @pl.when(nblk > 0)
        def _():
            seg_expert[nseg] = e
            seg_first[nseg] = first
            tail_rows = off_ref[e, 1] - off_ref[e, 0] - (nblk - 1) * MOE_ROWS
            pad_block(off_ref[e, 0] + (nblk - 1) * MOE_ROWS, tail_rows)

            def mark(b, c2):
                blk_seg[first + b] = nseg
                return c2
            lax.fori_loop(0, nblk, mark, 0)

            blk_seg[first + nblk - 1] = nseg + jnp.where(tail_rows <= HALF_ROWS, HALF_FLAG, 0)

        return nseg + jnp.where(nblk > 0, 1, 0), nblocks + nblk

    nseg, nblocks = lax.fori_loop(0, N_EXPERTS, scan_expert, (jnp.int32(0), jnp.int32(0)))
    pad_block(0)
    pad_block((nblocks + 1) * MOE_ROWS)
    pad_block((nblocks + 2) * MOE_ROWS)
    blk_seg[nblocks] = nseg - 1 + HALF_FLAG
    blk_seg[nblocks + 1] = nseg - 1 + HALF_FLAG

    for cp in weight_copies(seg_expert[0], 0):
        cp.start()

    for cp in table_copies:
        cp.wait()

    nvalid = jnp.where(last, GROUP_SLOTS, GROUP_PROMPT)
    for k in range(TOP_K):
        def fill(j, carry, k=k):
            c0 = k * K_STRIDE + j * SUBLANES
            for d in range(SUBLANES):
                src_ref[dest_ref[c0 + d]] = c0 + d
            return carry
        lax.fori_loop(0, nvalid // SUBLANES, fill, 0)

    for cp in prompt_copies():
        cp.wait()

    @pl.when(last)
    def _():
        for cp in sample_copies():
            cp.wait()

    def token_rows(code):
        slot_id = code & (K_STRIDE - 1)
        return pl.ds(pl.multiple_of(slot_id * LANE_CHUNKS, LANE_CHUNKS), LANE_CHUNKS)

    def gather(b, xs, lo=0, hi=MOE_ROWS):
        base = (b + 1) * MOE_ROWS
        for m in range(lo, hi):
            xs[pl.ds(m, LANE_CHUNKS, stride=XS_STRIDE), :] = h2buf[token_rows(src_ref[base + m]), :]

    def scatter_add(b, ys, lo=0, hi=MOE_ROWS):
        base = (b + 1) * MOE_ROWS
        for m0 in range(lo, hi, SCATTER_BATCH):
            pending = []
            for m in range(m0, m0 + SCATTER_BATCH):
                code = src_ref[base + m]
                rows = token_rows(code)
                pending.append((rows, acc[rows, :] + wts_ref[code] * ys[pl.ds(m, LANE_CHUNKS, stride=XS_STRIDE), :]))
            for rows, val in pending:
                acc[rows, :] = val

    gather_cuts = [round(i * MOE_ROWS / FFN_DOTS) for i in range(FFN_DOTS + 1)]
    scatter_cuts = [SCATTER_BATCH * round(i * (MOE_ROWS // SCATTER_BATCH) / FFN_DOTS) for i in range(FFN_DOTS + 1)]

    def step(b, xs_cur, xs_next, ys_cur, ys_prev):
        tag = blk_seg[b]
        seg = tag & (HALF_FLAG - 1)
        half = tag >= HALF_FLAG
        slot = seg & 1

        @pl.when(jnp.logical_and(b == seg_first[seg], b < nblocks))
        def _():
            for cp in weight_copies(seg_expert[seg], slot):
                cp.wait()

            @pl.when(seg + 1 < nseg)
            def _():
                for cp in weight_copies(seg_expert[seg + 1], 1 - slot):
                    cp.start()

        def work(rows):
            done = [0]

            def row_traffic():
                i = done[0]
                gather(b + 1, xs_next, gather_cuts[i], gather_cuts[i + 1])
                scatter_add(b - 1, ys_prev, scatter_cuts[i], scatter_cuts[i + 1])
                done[0] = i + 1

            x = jnp.concatenate(
                [xs_cur[c * XS_STRIDE:c * XS_STRIDE + rows, :] for c in range(LANE_CHUNKS)], axis=1).astype(BF16)
            acts = []
            for c in range(D_EXPERT // FFN_COLS):
                gc = pl.ds(c * FFN_COLS, FFN_COLS)
                uc = pl.ds(D_EXPERT + c * FFN_COLS, FFN_COLS)
                g = _bdot(x, wgubuf[slot, :, gc]) + bgubuf[slot, :, gc]
                row_traffic()
                u = _bdot(x, wgubuf[slot, :, uc]) + bgubuf[slot, :, uc]
                row_traffic()
                gl = jnp.minimum(g, SWIGLU_LIMIT)
                ul = jnp.clip(u, -SWIGLU_LIMIT, SWIGLU_LIMIT)
                acts.append((ul + 1.0) * (gl * jax.nn.sigmoid(SWIGLU_ALPHA * gl)))
            a = jnp.concatenate(acts, axis=1).astype(BF16)
            for c in range(D_MODEL // FFN_COLS):
                oc = pl.ds(c * FFN_COLS, FFN_COLS)
                y = _bdot(a, wdnbuf[slot, :, oc]) + bdnbuf[slot, :, oc]
                for j in range(FFN_COLS // LANES):
                    lc = c * (FFN_COLS // LANES) + j
                    ys_cur[lc * XS_STRIDE:lc * XS_STRIDE + rows, :] = y[:, j * LANES:(j + 1) * LANES]
                row_traffic()

        lax.cond(half, lambda: work(HALF_ROWS), lambda: work(MOE_ROWS))

    gather(0, xs0)
    npairs = (nblocks + 1) // 2

    def pair(t, carry):
        step(2 * t, xs0, xs1, ys0, ys1)
        step(2 * t + 1, xs1, xs0, ys1, ys0)
        return carry

    lax.fori_loop(0, npairs, pair, 0)
    scatter_add(2 * npairs - 1, ys1)

    for cp in output_copies():
        cp.start()

    @pl.when(last)
    def _():
        out_s = pltpu.make_async_copy(acc.at[pl.ds(GROUP_ROWS, SAMPLE_ROWS)], x2s_ref, act_sem.at[2])
        out_s.start()
        out_s.wait()

    for cp in output_copies():
        cp.wait()


def _moe(dest, wts, off, h2p, h2s, x1p, x1s, wgu, bgu, wdn, bdn):
    anyspec = pl.BlockSpec(memory_space=pl.ANY)
    dest = dest.reshape(N_GROUPS * TOP_K * K_STRIDE)
    wts = wts.reshape(N_GROUPS * TOP_K * K_STRIDE)
    in_specs = [
        pl.BlockSpec((None, N_EXPERTS, LANES), lambda g: (g, 0, 0), memory_space=pltpu.SMEM),
        anyspec, anyspec, anyspec, anyspec, anyspec, anyspec, anyspec, anyspec, anyspec, anyspec,
    ]
    scratch = [
        pltpu.VMEM((BUF_ROWS, LANES), F32),
        pltpu.VMEM((BUF_ROWS, LANES), F32),
        pltpu.VMEM((2, D_MODEL, 2 * D_EXPERT), BF16),
        pltpu.VMEM((2, 1, 2 * D_EXPERT), F32),
        pltpu.VMEM((2, D_EXPERT, D_MODEL), BF16),
        pltpu.VMEM((2, 1, D_MODEL), F32),
        pltpu.VMEM((LANE_CHUNKS * XS_STRIDE, LANES), F32),
        pltpu.VMEM((LANE_CHUNKS * XS_STRIDE, LANES), F32),
        pltpu.VMEM((LANE_CHUNKS * XS_STRIDE, LANES), F32),
        pltpu.VMEM((LANE_CHUNKS * XS_STRIDE, LANES), F32),
        pltpu.SMEM((TOP_K * K_STRIDE,), I32),
        pltpu.SMEM((TOP_K * K_STRIDE,), F32),
        pltpu.SMEM((POS_TABLE,), I32),
        pltpu.SMEM((N_EXPERTS,), I32),
        pltpu.SMEM((N_EXPERTS,), I32),
        pltpu.SMEM((LANES,), I32),
        pltpu.SemaphoreType.DMA((6,)),
        pltpu.SemaphoreType.DMA((4, 2)),
    ]
    return pl.pallas_call(
        _moe_kernel,
        grid=(N_GROUPS,),
        in_specs=in_specs,
        out_specs=[anyspec, anyspec],
        out_shape=[jax.ShapeDtypeStruct(x1p.shape, F32), jax.ShapeDtypeStruct(x1s.shape, F32)],
        scratch_shapes=scratch,
        compiler_params=pltpu.CompilerParams(dimension_semantics=("arbitrary",), vmem_limit_bytes=VMEM_LIMIT),
        name="moe",
    )(off, dest, wts, h2p, h2s, x1p, x1s, wgu, bgu, wdn, bdn)


def _ple_final_kernel(x2_ref, ple_ref, wple_ref, gple_ref, wpg_ref, gfin_ref, y_ref):
    rows = y_ref.shape[0]
    x2 = _load_token_major(x2_ref, rows)
    e = _rmsnorm(_bdot(ple_ref[...].astype(BF16), wple_ref[...]), gple_ref[...])
    x3 = x2 + jax.nn.sigmoid(_bdot(x2.astype(BF16), wpg_ref[...])) * e
    y_ref[...] = _rmsnorm(x3, gfin_ref[...])


def _ple_final(x2_tm, ple, wple, gple, wpg, gfin, tile):
    n = ple.shape[0]
    return pl.pallas_call(
        _ple_final_kernel,
        grid=(n // tile,),
        in_specs=[
            pl.BlockSpec((tile * LANE_CHUNKS, LANES), lambda i: (i, 0)),
            pl.BlockSpec((tile, PLE_DIM), lambda i: (i, 0)),
            _full((PLE_DIM, D_MODEL)),
            _full((1, D_MODEL)),
            _full((D_MODEL, D_MODEL)),
            _full((1, D_MODEL)),
        ],
        out_specs=pl.BlockSpec((tile, D_MODEL), lambda i: (i, 0)),
        out_shape=jax.ShapeDtypeStruct((n, D_MODEL), F32),
        compiler_params=pltpu.CompilerParams(dimension_semantics=("arbitrary",), vmem_limit_bytes=VMEM_LIMIT),
        name="ple_final",
    )(x2_tm, ple, wple, gple, wpg, gfin)


def _rope_tables(pos):
    half = HEAD_DIM // 2
    inv = ROPE_THETA ** (-jnp.arange(half, dtype=F32) / half)
    ang = pos.astype(F32)[:, None] * inv[None, :]
    cos, sin = jnp.cos(ang), jnp.sin(ang)
    cos2 = jnp.concatenate([cos, cos, cos, cos], axis=1)
    sin2 = jnp.concatenate([-sin, sin, -sin, sin], axis=1)
    return cos2, sin2


def _layout_w_in(w_in):
    col = jnp.arange(IN_COLS)
    return jnp.where((col >= O_Q) & (col < O_K), w_in * (HEAD_DIM ** -0.5), w_in)


def _router_passes(w_router):
    hi = w_router.astype(BF16)
    lo = (w_router - hi.astype(F32)).astype(BF16)
    w3 = jnp.concatenate([hi, lo, hi], axis=0)
    return jnp.pad(w3, ((0, 0), (0, LANES - N_EXPERTS)))


def _prep_weights(g_mix, w_in, a_ln_g, a_ln_b, a_ws, a_bs, w_pa, w_pb, w_o, g_ffn, w_router, b_router):
    causal = jnp.tril(jnp.ones((CHUNK, CHUNK), dtype=bool))
    return dict(
        gmix=g_mix.reshape(1, D_MODEL),
        win=_layout_w_in(w_in).astype(BF16),
        lng=a_ln_g.reshape(1, A_WIDTH),
        lnb=a_ln_b.reshape(1, A_WIDTH),
        ws=jnp.where(causal[None], a_ws, 0.0).astype(BF16),
        bsf=jnp.repeat(jnp.transpose(a_bs), A_GROUP_DIM, axis=1),
        wpa=w_pa.astype(BF16),
        wpb=w_pb.astype(BF16),
        wo=w_o.astype(BF16),
        gffn=g_ffn.reshape(1, D_MODEL),
        wrt=_router_passes(w_router),
        br=b_router.reshape(N_EXPERTS, 1),
    )


def kernel(x_prompt, x_sample, cache_win_k, cache_win_v, p_prompt, p_sample, g_mix, w_in, a_ln_g, a_ln_b, a_ws, a_bs, sinks, w_pa, w_pb, w_o, g_ffn, w_router, b_router, w_gu, b_gu, w_down, b_down, w_ple, g_ple, w_ple_gate, g_final):
    W = _prep_weights(g_mix[0], w_in[0], a_ln_g[0], a_ln_b[0], a_ws[0], a_bs[0], w_pa[0], w_pb[0], w_o[0],
                      g_ffn[0], w_router[0], b_router[0])
    cos_p, sin_p = _rope_tables(jnp.arange(SEQ, dtype=I32))
    cos_s, sin_s = _rope_tables(jnp.full((1,), PAST_LEN, I32))
    x1p, h2p, logits_p, kwin_p, vwin_p, wgu16, wdn16 = _prompt_front(
        x_prompt.reshape(N_PROMPT, D_MODEL), cos_p, sin_p, W["gmix"], W["win"], W["lng"], W["lnb"],
        W["ws"], W["bsf"], sinks[0], _band_bias(), W["wpa"], W["wpb"], W["wo"], W["gffn"], W["wrt"], W["br"],
        w_gu[0].reshape(N_EXPERTS * D_MODEL, 2 * D_EXPERT), w_down[0].reshape(N_EXPERTS * D_EXPERT, D_MODEL))

    wdiag = jnp.repeat(a_ws[0, :, 0, 0], A_GROUP_DIM)[None, :].astype(BF16)
    bs0 = jnp.repeat(a_bs[0, :, 0], A_GROUP_DIM)[None, :]
    x1s, h2s, logits_s, kwin_s, vwin_s, va_s = _sample_front(
        x_sample.reshape(DEC_BATCH, D_MODEL), cos_s, sin_s, W["gmix"], W["win"], W["lng"], W["lnb"], wdiag, bs0,
        sinks[0], cache_win_k[0].reshape(DEC_BATCH, WINDOW, KV_WIDTH), cache_win_v[0].reshape(DEC_BATCH, WINDOW, KV_WIDTH),
        W["wpa"], W["wpb"], W["wo"], W["gffn"], W["wrt"], W["br"])

    dest, wts, off = _route_plan(logits_p, logits_s)
    x2p, x2s = _moe(dest, wts, off, h2p, h2s, x1p, x1s,
                    wgu16.reshape(N_EXPERTS, D_MODEL, 2 * D_EXPERT), b_gu[0].reshape(N_EXPERTS, 1, 2 * D_EXPERT),
                    wdn16.reshape(N_EXPERTS, D_EXPERT, D_MODEL), b_down[0].reshape(N_EXPERTS, 1, D_MODEL))

    wple = w_ple[0].astype(BF16)
    gple = g_ple[0].reshape(1, D_MODEL)
    wpg = w_ple_gate[0].astype(BF16)
    gfin = g_final.reshape(1, D_MODEL)
    y_p = _ple_final(x2p, p_prompt[0].reshape(N_PROMPT, PLE_DIM), wple, gple, wpg, gfin, 4 * TM)
    y_s = _ple_final(x2s, p_sample[0].reshape(DEC_BATCH, PLE_DIM), wple, gple, wpg, gfin, DEC_BATCH)

    return (
        y_p.reshape(BATCH, SEQ, D_MODEL),
        y_s.reshape(DEC_BATCH, 1, D_MODEL),
        kwin_p.reshape(1, BATCH, WINDOW, N_KV_HEADS, HEAD_DIM),
        vwin_p.reshape(1, BATCH, WINDOW, N_KV_HEADS, HEAD_DIM),
        kwin_s.reshape(1, DEC_BATCH, WINDOW, N_KV_HEADS, HEAD_DIM),
        vwin_s.reshape(1, DEC_BATCH, WINDOW, N_KV_HEADS, HEAD_DIM),
        va_s.reshape(1, DEC_BATCH, 1, A_WIDTH),
    )
```

```python
import numpy as np

import jax
import jax.numpy as jnp
from jax import lax
from jax.experimental import pallas as pl
from jax.experimental.pallas import tpu as pltpu

F32 = jnp.float32
BF16 = jnp.bfloat16
I32 = jnp.int32

D_MODEL = 1024
BATCH = 4
SEQ = 4096
DEC_BATCH = 128
PAST_LEN = 8192
CHUNK = 128
A_GROUPS = 4
A_GROUP_DIM = 128
A_WIDTH = A_GROUPS * A_GROUP_DIM
N_HEADS = 8
N_KV_HEADS = 2
HEAD_DIM = 64
Q_WIDTH = N_HEADS * HEAD_DIM
KV_WIDTH = N_KV_HEADS * HEAD_DIM
GQA_GROUP = N_HEADS // N_KV_HEADS
WINDOW = 128
ROPE_THETA = 10000.0
N_EXPERTS = 32
TOP_K = 4
D_EXPERT = D_MODEL
SWIGLU_ALPHA = 1.702
SWIGLU_LIMIT = 7.0
PLE_DIM = 256
RMS_EPS = 1e-5
LN_EPS = 1e-5

LANES = 128

QPAD_WIDTH = N_HEADS * LANES
O_Q = 2 * A_WIDTH
O_K = O_Q + Q_WIDTH
O_V = O_K + KV_WIDTH
O_GA = O_V + KV_WIDTH
O_GB = O_GA + D_MODEL
IN_COLS = O_GB + D_MODEL
SUBLANES = 8
LANE_CHUNKS = D_MODEL // LANES
VMEM_LIMIT = 56 * 1024 * 1024

N_PROMPT = BATCH * SEQ
TM = 256
TILES_PER_SEQ = SEQ // TM
BLOCKS_PER_TILE = TM // WINDOW
FRONT_STEPS = N_PROMPT // TM
CAST_ROWS = N_EXPERTS * D_MODEL // FRONT_STEPS
CAST_SPLIT = 4
IN_PROJ_DOTS = 4
BF16_ROWS = 16
CAST_CUTS = [BF16_ROWS * round(i * (CAST_ROWS // BF16_ROWS) / IN_PROJ_DOTS) for i in range(IN_PROJ_DOTS + 1)]

N_GROUPS = 4
GROUP_PROMPT = N_PROMPT // N_GROUPS
GROUP_SLOTS = GROUP_PROMPT + DEC_BATCH
GROUP_ASSIGN = GROUP_SLOTS * TOP_K
SLOT_TILES = GROUP_SLOTS // LANES
MOE_ROWS = 256
XS_STRIDE = MOE_ROWS + SUBLANES
HALF_ROWS = MOE_ROWS // 2
HALF_FLAG = 64
assert N_EXPERTS <= HALF_FLAG
SLOT_BITS = 13
K_STRIDE = 1 << SLOT_BITS
assert GROUP_SLOTS < K_STRIDE
MAX_BLOCKS = GROUP_ASSIGN // MOE_ROWS + N_EXPERTS
POS_TABLE = 1 << 15
assert (MAX_BLOCKS + 3) * MOE_ROWS <= POS_TABLE
assert MAX_BLOCKS + 2 <= LANES


def _bdot(a, b):
    return jnp.dot(a, b, preferred_element_type=F32)


def _rmsnorm(x, g):
    return x * lax.rsqrt(jnp.mean(x * x, axis=-1, keepdims=True) + RMS_EPS) * g


def _gelu(x):
    return 0.5 * x * (1.0 + lax.erf(x * (0.5 ** 0.5)))


def _group_layernorm(v, g, b):
    cols = []
    for gi in range(A_GROUPS):
        s = slice(gi * A_GROUP_DIM, (gi + 1) * A_GROUP_DIM)
        vg = v[:, s]
        mu = jnp.mean(vg, axis=-1, keepdims=True)
        d = vg - mu
        var = jnp.mean(d * d, axis=-1, keepdims=True)
        cols.append(d * lax.rsqrt(var + LN_EPS) * g[:, s] + b[:, s])
    return jnp.concatenate(cols, axis=1)


def _rope(x, cos, sin_signed):
    width = x.shape[1]
    reps = width // LANES
    cosf = jnp.concatenate([cos] * reps, axis=1) if reps > 1 else cos
    sinf = jnp.concatenate([sin_signed] * reps, axis=1) if reps > 1 else sin_signed
    half = HEAD_DIM // 2
    lane = lax.broadcasted_iota(I32, x.shape, 1)
    up = pltpu.roll(x, width - half, 1)
    down = pltpu.roll(x, half, 1)
    partner = jnp.where((lane & (HEAD_DIM - 1)) < half, up, down)
    return x * cosf + partner * sinf


def _pad_heads(q):
    lane = lax.broadcasted_iota(I32, (q.shape[0], LANES), 1)
    chunks = []
    for hq in range(N_HEADS):
        pair = q[:, (hq // 2) * LANES:(hq // 2 + 1) * LANES]
        kv_head = hq // GQA_GROUP
        if hq % 2 != kv_head:
            pair = pltpu.roll(pair, HEAD_DIM, 1)
        keep = (lane < HEAD_DIM) if kv_head == 0 else (lane >= HEAD_DIM)
        chunks.append(jnp.where(keep, pair, 0.0))
    return jnp.concatenate(chunks, axis=1)


def _in_projection(x, gmix_ref, win_ref, lng_ref, lnb_ref, cos, sin_signed, after_dot=lambda: None):
    hb = _rmsnorm(x, gmix_ref[...]).astype(BF16)

    def dot_cols(lo, hi):
        z = _bdot(hb, win_ref[:, lo:hi])
        after_dot()
        return z

    u = _gelu(dot_cols(0, A_WIDTH))
    va = _group_layernorm(_gelu(dot_cols(A_WIDTH, O_Q)), lng_ref[...], lnb_ref[...])
    zq = dot_cols(O_Q, O_K)
    zkv = dot_cols(O_K, O_GA)
    q = _pad_heads(_rope(zq, cos, sin_signed))
    k = _rope(zkv[:, :KV_WIDTH], cos, sin_signed)
    v = zkv[:, KV_WIDTH:]
    return u, va, q, k, v, hb


def _gate_preact(hb, win_ref, part, parts):
    width = 2 * D_MODEL // parts
    return _bdot(hb, win_ref[:, O_GA + part * width:O_GA + (part + 1) * width])


def _gates(preacts):
    zg = jnp.concatenate(preacts, axis=1)
    return jax.nn.sigmoid(zg[:, :D_MODEL]), jax.nn.sigmoid(zg[:, D_MODEL:])


def _merge_and_route(x, ya_in, att, gate_a, gate_b, wpa_ref, wpb_ref, wo_ref, gffn_ref, wr3_ref, br_ref,
                     x1_ref, h2_ref):
    ya = _bdot(ya_in.astype(BF16), wpa_ref[...])
    yb = _bdot(att.astype(BF16), wpb_ref[...])
    mix = (gate_a * ya + gate_b * yb).astype(BF16)
    x1 = x + _bdot(mix, wo_ref[...])
    _store_token_major(x1_ref, x1)
    h2 = _rmsnorm(x1, gffn_ref[...])
    _store_token_major(h2_ref, h2)
    hi = h2.astype(BF16)
    lo = (h2 - hi.astype(F32)).astype(BF16)
    logits = _bdot(jnp.concatenate([hi, hi, lo], axis=1), wr3_ref[...])
    return jnp.transpose(logits)[:N_EXPERTS, :] + br_ref[...]


def _top4_softmax(logits):
    eid = lax.broadcasted_iota(I32, logits.shape, 0)
    vals, idxs = [], []
    for _ in range(TOP_K):
        m = jnp.max(logits, axis=0, keepdims=True)
        idx = jnp.min(jnp.where(logits == m, eid, N_EXPERTS), axis=0, keepdims=True)
        logits = jnp.where(eid == idx, -jnp.inf, logits)
        vals.append(m)
        idxs.append(idx)
    es = [jnp.exp(v - vals[0]) for v in vals]
    inv = 1.0 / (es[0] + es[1] + es[2] + es[3])
    return jnp.concatenate(idxs, axis=0), jnp.concatenate([e * inv for e in es], axis=0)


def _store_token_major(ref, val):
    rows = val.shape[0]
    for c in range(LANE_CHUNKS):
        ref[pl.ds(c, rows, stride=LANE_CHUNKS), :] = val[:, c * LANES:(c + 1) * LANES]


def _load_token_major(ref, rows):
    return jnp.concatenate([ref[pl.ds(c, rows, stride=LANE_CHUNKS), :] for c in range(LANE_CHUNKS)], axis=1)


def _band_attention(qpad, k, v, k_prev, v_prev, sinks_ref, bias_ref, seq_start, after_scores):
    kb = jnp.concatenate([k_prev, k], axis=0).astype(BF16)
    vt = jnp.transpose(jnp.concatenate([v_prev, v], axis=0)).astype(BF16)
    qb = qpad.astype(BF16)
    lane = lax.broadcasted_iota(I32, (1, GQA_GROUP * WINDOW), 1)
    blocks = []
    for b in range(BLOCKS_PER_TILE):
        bias = bias_ref[jnp.where(seq_start, 1, 0)] if b == 0 else bias_ref[0]
        keys = kb[b * WINDOW:(b + 2) * WINDOW, :]
        pieces = []
        for h in range(N_KV_HEADS):
            qh = jnp.concatenate(
                [qb[b * WINDOW:(b + 1) * WINDOW, (h * GQA_GROUP + j) * LANES:(h * GQA_GROUP + j + 1) * LANES]
                 for j in range(GQA_GROUP)], axis=0)
            st = lax.dot_general(keys, qh, (((1,), (1,)), ((), ())), preferred_element_type=F32) + bias
            after_scores(b * N_KV_HEADS + h)
            sink = jnp.zeros((1, GQA_GROUP * WINDOW), F32)
            for j in range(GQA_GROUP):
                sink = jnp.where(lane // WINDOW == j, sinks_ref[h * GQA_GROUP + j], sink)
            m = jnp.maximum(jnp.max(st, axis=0, keepdims=True), sink)
            e = jnp.exp(st - m)
            inv = 1.0 / (jnp.sum(e, axis=0, keepdims=True) + jnp.exp(sink - m))
            ot = _bdot(vt[h * HEAD_DIM:(h + 1) * HEAD_DIM, b * WINDOW:(b + 2) * WINDOW], (e * inv).astype(BF16))
            pieces.extend(ot[:, j * WINDOW:(j + 1) * WINDOW] for j in range(GQA_GROUP))
        blocks.append(jnp.transpose(jnp.concatenate(pieces, axis=0)))
    return jnp.concatenate(blocks, axis=0)


def _band_bias():
    kj = np.arange(2 * WINDOW)[None, :, None]
    qi = (np.arange(GQA_GROUP * WINDOW) % WINDOW)[None, None, :]
    lo = (np.arange(2) * WINDOW)[:, None, None]
    valid = (kj > qi) & (kj <= qi + WINDOW) & (kj >= lo)
    return jnp.asarray(np.where(valid, 0.0, -np.inf).astype(np.float32))


def _prompt_front_kernel(x_ref, cos_ref, sin_ref, gmix_ref, win_ref, lng_ref, lnb_ref, ws_ref, bsf_ref,
                         sinks_ref, bias_ref, wpa_ref, wpb_ref, wo_ref, gffn_ref, wr3_ref, br_ref, wgu32_ref, wdn32_ref,
                         x1_ref, h2_ref, logits_ref, kwin_ref, vwin_ref, wgu16_ref, wdn16_ref,
                         kprev_ref, vprev_ref, gu_in, dn_in, gu_out, dn_out, cast_sem):
    i = pl.program_id(0)
    seq_start = (i % TILES_PER_SEQ) == 0

    def cast_rows(c, j):
        part = CAST_ROWS // CAST_SPLIT
        return pl.ds(pl.multiple_of(c * CAST_ROWS + j * part, part), part), pl.ds(j * part, part)

    def cast_in(c, slot):
        cps = []
        for j in range(CAST_SPLIT):
            hbm, loc = cast_rows(c, j)
            cps.append(pltpu.make_async_copy(wgu32_ref.at[hbm], gu_in.at[slot, loc], cast_sem.at[0, slot]))
            cps.append(pltpu.make_async_copy(wdn32_ref.at[hbm], dn_in.at[slot, loc], cast_sem.at[1, slot]))
        return cps

    def cast_out(c, slot):
        cps = []
        for j in range(CAST_SPLIT):
            hbm, loc = cast_rows(c, j)
            cps.append(pltpu.make_async_copy(gu_out.at[slot, loc], wgu16_ref.at[hbm], cast_sem.at[2, slot]))
            cps.append(pltpu.make_async_copy(dn_out.at[slot, loc], wdn16_ref.at[hbm], cast_sem.at[3, slot]))
        return cps

    slot = i & 1

    @pl.when(i == 0)
    def _():
        for cp in cast_in(0, 0):
            cp.start()

    @pl.when(i + 1 < FRONT_STEPS)
    def _():
        for cp in cast_in(i + 1, 1 - slot):
            cp.start()

    for cp in cast_in(i, slot):
        cp.wait()

    @pl.when(i >= 2)
    def _():
        for cp in cast_out(i - 2, slot):
            cp.wait()


    @pl.when(seq_start)
    def _():
        kprev_ref[...] = jnp.zeros_like(kprev_ref)
        vprev_ref[...] = jnp.zeros_like(vprev_ref)

    x = x_ref[...]
    cast_done = [0]

    def cast_slice():
        i = cast_done[0]
        r = pl.ds(CAST_CUTS[i], CAST_CUTS[i + 1] - CAST_CUTS[i])
        gu_out[slot, r, :] = gu_in[slot, r, :].astype(BF16)
        dn_out[slot, r, :] = dn_in[slot, r, :].astype(BF16)
        cast_done[0] += 1

    u, va, q, k, v, hb = _in_projection(
        x, gmix_ref, win_ref, lng_ref, lnb_ref, cos_ref[...], sin_ref[...], cast_slice)
    assert cast_done[0] == IN_PROJ_DOTS
    for cp in cast_out(i, slot):
        cp.start()

    units = BLOCKS_PER_TILE * N_KV_HEADS
    preacts = []
    att = _band_attention(q, k, v, kprev_ref[...], vprev_ref[...], sinks_ref, bias_ref, seq_start,
                          lambda unit: preacts.append(_gate_preact(hb, win_ref, unit, units)))
    gate_a, gate_b = _gates(preacts)

    k_last, v_last = k[TM - WINDOW:], v[TM - WINDOW:]
    kprev_ref[...] = k_last
    vprev_ref[...] = v_last
    kwin_ref[0] = k_last
    vwin_ref[0] = v_last

    vab = va.astype(BF16)
    zc = jnp.concatenate(
        [jnp.concatenate(
            [_bdot(ws_ref[g], vab[b * CHUNK:(b + 1) * CHUNK, g * A_GROUP_DIM:(g + 1) * A_GROUP_DIM])
             for g in range(A_GROUPS)], axis=1) + bsf_ref[...]
         for b in range(BLOCKS_PER_TILE)], axis=0)

    logits_ref[...] = _merge_and_route(x, u * zc, att, gate_a, gate_b,
                                       wpa_ref, wpb_ref, wo_ref, gffn_ref, wr3_ref, br_ref, x1_ref, h2_ref)

    @pl.when(i == FRONT_STEPS - 1)
    def _():
        for cp in cast_out(i - 1, 1 - slot) + cast_out(i, slot):
            cp.wait()


def _full(shape):
    return pl.BlockSpec(shape, lambda i: (0,) * len(shape))


def _prompt_front(x, cos, sin, gmix, win, lng, lnb, ws, bsf, sinks, bias, wpa, wpb, wo, gffn, wrt, br, wgu32, wdn32):
    n = x.shape[0]
    assert n == N_PROMPT
    grid = (FRONT_STEPS,)
    anyspec = pl.BlockSpec(memory_space=pl.ANY)
    in_specs = [
        pl.BlockSpec((TM, D_MODEL), lambda i: (i, 0)),
        pl.BlockSpec((TM, LANES), lambda i: (i % TILES_PER_SEQ, 0)),
        pl.BlockSpec((TM, LANES), lambda i: (i % TILES_PER_SEQ, 0)),
        _full((1, D_MODEL)),
        _full((D_MODEL, IN_COLS)),
        _full((1, A_WIDTH)),
        _full((1, A_WIDTH)),
        _full((A_GROUPS, CHUNK, CHUNK)),
        _full((CHUNK, A_WIDTH)),
        pl.BlockSpec(memory_space=pltpu.SMEM),
        _full((2, 2 * WINDOW, GQA_GROUP * WINDOW)),
        _full((A_WIDTH, D_MODEL)),
        _full((Q_WIDTH, D_MODEL)),
        _full((D_MODEL, D_MODEL)),
        _full((1, D_MODEL)),
        _full((3 * D_MODEL, LANES)),
        _full((N_EXPERTS, 1)),
        anyspec,
        anyspec,
    ]
    out_shape = [
        jax.ShapeDtypeStruct((n * LANE_CHUNKS, LANES), F32),
        jax.ShapeDtypeStruct((n * LANE_CHUNKS, LANES), F32),
        jax.ShapeDtypeStruct((N_EXPERTS, n), F32),
        jax.ShapeDtypeStruct((n // SEQ, WINDOW, KV_WIDTH), F32),
        jax.ShapeDtypeStruct((n // SEQ, WINDOW, KV_WIDTH), F32),
        jax.ShapeDtypeStruct(wgu32.shape, BF16),
        jax.ShapeDtypeStruct(wdn32.shape, BF16),
    ]
    out_specs = [
        pl.BlockSpec((TM * LANE_CHUNKS, LANES), lambda i: (i, 0)),
        pl.BlockSpec((TM * LANE_CHUNKS, LANES), lambda i: (i, 0)),
        pl.BlockSpec((N_EXPERTS, TM), lambda i: (0, i)),
        pl.BlockSpec((1, WINDOW, KV_WIDTH), lambda i: (i // TILES_PER_SEQ, 0, 0)),
        pl.BlockSpec((1, WINDOW, KV_WIDTH), lambda i: (i // TILES_PER_SEQ, 0, 0)),
        anyspec,
        anyspec,
    ]
    scratch = [
        pltpu.VMEM((WINDOW, KV_WIDTH), F32),
        pltpu.VMEM((WINDOW, KV_WIDTH), F32),
        pltpu.VMEM((2, CAST_ROWS, 2 * D_EXPERT), F32),
        pltpu.VMEM((2, CAST_ROWS, D_MODEL), F32),
        pltpu.VMEM((2, CAST_ROWS, 2 * D_EXPERT), BF16),
        pltpu.VMEM((2, CAST_ROWS, D_MODEL), BF16),
        pltpu.SemaphoreType.DMA((4, 2)),
    ]
    return pl.pallas_call(
        _prompt_front_kernel,
        grid=grid,
        in_specs=in_specs,
        out_specs=out_specs,
        out_shape=out_shape,
        scratch_shapes=scratch,
        compiler_params=pltpu.CompilerParams(dimension_semantics=("arbitrary",), vmem_limit_bytes=VMEM_LIMIT),
        name="prompt_front",
    )(x, cos, sin, gmix, win, lng, lnb, ws, bsf, sinks, bias, wpa, wpb, wo, gffn, wrt, br, wgu32, wdn32)


SAMPLE_STEP = 16
SAMPLE_STEPS = DEC_BATCH // SAMPLE_STEP


def _sample_kernel(x_ref, cos_ref, sin_ref, gmix_ref, win_ref, lng_ref, lnb_ref, wdiag_ref, bs0_ref, sinks_ref,
                   kc_ref, vc_ref, wpa_ref, wpb_ref, wo_ref, gffn_ref, wr3_ref, br_ref,
                   x1_ref, h2_ref, logits_ref, kwin_ref, vwin_ref, va_ref,
                   q_s, k_s, v_s, yain_s, ga_s, gb_s, att_s):
    i = pl.program_id(0)

    @pl.when(i == 0)
    def _():
        x = x_ref[...]
        cos = jnp.broadcast_to(cos_ref[...], (DEC_BATCH, LANES))
        sin = jnp.broadcast_to(sin_ref[...], (DEC_BATCH, LANES))
        u, va, q, k, v, hb = _in_projection(x, gmix_ref, win_ref, lng_ref, lnb_ref, cos, sin)
        gate_a, gate_b = _gates([_gate_preact(hb, win_ref, 0, 1)])
        va_ref[...] = va
        z = wdiag_ref[...].astype(F32) * va.astype(BF16).astype(F32) + bs0_ref[...]
        yain_s[...] = u * z
        q_s[...] = q
        k_s[...] = k
        v_s[...] = v
        ga_s[...] = gate_a
        gb_s[...] = gate_b

    r0 = pl.multiple_of(i * SAMPLE_STEP, SAMPLE_STEP)
    kwin = jnp.concatenate([kc_ref[:, 1:, :], k_s[pl.ds(r0, SAMPLE_STEP), :][:, None, :]], axis=1)
    vwin = jnp.concatenate([vc_ref[:, 1:, :], v_s[pl.ds(r0, SAMPLE_STEP), :][:, None, :]], axis=1)
    kwin_ref[...] = kwin
    vwin_ref[...] = vwin

    q16 = q_s[pl.ds(r0, SAMPLE_STEP), :]
    lane = lax.broadcasted_iota(I32, (SAMPLE_STEP, LANES), 1)
    heads = [q16[:, hq * LANES:(hq + 1) * LANES] for hq in range(N_HEADS)]
    qpad = pltpu.einshape("hbd->bhd", jnp.stack(heads, axis=0)).astype(BF16)
    s = jnp.einsum("bhd,bkd->bhk", qpad, kwin.astype(BF16), preferred_element_type=F32)
    hid = lax.broadcasted_iota(I32, (1, N_HEADS, 1), 1)
    sink = jnp.zeros((1, N_HEADS, 1), F32)
    for hq in range(N_HEADS):
        sink = jnp.where(hid == hq, sinks_ref[hq], sink)
    m = jnp.maximum(jnp.max(s, axis=-1, keepdims=True), sink)
    e = jnp.exp(s - m)
    inv = 1.0 / (jnp.sum(e, axis=-1, keepdims=True) + jnp.exp(sink - m))
    o = jnp.einsum("bhk,bkd->bhd", (e * inv).astype(BF16), vwin.astype(BF16), preferred_element_type=F32)
    o = pltpu.einshape("bhd->hbd", o)
    chunks = []
    for c in range(N_HEADS // 2):
        parts = []
        for p in range(2):
            hq = 2 * c + p
            oh = o[hq]
            if p != hq // GQA_GROUP:
                oh = pltpu.roll(oh, HEAD_DIM, 1)
            parts.append(oh)
        chunks.append(jnp.where(lane < HEAD_DIM, parts[0], parts[1]))
    att_s[pl.ds(r0, SAMPLE_STEP), :] = jnp.concatenate(chunks, axis=1)

    @pl.when(i == SAMPLE_STEPS - 1)
    def _():
        logits_ref[...] = _merge_and_route(
            x_ref[...], yain_s[...], att_s[...], ga_s[...], gb_s[...],
            wpa_ref, wpb_ref, wo_ref, gffn_ref, wr3_ref, br_ref, x1_ref, h2_ref)


def _sample_front(x, cos, sin, gmix, win, lng, lnb, wdiag, bs0, sinks, kc, vc, wpa, wpb, wo, gffn, wrt, br):
    n = DEC_BATCH
    cache_spec = pl.BlockSpec((SAMPLE_STEP, WINDOW, KV_WIDTH), lambda i: (i, 0, 0))
    in_specs = [
        _full((n, D_MODEL)),
        _full((1, LANES)),
        _full((1, LANES)),
        _full((1, D_MODEL)),
        _full((D_MODEL, IN_COLS)),
        _full((1, A_WIDTH)),
        _full((1, A_WIDTH)),
        _full((1, A_WIDTH)),
        _full((1, A_WIDTH)),
        pl.BlockSpec(memory_space=pltpu.SMEM),
        cache_spec,
        cache_spec,
        _full((A_WIDTH, D_MODEL)),
        _full((Q_WIDTH, D_MODEL)),
        _full((D_MODEL, D_MODEL)),
        _full((1, D_MODEL)),
        _full((3 * D_MODEL, LANES)),
        _full((N_EXPERTS, 1)),
    ]
    out_shape = [
        jax.ShapeDtypeStruct((n * LANE_CHUNKS, LANES), F32),
        jax.ShapeDtypeStruct((n * LANE_CHUNKS, LANES), F32),
        jax.ShapeDtypeStruct((N_EXPERTS, n), F32),
        jax.ShapeDtypeStruct((n, WINDOW, KV_WIDTH), F32),
        jax.ShapeDtypeStruct((n, WINDOW, KV_WIDTH), F32),
        jax.ShapeDtypeStruct((n, A_WIDTH), F32),
    ]
    out_specs = [
        _full((n * LANE_CHUNKS, LANES)),
        _full((n * LANE_CHUNKS, LANES)),
        _full((N_EXPERTS, n)),
        cache_spec,
        cache_spec,
        _full((n, A_WIDTH)),
    ]
    scratch = [
        pltpu.VMEM((n, QPAD_WIDTH), F32), pltpu.VMEM((n, KV_WIDTH), F32), pltpu.VMEM((n, KV_WIDTH), F32),
        pltpu.VMEM((n, A_WIDTH), F32), pltpu.VMEM((n, D_MODEL), F32), pltpu.VMEM((n, D_MODEL), F32),
        pltpu.VMEM((n, Q_WIDTH), F32),
    ]
    return pl.pallas_call(
        _sample_kernel,
        grid=(SAMPLE_STEPS,),
        in_specs=in_specs,
        out_specs=out_specs,
        out_shape=out_shape,
        scratch_shapes=scratch,
        compiler_params=pltpu.CompilerParams(dimension_semantics=("arbitrary",), vmem_limit_bytes=VMEM_LIMIT),
        name="sample_front",
    )(x, cos, sin, gmix, win, lng, lnb, wdiag, bs0, sinks, kc, vc, wpa, wpb, wo, gffn, wrt, br)


def _route_plan_kernel(lp_ref, ls_ref, dest_ref, wts_ref, off_ref):
    g = pl.program_id(0)
    topi, topw = _top4_softmax(jnp.concatenate([lp_ref[...], ls_ref[...]], axis=1))
    slot = lax.broadcasted_iota(I32, (TOP_K, GROUP_SLOTS), 1)
    eall = jnp.where(jnp.logical_or(slot < GROUP_PROMPT, g == N_GROUPS - 1), topi, N_EXPERTS)
    wts_ref[:, 0:GROUP_SLOTS] = topw
    wts_ref[:, GROUP_SLOTS:] = jnp.zeros((TOP_K, K_STRIDE - GROUP_SLOTS), F32)
    dest_ref[:, GROUP_SLOTS:] = jnp.zeros((TOP_K, K_STRIDE - GROUP_SLOTS), I32)
    eid = lax.broadcasted_iota(I32, (N_EXPERTS, GROUP_SLOTS), 0)
    onehots = [eall[k:k + 1, :] == eid for k in range(TOP_K)]
    count = jnp.zeros((N_EXPERTS, GROUP_SLOTS), F32)
    for oh in onehots:
        count = count + oh.astype(F32)
    total = jnp.broadcast_to(jnp.sum(count, axis=1, keepdims=True), (N_EXPERTS, LANES))
    padded = total + (MOE_ROWS - 1)
    nblk = jnp.floor(padded * (1.0 / MOE_ROWS))
    rem = padded - nblk * MOE_ROWS
    nblk = jnp.where(rem >= MOE_ROWS, nblk + 1.0, jnp.where(rem < 0.0, nblk - 1.0, nblk))
    r = lax.broadcasted_iota(I32, (N_EXPERTS, N_EXPERTS), 0)
    c = lax.broadcasted_iota(I32, (N_EXPERTS, N_EXPERTS), 1)
    first_blk = lax.dot_general((c < r).astype(F32), nblk, (((1,), (0,)), ((), ())),
                                precision=lax.Precision.HIGHEST, preferred_element_type=F32)
    start = (first_blk + 1.0) * MOE_ROWS
    lane = lax.broadcasted_iota(I32, (N_EXPERTS, LANES), 1)
    info = jnp.where(lane == 0, start, jnp.where(lane == 1, start + total, jnp.where(lane == 2, nblk, first_blk)))
    off_ref[...] = info.astype(I32)
    ti = lax.broadcasted_iota(I32, (LANES, LANES), 0)
    tj = lax.broadcasted_iota(I32, (LANES, LANES), 1)
    before = (ti < tj).astype(BF16)
    ones = jnp.ones((LANES, LANES), BF16)
    running = start
    for t in range(SLOT_TILES):
        sl = slice(t * LANES, (t + 1) * LANES)
        cb = count[:, sl].astype(BF16)
        pos = running + _bdot(cb, before)
        rows = [jnp.sum(jnp.where(oh[:, sl], pos, 0.0), axis=0, keepdims=True) for oh in onehots]
        dest_ref[:, sl] = jnp.concatenate(rows, axis=0).astype(I32)
        running = running + _bdot(cb, ones)


def _route_plan(logits_p, logits_s):
    in_specs = [
        pl.BlockSpec((N_EXPERTS, GROUP_PROMPT), lambda g: (0, g)),
        pl.BlockSpec((N_EXPERTS, DEC_BATCH), lambda g: (0, 0)),
    ]
    out_shape = [
        jax.ShapeDtypeStruct((N_GROUPS, TOP_K, K_STRIDE), I32),
        jax.ShapeDtypeStruct((N_GROUPS, TOP_K, K_STRIDE), F32),
        jax.ShapeDtypeStruct((N_GROUPS, N_EXPERTS, LANES), I32),
    ]
    out_specs = [
        pl.BlockSpec((None, TOP_K, K_STRIDE), lambda g: (g, 0, 0)),
        pl.BlockSpec((None, TOP_K, K_STRIDE), lambda g: (g, 0, 0)),
        pl.BlockSpec((None, N_EXPERTS, LANES), lambda g: (g, 0, 0)),
    ]
    return pl.pallas_call(
        _route_plan_kernel,
        grid=(N_GROUPS,),
        in_specs=in_specs,
        out_specs=out_specs,
        out_shape=out_shape,
        compiler_params=pltpu.CompilerParams(dimension_semantics=("arbitrary",)),
        name="route_plan",
    )(logits_p, logits_s)


GROUP_ROWS = GROUP_PROMPT * LANE_CHUNKS
SAMPLE_ROWS = DEC_BATCH * LANE_CHUNKS
TRASH_SLOT = GROUP_SLOTS
BUF_ROWS = (GROUP_SLOTS + 1) * LANE_CHUNKS
SCATTER_BATCH = 8
ACC_CLEAR_ROWS = LANE_CHUNKS * 169
assert BUF_ROWS % ACC_CLEAR_ROWS == 0
DMA_SPLIT = 8
TABLE_USED = GROUP_SLOTS + LANES
assert TRASH_SLOT < TABLE_USED <= K_STRIDE
FFN_COLS = 256
FFN_DOTS = (2 * D_EXPERT + D_MODEL) // FFN_COLS


def _moe_kernel(off_ref, desth_ref, wtsh_ref, h2p_ref, h2s_ref, wgu_ref, bgu_ref, wdn_ref, bdn_ref,
                x2p_ref, x2s_ref,
                h2buf, acc, wgubuf, bgubuf, wdnbuf, bdnbuf, xs0, xs1, ys0, ys1,
                dest_ref, wts_ref, src_ref, seg_expert, seg_first, blk_seg, act_sem, w_sem):
    g = pl.program_id(0)
    last = g == N_GROUPS - 1
    row0 = pl.multiple_of(g * GROUP_ROWS, GROUP_ROWS)

    def prompt_copies():
        cps = []
        for j in range(DMA_SPLIT):
            src = pl.ds(row0 + j * (GROUP_ROWS // DMA_SPLIT), GROUP_ROWS // DMA_SPLIT)
            dst = pl.ds(j * (GROUP_ROWS // DMA_SPLIT), GROUP_ROWS // DMA_SPLIT)
            cps.append(pltpu.make_async_copy(h2p_ref.at[src], h2buf.at[dst], act_sem.at[0]))
        return cps

    def sample_copies():
        return (pltpu.make_async_copy(h2s_ref, h2buf.at[pl.ds(GROUP_ROWS, SAMPLE_ROWS)], act_sem.at[2]),)

    def weight_copies(e, slot):
        cps = [pltpu.make_async_copy(bgu_ref.at[e], bgubuf.at[slot], w_sem.at[1, slot]),
               pltpu.make_async_copy(bdn_ref.at[e], bdnbuf.at[slot], w_sem.at[3, slot])]
        for j in range(DMA_SPLIT):
            rg = pl.ds(j * (D_MODEL // DMA_SPLIT), D_MODEL // DMA_SPLIT)
            rd = pl.ds(j * (D_EXPERT // DMA_SPLIT), D_EXPERT // DMA_SPLIT)
            cps.append(pltpu.make_async_copy(wgu_ref.at[e, rg], wgubuf.at[slot, rg], w_sem.at[0, slot]))
            cps.append(pltpu.make_async_copy(wdn_ref.at[e, rd], wdnbuf.at[slot, rd], w_sem.at[2, slot]))
        return cps

    def output_copies(group):
        base = pl.multiple_of(group * GROUP_ROWS, GROUP_ROWS)
        return [pltpu.make_async_copy(
            acc.at[pl.ds(j * (GROUP_ROWS // DMA_SPLIT), GROUP_ROWS // DMA_SPLIT)],
            x2p_ref.at[pl.ds(base + j * (GROUP_ROWS // DMA_SPLIT), GROUP_ROWS // DMA_SPLIT)], act_sem.at[1])
            for j in range(DMA_SPLIT)]

    tab0 = pl.multiple_of(g * (TOP_K * K_STRIDE), TOP_K * K_STRIDE)
    table_copies = []
    for k in range(TOP_K):
        used = pl.ds(k * K_STRIDE, TABLE_USED)
        table_copies.append(pltpu.make_async_copy(
            desth_ref.at[pl.ds(tab0 + k * K_STRIDE, TABLE_USED)], dest_ref.at[used], act_sem.at[4]))
        table_copies.append(pltpu.make_async_copy(
            wtsh_ref.at[pl.ds(tab0 + k * K_STRIDE, TABLE_USED)], wts_ref.at[used], act_sem.at[5]))
    for cp in table_copies:
        cp.start()

    for cp in prompt_copies():
        cp.start()

    @pl.when(last)
    def _():
        for cp in sample_copies():
            cp.start()

    trash = pl.ds(TRASH_SLOT * LANE_CHUNKS, LANE_CHUNKS)
    h2buf[trash, :] = jnp.zeros((LANE_CHUNKS, LANES), F32)
    acc[trash, :] = jnp.zeros((LANE_CHUNKS, LANES), F32)
    ys0[...] = jnp.zeros_like(ys0)
    ys1[...] = jnp.zeros_like(ys1)

    def pad_block(pos0, real_rows=0):
        def body(j, carry):
            for d in range(SUBLANES):
                src_ref[pos0 + MOE_ROWS - SUBLANES - j * SUBLANES + d] = TRASH_SLOT
            return carry
        lax.fori_loop(0, (MOE_ROWS - real_rows + SUBLANES - 1) // SUBLANES, body, 0)

    def scan_expert(e, carry):
        nseg, nblocks = carry
        nblk = off_ref[e, 2]
        first = off_ref[e, 3]

        @pl.when(nblk > 0)
        def _():
            seg_expert[nseg] = e
            seg_first[nseg] = first
            tail_rows = off_ref[e, 1] - off_ref[e, 0] - (nblk - 1) * MOE_ROWS
            pad_block(off_ref[e, 0] + (nblk - 1) * MOE_ROWS, tail_rows)

            def mark(b, c2):
                blk_seg[first + b] = nseg
                return c2
            lax.fori_loop(0, nblk, mark, 0)

            blk_seg[first + nblk - 1] = nseg + jnp.where(tail_rows <= HALF_ROWS, HALF_FLAG, 0)

        return nseg + jnp.where(nblk > 0, 1, 0), nblocks + nblk

    nseg, nblocks = lax.fori_loop(0, N_EXPERTS, scan_expert, (jnp.int32(0), jnp.int32(0)))
    pad_block(0)
    pad_block((nblocks + 1) * MOE_ROWS)
    pad_block((nblocks + 2) * MOE_ROWS)
    blk_seg[nblocks] = nseg - 1 + HALF_FLAG
    blk_seg[nblocks + 1] = nseg - 1 + HALF_FLAG

    for cp in weight_copies(seg_expert[0], 0):
        cp.start()

    for cp in table_copies:
        cp.wait()

    nvalid = jnp.where(last, GROUP_SLOTS, GROUP_PROMPT)
    for k in range(TOP_K):
        def fill(j, carry, k=k):
            c0 = k * K_STRIDE + j * SUBLANES
            for d in range(SUBLANES):
                src_ref[dest_ref[c0 + d]] = c0 + d
            return carry
        lax.fori_loop(0, nvalid // SUBLANES, fill, 0)

    @pl.when(g > 0)
    def _():
        for cp in output_copies(g - 1):
            cp.wait()

    def clear(j, carry):
        acc[pl.ds(pl.multiple_of(j * ACC_CLEAR_ROWS, SUBLANES), ACC_CLEAR_ROWS), :] = jnp.zeros(
            (ACC_CLEAR_ROWS, LANES), F32)
        return carry
    lax.fori_loop(0, BUF_ROWS // ACC_CLEAR_ROWS, clear, 0)

    for cp in prompt_copies():
        cp.wait()

    @pl.when(last)
    def _():
        for cp in sample_copies():
            cp.wait()

    def token_rows(code):
        slot_id = code & (K_STRIDE - 1)
        return pl.ds(pl.multiple_of(slot_id * LANE_CHUNKS, LANE_CHUNKS), LANE_CHUNKS)

    def gather(b, xs, lo=0, hi=MOE_ROWS):
        base = (b + 1) * MOE_ROWS
        for m in range(lo, hi):
            xs[pl.ds(m, LANE_CHUNKS, stride=XS_STRIDE), :] = h2buf[token_rows(src_ref[base + m]), :]

    def scatter_add(b, ys, lo=0, hi=MOE_ROWS):
        base = (b + 1) * MOE_ROWS
        for m0 in range(lo, hi, SCATTER_BATCH):
            pending = []
            for m in range(m0, m0 + SCATTER_BATCH):
                code = src_ref[base + m]
                rows = token_rows(code)
                pending.append((rows, acc[rows, :] + wts_ref[code] * ys[pl.ds(m, LANE_CHUNKS, stride=XS_STRIDE), :]))
            for rows, val in pending:
                acc[rows, :] = val

    gather_cuts = [round(i * MOE_ROWS / FFN_DOTS) for i in range(FFN_DOTS + 1)]
    scatter_cuts = [SCATTER_BATCH * round(i * (MOE_ROWS // SCATTER_BATCH) / FFN_DOTS) for i in range(FFN_DOTS + 1)]

    def step(b, xs_cur, xs_next, ys_cur, ys_prev):
        tag = blk_seg[b]
        seg = tag & (HALF_FLAG - 1)
        half = tag >= HALF_FLAG
        slot = seg & 1

        @pl.when(jnp.logical_and(b == seg_first[seg], b < nblocks))
        def _():
            for cp in weight_copies(seg_expert[seg], slot):
                cp.wait()

            @pl.when(seg + 1 < nseg)
            def _():
                for cp in weight_copies(seg_expert[seg + 1], 1 - slot):
                    cp.start()

        def work(rows):
            done = [0]

            def row_traffic():
                i = done[0]
                gather(b + 1, xs_next, gather_cuts[i], gather_cuts[i + 1])
                scatter_add(b - 1, ys_prev, scatter_cuts[i], scatter_cuts[i + 1])
                done[0] = i + 1

            x = jnp.concatenate(
                [xs_cur[c * XS_STRIDE:c * XS_STRIDE + rows, :] for c in range(LANE_CHUNKS)], axis=1).astype(BF16)
            acts = []
            for c in range(D_EXPERT // FFN_COLS):
                gc = pl.ds(c * FFN_COLS, FFN_COLS)
                uc = pl.ds(D_EXPERT + c * FFN_COLS, FFN_COLS)
                g = _bdot(x, wgubuf[slot, :, gc]) + bgubuf[slot, :, gc]
                row_traffic()
                u = _bdot(x, wgubuf[slot, :, uc]) + bgubuf[slot, :, uc]
                row_traffic()
                gl = jnp.minimum(g, SWIGLU_LIMIT)
                ul = jnp.clip(u, -SWIGLU_LIMIT, SWIGLU_LIMIT)
                acts.append((ul + 1.0) * (gl * jax.nn.sigmoid(SWIGLU_ALPHA * gl)))
            a = jnp.concatenate(acts, axis=1).astype(BF16)
            for c in range(D_MODEL // FFN_COLS):
                oc = pl.ds(c * FFN_COLS, FFN_COLS)
                y = _bdot(a, wdnbuf[slot, :, oc]) + bdnbuf[slot, :, oc]
                for j in range(FFN_COLS // LANES):
                    lc = c * (FFN_COLS // LANES) + j
                    ys_cur[lc * XS_STRIDE:lc * XS_STRIDE + rows, :] = y[:, j * LANES:(j + 1) * LANES]
                row_traffic()

        lax.cond(half, lambda: work(HALF_ROWS), lambda: work(MOE_ROWS))

    gather(0, xs0)
    npairs = (nblocks + 1) // 2

    def pair(t, carry):
        step(2 * t, xs0, xs1, ys0, ys1)
        step(2 * t + 1, xs1, xs0, ys1, ys0)
        return carry

    lax.fori_loop(0, npairs, pair, 0)
    scatter_add(2 * npairs - 1, ys1)

    for cp in output_copies(g):
        cp.start()

    @pl.when(last)
    def _():
        out_s = pltpu.make_async_copy(acc.at[pl.ds(GROUP_ROWS, SAMPLE_ROWS)], x2s_ref, act_sem.at[3])
        out_s.start()
        out_s.wait()
        for cp in output_copies(g):
            cp.wait()


def _moe(dest, wts, off, h2p, h2s, wgu, bgu, wdn, bdn):
    anyspec = pl.BlockSpec(memory_space=pl.ANY)
    dest = dest.reshape(N_GROUPS * TOP_K * K_STRIDE)
    wts = wts.reshape(N_GROUPS * TOP_K * K_STRIDE)
    in_specs = [
        pl.BlockSpec((None, N_EXPERTS, LANES), lambda g: (g, 0, 0), memory_space=pltpu.SMEM),
        anyspec, anyspec, anyspec, anyspec, anyspec, anyspec, anyspec, anyspec,
    ]
    scratch = [
        pltpu.VMEM((BUF_ROWS, LANES), F32),
        pltpu.VMEM((BUF_ROWS, LANES), F32),
        pltpu.VMEM((2, D_MODEL, 2 * D_EXPERT), BF16),
        pltpu.VMEM((2, 1, 2 * D_EXPERT), F32),
        pltpu.VMEM((2, D_EXPERT, D_MODEL), BF16),
        pltpu.VMEM((2, 1, D_MODEL), F32),
        pltpu.VMEM((LANE_CHUNKS * XS_STRIDE, LANES), F32),
        pltpu.VMEM((LANE_CHUNKS * XS_STRIDE, LANES), F32),
        pltpu.VMEM((LANE_CHUNKS * XS_STRIDE, LANES), F32),
        pltpu.VMEM((LANE_CHUNKS * XS_STRIDE, LANES), F32),
        pltpu.SMEM((TOP_K * K_STRIDE,), I32),
        pltpu.SMEM((TOP_K * K_STRIDE,), F32),
        pltpu.SMEM((POS_TABLE,), I32),
        pltpu.SMEM((N_EXPERTS,), I32),
        pltpu.SMEM((N_EXPERTS,), I32),
        pltpu.SMEM((LANES,), I32),
        pltpu.SemaphoreType.DMA((6,)),
        pltpu.SemaphoreType.DMA((4, 2)),
    ]
    return pl.pallas_call(
        _moe_kernel,
        grid=(N_GROUPS,),
        in_specs=in_specs,
        out_specs=[anyspec, anyspec],
        out_shape=[jax.ShapeDtypeStruct(h2p.shape, F32), jax.ShapeDtypeStruct(h2s.shape, F32)],
        scratch_shapes=scratch,
        compiler_params=pltpu.CompilerParams(dimension_semantics=("arbitrary",), vmem_limit_bytes=VMEM_LIMIT),
        name="moe",
    )(off, dest, wts, h2p, h2s, wgu, bgu, wdn, bdn)


def _ple_final_kernel(x1_ref, moe_ref, ple_ref, wple_ref, gple_ref, wpg_ref, gfin_ref, y_ref):
    rows = y_ref.shape[0]
    x2 = _load_token_major(x1_ref, rows) + _load_token_major(moe_ref, rows)
    e = _rmsnorm(_bdot(ple_ref[...].astype(BF16), wple_ref[...]), gple_ref[...])
    x3 = x2 + jax.nn.sigmoid(_bdot(x2.astype(BF16), wpg_ref[...])) * e
    y_ref[...] = _rmsnorm(x3, gfin_ref[...])


def _ple_final(x1_tm, moe_tm, ple, wple, gple, wpg, gfin, tile):
    n = ple.shape[0]
    return pl.pallas_call(
        _ple_final_kernel,
        grid=(n // tile,),
        in_specs=[
            pl.BlockSpec((tile * LANE_CHUNKS, LANES), lambda i: (i, 0)),
            pl.BlockSpec((tile * LANE_CHUNKS, LANES), lambda i: (i, 0)),
            pl.BlockSpec((tile, PLE_DIM), lambda i: (i, 0)),
            _full((PLE_DIM, D_MODEL)),
            _full((1, D_MODEL)),
            _full((D_MODEL, D_MODEL)),
            _full((1, D_MODEL)),
        ],
        out_specs=pl.BlockSpec((tile, D_MODEL), lambda i: (i, 0)),
        out_shape=jax.ShapeDtypeStruct((n, D_MODEL), F32),
        compiler_params=pltpu.CompilerParams(dimension_semantics=("arbitrary",), vmem_limit_bytes=VMEM_LIMIT),
        name="ple_final",
    )(x1_tm, moe_tm, ple, wple, gple, wpg, gfin)


def _rope_tables(pos):
    half = HEAD_DIM // 2
    inv = ROPE_THETA ** (-jnp.arange(half, dtype=F32) / half)
    ang = pos.astype(F32)[:, None] * inv[None, :]
    cos, sin = jnp.cos(ang), jnp.sin(ang)
    cos2 = jnp.concatenate([cos, cos, cos, cos], axis=1)
    sin2 = jnp.concatenate([-sin, sin, -sin, sin], axis=1)
    return cos2, sin2


def _layout_w_in(w_in):
    col = jnp.arange(IN_COLS)
    return jnp.where((col >= O_Q) & (col < O_K), w_in * (HEAD_DIM ** -0.5), w_in)


def _router_passes(w_router):
    hi = w_router.astype(BF16)
    lo = (w_router - hi.astype(F32)).astype(BF16)
    w3 = jnp.concatenate([hi, lo, hi], axis=0)
    return jnp.pad(w3, ((0, 0), (0, LANES - N_EXPERTS)))


def _prep_weights(g_mix, w_in, a_ln_g, a_ln_b, a_ws, a_bs, w_pa, w_pb, w_o, g_ffn, w_router, b_router):
    causal = jnp.tril(jnp.ones((CHUNK, CHUNK), dtype=bool))
    return dict(
        gmix=g_mix.reshape(1, D_MODEL),
        win=_layout_w_in(w_in).astype(BF16),
        lng=a_ln_g.reshape(1, A_WIDTH),
        lnb=a_ln_b.reshape(1, A_WIDTH),
        ws=jnp.where(causal[None], a_ws, 0.0).astype(BF16),
        bsf=jnp.repeat(jnp.transpose(a_bs), A_GROUP_DIM, axis=1),
        wpa=w_pa.astype(BF16),
        wpb=w_pb.astype(BF16),
        wo=w_o.astype(BF16),
        gffn=g_ffn.reshape(1, D_MODEL),
        wrt=_router_passes(w_router),
        br=b_router.reshape(N_EXPERTS, 1),
    )


def kernel(x_prompt, x_sample, cache_win_k, cache_win_v, p_prompt, p_sample, g_mix, w_in, a_ln_g, a_ln_b, a_ws, a_bs, sinks, w_pa, w_pb, w_o, g_ffn, w_router, b_router, w_gu, b_gu, w_down, b_down, w_ple, g_ple, w_ple_gate, g_final):
    W = _prep_weights(g_mix[0], w_in[0], a_ln_g[0], a_ln_b[0], a_ws[0], a_bs[0], w_pa[0], w_pb[0], w_o[0],
                      g_ffn[0], w_router[0], b_router[0])
    cos_p, sin_p = _rope_tables(jnp.arange(SEQ, dtype=I32))
    cos_s, sin_s = _rope_tables(jnp.full((1,), PAST_LEN, I32))
    x1p, h2p, logits_p, kwin_p, vwin_p, wgu16, wdn16 = _prompt_front(
        x_prompt.reshape(N_PROMPT, D_MODEL), cos_p, sin_p, W["gmix"], W["win"], W["lng"], W["lnb"],
        W["ws"], W["bsf"], sinks[0], _band_bias(), W["wpa"], W["wpb"], W["wo"], W["gffn"], W["wrt"], W["br"],
        w_gu[0].reshape(N_EXPERTS * D_MODEL, 2 * D_EXPERT), w_down[0].reshape(N_EXPERTS * D_EXPERT, D_MODEL))

    wdiag = jnp.repeat(a_ws[0, :, 0, 0], A_GROUP_DIM)[None, :].astype(BF16)
    bs0 = jnp.repeat(a_bs[0, :, 0], A_GROUP_DIM)[None, :]
    x1s, h2s, logits_s, kwin_s, vwin_s, va_s = _sample_front(
        x_sample.reshape(DEC_BATCH, D_MODEL), cos_s, sin_s, W["gmix"], W["win"], W["lng"], W["lnb"], wdiag, bs0,
        sinks[0], cache_win_k[0].reshape(DEC_BATCH, WINDOW, KV_WIDTH), cache_win_v[0].reshape(DEC_BATCH, WINDOW, KV_WIDTH),
        W["wpa"], W["wpb"], W["wo"], W["gffn"], W["wrt"], W["br"])

    dest, wts, off = _route_plan(logits_p, logits_s)
    moe_p, moe_s = _moe(dest, wts, off, h2p, h2s,
                    wgu16.reshape(N_EXPERTS, D_MODEL, 2 * D_EXPERT), b_gu[0].reshape(N_EXPERTS, 1, 2 * D_EXPERT),
                    wdn16.reshape(N_EXPERTS, D_EXPERT, D_MODEL), b_down[0].reshape(N_EXPERTS, 1, D_MODEL))

    wple = w_ple[0].astype(BF16)
    gple = g_ple[0].reshape(1, D_MODEL)
    wpg = w_ple_gate[0].astype(BF16)
    gfin = g_final.reshape(1, D_MODEL)
    y_p = _ple_final(x1p, moe_p, p_prompt[0].reshape(N_PROMPT, PLE_DIM), wple, gple, wpg, gfin, 4 * TM)
    y_s = _ple_final(x1s, moe_s, p_sample[0].reshape(DEC_BATCH, PLE_DIM), wple, gple, wpg, gfin, DEC_BATCH)

    return (
        y_p.reshape(BATCH, SEQ, D_MODEL),
        y_s.reshape(DEC_BATCH, 1, D_MODEL),
        kwin_p.reshape(1, BATCH, WINDOW, N_KV_HEADS, HEAD_DIM),
        vwin_p.reshape(1, BATCH, WINDOW, N_KV_HEADS, HEAD_DIM),
        kwin_s.reshape(1, DEC_BATCH, WINDOW, N_KV_HEADS, HEAD_DIM),
        vwin_s.reshape(1, DEC_BATCH, WINDOW, N_KV_HEADS, HEAD_DIM),
        va_s.reshape(1, DEC_BATCH, 1, A_WIDTH),
    )
```
